```python
import jax, jax.numpy as jnp
from jax import lax
import numpy as np

D_MODEL = 2048
BATCH = 32
SEQ = 256
DEPTH = 1
DEC_BATCH = 4
DEC_SEQ = 1024
PAST_LEN = 512

GRID_W = 64
N_HEADS = 8
N_KV_HEADS = 2
HEAD_DIM = 128
Q_PER_KV = N_HEADS // N_KV_HEADS
ATTN_WIDTH = N_HEADS * HEAD_DIM
KV_WIDTH = N_KV_HEADS * HEAD_DIM
POOL_WIDTH = D_MODEL - ATTN_WIDTH
POOL_SIZES = (2, 4, 8, 16)
POOL_GROUP = POOL_WIDTH // len(POOL_SIZES)
IN_WIDTH = ATTN_WIDTH + 2 * KV_WIDTH + POOL_WIDTH
WINDOW = 128
BLOCK = 128
ROPE_THETA = 10000.0
ROPE_FREQS = HEAD_DIM // 4
N_EXPERTS = 16
EC_FACTOR = 2
D_EXPERT = 1024
N_MOD = 6
EPS = 1e-6
NEG = -1e30

kernel_name = 'hymba_pool_ec_diffusion_step'


def rmsnorm(x, g):
    xf = x.astype(jnp.float32)
    y = xf * lax.rsqrt(jnp.mean(xf * xf, axis=-1, keepdims=True) + EPS)
    return (y * g.astype(jnp.float32)).astype(x.dtype)


def modulation(cond, w_ada, b_ada):
    m = jax.nn.silu(cond) @ w_ada + b_ada
    return jnp.split(m[:, None, :], N_MOD, axis=-1)


def modulate(h, shift, scale):
    return h * (1 + scale) + shift


def split_projection(h, w_in):
    u = h @ w_in
    b, n = h.shape[0], h.shape[1]
    q = u[..., :ATTN_WIDTH].reshape(b, n, N_HEADS, HEAD_DIM)
    k = u[..., ATTN_WIDTH:ATTN_WIDTH + KV_WIDTH].reshape(b, n, N_KV_HEADS, HEAD_DIM)
    v = u[..., ATTN_WIDTH + KV_WIDTH:ATTN_WIDTH + 2 * KV_WIDTH].reshape(b, n, N_KV_HEADS, HEAD_DIM)
    p = u[..., ATTN_WIDTH + 2 * KV_WIDTH:]
    return q, k, v, p


def axial_cos_sin(n):
    rows = n // GRID_W
    row = jnp.repeat(jnp.arange(rows, dtype=jnp.float32), GRID_W)
    col = jnp.tile(jnp.arange(GRID_W, dtype=jnp.float32), rows)
    inv = ROPE_THETA ** (-jnp.arange(ROPE_FREQS, dtype=jnp.float32) / ROPE_FREQS)
    ang = jnp.stack([row[:, None] * inv, col[:, None] * inv], axis=1)
    return jnp.cos(ang), jnp.sin(ang)


def apply_axial_rope(x, cos, sin):
    b, n, h, _ = x.shape
    xr = x.astype(jnp.float32).reshape(b, n, h, 2, 2, ROPE_FREQS)
    x1, x2 = xr[..., 0, :], xr[..., 1, :]
    c = cos[None, :, None, :, :]
    s = sin[None, :, None, :, :]
    out = jnp.stack([x1 * c - x2 * s, x1 * s + x2 * c], axis=-2)
    return out.reshape(b, n, h, HEAD_DIM).astype(x.dtype)


def sink_column(sink, lead_shape):
    return jnp.broadcast_to(sink.astype(jnp.float32).reshape(N_KV_HEADS, Q_PER_KV)[:, :, None, None], lead_shape + (1,)) if False else None


def context_attention(q, k, v, sink):
    b, l = q.shape[0], q.shape[1]
    qg = q.reshape(b, l, N_KV_HEADS, Q_PER_KV, HEAD_DIM)
    s = jnp.einsum('blkgd,bmkd->bkglm', qg, k).astype(jnp.float32) * (HEAD_DIM ** -0.5)
    sk = jnp.broadcast_to(sink.astype(jnp.float32).reshape(1, N_KV_HEADS, Q_PER_KV, 1, 1), s.shape[:-1] + (1,))
    p = jax.nn.softmax(jnp.concatenate([s, sk], axis=-1), axis=-1)
    out = jnp.einsum('bkglm,bmkd->blkgd', p[..., :l].astype(v.dtype), v)
    return out.reshape(b, l, ATTN_WIDTH)


def latent_attention(q, k, v, k_ctx, v_ctx, sink):
    b, n = q.shape[0], q.shape[1]
    l = k_ctx.shape[1]
    nb = n // BLOCK
    qb = q.reshape(b, nb, BLOCK, N_KV_HEADS, Q_PER_KV, HEAD_DIM)
    pad = ((0, 0), (BLOCK, BLOCK), (0, 0), (0, 0))
    kp = jnp.pad(k, pad).reshape(b, nb + 2, BLOCK, N_KV_HEADS, HEAD_DIM)
    vp = jnp.pad(v, pad).reshape(b, nb + 2, BLOCK, N_KV_HEADS, HEAD_DIM)
    kband = jnp.concatenate([kp[:, :-2], kp[:, 1:-1], kp[:, 2:]], axis=2)
    vband = jnp.concatenate([vp[:, :-2], vp[:, 1:-1], vp[:, 2:]], axis=2)
    scale = HEAD_DIM ** -0.5
    s_loc = jnp.einsum('bnqkgd,bnskd->bnkgqs', qb, kband).astype(jnp.float32) * scale
    blk = jnp.arange(nb)[:, None, None]
    qpos = blk * BLOCK + jnp.arange(BLOCK)[None, :, None]
    kpos = (blk - 1) * BLOCK + jnp.arange(3 * BLOCK)[None, None, :]
    mask = (jnp.abs(kpos - qpos) <= WINDOW) & (kpos >= 0) & (kpos < n)
    s_loc = jnp.where(mask[None, :, None, None, :, :], s_loc, NEG)
    s_ctx = jnp.einsum('bnqkgd,blkd->bnkgql', qb, k_ctx).astype(jnp.float32) * scale
    sk = jnp.broadcast_to(sink.astype(jnp.float32).reshape(1, 1, N_KV_HEADS, Q_PER_KV, 1, 1), s_loc.shape[:-1] + (1,))
    p = jax.nn.softmax(jnp.concatenate([s_loc, s_ctx, sk], axis=-1), axis=-1)
    p_loc = p[..., :3 * BLOCK].astype(v.dtype)
    p_ctx = p[..., 3 * BLOCK:3 * BLOCK + l].astype(v.dtype)
    out = (jnp.einsum('bnkgqs,bnskd->bnqkgd', p_loc, vband)
           + jnp.einsum('bnkgql,blkd->bnqkgd', p_ctx, v_ctx))
    return out.reshape(b, n, ATTN_WIDTH)


def pool_mixer(p, pool_w, pool_scale):
    b, n, _ = p.shape
    t = jnp.arange(n)
    outs = []
    for g, w in enumerate(POOL_SIZES):
        pg = p[..., g * POOL_GROUP:(g + 1) * POOL_GROUP]
        cs = jnp.concatenate([jnp.zeros((b, 1, POOL_GROUP), jnp.float32),
                              jnp.cumsum(pg.astype(jnp.float32), axis=1)], axis=1)
        lo = jnp.clip(t - w // 2, 0, n)
        hi = jnp.clip(t + w - w // 2, 0, n)
        cnt = (hi - lo).astype(jnp.float32)
        mean = (jnp.take(cs, hi, axis=1) - jnp.take(cs, lo, axis=1)) / cnt[None, :, None]
        mixed = (mean - pg.astype(jnp.float32)).astype(p.dtype)
        outs.append(mixed @ pool_w[g])
    return jnp.concatenate(outs, axis=-1) * pool_scale


def expert_choice_ffn(h, w_router, w_gate, w_up, w_down):
    shp = h.shape
    xt = h.reshape(-1, shp[-1])
    cap = EC_FACTOR * xt.shape[0] // N_EXPERTS
    affin = jax.nn.softmax((xt @ w_router).astype(jnp.float32), axis=-1)
    gates, idx = lax.top_k(affin.T, cap)
    xs = xt[idx]
    hid = jax.nn.silu(jnp.einsum('ecd,edf->ecf', xs, w_gate)) * jnp.einsum('ecd,edf->ecf', xs, w_up)
    y = jnp.einsum('ecf,efd->ecd', hid, w_down) * gates[..., None].astype(xt.dtype)
    out = jnp.zeros_like(xt).at[idx.reshape(-1)].add(y.reshape(-1, shp[-1]))
    return out.reshape(shp)


def setup_inputs(seed: int = 0) -> dict:
    key = jax.random.key(seed)
    ks = jax.random.split(key, 20)
    f32 = jnp.float32
    D, E, F = D_MODEL, N_EXPERTS, D_EXPERT
    nrm = lambda k, s, sc: jax.random.normal(k, s, f32) * sc
    return {
        'x_prompt': nrm(ks[0], (BATCH, SEQ, D), 1.0),
        'x_sample': nrm(ks[1], (DEC_BATCH, DEC_SEQ, D), 1.0),
        'c': nrm(ks[2], (DEC_BATCH, D), 1.0),
        'cache_k': nrm(ks[3], (DEC_BATCH, DEPTH, PAST_LEN, N_KV_HEADS, HEAD_DIM), 1.0),
        'cache_v': nrm(ks[4], (DEC_BATCH, DEPTH, PAST_LEN, N_KV_HEADS, HEAD_DIM), 1.0),
        'c_ctx': nrm(ks[5], (D,), 1.0),
        'w_ada': nrm(ks[6], (DEPTH, D, N_MOD * D), D ** -0.5),
        'b_ada': nrm(ks[7], (DEPTH, N_MOD * D), 0.01),
        'norm_mix': 1.0 + nrm(ks[8], (DEPTH, D), 0.01),
        'w_in': nrm(ks[9], (DEPTH, D, IN_WIDTH), D ** -0.5),
        'sink_logits': nrm(ks[10], (DEPTH, N_HEADS), 1.0),
        'pool_w': nrm(ks[11], (DEPTH, len(POOL_SIZES), POOL_GROUP, POOL_GROUP), POOL_GROUP ** -0.5),
        'pool_scale': 1.0 + nrm(ks[12], (DEPTH, POOL_WIDTH), 0.02),
        'w_out': nrm(ks[13], (DEPTH, D, D), D ** -0.5),
        'norm_ffn': 1.0 + nrm(ks[14], (DEPTH, D), 0.01),
        'w_router': nrm(ks[15], (DEPTH, D, E), D ** -0.5),
        'w_gate': nrm(ks[16], (DEPTH, E, D, F), D ** -0.5),
        'w_up': nrm(ks[17], (DEPTH, E, D, F), D ** -0.5),
        'w_down': nrm(ks[18], (DEPTH, E, F, D), F ** -0.5),
        'norm_final': 1.0 + nrm(ks[19], (D,), 0.01),
    }


def reference(x_prompt, x_sample, c, cache_k, cache_v, c_ctx, w_ada, b_ada, norm_mix, w_in,
              sink_logits, pool_w, pool_scale, w_out, norm_ffn, w_router, w_gate, w_up,
              w_down, norm_final):
    xp = x_prompt
    xs = x_sample
    n_lat = xs.shape[1]
    cos, sin = axial_cos_sin(n_lat)
    new_k = []
    new_v = []
    for l in range(DEPTH):
        sh1, sc1, g1, sh2, sc2, g2 = modulation(c_ctx[None, :], w_ada[l], b_ada[l])
        h = modulate(rmsnorm(xp, norm_mix[l]), sh1, sc1)
        q, k, v, p = split_projection(h, w_in[l])
        new_k.append(k)
        new_v.append(v)
        mix = jnp.concatenate([context_attention(q, k, v, sink_logits[l]),
                               pool_mixer(p, pool_w[l], pool_scale[l])], axis=-1) @ w_out[l]
        xp = xp + g1 * mix
        h = modulate(rmsnorm(xp, norm_ffn[l]), sh2, sc2)
        xp = xp + g2 * expert_choice_ffn(h, w_router[l], w_gate[l], w_up[l], w_down[l])

        sh1, sc1, g1, sh2, sc2, g2 = modulation(c, w_ada[l], b_ada[l])
        h = modulate(rmsnorm(xs, norm_mix[l]), sh1, sc1)
        q, k, v, p = split_projection(h, w_in[l])
        q = apply_axial_rope(q, cos, sin)
        k = apply_axial_rope(k, cos, sin)
        attn = latent_attention(q, k, v, cache_k[:, l], cache_v[:, l], sink_logits[l])
        mix = jnp.concatenate([attn, pool_mixer(p, pool_w[l], pool_scale[l])], axis=-1) @ w_out[l]
        xs = xs + g1 * mix
        h = modulate(rmsnorm(xs, norm_ffn[l]), sh2, sc2)
        xs = xs + g2 * expert_choice_ffn(h, w_router[l], w_gate[l], w_up[l], w_down[l])

    y_prompt = rmsnorm(xp, norm_final)
    y_sample = rmsnorm(xs, norm_final)
    state_k = jnp.stack(new_k, axis=1)
    state_v = jnp.stack(new_v, axis=1)
    return (y_prompt, y_sample, state_k, state_v)
```

```python
import functools

import jax
import jax.numpy as jnp
import numpy as np
from jax import lax
from jax.experimental import pallas as pl
from jax.experimental.pallas import tpu as pltpu

F32 = jnp.float32
BF16 = jnp.bfloat16
I32 = jnp.int32

D_MODEL = 2048
N_HEADS = 8
N_KV_HEADS = 2
HEAD_DIM = 128
Q_PER_KV = N_HEADS // N_KV_HEADS
ATTN_WIDTH = N_HEADS * HEAD_DIM
KV_WIDTH = N_KV_HEADS * HEAD_DIM
POOL_WIDTH = D_MODEL - ATTN_WIDTH
POOL_SIZES = (2, 4, 8, 16)
POOL_GROUP = POOL_WIDTH // len(POOL_SIZES)
IN_WIDTH = ATTN_WIDTH + 2 * KV_WIDTH + POOL_WIDTH
WINDOW = 128
BLOCK = 128
GRID_W = 64
ROPE_THETA = 10000.0
ROPE_FREQS = HEAD_DIM // 4
N_EXPERTS = 16
EC_FACTOR = 2
D_EXPERT = 1024
N_MOD = 6
EPS = 1e-6
NEG = -1e30

LANES = 128
SUBLANES = 8
BF16_ROWS = 16
VMEM_CAP = 64 * 1024 * 1024

MOD_ROWS = 8
TOKEN_TILE = 256
H_EXT = D_MODEL + LANES
POOL_HALO = 8
COMBINE_CHUNK = BF16_ROWS


def _vmem_limit(nbytes):
    return int(min(VMEM_CAP - (4 << 20), max(nbytes, 16 << 20)))


def _mod_kernel(c_ref, w_ref, b_ref, o_ref):
    c = c_ref[...]
    s = c * jax.nn.sigmoid(c)
    o_ref[...] = jnp.dot(s.astype(BF16), w_ref[...].astype(BF16),
                         preferred_element_type=F32) + b_ref[...]


def _modulation(cond, w_ada, b_ada):
    n = w_ada.shape[1]
    tn = 1024
    return pl.pallas_call(
        _mod_kernel,
        grid=(n // tn,),
        in_specs=[pl.BlockSpec((MOD_ROWS, D_MODEL), lambda j: (0, 0)),
                  pl.BlockSpec((D_MODEL, tn), lambda j: (0, j)),
                  pl.BlockSpec((1, tn), lambda j: (0, j))],
        out_specs=pl.BlockSpec((MOD_ROWS, tn), lambda j: (0, j)),
        out_shape=jax.ShapeDtypeStruct((MOD_ROWS, n), F32),
        compiler_params=pltpu.CompilerParams(
            vmem_limit_bytes=_vmem_limit(3 * D_MODEL * tn * 4)),
        name="modulation",
    )(cond, w_ada, b_ada.reshape(1, n))


def _norm_mod(x, g, shift, scale):
    ms = jnp.mean(x * x, axis=-1, keepdims=True)
    y = x * lax.rsqrt(ms + EPS)
    return (y * g) * (1.0 + scale) + shift


def _inproj_kernel(*refs, rope):
    if rope:
        x_ref, mod_ref, g_ref, w_ref, cos_ref, sa_ref, sb_ref, q_ref, k_ref, v_ref, p_ref = refs
    else:
        x_ref, mod_ref, g_ref, w_ref, q_ref, k_ref, v_ref, p_ref = refs
    mod = mod_ref[0]
    h = _norm_mod(x_ref[...], g_ref[...], mod[0:1], mod[1:2])
    u = jnp.dot(h.astype(BF16), w_ref[...], preferred_element_type=F32)

    def rot(xh):
        return (xh * cos_ref[...] + pltpu.roll(xh, LANES - ROPE_FREQS, 1) * sa_ref[...]
                + pltpu.roll(xh, ROPE_FREQS, 1) * sb_ref[...])

    for hd in range(N_HEADS):
        xh = u[:, hd * HEAD_DIM:(hd + 1) * HEAD_DIM]
        q_ref[:, hd * HEAD_DIM:(hd + 1) * HEAD_DIM] = (rot(xh) if rope else xh).astype(BF16)
    for hd in range(N_KV_HEADS):
        lo = ATTN_WIDTH + hd * HEAD_DIM
        xh = u[:, lo:lo + HEAD_DIM]
        k_ref[:, hd * HEAD_DIM:(hd + 1) * HEAD_DIM] = rot(xh) if rope else xh
    v_ref[...] = u[:, ATTN_WIDTH + KV_WIDTH:ATTN_WIDTH + 2 * KV_WIDTH]
    p_ref[...] = u[:, ATTN_WIDTH + 2 * KV_WIDTH:]


def _in_projection(x2, mod3, mod_row, g, w_in_bf, rope_tabs):
    t = x2.shape[0]
    tm = TOKEN_TILE
    rope = rope_tabs is not None
    row = lambda i: (i, 0)
    in_specs = [pl.BlockSpec((tm, D_MODEL), row),
                pl.BlockSpec((1, N_MOD, D_MODEL), lambda i: (mod_row(i), 0, 0)),
                pl.BlockSpec((1, D_MODEL), lambda i: (0, 0)),
                pl.BlockSpec((D_MODEL, IN_WIDTH), lambda i: (0, 0))]
    args = [x2, mod3, g, w_in_bf]
    if rope:
        n_seq = rope_tabs[0].shape[0]
        seq_blocks = n_seq // tm
        for tab in rope_tabs:
            in_specs.append(pl.BlockSpec((tm, HEAD_DIM), lambda i: (i % seq_blocks, 0)))
            args.append(tab)
    return pl.pallas_call(
        functools.partial(_inproj_kernel, rope=rope),
        grid=(t // tm,),
        in_specs=in_specs,
        out_specs=[pl.BlockSpec((tm, ATTN_WIDTH), row),
                   pl.BlockSpec((tm, KV_WIDTH), row),
                   pl.BlockSpec((tm, KV_WIDTH), row),
                   pl.BlockSpec((tm, POOL_WIDTH), row)],
        out_shape=[jax.ShapeDtypeStruct((t, ATTN_WIDTH), BF16),
                   jax.ShapeDtypeStruct((t, KV_WIDTH), F32),
                   jax.ShapeDtypeStruct((t, KV_WIDTH), F32),
                   jax.ShapeDtypeStruct((t, POOL_WIDTH), F32)],
        compiler_params=pltpu.CompilerParams(
            vmem_limit_bytes=_vmem_limit(2 * D_MODEL * IN_WIDTH * 2 + 24 * tm * D_MODEL * 4)),
        name="in_projection",
    )(*args)


def _softmax_pv(s_list, v_list, sink_col):
    m = sink_col
    for s in s_list:
        m = jnp.maximum(m, jnp.max(s, axis=-1, keepdims=True))
    denom = jnp.exp(sink_col - m)
    out = None
    for s, v in zip(s_list, v_list):
        e = jnp.exp(s - m)
        denom = denom + jnp.sum(e, axis=-1, keepdims=True)
        o = jnp.dot(e.astype(BF16), v, preferred_element_type=F32)
        out = o if out is None else out + o
    return out * (1.0 / denom)


def _stack_heads(q, kv):
    return jnp.concatenate(
        [q[:, (kv * Q_PER_KV + g) * HEAD_DIM:(kv * Q_PER_KV + g + 1) * HEAD_DIM]
         for g in range(Q_PER_KV)], axis=0)


def _sink_column(sink_ref, kv, rows):
    r = lax.broadcasted_iota(I32, (Q_PER_KV * rows, 1), 0)
    col = jnp.zeros((Q_PER_KV * rows, 1), F32)
    for g in range(Q_PER_KV):
        col = jnp.where((r >= g * rows) & (r < (g + 1) * rows), sink_ref[kv * Q_PER_KV + g], col)
    return col


def _qk(q, k):
    return lax.dot_general(q, k, (((1,), (1,)), ((), ())), preferred_element_type=F32)


def _ctx_attn_kernel(sink_ref, q_ref, k_ref, v_ref, o_ref):
    rows = q_ref.shape[0]
    scale = HEAD_DIM ** -0.5
    q = q_ref[...]
    for kv in range(N_KV_HEADS):
        kh = k_ref[:, kv * HEAD_DIM:(kv + 1) * HEAD_DIM].astype(BF16)
        vh = v_ref[:, kv * HEAD_DIM:(kv + 1) * HEAD_DIM].astype(BF16)
        qs = _stack_heads(q, kv)
        s = _qk(qs, kh) * scale
        o = _softmax_pv([s], [vh], _sink_column(sink_ref, kv, rows))
        for g in range(Q_PER_KV):
            hd = kv * Q_PER_KV + g
            o_ref[:, hd * HEAD_DIM:(hd + 1) * HEAD_DIM] = o[g * rows:(g + 1) * rows].astype(BF16)


def _context_attention(q, k, v, sink, seq):
    t = q.shape[0]
    row = lambda b: (b, 0)
    return pl.pallas_call(
        _ctx_attn_kernel,
        grid=(t // seq,),
        in_specs=[pl.BlockSpec(memory_space=pltpu.SMEM),
                  pl.BlockSpec((seq, ATTN_WIDTH), row),
                  pl.BlockSpec((seq, KV_WIDTH), row),
                  pl.BlockSpec((seq, KV_WIDTH), row)],
        out_specs=pl.BlockSpec((seq, ATTN_WIDTH), row),
        out_shape=jax.ShapeDtypeStruct((t, ATTN_WIDTH), BF16),
        name="context_attention",
    )(sink, q, k, v)


def _lat_attn_kernel(sink_ref, q_ref, k_ref, v_ref, ck_ref, cv_ref, o_ref, *, n_seq):
    i = pl.program_id(1)
    scale = HEAD_DIM ** -0.5
    band = 3 * BLOCK
    start = pl.multiple_of(jnp.clip((i - 1) * BLOCK, 0, n_seq - band), BLOCK)
    rows = Q_PER_KV * BLOCK
    qpos = i * BLOCK + lax.broadcasted_iota(I32, (rows, band), 0) % BLOCK
    kpos = start + lax.broadcasted_iota(I32, (rows, band), 1)
    mask = jnp.abs(kpos - qpos) <= WINDOW
    q = q_ref[...]
    for kv in range(N_KV_HEADS):
        cols = slice(kv * HEAD_DIM, (kv + 1) * HEAD_DIM)
        kb = k_ref[pl.ds(start, band), cols].astype(BF16)
        vb = v_ref[pl.ds(start, band), cols].astype(BF16)
        ck = ck_ref[0, :, cols].astype(BF16)
        cv = cv_ref[0, :, cols].astype(BF16)
        qs = _stack_heads(q, kv)
        s_loc = jnp.where(mask, _qk(qs, kb) * scale, NEG)
        s_ctx = _qk(qs, ck) * scale
        o = _softmax_pv([s_loc, s_ctx], [vb, cv], _sink_column(sink_ref, kv, BLOCK))
        for g in range(Q_PER_KV):
            hd = kv * Q_PER_KV + g
            o_ref[:, hd * HEAD_DIM:(hd + 1) * HEAD_DIM] = o[g * BLOCK:(g + 1) * BLOCK].astype(BF16)


def _latent_attention(q, k, v, ck, cv, sink, n_seq):
    t = q.shape[0]
    nb = n_seq // BLOCK
    past = ck.shape[1]
    return pl.pallas_call(
        functools.partial(_lat_attn_kernel, n_seq=n_seq),
        grid=(t // n_seq, nb),
        in_specs=[pl.BlockSpec(memory_space=pltpu.SMEM),
                  pl.BlockSpec((BLOCK, ATTN_WIDTH), lambda b, i: (b * nb + i, 0)),
                  pl.BlockSpec((n_seq, KV_WIDTH), lambda b, i: (b, 0)),
                  pl.BlockSpec((n_seq, KV_WIDTH), lambda b, i: (b, 0)),
                  pl.BlockSpec((1, past, KV_WIDTH), lambda b, i: (b, 0, 0)),
                  pl.BlockSpec((1, past, KV_WIDTH), lambda b, i: (b, 0, 0))],
        out_specs=pl.BlockSpec((BLOCK, ATTN_WIDTH), lambda b, i: (b * nb + i, 0)),
        out_shape=jax.ShapeDtypeStruct((t, ATTN_WIDTH), BF16),
        name="latent_attention",
    )(sink, q, k, v, ck, cv)


def _pool_kernel(p_ref, w_ref, s_ref, o_ref, pad_ref):
    n = p_ref.shape[0]
    rows = n + 2 * POOL_HALO
    t = lax.broadcasted_iota(I32, (n, 1), 0)
    zeros = jnp.zeros((POOL_HALO, POOL_GROUP), F32)
    pad_ref[0:POOL_HALO, :] = zeros
    pad_ref[POOL_HALO + n:rows, :] = zeros
    for g, w in enumerate(POOL_SIZES):
        cols = slice(g * POOL_GROUP, (g + 1) * POOL_GROUP)
        pg = p_ref[:, cols]
        pad_ref[POOL_HALO:POOL_HALO + n, :] = pg
        x = pad_ref[...]
        acc = x + pltpu.roll(x, 1, 0)
        step = 1
        while 2 * step < w:
            acc = pltpu.roll(acc, step, 0) + pltpu.roll(acc, rows - step, 0)
            step *= 2
        wsum = acc[POOL_HALO:POOL_HALO + n]
        lo = jnp.maximum(t - w // 2, 0)
        hi = jnp.minimum(t + w - w // 2, n)
        inv_cnt = 1.0 / (hi - lo).astype(F32)
        mixed = wsum * inv_cnt - pg
        y = jnp.dot(mixed.astype(BF16), w_ref[g].astype(BF16), preferred_element_type=F32)
        o_ref[:, cols] = (y * s_ref[:, cols]).astype(BF16)


def _pool_mixer(p, pool_w, pool_scale, seq):
    t = p.shape[0]
    row = lambda b: (b, 0)
    return pl.pallas_call(
        _pool_kernel,
        grid=(t // seq,),
        in_specs=[pl.BlockSpec((seq, POOL_WIDTH), row),
                  pl.BlockSpec((len(POOL_SIZES), POOL_GROUP, POOL_GROUP), lambda b: (0, 0, 0)),
                  pl.BlockSpec((1, POOL_WIDTH), lambda b: (0, 0))],
        out_specs=pl.BlockSpec((seq, POOL_WIDTH), row),
        out_shape=jax.ShapeDtypeStruct((t, POOL_WIDTH), BF16),
        scratch_shapes=[pltpu.VMEM((seq + 2 * POOL_HALO, POOL_GROUP), F32)],
        compiler_params=pltpu.CompilerParams(
            vmem_limit_bytes=_vmem_limit(8 * seq * POOL_WIDTH * 4)),
        name="pool_mixer",
    )(p, pool_w, pool_scale)


def _outproj_kernel(a_ref, p_ref, x_ref, mod_ref, g_ref, wo_ref, wr_ref, x1_ref, h_ref, aff_ref):
    mod = mod_ref[0]
    mix = (jnp.dot(a_ref[...], wo_ref[0:ATTN_WIDTH, :], preferred_element_type=F32)
           + jnp.dot(p_ref[...], wo_ref[ATTN_WIDTH:D_MODEL, :], preferred_element_type=F32))
    x1 = x_ref[...] + mod[2:3] * mix
    x1_ref[...] = x1
    h = _norm_mod(x1, g_ref[...], mod[3:4], mod[4:5])
    logits = jnp.dot(h.astype(BF16), wr_ref[...], preferred_element_type=F32)
    lane = lax.broadcasted_iota(I32, logits.shape, 1)
    logits = jnp.where(lane < N_EXPERTS, logits, -jnp.inf)
    m = jnp.max(logits, axis=-1, keepdims=True)
    e = jnp.exp(logits - m)
    aff = e / jnp.sum(e, axis=-1, keepdims=True)
    aff_ref[...] = aff[:, 0:N_EXPERTS]
    h_ref[:, 0:D_MODEL] = h
    h_ref[:, D_MODEL:H_EXT] = aff


def _out_projection(attn, pooled, x2, mod3, mod_row, g, w_out_bf, w_router):
    t = x2.shape[0]
    tm = TOKEN_TILE
    row = lambda i: (i, 0)
    return pl.pallas_call(
        _outproj_kernel,
        grid=(t // tm,),
        in_specs=[pl.BlockSpec((tm, ATTN_WIDTH), row),
                  pl.BlockSpec((tm, POOL_WIDTH), row),
                  pl.BlockSpec((tm, D_MODEL), row),
                  pl.BlockSpec((1, N_MOD, D_MODEL), lambda i: (mod_row(i), 0, 0)),
                  pl.BlockSpec((1, D_MODEL), lambda i: (0, 0)),
                  pl.BlockSpec((D_MODEL, D_MODEL), lambda i: (0, 0)),
                  pl.BlockSpec((D_MODEL, LANES), lambda i: (0, 0))],
        out_specs=[pl.BlockSpec((tm, D_MODEL), row),
                   pl.BlockSpec((tm, H_EXT), row),
                   pl.BlockSpec((tm, N_EXPERTS), row)],
        out_shape=[jax.ShapeDtypeStruct((t, D_MODEL), F32),
                   jax.ShapeDtypeStruct((t, H_EXT), F32),
                   jax.ShapeDtypeStruct((t, N_EXPERTS), F32)],
        compiler_params=pltpu.CompilerParams(
            vmem_limit_bytes=_vmem_limit(2 * D_MODEL * D_MODEL * 2 + 24 * tm * D_MODEL * 4)),
        name="out_projection",
    )(attn, pooled, x2, mod3, g, w_out_bf, w_router)


def _route_kernel(a_ref, idx_ref, slot_ref, off_ref, *, cap):
    a = a_ref[...]
    n_e, n_c, _ = a.shape
    rows = n_e * n_c

    def bisect(it, thr):
        cand = thr | jnp.left_shift(jnp.int32(1), 30 - it)
        cand_f = lax.bitcast_convert_type(cand, F32)
        cnt = jnp.sum(jnp.sum((a >= cand_f).astype(F32), axis=2, keepdims=True),
                      axis=1, keepdims=True)
        return jnp.where(cnt >= cap, cand, thr)

    thr = lax.fori_loop(0, 31, bisect, jnp.zeros((n_e, 1, 1), I32))
    thr_f = lax.bitcast_convert_type(thr, F32)
    gt = (a > thr_f).astype(F32).reshape(rows, LANES)
    eq = (a == thr_f).astype(F32).reshape(rows, LANES)

    li = lax.broadcasted_iota(I32, (LANES, LANES), 0)
    lj = lax.broadcasted_iota(I32, (LANES, LANES), 1)
    upper_incl = (li <= lj).astype(BF16)
    ri = lax.broadcasted_iota(I32, (rows, rows), 0)
    rj = lax.broadcasted_iota(I32, (rows, rows), 1)
    same_expert = (ri // n_c) == (rj // n_c)
    before = (same_expert & (rj < ri)).astype(BF16)
    whole = same_expert.astype(BF16)

    def lane_bcast(col):
        return jnp.broadcast_to(col, (rows, LANES)).astype(BF16)

    def prefix(x):
        incl = jnp.dot(x.astype(BF16), upper_incl, preferred_element_type=F32)
        tot = incl[:, LANES - 1:LANES]
        off = jnp.dot(before, lane_bcast(tot), preferred_element_type=F32)
        return incl, tot, off

    n_gt = jnp.dot(whole, lane_bcast(jnp.sum(gt, axis=1, keepdims=True)),
                   preferred_element_type=F32)
    need = cap - n_gt
    incl_eq, _, off_eq = prefix(eq)
    rank_eq = off_eq + incl_eq - eq
    sel = jnp.where((eq > 0) & (rank_eq < need), 1.0, gt)
    incl, tot, off = prefix(sel)
    slot = off + incl - sel
    slot_ref[...] = jnp.where(sel > 0, slot, -1.0).astype(I32).reshape(n_e, n_c, LANES)
    off_ref[...] = off.astype(I32).reshape(n_e, n_c, LANES)

    s_lane = lax.broadcasted_iota(I32, (1, cap), 1).astype(F32)
    c_col = lax.broadcasted_iota(I32, (n_c, 1), 0).astype(F32)
    for e in range(n_e):
        r0 = e * n_c
        incl_e = incl[r0:r0 + n_c]
        off_e = off[r0:r0 + n_c, 0:1]
        tot_e = tot[r0:r0 + n_c]
        onehot = ((off_e <= s_lane) & (s_lane < off_e + tot_e)).astype(F32)
        counts = lax.dot_general(incl_e.astype(BF16), onehot.astype(BF16),
                                 (((0,), (0,)), ((), ())), preferred_element_type=F32)
        local = s_lane - jnp.sum(onehot * off_e, axis=0, keepdims=True)
        lane = jnp.sum((counts <= local).astype(F32), axis=0, keepdims=True)
        chunk = jnp.sum(onehot * c_col, axis=0, keepdims=True)
        idx_ref[e] = (chunk * LANES + lane).astype(I32)


def _routing(aff, cap):
    t = aff.shape[0]
    n_c = t // LANES
    a3 = aff.T.reshape(N_EXPERTS, n_c, LANES)
    return pl.pallas_call(
        functools.partial(_route_kernel, cap=cap),
        out_shape=[jax.ShapeDtypeStruct((N_EXPERTS, 1, cap), I32),
                   jax.ShapeDtypeStruct((N_EXPERTS, n_c, LANES), I32),
                   jax.ShapeDtypeStruct((N_EXPERTS, n_c, LANES), I32)],
        compiler_params=pltpu.CompilerParams(vmem_limit_bytes=_vmem_limit(48 << 20)),
        name="routing",
    )(a3)


def _ffn_kernel(idx_ref, h_hbm, wg_ref, wu_ref, wd_ref, y_ref, rows_ref, x_ref, acc_ref, sem,
                *, cap, n_f):
    e = pl.program_id(0)
    f = pl.program_id(1)

    def row_copy(src_row, dst_row):
        return pltpu.make_async_copy(h_hbm.at[pl.ds(src_row, 1)],
                                     rows_ref.at[pl.ds(dst_row, 1)], sem)

    @pl.when(f == 0)
    def _():
        def issue(s, carry):
            row_copy(idx_ref[0, 0, s], s).start()
            return carry
        lax.fori_loop(0, cap, issue, 0)
        pltpu.make_async_copy(h_hbm.at[pl.ds(0, cap)], rows_ref, sem).wait()
        x_ref[...] = rows_ref[:, 0:D_MODEL].astype(BF16)
        acc_ref[...] = jnp.zeros_like(acc_ref)

    x = x_ref[...]
    gate_act = jnp.dot(x, wg_ref[0].astype(BF16), preferred_element_type=F32)
    up = jnp.dot(x, wu_ref[0].astype(BF16), preferred_element_type=F32)
    hid = (gate_act * jax.nn.sigmoid(gate_act)) * up
    acc_ref[...] += jnp.dot(hid.astype(BF16), wd_ref[0].astype(BF16), preferred_element_type=F32)

    @pl.when(f == n_f - 1)
    def _():
        aff = rows_ref[:, D_MODEL:H_EXT]
        lane = lax.broadcasted_iota(I32, aff.shape, 1)
        gates = jnp.sum(jnp.where(lane == e, aff, 0.0), axis=1, keepdims=True)
        y_ref[0] = (acc_ref[...] * gates).astype(BF16)


def _expert_ffn(idx, h_ext, w_gate, w_up, w_down, cap):
    n_f = 4
    tf = D_EXPERT // n_f
    est = (cap * H_EXT * 4 + cap * D_MODEL * (2 + 4 + 2 * 2) + 2 * 3 * D_MODEL * tf * 4
           + 3 * D_MODEL * tf * 2 + 4 * cap * tf * 4)
    return pl.pallas_call(
        functools.partial(_ffn_kernel, cap=cap, n_f=n_f),
        grid=(N_EXPERTS, n_f),
        in_specs=[pl.BlockSpec((1, 1, cap), lambda e, f: (e, 0, 0), memory_space=pltpu.SMEM),
                  pl.BlockSpec(memory_space=pl.ANY),
                  pl.BlockSpec((1, D_MODEL, tf), lambda e, f: (e, 0, f)),
                  pl.BlockSpec((1, D_MODEL, tf), lambda e, f: (e, 0, f)),
                  pl.BlockSpec((1, tf, D_MODEL), lambda e, f: (e, f, 0))],
        out_specs=pl.BlockSpec((1, cap, D_MODEL), lambda e, f: (e, 0, 0)),
        out_shape=jax.ShapeDtypeStruct((N_EXPERTS, cap, D_MODEL), BF16),
        scratch_shapes=[pltpu.VMEM((cap, H_EXT), F32),
                        pltpu.VMEM((cap, D_MODEL), BF16),
                        pltpu.VMEM((cap, D_MODEL), F32),
                        pltpu.SemaphoreType.DMA],
        compiler_params=pltpu.CompilerParams(
            dimension_semantics=("arbitrary", "arbitrary"),
            vmem_limit_bytes=_vmem_limit(est + (6 << 20))),
        name="expert_ffn",
    )(idx, h_ext, w_gate, w_up, w_down)


def _combine_kernel(start_ref, nch_ref, x1_ref, mod_ref, g_ref, slot_ref, y_hbm, o_ref,
                    ybuf_ref, ffn_ref, sem, *, kblock):
    i = pl.program_id(0)
    tm = x1_ref.shape[0]

    @pl.when(i == 0)
    def _():
        ybuf_ref[...] = jnp.zeros_like(ybuf_ref)

    def chunk_copy(e, src_row, dst_row):
        return pltpu.make_async_copy(
            y_hbm.at[e, pl.ds(pl.multiple_of(src_row, COMBINE_CHUNK), COMBINE_CHUNK)],
            ybuf_ref.at[pl.ds(pl.multiple_of(dst_row, COMBINE_CHUNK), COMBINE_CHUNK)], sem)

    pos = jnp.int32(0)
    bases = []
    for e in range(N_EXPERTS):
        start = start_ref[i, e]
        nch = nch_ref[i, e]
        bases.append(pos - start)

        def issue(c, carry, e=e, start=start, pos=pos):
            chunk_copy(e, start + c * COMBINE_CHUNK, pos + c * COMBINE_CHUNK).start()
            return carry
        lax.fori_loop(0, nch, issue, 0)
        pos = pos + nch * COMBINE_CHUNK
    total_chunks = pos // COMBINE_CHUNK

    def drain(c, carry):
        chunk_copy(0, jnp.int32(0), jnp.int32(0)).wait()
        return carry
    lax.fori_loop(0, total_chunks, drain, 0)

    slots = slot_ref[...]
    cols = []
    for e in range(N_EXPERTS):
        s_e = slots[:, e:e + 1]
        cols.append(jnp.where(s_e >= 0, s_e + bases[e], -1))

    ffn_ref[...] = jnp.zeros_like(ffn_ref)
    lane = lax.broadcasted_iota(I32, (tm, kblock), 1)

    def kstep(kb, carry):
        k0 = pl.multiple_of(kb * kblock, kblock)
        hit = cols[0] - k0 == lane
        for e in range(1, N_EXPERTS):
            hit = hit | (cols[e] - k0 == lane)
        sel = jnp.where(hit, 1.0, 0.0).astype(BF16)
        ffn_ref[...] += jnp.dot(sel, ybuf_ref[pl.ds(k0, kblock), :], preferred_element_type=F32)
        return carry
    lax.fori_loop(0, (pos + kblock - 1) // kblock, kstep, 0)

    out = x1_ref[...] + mod_ref[0][5:6] * ffn_ref[...]
    ms = jnp.mean(out * out, axis=-1, keepdims=True)
    o_ref[...] = (out * lax.rsqrt(ms + EPS)) * g_ref[...]


def _combine(x1, mod3, mod_row, g_final, slot_t, tile_start, tile_nch, y):
    t = x1.shape[0]
    tm = TOKEN_TILE
    kblock = 256
    max_rows = N_EXPERTS * (tm + 2 * COMBINE_CHUNK)
    max_rows = -(-max_rows // kblock) * kblock
    row = lambda i, *_: (i, 0)
    grid_spec = pltpu.PrefetchScalarGridSpec(
        num_scalar_prefetch=2,
        grid=(t // tm,),
        in_specs=[pl.BlockSpec((tm, D_MODEL), row),
                  pl.BlockSpec((1, N_MOD, D_MODEL), lambda i, *_: (mod_row(i), 0, 0)),
                  pl.BlockSpec((1, D_MODEL), lambda i, *_: (0, 0)),
                  pl.BlockSpec((tm, N_EXPERTS), row),
                  pl.BlockSpec(memory_space=pl.ANY)],
        out_specs=pl.BlockSpec((tm, D_MODEL), row),
        scratch_shapes=[pltpu.VMEM((max_rows, D_MODEL), BF16),
                        pltpu.VMEM((tm, D_MODEL), F32),
                        pltpu.SemaphoreType.DMA],
    )
    return pl.pallas_call(
        functools.partial(_combine_kernel, kblock=kblock),
        grid_spec=grid_spec,
        out_shape=jax.ShapeDtypeStruct((t, D_MODEL), F32),
        compiler_params=pltpu.CompilerParams(
            dimension_semantics=("arbitrary",),
            vmem_limit_bytes=_vmem_limit(max_rows * D_MODEL * 2 + 16 * tm * D_MODEL * 4)),
        name="combine",
    )(tile_start, tile_nch, x1, mod3, g_final, slot_t, y)


def _rope_tables(n):
    rows = n // GRID_W
    row = jnp.repeat(jnp.arange(rows, dtype=F32), GRID_W)
    col = jnp.tile(jnp.arange(GRID_W, dtype=F32), rows)
    inv = ROPE_THETA ** (-jnp.arange(ROPE_FREQS, dtype=F32) / ROPE_FREQS)
    ang_r = row[:, None] * inv
    ang_c = col[:, None] * inv
    zero = jnp.zeros_like(ang_r)
    cos = jnp.concatenate([jnp.cos(ang_r)] * 2 + [jnp.cos(ang_c)] * 2, axis=1)
    sin_a = jnp.concatenate([-jnp.sin(ang_r), zero, -jnp.sin(ang_c), zero], axis=1)
    sin_b = jnp.concatenate([zero, jnp.sin(ang_r), zero, jnp.sin(ang_c)], axis=1)
    return cos, sin_a, sin_b


def _token_group(x, mod3, mod_row, seq, weights, rope_tabs, ctx_kv):
    (norm_mix, w_in_bf, sink, pool_w, pool_scale, w_out_bf, norm_ffn, w_router,
     w_gate, w_up, w_down, norm_final) = weights
    b = x.shape[0]
    t = b * seq
    x2 = x.reshape(t, D_MODEL)
    q, k, v, p = _in_projection(x2, mod3, mod_row, norm_mix, w_in_bf, rope_tabs)
    if ctx_kv is None:
        attn = _context_attention(q, k, v, sink, seq)
    else:
        attn = _latent_attention(q, k, v, ctx_kv[0], ctx_kv[1], sink, seq)
    pooled = _pool_mixer(p, pool_w, pool_scale, seq)
    x1, h_ext, aff = _out_projection(attn, pooled, x2, mod3, mod_row, norm_ffn, w_out_bf, w_router)

    cap = EC_FACTOR * t // N_EXPERTS
    idx, slot3, off3 = _routing(aff, cap)
    y = _expert_ffn(idx, h_ext, w_gate, w_up, w_down, cap)

    chunks_per_tile = TOKEN_TILE // LANES
    tile_off = off3[:, ::chunks_per_tile, 0]
    tile_end = jnp.concatenate([tile_off[:, 1:], jnp.full((N_EXPERTS, 1), cap, I32)], axis=1)
    tile_start = (tile_off // COMBINE_CHUNK) * COMBINE_CHUNK
    tile_nch = jnp.where(tile_end > tile_off,
                         (tile_end - tile_start + COMBINE_CHUNK - 1) // COMBINE_CHUNK, 0)
    slot_t = slot3.reshape(N_EXPERTS, t).T
    out = _combine(x1, mod3, mod_row, norm_final, slot_t, tile_start.T, tile_nch.T, y)
    return out.reshape(b, seq, D_MODEL), k, v


def kernel(x_prompt, x_sample, c, cache_k, cache_v, c_ctx, w_ada, b_ada, norm_mix, w_in,
           sink_logits, pool_w, pool_scale, w_out, norm_ffn, w_router, w_gate, w_up, w_down,
           norm_final):
    n_b, seq, _ = x_prompt.shape
    n_db, n_lat, _ = x_sample.shape
    assert 1 + n_db <= MOD_ROWS and seq == TOKEN_TILE and n_lat % TOKEN_TILE == 0

    cond = jnp.concatenate(
        [c_ctx[None, :], c, jnp.zeros((MOD_ROWS - 1 - n_db, D_MODEL), F32)], axis=0)
    mod3 = _modulation(cond, w_ada[0], b_ada[0]).reshape(MOD_ROWS, N_MOD, D_MODEL)

    w_router_bf = jnp.pad(w_router[0], ((0, 0), (0, LANES - N_EXPERTS))).astype(BF16)
    weights = (norm_mix[0][None, :], w_in[0].astype(BF16), sink_logits[0], pool_w[0],
               pool_scale[0][None, :], w_out[0].astype(BF16), norm_ffn[0][None, :], w_router_bf,
               w_gate.reshape(w_gate.shape[1:]), w_up.reshape(w_up.shape[1:]),
               w_down.reshape(w_down.shape[1:]), norm_final[None, :])

    y_prompt, k_p, v_p = _token_group(
        x_prompt, mod3, lambda i: 0, seq, weights, None, None)

    tiles_per_seq = n_lat // TOKEN_TILE
    past = cache_k.shape[2]
    ck = cache_k[:, 0].reshape(n_db, past, KV_WIDTH)
    cv = cache_v[:, 0].reshape(n_db, past, KV_WIDTH)
    y_sample, _, _ = _token_group(
        x_sample, mod3, lambda i: 1 + i // tiles_per_seq, n_lat, weights,
        _rope_tables(n_lat), (ck, cv))

    state_k = k_p.reshape(n_b, 1, seq, N_KV_HEADS, HEAD_DIM)
    state_v = v_p.reshape(n_b, 1, seq, N_KV_HEADS, HEAD_DIM)
    return (y_prompt, y_sample, state_k, state_v)
```

```python
import functools

import jax
import jax.numpy as jnp
import numpy as np
from jax import lax
from jax.experimental import pallas as pl
from jax.experimental.pallas import tpu as pltpu

F32 = jnp.float32
BF16 = jnp.bfloat16
I32 = jnp.int32

D_MODEL = 2048
N_HEADS = 8
N_KV_HEADS = 2
HEAD_DIM = 128
Q_PER_KV = N_HEADS // N_KV_HEADS
ATTN_WIDTH = N_HEADS * HEAD_DIM
KV_WIDTH = N_KV_HEADS * HEAD_DIM
POOL_WIDTH = D_MODEL - ATTN_WIDTH
POOL_SIZES = (2, 4, 8, 16)
POOL_GROUP = POOL_WIDTH // len(POOL_SIZES)
IN_WIDTH = ATTN_WIDTH + 2 * KV_WIDTH + POOL_WIDTH
WINDOW = 128
BLOCK = 128
GRID_W = 64
ROPE_THETA = 10000.0
ROPE_FREQS = HEAD_DIM // 4
N_EXPERTS = 16
EC_FACTOR = 2
D_EXPERT = 1024
N_MOD = 6
EPS = 1e-6
NEG = -1e30

LANES = 128
SUBLANES = 8
BF16_ROWS = 16
VMEM_CAP = 64 * 1024 * 1024

MOD_ROWS = 8
TOKEN_TILE = 256
H_EXT = D_MODEL + LANES
POOL_HALO = 8
COMBINE_CHUNK = BF16_ROWS


def _vmem_limit(nbytes):
    return int(min(VMEM_CAP - (4 << 20), max(nbytes, 16 << 20)))


def _mod_kernel(c_ref, w_ref, b_ref, o_ref):
    c = c_ref[...]
    s = c * jax.nn.sigmoid(c)
    o_ref[...] = jnp.dot(s.astype(BF16), w_ref[...].astype(BF16),
                         preferred_element_type=F32) + b_ref[...]


def _modulation(cond, w_ada, b_ada):
    n = w_ada.shape[1]
    tn = 1024
    return pl.pallas_call(
        _mod_kernel,
        grid=(n // tn,),
        in_specs=[pl.BlockSpec((MOD_ROWS, D_MODEL), lambda j: (0, 0)),
                  pl.BlockSpec((D_MODEL, tn), lambda j: (0, j)),
                  pl.BlockSpec((1, tn), lambda j: (0, j))],
        out_specs=pl.BlockSpec((MOD_ROWS, tn), lambda j: (0, j)),
        out_shape=jax.ShapeDtypeStruct((MOD_ROWS, n), F32),
        compiler_params=pltpu.CompilerParams(
            vmem_limit_bytes=_vmem_limit(3 * D_MODEL * tn * 4)),
        name="modulation",
    )(cond, w_ada, b_ada.reshape(1, n))


def _norm_mod(x, g, shift, scale):
    ms = jnp.mean(x * x, axis=-1, keepdims=True)
    y = x * lax.rsqrt(ms + EPS)
    return (y * g) * (1.0 + scale) + shift


def _inproj_kernel(*refs, rope):
    if rope:
        x_ref, mod_ref, g_ref, w_ref, cos_ref, sa_ref, sb_ref, q_ref, k_ref, v_ref, p_ref = refs
    else:
        x_ref, mod_ref, g_ref, w_ref, q_ref, k_ref, v_ref, p_ref = refs
    mod = mod_ref[0]
    h = _norm_mod(x_ref[...], g_ref[...], mod[0:1], mod[1:2])
    u = jnp.dot(h.astype(BF16), w_ref[...], preferred_element_type=F32)

    def rot(xh):
        return (xh * cos_ref[...] + pltpu.roll(xh, LANES - ROPE_FREQS, 1) * sa_ref[...]
                + pltpu.roll(xh, ROPE_FREQS, 1) * sb_ref[...])

    for hd in range(N_HEADS):
        xh = u[:, hd * HEAD_DIM:(hd + 1) * HEAD_DIM]
        q_ref[:, hd * HEAD_DIM:(hd + 1) * HEAD_DIM] = (rot(xh) if rope else xh).astype(BF16)
    for hd in range(N_KV_HEADS):
        lo = ATTN_WIDTH + hd * HEAD_DIM
        xh = u[:, lo:lo + HEAD_DIM]
        k_ref[:, hd * HEAD_DIM:(hd + 1) * HEAD_DIM] = rot(xh) if rope else xh
    v_ref[...] = u[:, ATTN_WIDTH + KV_WIDTH:ATTN_WIDTH + 2 * KV_WIDTH]
    p_ref[...] = u[:, ATTN_WIDTH + 2 * KV_WIDTH:]


def _in_projection(x2, mod3, mod_row, g, w_in_bf, rope_tabs):
    t = x2.shape[0]
    tm = TOKEN_TILE
    rope = rope_tabs is not None
    row = lambda i: (i, 0)
    in_specs = [pl.BlockSpec((tm, D_MODEL), row),
                pl.BlockSpec((1, N_MOD, D_MODEL), lambda i: (mod_row(i), 0, 0)),
                pl.BlockSpec((1, D_MODEL), lambda i: (0, 0)),
                pl.BlockSpec((D_MODEL, IN_WIDTH), lambda i: (0, 0))]
    args = [x2, mod3, g, w_in_bf]
    if rope:
        n_seq = rope_tabs[0].shape[0]
        seq_blocks = n_seq // tm
        for tab in rope_tabs:
            in_specs.append(pl.BlockSpec((tm, HEAD_DIM), lambda i: (i % seq_blocks, 0)))
            args.append(tab)
    return pl.pallas_call(
        functools.partial(_inproj_kernel, rope=rope),
        grid=(t // tm,),
        in_specs=in_specs,
        out_specs=[pl.BlockSpec((tm, ATTN_WIDTH), row),
                   pl.BlockSpec((tm, KV_WIDTH), row),
                   pl.BlockSpec((tm, KV_WIDTH), row),
                   pl.BlockSpec((tm, POOL_WIDTH), row)],
        out_shape=[jax.ShapeDtypeStruct((t, ATTN_WIDTH), BF16),
                   jax.ShapeDtypeStruct((t, KV_WIDTH), F32),
                   jax.ShapeDtypeStruct((t, KV_WIDTH), F32),
                   jax.ShapeDtypeStruct((t, POOL_WIDTH), F32)],
        compiler_params=pltpu.CompilerParams(
            vmem_limit_bytes=_vmem_limit(2 * D_MODEL * IN_WIDTH * 2 + 24 * tm * D_MODEL * 4)),
        name="in_projection",
    )(*args)


def _softmax_pv(s_list, v_list, sink_col):
    m = sink_col
    for s in s_list:
        m = jnp.maximum(m, jnp.max(s, axis=-1, keepdims=True))
    denom = jnp.exp(sink_col - m)
    out = None
    for s, v in zip(s_list, v_list):
        e = jnp.exp(s - m)
        denom = denom + jnp.sum(e, axis=-1, keepdims=True)
        o = jnp.dot(e.astype(BF16), v, preferred_element_type=F32)
        out = o if out is None else out + o
    return out * (1.0 / denom)


def _stack_heads(q, kv):
    return jnp.concatenate(
        [q[:, (kv * Q_PER_KV + g) * HEAD_DIM:(kv * Q_PER_KV + g + 1) * HEAD_DIM]
         for g in range(Q_PER_KV)], axis=0)


def _sink_column(sink_ref, kv, rows):
    r = lax.broadcasted_iota(I32, (Q_PER_KV * rows, 1), 0)
    col = jnp.zeros((Q_PER_KV * rows, 1), F32)
    for g in range(Q_PER_KV):
        col = jnp.where((r >= g * rows) & (r < (g + 1) * rows), sink_ref[kv * Q_PER_KV + g], col)
    return col


def _qk(q, k):
    return lax.dot_general(q, k, (((1,), (1,)), ((), ())), preferred_element_type=F32)


def _ctx_attn_kernel(sink_ref, q_ref, k_ref, v_ref, o_ref):
    rows = q_ref.shape[0]
    scale = HEAD_DIM ** -0.5
    q = q_ref[...]
    for kv in range(N_KV_HEADS):
        kh = k_ref[:, kv * HEAD_DIM:(kv + 1) * HEAD_DIM].astype(BF16)
        vh = v_ref[:, kv * HEAD_DIM:(kv + 1) * HEAD_DIM].astype(BF16)
        qs = _stack_heads(q, kv)
        s = _qk(qs, kh) * scale
        o = _softmax_pv([s], [vh], _sink_column(sink_ref, kv, rows))
        for g in range(Q_PER_KV):
            hd = kv * Q_PER_KV + g
            o_ref[:, hd * HEAD_DIM:(hd + 1) * HEAD_DIM] = o[g * rows:(g + 1) * rows].astype(BF16)


def _context_attention(q, k, v, sink, seq):
    t = q.shape[0]
    row = lambda b: (b, 0)
    return pl.pallas_call(
        _ctx_attn_kernel,
        grid=(t // seq,),
        in_specs=[pl.BlockSpec(memory_space=pltpu.SMEM),
                  pl.BlockSpec((seq, ATTN_WIDTH), row),
                  pl.BlockSpec((seq, KV_WIDTH), row),
                  pl.BlockSpec((seq, KV_WIDTH), row)],
        out_specs=pl.BlockSpec((seq, ATTN_WIDTH), row),
        out_shape=jax.ShapeDtypeStruct((t, ATTN_WIDTH), BF16),
        name="context_attention",
    )(sink, q, k, v)


def _lat_attn_kernel(sink_ref, q_ref, k_ref, v_ref, ck_ref, cv_ref, o_ref, *, n_seq):
    i = pl.program_id(1)
    scale = HEAD_DIM ** -0.5
    band = 3 * BLOCK
    start = pl.multiple_of(jnp.clip((i - 1) * BLOCK, 0, n_seq - band), BLOCK)
    rows = Q_PER_KV * BLOCK
    qpos = i * BLOCK + lax.broadcasted_iota(I32, (rows, band), 0) % BLOCK
    kpos = start + lax.broadcasted_iota(I32, (rows, band), 1)
    mask = jnp.abs(kpos - qpos) <= WINDOW
    q = q_ref[...]
    for kv in range(N_KV_HEADS):
        cols = slice(kv * HEAD_DIM, (kv + 1) * HEAD_DIM)
        kb = k_ref[pl.ds(start, band), cols].astype(BF16)
        vb = v_ref[pl.ds(start, band), cols].astype(BF16)
        ck = ck_ref[0, :, cols].astype(BF16)
        cv = cv_ref[0, :, cols].astype(BF16)
        qs = _stack_heads(q, kv)
        s_loc = jnp.where(mask, _qk(qs, kb) * scale, NEG)
        s_ctx = _qk(qs, ck) * scale
        o = _softmax_pv([s_loc, s_ctx], [vb, cv], _sink_column(sink_ref, kv, BLOCK))
        for g in range(Q_PER_KV):
            hd = kv * Q_PER_KV + g
            o_ref[:, hd * HEAD_DIM:(hd + 1) * HEAD_DIM] = o[g * BLOCK:(g + 1) * BLOCK].astype(BF16)


def _latent_attention(q, k, v, ck, cv, sink, n_seq):
    t = q.shape[0]
    nb = n_seq // BLOCK
    past = ck.shape[1]
    return pl.pallas_call(
        functools.partial(_lat_attn_kernel, n_seq=n_seq),
        grid=(t // n_seq, nb),
        in_specs=[pl.BlockSpec(memory_space=pltpu.SMEM),
                  pl.BlockSpec((BLOCK, ATTN_WIDTH), lambda b, i: (b * nb + i, 0)),
                  pl.BlockSpec((n_seq, KV_WIDTH), lambda b, i: (b, 0)),
                  pl.BlockSpec((n_seq, KV_WIDTH), lambda b, i: (b, 0)),
                  pl.BlockSpec((1, past, KV_WIDTH), lambda b, i: (b, 0, 0)),
                  pl.BlockSpec((1, past, KV_WIDTH), lambda b, i: (b, 0, 0))],
        out_specs=pl.BlockSpec((BLOCK, ATTN_WIDTH), lambda b, i: (b * nb + i, 0)),
        out_shape=jax.ShapeDtypeStruct((t, ATTN_WIDTH), BF16),
        name="latent_attention",
    )(sink, q, k, v, ck, cv)


def _pool_kernel(p_ref, w_ref, s_ref, o_ref, pad_ref):
    n = p_ref.shape[0]
    rows = n + 2 * POOL_HALO
    t = lax.broadcasted_iota(I32, (n, 1), 0)
    zeros = jnp.zeros((POOL_HALO, POOL_GROUP), F32)
    pad_ref[0:POOL_HALO, :] = zeros
    pad_ref[POOL_HALO + n:rows, :] = zeros
    for g, w in enumerate(POOL_SIZES):
        cols = slice(g * POOL_GROUP, (g + 1) * POOL_GROUP)
        pg = p_ref[:, cols]
        pad_ref[POOL_HALO:POOL_HALO + n, :] = pg
        x = pad_ref[...]
        acc = x + pltpu.roll(x, 1, 0)
        step = 1
        while 2 * step < w:
            acc = pltpu.roll(acc, step, 0) + pltpu.roll(acc, rows - step, 0)
            step *= 2
        wsum = acc[POOL_HALO:POOL_HALO + n]
        lo = jnp.maximum(t - w // 2, 0)
        hi = jnp.minimum(t + w - w // 2, n)
        inv_cnt = 1.0 / (hi - lo).astype(F32)
        mixed = wsum * inv_cnt - pg
        y = jnp.dot(mixed.astype(BF16), w_ref[g].astype(BF16), preferred_element_type=F32)
        o_ref[:, cols] = (y * s_ref[:, cols]).astype(BF16)


def _pool_mixer(p, pool_w, pool_scale, seq):
    t = p.shape[0]
    row = lambda b: (b, 0)
    return pl.pallas_call(
        _pool_kernel,
        grid=(t // seq,),
        in_specs=[pl.BlockSpec((seq, POOL_WIDTH), row),
                  pl.BlockSpec((len(POOL_SIZES), POOL_GROUP, POOL_GROUP), lambda b: (0, 0, 0)),
                  pl.BlockSpec((1, POOL_WIDTH), lambda b: (0, 0))],
        out_specs=pl.BlockSpec((seq, POOL_WIDTH), row),
        out_shape=jax.ShapeDtypeStruct((t, POOL_WIDTH), BF16),
        scratch_shapes=[pltpu.VMEM((seq + 2 * POOL_HALO, POOL_GROUP), F32)],
        compiler_params=pltpu.CompilerParams(
            vmem_limit_bytes=_vmem_limit(8 * seq * POOL_WIDTH * 4)),
        name="pool_mixer",
    )(p, pool_w, pool_scale)


def _outproj_kernel(a_ref, p_ref, x_ref, mod_ref, g_ref, wo_ref, wr_ref, x1_ref, h_ref, aff_ref):
    mod = mod_ref[0]
    mix = (jnp.dot(a_ref[...], wo_ref[0:ATTN_WIDTH, :], preferred_element_type=F32)
           + jnp.dot(p_ref[...], wo_ref[ATTN_WIDTH:D_MODEL, :], preferred_element_type=F32))
    x1 = x_ref[...] + mod[2:3] * mix
    x1_ref[...] = x1
    h = _norm_mod(x1, g_ref[...], mod[3:4], mod[4:5])
    logits = jnp.dot(h.astype(BF16), wr_ref[...], preferred_element_type=F32)
    lane = lax.broadcasted_iota(I32, logits.shape, 1)
    logits = jnp.where(lane < N_EXPERTS, logits, -jnp.inf)
    m = jnp.max(logits, axis=-1, keepdims=True)
    e = jnp.exp(logits - m)
    aff = e / jnp.sum(e, axis=-1, keepdims=True)
    aff_ref[...] = aff[:, 0:N_EXPERTS]
    h_ref[:, 0:D_MODEL] = h
    h_ref[:, D_MODEL:H_EXT] = aff


def _out_projection(attn, pooled, x2, mod3, mod_row, g, w_out_bf, w_router):
    t = x2.shape[0]
    tm = TOKEN_TILE
    row = lambda i: (i, 0)
    return pl.pallas_call(
        _outproj_kernel,
        grid=(t // tm,),
        in_specs=[pl.BlockSpec((tm, ATTN_WIDTH), row),
                  pl.BlockSpec((tm, POOL_WIDTH), row),
                  pl.BlockSpec((tm, D_MODEL), row),
                  pl.BlockSpec((1, N_MOD, D_MODEL), lambda i: (mod_row(i), 0, 0)),
                  pl.BlockSpec((1, D_MODEL), lambda i: (0, 0)),
                  pl.BlockSpec((D_MODEL, D_MODEL), lambda i: (0, 0)),
                  pl.BlockSpec((D_MODEL, LANES), lambda i: (0, 0))],
        out_specs=[pl.BlockSpec((tm, D_MODEL), row),
                   pl.BlockSpec((tm, H_EXT), row),
                   pl.BlockSpec((tm, N_EXPERTS), row)],
        out_shape=[jax.ShapeDtypeStruct((t, D_MODEL), F32),
                   jax.ShapeDtypeStruct((t, H_EXT), F32),
                   jax.ShapeDtypeStruct((t, N_EXPERTS), F32)],
        compiler_params=pltpu.CompilerParams(
            vmem_limit_bytes=_vmem_limit(2 * D_MODEL * D_MODEL * 2 + 24 * tm * D_MODEL * 4)),
        name="out_projection",
    )(attn, pooled, x2, mod3, g, w_out_bf, w_router)


def _route_kernel(a_ref, idx_ref, slot_ref, off_ref, *, cap):
    a = a_ref[...]
    n_e, n_c, _ = a.shape
    rows = n_e * n_c

    def bisect(it, thr):
        cand = thr | jnp.left_shift(jnp.int32(1), 30 - it)
        cand_f = lax.bitcast_convert_type(cand, F32)
        cnt = jnp.sum(jnp.sum((a >= cand_f).astype(F32), axis=2, keepdims=True),
                      axis=1, keepdims=True)
        return jnp.where(cnt >= cap, cand, thr)

    thr = lax.fori_loop(0, 31, bisect, jnp.zeros((n_e, 1, 1), I32))
    thr_f = lax.bitcast_convert_type(thr, F32)
    gt = (a > thr_f).astype(F32).reshape(rows, LANES)
    eq = (a == thr_f).astype(F32).reshape(rows, LANES)

    li = lax.broadcasted_iota(I32, (LANES, LANES), 0)
    lj = lax.broadcasted_iota(I32, (LANES, LANES), 1)
    upper_incl = (li <= lj).astype(BF16)
    ri = lax.broadcasted_iota(I32, (rows, rows), 0)
    rj = lax.broadcasted_iota(I32, (rows, rows), 1)
    same_expert = (ri // n_c) == (rj // n_c)
    before = (same_expert & (rj < ri)).astype(BF16)
    whole = same_expert.astype(BF16)

    def lane_bcast(col):
        return jnp.broadcast_to(col, (rows, LANES)).astype(BF16)

    def prefix(x):
        incl = jnp.dot(x.astype(BF16), upper_incl, preferred_element_type=F32)
        tot = incl[:, LANES - 1:LANES]
        off = jnp.dot(before, lane_bcast(tot), preferred_element_type=F32)
        return incl, tot, off

    n_gt = jnp.dot(whole, lane_bcast(jnp.sum(gt, axis=1, keepdims=True)),
                   preferred_element_type=F32)
    need = cap - n_gt
    incl_eq, _, off_eq = prefix(eq)
    rank_eq = off_eq + incl_eq - eq
    sel = jnp.where((eq > 0) & (rank_eq < need), 1.0, gt)
    incl, tot, off = prefix(sel)
    slot = off + incl - sel
    slot_ref[...] = jnp.where(sel > 0, slot, -1.0).astype(I32).reshape(n_e, n_c, LANES)
    off_ref[...] = off.astype(I32).reshape(n_e, n_c, LANES)

    s_lane = lax.broadcasted_iota(I32, (1, cap), 1).astype(F32)
    c_col = lax.broadcasted_iota(I32, (n_c, 1), 0).astype(F32)
    for e in range(n_e):
        r0 = e * n_c
        incl_e = incl[r0:r0 + n_c]
        off_e = off[r0:r0 + n_c, 0:1]
        tot_e = tot[r0:r0 + n_c]
        onehot = ((off_e <= s_lane) & (s_lane < off_e + tot_e)).astype(F32)
        counts = lax.dot_general(incl_e.astype(BF16), onehot.astype(BF16),
                                 (((0,), (0,)), ((), ())), preferred_element_type=F32)
        local = s_lane - jnp.sum(onehot * off_e, axis=0, keepdims=True)
        lane = jnp.sum((counts <= local).astype(F32), axis=0, keepdims=True)
        chunk = jnp.sum(onehot * c_col, axis=0, keepdims=True)
        idx_ref[e] = (chunk * LANES + lane).astype(I32)


def _routing(aff, cap):
    t = aff.shape[0]
    n_c = t // LANES
    a3 = aff.T.reshape(N_EXPERTS, n_c, LANES)
    return pl.pallas_call(
        functools.partial(_route_kernel, cap=cap),
        out_shape=[jax.ShapeDtypeStruct((N_EXPERTS, 1, cap), I32),
                   jax.ShapeDtypeStruct((N_EXPERTS, n_c, LANES), I32),
                   jax.ShapeDtypeStruct((N_EXPERTS, n_c, LANES), I32)],
        compiler_params=pltpu.CompilerParams(vmem_limit_bytes=_vmem_limit(48 << 20)),
        name="routing",
    )(a3)


def _ffn_kernel(idx_ref, idx_next_ref, h_hbm, wg_ref, wu_ref, wd_ref, y_ref, rows_ref, acc_ref,
                sem, *, cap, n_f, row_chunk):
    e = pl.program_id(0)
    f = pl.program_id(1)
    cur = e % 2
    nxt = 1 - cur
    per_step = cap // n_f

    def row_copy(src_row, buf, dst_row):
        return pltpu.make_async_copy(h_hbm.at[pl.ds(src_row, 1)],
                                     rows_ref.at[buf, pl.ds(dst_row, 1)], sem.at[buf])

    def wait_rows(buf):
        pltpu.make_async_copy(h_hbm.at[pl.ds(0, cap)], rows_ref.at[buf], sem.at[buf]).wait()

    @pl.when((e == 0) & (f == 0))
    def _():
        def issue(s, carry):
            row_copy(idx_ref[0, 0, s], 0, s).start()
            return carry
        lax.fori_loop(0, cap, issue, 0)
        acc_ref[...] = jnp.zeros_like(acc_ref)

    @pl.when(f == 0)
    def _():
        wait_rows(cur)

    for j in range(per_step):
        s = f * per_step + j
        row_copy(idx_next_ref[0, 0, s], nxt, s).start()

    first = f == 0
    for r in range(cap // row_chunk):
        rs = slice(r * row_chunk, (r + 1) * row_chunk)
        x = rows_ref[cur, rs, 0:D_MODEL]
        gate_act = jnp.dot(x, wg_ref[0], preferred_element_type=F32)
        up = jnp.dot(x, wu_ref[0], preferred_element_type=F32)
        hid = (gate_act * jax.nn.sigmoid(gate_act)) * up
        part = jnp.dot(hid, wd_ref[0], preferred_element_type=F32)
        acc_ref[rs, :] = part + jnp.where(first, 0.0, acc_ref[rs, :])

    @pl.when(f == n_f - 1)
    def _():
        aff = rows_ref[cur, :, D_MODEL:H_EXT]
        lane = lax.broadcasted_iota(I32, aff.shape, 1)
        gates = jnp.sum(jnp.where(lane == e, aff, 0.0), axis=1, keepdims=True)
        y_ref[0] = (acc_ref[...] * gates).astype(BF16)

    @pl.when((e == pl.num_programs(0) - 1) & (f == n_f - 1))
    def _():
        wait_rows(nxt)


def _expert_ffn(idx, h_ext, w_gate, w_up, w_down, cap):
    n_f = 4
    tf = D_EXPERT // n_f
    row_chunk = 512
    est = (2 * cap * H_EXT * 4 + cap * D_MODEL * (4 + 2 * 2) + 2 * 3 * D_MODEL * tf * 4
           + row_chunk * (3 * tf + D_MODEL) * 4)
    return pl.pallas_call(
        functools.partial(_ffn_kernel, cap=cap, n_f=n_f, row_chunk=row_chunk),
        grid=(N_EXPERTS, n_f),
        in_specs=[pl.BlockSpec((1, 1, cap), lambda e, f: (e, 0, 0), memory_space=pltpu.SMEM),
                  pl.BlockSpec((1, 1, cap), lambda e, f: ((e + 1) % N_EXPERTS, 0, 0),
                               memory_space=pltpu.SMEM),
                  pl.BlockSpec(memory_space=pl.ANY),
                  pl.BlockSpec((1, D_MODEL, tf), lambda e, f: (e, 0, f)),
                  pl.BlockSpec((1, D_MODEL, tf), lambda e, f: (e, 0, f)),
                  pl.BlockSpec((1, tf, D_MODEL), lambda e, f: (e, f, 0))],
        out_specs=pl.BlockSpec((1, cap, D_MODEL), lambda e, f: (e, 0, 0)),
        out_shape=jax.ShapeDtypeStruct((N_EXPERTS, cap, D_MODEL), BF16),
        scratch_shapes=[pltpu.VMEM((2, cap, H_EXT), F32),
                        pltpu.VMEM((cap, D_MODEL), F32),
                        pltpu.SemaphoreType.DMA((2,))],
        compiler_params=pltpu.CompilerParams(
            dimension_semantics=("arbitrary", "arbitrary"),
            vmem_limit_bytes=_vmem_limit(est + (4 << 20))),
        name="expert_ffn",
    )(idx, idx, h_ext, w_gate, w_up, w_down)


def _combine_kernel(start_ref, nch_ref, x1_ref, mod_ref, g_ref, slot_ref, y_hbm, o_ref,
                    ybuf_ref, cols_ref, ffn_ref, sem, *, kblock):
    i = pl.program_id(0)
    tm = x1_ref.shape[0]
    cur = i % 2

    def chunk_copy(e, src_row, buf, dst_row):
        return pltpu.make_async_copy(
            y_hbm.at[e, pl.ds(pl.multiple_of(src_row, COMBINE_CHUNK), COMBINE_CHUNK)],
            ybuf_ref.at[buf, pl.ds(pl.multiple_of(dst_row, COMBINE_CHUNK), COMBINE_CHUNK)],
            sem.at[buf])

    def fetch(tile, buf):
        pos = jnp.int32(0)
        for e in range(N_EXPERTS):
            start = start_ref[tile, e]
            nch = nch_ref[tile, e]

            def issue(c, carry, e=e, start=start, pos=pos):
                chunk_copy(e, start + c * COMBINE_CHUNK, buf, pos + c * COMBINE_CHUNK).start()
                return carry
            lax.fori_loop(0, nch, issue, 0)
            pos = pos + nch * COMBINE_CHUNK

    @pl.when(i == 0)
    def _():
        ybuf_ref[...] = jnp.zeros_like(ybuf_ref)
        fetch(0, 0)

    @pl.when(i + 1 < pl.num_programs(0))
    def _():
        fetch(i + 1, 1 - cur)

    pos = jnp.int32(0)
    bases = []
    for e in range(N_EXPERTS):
        bases.append(pos - start_ref[i, e])
        pos = pos + nch_ref[i, e] * COMBINE_CHUNK

    def drain(c, carry):
        chunk_copy(0, jnp.int32(0), cur, jnp.int32(0)).wait()
        return carry
    lax.fori_loop(0, pos // COMBINE_CHUNK, drain, 0)

    slots = slot_ref[...]
    for e in range(N_EXPERTS):
        s_e = slots[:, e:e + 1]
        cols_ref[e] = jnp.broadcast_to(jnp.where(s_e >= 0, s_e + bases[e], -1), (tm, LANES))

    ffn_ref[...] = jnp.zeros_like(ffn_ref)
    lane = lax.broadcasted_iota(I32, (tm, LANES), 1)

    def kstep(kb, carry):
        k0 = pl.multiple_of(kb * kblock, kblock)
        halves = []
        for h in range(kblock // LANES):
            target = lane + (k0 + h * LANES)
            hit = cols_ref[0] == target
            for e in range(1, N_EXPERTS):
                hit = hit | (cols_ref[e] == target)
            halves.append(jnp.where(hit, 1.0, 0.0).astype(BF16))
        sel = jnp.concatenate(halves, axis=1)
        ffn_ref[...] += jnp.dot(sel, ybuf_ref[cur, pl.ds(k0, kblock), :],
                                preferred_element_type=F32)
        return carry
    lax.fori_loop(0, (pos + kblock - 1) // kblock, kstep, 0)

    out = x1_ref[...] + mod_ref[0][5:6] * ffn_ref[...]
    ms = jnp.mean(out * out, axis=-1, keepdims=True)
    o_ref[...] = (out * lax.rsqrt(ms + EPS)) * g_ref[...]


def _combine(x1, mod3, mod_row, g_final, slot_t, tile_start, tile_nch, y):
    t = x1.shape[0]
    tm = TOKEN_TILE
    kblock = 256
    max_rows = N_EXPERTS * (tm + 2 * COMBINE_CHUNK)
    max_rows = -(-max_rows // kblock) * kblock
    row = lambda i, *_: (i, 0)
    grid_spec = pltpu.PrefetchScalarGridSpec(
        num_scalar_prefetch=2,
        grid=(t // tm,),
        in_specs=[pl.BlockSpec((tm, D_MODEL), row),
                  pl.BlockSpec((1, N_MOD, D_MODEL), lambda i, *_: (mod_row(i), 0, 0)),
                  pl.BlockSpec((1, D_MODEL), lambda i, *_: (0, 0)),
                  pl.BlockSpec((tm, N_EXPERTS), row),
                  pl.BlockSpec(memory_space=pl.ANY)],
        out_specs=pl.BlockSpec((tm, D_MODEL), row),
        scratch_shapes=[pltpu.VMEM((2, max_rows, D_MODEL), BF16),
                        pltpu.VMEM((N_EXPERTS, tm, LANES), I32),
                        pltpu.VMEM((tm, D_MODEL), F32),
                        pltpu.SemaphoreType.DMA((2,))],
    )
    return pl.pallas_call(
        functools.partial(_combine_kernel, kblock=kblock),
        grid_spec=grid_spec,
        out_shape=jax.ShapeDtypeStruct((t, D_MODEL), F32),
        compiler_params=pltpu.CompilerParams(
            dimension_semantics=("arbitrary",),
            vmem_limit_bytes=_vmem_limit(2 * max_rows * D_MODEL * 2 + 16 * tm * D_MODEL * 4)),
        name="combine",
    )(tile_start, tile_nch, x1, mod3, g_final, slot_t, y)


def _rope_tables(n):
    rows = n // GRID_W
    row = jnp.repeat(jnp.arange(rows, dtype=F32), GRID_W)
    col = jnp.tile(jnp.arange(GRID_W, dtype=F32), rows)
    inv = ROPE_THETA ** (-jnp.arange(ROPE_FREQS, dtype=F32) / ROPE_FREQS)
    ang_r = row[:, None] * inv
    ang_c = col[:, None] * inv
    zero = jnp.zeros_like(ang_r)
    cos = jnp.concatenate([jnp.cos(ang_r)] * 2 + [jnp.cos(ang_c)] * 2, axis=1)
    sin_a = jnp.concatenate([-jnp.sin(ang_r), zero, -jnp.sin(ang_c), zero], axis=1)
    sin_b = jnp.concatenate([zero, jnp.sin(ang_r), zero, jnp.sin(ang_c)], axis=1)
    return cos, sin_a, sin_b


def _token_group(x, mod3, mod_row, seq, weights, rope_tabs, ctx_kv):
    (norm_mix, w_in_bf, sink, pool_w, pool_scale, w_out_bf, norm_ffn, w_router,
     w_gate, w_up, w_down, norm_final) = weights
    b = x.shape[0]
    t = b * seq
    x2 = x.reshape(t, D_MODEL)
    q, k, v, p = _in_projection(x2, mod3, mod_row, norm_mix, w_in_bf, rope_tabs)
    if ctx_kv is None:
        attn = _context_attention(q, k, v, sink, seq)
    else:
        attn = _latent_attention(q, k, v, ctx_kv[0], ctx_kv[1], sink, seq)
    pooled = _pool_mixer(p, pool_w, pool_scale, seq)
    x1, h_ext, aff = _out_projection(attn, pooled, x2, mod3, mod_row, norm_ffn, w_out_bf, w_router)

    cap = EC_FACTOR * t // N_EXPERTS
    idx, slot3, off3 = _routing(aff, cap)
    y = _expert_ffn(idx, h_ext, w_gate, w_up, w_down, cap)

    chunks_per_tile = TOKEN_TILE // LANES
    tile_off = off3[:, ::chunks_per_tile, 0]
    tile_end = jnp.concatenate([tile_off[:, 1:], jnp.full((N_EXPERTS, 1), cap, I32)], axis=1)
    tile_start = (tile_off // COMBINE_CHUNK) * COMBINE_CHUNK
    tile_nch = jnp.where(tile_end > tile_off,
                         (tile_end - tile_start + COMBINE_CHUNK - 1) // COMBINE_CHUNK, 0)
    slot_t = slot3.reshape(N_EXPERTS, t).T
    out = _combine(x1, mod3, mod_row, norm_final, slot_t, tile_start.T, tile_nch.T, y)
    return out.reshape(b, seq, D_MODEL), k, v


def kernel(x_prompt, x_sample, c, cache_k, cache_v, c_ctx, w_ada, b_ada, norm_mix, w_in,
           sink_logits, pool_w, pool_scale, w_out, norm_ffn, w_router, w_gate, w_up, w_down,
           norm_final):
    n_b, seq, _ = x_prompt.shape
    n_db, n_lat, _ = x_sample.shape
    assert 1 + n_db <= MOD_ROWS and seq == TOKEN_TILE and n_lat % TOKEN_TILE == 0

    cond = jnp.concatenate(
        [c_ctx[None, :], c, jnp.zeros((MOD_ROWS - 1 - n_db, D_MODEL), F32)], axis=0)
    mod3 = _modulation(cond, w_ada[0], b_ada[0]).reshape(MOD_ROWS, N_MOD, D_MODEL)

    w_router_bf = jnp.pad(w_router[0], ((0, 0), (0, LANES - N_EXPERTS))).astype(BF16)
    weights = (norm_mix[0][None, :], w_in[0].astype(BF16), sink_logits[0], pool_w[0],
               pool_scale[0][None, :], w_out[0].astype(BF16), norm_ffn[0][None, :], w_router_bf,
               w_gate.reshape(w_gate.shape[1:]), w_up.reshape(w_up.shape[1:]),
               w_down.reshape(w_down.shape[1:]), norm_final[None, :])

    y_prompt, k_p, v_p = _token_group(
        x_prompt, mod3, lambda i: 0, seq, weights, None, None)

    tiles_per_seq = n_lat // TOKEN_TILE
    past = cache_k.shape[2]
    ck = cache_k[:, 0].reshape(n_db, past, KV_WIDTH)
    cv = cache_v[:, 0].reshape(n_db, past, KV_WIDTH)
    y_sample, _, _ = _token_group(
        x_sample, mod3, lambda i: 1 + i // tiles_per_seq, n_lat, weights,
        _rope_tables(n_lat), (ck, cv))

    state_k = k_p.reshape(n_b, 1, seq, N_KV_HEADS, HEAD_DIM)
    state_v = v_p.reshape(n_b, 1, seq, N_KV_HEADS, HEAD_DIM)
    return (y_prompt, y_sample, state_k, state_v)
```

```python
import functools

import jax
import jax.numpy as jnp
import numpy as np
from jax import lax
from jax.experimental import pallas as pl
from jax.experimental.pallas import tpu as pltpu

F32 = jnp.float32
BF16 = jnp.bfloat16
I32 = jnp.int32

D_MODEL = 2048
N_HEADS = 8
N_KV_HEADS = 2
HEAD_DIM = 128
Q_PER_KV = N_HEADS // N_KV_HEADS
ATTN_WIDTH = N_HEADS * HEAD_DIM
KV_WIDTH = N_KV_HEADS * HEAD_DIM
POOL_WIDTH = D_MODEL - ATTN_WIDTH
POOL_SIZES = (2, 4, 8, 16)
POOL_GROUP = POOL_WIDTH // len(POOL_SIZES)
IN_WIDTH = ATTN_WIDTH + 2 * KV_WIDTH + POOL_WIDTH
WINDOW = 128
BLOCK = 128
GRID_W = 64
ROPE_THETA = 10000.0
ROPE_FREQS = HEAD_DIM // 4
N_EXPERTS = 16
EC_FACTOR = 2
D_EXPERT = 1024
N_MOD = 6
EPS = 1e-6
NEG = -1e30

LANES = 128
SUBLANES = 8
BF16_ROWS = 16
VMEM_CAP = 64 * 1024 * 1024

MOD_ROWS = 8
TOKEN_TILE = 256
H_EXT = D_MODEL + LANES
POOL_HALO = 8
COMBINE_CHUNK = BF16_ROWS


def _vmem_limit(nbytes):
    return int(min(VMEM_CAP - (4 << 20), max(nbytes, 16 << 20)))


def _mod_kernel(c_ref, w_ref, b_ref, o_ref):
    c = c_ref[...]
    s = c * jax.nn.sigmoid(c)
    o_ref[...] = jnp.dot(s.astype(BF16), w_ref[...].astype(BF16),
                         preferred_element_type=F32) + b_ref[...]


def _modulation(cond, w_ada, b_ada):
    n = w_ada.shape[1]
    tn = 1024
    return pl.pallas_call(
        _mod_kernel,
        grid=(n // tn,),
        in_specs=[pl.BlockSpec((MOD_ROWS, D_MODEL), lambda j: (0, 0)),
                  pl.BlockSpec((D_MODEL, tn), lambda j: (0, j)),
                  pl.BlockSpec((1, tn), lambda j: (0, j))],
        out_specs=pl.BlockSpec((MOD_ROWS, tn), lambda j: (0, j)),
        out_shape=jax.ShapeDtypeStruct((MOD_ROWS, n), F32),
        compiler_params=pltpu.CompilerParams(
            vmem_limit_bytes=_vmem_limit(3 * D_MODEL * tn * 4)),
        name="modulation",
    )(cond, w_ada, b_ada.reshape(1, n))


def _norm_mod(x, g, shift, scale):
    ms = jnp.mean(x * x, axis=-1, keepdims=True)
    y = x * lax.rsqrt(ms + EPS)
    return (y * g) * (1.0 + scale) + shift


def _inproj_kernel(*refs, rope):
    if rope:
        x_ref, mod_ref, g_ref, w_ref, cos_ref, sa_ref, sb_ref, q_ref, k_ref, v_ref, p_ref = refs
    else:
        x_ref, mod_ref, g_ref, w_ref, q_ref, k_ref, v_ref, p_ref = refs
    mod = mod_ref[0]
    h = _norm_mod(x_ref[...], g_ref[...], mod[0:1], mod[1:2])
    u = jnp.dot(h.astype(BF16), w_ref[...], preferred_element_type=F32)

    def rot(xh):
        return (xh * cos_ref[...] + pltpu.roll(xh, LANES - ROPE_FREQS, 1) * sa_ref[...]
                + pltpu.roll(xh, ROPE_FREQS, 1) * sb_ref[...])

    for hd in range(N_HEADS):
        xh = u[:, hd * HEAD_DIM:(hd + 1) * HEAD_DIM]
        q_ref[:, hd * HEAD_DIM:(hd + 1) * HEAD_DIM] = (rot(xh) if rope else xh).astype(BF16)
    for hd in range(N_KV_HEADS):
        lo = ATTN_WIDTH + hd * HEAD_DIM
        xh = u[:, lo:lo + HEAD_DIM]
        k_ref[:, hd * HEAD_DIM:(hd + 1) * HEAD_DIM] = rot(xh) if rope else xh
    v_ref[...] = u[:, ATTN_WIDTH + KV_WIDTH:ATTN_WIDTH + 2 * KV_WIDTH]
    p_ref[...] = u[:, ATTN_WIDTH + 2 * KV_WIDTH:]


def _in_projection(x2, mod3, mod_row, g, w_in_bf, rope_tabs):
    t = x2.shape[0]
    tm = TOKEN_TILE
    rope = rope_tabs is not None
    row = lambda i: (i, 0)
    in_specs = [pl.BlockSpec((tm, D_MODEL), row),
                pl.BlockSpec((1, N_MOD, D_MODEL), lambda i: (mod_row(i), 0, 0)),
                pl.BlockSpec((1, D_MODEL), lambda i: (0, 0)),
                pl.BlockSpec((D_MODEL, IN_WIDTH), lambda i: (0, 0))]
    args = [x2, mod3, g, w_in_bf]
    if rope:
        n_seq = rope_tabs[0].shape[0]
        seq_blocks = n_seq // tm
        for tab in rope_tabs:
            in_specs.append(pl.BlockSpec((tm, HEAD_DIM), lambda i: (i % seq_blocks, 0)))
            args.append(tab)
    return pl.pallas_call(
        functools.partial(_inproj_kernel, rope=rope),
        grid=(t // tm,),
        in_specs=in_specs,
        out_specs=[pl.BlockSpec((tm, ATTN_WIDTH), row),
                   pl.BlockSpec((tm, KV_WIDTH), row),
                   pl.BlockSpec((tm, KV_WIDTH), row),
                   pl.BlockSpec((tm, POOL_WIDTH), row)],
        out_shape=[jax.ShapeDtypeStruct((t, ATTN_WIDTH), BF16),
                   jax.ShapeDtypeStruct((t, KV_WIDTH), F32),
                   jax.ShapeDtypeStruct((t, KV_WIDTH), F32),
                   jax.ShapeDtypeStruct((t, POOL_WIDTH), F32)],
        compiler_params=pltpu.CompilerParams(
            vmem_limit_bytes=_vmem_limit(2 * D_MODEL * IN_WIDTH * 2 + 24 * tm * D_MODEL * 4)),
        name="in_projection",
    )(*args)


def _softmax_pv(s_list, v_list, sink_col):
    m = sink_col
    for s in s_list:
        m = jnp.maximum(m, jnp.max(s, axis=-1, keepdims=True))
    denom = jnp.exp(sink_col - m)
    out = None
    for s, v in zip(s_list, v_list):
        e = jnp.exp(s - m)
        denom = denom + jnp.sum(e, axis=-1, keepdims=True)
        o = jnp.dot(e.astype(BF16), v, preferred_element_type=F32)
        out = o if out is None else out + o
    return out * (1.0 / denom)


def _stack_heads(q, kv):
    return jnp.concatenate(
        [q[:, (kv * Q_PER_KV + g) * HEAD_DIM:(kv * Q_PER_KV + g + 1) * HEAD_DIM]
         for g in range(Q_PER_KV)], axis=0)


def _sink_column(sink_ref, kv, rows):
    r = lax.broadcasted_iota(I32, (Q_PER_KV * rows, 1), 0)
    col = jnp.zeros((Q_PER_KV * rows, 1), F32)
    for g in range(Q_PER_KV):
        col = jnp.where((r >= g * rows) & (r < (g + 1) * rows), sink_ref[kv * Q_PER_KV + g], col)
    return col


def _qk(q, k):
    return lax.dot_general(q, k, (((1,), (1,)), ((), ())), preferred_element_type=F32)


def _ctx_attn_kernel(sink_ref, q_ref, k_ref, v_ref, o_ref):
    rows = q_ref.shape[0]
    scale = HEAD_DIM ** -0.5
    q = q_ref[...]
    for kv in range(N_KV_HEADS):
        kh = k_ref[:, kv * HEAD_DIM:(kv + 1) * HEAD_DIM].astype(BF16)
        vh = v_ref[:, kv * HEAD_DIM:(kv + 1) * HEAD_DIM].astype(BF16)
        qs = _stack_heads(q, kv)
        s = _qk(qs, kh) * scale
        o = _softmax_pv([s], [vh], _sink_column(sink_ref, kv, rows))
        for g in range(Q_PER_KV):
            hd = kv * Q_PER_KV + g
            o_ref[:, hd * HEAD_DIM:(hd + 1) * HEAD_DIM] = o[g * rows:(g + 1) * rows].astype(BF16)


def _context_attention(q, k, v, sink, seq):
    t = q.shape[0]
    row = lambda b: (b, 0)
    return pl.pallas_call(
        _ctx_attn_kernel,
        grid=(t // seq,),
        in_specs=[pl.BlockSpec(memory_space=pltpu.SMEM),
                  pl.BlockSpec((seq, ATTN_WIDTH), row),
                  pl.BlockSpec((seq, KV_WIDTH), row),
                  pl.BlockSpec((seq, KV_WIDTH), row)],
        out_specs=pl.BlockSpec((seq, ATTN_WIDTH), row),
        out_shape=jax.ShapeDtypeStruct((t, ATTN_WIDTH), BF16),
        name="context_attention",
    )(sink, q, k, v)


def _lat_attn_kernel(sink_ref, q_ref, k_ref, v_ref, ck_ref, cv_ref, o_ref, *, n_seq):
    i = pl.program_id(1)
    scale = HEAD_DIM ** -0.5
    band = 3 * BLOCK
    start = pl.multiple_of(jnp.clip((i - 1) * BLOCK, 0, n_seq - band), BLOCK)
    rows = Q_PER_KV * BLOCK
    qpos = i * BLOCK + lax.broadcasted_iota(I32, (rows, band), 0) % BLOCK
    kpos = start + lax.broadcasted_iota(I32, (rows, band), 1)
    mask = jnp.abs(kpos - qpos) <= WINDOW
    q = q_ref[...]
    for kv in range(N_KV_HEADS):
        cols = slice(kv * HEAD_DIM, (kv + 1) * HEAD_DIM)
        kb = k_ref[pl.ds(start, band), cols].astype(BF16)
        vb = v_ref[pl.ds(start, band), cols].astype(BF16)
        ck = ck_ref[0, :, cols].astype(BF16)
        cv = cv_ref[0, :, cols].astype(BF16)
        qs = _stack_heads(q, kv)
        s_loc = jnp.where(mask, _qk(qs, kb) * scale, NEG)
        s_ctx = _qk(qs, ck) * scale
        o = _softmax_pv([s_loc, s_ctx], [vb, cv], _sink_column(sink_ref, kv, BLOCK))
        for g in range(Q_PER_KV):
            hd = kv * Q_PER_KV + g
            o_ref[:, hd * HEAD_DIM:(hd + 1) * HEAD_DIM] = o[g * BLOCK:(g + 1) * BLOCK].astype(BF16)


def _latent_attention(q, k, v, ck, cv, sink, n_seq):
    t = q.shape[0]
    nb = n_seq // BLOCK
    past = ck.shape[1]
    return pl.pallas_call(
        functools.partial(_lat_attn_kernel, n_seq=n_seq),
        grid=(t // n_seq, nb),
        in_specs=[pl.BlockSpec(memory_space=pltpu.SMEM),
                  pl.BlockSpec((BLOCK, ATTN_WIDTH), lambda b, i: (b * nb + i, 0)),
                  pl.BlockSpec((n_seq, KV_WIDTH), lambda b, i: (b, 0)),
                  pl.BlockSpec((n_seq, KV_WIDTH), lambda b, i: (b, 0)),
                  pl.BlockSpec((1, past, KV_WIDTH), lambda b, i: (b, 0, 0)),
                  pl.BlockSpec((1, past, KV_WIDTH), lambda b, i: (b, 0, 0))],
        out_specs=pl.BlockSpec((BLOCK, ATTN_WIDTH), lambda b, i: (b * nb + i, 0)),
        out_shape=jax.ShapeDtypeStruct((t, ATTN_WIDTH), BF16),
        name="latent_attention",
    )(sink, q, k, v, ck, cv)


def _pool_kernel(p_ref, w_ref, s_ref, o_ref, pad_ref):
    n = p_ref.shape[0]
    rows = n + 2 * POOL_HALO
    t = lax.broadcasted_iota(I32, (n, 1), 0)
    zeros = jnp.zeros((POOL_HALO, POOL_GROUP), F32)
    pad_ref[0:POOL_HALO, :] = zeros
    pad_ref[POOL_HALO + n:rows, :] = zeros
    for g, w in enumerate(POOL_SIZES):
        cols = slice(g * POOL_GROUP, (g + 1) * POOL_GROUP)
        pg = p_ref[:, cols]
        pad_ref[POOL_HALO:POOL_HALO + n, :] = pg
        x = pad_ref[...]
        acc = x + pltpu.roll(x, 1, 0)
        step = 1
        while 2 * step < w:
            acc = pltpu.roll(acc, step, 0) + pltpu.roll(acc, rows - step, 0)
            step *= 2
        wsum = acc[POOL_HALO:POOL_HALO + n]
        lo = jnp.maximum(t - w // 2, 0)
        hi = jnp.minimum(t + w - w // 2, n)
        inv_cnt = 1.0 / (hi - lo).astype(F32)
        mixed = wsum * inv_cnt - pg
        y = jnp.dot(mixed.astype(BF16), w_ref[g].astype(BF16), preferred_element_type=F32)
        o_ref[:, cols] = (y * s_ref[:, cols]).astype(BF16)


def _pool_mixer(p, pool_w, pool_scale, seq):
    t = p.shape[0]
    row = lambda b: (b, 0)
    return pl.pallas_call(
        _pool_kernel,
        grid=(t // seq,),
        in_specs=[pl.BlockSpec((seq, POOL_WIDTH), row),
                  pl.BlockSpec((len(POOL_SIZES), POOL_GROUP, POOL_GROUP), lambda b: (0, 0, 0)),
                  pl.BlockSpec((1, POOL_WIDTH), lambda b: (0, 0))],
        out_specs=pl.BlockSpec((seq, POOL_WIDTH), row),
        out_shape=jax.ShapeDtypeStruct((t, POOL_WIDTH), BF16),
        scratch_shapes=[pltpu.VMEM((seq + 2 * POOL_HALO, POOL_GROUP), F32)],
        compiler_params=pltpu.CompilerParams(
            vmem_limit_bytes=_vmem_limit(8 * seq * POOL_WIDTH * 4)),
        name="pool_mixer",
    )(p, pool_w, pool_scale)


def _outproj_kernel(a_ref, p_ref, x_ref, mod_ref, g_ref, wo_ref, wr_ref, x1_ref, h_ref, aff_ref):
    mod = mod_ref[0]
    mix = (jnp.dot(a_ref[...], wo_ref[0:ATTN_WIDTH, :], preferred_element_type=F32)
           + jnp.dot(p_ref[...], wo_ref[ATTN_WIDTH:D_MODEL, :], preferred_element_type=F32))
    x1 = x_ref[...] + mod[2:3] * mix
    x1_ref[...] = x1
    h = _norm_mod(x1, g_ref[...], mod[3:4], mod[4:5])
    logits = jnp.dot(h.astype(BF16), wr_ref[...], preferred_element_type=F32)
    lane = lax.broadcasted_iota(I32, logits.shape, 1)
    logits = jnp.where(lane < N_EXPERTS, logits, -jnp.inf)
    m = jnp.max(logits, axis=-1, keepdims=True)
    e = jnp.exp(logits - m)
    aff = e / jnp.sum(e, axis=-1, keepdims=True)
    aff_ref[...] = aff[:, 0:N_EXPERTS]
    h_ref[:, 0:D_MODEL] = h
    h_ref[:, D_MODEL:H_EXT] = aff


def _out_projection(attn, pooled, x2, mod3, mod_row, g, w_out_bf, w_router):
    t = x2.shape[0]
    tm = TOKEN_TILE
    row = lambda i: (i, 0)
    return pl.pallas_call(
        _outproj_kernel,
        grid=(t // tm,),
        in_specs=[pl.BlockSpec((tm, ATTN_WIDTH), row),
                  pl.BlockSpec((tm, POOL_WIDTH), row),
                  pl.BlockSpec((tm, D_MODEL), row),
                  pl.BlockSpec((1, N_MOD, D_MODEL), lambda i: (mod_row(i), 0, 0)),
                  pl.BlockSpec((1, D_MODEL), lambda i: (0, 0)),
                  pl.BlockSpec((D_MODEL, D_MODEL), lambda i: (0, 0)),
                  pl.BlockSpec((D_MODEL, LANES), lambda i: (0, 0))],
        out_specs=[pl.BlockSpec((tm, D_MODEL), row),
                   pl.BlockSpec((tm, H_EXT), row),
                   pl.BlockSpec((tm, N_EXPERTS), row)],
        out_shape=[jax.ShapeDtypeStruct((t, D_MODEL), F32),
                   jax.ShapeDtypeStruct((t, H_EXT), F32),
                   jax.ShapeDtypeStruct((t, N_EXPERTS), F32)],
        compiler_params=pltpu.CompilerParams(
            vmem_limit_bytes=_vmem_limit(2 * D_MODEL * D_MODEL * 2 + 24 * tm * D_MODEL * 4)),
        name="out_projection",
    )(attn, pooled, x2, mod3, g, w_out_bf, w_router)


def _route_kernel(a_ref, idx_ref, slot_ref, off_ref, *, cap):
    a = a_ref[...]
    n_e, n_c, _ = a.shape
    rows = n_e * n_c

    def bisect(it, thr):
        cand = thr | jnp.left_shift(jnp.int32(1), 30 - it)
        cand_f = lax.bitcast_convert_type(cand, F32)
        cnt = jnp.sum(jnp.sum((a >= cand_f).astype(F32), axis=2, keepdims=True),
                      axis=1, keepdims=True)
        return jnp.where(cnt >= cap, cand, thr)

    thr = lax.fori_loop(0, 31, bisect, jnp.zeros((n_e, 1, 1), I32))
    thr_f = lax.bitcast_convert_type(thr, F32)
    gt = (a > thr_f).astype(F32).reshape(rows, LANES)
    eq = (a == thr_f).astype(F32).reshape(rows, LANES)

    li = lax.broadcasted_iota(I32, (LANES, LANES), 0)
    lj = lax.broadcasted_iota(I32, (LANES, LANES), 1)
    upper_incl = (li <= lj).astype(BF16)
    ri = lax.broadcasted_iota(I32, (rows, rows), 0)
    rj = lax.broadcasted_iota(I32, (rows, rows), 1)
    same_expert = (ri // n_c) == (rj // n_c)
    before = (same_expert & (rj < ri)).astype(BF16)
    whole = same_expert.astype(BF16)

    def lane_bcast(col):
        return jnp.broadcast_to(col, (rows, LANES)).astype(BF16)

    def prefix(x):
        incl = jnp.dot(x.astype(BF16), upper_incl, preferred_element_type=F32)
        tot = incl[:, LANES - 1:LANES]
        off = jnp.dot(before, lane_bcast(tot), preferred_element_type=F32)
        return incl, tot, off

    n_gt = jnp.dot(whole, lane_bcast(jnp.sum(gt, axis=1, keepdims=True)),
                   preferred_element_type=F32)
    need = cap - n_gt
    incl_eq, _, off_eq = prefix(eq)
    rank_eq = off_eq + incl_eq - eq
    sel = jnp.where((eq > 0) & (rank_eq < need), 1.0, gt)
    incl, tot, off = prefix(sel)
    slot = off + incl - sel
    slot_ref[...] = jnp.where(sel > 0, slot, -1.0).astype(I32).reshape(n_e, n_c, LANES)
    off_ref[...] = off.astype(I32).reshape(n_e, n_c, LANES)

    s_lane = lax.broadcasted_iota(I32, (1, cap), 1).astype(F32)
    c_col = lax.broadcasted_iota(I32, (n_c, 1), 0).astype(F32)
    for e in range(n_e):
        r0 = e * n_c
        incl_e = incl[r0:r0 + n_c]
        off_e = off[r0:r0 + n_c, 0:1]
        tot_e = tot[r0:r0 + n_c]
        onehot = ((off_e <= s_lane) & (s_lane < off_e + tot_e)).astype(F32)
        counts = lax.dot_general(incl_e.astype(BF16), onehot.astype(BF16),
                                 (((0,), (0,)), ((), ())), preferred_element_type=F32)
        local = s_lane - jnp.sum(onehot * off_e, axis=0, keepdims=True)
        lane = jnp.sum((counts <= local).astype(F32), axis=0, keepdims=True)
        chunk = jnp.sum(onehot * c_col, axis=0, keepdims=True)
        idx_ref[e] = (chunk * LANES + lane).astype(I32)


def _routing(aff, cap):
    t = aff.shape[0]
    n_c = t // LANES
    a3 = aff.T.reshape(N_EXPERTS, n_c, LANES)
    return pl.pallas_call(
        functools.partial(_route_kernel, cap=cap),
        out_shape=[jax.ShapeDtypeStruct((N_EXPERTS, 1, cap), I32),
                   jax.ShapeDtypeStruct((N_EXPERTS, n_c, LANES), I32),
                   jax.ShapeDtypeStruct((N_EXPERTS, n_c, LANES), I32)],
        compiler_params=pltpu.CompilerParams(vmem_limit_bytes=_vmem_limit(48 << 20)),
        name="routing",
    )(a3)


def _ffn_kernel(idx_ref, idx_next_ref, h_hbm, wg_ref, wu_ref, wd_ref, y_ref, rows_a, rows_b,
                acc_ref, sem, *, cap, n_f, row_chunk):
    e = pl.program_id(0)
    f = pl.program_id(1)
    per_step = cap // n_f

    def row_copy(idx, src_slot, buf, buf_sem, dst_row):
        return pltpu.make_async_copy(h_hbm.at[pl.ds(idx[0, 0, src_slot], 1)],
                                     buf.at[pl.ds(dst_row, 1)], buf_sem)

    def wait_rows(buf, buf_sem):
        pltpu.make_async_copy(h_hbm.at[pl.ds(0, cap)], buf, buf_sem).wait()

    @pl.when((e == 0) & (f == 0))
    def _():
        def issue(s, carry):
            row_copy(idx_ref, s, rows_a, sem.at[0], s).start()
            return carry
        lax.fori_loop(0, cap, issue, 0)
        acc_ref[...] = jnp.zeros_like(acc_ref)

    def step(cur, cur_sem, nxt, nxt_sem):
        @pl.when(f == 0)
        def _():
            wait_rows(cur, cur_sem)

        for j in range(per_step):
            s = f * per_step + j
            row_copy(idx_next_ref, s, nxt, nxt_sem, s).start()

        first = f == 0
        for r in range(cap // row_chunk):
            rs = slice(r * row_chunk, (r + 1) * row_chunk)
            x = cur[rs, 0:D_MODEL]
            gate_act = jnp.dot(x, wg_ref[0], preferred_element_type=F32)
            up = jnp.dot(x, wu_ref[0], preferred_element_type=F32)
            hid = (gate_act * jax.nn.sigmoid(gate_act)) * up
            part = jnp.dot(hid, wd_ref[0], preferred_element_type=F32)
            acc_ref[rs, :] = part + jnp.where(first, 0.0, acc_ref[rs, :])

        @pl.when(f == n_f - 1)
        def _():
            aff = cur[:, D_MODEL:H_EXT]
            lane = lax.broadcasted_iota(I32, aff.shape, 1)
            gates = jnp.sum(jnp.where(lane == e, aff, 0.0), axis=1, keepdims=True)
            y_ref[0] = (acc_ref[...] * gates).astype(BF16)

        @pl.when((e == pl.num_programs(0) - 1) & (f == n_f - 1))
        def _():
            wait_rows(nxt, nxt_sem)

    @pl.when(e % 2 == 0)
    def _():
        step(rows_a, sem.at[0], rows_b, sem.at[1])

    @pl.when(e % 2 == 1)
    def _():
        step(rows_b, sem.at[1], rows_a, sem.at[0])


def _expert_ffn(idx, h_ext, w_gate, w_up, w_down, cap):
    n_f = 4
    tf = D_EXPERT // n_f
    row_chunk = 512
    est = (2 * cap * H_EXT * 4 + cap * D_MODEL * (4 + 2 * 2) + 2 * 3 * D_MODEL * tf * 4
           + row_chunk * (3 * tf + D_MODEL) * 4)
    return pl.pallas_call(
        functools.partial(_ffn_kernel, cap=cap, n_f=n_f, row_chunk=row_chunk),
        grid=(N_EXPERTS, n_f),
        in_specs=[pl.BlockSpec((1, 1, cap), lambda e, f: (e, 0, 0), memory_space=pltpu.SMEM),
                  pl.BlockSpec((1, 1, cap), lambda e, f: ((e + 1) % N_EXPERTS, 0, 0),
                               memory_space=pltpu.SMEM),
                  pl.BlockSpec(memory_space=pl.ANY),
                  pl.BlockSpec((1, D_MODEL, tf), lambda e, f: (e, 0, f)),
                  pl.BlockSpec((1, D_MODEL, tf), lambda e, f: (e, 0, f)),
                  pl.BlockSpec((1, tf, D_MODEL), lambda e, f: (e, f, 0))],
        out_specs=pl.BlockSpec((1, cap, D_MODEL), lambda e, f: (e, 0, 0)),
        out_shape=jax.ShapeDtypeStruct((N_EXPERTS, cap, D_MODEL), BF16),
        scratch_shapes=[pltpu.VMEM((cap, H_EXT), F32),
                        pltpu.VMEM((cap, H_EXT), F32),
                        pltpu.VMEM((cap, D_MODEL), F32),
                        pltpu.SemaphoreType.DMA((2,))],
        compiler_params=pltpu.CompilerParams(
            dimension_semantics=("arbitrary", "arbitrary"),
            vmem_limit_bytes=_vmem_limit(est + (4 << 20))),
        name="expert_ffn",
    )(idx, idx, h_ext, w_gate, w_up, w_down)


def _combine_kernel(start_ref, nch_ref, x1_ref, mod_ref, g_ref, slot_ref, y_hbm, o_ref,
                    ybuf_ref, cols_ref, ffn_ref, sem, *, first_k, kblock):
    i = pl.program_id(0)
    tm = x1_ref.shape[0]
    cur = i % 2

    def chunk_copy(e, src_row, buf, dst_row):
        return pltpu.make_async_copy(
            y_hbm.at[e, pl.ds(pl.multiple_of(src_row, COMBINE_CHUNK), COMBINE_CHUNK)],
            ybuf_ref.at[buf, pl.ds(pl.multiple_of(dst_row, COMBINE_CHUNK), COMBINE_CHUNK)],
            sem.at[buf])

    def fetch(tile, buf):
        pos = jnp.int32(0)
        for e in range(N_EXPERTS):
            start = start_ref[tile, e]
            nch = nch_ref[tile, e]

            def issue(c, carry, e=e, start=start, pos=pos):
                chunk_copy(e, start + c * COMBINE_CHUNK, buf, pos + c * COMBINE_CHUNK).start()
                return carry
            lax.fori_loop(0, nch, issue, 0)
            pos = pos + nch * COMBINE_CHUNK

    @pl.when(i == 0)
    def _():
        ybuf_ref[...] = jnp.zeros_like(ybuf_ref)
        fetch(0, 0)

    @pl.when(i + 1 < pl.num_programs(0))
    def _():
        fetch(i + 1, 1 - cur)

    pos = jnp.int32(0)
    begins = []
    for e in range(N_EXPERTS):
        begins.append(pos)
        pos = pos + nch_ref[i, e] * COMBINE_CHUNK
    begins.append(pos)

    def drain(c, carry):
        chunk_copy(0, jnp.int32(0), cur, jnp.int32(0)).wait()
        return carry
    lax.fori_loop(0, pos // COMBINE_CHUNK, drain, 0)

    slots = slot_ref[...]
    for e in range(N_EXPERTS):
        s_e = slots[:, e:e + 1]
        col = jnp.where(s_e >= 0, s_e + (begins[e] - start_ref[i, e]), -1)
        cols_ref[e] = jnp.broadcast_to(col, (tm, LANES))

    lane = lax.broadcasted_iota(I32, (tm, LANES), 1)

    def selection(k0, width):
        halves = []
        for h in range(width // LANES):
            c0 = k0 + h * LANES
            target = lane + c0
            e_lo = jnp.int32(0)
            e_hi = jnp.int32(0)
            for e in range(N_EXPERTS):
                e_lo = e_lo + (begins[e + 1] <= c0).astype(I32)
                e_hi = e_hi + (begins[e] < c0 + LANES).astype(I32)

            def mark(e, hit, target=target):
                return jnp.where(cols_ref[e] == target, 1.0, hit)
            hit = lax.fori_loop(e_lo, e_hi, mark, jnp.zeros((tm, LANES), F32))
            halves.append(hit.astype(BF16))
        return jnp.concatenate(halves, axis=1)

    def block(k0, width):
        return jnp.dot(selection(k0, width), ybuf_ref[cur, pl.ds(k0, width), :],
                       preferred_element_type=F32)

    ffn_ref[...] = block(0, first_k)

    def kstep(kb, carry):
        ffn_ref[...] += block(pl.multiple_of(kb * kblock, kblock), kblock)
        return carry
    lax.fori_loop(first_k // kblock, (pos + kblock - 1) // kblock, kstep, 0)

    out = x1_ref[...] + mod_ref[0][5:6] * ffn_ref[...]
    ms = jnp.mean(out * out, axis=-1, keepdims=True)
    o_ref[...] = (out * lax.rsqrt(ms + EPS)) * g_ref[...]


def _combine(x1, mod3, mod_row, g_final, slot_t, tile_start, tile_nch, y):
    t = x1.shape[0]
    tm = TOKEN_TILE
    first_k = 2 * tm + 2 * N_EXPERTS * COMBINE_CHUNK
    kblock = 512
    max_rows = N_EXPERTS * (tm + 2 * COMBINE_CHUNK)
    max_rows = -(-max_rows // kblock) * kblock
    row = lambda i, *_: (i, 0)
    grid_spec = pltpu.PrefetchScalarGridSpec(
        num_scalar_prefetch=2,
        grid=(t // tm,),
        in_specs=[pl.BlockSpec((tm, D_MODEL), row),
                  pl.BlockSpec((1, N_MOD, D_MODEL), lambda i, *_: (mod_row(i), 0, 0)),
                  pl.BlockSpec((1, D_MODEL), lambda i, *_: (0, 0)),
                  pl.BlockSpec((tm, N_EXPERTS), row),
                  pl.BlockSpec(memory_space=pl.ANY)],
        out_specs=pl.BlockSpec((tm, D_MODEL), row),
        scratch_shapes=[pltpu.VMEM((2, max_rows, D_MODEL), BF16),
                        pltpu.VMEM((N_EXPERTS, tm, LANES), I32),
                        pltpu.VMEM((tm, D_MODEL), F32),
                        pltpu.SemaphoreType.DMA((2,))],
    )
    return pl.pallas_call(
        functools.partial(_combine_kernel, first_k=first_k, kblock=kblock),
        grid_spec=grid_spec,
        out_shape=jax.ShapeDtypeStruct((t, D_MODEL), F32),
        compiler_params=pltpu.CompilerParams(
            dimension_semantics=("arbitrary",),
            vmem_limit_bytes=_vmem_limit(2 * max_rows * D_MODEL * 2 + 16 * tm * D_MODEL * 4)),
        name="combine",
    )(tile_start, tile_nch, x1, mod3, g_final, slot_t, y)


def _rope_tables(n):
    rows = n // GRID_W
    row = jnp.repeat(jnp.arange(rows, dtype=F32), GRID_W)
    col = jnp.tile(jnp.arange(GRID_W, dtype=F32), rows)
    inv = ROPE_THETA ** (-jnp.arange(ROPE_FREQS, dtype=F32) / ROPE_FREQS)
    ang_r = row[:, None] * inv
    ang_c = col[:, None] * inv
    zero = jnp.zeros_like(ang_r)
    cos = jnp.concatenate([jnp.cos(ang_r)] * 2 + [jnp.cos(ang_c)] * 2, axis=1)
    sin_a = jnp.concatenate([-jnp.sin(ang_r), zero, -jnp.sin(ang_c), zero], axis=1)
    sin_b = jnp.concatenate([zero, jnp.sin(ang_r), zero, jnp.sin(ang_c)], axis=1)
    return cos, sin_a, sin_b


def _token_group(x, mod3, mod_row, seq, weights, rope_tabs, ctx_kv):
    (norm_mix, w_in_bf, sink, pool_w, pool_scale, w_out_bf, norm_ffn, w_router,
     w_gate, w_up, w_down, norm_final) = weights
    b = x.shape[0]
    t = b * seq
    x2 = x.reshape(t, D_MODEL)
    q, k, v, p = _in_projection(x2, mod3, mod_row, norm_mix, w_in_bf, rope_tabs)
    if ctx_kv is None:
        attn = _context_attention(q, k, v, sink, seq)
    else:
        attn = _latent_attention(q, k, v, ctx_kv[0], ctx_kv[1], sink, seq)
    pooled = _pool_mixer(p, pool_w, pool_scale, seq)
    x1, h_ext, aff = _out_projection(attn, pooled, x2, mod3, mod_row, norm_ffn, w_out_bf, w_router)

    cap = EC_FACTOR * t // N_EXPERTS
    idx, slot3, off3 = _routing(aff, cap)
    y = _expert_ffn(idx, h_ext, w_gate, w_up, w_down, cap)

    chunks_per_tile = TOKEN_TILE // LANES
    tile_off = off3[:, ::chunks_per_tile, 0]
    tile_end = jnp.concatenate([tile_off[:, 1:], jnp.full((N_EXPERTS, 1), cap, I32)], axis=1)
    tile_start = (tile_off // COMBINE_CHUNK) * COMBINE_CHUNK
    tile_nch = jnp.where(tile_end > tile_off,
                         (tile_end - tile_start + COMBINE_CHUNK - 1) // COMBINE_CHUNK, 0)
    slot_t = slot3.reshape(N_EXPERTS, t).T
    out = _combine(x1, mod3, mod_row, norm_final, slot_t, tile_start.T, tile_nch.T, y)
    return out.reshape(b, seq, D_MODEL), k, v


def kernel(x_prompt, x_sample, c, cache_k, cache_v, c_ctx, w_ada, b_ada, norm_mix, w_in,
           sink_logits, pool_w, pool_scale, w_out, norm_ffn, w_router, w_gate, w_up, w_down,
           norm_final):
    n_b, seq, _ = x_prompt.shape
    n_db, n_lat, _ = x_sample.shape
    assert 1 + n_db <= MOD_ROWS and seq == TOKEN_TILE and n_lat % TOKEN_TILE == 0

    cond = jnp.concatenate(
        [c_ctx[None, :], c, jnp.zeros((MOD_ROWS - 1 - n_db, D_MODEL), F32)], axis=0)
    mod3 = _modulation(cond, w_ada[0], b_ada[0]).reshape(MOD_ROWS, N_MOD, D_MODEL)

    w_router_bf = jnp.pad(w_router[0], ((0, 0), (0, LANES - N_EXPERTS))).astype(BF16)
    weights = (norm_mix[0][None, :], w_in[0].astype(BF16), sink_logits[0], pool_w[0],
               pool_scale[0][None, :], w_out[0].astype(BF16), norm_ffn[0][None, :], w_router_bf,
               w_gate.reshape(w_gate.shape[1:]), w_up.reshape(w_up.shape[1:]),
               w_down.reshape(w_down.shape[1:]), norm_final[None, :])

    y_prompt, k_p, v_p = _token_group(
        x_prompt, mod3, lambda i: 0, seq, weights, None, None)

    tiles_per_seq = n_lat // TOKEN_TILE
    past = cache_k.shape[2]
    ck = cache_k[:, 0].reshape(n_db, past, KV_WIDTH)
    cv = cache_v[:, 0].reshape(n_db, past, KV_WIDTH)
    y_sample, _, _ = _token_group(
        x_sample, mod3, lambda i: 1 + i // tiles_per_seq, n_lat, weights,
        _rope_tables(n_lat), (ck, cv))

    state_k = k_p.reshape(n_b, 1, seq, N_KV_HEADS, HEAD_DIM)
    state_v = v_p.reshape(n_b, 1, seq, N_KV_HEADS, HEAD_DIM)
    return (y_prompt, y_sample, state_k, state_v)
```

```python
import functools

import jax
import jax.numpy as jnp
import numpy as np
from jax import lax
from jax.experimental import pallas as pl
from jax.experimental.pallas import tpu as pltpu
from jax.experimental.pallas import tpu_sc as plsc

F32 = jnp.float32
BF16 = jnp.bfloat16
I32 = jnp.int32

D_MODEL = 2048
N_HEADS = 8
N_KV_HEADS = 2
HEAD_DIM = 128
Q_PER_KV = N_HEADS // N_KV_HEADS
ATTN_WIDTH = N_HEADS * HEAD_DIM
KV_WIDTH = N_KV_HEADS * HEAD_DIM
POOL_WIDTH = D_MODEL - ATTN_WIDTH
POOL_SIZES = (2, 4, 8, 16)
POOL_GROUP = POOL_WIDTH // len(POOL_SIZES)
IN_WIDTH = ATTN_WIDTH + 2 * KV_WIDTH + POOL_WIDTH
WINDOW = 128
BLOCK = 128
GRID_W = 64
ROPE_THETA = 10000.0
ROPE_FREQS = HEAD_DIM // 4
N_EXPERTS = 16
EC_FACTOR = 2
D_EXPERT = 1024
N_MOD = 6
EPS = 1e-6
NEG = -1e30

LANES = 128
SUBLANES = 8
BF16_ROWS = 16
VMEM_CAP = 64 * 1024 * 1024
SC_CORES = 2
SC_SUBCORES = 16

MOD_ROWS = 8
TOKEN_TILE = 256
H_EXT = D_MODEL + LANES
H_TILES = H_EXT // LANES
GATHER_CHUNK = 128
POOL_HALO = 8
COMBINE_CHUNK = BF16_ROWS


def _vmem_limit(nbytes):
    return int(min(VMEM_CAP - (4 << 20), max(nbytes, 16 << 20)))


def _mod_kernel(c_ref, w_ref, b_ref, o_ref):
    c = c_ref[...]
    s = c * jax.nn.sigmoid(c)
    o_ref[...] = jnp.dot(s.astype(BF16), w_ref[...].astype(BF16),
                         preferred_element_type=F32) + b_ref[...]


def _modulation(cond, w_ada, b_ada):
    n = w_ada.shape[1]
    tn = 1024
    return pl.pallas_call(
        _mod_kernel,
        grid=(n // tn,),
        in_specs=[pl.BlockSpec((MOD_ROWS, D_MODEL), lambda j: (0, 0)),
                  pl.BlockSpec((D_MODEL, tn), lambda j: (0, j)),
                  pl.BlockSpec((1, tn), lambda j: (0, j))],
        out_specs=pl.BlockSpec((MOD_ROWS, tn), lambda j: (0, j)),
        out_shape=jax.ShapeDtypeStruct((MOD_ROWS, n), F32),
        compiler_params=pltpu.CompilerParams(
            vmem_limit_bytes=_vmem_limit(3 * D_MODEL * tn * 4)),
        name="modulation",
    )(cond, w_ada, b_ada.reshape(1, n))


def _norm_mod(x, g, shift, scale):
    ms = jnp.mean(x * x, axis=-1, keepdims=True)
    y = x * lax.rsqrt(ms + EPS)
    return (y * g) * (1.0 + scale) + shift


def _inproj_kernel(*refs, rope):
    if rope:
        x_ref, mod_ref, g_ref, w_ref, cos_ref, sa_ref, sb_ref, q_ref, k_ref, v_ref, p_ref = refs
    else:
        x_ref, mod_ref, g_ref, w_ref, q_ref, k_ref, v_ref, p_ref = refs
    mod = mod_ref[0]
    h = _norm_mod(x_ref[...], g_ref[...], mod[0:1], mod[1:2])
    u = jnp.dot(h.astype(BF16), w_ref[...], preferred_element_type=F32)

    def rot(xh):
        return (xh * cos_ref[...] + pltpu.roll(xh, LANES - ROPE_FREQS, 1) * sa_ref[...]
                + pltpu.roll(xh, ROPE_FREQS, 1) * sb_ref[...])

    for hd in range(N_HEADS):
        xh = u[:, hd * HEAD_DIM:(hd + 1) * HEAD_DIM]
        q_ref[:, hd * HEAD_DIM:(hd + 1) * HEAD_DIM] = (rot(xh) if rope else xh).astype(BF16)
    for hd in range(N_KV_HEADS):
        lo = ATTN_WIDTH + hd * HEAD_DIM
        xh = u[:, lo:lo + HEAD_DIM]
        k_ref[:, hd * HEAD_DIM:(hd + 1) * HEAD_DIM] = rot(xh) if rope else xh
    v_ref[...] = u[:, ATTN_WIDTH + KV_WIDTH:ATTN_WIDTH + 2 * KV_WIDTH]
    p_ref[...] = u[:, ATTN_WIDTH + 2 * KV_WIDTH:]


def _in_projection(x2, mod3, mod_row, g, w_in_bf, rope_tabs):
    t = x2.shape[0]
    tm = TOKEN_TILE
    rope = rope_tabs is not None
    row = lambda i: (i, 0)
    in_specs = [pl.BlockSpec((tm, D_MODEL), row),
                pl.BlockSpec((1, N_MOD, D_MODEL), lambda i: (mod_row(i), 0, 0)),
                pl.BlockSpec((1, D_MODEL), lambda i: (0, 0)),
                pl.BlockSpec((D_MODEL, IN_WIDTH), lambda i: (0, 0))]
    args = [x2, mod3, g, w_in_bf]
    if rope:
        n_seq = rope_tabs[0].shape[0]
        seq_blocks = n_seq // tm
        for tab in rope_tabs:
            in_specs.append(pl.BlockSpec((tm, HEAD_DIM), lambda i: (i % seq_blocks, 0)))
            args.append(tab)
    return pl.pallas_call(
        functools.partial(_inproj_kernel, rope=rope),
        grid=(t // tm,),
        in_specs=in_specs,
        out_specs=[pl.BlockSpec((tm, ATTN_WIDTH), row),
                   pl.BlockSpec((tm, KV_WIDTH), row),
                   pl.BlockSpec((tm, KV_WIDTH), row),
                   pl.BlockSpec((tm, POOL_WIDTH), row)],
        out_shape=[jax.ShapeDtypeStruct((t, ATTN_WIDTH), BF16),
                   jax.ShapeDtypeStruct((t, KV_WIDTH), F32),
                   jax.ShapeDtypeStruct((t, KV_WIDTH), F32),
                   jax.ShapeDtypeStruct((t, POOL_WIDTH), F32)],
        compiler_params=pltpu.CompilerParams(
            vmem_limit_bytes=_vmem_limit(2 * D_MODEL * IN_WIDTH * 2 + 24 * tm * D_MODEL * 4)),
        name="in_projection",
    )(*args)


def _softmax_pv(s_list, v_list, sink_col):
    m = sink_col
    for s in s_list:
        m = jnp.maximum(m, jnp.max(s, axis=-1, keepdims=True))
    denom = jnp.exp(sink_col - m)
    out = None
    for s, v in zip(s_list, v_list):
        e = jnp.exp(s - m)
        denom = denom + jnp.sum(e, axis=-1, keepdims=True)
        o = jnp.dot(e.astype(BF16), v, preferred_element_type=F32)
        out = o if out is None else out + o
    return out * (1.0 / denom)


def _stack_heads(q, kv):
    return jnp.concatenate(
        [q[:, (kv * Q_PER_KV + g) * HEAD_DIM:(kv * Q_PER_KV + g + 1) * HEAD_DIM]
         for g in range(Q_PER_KV)], axis=0)


def _sink_column(sink_ref, kv, rows):
    r = lax.broadcasted_iota(I32, (Q_PER_KV * rows, 1), 0)
    col = jnp.zeros((Q_PER_KV * rows, 1), F32)
    for g in range(Q_PER_KV):
        col = jnp.where((r >= g * rows) & (r < (g + 1) * rows), sink_ref[kv * Q_PER_KV + g], col)
    return col


def _qk(q, k):
    return lax.dot_general(q, k, (((1,), (1,)), ((), ())), preferred_element_type=F32)


def _ctx_attn_kernel(sink_ref, q_ref, k_ref, v_ref, o_ref):
    rows = q_ref.shape[0]
    scale = HEAD_DIM ** -0.5
    q = q_ref[...]
    for kv in range(N_KV_HEADS):
        kh = k_ref[:, kv * HEAD_DIM:(kv + 1) * HEAD_DIM].astype(BF16)
        vh = v_ref[:, kv * HEAD_DIM:(kv + 1) * HEAD_DIM].astype(BF16)
        qs = _stack_heads(q, kv)
        s = _qk(qs, kh) * scale
        o = _softmax_pv([s], [vh], _sink_column(sink_ref, kv, rows))
        for g in range(Q_PER_KV):
            hd = kv * Q_PER_KV + g
            o_ref[:, hd * HEAD_DIM:(hd + 1) * HEAD_DIM] = o[g * rows:(g + 1) * rows].astype(BF16)


def _context_attention(q, k, v, sink, seq):
    t = q.shape[0]
    row = lambda b: (b, 0)
    return pl.pallas_call(
        _ctx_attn_kernel,
        grid=(t // seq,),
        in_specs=[pl.BlockSpec(memory_space=pltpu.SMEM),
                  pl.BlockSpec((seq, ATTN_WIDTH), row),
                  pl.BlockSpec((seq, KV_WIDTH), row),
                  pl.BlockSpec((seq, KV_WIDTH), row)],
        out_specs=pl.BlockSpec((seq, ATTN_WIDTH), row),
        out_shape=jax.ShapeDtypeStruct((t, ATTN_WIDTH), BF16),
        name="context_attention",
    )(sink, q, k, v)


def _lat_attn_kernel(sink_ref, q_ref, k_ref, v_ref, ck_ref, cv_ref, o_ref, *, n_seq):
    i = pl.program_id(1)
    scale = HEAD_DIM ** -0.5
    band = 3 * BLOCK
    start = pl.multiple_of(jnp.clip((i - 1) * BLOCK, 0, n_seq - band), BLOCK)
    rows = Q_PER_KV * BLOCK
    qpos = i * BLOCK + lax.broadcasted_iota(I32, (rows, band), 0) % BLOCK
    kpos = start + lax.broadcasted_iota(I32, (rows, band), 1)
    mask = jnp.abs(kpos - qpos) <= WINDOW
    q = q_ref[...]
    for kv in range(N_KV_HEADS):
        cols = slice(kv * HEAD_DIM, (kv + 1) * HEAD_DIM)
        kb = k_ref[pl.ds(start, band), cols].astype(BF16)
        vb = v_ref[pl.ds(start, band), cols].astype(BF16)
        ck = ck_ref[0, :, cols].astype(BF16)
        cv = cv_ref[0, :, cols].astype(BF16)
        qs = _stack_heads(q, kv)
        s_loc = jnp.where(mask, _qk(qs, kb) * scale, NEG)
        s_ctx = _qk(qs, ck) * scale
        o = _softmax_pv([s_loc, s_ctx], [vb, cv], _sink_column(sink_ref, kv, BLOCK))
        for g in range(Q_PER_KV):
            hd = kv * Q_PER_KV + g
            o_ref[:, hd * HEAD_DIM:(hd + 1) * HEAD_DIM] = o[g * BLOCK:(g + 1) * BLOCK].astype(BF16)


def _latent_attention(q, k, v, ck, cv, sink, n_seq):
    t = q.shape[0]
    nb = n_seq // BLOCK
    past = ck.shape[1]
    return pl.pallas_call(
        functools.partial(_lat_attn_kernel, n_seq=n_seq),
        grid=(t // n_seq, nb),
        in_specs=[pl.BlockSpec(memory_space=pltpu.SMEM),
                  pl.BlockSpec((BLOCK, ATTN_WIDTH), lambda b, i: (b * nb + i, 0)),
                  pl.BlockSpec((n_seq, KV_WIDTH), lambda b, i: (b, 0)),
                  pl.BlockSpec((n_seq, KV_WIDTH), lambda b, i: (b, 0)),
                  pl.BlockSpec((1, past, KV_WIDTH), lambda b, i: (b, 0, 0)),
                  pl.BlockSpec((1, past, KV_WIDTH), lambda b, i: (b, 0, 0))],
        out_specs=pl.BlockSpec((BLOCK, ATTN_WIDTH), lambda b, i: (b * nb + i, 0)),
        out_shape=jax.ShapeDtypeStruct((t, ATTN_WIDTH), BF16),
        name="latent_attention",
    )(sink, q, k, v, ck, cv)


def _pool_kernel(p_ref, w_ref, s_ref, o_ref, pad_ref):
    n = p_ref.shape[0]
    rows = n + 2 * POOL_HALO
    t = lax.broadcasted_iota(I32, (n, 1), 0)
    zeros = jnp.zeros((POOL_HALO, POOL_GROUP), F32)
    pad_ref[0:POOL_HALO, :] = zeros
    pad_ref[POOL_HALO + n:rows, :] = zeros
    for g, w in enumerate(POOL_SIZES):
        cols = slice(g * POOL_GROUP, (g + 1) * POOL_GROUP)
        pg = p_ref[:, cols]
        pad_ref[POOL_HALO:POOL_HALO + n, :] = pg
        x = pad_ref[...]
        acc = x + pltpu.roll(x, 1, 0)
        step = 1
        while 2 * step < w:
            acc = pltpu.roll(acc, step, 0) + pltpu.roll(acc, rows - step, 0)
            step *= 2
        wsum = acc[POOL_HALO:POOL_HALO + n]
        lo = jnp.maximum(t - w // 2, 0)
        hi = jnp.minimum(t + w - w // 2, n)
        inv_cnt = 1.0 / (hi - lo).astype(F32)
        mixed = wsum * inv_cnt - pg
        y = jnp.dot(mixed.astype(BF16), w_ref[g].astype(BF16), preferred_element_type=F32)
        o_ref[:, cols] = (y * s_ref[:, cols]).astype(BF16)


def _pool_mixer(p, pool_w, pool_scale, seq):
    t = p.shape[0]
    row = lambda b: (b, 0)
    return pl.pallas_call(
        _pool_kernel,
        grid=(t // seq,),
        in_specs=[pl.BlockSpec((seq, POOL_WIDTH), row),
                  pl.BlockSpec((len(POOL_SIZES), POOL_GROUP, POOL_GROUP), lambda b: (0, 0, 0)),
                  pl.BlockSpec((1, POOL_WIDTH), lambda b: (0, 0))],
        out_specs=pl.BlockSpec((seq, POOL_WIDTH), row),
        out_shape=jax.ShapeDtypeStruct((t, POOL_WIDTH), BF16),
        scratch_shapes=[pltpu.VMEM((seq + 2 * POOL_HALO, POOL_GROUP), F32)],
        compiler_params=pltpu.CompilerParams(
            vmem_limit_bytes=_vmem_limit(8 * seq * POOL_WIDTH * 4)),
        name="pool_mixer",
    )(p, pool_w, pool_scale)


def _outproj_kernel(a_ref, p_ref, x_ref, mod_ref, g_ref, wo_ref, wr_ref, x1_ref, h_ref, aff_ref):
    mod = mod_ref[0]
    mix = (jnp.dot(a_ref[...], wo_ref[0:ATTN_WIDTH, :], preferred_element_type=F32)
           + jnp.dot(p_ref[...], wo_ref[ATTN_WIDTH:D_MODEL, :], preferred_element_type=F32))
    x1 = x_ref[...] + mod[2:3] * mix
    x1_ref[...] = x1
    h = _norm_mod(x1, g_ref[...], mod[3:4], mod[4:5])
    logits = jnp.dot(h.astype(BF16), wr_ref[...], preferred_element_type=F32)
    lane = lax.broadcasted_iota(I32, logits.shape, 1)
    logits = jnp.where(lane < N_EXPERTS, logits, -jnp.inf)
    m = jnp.max(logits, axis=-1, keepdims=True)
    e = jnp.exp(logits - m)
    aff = e / jnp.sum(e, axis=-1, keepdims=True)
    aff_ref[...] = aff[:, 0:N_EXPERTS]
    groups = h.shape[0] // SUBLANES
    for c in range(D_MODEL // LANES):
        h_ref[:, c * SUBLANES:(c + 1) * SUBLANES, :] = (
            h[:, c * LANES:(c + 1) * LANES].reshape(groups, SUBLANES, LANES))
    h_ref[:, D_MODEL // LANES * SUBLANES:, :] = aff.reshape(groups, SUBLANES, LANES)


def _out_projection(attn, pooled, x2, mod3, mod_row, g, w_out_bf, w_router):
    t = x2.shape[0]
    tm = TOKEN_TILE
    row = lambda i: (i, 0)
    return pl.pallas_call(
        _outproj_kernel,
        grid=(t // tm,),
        in_specs=[pl.BlockSpec((tm, ATTN_WIDTH), row),
                  pl.BlockSpec((tm, POOL_WIDTH), row),
                  pl.BlockSpec((tm, D_MODEL), row),
                  pl.BlockSpec((1, N_MOD, D_MODEL), lambda i: (mod_row(i), 0, 0)),
                  pl.BlockSpec((1, D_MODEL), lambda i: (0, 0)),
                  pl.BlockSpec((D_MODEL, D_MODEL), lambda i: (0, 0)),
                  pl.BlockSpec((D_MODEL, LANES), lambda i: (0, 0))],
        out_specs=[pl.BlockSpec((tm, D_MODEL), row),
                   pl.BlockSpec((tm // SUBLANES, H_TILES * SUBLANES, LANES), lambda i: (i, 0, 0)),
                   pl.BlockSpec((tm, N_EXPERTS), row)],
        out_shape=[jax.ShapeDtypeStruct((t, D_MODEL), F32),
                   jax.ShapeDtypeStruct((t // SUBLANES, H_TILES * SUBLANES, LANES), F32),
                   jax.ShapeDtypeStruct((t, N_EXPERTS), F32)],
        compiler_params=pltpu.CompilerParams(
            vmem_limit_bytes=_vmem_limit(2 * D_MODEL * D_MODEL * 2 + 24 * tm * D_MODEL * 4)),
        name="out_projection",
    )(attn, pooled, x2, mod3, g, w_out_bf, w_router)


def _route_kernel(a_ref, idx_ref, slot_ref, off_ref, *, cap):
    a = a_ref[...]
    n_e, n_c, _ = a.shape
    rows = n_e * n_c

    def bisect(it, thr):
        cand = thr | jnp.left_shift(jnp.int32(1), 30 - it)
        cand_f = lax.bitcast_convert_type(cand, F32)
        cnt = jnp.sum(jnp.sum((a >= cand_f).astype(F32), axis=2, keepdims=True),
                      axis=1, keepdims=True)
        return jnp.where(cnt >= cap, cand, thr)

    thr = lax.fori_loop(0, 31, bisect, jnp.zeros((n_e, 1, 1), I32))
    thr_f = lax.bitcast_convert_type(thr, F32)
    gt = (a > thr_f).astype(F32).reshape(rows, LANES)
    eq = (a == thr_f).astype(F32).reshape(rows, LANES)

    li = lax.broadcasted_iota(I32, (LANES, LANES), 0)
    lj = lax.broadcasted_iota(I32, (LANES, LANES), 1)
    upper_incl = (li <= lj).astype(BF16)
    ri = lax.broadcasted_iota(I32, (rows, rows), 0)
    rj = lax.broadcasted_iota(I32, (rows, rows), 1)
    same_expert = (ri // n_c) == (rj // n_c)
    before = (same_expert & (rj < ri)).astype(BF16)
    whole = same_expert.astype(BF16)

    def lane_bcast(col):
        return jnp.broadcast_to(col, (rows, LANES)).astype(BF16)

    def prefix(x):
        incl = jnp.dot(x.astype(BF16), upper_incl, preferred_element_type=F32)
        tot = incl[:, LANES - 1:LANES]
        off = jnp.dot(before, lane_bcast(tot), preferred_element_type=F32)
        return incl, tot, off

    n_gt = jnp.dot(whole, lane_bcast(jnp.sum(gt, axis=1, keepdims=True)),
                   preferred_element_type=F32)
    need = cap - n_gt
    incl_eq, _, off_eq = prefix(eq)
    rank_eq = off_eq + incl_eq - eq
    sel = jnp.where((eq > 0) & (rank_eq < need), 1.0, gt)
    incl, tot, off = prefix(sel)
    slot = off + incl - sel
    slot_ref[...] = jnp.where(sel > 0, slot, -1.0).astype(I32).reshape(n_e, n_c, LANES)
    off_ref[...] = off.astype(I32).reshape(n_e, n_c, LANES)

    s_lane = lax.broadcasted_iota(I32, (1, cap), 1).astype(F32)
    c_col = lax.broadcasted_iota(I32, (n_c, 1), 0).astype(F32)
    for e in range(n_e):
        r0 = e * n_c
        incl_e = incl[r0:r0 + n_c]
        off_e = off[r0:r0 + n_c, 0:1]
        tot_e = tot[r0:r0 + n_c]
        onehot = ((off_e <= s_lane) & (s_lane < off_e + tot_e)).astype(F32)
        counts = lax.dot_general(incl_e.astype(BF16), onehot.astype(BF16),
                                 (((0,), (0,)), ((), ())), preferred_element_type=F32)
        local = s_lane - jnp.sum(onehot * off_e, axis=0, keepdims=True)
        lane = jnp.sum((counts <= local).astype(F32), axis=0, keepdims=True)
        chunk = jnp.sum(onehot * c_col, axis=0, keepdims=True)
        idx_ref[e] = (chunk * LANES + lane).astype(I32)


def _routing(aff, cap):
    t = aff.shape[0]
    n_c = t // LANES
    a3 = aff.T.reshape(N_EXPERTS, n_c, LANES)
    return pl.pallas_call(
        functools.partial(_route_kernel, cap=cap),
        out_shape=[jax.ShapeDtypeStruct((N_EXPERTS, 1, cap), I32),
                   jax.ShapeDtypeStruct((N_EXPERTS, n_c, LANES), I32),
                   jax.ShapeDtypeStruct((N_EXPERTS, n_c, LANES), I32)],
        compiler_params=pltpu.CompilerParams(vmem_limit_bytes=_vmem_limit(48 << 20)),
        name="routing",
    )(a3)


def _gather_rows(table, row_ids):
    n_rows = row_ids.shape[0]
    n_workers = SC_CORES * SC_SUBCORES
    per_worker = n_rows // n_workers
    chunk = GATHER_CHUNK
    assert n_rows % n_workers == 0 and per_worker % chunk == 0
    mesh = plsc.VectorSubcoreMesh(core_axis_name="core", subcore_axis_name="subcore")

    @functools.partial(
        pl.kernel, mesh=mesh,
        out_type=jax.ShapeDtypeStruct((n_rows, LANES), F32),
        scratch_types=[pltpu.VMEM((chunk,), I32),
                       pltpu.VMEM((chunk, LANES), F32),
                       pltpu.SemaphoreType.DMA],
        name="gather_rows",
    )
    def gather(table_hbm, ids_hbm, out_hbm, ids_v, rows_v, sem):
        worker = lax.axis_index("subcore") * SC_CORES + lax.axis_index("core")
        base = worker * per_worker

        @pl.loop(0, per_worker // chunk)
        def _(j):
            off = pl.multiple_of(base + j * chunk, chunk)
            pltpu.sync_copy(ids_hbm.at[pl.ds(off, chunk)], ids_v)
            pltpu.async_copy(table_hbm.at[ids_v], rows_v, sem).wait()
            pltpu.sync_copy(rows_v, out_hbm.at[pl.ds(off, chunk)])

    return gather(table, row_ids)


def _tiled_row_ids(tokens):
    tok = tokens.reshape(-1, 1, SUBLANES)
    tile = jnp.arange(H_TILES, dtype=I32).reshape(1, H_TILES, 1)
    ids = ((tok // SUBLANES) * H_TILES + tile) * SUBLANES + tok % SUBLANES
    return ids.reshape(-1)


def _ffn_kernel(x_ref, wg_ref, wu_ref, wd_ref, y_ref, acc_ref, *, n_f, row_chunk):
    e = pl.program_id(0)
    f = pl.program_id(1)
    cap = y_ref.shape[1]
    groups = row_chunk // SUBLANES

    def tile_cols(gs, c):
        return x_ref[gs, c * SUBLANES:(c + 1) * SUBLANES, :].reshape(-1, LANES)

    @pl.when((e == 0) & (f == 0))
    def _():
        acc_ref[...] = jnp.zeros_like(acc_ref)

    first = f == 0
    for r in range(cap // row_chunk):
        gs = slice(r * groups, (r + 1) * groups)
        rs = slice(r * row_chunk, (r + 1) * row_chunk)
        x = jnp.concatenate([tile_cols(gs, c) for c in range(D_MODEL // LANES)], axis=1)
        gate_act = jnp.dot(x, wg_ref[0], preferred_element_type=F32)
        up = jnp.dot(x, wu_ref[0], preferred_element_type=F32)
        hid = (gate_act * jax.nn.sigmoid(gate_act)) * up
        part = jnp.dot(hid, wd_ref[0], preferred_element_type=F32)
        acc_ref[rs, :] = part + jnp.where(first, 0.0, acc_ref[rs, :])

    @pl.when(f == n_f - 1)
    def _():
        aff = tile_cols(slice(None), D_MODEL // LANES)
        lane = lax.broadcasted_iota(I32, aff.shape, 1)
        gates = jnp.sum(jnp.where(lane == e, aff, 0.0), axis=1, keepdims=True)
        y_ref[0] = (acc_ref[...] * gates).astype(BF16)


def _expert_ffn(idx, h_tiled, w_gate, w_up, w_down, cap):
    n_f = 4
    tf = D_EXPERT // n_f
    row_chunk = 512
    group_rows = H_TILES * SUBLANES
    table = h_tiled.reshape(-1, LANES)
    xs = _gather_rows(table, _tiled_row_ids(idx.reshape(-1)))
    xs = xs.reshape(N_EXPERTS * cap // SUBLANES, group_rows, LANES)
    est = (2 * cap * H_EXT * 4 + cap * D_MODEL * (4 + 2 * 2) + 2 * 3 * D_MODEL * tf * 4
           + row_chunk * (3 * tf + 2 * D_MODEL) * 4)
    return pl.pallas_call(
        functools.partial(_ffn_kernel, n_f=n_f, row_chunk=row_chunk),
        grid=(N_EXPERTS, n_f),
        in_specs=[pl.BlockSpec((cap // SUBLANES, group_rows, LANES), lambda e, f: (e, 0, 0)),
                  pl.BlockSpec((1, D_MODEL, tf), lambda e, f: (e, 0, f)),
                  pl.BlockSpec((1, D_MODEL, tf), lambda e, f: (e, 0, f)),
                  pl.BlockSpec((1, tf, D_MODEL), lambda e, f: (e, f, 0))],
        out_specs=pl.BlockSpec((1, cap, D_MODEL), lambda e, f: (e, 0, 0)),
        out_shape=jax.ShapeDtypeStruct((N_EXPERTS, cap, D_MODEL), BF16),
        scratch_shapes=[pltpu.VMEM((cap, D_MODEL), F32)],
        compiler_params=pltpu.CompilerParams(
            dimension_semantics=("arbitrary", "arbitrary"),
            vmem_limit_bytes=_vmem_limit(est + (4 << 20))),
        name="expert_ffn",
    )(xs, w_gate, w_up, w_down)


def _combine_kernel(start_ref, nch_ref, x1_ref, mod_ref, g_ref, slot_ref, y_hbm, o_ref,
                    ybuf_ref, cols_ref, ffn_ref, sem, *, first_k, kblock):
    i = pl.program_id(0)
    tm = x1_ref.shape[0]
    cur = i % 2

    def chunk_copy(e, src_row, buf, dst_row):
        return pltpu.make_async_copy(
            y_hbm.at[e, pl.ds(pl.multiple_of(src_row, COMBINE_CHUNK), COMBINE_CHUNK)],
            ybuf_ref.at[buf, pl.ds(pl.multiple_of(dst_row, COMBINE_CHUNK), COMBINE_CHUNK)],
            sem.at[buf])

    def fetch(tile, buf):
        pos = jnp.int32(0)
        for e in range(N_EXPERTS):
            start = start_ref[tile, e]
            nch = nch_ref[tile, e]

            def issue(c, carry, e=e, start=start, pos=pos):
                chunk_copy(e, start + c * COMBINE_CHUNK, buf, pos + c * COMBINE_CHUNK).start()
                return carry
            lax.fori_loop(0, nch, issue, 0)
            pos = pos + nch * COMBINE_CHUNK

    @pl.when(i == 0)
    def _():
        ybuf_ref[...] = jnp.zeros_like(ybuf_ref)
        fetch(0, 0)

    @pl.when(i + 1 < pl.num_programs(0))
    def _():
        fetch(i + 1, 1 - cur)

    pos = jnp.int32(0)
    begins = []
    for e in range(N_EXPERTS):
        begins.append(pos)
        pos = pos + nch_ref[i, e] * COMBINE_CHUNK
    begins.append(pos)

    def drain(c, carry):
        chunk_copy(0, jnp.int32(0), cur, jnp.int32(0)).wait()
        return carry
    lax.fori_loop(0, pos // COMBINE_CHUNK, drain, 0)

    slots = slot_ref[...]
    for e in range(N_EXPERTS):
        s_e = slots[:, e:e + 1]
        col = jnp.where(s_e >= 0, s_e + (begins[e] - start_ref[i, e]), -1)
        cols_ref[e] = jnp.broadcast_to(col, (tm, LANES))

    lane = lax.broadcasted_iota(I32, (tm, LANES), 1)

    def selection(k0, width):
        halves = []
        for h in range(width // LANES):
            c0 = k0 + h * LANES
            target = lane + c0
            e_lo = jnp.int32(0)
            e_hi = jnp.int32(0)
            for e in range(N_EXPERTS):
                e_lo = e_lo + (begins[e + 1] <= c0).astype(I32)
                e_hi = e_hi + (begins[e] < c0 + LANES).astype(I32)

            def mark(e, hit, target=target):
                return jnp.where(cols_ref[e] == target, 1.0, hit)
            hit = lax.fori_loop(e_lo, e_hi, mark, jnp.zeros((tm, LANES), F32))
            halves.append(hit.astype(BF16))
        return jnp.concatenate(halves, axis=1)

    def block(k0, width):
        return jnp.dot(selection(k0, width), ybuf_ref[cur, pl.ds(k0, width), :],
                       preferred_element_type=F32)

    ffn_ref[...] = block(0, first_k)

    def kstep(kb, carry):
        ffn_ref[...] += block(pl.multiple_of(kb * kblock, kblock), kblock)
        return carry
    lax.fori_loop(first_k // kblock, (pos + kblock - 1) // kblock, kstep, 0)

    out = x1_ref[...] + mod_ref[0][5:6] * ffn_ref[...]
    ms = jnp.mean(out * out, axis=-1, keepdims=True)
    o_ref[...] = (out * lax.rsqrt(ms + EPS)) * g_ref[...]


def _combine(x1, mod3, mod_row, g_final, slot_t, tile_start, tile_nch, y):
    t = x1.shape[0]
    tm = TOKEN_TILE
    first_k = 2 * tm + 2 * N_EXPERTS * COMBINE_CHUNK
    kblock = 512
    max_rows = N_EXPERTS * (tm + 2 * COMBINE_CHUNK)
    max_rows = -(-max_rows // kblock) * kblock
    row = lambda i, *_: (i, 0)
    grid_spec = pltpu.PrefetchScalarGridSpec(
        num_scalar_prefetch=2,
        grid=(t // tm,),
        in_specs=[pl.BlockSpec((tm, D_MODEL), row),
                  pl.BlockSpec((1, N_MOD, D_MODEL), lambda i, *_: (mod_row(i), 0, 0)),
                  pl.BlockSpec((1, D_MODEL), lambda i, *_: (0, 0)),
                  pl.BlockSpec((tm, N_EXPERTS), row),
                  pl.BlockSpec(memory_space=pl.ANY)],
        out_specs=pl.BlockSpec((tm, D_MODEL), row),
        scratch_shapes=[pltpu.VMEM((2, max_rows, D_MODEL), BF16),
                        pltpu.VMEM((N_EXPERTS, tm, LANES), I32),
                        pltpu.VMEM((tm, D_MODEL), F32),
                        pltpu.SemaphoreType.DMA((2,))],
    )
    return pl.pallas_call(
        functools.partial(_combine_kernel, first_k=first_k, kblock=kblock),
        grid_spec=grid_spec,
        out_shape=jax.ShapeDtypeStruct((t, D_MODEL), F32),
        compiler_params=pltpu.CompilerParams(
            dimension_semantics=("arbitrary",),
            vmem_limit_bytes=_vmem_limit(2 * max_rows * D_MODEL * 2 + 16 * tm * D_MODEL * 4)),
        name="combine",
    )(tile_start, tile_nch, x1, mod3, g_final, slot_t, y)


def _rope_tables(n):
    rows = n // GRID_W
    row = jnp.repeat(jnp.arange(rows, dtype=F32), GRID_W)
    col = jnp.tile(jnp.arange(GRID_W, dtype=F32), rows)
    inv = ROPE_THETA ** (-jnp.arange(ROPE_FREQS, dtype=F32) / ROPE_FREQS)
    ang_r = row[:, None] * inv
    ang_c = col[:, None] * inv
    zero = jnp.zeros_like(ang_r)
    cos = jnp.concatenate([jnp.cos(ang_r)] * 2 + [jnp.cos(ang_c)] * 2, axis=1)
    sin_a = jnp.concatenate([-jnp.sin(ang_r), zero, -jnp.sin(ang_c), zero], axis=1)
    sin_b = jnp.concatenate([zero, jnp.sin(ang_r), zero, jnp.sin(ang_c)], axis=1)
    return cos, sin_a, sin_b


def _token_group(x, mod3, mod_row, seq, weights, rope_tabs, ctx_kv):
    (norm_mix, w_in_bf, sink, pool_w, pool_scale, w_out_bf, norm_ffn, w_router,
     w_gate, w_up, w_down, norm_final) = weights
    b = x.shape[0]
    t = b * seq
    x2 = x.reshape(t, D_MODEL)
    q, k, v, p = _in_projection(x2, mod3, mod_row, norm_mix, w_in_bf, rope_tabs)
    if ctx_kv is None:
        attn = _context_attention(q, k, v, sink, seq)
    else:
        attn = _latent_attention(q, k, v, ctx_kv[0], ctx_kv[1], sink, seq)
    pooled = _pool_mixer(p, pool_w, pool_scale, seq)
    x1, h_ext, aff = _out_projection(attn, pooled, x2, mod3, mod_row, norm_ffn, w_out_bf, w_router)

    cap = EC_FACTOR * t // N_EXPERTS
    idx, slot3, off3 = _routing(aff, cap)
    y = _expert_ffn(idx, h_ext, w_gate, w_up, w_down, cap)

    chunks_per_tile = TOKEN_TILE // LANES
    tile_off = off3[:, ::chunks_per_tile, 0]
    tile_end = jnp.concatenate([tile_off[:, 1:], jnp.full((N_EXPERTS, 1), cap, I32)], axis=1)
    tile_start = (tile_off // COMBINE_CHUNK) * COMBINE_CHUNK
    tile_nch = jnp.where(tile_end > tile_off,
                         (tile_end - tile_start + COMBINE_CHUNK - 1) // COMBINE_CHUNK, 0)
    slot_t = slot3.reshape(N_EXPERTS, t).T
    out = _combine(x1, mod3, mod_row, norm_final, slot_t, tile_start.T, tile_nch.T, y)
    return out.reshape(b, seq, D_MODEL), k, v


def kernel(x_prompt, x_sample, c, cache_k, cache_v, c_ctx, w_ada, b_ada, norm_mix, w_in,
           sink_logits, pool_w, pool_scale, w_out, norm_ffn, w_router, w_gate, w_up, w_down,
           norm_final):
    n_b, seq, _ = x_prompt.shape
    n_db, n_lat, _ = x_sample.shape
    assert 1 + n_db <= MOD_ROWS and seq == TOKEN_TILE and n_lat % TOKEN_TILE == 0

    cond = jnp.concatenate(
        [c_ctx[None, :], c, jnp.zeros((MOD_ROWS - 1 - n_db, D_MODEL), F32)], axis=0)
    mod3 = _modulation(cond, w_ada[0], b_ada[0]).reshape(MOD_ROWS, N_MOD, D_MODEL)

    w_router_bf = jnp.pad(w_router[0], ((0, 0), (0, LANES - N_EXPERTS))).astype(BF16)
    weights = (norm_mix[0][None, :], w_in[0].astype(BF16), sink_logits[0], pool_w[0],
               pool_scale[0][None, :], w_out[0].astype(BF16), norm_ffn[0][None, :], w_router_bf,
               w_gate.reshape(w_gate.shape[1:]), w_up.reshape(w_up.shape[1:]),
               w_down.reshape(w_down.shape[1:]), norm_final[None, :])

    y_prompt, k_p, v_p = _token_group(
        x_prompt, mod3, lambda i: 0, seq, weights, None, None)

    tiles_per_seq = n_lat // TOKEN_TILE
    past = cache_k.shape[2]
    ck = cache_k[:, 0].reshape(n_db, past, KV_WIDTH)
    cv = cache_v[:, 0].reshape(n_db, past, KV_WIDTH)
    y_sample, _, _ = _token_group(
        x_sample, mod3, lambda i: 1 + i // tiles_per_seq, n_lat, weights,
        _rope_tables(n_lat), (ck, cv))

    state_k = k_p.reshape(n_b, 1, seq, N_KV_HEADS, HEAD_DIM)
    state_v = v_p.reshape(n_b, 1, seq, N_KV_HEADS, HEAD_DIM)
    return (y_prompt, y_sample, state_k, state_v)
```

```python
import functools

import jax
import jax.numpy as jnp
import numpy as np
from jax import lax
from jax.experimental import pallas as pl
from jax.experimental.pallas import tpu as pltpu
from jax.experimental.pallas import tpu_sc as plsc

F32 = jnp.float32
BF16 = jnp.bfloat16
I32 = jnp.int32

D_MODEL = 2048
N_HEADS = 8
N_KV_HEADS = 2
HEAD_DIM = 128
Q_PER_KV = N_HEADS // N_KV_HEADS
ATTN_WIDTH = N_HEADS * HEAD_DIM
KV_WIDTH = N_KV_HEADS * HEAD_DIM
POOL_WIDTH = D_MODEL - ATTN_WIDTH
POOL_SIZES = (2, 4, 8, 16)
POOL_GROUP = POOL_WIDTH // len(POOL_SIZES)
IN_WIDTH = ATTN_WIDTH + 2 * KV_WIDTH + POOL_WIDTH
WINDOW = 128
BLOCK = 128
GRID_W = 64
ROPE_THETA = 10000.0
ROPE_FREQS = HEAD_DIM // 4
N_EXPERTS = 16
EC_FACTOR = 2
D_EXPERT = 1024
N_MOD = 6
EPS = 1e-6
NEG = -1e30

LANES = 128
SUBLANES = 8
BF16_ROWS = 16
VMEM_CAP = 64 * 1024 * 1024
SC_CORES = 2
SC_SUBCORES = 16

MOD_ROWS = 8
TOKEN_TILE = 256
HQ_TILES = D_MODEL // 2 // LANES + 1
GATHER_CHUNK = 128
POOL_HALO = 8
COMBINE_CHUNK = BF16_ROWS


def _vmem_limit(nbytes):
    return int(min(VMEM_CAP - (4 << 20), max(nbytes, 16 << 20)))


def _mod_kernel(c_ref, w_ref, b_ref, o_ref):
    c = c_ref[...]
    s = c * jax.nn.sigmoid(c)
    o_ref[...] = jnp.dot(s.astype(BF16), w_ref[...].astype(BF16),
                         preferred_element_type=F32) + b_ref[...]


def _modulation(cond, w_ada, b_ada):
    n = w_ada.shape[1]
    tn = 1024
    return pl.pallas_call(
        _mod_kernel,
        grid=(n // tn,),
        in_specs=[pl.BlockSpec((MOD_ROWS, D_MODEL), lambda j: (0, 0)),
                  pl.BlockSpec((D_MODEL, tn), lambda j: (0, j)),
                  pl.BlockSpec((1, tn), lambda j: (0, j))],
        out_specs=pl.BlockSpec((MOD_ROWS, tn), lambda j: (0, j)),
        out_shape=jax.ShapeDtypeStruct((MOD_ROWS, n), F32),
        compiler_params=pltpu.CompilerParams(
            vmem_limit_bytes=_vmem_limit(3 * D_MODEL * tn * 4)),
        name="modulation",
    )(cond, w_ada, b_ada.reshape(1, n))


def _norm_mod(x, g, shift, scale):
    ms = jnp.mean(x * x, axis=-1, keepdims=True)
    y = x * lax.rsqrt(ms + EPS)
    return (y * g) * (1.0 + scale) + shift


def _inproj_kernel(*refs, rope):
    if rope:
        x_ref, mod_ref, g_ref, w_ref, cos_ref, sa_ref, sb_ref, q_ref, k_ref, v_ref, p_ref = refs
    else:
        x_ref, mod_ref, g_ref, w_ref, q_ref, k_ref, v_ref, p_ref = refs
    mod = mod_ref[0]
    h = _norm_mod(x_ref[...], g_ref[...], mod[0:1], mod[1:2])
    u = jnp.dot(h.astype(BF16), w_ref[...], preferred_element_type=F32)

    def rot(xh):
        return (xh * cos_ref[...] + pltpu.roll(xh, LANES - ROPE_FREQS, 1) * sa_ref[...]
                + pltpu.roll(xh, ROPE_FREQS, 1) * sb_ref[...])

    for hd in range(N_HEADS):
        xh = u[:, hd * HEAD_DIM:(hd + 1) * HEAD_DIM]
        q_ref[:, hd * HEAD_DIM:(hd + 1) * HEAD_DIM] = (rot(xh) if rope else xh).astype(BF16)
    for hd in range(N_KV_HEADS):
        lo = ATTN_WIDTH + hd * HEAD_DIM
        xh = u[:, lo:lo + HEAD_DIM]
        k_ref[:, hd * HEAD_DIM:(hd + 1) * HEAD_DIM] = rot(xh) if rope else xh
    v_ref[...] = u[:, ATTN_WIDTH + KV_WIDTH:ATTN_WIDTH + 2 * KV_WIDTH]
    p_ref[...] = u[:, ATTN_WIDTH + 2 * KV_WIDTH:]


def _in_projection(x2, mod3, mod_row, g, w_in_bf, rope_tabs):
    t = x2.shape[0]
    tm = TOKEN_TILE
    rope = rope_tabs is not None
    row = lambda i: (i, 0)
    in_specs = [pl.BlockSpec((tm, D_MODEL), row),
                pl.BlockSpec((1, N_MOD, D_MODEL), lambda i: (mod_row(i), 0, 0)),
                pl.BlockSpec((1, D_MODEL), lambda i: (0, 0)),
                pl.BlockSpec((D_MODEL, IN_WIDTH), lambda i: (0, 0))]
    args = [x2, mod3, g, w_in_bf]
    if rope:
        n_seq = rope_tabs[0].shape[0]
        seq_blocks = n_seq // tm
        for tab in rope_tabs:
            in_specs.append(pl.BlockSpec((tm, HEAD_DIM), lambda i: (i % seq_blocks, 0)))
            args.append(tab)
    return pl.pallas_call(
        functools.partial(_inproj_kernel, rope=rope),
        grid=(t // tm,),
        in_specs=in_specs,
        out_specs=[pl.BlockSpec((tm, ATTN_WIDTH), row),
                   pl.BlockSpec((tm, KV_WIDTH), row),
                   pl.BlockSpec((tm, KV_WIDTH), row),
                   pl.BlockSpec((tm, POOL_WIDTH), row)],
        out_shape=[jax.ShapeDtypeStruct((t, ATTN_WIDTH), BF16),
                   jax.ShapeDtypeStruct((t, KV_WIDTH), F32),
                   jax.ShapeDtypeStruct((t, KV_WIDTH), F32),
                   jax.ShapeDtypeStruct((t, POOL_WIDTH), F32)],
        compiler_params=pltpu.CompilerParams(
            vmem_limit_bytes=_vmem_limit(2 * D_MODEL * IN_WIDTH * 2 + 24 * tm * D_MODEL * 4)),
        name="in_projection",
    )(*args)


def _softmax_pv(s_list, v_list, sink_col):
    m = sink_col
    for s in s_list:
        m = jnp.maximum(m, jnp.max(s, axis=-1, keepdims=True))
    denom = jnp.exp(sink_col - m)
    out = None
    for s, v in zip(s_list, v_list):
        e = jnp.exp(s - m)
        denom = denom + jnp.sum(e, axis=-1, keepdims=True)
        o = jnp.dot(e.astype(BF16), v, preferred_element_type=F32)
        out = o if out is None else out + o
    return out * (1.0 / denom)


def _stack_heads(q, kv):
    return jnp.concatenate(
        [q[:, (kv * Q_PER_KV + g) * HEAD_DIM:(kv * Q_PER_KV + g + 1) * HEAD_DIM]
         for g in range(Q_PER_KV)], axis=0)


def _sink_column(sink_ref, kv, rows):
    r = lax.broadcasted_iota(I32, (Q_PER_KV * rows, 1), 0)
    col = jnp.zeros((Q_PER_KV * rows, 1), F32)
    for g in range(Q_PER_KV):
        col = jnp.where((r >= g * rows) & (r < (g + 1) * rows), sink_ref[kv * Q_PER_KV + g], col)
    return col


def _qk(q, k):
    return lax.dot_general(q, k, (((1,), (1,)), ((), ())), preferred_element_type=F32)


def _ctx_attn_kernel(sink_ref, q_ref, k_ref, v_ref, o_ref):
    rows = q_ref.shape[0]
    scale = HEAD_DIM ** -0.5
    q = q_ref[...]
    for kv in range(N_KV_HEADS):
        kh = k_ref[:, kv * HEAD_DIM:(kv + 1) * HEAD_DIM].astype(BF16)
        vh = v_ref[:, kv * HEAD_DIM:(kv + 1) * HEAD_DIM].astype(BF16)
        qs = _stack_heads(q, kv)
        s = _qk(qs, kh) * scale
        o = _softmax_pv([s], [vh], _sink_column(sink_ref, kv, rows))
        for g in range(Q_PER_KV):
            hd = kv * Q_PER_KV + g
            o_ref[:, hd * HEAD_DIM:(hd + 1) * HEAD_DIM] = o[g * rows:(g + 1) * rows].astype(BF16)


def _context_attention(q, k, v, sink, seq):
    t = q.shape[0]
    row = lambda b: (b, 0)
    return pl.pallas_call(
        _ctx_attn_kernel,
        grid=(t // seq,),
        in_specs=[pl.BlockSpec(memory_space=pltpu.SMEM),
                  pl.BlockSpec((seq, ATTN_WIDTH), row),
                  pl.BlockSpec((seq, KV_WIDTH), row),
                  pl.BlockSpec((seq, KV_WIDTH), row)],
        out_specs=pl.BlockSpec((seq, ATTN_WIDTH), row),
        out_shape=jax.ShapeDtypeStruct((t, ATTN_WIDTH), BF16),
        name="context_attention",
    )(sink, q, k, v)


def _lat_attn_kernel(sink_ref, q_ref, k_ref, v_ref, ck_ref, cv_ref, o_ref, *, n_seq):
    i = pl.program_id(1)
    scale = HEAD_DIM ** -0.5
    band = 3 * BLOCK
    start = pl.multiple_of(jnp.clip((i - 1) * BLOCK, 0, n_seq - band), BLOCK)
    rows = Q_PER_KV * BLOCK
    qpos = i * BLOCK + lax.broadcasted_iota(I32, (rows, band), 0) % BLOCK
    kpos = start + lax.broadcasted_iota(I32, (rows, band), 1)
    mask = jnp.abs(kpos - qpos) <= WINDOW
    q = q_ref[...]
    for kv in range(N_KV_HEADS):
        cols = slice(kv * HEAD_DIM, (kv + 1) * HEAD_DIM)
        kb = k_ref[pl.ds(start, band), cols].astype(BF16)
        vb = v_ref[pl.ds(start, band), cols].astype(BF16)
        ck = ck_ref[0, :, cols].astype(BF16)
        cv = cv_ref[0, :, cols].astype(BF16)
        qs = _stack_heads(q, kv)
        s_loc = jnp.where(mask, _qk(qs, kb) * scale, NEG)
        s_ctx = _qk(qs, ck) * scale
        o = _softmax_pv([s_loc, s_ctx], [vb, cv], _sink_column(sink_ref, kv, BLOCK))
        for g in range(Q_PER_KV):
            hd = kv * Q_PER_KV + g
            o_ref[:, hd * HEAD_DIM:(hd + 1) * HEAD_DIM] = o[g * BLOCK:(g + 1) * BLOCK].astype(BF16)


def _latent_attention(q, k, v, ck, cv, sink, n_seq):
    t = q.shape[0]
    nb = n_seq // BLOCK
    past = ck.shape[1]
    return pl.pallas_call(
        functools.partial(_lat_attn_kernel, n_seq=n_seq),
        grid=(t // n_seq, nb),
        in_specs=[pl.BlockSpec(memory_space=pltpu.SMEM),
                  pl.BlockSpec((BLOCK, ATTN_WIDTH), lambda b, i: (b * nb + i, 0)),
                  pl.BlockSpec((n_seq, KV_WIDTH), lambda b, i: (b, 0)),
                  pl.BlockSpec((n_seq, KV_WIDTH), lambda b, i: (b, 0)),
                  pl.BlockSpec((1, past, KV_WIDTH), lambda b, i: (b, 0, 0)),
                  pl.BlockSpec((1, past, KV_WIDTH), lambda b, i: (b, 0, 0))],
        out_specs=pl.BlockSpec((BLOCK, ATTN_WIDTH), lambda b, i: (b * nb + i, 0)),
        out_shape=jax.ShapeDtypeStruct((t, ATTN_WIDTH), BF16),
        name="latent_attention",
    )(sink, q, k, v, ck, cv)


def _pool_kernel(p_ref, w_ref, s_ref, o_ref, pad_ref):
    n = p_ref.shape[0]
    rows = n + 2 * POOL_HALO
    t = lax.broadcasted_iota(I32, (n, 1), 0)
    zeros = jnp.zeros((POOL_HALO, POOL_GROUP), F32)
    pad_ref[0:POOL_HALO, :] = zeros
    pad_ref[POOL_HALO + n:rows, :] = zeros
    for g, w in enumerate(POOL_SIZES):
        cols = slice(g * POOL_GROUP, (g + 1) * POOL_GROUP)
        pg = p_ref[:, cols]
        pad_ref[POOL_HALO:POOL_HALO + n, :] = pg
        x = pad_ref[...]
        acc = x + pltpu.roll(x, 1, 0)
        step = 1
        while 2 * step < w:
            acc = pltpu.roll(acc, step, 0) + pltpu.roll(acc, rows - step, 0)
            step *= 2
        wsum = acc[POOL_HALO:POOL_HALO + n]
        lo = jnp.maximum(t - w // 2, 0)
        hi = jnp.minimum(t + w - w // 2, n)
        inv_cnt = 1.0 / (hi - lo).astype(F32)
        mixed = wsum * inv_cnt - pg
        y = jnp.dot(mixed.astype(BF16), w_ref[g].astype(BF16), preferred_element_type=F32)
        o_ref[:, cols] = (y * s_ref[:, cols]).astype(BF16)


def _pool_mixer(p, pool_w, pool_scale, seq):
    t = p.shape[0]
    row = lambda b: (b, 0)
    return pl.pallas_call(
        _pool_kernel,
        grid=(t // seq,),
        in_specs=[pl.BlockSpec((seq, POOL_WIDTH), row),
                  pl.BlockSpec((len(POOL_SIZES), POOL_GROUP, POOL_GROUP), lambda b: (0, 0, 0)),
                  pl.BlockSpec((1, POOL_WIDTH), lambda b: (0, 0))],
        out_specs=pl.BlockSpec((seq, POOL_WIDTH), row),
        out_shape=jax.ShapeDtypeStruct((t, POOL_WIDTH), BF16),
        scratch_shapes=[pltpu.VMEM((seq + 2 * POOL_HALO, POOL_GROUP), F32)],
        compiler_params=pltpu.CompilerParams(
            vmem_limit_bytes=_vmem_limit(8 * seq * POOL_WIDTH * 4)),
        name="pool_mixer",
    )(p, pool_w, pool_scale)


def _bf16_bits(x):
    u = lax.bitcast_convert_type(x, jnp.uint32)
    odd = lax.shift_right_logical(u, jnp.uint32(16)) & jnp.uint32(1)
    return (u + jnp.uint32(0x7FFF) + odd) & jnp.uint32(0xFFFF0000)


def _outproj_kernel(a_ref, p_ref, x_ref, mod_ref, g_ref, wo_ref, wr_ref, x1_ref, h_ref, aff_ref):
    mod = mod_ref[0]
    mix = (jnp.dot(a_ref[...], wo_ref[0:ATTN_WIDTH, :], preferred_element_type=F32)
           + jnp.dot(p_ref[...], wo_ref[ATTN_WIDTH:D_MODEL, :], preferred_element_type=F32))
    x1 = x_ref[...] + mod[2:3] * mix
    x1_ref[...] = x1
    h = _norm_mod(x1, g_ref[...], mod[3:4], mod[4:5])
    logits = jnp.dot(h.astype(BF16), wr_ref[...], preferred_element_type=F32)
    lane = lax.broadcasted_iota(I32, logits.shape, 1)
    logits = jnp.where(lane < N_EXPERTS, logits, -jnp.inf)
    m = jnp.max(logits, axis=-1, keepdims=True)
    e = jnp.exp(logits - m)
    aff = e / jnp.sum(e, axis=-1, keepdims=True)
    aff_ref[...] = aff[:, 0:N_EXPERTS]
    groups = h.shape[0] // SUBLANES
    half = D_MODEL // 2
    for c in range(HQ_TILES - 1):
        lo = _bf16_bits(h[:, c * LANES:(c + 1) * LANES])
        hi = _bf16_bits(h[:, half + c * LANES:half + (c + 1) * LANES])
        words = lax.shift_right_logical(lo, jnp.uint32(16)) | hi
        h_ref[:, c * SUBLANES:(c + 1) * SUBLANES, :] = (
            lax.bitcast_convert_type(words, F32).reshape(groups, SUBLANES, LANES))
    h_ref[:, (HQ_TILES - 1) * SUBLANES:, :] = aff.reshape(groups, SUBLANES, LANES)


def _out_projection(attn, pooled, x2, mod3, mod_row, g, w_out_bf, w_router):
    t = x2.shape[0]
    tm = TOKEN_TILE
    row = lambda i: (i, 0)
    return pl.pallas_call(
        _outproj_kernel,
        grid=(t // tm,),
        in_specs=[pl.BlockSpec((tm, ATTN_WIDTH), row),
                  pl.BlockSpec((tm, POOL_WIDTH), row),
                  pl.BlockSpec((tm, D_MODEL), row),
                  pl.BlockSpec((1, N_MOD, D_MODEL), lambda i: (mod_row(i), 0, 0)),
                  pl.BlockSpec((1, D_MODEL), lambda i: (0, 0)),
                  pl.BlockSpec((D_MODEL, D_MODEL), lambda i: (0, 0)),
                  pl.BlockSpec((D_MODEL, LANES), lambda i: (0, 0))],
        out_specs=[pl.BlockSpec((tm, D_MODEL), row),
                   pl.BlockSpec((tm // SUBLANES, HQ_TILES * SUBLANES, LANES), lambda i: (i, 0, 0)),
                   pl.BlockSpec((tm, N_EXPERTS), row)],
        out_shape=[jax.ShapeDtypeStruct((t, D_MODEL), F32),
                   jax.ShapeDtypeStruct((t // SUBLANES, HQ_TILES * SUBLANES, LANES), F32),
                   jax.ShapeDtypeStruct((t, N_EXPERTS), F32)],
        compiler_params=pltpu.CompilerParams(
            vmem_limit_bytes=_vmem_limit(2 * D_MODEL * D_MODEL * 2 + 24 * tm * D_MODEL * 4)),
        name="out_projection",
    )(attn, pooled, x2, mod3, g, w_out_bf, w_router)


def _route_kernel(a_ref, idx_ref, slot_ref, off_ref, *, cap):
    a = a_ref[...]
    n_e, n_c, _ = a.shape
    rows = n_e * n_c

    def bisect(it, thr):
        cand = thr | jnp.left_shift(jnp.int32(1), 30 - it)
        cand_f = lax.bitcast_convert_type(cand, F32)
        cnt = jnp.sum(jnp.sum((a >= cand_f).astype(F32), axis=2, keepdims=True),
                      axis=1, keepdims=True)
        return jnp.where(cnt >= cap, cand, thr)

    thr = lax.fori_loop(0, 31, bisect, jnp.zeros((n_e, 1, 1), I32))
    thr_f = lax.bitcast_convert_type(thr, F32)
    gt = (a > thr_f).astype(F32).reshape(rows, LANES)
    eq = (a == thr_f).astype(F32).reshape(rows, LANES)

    li = lax.broadcasted_iota(I32, (LANES, LANES), 0)
    lj = lax.broadcasted_iota(I32, (LANES, LANES), 1)
    upper_incl = (li <= lj).astype(BF16)
    ri = lax.broadcasted_iota(I32, (rows, rows), 0)
    rj = lax.broadcasted_iota(I32, (rows, rows), 1)
    same_expert = (ri // n_c) == (rj // n_c)
    before = (same_expert & (rj < ri)).astype(BF16)
    whole = same_expert.astype(BF16)

    def lane_bcast(col):
        return jnp.broadcast_to(col, (rows, LANES)).astype(BF16)

    def prefix(x):
        incl = jnp.dot(x.astype(BF16), upper_incl, preferred_element_type=F32)
        tot = incl[:, LANES - 1:LANES]
        off = jnp.dot(before, lane_bcast(tot), preferred_element_type=F32)
        return incl, tot, off

    n_gt = jnp.dot(whole, lane_bcast(jnp.sum(gt, axis=1, keepdims=True)),
                   preferred_element_type=F32)
    need = cap - n_gt
    incl_eq, _, off_eq = prefix(eq)
    rank_eq = off_eq + incl_eq - eq
    sel = jnp.where((eq > 0) & (rank_eq < need), 1.0, gt)
    incl, tot, off = prefix(sel)
    slot = off + incl - sel
    slot_ref[...] = jnp.where(sel > 0, slot, -1.0).astype(I32).reshape(n_e, n_c, LANES)
    off_ref[...] = off.astype(I32).reshape(n_e, n_c, LANES)

    s_lane = lax.broadcasted_iota(I32, (1, cap), 1).astype(F32)
    c_col = lax.broadcasted_iota(I32, (n_c, 1), 0).astype(F32)
    for e in range(n_e):
        r0 = e * n_c
        incl_e = incl[r0:r0 + n_c]
        off_e = off[r0:r0 + n_c, 0:1]
        tot_e = tot[r0:r0 + n_c]
        onehot = ((off_e <= s_lane) & (s_lane < off_e + tot_e)).astype(F32)
        counts = lax.dot_general(incl_e.astype(BF16), onehot.astype(BF16),
                                 (((0,), (0,)), ((), ())), preferred_element_type=F32)
        local = s_lane - jnp.sum(onehot * off_e, axis=0, keepdims=True)
        lane = jnp.sum((counts <= local).astype(F32), axis=0, keepdims=True)
        chunk = jnp.sum(onehot * c_col, axis=0, keepdims=True)
        idx_ref[e] = (chunk * LANES + lane).astype(I32)


def _routing(aff, cap):
    t = aff.shape[0]
    n_c = t // LANES
    a3 = aff.T.reshape(N_EXPERTS, n_c, LANES)
    return pl.pallas_call(
        functools.partial(_route_kernel, cap=cap),
        out_shape=[jax.ShapeDtypeStruct((N_EXPERTS, 1, cap), I32),
                   jax.ShapeDtypeStruct((N_EXPERTS, n_c, LANES), I32),
                   jax.ShapeDtypeStruct((N_EXPERTS, n_c, LANES), I32)],
        compiler_params=pltpu.CompilerParams(vmem_limit_bytes=_vmem_limit(48 << 20)),
        name="routing",
    )(a3)


def _gather_rows(table, row_ids):
    n_chunks = row_ids.shape[0]
    n_workers = SC_CORES * SC_SUBCORES
    per_worker = n_chunks // n_workers
    assert row_ids.shape[1] == GATHER_CHUNK and n_chunks % n_workers == 0
    mesh = plsc.VectorSubcoreMesh(core_axis_name="core", subcore_axis_name="subcore")

    @functools.partial(
        pl.kernel, mesh=mesh,
        out_type=jax.ShapeDtypeStruct((n_chunks * GATHER_CHUNK, LANES), F32),
        scratch_types=[pltpu.VMEM((GATHER_CHUNK,), I32),
                       pltpu.VMEM((GATHER_CHUNK, LANES), F32),
                       pltpu.SemaphoreType.DMA],
        name="gather_rows",
    )
    def gather(table_hbm, ids_hbm, out_hbm, ids_v, rows_v, sem):
        worker = lax.axis_index("subcore") * SC_CORES + lax.axis_index("core")

        @pl.loop(0, per_worker)
        def _(j):
            chunk = worker * per_worker + j
            pltpu.sync_copy(ids_hbm.at[chunk], ids_v)
            pltpu.async_copy(table_hbm.at[ids_v], rows_v, sem).wait()
            pltpu.sync_copy(
                rows_v, out_hbm.at[pl.ds(pl.multiple_of(chunk * GATHER_CHUNK, GATHER_CHUNK),
                                         GATHER_CHUNK)])

    return gather(table, row_ids)


def _packed_row_ids(idx, cap):
    tok = idx.reshape(N_EXPERTS, 1, cap // GATHER_CHUNK, GATHER_CHUNK)
    tile = jnp.arange(HQ_TILES, dtype=I32).reshape(1, HQ_TILES, 1, 1)
    ids = ((tok // SUBLANES) * HQ_TILES + tile) * SUBLANES + tok % SUBLANES
    return ids.reshape(-1, GATHER_CHUNK)


def _unpack_pair(words):
    u = lax.bitcast_convert_type(words, jnp.uint32)
    lo = lax.bitcast_convert_type(lax.shift_left(u, jnp.uint32(16)), F32)
    hi = lax.bitcast_convert_type(u & jnp.uint32(0xFFFF0000), F32)
    return lo, hi


def _ffn_kernel(x_ref, wg_ref, wu_ref, wd_ref, y_ref, acc_ref, *, n_f, row_chunk):
    e = pl.program_id(0)
    f = pl.program_id(1)
    cap = y_ref.shape[1]

    @pl.when((e == 0) & (f == 0))
    def _():
        acc_ref[...] = jnp.zeros_like(acc_ref)

    first = f == 0
    for r in range(cap // row_chunk):
        rs = slice(r * row_chunk, (r + 1) * row_chunk)
        pairs = [_unpack_pair(x_ref[0, c, rs, :]) for c in range(HQ_TILES - 1)]
        x = jnp.concatenate([p[0] for p in pairs] + [p[1] for p in pairs], axis=1)
        gate_act = jnp.dot(x, wg_ref[0], preferred_element_type=F32)
        up = jnp.dot(x, wu_ref[0], preferred_element_type=F32)
        hid = (gate_act * jax.nn.sigmoid(gate_act)) * up
        part = jnp.dot(hid, wd_ref[0], preferred_element_type=F32)
        acc_ref[rs, :] = part + jnp.where(first, 0.0, acc_ref[rs, :])

    @pl.when(f == n_f - 1)
    def _():
        aff = x_ref[0, HQ_TILES - 1]
        lane = lax.broadcasted_iota(I32, aff.shape, 1)
        gates = jnp.sum(jnp.where(lane == e, aff, 0.0), axis=1, keepdims=True)
        y_ref[0] = (acc_ref[...] * gates).astype(BF16)


def _expert_ffn(idx, h_packed, w_gate, w_up, w_down, cap):
    n_f = 4
    tf = D_EXPERT // n_f
    row_chunk = 512
    table = h_packed.reshape(-1, LANES)
    xs = _gather_rows(table, _packed_row_ids(idx, cap))
    xs = xs.reshape(N_EXPERTS, HQ_TILES, cap, LANES)
    est = (2 * HQ_TILES * cap * LANES * 4 + cap * D_MODEL * (4 + 2 * 2) + 2 * 3 * D_MODEL * tf * 4
           + row_chunk * (3 * tf + 2 * D_MODEL) * 4)
    return pl.pallas_call(
        functools.partial(_ffn_kernel, n_f=n_f, row_chunk=row_chunk),
        grid=(N_EXPERTS, n_f),
        in_specs=[pl.BlockSpec((1, HQ_TILES, cap, LANES), lambda e, f: (e, 0, 0, 0)),
                  pl.BlockSpec((1, D_MODEL, tf), lambda e, f: (e, 0, f)),
                  pl.BlockSpec((1, D_MODEL, tf), lambda e, f: (e, 0, f)),
                  pl.BlockSpec((1, tf, D_MODEL), lambda e, f: (e, f, 0))],
        out_specs=pl.BlockSpec((1, cap, D_MODEL), lambda e, f: (e, 0, 0)),
        out_shape=jax.ShapeDtypeStruct((N_EXPERTS, cap, D_MODEL), BF16),
        scratch_shapes=[pltpu.VMEM((cap, D_MODEL), F32)],
        compiler_params=pltpu.CompilerParams(
            dimension_semantics=("arbitrary", "arbitrary"),
            vmem_limit_bytes=_vmem_limit(est + (4 << 20))),
        name="expert_ffn",
    )(xs, w_gate, w_up, w_down)


def _combine_kernel(start_ref, nch_ref, x1_ref, mod_ref, g_ref, slot_ref, y_hbm, o_ref,
                    ybuf_ref, cols_ref, ffn_ref, sem, *, first_k, kblock):
    i = pl.program_id(0)
    tm = x1_ref.shape[0]
    cur = i % 2

    def chunk_copy(e, src_row, buf, dst_row):
        return pltpu.make_async_copy(
            y_hbm.at[e, pl.ds(pl.multiple_of(src_row, COMBINE_CHUNK), COMBINE_CHUNK)],
            ybuf_ref.at[buf, pl.ds(pl.multiple_of(dst_row, COMBINE_CHUNK), COMBINE_CHUNK)],
            sem.at[buf])

    def fetch(tile, buf):
        pos = jnp.int32(0)
        for e in range(N_EXPERTS):
            start = start_ref[tile, e]
            nch = nch_ref[tile, e]

            def issue(c, carry, e=e, start=start, pos=pos):
                chunk_copy(e, start + c * COMBINE_CHUNK, buf, pos + c * COMBINE_CHUNK).start()
                return carry
            lax.fori_loop(0, nch, issue, 0)
            pos = pos + nch * COMBINE_CHUNK

    @pl.when(i == 0)
    def _():
        ybuf_ref[...] = jnp.zeros_like(ybuf_ref)
        fetch(0, 0)

    @pl.when(i + 1 < pl.num_programs(0))
    def _():
        fetch(i + 1, 1 - cur)

    pos = jnp.int32(0)
    begins = []
    for e in range(N_EXPERTS):
        begins.append(pos)
        pos = pos + nch_ref[i, e] * COMBINE_CHUNK
    begins.append(pos)

    def drain(c, carry):
        chunk_copy(0, jnp.int32(0), cur, jnp.int32(0)).wait()
        return carry
    lax.fori_loop(0, pos // COMBINE_CHUNK, drain, 0)

    slots = slot_ref[...]
    for e in range(N_EXPERTS):
        s_e = slots[:, e:e + 1]
        col = jnp.where(s_e >= 0, s_e + (begins[e] - start_ref[i, e]), -1)
        cols_ref[e] = jnp.broadcast_to(col, (tm, LANES))

    lane = lax.broadcasted_iota(I32, (tm, LANES), 1)

    def selection(k0, width):
        halves = []
        for h in range(width // LANES):
            c0 = k0 + h * LANES
            target = lane + c0
            e_lo = jnp.int32(0)
            e_hi = jnp.int32(0)
            for e in range(N_EXPERTS):
                e_lo = e_lo + (begins[e + 1] <= c0).astype(I32)
                e_hi = e_hi + (begins[e] < c0 + LANES).astype(I32)

            def mark(e, hit, target=target):
                return jnp.where(cols_ref[e] == target, 1.0, hit)
            hit = lax.fori_loop(e_lo, e_hi, mark, jnp.zeros((tm, LANES), F32))
            halves.append(hit.astype(BF16))
        return jnp.concatenate(halves, axis=1)

    def block(k0, width):
        return jnp.dot(selection(k0, width), ybuf_ref[cur, pl.ds(k0, width), :],
                       preferred_element_type=F32)

    ffn_ref[...] = block(0, first_k)

    def kstep(kb, carry):
        ffn_ref[...] += block(pl.multiple_of(kb * kblock, kblock), kblock)
        return carry
    lax.fori_loop(first_k // kblock, (pos + kblock - 1) // kblock, kstep, 0)

    out = x1_ref[...] + mod_ref[0][5:6] * ffn_ref[...]
    ms = jnp.mean(out * out, axis=-1, keepdims=True)
    o_ref[...] = (out * lax.rsqrt(ms + EPS)) * g_ref[...]


def _combine(x1, mod3, mod_row, g_final, slot_t, tile_start, tile_nch, y):
    t = x1.shape[0]
    tm = TOKEN_TILE
    first_k = 2 * tm + 2 * N_EXPERTS * COMBINE_CHUNK
    kblock = 512
    max_rows = N_EXPERTS * (tm + 2 * COMBINE_CHUNK)
    max_rows = -(-max_rows // kblock) * kblock
    row = lambda i, *_: (i, 0)
    grid_spec = pltpu.PrefetchScalarGridSpec(
        num_scalar_prefetch=2,
        grid=(t // tm,),
        in_specs=[pl.BlockSpec((tm, D_MODEL), row),
                  pl.BlockSpec((1, N_MOD, D_MODEL), lambda i, *_: (mod_row(i), 0, 0)),
                  pl.BlockSpec((1, D_MODEL), lambda i, *_: (0, 0)),
                  pl.BlockSpec((tm, N_EXPERTS), row),
                  pl.BlockSpec(memory_space=pl.ANY)],
        out_specs=pl.BlockSpec((tm, D_MODEL), row),
        scratch_shapes=[pltpu.VMEM((2, max_rows, D_MODEL), BF16),
                        pltpu.VMEM((N_EXPERTS, tm, LANES), I32),
                        pltpu.VMEM((tm, D_MODEL), F32),
                        pltpu.SemaphoreType.DMA((2,))],
    )
    return pl.pallas_call(
        functools.partial(_combine_kernel, first_k=first_k, kblock=kblock),
        grid_spec=grid_spec,
        out_shape=jax.ShapeDtypeStruct((t, D_MODEL), F32),
        compiler_params=pltpu.CompilerParams(
            dimension_semantics=("arbitrary",),
            vmem_limit_bytes=_vmem_limit(2 * max_rows * D_MODEL * 2 + 16 * tm * D_MODEL * 4)),
        name="combine",
    )(tile_start, tile_nch, x1, mod3, g_final, slot_t, y)


def _rope_tables(n):
    rows = n // GRID_W
    row = jnp.repeat(jnp.arange(rows, dtype=F32), GRID_W)
    col = jnp.tile(jnp.arange(GRID_W, dtype=F32), rows)
    inv = ROPE_THETA ** (-jnp.arange(ROPE_FREQS, dtype=F32) / ROPE_FREQS)
    ang_r = row[:, None] * inv
    ang_c = col[:, None] * inv
    zero = jnp.zeros_like(ang_r)
    cos = jnp.concatenate([jnp.cos(ang_r)] * 2 + [jnp.cos(ang_c)] * 2, axis=1)
    sin_a = jnp.concatenate([-jnp.sin(ang_r), zero, -jnp.sin(ang_c), zero], axis=1)
    sin_b = jnp.concatenate([zero, jnp.sin(ang_r), zero, jnp.sin(ang_c)], axis=1)
    return cos, sin_a, sin_b


def _token_group(x, mod3, mod_row, seq, weights, rope_tabs, ctx_kv):
    (norm_mix, w_in_bf, sink, pool_w, pool_scale, w_out_bf, norm_ffn, w_router,
     w_gate, w_up, w_down, norm_final) = weights
    b = x.shape[0]
    t = b * seq
    x2 = x.reshape(t, D_MODEL)
    q, k, v, p = _in_projection(x2, mod3, mod_row, norm_mix, w_in_bf, rope_tabs)
    if ctx_kv is None:
        attn = _context_attention(q, k, v, sink, seq)
    else:
        attn = _latent_attention(q, k, v, ctx_kv[0], ctx_kv[1], sink, seq)
    pooled = _pool_mixer(p, pool_w, pool_scale, seq)
    x1, h_ext, aff = _out_projection(attn, pooled, x2, mod3, mod_row, norm_ffn, w_out_bf, w_router)

    cap = EC_FACTOR * t // N_EXPERTS
    idx, slot3, off3 = _routing(aff, cap)
    y = _expert_ffn(idx, h_ext, w_gate, w_up, w_down, cap)

    chunks_per_tile = TOKEN_TILE // LANES
    tile_off = off3[:, ::chunks_per_tile, 0]
    tile_end = jnp.concatenate([tile_off[:, 1:], jnp.full((N_EXPERTS, 1), cap, I32)], axis=1)
    tile_start = (tile_off // COMBINE_CHUNK) * COMBINE_CHUNK
    tile_nch = jnp.where(tile_end > tile_off,
                         (tile_end - tile_start + COMBINE_CHUNK - 1) // COMBINE_CHUNK, 0)
    slot_t = slot3.reshape(N_EXPERTS, t).T
    out = _combine(x1, mod3, mod_row, norm_final, slot_t, tile_start.T, tile_nch.T, y)
    return out.reshape(b, seq, D_MODEL), k, v


def kernel(x_prompt, x_sample, c, cache_k, cache_v, c_ctx, w_ada, b_ada, norm_mix, w_in,
           sink_logits, pool_w, pool_scale, w_out, norm_ffn, w_router, w_gate, w_up, w_down,
           norm_final):
    n_b, seq, _ = x_prompt.shape
    n_db, n_lat, _ = x_sample.shape
    assert 1 + n_db <= MOD_ROWS and seq == TOKEN_TILE and n_lat % TOKEN_TILE == 0

    cond = jnp.concatenate(
        [c_ctx[None, :], c, jnp.zeros((MOD_ROWS - 1 - n_db, D_MODEL), F32)], axis=0)
    mod3 = _modulation(cond, w_ada[0], b_ada[0]).reshape(MOD_ROWS, N_MOD, D_MODEL)

    w_router_bf = jnp.pad(w_router[0], ((0, 0), (0, LANES - N_EXPERTS))).astype(BF16)
    weights = (norm_mix[0][None, :], w_in[0].astype(BF16), sink_logits[0], pool_w[0],
               pool_scale[0][None, :], w_out[0].astype(BF16), norm_ffn[0][None, :], w_router_bf,
               w_gate.reshape(w_gate.shape[1:]), w_up.reshape(w_up.shape[1:]),
               w_down.reshape(w_down.shape[1:]), norm_final[None, :])

    y_prompt, k_p, v_p = _token_group(
        x_prompt, mod3, lambda i: 0, seq, weights, None, None)

    tiles_per_seq = n_lat // TOKEN_TILE
    past = cache_k.shape[2]
    ck = cache_k[:, 0].reshape(n_db, past, KV_WIDTH)
    cv = cache_v[:, 0].reshape(n_db, past, KV_WIDTH)
    y_sample, _, _ = _token_group(
        x_sample, mod3, lambda i: 1 + i // tiles_per_seq, n_lat, weights,
        _rope_tables(n_lat), (ck, cv))

    state_k = k_p.reshape(n_b, 1, seq, N_KV_HEADS, HEAD_DIM)
    state_v = v_p.reshape(n_b, 1, seq, N_KV_HEADS, HEAD_DIM)
    return (y_prompt, y_sample, state_k, state_v)
```

```python
import functools

import jax
import jax.numpy as jnp
import numpy as np
from jax import lax
from jax.experimental import pallas as pl
from jax.experimental.pallas import tpu as pltpu
from jax.experimental.pallas import tpu_sc as plsc

F32 = jnp.float32
BF16 = jnp.bfloat16
I32 = jnp.int32

D_MODEL = 2048
N_HEADS = 8
N_KV_HEADS = 2
HEAD_DIM = 128
Q_PER_KV = N_HEADS // N_KV_HEADS
ATTN_WIDTH = N_HEADS * HEAD_DIM
KV_WIDTH = N_KV_HEADS * HEAD_DIM
POOL_WIDTH = D_MODEL - ATTN_WIDTH
POOL_SIZES = (2, 4, 8, 16)
POOL_GROUP = POOL_WIDTH // len(POOL_SIZES)
IN_WIDTH = ATTN_WIDTH + 2 * KV_WIDTH + POOL_WIDTH
WINDOW = 128
BLOCK = 128
GRID_W = 64
ROPE_THETA = 10000.0
ROPE_FREQS = HEAD_DIM // 4
N_EXPERTS = 16
EC_FACTOR = 2
D_EXPERT = 1024
N_MOD = 6
EPS = 1e-6
NEG = -1e30

LANES = 128
SUBLANES = 8
BF16_ROWS = 16
VMEM_CAP = 64 * 1024 * 1024
SC_CORES = 2
SC_SUBCORES = 16

MOD_ROWS = 8
TOKEN_TILE = 256
HQ_TILES = D_MODEL // 2 // LANES + 1
GATHER_CHUNK = 128
POOL_HALO = 8
COMBINE_CHUNK = BF16_ROWS


def _vmem_limit(nbytes):
    return int(min(VMEM_CAP - (4 << 20), max(nbytes, 16 << 20)))


def _mod_kernel(c_ref, w_ref, b_ref, o_ref):
    c = c_ref[...]
    s = c * jax.nn.sigmoid(c)
    o_ref[...] = jnp.dot(s.astype(BF16), w_ref[...].astype(BF16),
                         preferred_element_type=F32) + b_ref[...]


def _modulation(cond, w_ada, b_ada):
    n = w_ada.shape[1]
    tn = 1024
    return pl.pallas_call(
        _mod_kernel,
        grid=(n // tn,),
        in_specs=[pl.BlockSpec((MOD_ROWS, D_MODEL), lambda j: (0, 0)),
                  pl.BlockSpec((D_MODEL, tn), lambda j: (0, j)),
                  pl.BlockSpec((1, tn), lambda j: (0, j))],
        out_specs=pl.BlockSpec((MOD_ROWS, tn), lambda j: (0, j)),
        out_shape=jax.ShapeDtypeStruct((MOD_ROWS, n), F32),
        compiler_params=pltpu.CompilerParams(
            vmem_limit_bytes=_vmem_limit(3 * D_MODEL * tn * 4)),
        name="modulation",
    )(cond, w_ada, b_ada.reshape(1, n))


def _norm_mod(x, g, shift, scale):
    ms = jnp.mean(x * x, axis=-1, keepdims=True)
    y = x * lax.rsqrt(ms + EPS)
    return (y * g) * (1.0 + scale) + shift


def _inproj_kernel(*refs, rope):
    if rope:
        x_ref, mod_ref, g_ref, w_ref, cos_ref, sa_ref, sb_ref, q_ref, k_ref, v_ref, p_ref = refs
    else:
        x_ref, mod_ref, g_ref, w_ref, q_ref, k_ref, v_ref, p_ref = refs
    mod = mod_ref[0]
    h = _norm_mod(x_ref[...], g_ref[...], mod[0:1], mod[1:2])
    u = jnp.dot(h.astype(BF16), w_ref[...], preferred_element_type=F32)

    def rot(xh):
        return (xh * cos_ref[...] + pltpu.roll(xh, LANES - ROPE_FREQS, 1) * sa_ref[...]
                + pltpu.roll(xh, ROPE_FREQS, 1) * sb_ref[...])

    for hd in range(N_HEADS):
        xh = u[:, hd * HEAD_DIM:(hd + 1) * HEAD_DIM]
        q_ref[:, hd * HEAD_DIM:(hd + 1) * HEAD_DIM] = (rot(xh) if rope else xh).astype(BF16)
    for hd in range(N_KV_HEADS):
        lo = ATTN_WIDTH + hd * HEAD_DIM
        xh = u[:, lo:lo + HEAD_DIM]
        k_ref[:, hd * HEAD_DIM:(hd + 1) * HEAD_DIM] = rot(xh) if rope else xh
    v_ref[...] = u[:, ATTN_WIDTH + KV_WIDTH:ATTN_WIDTH + 2 * KV_WIDTH]
    p_ref[...] = u[:, ATTN_WIDTH + 2 * KV_WIDTH:]


def _in_projection(x2, mod3, mod_row, g, w_in_bf, rope_tabs):
    t = x2.shape[0]
    tm = TOKEN_TILE
    rope = rope_tabs is not None
    row = lambda i: (i, 0)
    in_specs = [pl.BlockSpec((tm, D_MODEL), row),
                pl.BlockSpec((1, N_MOD, D_MODEL), lambda i: (mod_row(i), 0, 0)),
                pl.BlockSpec((1, D_MODEL), lambda i: (0, 0)),
                pl.BlockSpec((D_MODEL, IN_WIDTH), lambda i: (0, 0))]
    args = [x2, mod3, g, w_in_bf]
    if rope:
        n_seq = rope_tabs[0].shape[0]
        seq_blocks = n_seq // tm
        for tab in rope_tabs:
            in_specs.append(pl.BlockSpec((tm, HEAD_DIM), lambda i: (i % seq_blocks, 0)))
            args.append(tab)
    return pl.pallas_call(
        functools.partial(_inproj_kernel, rope=rope),
        grid=(t // tm,),
        in_specs=in_specs,
        out_specs=[pl.BlockSpec((tm, ATTN_WIDTH), row),
                   pl.BlockSpec((tm, KV_WIDTH), row),
                   pl.BlockSpec((tm, KV_WIDTH), row),
                   pl.BlockSpec((tm, POOL_WIDTH), row)],
        out_shape=[jax.ShapeDtypeStruct((t, ATTN_WIDTH), BF16),
                   jax.ShapeDtypeStruct((t, KV_WIDTH), F32),
                   jax.ShapeDtypeStruct((t, KV_WIDTH), F32),
                   jax.ShapeDtypeStruct((t, POOL_WIDTH), F32)],
        compiler_params=pltpu.CompilerParams(
            vmem_limit_bytes=_vmem_limit(2 * D_MODEL * IN_WIDTH * 2 + 24 * tm * D_MODEL * 4)),
        name="in_projection",
    )(*args)


def _softmax_pv(s_list, v_list, sink_col):
    m = sink_col
    for s in s_list:
        m = jnp.maximum(m, jnp.max(s, axis=-1, keepdims=True))
    denom = jnp.exp(sink_col - m)
    out = None
    for s, v in zip(s_list, v_list):
        e = jnp.exp(s - m)
        denom = denom + jnp.sum(e, axis=-1, keepdims=True)
        o = jnp.dot(e.astype(BF16), v, preferred_element_type=F32)
        out = o if out is None else out + o
    return out * (1.0 / denom)


def _stack_heads(q, kv):
    return jnp.concatenate(
        [q[:, (kv * Q_PER_KV + g) * HEAD_DIM:(kv * Q_PER_KV + g + 1) * HEAD_DIM]
         for g in range(Q_PER_KV)], axis=0)


def _sink_column(sink_ref, kv, rows):
    r = lax.broadcasted_iota(I32, (Q_PER_KV * rows, 1), 0)
    col = jnp.zeros((Q_PER_KV * rows, 1), F32)
    for g in range(Q_PER_KV):
        col = jnp.where((r >= g * rows) & (r < (g + 1) * rows), sink_ref[kv * Q_PER_KV + g], col)
    return col


def _qk(q, k):
    return lax.dot_general(q, k, (((1,), (1,)), ((), ())), preferred_element_type=F32)


def _ctx_attn_kernel(sink_ref, q_ref, k_ref, v_ref, o_ref):
    rows = q_ref.shape[0]
    scale = HEAD_DIM ** -0.5
    q = q_ref[...]
    for kv in range(N_KV_HEADS):
        kh = k_ref[:, kv * HEAD_DIM:(kv + 1) * HEAD_DIM].astype(BF16)
        vh = v_ref[:, kv * HEAD_DIM:(kv + 1) * HEAD_DIM].astype(BF16)
        qs = _stack_heads(q, kv)
        s = _qk(qs, kh) * scale
        o = _softmax_pv([s], [vh], _sink_column(sink_ref, kv, rows))
        for g in range(Q_PER_KV):
            hd = kv * Q_PER_KV + g
            o_ref[:, hd * HEAD_DIM:(hd + 1) * HEAD_DIM] = o[g * rows:(g + 1) * rows].astype(BF16)


def _context_attention(q, k, v, sink, seq):
    t = q.shape[0]
    row = lambda b: (b, 0)
    return pl.pallas_call(
        _ctx_attn_kernel,
        grid=(t // seq,),
        in_specs=[pl.BlockSpec(memory_space=pltpu.SMEM),
                  pl.BlockSpec((seq, ATTN_WIDTH), row),
                  pl.BlockSpec((seq, KV_WIDTH), row),
                  pl.BlockSpec((seq, KV_WIDTH), row)],
        out_specs=pl.BlockSpec((seq, ATTN_WIDTH), row),
        out_shape=jax.ShapeDtypeStruct((t, ATTN_WIDTH), BF16),
        name="context_attention",
    )(sink, q, k, v)


def _lat_attn_kernel(sink_ref, q_ref, k_ref, v_ref, ck_ref, cv_ref, o_ref, *, n_seq):
    i = pl.program_id(1)
    scale = HEAD_DIM ** -0.5
    band = 3 * BLOCK
    start = pl.multiple_of(jnp.clip((i - 1) * BLOCK, 0, n_seq - band), BLOCK)
    rows = Q_PER_KV * BLOCK
    qpos = i * BLOCK + lax.broadcasted_iota(I32, (rows, band), 0) % BLOCK
    kpos = start + lax.broadcasted_iota(I32, (rows, band), 1)
    mask = jnp.abs(kpos - qpos) <= WINDOW
    q = q_ref[...]
    for kv in range(N_KV_HEADS):
        cols = slice(kv * HEAD_DIM, (kv + 1) * HEAD_DIM)
        kb = k_ref[pl.ds(start, band), cols].astype(BF16)
        vb = v_ref[pl.ds(start, band), cols].astype(BF16)
        ck = ck_ref[0, :, cols].astype(BF16)
        cv = cv_ref[0, :, cols].astype(BF16)
        qs = _stack_heads(q, kv)
        s_loc = jnp.where(mask, _qk(qs, kb) * scale, NEG)
        s_ctx = _qk(qs, ck) * scale
        o = _softmax_pv([s_loc, s_ctx], [vb, cv], _sink_column(sink_ref, kv, BLOCK))
        for g in range(Q_PER_KV):
            hd = kv * Q_PER_KV + g
            o_ref[:, hd * HEAD_DIM:(hd + 1) * HEAD_DIM] = o[g * BLOCK:(g + 1) * BLOCK].astype(BF16)


def _latent_attention(q, k, v, ck, cv, sink, n_seq):
    t = q.shape[0]
    nb = n_seq // BLOCK
    past = ck.shape[1]
    return pl.pallas_call(
        functools.partial(_lat_attn_kernel, n_seq=n_seq),
        grid=(t // n_seq, nb),
        in_specs=[pl.BlockSpec(memory_space=pltpu.SMEM),
                  pl.BlockSpec((BLOCK, ATTN_WIDTH), lambda b, i: (b * nb + i, 0)),
                  pl.BlockSpec((n_seq, KV_WIDTH), lambda b, i: (b, 0)),
                  pl.BlockSpec((n_seq, KV_WIDTH), lambda b, i: (b, 0)),
                  pl.BlockSpec((1, past, KV_WIDTH), lambda b, i: (b, 0, 0)),
                  pl.BlockSpec((1, past, KV_WIDTH), lambda b, i: (b, 0, 0))],
        out_specs=pl.BlockSpec((BLOCK, ATTN_WIDTH), lambda b, i: (b * nb + i, 0)),
        out_shape=jax.ShapeDtypeStruct((t, ATTN_WIDTH), BF16),
        name="latent_attention",
    )(sink, q, k, v, ck, cv)


def _pool_kernel(p_ref, w_ref, s_ref, o_ref, pad_ref):
    n = p_ref.shape[0]
    rows = n + 2 * POOL_HALO
    t = lax.broadcasted_iota(I32, (n, 1), 0)
    zeros = jnp.zeros((POOL_HALO, POOL_GROUP), F32)
    pad_ref[0:POOL_HALO, :] = zeros
    pad_ref[POOL_HALO + n:rows, :] = zeros
    for g, w in enumerate(POOL_SIZES):
        cols = slice(g * POOL_GROUP, (g + 1) * POOL_GROUP)
        pg = p_ref[:, cols]
        pad_ref[POOL_HALO:POOL_HALO + n, :] = pg
        x = pad_ref[...]
        acc = x + pltpu.roll(x, 1, 0)
        step = 1
        while 2 * step < w:
            acc = pltpu.roll(acc, step, 0) + pltpu.roll(acc, rows - step, 0)
            step *= 2
        wsum = acc[POOL_HALO:POOL_HALO + n]
        lo = jnp.maximum(t - w // 2, 0)
        hi = jnp.minimum(t + w - w // 2, n)
        inv_cnt = 1.0 / (hi - lo).astype(F32)
        mixed = wsum * inv_cnt - pg
        y = jnp.dot(mixed.astype(BF16), w_ref[g].astype(BF16), preferred_element_type=F32)
        o_ref[:, cols] = (y * s_ref[:, cols]).astype(BF16)


def _pool_mixer(p, pool_w, pool_scale, seq):
    t = p.shape[0]
    row = lambda b: (b, 0)
    return pl.pallas_call(
        _pool_kernel,
        grid=(t // seq,),
        in_specs=[pl.BlockSpec((seq, POOL_WIDTH), row),
                  pl.BlockSpec((len(POOL_SIZES), POOL_GROUP, POOL_GROUP), lambda b: (0, 0, 0)),
                  pl.BlockSpec((1, POOL_WIDTH), lambda b: (0, 0))],
        out_specs=pl.BlockSpec((seq, POOL_WIDTH), row),
        out_shape=jax.ShapeDtypeStruct((t, POOL_WIDTH), BF16),
        scratch_shapes=[pltpu.VMEM((seq + 2 * POOL_HALO, POOL_GROUP), F32)],
        compiler_params=pltpu.CompilerParams(
            vmem_limit_bytes=_vmem_limit(8 * seq * POOL_WIDTH * 4)),
        name="pool_mixer",
    )(p, pool_w, pool_scale)


def _bf16_bits(x):
    u = lax.bitcast_convert_type(x, jnp.uint32)
    odd = lax.shift_right_logical(u, jnp.uint32(16)) & jnp.uint32(1)
    return (u + jnp.uint32(0x7FFF) + odd) & jnp.uint32(0xFFFF0000)


def _outproj_kernel(a_ref, p_ref, x_ref, mod_ref, g_ref, wo_ref, wr_ref, x1_ref, h_ref, aff_ref):
    mod = mod_ref[0]
    mix = (jnp.dot(a_ref[...], wo_ref[0:ATTN_WIDTH, :], preferred_element_type=F32)
           + jnp.dot(p_ref[...], wo_ref[ATTN_WIDTH:D_MODEL, :], preferred_element_type=F32))
    x1 = x_ref[...] + mod[2:3] * mix
    x1_ref[...] = x1
    h = _norm_mod(x1, g_ref[...], mod[3:4], mod[4:5])
    logits = jnp.dot(h.astype(BF16), wr_ref[...], preferred_element_type=F32)
    lane = lax.broadcasted_iota(I32, logits.shape, 1)
    logits = jnp.where(lane < N_EXPERTS, logits, -jnp.inf)
    m = jnp.max(logits, axis=-1, keepdims=True)
    e = jnp.exp(logits - m)
    aff = e / jnp.sum(e, axis=-1, keepdims=True)
    aff_ref[...] = aff[:, 0:N_EXPERTS]
    groups = h.shape[0] // SUBLANES
    half = D_MODEL // 2
    for c in range(HQ_TILES - 1):
        lo = _bf16_bits(h[:, c * LANES:(c + 1) * LANES])
        hi = _bf16_bits(h[:, half + c * LANES:half + (c + 1) * LANES])
        words = lax.shift_right_logical(lo, jnp.uint32(16)) | hi
        h_ref[:, c * SUBLANES:(c + 1) * SUBLANES, :] = (
            lax.bitcast_convert_type(words, F32).reshape(groups, SUBLANES, LANES))
    h_ref[:, (HQ_TILES - 1) * SUBLANES:, :] = aff.reshape(groups, SUBLANES, LANES)


def _out_projection(attn, pooled, x2, mod3, mod_row, g, w_out_bf, w_router):
    t = x2.shape[0]
    tm = TOKEN_TILE
    row = lambda i: (i, 0)
    return pl.pallas_call(
        _outproj_kernel,
        grid=(t // tm,),
        in_specs=[pl.BlockSpec((tm, ATTN_WIDTH), row),
                  pl.BlockSpec((tm, POOL_WIDTH), row),
                  pl.BlockSpec((tm, D_MODEL), row),
                  pl.BlockSpec((1, N_MOD, D_MODEL), lambda i: (mod_row(i), 0, 0)),
                  pl.BlockSpec((1, D_MODEL), lambda i: (0, 0)),
                  pl.BlockSpec((D_MODEL, D_MODEL), lambda i: (0, 0)),
                  pl.BlockSpec((D_MODEL, LANES), lambda i: (0, 0))],
        out_specs=[pl.BlockSpec((tm, D_MODEL), row),
                   pl.BlockSpec((tm // SUBLANES, HQ_TILES * SUBLANES, LANES), lambda i: (i, 0, 0)),
                   pl.BlockSpec((tm, N_EXPERTS), row)],
        out_shape=[jax.ShapeDtypeStruct((t, D_MODEL), F32),
                   jax.ShapeDtypeStruct((t // SUBLANES, HQ_TILES * SUBLANES, LANES), F32),
                   jax.ShapeDtypeStruct((t, N_EXPERTS), F32)],
        compiler_params=pltpu.CompilerParams(
            vmem_limit_bytes=_vmem_limit(2 * D_MODEL * D_MODEL * 2 + 24 * tm * D_MODEL * 4)),
        name="out_projection",
    )(attn, pooled, x2, mod3, g, w_out_bf, w_router)


def _route_kernel(a_ref, idx_ref, slot_ref, off_ref, *, cap):
    a = a_ref[...]
    n_e, n_c, _ = a.shape
    rows = n_e * n_c

    def bisect(it, thr):
        cand = thr | jnp.left_shift(jnp.int32(1), 30 - it)
        cand_f = lax.bitcast_convert_type(cand, F32)
        cnt = jnp.sum(jnp.sum((a >= cand_f).astype(F32), axis=2, keepdims=True),
                      axis=1, keepdims=True)
        return jnp.where(cnt >= cap, cand, thr)

    thr = lax.fori_loop(0, 31, bisect, jnp.zeros((n_e, 1, 1), I32))
    thr_f = lax.bitcast_convert_type(thr, F32)
    gt = (a > thr_f).astype(F32).reshape(rows, LANES)
    eq = (a == thr_f).astype(F32).reshape(rows, LANES)

    li = lax.broadcasted_iota(I32, (LANES, LANES), 0)
    lj = lax.broadcasted_iota(I32, (LANES, LANES), 1)
    upper_incl = (li <= lj).astype(BF16)
    ri = lax.broadcasted_iota(I32, (rows, rows), 0)
    rj = lax.broadcasted_iota(I32, (rows, rows), 1)
    same_expert = (ri // n_c) == (rj // n_c)
    before = (same_expert & (rj < ri)).astype(BF16)
    whole = same_expert.astype(BF16)

    def lane_bcast(col):
        return jnp.broadcast_to(col, (rows, LANES)).astype(BF16)

    def prefix(x):
        incl = jnp.dot(x.astype(BF16), upper_incl, preferred_element_type=F32)
        tot = incl[:, LANES - 1:LANES]
        off = jnp.dot(before, lane_bcast(tot), preferred_element_type=F32)
        return incl, tot, off

    n_gt = jnp.dot(whole, lane_bcast(jnp.sum(gt, axis=1, keepdims=True)),
                   preferred_element_type=F32)
    need = cap - n_gt
    incl_eq, _, off_eq = prefix(eq)
    rank_eq = off_eq + incl_eq - eq
    sel = jnp.where((eq > 0) & (rank_eq < need), 1.0, gt)
    incl, tot, off = prefix(sel)
    slot = off + incl - sel
    slot_ref[...] = jnp.where(sel > 0, slot, -1.0).astype(I32).reshape(n_e, n_c, LANES)
    off_ref[...] = off.astype(I32).reshape(n_e, n_c, LANES)

    s_lane = lax.broadcasted_iota(I32, (1, cap), 1).astype(F32)
    c_col = lax.broadcasted_iota(I32, (n_c, 1), 0).astype(F32)
    for e in range(n_e):
        r0 = e * n_c
        incl_e = incl[r0:r0 + n_c]
        off_e = off[r0:r0 + n_c, 0:1]
        tot_e = tot[r0:r0 + n_c]
        onehot = ((off_e <= s_lane) & (s_lane < off_e + tot_e)).astype(F32)
        counts = lax.dot_general(incl_e.astype(BF16), onehot.astype(BF16),
                                 (((0,), (0,)), ((), ())), preferred_element_type=F32)
        local = s_lane - jnp.sum(onehot * off_e, axis=0, keepdims=True)
        lane = jnp.sum((counts <= local).astype(F32), axis=0, keepdims=True)
        chunk = jnp.sum(onehot * c_col, axis=0, keepdims=True)
        idx_ref[e] = (chunk * LANES + lane).astype(I32)


def _routing(aff, cap):
    t = aff.shape[0]
    n_c = t // LANES
    a3 = aff.T.reshape(N_EXPERTS, n_c, LANES)
    return pl.pallas_call(
        functools.partial(_route_kernel, cap=cap),
        out_shape=[jax.ShapeDtypeStruct((N_EXPERTS, 1, cap), I32),
                   jax.ShapeDtypeStruct((N_EXPERTS, n_c, LANES), I32),
                   jax.ShapeDtypeStruct((N_EXPERTS, n_c, LANES), I32)],
        compiler_params=pltpu.CompilerParams(vmem_limit_bytes=_vmem_limit(48 << 20)),
        name="routing",
    )(a3)


def _gather_rows(table, row_ids):
    n_chunks = row_ids.shape[0]
    n_workers = SC_CORES * SC_SUBCORES
    per_worker = n_chunks // n_workers
    assert row_ids.shape[1] == GATHER_CHUNK and n_chunks % n_workers == 0
    mesh = plsc.VectorSubcoreMesh(core_axis_name="core", subcore_axis_name="subcore")

    @functools.partial(
        pl.kernel, mesh=mesh,
        out_type=jax.ShapeDtypeStruct((n_chunks * GATHER_CHUNK, LANES), F32),
        scratch_types=[pltpu.VMEM((GATHER_CHUNK,), I32),
                       pltpu.VMEM((GATHER_CHUNK, LANES), F32),
                       pltpu.SemaphoreType.DMA],
        name="gather_rows",
    )
    def gather(table_hbm, ids_hbm, out_hbm, ids_v, rows_v, sem):
        worker = lax.axis_index("subcore") * SC_CORES + lax.axis_index("core")

        @pl.loop(0, per_worker)
        def _(j):
            chunk = worker * per_worker + j
            pltpu.sync_copy(ids_hbm.at[chunk], ids_v)
            pltpu.async_copy(table_hbm.at[ids_v], rows_v, sem).wait()
            pltpu.sync_copy(
                rows_v, out_hbm.at[pl.ds(pl.multiple_of(chunk * GATHER_CHUNK, GATHER_CHUNK),
                                         GATHER_CHUNK)])

    return gather(table, row_ids)


def _packed_row_ids(idx, cap):
    tok = idx.reshape(N_EXPERTS, 1, cap // GATHER_CHUNK, GATHER_CHUNK)
    tile = jnp.arange(HQ_TILES, dtype=I32).reshape(1, HQ_TILES, 1, 1)
    ids = ((tok // SUBLANES) * HQ_TILES + tile) * SUBLANES + tok % SUBLANES
    return ids.reshape(-1, GATHER_CHUNK)


def _unpack_pair(words):
    u = lax.bitcast_convert_type(words, jnp.uint32)
    lo = lax.bitcast_convert_type(lax.shift_left(u, jnp.uint32(16)), F32)
    hi = lax.bitcast_convert_type(u & jnp.uint32(0xFFFF0000), F32)
    return lo, hi


def _ffn_kernel(*refs, caps, n_f, row_chunk):
    n_g = len(caps)
    x_refs = refs[:n_g]
    wg_ref, wu_ref, wd_ref = refs[n_g:n_g + 3]
    y_refs = refs[n_g + 3:2 * n_g + 3]
    acc_ref = refs[2 * n_g + 3]
    e = pl.program_id(0)
    f = pl.program_id(1)

    @pl.when((e == 0) & (f == 0))
    def _():
        acc_ref[...] = jnp.zeros_like(acc_ref)

    first = f == 0
    base = 0
    for x_ref, cap in zip(x_refs, caps):
        for r in range(cap // row_chunk):
            rs = slice(r * row_chunk, (r + 1) * row_chunk)
            acc_rows = slice(base + r * row_chunk, base + (r + 1) * row_chunk)
            pairs = [_unpack_pair(x_ref[0, c, rs, :]) for c in range(HQ_TILES - 1)]
            x = jnp.concatenate([p[0] for p in pairs] + [p[1] for p in pairs], axis=1)
            gate_act = jnp.dot(x, wg_ref[0], preferred_element_type=F32)
            up = jnp.dot(x, wu_ref[0], preferred_element_type=F32)
            hid = (gate_act * jax.nn.sigmoid(gate_act)) * up
            part = jnp.dot(hid, wd_ref[0], preferred_element_type=F32)
            acc_ref[acc_rows, :] = part + jnp.where(first, 0.0, acc_ref[acc_rows, :])
        base += cap

    @pl.when(f == n_f - 1)
    def _():
        base = 0
        for x_ref, y_ref, cap in zip(x_refs, y_refs, caps):
            aff = x_ref[0, HQ_TILES - 1]
            lane = lax.broadcasted_iota(I32, aff.shape, 1)
            gates = jnp.sum(jnp.where(lane == e, aff, 0.0), axis=1, keepdims=True)
            y_ref[0] = (acc_ref[base:base + cap, :] * gates).astype(BF16)
            base += cap


def _expert_ffn(xs_groups, w_gate, w_up, w_down):
    n_f = 4
    tf = D_EXPERT // n_f
    row_chunk = 512
    caps = tuple(xs.shape[2] for xs in xs_groups)
    rows = sum(caps)
    est = (2 * HQ_TILES * rows * LANES * 4 + rows * D_MODEL * (4 + 2 * 2) + 2 * 3 * D_MODEL * tf * 4
           + row_chunk * (3 * tf + 2 * D_MODEL) * 4)
    x_specs = [pl.BlockSpec((1, HQ_TILES, cap, LANES), lambda e, f: (e, 0, 0, 0)) for cap in caps]
    y_specs = [pl.BlockSpec((1, cap, D_MODEL), lambda e, f: (e, 0, 0)) for cap in caps]
    return pl.pallas_call(
        functools.partial(_ffn_kernel, caps=caps, n_f=n_f, row_chunk=row_chunk),
        grid=(N_EXPERTS, n_f),
        in_specs=x_specs + [pl.BlockSpec((1, D_MODEL, tf), lambda e, f: (e, 0, f)),
                            pl.BlockSpec((1, D_MODEL, tf), lambda e, f: (e, 0, f)),
                            pl.BlockSpec((1, tf, D_MODEL), lambda e, f: (e, f, 0))],
        out_specs=y_specs,
        out_shape=[jax.ShapeDtypeStruct((N_EXPERTS, cap, D_MODEL), BF16) for cap in caps],
        scratch_shapes=[pltpu.VMEM((rows, D_MODEL), F32)],
        compiler_params=pltpu.CompilerParams(
            dimension_semantics=("arbitrary", "arbitrary"),
            vmem_limit_bytes=_vmem_limit(est + (4 << 20))),
        name="expert_ffn",
    )(*xs_groups, w_gate, w_up, w_down)


def _combine_kernel(start_ref, nch_ref, x1_ref, mod_ref, g_ref, slot_ref, y_hbm, o_ref,
                    ybuf_ref, cols_ref, ffn_ref, sem, *, first_k, kblock):
    i = pl.program_id(0)
    tm = x1_ref.shape[0]
    cur = i % 2

    def chunk_copy(e, src_row, buf, dst_row):
        return pltpu.make_async_copy(
            y_hbm.at[e, pl.ds(pl.multiple_of(src_row, COMBINE_CHUNK), COMBINE_CHUNK)],
            ybuf_ref.at[buf, pl.ds(pl.multiple_of(dst_row, COMBINE_CHUNK), COMBINE_CHUNK)],
            sem.at[buf])

    def fetch(tile, buf):
        pos = jnp.int32(0)
        for e in range(N_EXPERTS):
            start = start_ref[tile, e]
            nch = nch_ref[tile, e]

            def issue(c, carry, e=e, start=start, pos=pos):
                chunk_copy(e, start + c * COMBINE_CHUNK, buf, pos + c * COMBINE_CHUNK).start()
                return carry
            lax.fori_loop(0, nch, issue, 0)
            pos = pos + nch * COMBINE_CHUNK

    @pl.when(i == 0)
    def _():
        ybuf_ref[...] = jnp.zeros_like(ybuf_ref)
        fetch(0, 0)

    @pl.when(i + 1 < pl.num_programs(0))
    def _():
        fetch(i + 1, 1 - cur)

    pos = jnp.int32(0)
    begins = []
    for e in range(N_EXPERTS):
        begins.append(pos)
        pos = pos + nch_ref[i, e] * COMBINE_CHUNK
    begins.append(pos)

    def drain(c, carry):
        chunk_copy(0, jnp.int32(0), cur, jnp.int32(0)).wait()
        return carry
    lax.fori_loop(0, pos // COMBINE_CHUNK, drain, 0)

    slots = slot_ref[...]
    for e in range(N_EXPERTS):
        s_e = slots[:, e:e + 1]
        col = jnp.where(s_e >= 0, s_e + (begins[e] - start_ref[i, e]), -1)
        cols_ref[e] = jnp.broadcast_to(col, (tm, LANES))

    lane = lax.broadcasted_iota(I32, (tm, LANES), 1)

    def selection(k0, width):
        halves = []
        for h in range(width // LANES):
            c0 = k0 + h * LANES
            target = lane + c0
            e_lo = jnp.int32(0)
            e_hi = jnp.int32(0)
            for e in range(N_EXPERTS):
                e_lo = e_lo + (begins[e + 1] <= c0).astype(I32)
                e_hi = e_hi + (begins[e] < c0 + LANES).astype(I32)

            def mark(e, hit, target=target):
                return jnp.where(cols_ref[e] == target, 1.0, hit)
            hit = lax.fori_loop(e_lo, e_hi, mark, jnp.zeros((tm, LANES), F32))
            halves.append(hit.astype(BF16))
        return jnp.concatenate(halves, axis=1)

    def block(k0, width):
        return jnp.dot(selection(k0, width), ybuf_ref[cur, pl.ds(k0, width), :],
                       preferred_element_type=F32)

    ffn_ref[...] = block(0, first_k)

    def kstep(kb, carry):
        ffn_ref[...] += block(pl.multiple_of(kb * kblock, kblock), kblock)
        return carry
    lax.fori_loop(first_k // kblock, (pos + kblock - 1) // kblock, kstep, 0)

    out = x1_ref[...] + mod_ref[0][5:6] * ffn_ref[...]
    ms = jnp.mean(out * out, axis=-1, keepdims=True)
    o_ref[...] = (out * lax.rsqrt(ms + EPS)) * g_ref[...]


def _combine(x1, mod3, mod_row, g_final, slot_t, tile_start, tile_nch, y):
    t = x1.shape[0]
    tm = TOKEN_TILE
    first_k = 2 * tm + 2 * N_EXPERTS * COMBINE_CHUNK
    kblock = 512
    max_rows = N_EXPERTS * (tm + 2 * COMBINE_CHUNK)
    max_rows = -(-max_rows // kblock) * kblock
    row = lambda i, *_: (i, 0)
    grid_spec = pltpu.PrefetchScalarGridSpec(
        num_scalar_prefetch=2,
        grid=(t // tm,),
        in_specs=[pl.BlockSpec((tm, D_MODEL), row),
                  pl.BlockSpec((1, N_MOD, D_MODEL), lambda i, *_: (mod_row(i), 0, 0)),
                  pl.BlockSpec((1, D_MODEL), lambda i, *_: (0, 0)),
                  pl.BlockSpec((tm, N_EXPERTS), row),
                  pl.BlockSpec(memory_space=pl.ANY)],
        out_specs=pl.BlockSpec((tm, D_MODEL), row),
        scratch_shapes=[pltpu.VMEM((2, max_rows, D_MODEL), BF16),
                        pltpu.VMEM((N_EXPERTS, tm, LANES), I32),
                        pltpu.VMEM((tm, D_MODEL), F32),
                        pltpu.SemaphoreType.DMA((2,))],
    )
    return pl.pallas_call(
        functools.partial(_combine_kernel, first_k=first_k, kblock=kblock),
        grid_spec=grid_spec,
        out_shape=jax.ShapeDtypeStruct((t, D_MODEL), F32),
        compiler_params=pltpu.CompilerParams(
            dimension_semantics=("arbitrary",),
            vmem_limit_bytes=_vmem_limit(2 * max_rows * D_MODEL * 2 + 16 * tm * D_MODEL * 4)),
        name="combine",
    )(tile_start, tile_nch, x1, mod3, g_final, slot_t, y)


def _rope_tables(n):
    rows = n // GRID_W
    row = jnp.repeat(jnp.arange(rows, dtype=F32), GRID_W)
    col = jnp.tile(jnp.arange(GRID_W, dtype=F32), rows)
    inv = ROPE_THETA ** (-jnp.arange(ROPE_FREQS, dtype=F32) / ROPE_FREQS)
    ang_r = row[:, None] * inv
    ang_c = col[:, None] * inv
    zero = jnp.zeros_like(ang_r)
    cos = jnp.concatenate([jnp.cos(ang_r)] * 2 + [jnp.cos(ang_c)] * 2, axis=1)
    sin_a = jnp.concatenate([-jnp.sin(ang_r), zero, -jnp.sin(ang_c), zero], axis=1)
    sin_b = jnp.concatenate([zero, jnp.sin(ang_r), zero, jnp.sin(ang_c)], axis=1)
    return cos, sin_a, sin_b


def _token_group(x, mod3, mod_row, seq, weights, rope_tabs, ctx_kv):
    (norm_mix, w_in_bf, sink, pool_w, pool_scale, w_out_bf, norm_ffn, w_router,
     w_gate, w_up, w_down, norm_final) = weights
    b = x.shape[0]
    t = b * seq
    x2 = x.reshape(t, D_MODEL)
    q, k, v, p = _in_projection(x2, mod3, mod_row, norm_mix, w_in_bf, rope_tabs)
    if ctx_kv is None:
        attn = _context_attention(q, k, v, sink, seq)
    else:
        attn = _latent_attention(q, k, v, ctx_kv[0], ctx_kv[1], sink, seq)
    pooled = _pool_mixer(p, pool_w, pool_scale, seq)
    x1, h_packed, aff = _out_projection(attn, pooled, x2, mod3, mod_row, norm_ffn, w_out_bf,
                                        w_router)

    cap = EC_FACTOR * t // N_EXPERTS
    idx, slot3, off3 = _routing(aff, cap)
    xs = _gather_rows(h_packed.reshape(-1, LANES), _packed_row_ids(idx, cap))
    xs = xs.reshape(N_EXPERTS, HQ_TILES, cap, LANES)

    chunks_per_tile = TOKEN_TILE // LANES
    tile_off = off3[:, ::chunks_per_tile, 0]
    tile_end = jnp.concatenate([tile_off[:, 1:], jnp.full((N_EXPERTS, 1), cap, I32)], axis=1)
    tile_start = (tile_off // COMBINE_CHUNK) * COMBINE_CHUNK
    tile_nch = jnp.where(tile_end > tile_off,
                         (tile_end - tile_start + COMBINE_CHUNK - 1) // COMBINE_CHUNK, 0)
    slot_t = slot3.reshape(N_EXPERTS, t).T

    def finish(y):
        out = _combine(x1, mod3, mod_row, norm_final, slot_t, tile_start.T, tile_nch.T, y)
        return out.reshape(b, seq, D_MODEL)
    return xs, finish, k, v


def kernel(x_prompt, x_sample, c, cache_k, cache_v, c_ctx, w_ada, b_ada, norm_mix, w_in,
           sink_logits, pool_w, pool_scale, w_out, norm_ffn, w_router, w_gate, w_up, w_down,
           norm_final):
    n_b, seq, _ = x_prompt.shape
    n_db, n_lat, _ = x_sample.shape
    assert 1 + n_db <= MOD_ROWS and seq == TOKEN_TILE and n_lat % TOKEN_TILE == 0

    cond = jnp.concatenate(
        [c_ctx[None, :], c, jnp.zeros((MOD_ROWS - 1 - n_db, D_MODEL), F32)], axis=0)
    mod3 = _modulation(cond, w_ada[0], b_ada[0]).reshape(MOD_ROWS, N_MOD, D_MODEL)

    w_router_bf = jnp.pad(w_router[0], ((0, 0), (0, LANES - N_EXPERTS))).astype(BF16)
    weights = (norm_mix[0][None, :], w_in[0].astype(BF16), sink_logits[0], pool_w[0],
               pool_scale[0][None, :], w_out[0].astype(BF16), norm_ffn[0][None, :], w_router_bf,
               w_gate.reshape(w_gate.shape[1:]), w_up.reshape(w_up.shape[1:]),
               w_down.reshape(w_down.shape[1:]), norm_final[None, :])

    xs_p, finish_p, k_p, v_p = _token_group(
        x_prompt, mod3, lambda i: 0, seq, weights, None, None)

    tiles_per_seq = n_lat // TOKEN_TILE
    past = cache_k.shape[2]
    ck = cache_k[:, 0].reshape(n_db, past, KV_WIDTH)
    cv = cache_v[:, 0].reshape(n_db, past, KV_WIDTH)
    xs_l, finish_l, _, _ = _token_group(
        x_sample, mod3, lambda i: 1 + i // tiles_per_seq, n_lat, weights,
        _rope_tables(n_lat), (ck, cv))

    y_p, y_l = _expert_ffn([xs_p, xs_l], weights[8], weights[9], weights[10])
    y_prompt = finish_p(y_p)
    y_sample = finish_l(y_l)

    state_k = k_p.reshape(n_b, 1, seq, N_KV_HEADS, HEAD_DIM)
    state_v = v_p.reshape(n_b, 1, seq, N_KV_HEADS, HEAD_DIM)
    return (y_prompt, y_sample, state_k, state_v)
```

```python
import functools

import jax
import jax.numpy as jnp
import numpy as np
from jax import lax
from jax.experimental import pallas as pl
from jax.experimental.pallas import tpu as pltpu
from jax.experimental.pallas import tpu_sc as plsc

F32 = jnp.float32
BF16 = jnp.bfloat16
I32 = jnp.int32

D_MODEL = 2048
N_HEADS = 8
N_KV_HEADS = 2
HEAD_DIM = 128
Q_PER_KV = N_HEADS // N_KV_HEADS
ATTN_WIDTH = N_HEADS * HEAD_DIM
KV_WIDTH = N_KV_HEADS * HEAD_DIM
POOL_WIDTH = D_MODEL - ATTN_WIDTH
POOL_SIZES = (2, 4, 8, 16)
POOL_GROUP = POOL_WIDTH // len(POOL_SIZES)
IN_WIDTH = ATTN_WIDTH + 2 * KV_WIDTH + POOL_WIDTH
WINDOW = 128
BLOCK = 128
GRID_W = 64
ROPE_THETA = 10000.0
ROPE_FREQS = HEAD_DIM // 4
N_EXPERTS = 16
EC_FACTOR = 2
D_EXPERT = 1024
N_MOD = 6
EPS = 1e-6
NEG = -1e30

LANES = 128
SUBLANES = 8
BF16_ROWS = 16
VMEM_CAP = 64 * 1024 * 1024
SC_CORES = 2
SC_SUBCORES = 16

MOD_ROWS = 8
TOKEN_TILE = 256
HQ_TILES = D_MODEL // 2 // LANES + 1
GATHER_CHUNK = 128
POOL_HALO = 8
COMBINE_CHUNK = BF16_ROWS


def _vmem_limit(nbytes):
    return int(min(VMEM_CAP - (4 << 20), max(nbytes, 16 << 20)))


def _mod_kernel(c_ref, w_ref, b_ref, o_ref):
    c = c_ref[...]
    s = c * jax.nn.sigmoid(c)
    o_ref[...] = jnp.dot(s.astype(BF16), w_ref[...].astype(BF16),
                         preferred_element_type=F32) + b_ref[...]


def _modulation(cond, w_ada, b_ada):
    n = w_ada.shape[1]
    tn = 1024
    return pl.pallas_call(
        _mod_kernel,
        grid=(n // tn,),
        in_specs=[pl.BlockSpec((MOD_ROWS, D_MODEL), lambda j: (0, 0)),
                  pl.BlockSpec((D_MODEL, tn), lambda j: (0, j)),
                  pl.BlockSpec((1, tn), lambda j: (0, j))],
        out_specs=pl.BlockSpec((MOD_ROWS, tn), lambda j: (0, j)),
        out_shape=jax.ShapeDtypeStruct((MOD_ROWS, n), F32),
        compiler_params=pltpu.CompilerParams(
            vmem_limit_bytes=_vmem_limit(3 * D_MODEL * tn * 4)),
        name="modulation",
    )(cond, w_ada, b_ada.reshape(1, n))


def _norm_mod(x, g, shift, scale):
    ms = jnp.mean(x * x, axis=-1, keepdims=True)
    y = x * lax.rsqrt(ms + EPS)
    return (y * g) * (1.0 + scale) + shift


def _inproj_kernel(*refs, rope):
    if rope:
        x_ref, mod_ref, g_ref, w_ref, cos_ref, sa_ref, sb_ref, q_ref, k_ref, v_ref, p_ref = refs
    else:
        x_ref, mod_ref, g_ref, w_ref, q_ref, k_ref, v_ref, p_ref = refs
    mod = mod_ref[0]
    h = _norm_mod(x_ref[...], g_ref[...], mod[0:1], mod[1:2])
    u = jnp.dot(h.astype(BF16), w_ref[...], preferred_element_type=F32)

    def rot(xh):
        return (xh * cos_ref[...] + pltpu.roll(xh, LANES - ROPE_FREQS, 1) * sa_ref[...]
                + pltpu.roll(xh, ROPE_FREQS, 1) * sb_ref[...])

    for hd in range(N_HEADS):
        xh = u[:, hd * HEAD_DIM:(hd + 1) * HEAD_DIM]
        q_ref[:, hd * HEAD_DIM:(hd + 1) * HEAD_DIM] = (rot(xh) if rope else xh).astype(BF16)
    for hd in range(N_KV_HEADS):
        lo = ATTN_WIDTH + hd * HEAD_DIM
        xh = u[:, lo:lo + HEAD_DIM]
        k_ref[:, hd * HEAD_DIM:(hd + 1) * HEAD_DIM] = rot(xh) if rope else xh
    v_ref[...] = u[:, ATTN_WIDTH + KV_WIDTH:ATTN_WIDTH + 2 * KV_WIDTH]
    p_ref[...] = u[:, ATTN_WIDTH + 2 * KV_WIDTH:]


def _in_projection(x2, mod3, mod_row, g, w_in_bf, rope_tabs):
    t = x2.shape[0]
    tm = TOKEN_TILE
    rope = rope_tabs is not None
    row = lambda i: (i, 0)
    in_specs = [pl.BlockSpec((tm, D_MODEL), row),
                pl.BlockSpec((1, N_MOD, D_MODEL), lambda i: (mod_row(i), 0, 0)),
                pl.BlockSpec((1, D_MODEL), lambda i: (0, 0)),
                pl.BlockSpec((D_MODEL, IN_WIDTH), lambda i: (0, 0))]
    args = [x2, mod3, g, w_in_bf]
    if rope:
        n_seq = rope_tabs[0].shape[0]
        seq_blocks = n_seq // tm
        for tab in rope_tabs:
            in_specs.append(pl.BlockSpec((tm, HEAD_DIM), lambda i: (i % seq_blocks, 0)))
            args.append(tab)
    return pl.pallas_call(
        functools.partial(_inproj_kernel, rope=rope),
        grid=(t // tm,),
        in_specs=in_specs,
        out_specs=[pl.BlockSpec((tm, ATTN_WIDTH), row),
                   pl.BlockSpec((tm, KV_WIDTH), row),
                   pl.BlockSpec((tm, KV_WIDTH), row),
                   pl.BlockSpec((tm, POOL_WIDTH), row)],
        out_shape=[jax.ShapeDtypeStruct((t, ATTN_WIDTH), BF16),
                   jax.ShapeDtypeStruct((t, KV_WIDTH), F32),
                   jax.ShapeDtypeStruct((t, KV_WIDTH), F32),
                   jax.ShapeDtypeStruct((t, POOL_WIDTH), F32)],
        compiler_params=pltpu.CompilerParams(
            vmem_limit_bytes=_vmem_limit(2 * D_MODEL * IN_WIDTH * 2 + 24 * tm * D_MODEL * 4)),
        name="in_projection",
    )(*args)


def _softmax_pv(s_list, v_list, sink_col):
    m = sink_col
    for s in s_list:
        m = jnp.maximum(m, jnp.max(s, axis=-1, keepdims=True))
    denom = jnp.exp(sink_col - m)
    out = None
    for s, v in zip(s_list, v_list):
        e = jnp.exp(s - m)
        denom = denom + jnp.sum(e, axis=-1, keepdims=True)
        o = jnp.dot(e.astype(BF16), v, preferred_element_type=F32)
        out = o if out is None else out + o
    return out * (1.0 / denom)


def _stack_heads(q, kv):
    return jnp.concatenate(
        [q[:, (kv * Q_PER_KV + g) * HEAD_DIM:(kv * Q_PER_KV + g + 1) * HEAD_DIM]
         for g in range(Q_PER_KV)], axis=0)


def _sink_column(sink_ref, kv, rows):
    r = lax.broadcasted_iota(I32, (Q_PER_KV * rows, 1), 0)
    col = jnp.zeros((Q_PER_KV * rows, 1), F32)
    for g in range(Q_PER_KV):
        col = jnp.where((r >= g * rows) & (r < (g + 1) * rows), sink_ref[kv * Q_PER_KV + g], col)
    return col


def _qk(q, k):
    return lax.dot_general(q, k, (((1,), (1,)), ((), ())), preferred_element_type=F32)


def _ctx_attn_kernel(sink_ref, q_ref, k_ref, v_ref, o_ref):
    rows = q_ref.shape[0]
    scale = HEAD_DIM ** -0.5
    q = q_ref[...]
    for kv in range(N_KV_HEADS):
        kh = k_ref[:, kv * HEAD_DIM:(kv + 1) * HEAD_DIM].astype(BF16)
        vh = v_ref[:, kv * HEAD_DIM:(kv + 1) * HEAD_DIM].astype(BF16)
        qs = _stack_heads(q, kv)
        s = _qk(qs, kh) * scale
        o = _softmax_pv([s], [vh], _sink_column(sink_ref, kv, rows))
        for g in range(Q_PER_KV):
            hd = kv * Q_PER_KV + g
            o_ref[:, hd * HEAD_DIM:(hd + 1) * HEAD_DIM] = o[g * rows:(g + 1) * rows].astype(BF16)


def _context_attention(q, k, v, sink, seq):
    t = q.shape[0]
    row = lambda b: (b, 0)
    return pl.pallas_call(
        _ctx_attn_kernel,
        grid=(t // seq,),
        in_specs=[pl.BlockSpec(memory_space=pltpu.SMEM),
                  pl.BlockSpec((seq, ATTN_WIDTH), row),
                  pl.BlockSpec((seq, KV_WIDTH), row),
                  pl.BlockSpec((seq, KV_WIDTH), row)],
        out_specs=pl.BlockSpec((seq, ATTN_WIDTH), row),
        out_shape=jax.ShapeDtypeStruct((t, ATTN_WIDTH), BF16),
        name="context_attention",
    )(sink, q, k, v)


def _lat_attn_kernel(sink_ref, q_ref, k_ref, v_ref, ck_ref, cv_ref, o_ref, *, n_seq):
    i = pl.program_id(1)
    scale = HEAD_DIM ** -0.5
    band = 3 * BLOCK
    start = pl.multiple_of(jnp.clip((i - 1) * BLOCK, 0, n_seq - band), BLOCK)
    rows = Q_PER_KV * BLOCK
    qpos = i * BLOCK + lax.broadcasted_iota(I32, (rows, band), 0) % BLOCK
    kpos = start + lax.broadcasted_iota(I32, (rows, band), 1)
    mask = jnp.abs(kpos - qpos) <= WINDOW
    q = q_ref[...]
    for kv in range(N_KV_HEADS):
        cols = slice(kv * HEAD_DIM, (kv + 1) * HEAD_DIM)
        kb = k_ref[pl.ds(start, band), cols].astype(BF16)
        vb = v_ref[pl.ds(start, band), cols].astype(BF16)
        ck = ck_ref[0, :, cols].astype(BF16)
        cv = cv_ref[0, :, cols].astype(BF16)
        qs = _stack_heads(q, kv)
        s_loc = jnp.where(mask, _qk(qs, kb) * scale, NEG)
        s_ctx = _qk(qs, ck) * scale
        o = _softmax_pv([s_loc, s_ctx], [vb, cv], _sink_column(sink_ref, kv, BLOCK))
        for g in range(Q_PER_KV):
            hd = kv * Q_PER_KV + g
            o_ref[:, hd * HEAD_DIM:(hd + 1) * HEAD_DIM] = o[g * BLOCK:(g + 1) * BLOCK].astype(BF16)


def _latent_attention(q, k, v, ck, cv, sink, n_seq):
    t = q.shape[0]
    nb = n_seq // BLOCK
    past = ck.shape[1]
    return pl.pallas_call(
        functools.partial(_lat_attn_kernel, n_seq=n_seq),
        grid=(t // n_seq, nb),
        in_specs=[pl.BlockSpec(memory_space=pltpu.SMEM),
                  pl.BlockSpec((BLOCK, ATTN_WIDTH), lambda b, i: (b * nb + i, 0)),
                  pl.BlockSpec((n_seq, KV_WIDTH), lambda b, i: (b, 0)),
                  pl.BlockSpec((n_seq, KV_WIDTH), lambda b, i: (b, 0)),
                  pl.BlockSpec((1, past, KV_WIDTH), lambda b, i: (b, 0, 0)),
                  pl.BlockSpec((1, past, KV_WIDTH), lambda b, i: (b, 0, 0))],
        out_specs=pl.BlockSpec((BLOCK, ATTN_WIDTH), lambda b, i: (b * nb + i, 0)),
        out_shape=jax.ShapeDtypeStruct((t, ATTN_WIDTH), BF16),
        name="latent_attention",
    )(sink, q, k, v, ck, cv)


def _pool_kernel(p_ref, w_ref, s_ref, o_ref, pad_ref):
    n = p_ref.shape[0]
    rows = n + 2 * POOL_HALO
    t = lax.broadcasted_iota(I32, (n, 1), 0)
    zeros = jnp.zeros((POOL_HALO, POOL_GROUP), F32)
    pad_ref[0:POOL_HALO, :] = zeros
    pad_ref[POOL_HALO + n:rows, :] = zeros
    for g, w in enumerate(POOL_SIZES):
        cols = slice(g * POOL_GROUP, (g + 1) * POOL_GROUP)
        pg = p_ref[:, cols]
        pad_ref[POOL_HALO:POOL_HALO + n, :] = pg
        x = pad_ref[...]
        acc = x + pltpu.roll(x, 1, 0)
        step = 1
        while 2 * step < w:
            acc = pltpu.roll(acc, step, 0) + pltpu.roll(acc, rows - step, 0)
            step *= 2
        wsum = acc[POOL_HALO:POOL_HALO + n]
        lo = jnp.maximum(t - w // 2, 0)
        hi = jnp.minimum(t + w - w // 2, n)
        inv_cnt = 1.0 / (hi - lo).astype(F32)
        mixed = wsum * inv_cnt - pg
        y = jnp.dot(mixed.astype(BF16), w_ref[g].astype(BF16), preferred_element_type=F32)
        o_ref[:, cols] = (y * s_ref[:, cols]).astype(BF16)


def _pool_mixer(p, pool_w, pool_scale, seq):
    t = p.shape[0]
    row = lambda b: (b, 0)
    return pl.pallas_call(
        _pool_kernel,
        grid=(t // seq,),
        in_specs=[pl.BlockSpec((seq, POOL_WIDTH), row),
                  pl.BlockSpec((len(POOL_SIZES), POOL_GROUP, POOL_GROUP), lambda b: (0, 0, 0)),
                  pl.BlockSpec((1, POOL_WIDTH), lambda b: (0, 0))],
        out_specs=pl.BlockSpec((seq, POOL_WIDTH), row),
        out_shape=jax.ShapeDtypeStruct((t, POOL_WIDTH), BF16),
        scratch_shapes=[pltpu.VMEM((seq + 2 * POOL_HALO, POOL_GROUP), F32)],
        compiler_params=pltpu.CompilerParams(
            vmem_limit_bytes=_vmem_limit(8 * seq * POOL_WIDTH * 4)),
        name="pool_mixer",
    )(p, pool_w, pool_scale)


def _pack_pair(lo, hi):
    return lax.bitcast_convert_type(pltpu.pack_elementwise([lo, hi], packed_dtype=BF16), I32)


def _unpack_pair(words):
    lo = pltpu.unpack_elementwise(words, index=0, packed_dtype=BF16, unpacked_dtype=F32)
    hi = pltpu.unpack_elementwise(words, index=1, packed_dtype=BF16, unpacked_dtype=F32)
    return lo, hi


def _outproj_kernel(a_ref, p_ref, x_ref, mod_ref, g_ref, wo_ref, wr_ref, x1_ref, h_ref, aff_ref):
    mod = mod_ref[0]
    mix = (jnp.dot(a_ref[...], wo_ref[0:ATTN_WIDTH, :], preferred_element_type=F32)
           + jnp.dot(p_ref[...], wo_ref[ATTN_WIDTH:D_MODEL, :], preferred_element_type=F32))
    x1 = x_ref[...] + mod[2:3] * mix
    x1_ref[...] = x1
    h = _norm_mod(x1, g_ref[...], mod[3:4], mod[4:5])
    logits = jnp.dot(h.astype(BF16), wr_ref[...], preferred_element_type=F32)
    lane = lax.broadcasted_iota(I32, logits.shape, 1)
    logits = jnp.where(lane < N_EXPERTS, logits, -jnp.inf)
    m = jnp.max(logits, axis=-1, keepdims=True)
    e = jnp.exp(logits - m)
    aff = e / jnp.sum(e, axis=-1, keepdims=True)
    aff_ref[...] = aff[:, 0:N_EXPERTS]
    groups = h.shape[0] // SUBLANES
    half = D_MODEL // 2
    for c in range(HQ_TILES - 1):
        words = _pack_pair(h[:, c * LANES:(c + 1) * LANES],
                           h[:, half + c * LANES:half + (c + 1) * LANES])
        h_ref[:, c * SUBLANES:(c + 1) * SUBLANES, :] = (
            lax.bitcast_convert_type(words, I32).reshape(groups, SUBLANES, LANES))
    h_ref[:, (HQ_TILES - 1) * SUBLANES:, :] = (
        lax.bitcast_convert_type(aff, I32).reshape(groups, SUBLANES, LANES))


def _out_projection(attn, pooled, x2, mod3, mod_row, g, w_out_bf, w_router):
    t = x2.shape[0]
    tm = TOKEN_TILE
    row = lambda i: (i, 0)
    return pl.pallas_call(
        _outproj_kernel,
        grid=(t // tm,),
        in_specs=[pl.BlockSpec((tm, ATTN_WIDTH), row),
                  pl.BlockSpec((tm, POOL_WIDTH), row),
                  pl.BlockSpec((tm, D_MODEL), row),
                  pl.BlockSpec((1, N_MOD, D_MODEL), lambda i: (mod_row(i), 0, 0)),
                  pl.BlockSpec((1, D_MODEL), lambda i: (0, 0)),
                  pl.BlockSpec((D_MODEL, D_MODEL), lambda i: (0, 0)),
                  pl.BlockSpec((D_MODEL, LANES), lambda i: (0, 0))],
        out_specs=[pl.BlockSpec((tm, D_MODEL), row),
                   pl.BlockSpec((tm // SUBLANES, HQ_TILES * SUBLANES, LANES), lambda i: (i, 0, 0)),
                   pl.BlockSpec((tm, N_EXPERTS), row)],
        out_shape=[jax.ShapeDtypeStruct((t, D_MODEL), F32),
                   jax.ShapeDtypeStruct((t // SUBLANES, HQ_TILES * SUBLANES, LANES), I32),
                   jax.ShapeDtypeStruct((t, N_EXPERTS), F32)],
        compiler_params=pltpu.CompilerParams(
            vmem_limit_bytes=_vmem_limit(2 * D_MODEL * D_MODEL * 2 + 24 * tm * D_MODEL * 4)),
        name="out_projection",
    )(attn, pooled, x2, mod3, g, w_out_bf, w_router)


def _route_kernel(a_ref, idx_ref, slot_ref, off_ref, *, cap):
    a = a_ref[...]
    n_e, n_c, _ = a.shape
    rows = n_e * n_c

    def bisect(it, thr):
        cand = thr | jnp.left_shift(jnp.int32(1), 30 - it)
        cand_f = lax.bitcast_convert_type(cand, F32)
        cnt = jnp.sum(jnp.sum((a >= cand_f).astype(F32), axis=2, keepdims=True),
                      axis=1, keepdims=True)
        return jnp.where(cnt >= cap, cand, thr)

    thr = lax.fori_loop(0, 31, bisect, jnp.zeros((n_e, 1, 1), I32))
    thr_f = lax.bitcast_convert_type(thr, F32)
    gt = (a > thr_f).astype(F32).reshape(rows, LANES)
    eq = (a == thr_f).astype(F32).reshape(rows, LANES)

    li = lax.broadcasted_iota(I32, (LANES, LANES), 0)
    lj = lax.broadcasted_iota(I32, (LANES, LANES), 1)
    upper_incl = (li <= lj).astype(BF16)
    ri = lax.broadcasted_iota(I32, (rows, rows), 0)
    rj = lax.broadcasted_iota(I32, (rows, rows), 1)
    same_expert = (ri // n_c) == (rj // n_c)
    before = (same_expert & (rj < ri)).astype(BF16)
    whole = same_expert.astype(BF16)

    def lane_bcast(col):
        return jnp.broadcast_to(col, (rows, LANES)).astype(BF16)

    def prefix(x):
        incl = jnp.dot(x.astype(BF16), upper_incl, preferred_element_type=F32)
        tot = incl[:, LANES - 1:LANES]
        off = jnp.dot(before, lane_bcast(tot), preferred_element_type=F32)
        return incl, tot, off

    n_gt = jnp.dot(whole, lane_bcast(jnp.sum(gt, axis=1, keepdims=True)),
                   preferred_element_type=F32)
    need = cap - n_gt
    incl_eq, _, off_eq = prefix(eq)
    rank_eq = off_eq + incl_eq - eq
    sel = jnp.where((eq > 0) & (rank_eq < need), 1.0, gt)
    incl, tot, off = prefix(sel)
    slot = off + incl - sel
    slot_ref[...] = jnp.where(sel > 0, slot, -1.0).astype(I32).reshape(n_e, n_c, LANES)
    off_ref[...] = off.astype(I32).reshape(n_e, n_c, LANES)

    s_lane = lax.broadcasted_iota(I32, (1, cap), 1).astype(F32)
    c_col = lax.broadcasted_iota(I32, (n_c, 1), 0).astype(F32)
    for e in range(n_e):
        r0 = e * n_c
        incl_e = incl[r0:r0 + n_c]
        off_e = off[r0:r0 + n_c, 0:1]
        tot_e = tot[r0:r0 + n_c]
        onehot = ((off_e <= s_lane) & (s_lane < off_e + tot_e)).astype(F32)
        counts = lax.dot_general(incl_e.astype(BF16), onehot.astype(BF16),
                                 (((0,), (0,)), ((), ())), preferred_element_type=F32)
        local = s_lane - jnp.sum(onehot * off_e, axis=0, keepdims=True)
        lane = jnp.sum((counts <= local).astype(F32), axis=0, keepdims=True)
        chunk = jnp.sum(onehot * c_col, axis=0, keepdims=True)
        idx_ref[e] = (chunk * LANES + lane).astype(I32)


def _routing(aff, cap):
    t = aff.shape[0]
    n_c = t // LANES
    a3 = aff.T.reshape(N_EXPERTS, n_c, LANES)
    return pl.pallas_call(
        functools.partial(_route_kernel, cap=cap),
        out_shape=[jax.ShapeDtypeStruct((N_EXPERTS, 1, cap), I32),
                   jax.ShapeDtypeStruct((N_EXPERTS, n_c, LANES), I32),
                   jax.ShapeDtypeStruct((N_EXPERTS, n_c, LANES), I32)],
        compiler_params=pltpu.CompilerParams(vmem_limit_bytes=_vmem_limit(48 << 20)),
        name="routing",
    )(a3)


def _gather_rows(table, row_ids):
    n_chunks = row_ids.shape[0]
    n_workers = SC_CORES * SC_SUBCORES
    per_worker = n_chunks // n_workers
    assert row_ids.shape[1] == GATHER_CHUNK and n_chunks % n_workers == 0
    mesh = plsc.VectorSubcoreMesh(core_axis_name="core", subcore_axis_name="subcore")

    @functools.partial(
        pl.kernel, mesh=mesh,
        out_type=jax.ShapeDtypeStruct((n_chunks * GATHER_CHUNK, LANES), I32),
        scratch_types=[pltpu.VMEM((GATHER_CHUNK,), I32),
                       pltpu.VMEM((GATHER_CHUNK, LANES), I32),
                       pltpu.SemaphoreType.DMA],
        name="gather_rows",
    )
    def gather(table_hbm, ids_hbm, out_hbm, ids_v, rows_v, sem):
        worker = lax.axis_index("subcore") * SC_CORES + lax.axis_index("core")

        @pl.loop(0, per_worker)
        def _(j):
            chunk = worker * per_worker + j
            pltpu.sync_copy(ids_hbm.at[chunk], ids_v)
            pltpu.async_copy(table_hbm.at[ids_v], rows_v, sem).wait()
            pltpu.sync_copy(
                rows_v, out_hbm.at[pl.ds(pl.multiple_of(chunk * GATHER_CHUNK, GATHER_CHUNK),
                                         GATHER_CHUNK)])

    return gather(table, row_ids)


def _packed_row_ids(idx, cap):
    tok = idx.reshape(N_EXPERTS, 1, cap // GATHER_CHUNK, GATHER_CHUNK)
    tile = jnp.arange(HQ_TILES, dtype=I32).reshape(1, HQ_TILES, 1, 1)
    ids = ((tok // SUBLANES) * HQ_TILES + tile) * SUBLANES + tok % SUBLANES
    return ids.reshape(-1, GATHER_CHUNK)


def _ffn_kernel(*refs, caps, n_f, row_chunk):
    n_g = len(caps)
    x_refs = refs[:n_g]
    wg_ref, wu_ref, wd_ref = refs[n_g:n_g + 3]
    y_refs = refs[n_g + 3:2 * n_g + 3]
    acc_ref = refs[2 * n_g + 3]
    e = pl.program_id(0)
    f = pl.program_id(1)

    @pl.when((e == 0) & (f == 0))
    def _():
        acc_ref[...] = jnp.zeros_like(acc_ref)

    first = f == 0
    base = 0
    for x_ref, cap in zip(x_refs, caps):
        for r in range(cap // row_chunk):
            rs = slice(r * row_chunk, (r + 1) * row_chunk)
            acc_rows = slice(base + r * row_chunk, base + (r + 1) * row_chunk)
            pairs = [_unpack_pair(x_ref[0, c, rs, :]) for c in range(HQ_TILES - 1)]
            x = jnp.concatenate([p[0] for p in pairs] + [p[1] for p in pairs], axis=1)
            gate_act = jnp.dot(x, wg_ref[0], preferred_element_type=F32)
            up = jnp.dot(x, wu_ref[0], preferred_element_type=F32)
            hid = (gate_act * jax.nn.sigmoid(gate_act)) * up
            part = jnp.dot(hid, wd_ref[0], preferred_element_type=F32)
            acc_ref[acc_rows, :] = part + jnp.where(first, 0.0, acc_ref[acc_rows, :])
        base += cap

    @pl.when(f == n_f - 1)
    def _():
        base = 0
        for x_ref, y_ref, cap in zip(x_refs, y_refs, caps):
            aff = lax.bitcast_convert_type(x_ref[0, HQ_TILES - 1], F32)
            lane = lax.broadcasted_iota(I32, aff.shape, 1)
            gates = jnp.sum(jnp.where(lane == e, aff, 0.0), axis=1, keepdims=True)
            y = acc_ref[base:base + cap, :] * gates
            y_ref[0, 0:cap, :] = _pack_pair(y[:, 0:D_MODEL // 2], y[:, D_MODEL // 2:])
            y_ref[0, cap:, :] = jnp.zeros((COMBINE_CHUNK, D_MODEL // 2), I32)
            base += cap


def _expert_ffn(xs_groups, w_gate, w_up, w_down):
    n_f = 4
    tf = D_EXPERT // n_f
    row_chunk = 512
    caps = tuple(xs.shape[2] for xs in xs_groups)
    rows = sum(caps)
    est = (2 * HQ_TILES * rows * LANES * 4 + rows * D_MODEL * (4 + 2 * 2) + 2 * 3 * D_MODEL * tf * 4
           + row_chunk * (3 * tf + 2 * D_MODEL) * 4)
    x_specs = [pl.BlockSpec((1, HQ_TILES, cap, LANES), lambda e, f: (e, 0, 0, 0)) for cap in caps]
    y_shapes = [(N_EXPERTS, cap + COMBINE_CHUNK, D_MODEL // 2) for cap in caps]
    y_specs = [pl.BlockSpec((1,) + s[1:], lambda e, f: (e, 0, 0)) for s in y_shapes]
    return pl.pallas_call(
        functools.partial(_ffn_kernel, caps=caps, n_f=n_f, row_chunk=row_chunk),
        grid=(N_EXPERTS, n_f),
        in_specs=x_specs + [pl.BlockSpec((1, D_MODEL, tf), lambda e, f: (e, 0, f)),
                            pl.BlockSpec((1, D_MODEL, tf), lambda e, f: (e, 0, f)),
                            pl.BlockSpec((1, tf, D_MODEL), lambda e, f: (e, f, 0))],
        out_specs=y_specs,
        out_shape=[jax.ShapeDtypeStruct(s, I32) for s in y_shapes],
        scratch_shapes=[pltpu.VMEM((rows, D_MODEL), F32)],
        compiler_params=pltpu.CompilerParams(
            dimension_semantics=("arbitrary", "arbitrary"),
            vmem_limit_bytes=_vmem_limit(est + (4 << 20))),
        name="expert_ffn",
    )(*xs_groups, w_gate, w_up, w_down)


def _combine_kernel(start_ref, nch_ref, x1_ref, mod_ref, g_ref, slot_ref, y_hbm, o_ref,
                    ybuf_ref, cols_ref, ffn_ref, sem, *, first_k, kblock):
    i = pl.program_id(0)
    tm = x1_ref.shape[0]
    cur = i % 2

    def chunk_copy(e, src_row, buf, dst_row):
        return pltpu.make_async_copy(
            y_hbm.at[e, pl.ds(pl.multiple_of(src_row, SUBLANES), COMBINE_CHUNK)],
            ybuf_ref.at[buf, pl.ds(pl.multiple_of(dst_row, COMBINE_CHUNK), COMBINE_CHUNK)],
            sem.at[buf])

    def fetch(tile, buf):
        pos = jnp.int32(0)
        for e in range(N_EXPERTS):
            start = start_ref[tile, e]
            nch = nch_ref[tile, e]

            def issue(c, carry, e=e, start=start, pos=pos):
                chunk_copy(e, start + c * COMBINE_CHUNK, buf, pos + c * COMBINE_CHUNK).start()
                return carry
            lax.fori_loop(0, nch, issue, 0)
            pos = pos + nch * COMBINE_CHUNK

    @pl.when(i == 0)
    def _():
        ybuf_ref[...] = jnp.zeros_like(ybuf_ref)
        fetch(0, 0)

    @pl.when(i + 1 < pl.num_programs(0))
    def _():
        fetch(i + 1, 1 - cur)

    pos = jnp.int32(0)
    begins = []
    for e in range(N_EXPERTS):
        begins.append(pos)
        pos = pos + nch_ref[i, e] * COMBINE_CHUNK
    begins.append(pos)

    def drain(c, carry):
        chunk_copy(0, jnp.int32(0), cur, jnp.int32(0)).wait()
        return carry
    lax.fori_loop(0, pos // COMBINE_CHUNK, drain, 0)

    slots = slot_ref[...]
    for e in range(N_EXPERTS):
        s_e = slots[:, e:e + 1]
        col = jnp.where(s_e >= 0, s_e + (begins[e] - start_ref[i, e]), -1)
        cols_ref[e] = jnp.broadcast_to(col, (tm, LANES))

    lane = lax.broadcasted_iota(I32, (tm, LANES), 1)

    def selection(k0, width):
        halves = []
        for h in range(width // LANES):
            c0 = k0 + h * LANES
            target = lane + c0
            e_lo = jnp.int32(0)
            e_hi = jnp.int32(0)
            for e in range(N_EXPERTS):
                e_lo = e_lo + (begins[e + 1] <= c0).astype(I32)
                e_hi = e_hi + (begins[e] < c0 + LANES).astype(I32)

            def mark(e, hit, target=target):
                return jnp.where(cols_ref[e] == target, 1.0, hit)
            halves.append(lax.fori_loop(e_lo, e_hi, mark, jnp.zeros((tm, LANES), F32)))
        return jnp.concatenate(halves, axis=1)

    def block(k0, width):
        sel = selection(k0, width)
        lo, hi = _unpack_pair(ybuf_ref[cur, pl.ds(k0, width), :])
        return jnp.concatenate([jnp.dot(sel, lo, preferred_element_type=F32),
                                jnp.dot(sel, hi, preferred_element_type=F32)], axis=1)

    ffn_ref[...] = block(0, first_k)

    def kstep(kb, carry):
        ffn_ref[...] += block(pl.multiple_of(kb * kblock, kblock), kblock)
        return carry
    lax.fori_loop(first_k // kblock, (pos + kblock - 1) // kblock, kstep, 0)

    out = x1_ref[...] + mod_ref[0][5:6] * ffn_ref[...]
    ms = jnp.mean(out * out, axis=-1, keepdims=True)
    o_ref[...] = (out * lax.rsqrt(ms + EPS)) * g_ref[...]


def _combine(x1, mod3, mod_row, g_final, slot_t, tile_start, tile_nch, y):
    t = x1.shape[0]
    tm = TOKEN_TILE
    first_k = 2 * tm + N_EXPERTS * COMBINE_CHUNK
    kblock = 256
    max_rows = N_EXPERTS * (tm + 2 * COMBINE_CHUNK)
    max_rows = -(-max_rows // kblock) * kblock
    row = lambda i, *_: (i, 0)
    grid_spec = pltpu.PrefetchScalarGridSpec(
        num_scalar_prefetch=2,
        grid=(t // tm,),
        in_specs=[pl.BlockSpec((tm, D_MODEL), row),
                  pl.BlockSpec((1, N_MOD, D_MODEL), lambda i, *_: (mod_row(i), 0, 0)),
                  pl.BlockSpec((1, D_MODEL), lambda i, *_: (0, 0)),
                  pl.BlockSpec((tm, N_EXPERTS), row),
                  pl.BlockSpec(memory_space=pl.ANY)],
        out_specs=pl.BlockSpec((tm, D_MODEL), row),
        scratch_shapes=[pltpu.VMEM((2, max_rows, D_MODEL // 2), I32),
                        pltpu.VMEM((N_EXPERTS, tm, LANES), I32),
                        pltpu.VMEM((tm, D_MODEL), F32),
                        pltpu.SemaphoreType.DMA((2,))],
    )
    return pl.pallas_call(
        functools.partial(_combine_kernel, first_k=first_k, kblock=kblock),
        grid_spec=grid_spec,
        out_shape=jax.ShapeDtypeStruct((t, D_MODEL), F32),
        compiler_params=pltpu.CompilerParams(
            dimension_semantics=("arbitrary",),
            vmem_limit_bytes=_vmem_limit(2 * max_rows * D_MODEL * 2 + 16 * tm * D_MODEL * 4)),
        name="combine",
    )(tile_start, tile_nch, x1, mod3, g_final, slot_t, y)


def _rope_tables(n):
    rows = n // GRID_W
    row = jnp.repeat(jnp.arange(rows, dtype=F32), GRID_W)
    col = jnp.tile(jnp.arange(GRID_W, dtype=F32), rows)
    inv = ROPE_THETA ** (-jnp.arange(ROPE_FREQS, dtype=F32) / ROPE_FREQS)
    ang_r = row[:, None] * inv
    ang_c = col[:, None] * inv
    zero = jnp.zeros_like(ang_r)
    cos = jnp.concatenate([jnp.cos(ang_r)] * 2 + [jnp.cos(ang_c)] * 2, axis=1)
    sin_a = jnp.concatenate([-jnp.sin(ang_r), zero, -jnp.sin(ang_c), zero], axis=1)
    sin_b = jnp.concatenate([zero, jnp.sin(ang_r), zero, jnp.sin(ang_c)], axis=1)
    return cos, sin_a, sin_b


def _token_group(x, mod3, mod_row, seq, weights, rope_tabs, ctx_kv):
    (norm_mix, w_in_bf, sink, pool_w, pool_scale, w_out_bf, norm_ffn, w_router,
     w_gate, w_up, w_down, norm_final) = weights
    b = x.shape[0]
    t = b * seq
    x2 = x.reshape(t, D_MODEL)
    q, k, v, p = _in_projection(x2, mod3, mod_row, norm_mix, w_in_bf, rope_tabs)
    if ctx_kv is None:
        attn = _context_attention(q, k, v, sink, seq)
    else:
        attn = _latent_attention(q, k, v, ctx_kv[0], ctx_kv[1], sink, seq)
    pooled = _pool_mixer(p, pool_w, pool_scale, seq)
    x1, h_packed, aff = _out_projection(attn, pooled, x2, mod3, mod_row, norm_ffn, w_out_bf,
                                        w_router)

    cap = EC_FACTOR * t // N_EXPERTS
    idx, slot3, off3 = _routing(aff, cap)
    xs = _gather_rows(h_packed.reshape(-1, LANES), _packed_row_ids(idx, cap))
    xs = xs.reshape(N_EXPERTS, HQ_TILES, cap, LANES)

    chunks_per_tile = TOKEN_TILE // LANES
    tile_off = off3[:, ::chunks_per_tile, 0]
    tile_end = jnp.concatenate([tile_off[:, 1:], jnp.full((N_EXPERTS, 1), cap, I32)], axis=1)
    tile_start = (tile_off // SUBLANES) * SUBLANES
    tile_nch = jnp.where(tile_end > tile_off,
                         (tile_end - tile_start + COMBINE_CHUNK - 1) // COMBINE_CHUNK, 0)
    slot_t = slot3.reshape(N_EXPERTS, t).T

    def finish(y):
        out = _combine(x1, mod3, mod_row, norm_final, slot_t, tile_start.T, tile_nch.T, y)
        return out.reshape(b, seq, D_MODEL)
    return xs, finish, k, v


def kernel(x_prompt, x_sample, c, cache_k, cache_v, c_ctx, w_ada, b_ada, norm_mix, w_in,
           sink_logits, pool_w, pool_scale, w_out, norm_ffn, w_router, w_gate, w_up, w_down,
           norm_final):
    n_b, seq, _ = x_prompt.shape
    n_db, n_lat, _ = x_sample.shape
    assert 1 + n_db <= MOD_ROWS and seq == TOKEN_TILE and n_lat % TOKEN_TILE == 0

    cond = jnp.concatenate(
        [c_ctx[None, :], c, jnp.zeros((MOD_ROWS - 1 - n_db, D_MODEL), F32)], axis=0)
    mod3 = _modulation(cond, w_ada[0], b_ada[0]).reshape(MOD_ROWS, N_MOD, D_MODEL)

    w_router_bf = jnp.pad(w_router[0], ((0, 0), (0, LANES - N_EXPERTS))).astype(BF16)
    weights = (norm_mix[0][None, :], w_in[0].astype(BF16), sink_logits[0], pool_w[0],
               pool_scale[0][None, :], w_out[0].astype(BF16), norm_ffn[0][None, :], w_router_bf,
               w_gate.reshape(w_gate.shape[1:]), w_up.reshape(w_up.shape[1:]),
               w_down.reshape(w_down.shape[1:]), norm_final[None, :])

    xs_p, finish_p, k_p, v_p = _token_group(
        x_prompt, mod3, lambda i: 0, seq, weights, None, None)

    tiles_per_seq = n_lat // TOKEN_TILE
    past = cache_k.shape[2]
    ck = cache_k[:, 0].reshape(n_db, past, KV_WIDTH)
    cv = cache_v[:, 0].reshape(n_db, past, KV_WIDTH)
    xs_l, finish_l, _, _ = _token_group(
        x_sample, mod3, lambda i: 1 + i // tiles_per_seq, n_lat, weights,
        _rope_tables(n_lat), (ck, cv))

    y_p, y_l = _expert_ffn([xs_p, xs_l], weights[8], weights[9], weights[10])
    y_prompt = finish_p(y_p)
    y_sample = finish_l(y_l)

    state_k = k_p.reshape(n_b, 1, seq, N_KV_HEADS, HEAD_DIM)
    state_v = v_p.reshape(n_b, 1, seq, N_KV_HEADS, HEAD_DIM)
    return (y_prompt, y_sample, state_k, state_v)
```

```python
import functools

import jax
import jax.numpy as jnp
import numpy as np
from jax import lax
from jax.experimental import pallas as pl
from jax.experimental.pallas import tpu as pltpu
from jax.experimental.pallas import tpu_sc as plsc

F32 = jnp.float32
BF16 = jnp.bfloat16
I32 = jnp.int32

D_MODEL = 2048
N_HEADS = 8
N_KV_HEADS = 2
HEAD_DIM = 128
Q_PER_KV = N_HEADS // N_KV_HEADS
ATTN_WIDTH = N_HEADS * HEAD_DIM
KV_WIDTH = N_KV_HEADS * HEAD_DIM
POOL_WIDTH = D_MODEL - ATTN_WIDTH
POOL_SIZES = (2, 4, 8, 16)
POOL_GROUP = POOL_WIDTH // len(POOL_SIZES)
IN_WIDTH = ATTN_WIDTH + 2 * KV_WIDTH + POOL_WIDTH
WINDOW = 128
BLOCK = 128
GRID_W = 64
ROPE_THETA = 10000.0
ROPE_FREQS = HEAD_DIM // 4
N_EXPERTS = 16
EC_FACTOR = 2
D_EXPERT = 1024
N_MOD = 6
EPS = 1e-6
NEG = -1e30

LANES = 128
SUBLANES = 8
BF16_ROWS = 16
VMEM_CAP = 64 * 1024 * 1024
SC_CORES = 2
SC_SUBCORES = 16

MOD_ROWS = 8
TOKEN_TILE = 256
PROJ_TILE = 512
SUB_TILE = 256
HQ_TILES = D_MODEL // 2 // LANES + 1
GATHER_CHUNK = 128
FFN_RANGES = 2
POOL_HALO = 8
COMBINE_CHUNK = BF16_ROWS


def _vmem_limit(nbytes):
    return int(min(VMEM_CAP - (4 << 20), max(nbytes, 16 << 20)))


def _mod_kernel(c_ref, w_ref, b_ref, o_ref):
    c = c_ref[...]
    s = c * jax.nn.sigmoid(c)
    o_ref[...] = jnp.dot(s.astype(BF16), w_ref[...].astype(BF16),
                         preferred_element_type=F32) + b_ref[...]


def _modulation(cond, w_ada, b_ada):
    n = w_ada.shape[1]
    tn = 1024
    return pl.pallas_call(
        _mod_kernel,
        grid=(n // tn,),
        in_specs=[pl.BlockSpec((MOD_ROWS, D_MODEL), lambda j: (0, 0)),
                  pl.BlockSpec((D_MODEL, tn), lambda j: (0, j)),
                  pl.BlockSpec((1, tn), lambda j: (0, j))],
        out_specs=pl.BlockSpec((MOD_ROWS, tn), lambda j: (0, j)),
        out_shape=jax.ShapeDtypeStruct((MOD_ROWS, n), F32),
        compiler_params=pltpu.CompilerParams(
            vmem_limit_bytes=_vmem_limit(3 * D_MODEL * tn * 4)),
        name="modulation",
    )(cond, w_ada, b_ada.reshape(1, n))


def _norm_mod(x, g, shift, scale):
    ms = jnp.mean(x * x, axis=-1, keepdims=True)
    y = x * lax.rsqrt(ms + EPS)
    return (y * g) * (1.0 + scale) + shift


def _inproj_kernel(*refs, rope):
    if rope:
        x_ref, mod_ref, g_ref, w_ref, cos_ref, sa_ref, sb_ref, q_ref, k_ref, v_ref, p_ref = refs
    else:
        x_ref, mod_ref, g_ref, w_ref, q_ref, k_ref, v_ref, p_ref = refs
    mod = mod_ref[0]
    for s in range(x_ref.shape[0] // SUB_TILE):
        rows = slice(s * SUB_TILE, (s + 1) * SUB_TILE)
        h = _norm_mod(x_ref[rows, :], g_ref[...], mod[0:1], mod[1:2])
        u = jnp.dot(h.astype(BF16), w_ref[...], preferred_element_type=F32)

        def rot(xh, rows=rows):
            return (xh * cos_ref[rows, :] + pltpu.roll(xh, LANES - ROPE_FREQS, 1) * sa_ref[rows, :]
                    + pltpu.roll(xh, ROPE_FREQS, 1) * sb_ref[rows, :])

        for hd in range(N_HEADS):
            xh = u[:, hd * HEAD_DIM:(hd + 1) * HEAD_DIM]
            q_ref[rows, hd * HEAD_DIM:(hd + 1) * HEAD_DIM] = (rot(xh) if rope else xh).astype(BF16)
        for hd in range(N_KV_HEADS):
            lo = ATTN_WIDTH + hd * HEAD_DIM
            xh = u[:, lo:lo + HEAD_DIM]
            k_ref[rows, hd * HEAD_DIM:(hd + 1) * HEAD_DIM] = rot(xh) if rope else xh
        v_ref[rows, :] = u[:, ATTN_WIDTH + KV_WIDTH:ATTN_WIDTH + 2 * KV_WIDTH]
        p_ref[rows, :] = u[:, ATTN_WIDTH + 2 * KV_WIDTH:]


def _in_projection(x2, mod3, mod_row, g, w_in_bf, rope_tabs):
    t = x2.shape[0]
    tm = PROJ_TILE
    rope = rope_tabs is not None
    row = lambda i: (i, 0)
    in_specs = [pl.BlockSpec((tm, D_MODEL), row),
                pl.BlockSpec((1, N_MOD, D_MODEL), lambda i: (mod_row(i), 0, 0)),
                pl.BlockSpec((1, D_MODEL), lambda i: (0, 0)),
                pl.BlockSpec((D_MODEL, IN_WIDTH), lambda i: (0, 0))]
    args = [x2, mod3, g, w_in_bf]
    if rope:
        n_seq = rope_tabs[0].shape[0]
        seq_blocks = n_seq // tm
        for tab in rope_tabs:
            in_specs.append(pl.BlockSpec((tm, HEAD_DIM), lambda i: (i % seq_blocks, 0)))
            args.append(tab)
    return pl.pallas_call(
        functools.partial(_inproj_kernel, rope=rope),
        grid=(t // tm,),
        in_specs=in_specs,
        out_specs=[pl.BlockSpec((tm, ATTN_WIDTH), row),
                   pl.BlockSpec((tm, KV_WIDTH), row),
                   pl.BlockSpec((tm, KV_WIDTH), row),
                   pl.BlockSpec((tm, POOL_WIDTH), row)],
        out_shape=[jax.ShapeDtypeStruct((t, ATTN_WIDTH), BF16),
                   jax.ShapeDtypeStruct((t, KV_WIDTH), F32),
                   jax.ShapeDtypeStruct((t, KV_WIDTH), F32),
                   jax.ShapeDtypeStruct((t, POOL_WIDTH), F32)],
        compiler_params=pltpu.CompilerParams(
            vmem_limit_bytes=_vmem_limit(2 * D_MODEL * IN_WIDTH * 2 + 24 * tm * D_MODEL * 4)),
        name="in_projection",
    )(*args)


def _softmax_pv(s_list, v_list, sink_col):
    m = sink_col
    for s in s_list:
        m = jnp.maximum(m, jnp.max(s, axis=-1, keepdims=True))
    denom = jnp.exp(sink_col - m)
    out = None
    for s, v in zip(s_list, v_list):
        e = jnp.exp(s - m)
        denom = denom + jnp.sum(e, axis=-1, keepdims=True)
        o = jnp.dot(e.astype(BF16), v, preferred_element_type=F32)
        out = o if out is None else out + o
    return out * (1.0 / denom)


def _stack_heads(q, kv):
    return jnp.concatenate(
        [q[:, (kv * Q_PER_KV + g) * HEAD_DIM:(kv * Q_PER_KV + g + 1) * HEAD_DIM]
         for g in range(Q_PER_KV)], axis=0)


def _sink_column(sink_ref, kv, rows):
    r = lax.broadcasted_iota(I32, (Q_PER_KV * rows, 1), 0)
    col = jnp.zeros((Q_PER_KV * rows, 1), F32)
    for g in range(Q_PER_KV):
        col = jnp.where((r >= g * rows) & (r < (g + 1) * rows), sink_ref[kv * Q_PER_KV + g], col)
    return col


def _qk(q, k):
    return lax.dot_general(q, k, (((1,), (1,)), ((), ())), preferred_element_type=F32)


def _ctx_attn_kernel(sink_ref, q_ref, k_ref, v_ref, o_ref):
    rows = q_ref.shape[0]
    scale = HEAD_DIM ** -0.5
    q = q_ref[...]
    for kv in range(N_KV_HEADS):
        kh = k_ref[:, kv * HEAD_DIM:(kv + 1) * HEAD_DIM].astype(BF16)
        vh = v_ref[:, kv * HEAD_DIM:(kv + 1) * HEAD_DIM].astype(BF16)
        qs = _stack_heads(q, kv)
        s = _qk(qs, kh) * scale
        o = _softmax_pv([s], [vh], _sink_column(sink_ref, kv, rows))
        for g in range(Q_PER_KV):
            hd = kv * Q_PER_KV + g
            o_ref[:, hd * HEAD_DIM:(hd + 1) * HEAD_DIM] = o[g * rows:(g + 1) * rows].astype(BF16)


def _context_attention(q, k, v, sink, seq):
    t = q.shape[0]
    row = lambda b: (b, 0)
    return pl.pallas_call(
        _ctx_attn_kernel,
        grid=(t // seq,),
        in_specs=[pl.BlockSpec(memory_space=pltpu.SMEM),
                  pl.BlockSpec((seq, ATTN_WIDTH), row),
                  pl.BlockSpec((seq, KV_WIDTH), row),
                  pl.BlockSpec((seq, KV_WIDTH), row)],
        out_specs=pl.BlockSpec((seq, ATTN_WIDTH), row),
        out_shape=jax.ShapeDtypeStruct((t, ATTN_WIDTH), BF16),
        name="context_attention",
    )(sink, q, k, v)


def _lat_attn_kernel(sink_ref, q_ref, k_ref, v_ref, ck_ref, cv_ref, o_ref, *, n_seq):
    i = pl.program_id(1)
    scale = HEAD_DIM ** -0.5
    band = 3 * BLOCK
    start = pl.multiple_of(jnp.clip((i - 1) * BLOCK, 0, n_seq - band), BLOCK)
    rows = Q_PER_KV * BLOCK
    qpos = i * BLOCK + lax.broadcasted_iota(I32, (rows, band), 0) % BLOCK
    kpos = start + lax.broadcasted_iota(I32, (rows, band), 1)
    mask = jnp.abs(kpos - qpos) <= WINDOW
    q = q_ref[...]
    for kv in range(N_KV_HEADS):
        cols = slice(kv * HEAD_DIM, (kv + 1) * HEAD_DIM)
        kb = k_ref[pl.ds(start, band), cols].astype(BF16)
        vb = v_ref[pl.ds(start, band), cols].astype(BF16)
        ck = ck_ref[0, :, cols].astype(BF16)
        cv = cv_ref[0, :, cols].astype(BF16)
        qs = _stack_heads(q, kv)
        s_loc = jnp.where(mask, _qk(qs, kb) * scale, NEG)
        s_ctx = _qk(qs, ck) * scale
        o = _softmax_pv([s_loc, s_ctx], [vb, cv], _sink_column(sink_ref, kv, BLOCK))
        for g in range(Q_PER_KV):
            hd = kv * Q_PER_KV + g
            o_ref[:, hd * HEAD_DIM:(hd + 1) * HEAD_DIM] = o[g * BLOCK:(g + 1) * BLOCK].astype(BF16)


def _latent_attention(q, k, v, ck, cv, sink, n_seq):
    t = q.shape[0]
    nb = n_seq // BLOCK
    past = ck.shape[1]
    return pl.pallas_call(
        functools.partial(_lat_attn_kernel, n_seq=n_seq),
        grid=(t // n_seq, nb),
        in_specs=[pl.BlockSpec(memory_space=pltpu.SMEM),
                  pl.BlockSpec((BLOCK, ATTN_WIDTH), lambda b, i: (b * nb + i, 0)),
                  pl.BlockSpec((n_seq, KV_WIDTH), lambda b, i: (b, 0)),
                  pl.BlockSpec((n_seq, KV_WIDTH), lambda b, i: (b, 0)),
                  pl.BlockSpec((1, past, KV_WIDTH), lambda b, i: (b, 0, 0)),
                  pl.BlockSpec((1, past, KV_WIDTH), lambda b, i: (b, 0, 0))],
        out_specs=pl.BlockSpec((BLOCK, ATTN_WIDTH), lambda b, i: (b * nb + i, 0)),
        out_shape=jax.ShapeDtypeStruct((t, ATTN_WIDTH), BF16),
        name="latent_attention",
    )(sink, q, k, v, ck, cv)


def _pool_kernel(p_ref, w_ref, s_ref, o_ref, pad_ref):
    n = p_ref.shape[0]
    rows = n + 2 * POOL_HALO
    t = lax.broadcasted_iota(I32, (n, 1), 0)
    zeros = jnp.zeros((POOL_HALO, POOL_GROUP), F32)
    pad_ref[0:POOL_HALO, :] = zeros
    pad_ref[POOL_HALO + n:rows, :] = zeros
    for g, w in enumerate(POOL_SIZES):
        cols = slice(g * POOL_GROUP, (g + 1) * POOL_GROUP)
        pg = p_ref[:, cols]
        pad_ref[POOL_HALO:POOL_HALO + n, :] = pg
        x = pad_ref[...]
        acc = x + pltpu.roll(x, 1, 0)
        step = 1
        while 2 * step < w:
            acc = pltpu.roll(acc, step, 0) + pltpu.roll(acc, rows - step, 0)
            step *= 2
        wsum = acc[POOL_HALO:POOL_HALO + n]
        lo = jnp.maximum(t - w // 2, 0)
        hi = jnp.minimum(t + w - w // 2, n)
        inv_cnt = 1.0 / (hi - lo).astype(F32)
        mixed = wsum * inv_cnt - pg
        y = jnp.dot(mixed.astype(BF16), w_ref[g].astype(BF16), preferred_element_type=F32)
        o_ref[:, cols] = (y * s_ref[:, cols]).astype(BF16)


def _pool_mixer(p, pool_w, pool_scale, seq):
    t = p.shape[0]
    row = lambda b: (b, 0)
    return pl.pallas_call(
        _pool_kernel,
        grid=(t // seq,),
        in_specs=[pl.BlockSpec((seq, POOL_WIDTH), row),
                  pl.BlockSpec((len(POOL_SIZES), POOL_GROUP, POOL_GROUP), lambda b: (0, 0, 0)),
                  pl.BlockSpec((1, POOL_WIDTH), lambda b: (0, 0))],
        out_specs=pl.BlockSpec((seq, POOL_WIDTH), row),
        out_shape=jax.ShapeDtypeStruct((t, POOL_WIDTH), BF16),
        scratch_shapes=[pltpu.VMEM((seq + 2 * POOL_HALO, POOL_GROUP), F32)],
        compiler_params=pltpu.CompilerParams(
            vmem_limit_bytes=_vmem_limit(8 * seq * POOL_WIDTH * 4)),
        name="pool_mixer",
    )(p, pool_w, pool_scale)


def _pack_pair(lo, hi):
    return lax.bitcast_convert_type(pltpu.pack_elementwise([lo, hi], packed_dtype=BF16), I32)


def _unpack_pair(words):
    lo = pltpu.unpack_elementwise(words, index=0, packed_dtype=BF16, unpacked_dtype=F32)
    hi = pltpu.unpack_elementwise(words, index=1, packed_dtype=BF16, unpacked_dtype=F32)
    return lo, hi


def _outproj_kernel(a_ref, p_ref, x_ref, mod_ref, g_ref, wo_ref, wr_ref, x1_ref, h_ref, aff_ref):
    mod = mod_ref[0]
    groups = SUB_TILE // SUBLANES
    half = D_MODEL // 2
    for s in range(x_ref.shape[0] // SUB_TILE):
        rows = slice(s * SUB_TILE, (s + 1) * SUB_TILE)
        grp = slice(s * groups, (s + 1) * groups)
        mix = (jnp.dot(a_ref[rows, :], wo_ref[0:ATTN_WIDTH, :], preferred_element_type=F32)
               + jnp.dot(p_ref[rows, :], wo_ref[ATTN_WIDTH:D_MODEL, :],
                         preferred_element_type=F32))
        x1 = x_ref[rows, :] + mod[2:3] * mix
        x1_ref[rows, :] = x1
        h = _norm_mod(x1, g_ref[...], mod[3:4], mod[4:5])
        logits = jnp.dot(h.astype(BF16), wr_ref[...], preferred_element_type=F32)
        lane = lax.broadcasted_iota(I32, logits.shape, 1)
        logits = jnp.where(lane < N_EXPERTS, logits, -jnp.inf)
        m = jnp.max(logits, axis=-1, keepdims=True)
        e = jnp.exp(logits - m)
        aff = e / jnp.sum(e, axis=-1, keepdims=True)
        aff_ref[rows, :] = aff[:, 0:N_EXPERTS]
        for c in range(HQ_TILES - 1):
            words = _pack_pair(h[:, c * LANES:(c + 1) * LANES],
                               h[:, half + c * LANES:half + (c + 1) * LANES])
            h_ref[grp, c * SUBLANES:(c + 1) * SUBLANES, :] = (
                words.reshape(groups, SUBLANES, LANES))
        h_ref[grp, (HQ_TILES - 1) * SUBLANES:, :] = (
            lax.bitcast_convert_type(aff, I32).reshape(groups, SUBLANES, LANES))


def _out_projection(attn, pooled, x2, mod3, mod_row, g, w_out_bf, w_router):
    t = x2.shape[0]
    tm = PROJ_TILE
    row = lambda i: (i, 0)
    return pl.pallas_call(
        _outproj_kernel,
        grid=(t // tm,),
        in_specs=[pl.BlockSpec((tm, ATTN_WIDTH), row),
                  pl.BlockSpec((tm, POOL_WIDTH), row),
                  pl.BlockSpec((tm, D_MODEL), row),
                  pl.BlockSpec((1, N_MOD, D_MODEL), lambda i: (mod_row(i), 0, 0)),
                  pl.BlockSpec((1, D_MODEL), lambda i: (0, 0)),
                  pl.BlockSpec((D_MODEL, D_MODEL), lambda i: (0, 0)),
                  pl.BlockSpec((D_MODEL, LANES), lambda i: (0, 0))],
        out_specs=[pl.BlockSpec((tm, D_MODEL), row),
                   pl.BlockSpec((tm // SUBLANES, HQ_TILES * SUBLANES, LANES), lambda i: (i, 0, 0)),
                   pl.BlockSpec((tm, N_EXPERTS), row)],
        out_shape=[jax.ShapeDtypeStruct((t, D_MODEL), F32),
                   jax.ShapeDtypeStruct((t // SUBLANES, HQ_TILES * SUBLANES, LANES), I32),
                   jax.ShapeDtypeStruct((t, N_EXPERTS), F32)],
        compiler_params=pltpu.CompilerParams(
            vmem_limit_bytes=_vmem_limit(2 * D_MODEL * D_MODEL * 2 + 24 * tm * D_MODEL * 4)),
        name="out_projection",
    )(attn, pooled, x2, mod3, g, w_out_bf, w_router)


def _route_kernel(a_ref, idx_ref, slot_ref, off_ref, *, cap):
    a = a_ref[...]
    n_e, n_c, _ = a.shape
    rows = n_e * n_c

    def bisect(it, thr):
        cand = thr | jnp.left_shift(jnp.int32(1), 30 - it)
        cand_f = lax.bitcast_convert_type(cand, F32)
        cnt = jnp.sum(jnp.sum((a >= cand_f).astype(F32), axis=2, keepdims=True),
                      axis=1, keepdims=True)
        return jnp.where(cnt >= cap, cand, thr)

    thr = lax.fori_loop(0, 31, bisect, jnp.zeros((n_e, 1, 1), I32))
    thr_f = lax.bitcast_convert_type(thr, F32)
    gt = (a > thr_f).astype(F32).reshape(rows, LANES)
    eq = (a == thr_f).astype(F32).reshape(rows, LANES)

    li = lax.broadcasted_iota(I32, (LANES, LANES), 0)
    lj = lax.broadcasted_iota(I32, (LANES, LANES), 1)
    upper_incl = (li <= lj).astype(BF16)
    ri = lax.broadcasted_iota(I32, (rows, rows), 0)
    rj = lax.broadcasted_iota(I32, (rows, rows), 1)
    same_expert = (ri // n_c) == (rj // n_c)
    before = (same_expert & (rj < ri)).astype(BF16)
    whole = same_expert.astype(BF16)

    def lane_bcast(col):
        return jnp.broadcast_to(col, (rows, LANES)).astype(BF16)

    def prefix(x):
        incl = jnp.dot(x.astype(BF16), upper_incl, preferred_element_type=F32)
        tot = incl[:, LANES - 1:LANES]
        off = jnp.dot(before, lane_bcast(tot), preferred_element_type=F32)
        return incl, tot, off

    n_gt = jnp.dot(whole, lane_bcast(jnp.sum(gt, axis=1, keepdims=True)),
                   preferred_element_type=F32)
    need = cap - n_gt
    incl_eq, _, off_eq = prefix(eq)
    rank_eq = off_eq + incl_eq - eq
    sel = jnp.where((eq > 0) & (rank_eq < need), 1.0, gt)
    incl, tot, off = prefix(sel)
    slot = off + incl - sel
    slot_ref[...] = jnp.where(sel > 0, slot, -1.0).astype(I32).reshape(n_e, n_c, LANES)
    off_ref[...] = off.astype(I32).reshape(n_e, n_c, LANES)

    s_lane = lax.broadcasted_iota(I32, (1, cap), 1).astype(F32)
    c_col = lax.broadcasted_iota(I32, (n_c, 1), 0).astype(F32)
    for e in range(n_e):
        r0 = e * n_c
        incl_e = incl[r0:r0 + n_c]
        off_e = off[r0:r0 + n_c, 0:1]
        tot_e = tot[r0:r0 + n_c]
        onehot = ((off_e <= s_lane) & (s_lane < off_e + tot_e)).astype(F32)
        counts = lax.dot_general(incl_e.astype(BF16), onehot.astype(BF16),
                                 (((0,), (0,)), ((), ())), preferred_element_type=F32)
        local = s_lane - jnp.sum(onehot * off_e, axis=0, keepdims=True)
        lane = jnp.sum((counts <= local).astype(F32), axis=0, keepdims=True)
        chunk = jnp.sum(onehot * c_col, axis=0, keepdims=True)
        idx_ref[e] = (chunk * LANES + lane).astype(I32)


def _routing(aff, cap):
    t = aff.shape[0]
    n_c = t // LANES
    a3 = aff.T.reshape(N_EXPERTS, n_c, LANES)
    return pl.pallas_call(
        functools.partial(_route_kernel, cap=cap),
        out_shape=[jax.ShapeDtypeStruct((N_EXPERTS, 1, cap), I32),
                   jax.ShapeDtypeStruct((N_EXPERTS, n_c, LANES), I32),
                   jax.ShapeDtypeStruct((N_EXPERTS, n_c, LANES), I32)],
        compiler_params=pltpu.CompilerParams(vmem_limit_bytes=_vmem_limit(48 << 20)),
        name="routing",
    )(a3)


def _gather_rows(table, row_ids):
    n_chunks = row_ids.shape[0]
    n_workers = SC_CORES * SC_SUBCORES
    per_worker = n_chunks // n_workers
    assert row_ids.shape[1] == GATHER_CHUNK and n_chunks % n_workers == 0
    mesh = plsc.VectorSubcoreMesh(core_axis_name="core", subcore_axis_name="subcore")

    @functools.partial(
        pl.kernel, mesh=mesh,
        out_type=jax.ShapeDtypeStruct((n_chunks * GATHER_CHUNK, LANES), I32),
        scratch_types=[pltpu.VMEM((GATHER_CHUNK,), I32),
                       pltpu.VMEM((GATHER_CHUNK, LANES), I32),
                       pltpu.SemaphoreType.DMA],
        name="gather_rows",
    )
    def gather(table_hbm, ids_hbm, out_hbm, ids_v, rows_v, sem):
        worker = lax.axis_index("subcore") * SC_CORES + lax.axis_index("core")

        @pl.loop(0, per_worker)
        def _(j):
            chunk = worker * per_worker + j
            pltpu.sync_copy(ids_hbm.at[chunk], ids_v)
            pltpu.async_copy(table_hbm.at[ids_v], rows_v, sem).wait()
            pltpu.sync_copy(
                rows_v, out_hbm.at[pl.ds(pl.multiple_of(chunk * GATHER_CHUNK, GATHER_CHUNK),
                                         GATHER_CHUNK)])

    return gather(table, row_ids)


def _packed_row_ids(idx, cap):
    tok = idx.reshape(-1, 1, cap // GATHER_CHUNK, GATHER_CHUNK)
    tile = jnp.arange(HQ_TILES, dtype=I32).reshape(1, HQ_TILES, 1, 1)
    ids = ((tok // SUBLANES) * HQ_TILES + tile) * SUBLANES + tok % SUBLANES
    return ids.reshape(-1, GATHER_CHUNK)


def _ffn_kernel(*refs, caps, n_f, row_chunk, first_expert, n_prior):
    n_g = len(caps)
    x_refs = refs[:n_g]
    wg_ref, wu_ref, wd_ref = refs[n_g:n_g + 3]
    y_refs = refs[n_g + 3 + n_prior:2 * n_g + 3 + n_prior]
    acc_ref = refs[2 * n_g + 3 + n_prior]
    e = pl.program_id(0)
    f = pl.program_id(1)

    @pl.when((e == 0) & (f == 0))
    def _():
        acc_ref[...] = jnp.zeros_like(acc_ref)

    first = f == 0
    base = 0
    for x_ref, cap in zip(x_refs, caps):
        for r in range(cap // row_chunk):
            rs = slice(r * row_chunk, (r + 1) * row_chunk)
            acc_rows = slice(base + r * row_chunk, base + (r + 1) * row_chunk)
            pairs = [_unpack_pair(x_ref[0, c, rs, :]) for c in range(HQ_TILES - 1)]
            x = jnp.concatenate([p[0] for p in pairs] + [p[1] for p in pairs], axis=1)
            gate_act = jnp.dot(x, wg_ref[0], preferred_element_type=F32)
            up = jnp.dot(x, wu_ref[0], preferred_element_type=F32)
            hid = (gate_act * jax.nn.sigmoid(gate_act)) * up
            part = jnp.dot(hid, wd_ref[0], preferred_element_type=F32)
            acc_ref[acc_rows, :] = part + jnp.where(first, 0.0, acc_ref[acc_rows, :])
        base += cap

    @pl.when(f == n_f - 1)
    def _():
        base = 0
        for x_ref, y_ref, cap in zip(x_refs, y_refs, caps):
            aff = lax.bitcast_convert_type(x_ref[0, HQ_TILES - 1], F32)
            lane = lax.broadcasted_iota(I32, aff.shape, 1)
            gates = jnp.sum(jnp.where(lane == e + first_expert, aff, 0.0), axis=1, keepdims=True)
            y = acc_ref[base:base + cap, :] * gates
            y_ref[0, 0:cap, :] = _pack_pair(y[:, 0:D_MODEL // 2], y[:, D_MODEL // 2:])
            y_ref[0, cap:, :] = jnp.zeros((COMBINE_CHUNK, D_MODEL // 2), I32)
            base += cap


def _expert_ffn(xs_groups, w_gate, w_up, w_down, first_expert, prior_outputs):
    n_f = 4
    tf = D_EXPERT // n_f
    row_chunk = 512
    n_e = xs_groups[0].shape[0]
    e0 = first_expert
    caps = tuple(xs.shape[2] for xs in xs_groups)
    rows = sum(caps)
    est = (2 * HQ_TILES * rows * LANES * 4 + rows * D_MODEL * (4 + 2 * 2) + 2 * 3 * D_MODEL * tf * 4
           + row_chunk * (3 * tf + 2 * D_MODEL) * 4)
    x_specs = [pl.BlockSpec((1, HQ_TILES, cap, LANES), lambda e, f: (e, 0, 0, 0)) for cap in caps]
    y_shapes = [(N_EXPERTS, cap + COMBINE_CHUNK, D_MODEL // 2) for cap in caps]
    y_specs = [pl.BlockSpec((1,) + s[1:], lambda e, f: (e + e0, 0, 0)) for s in y_shapes]
    prior = list(prior_outputs or [])
    n_in = len(caps) + 3
    return pl.pallas_call(
        functools.partial(_ffn_kernel, caps=caps, n_f=n_f, row_chunk=row_chunk,
                          first_expert=e0, n_prior=len(prior)),
        grid=(n_e, n_f),
        in_specs=(x_specs + [pl.BlockSpec((1, D_MODEL, tf), lambda e, f: (e + e0, 0, f)),
                             pl.BlockSpec((1, D_MODEL, tf), lambda e, f: (e + e0, 0, f)),
                             pl.BlockSpec((1, tf, D_MODEL), lambda e, f: (e + e0, f, 0))]
                  + [pl.BlockSpec(memory_space=pl.ANY) for _ in prior]),
        out_specs=y_specs,
        out_shape=[jax.ShapeDtypeStruct(s, I32) for s in y_shapes],
        input_output_aliases={n_in + j: j for j in range(len(prior))},
        scratch_shapes=[pltpu.VMEM((rows, D_MODEL), F32)],
        compiler_params=pltpu.CompilerParams(
            dimension_semantics=("arbitrary", "arbitrary"),
            vmem_limit_bytes=_vmem_limit(est + (4 << 20))),
        name="expert_ffn",
    )(*xs_groups, w_gate, w_up, w_down, *prior)


def _combine_kernel(start_ref, nch_ref, x1_ref, mod_ref, g_ref, slot_ref, y_hbm, o_ref,
                    ybuf_ref, cols_ref, ffn_ref, sem, *, first_k, kblock):
    i = pl.program_id(0)
    tm = x1_ref.shape[0]
    cur = i % 2

    def chunk_copy(e, src_row, buf, dst_row):
        return pltpu.make_async_copy(
            y_hbm.at[e, pl.ds(pl.multiple_of(src_row, SUBLANES), COMBINE_CHUNK)],
            ybuf_ref.at[buf, pl.ds(pl.multiple_of(dst_row, COMBINE_CHUNK), COMBINE_CHUNK)],
            sem.at[buf])

    def fetch(tile, buf):
        pos = jnp.int32(0)
        for e in range(N_EXPERTS):
            start = start_ref[tile, e]
            nch = nch_ref[tile, e]

            def issue(c, carry, e=e, start=start, pos=pos):
                chunk_copy(e, start + c * COMBINE_CHUNK, buf, pos + c * COMBINE_CHUNK).start()
                return carry
            lax.fori_loop(0, nch, issue, 0)
            pos = pos + nch * COMBINE_CHUNK

    @pl.when(i == 0)
    def _():
        ybuf_ref[...] = jnp.zeros_like(ybuf_ref)
        fetch(0, 0)

    @pl.when(i + 1 < pl.num_programs(0))
    def _():
        fetch(i + 1, 1 - cur)

    pos = jnp.int32(0)
    begins = []
    for e in range(N_EXPERTS):
        begins.append(pos)
        pos = pos + nch_ref[i, e] * COMBINE_CHUNK
    begins.append(pos)

    def drain(c, carry):
        chunk_copy(0, jnp.int32(0), cur, jnp.int32(0)).wait()
        return carry
    lax.fori_loop(0, pos // COMBINE_CHUNK, drain, 0)

    slots = slot_ref[...]
    for e in range(N_EXPERTS):
        s_e = slots[:, e:e + 1]
        col = jnp.where(s_e >= 0, s_e + (begins[e] - start_ref[i, e]), -1)
        cols_ref[e] = jnp.broadcast_to(col, (tm, LANES))

    lane = lax.broadcasted_iota(I32, (tm, LANES), 1)

    def selection(k0, width):
        halves = []
        for h in range(width // LANES):
            c0 = k0 + h * LANES
            target = lane + c0
            e_lo = jnp.int32(0)
            e_hi = jnp.int32(0)
            for e in range(N_EXPERTS):
                e_lo = e_lo + (begins[e + 1] <= c0).astype(I32)
                e_hi = e_hi + (begins[e] < c0 + LANES).astype(I32)

            def mark(e, hit, target=target):
                return jnp.where(cols_ref[e] == target, 1.0, hit)
            halves.append(lax.fori_loop(e_lo, e_hi, mark, jnp.zeros((tm, LANES), F32)))
        return jnp.concatenate(halves, axis=1)

    def block(k0, width):
        sel = selection(k0, width)
        lo, hi = _unpack_pair(ybuf_ref[cur, pl.ds(k0, width), :])
        return jnp.concatenate([jnp.dot(sel, lo, preferred_element_type=F32),
                                jnp.dot(sel, hi, preferred_element_type=F32)], axis=1)

    ffn_ref[...] = block(0, first_k)

    def kstep(kb, carry):
        ffn_ref[...] += block(pl.multiple_of(kb * kblock, kblock), kblock)
        return carry
    lax.fori_loop(first_k // kblock, (pos + kblock - 1) // kblock, kstep, 0)

    out = x1_ref[...] + mod_ref[0][5:6] * ffn_ref[...]
    ms = jnp.mean(out * out, axis=-1, keepdims=True)
    o_ref[...] = (out * lax.rsqrt(ms + EPS)) * g_ref[...]


def _combine(x1, mod3, mod_row, g_final, slot_t, tile_start, tile_nch, y):
    t = x1.shape[0]
    tm = TOKEN_TILE
    first_k = 2 * tm + N_EXPERTS * COMBINE_CHUNK
    kblock = 256
    max_rows = N_EXPERTS * (tm + 2 * COMBINE_CHUNK)
    max_rows = -(-max_rows // kblock) * kblock
    row = lambda i, *_: (i, 0)
    grid_spec = pltpu.PrefetchScalarGridSpec(
        num_scalar_prefetch=2,
        grid=(t // tm,),
        in_specs=[pl.BlockSpec((tm, D_MODEL), row),
                  pl.BlockSpec((1, N_MOD, D_MODEL), lambda i, *_: (mod_row(i), 0, 0)),
                  pl.BlockSpec((1, D_MODEL), lambda i, *_: (0, 0)),
                  pl.BlockSpec((tm, N_EXPERTS), row),
                  pl.BlockSpec(memory_space=pl.ANY)],
        out_specs=pl.BlockSpec((tm, D_MODEL), row),
        scratch_shapes=[pltpu.VMEM((2, max_rows, D_MODEL // 2), I32),
                        pltpu.VMEM((N_EXPERTS, tm, LANES), I32),
                        pltpu.VMEM((tm, D_MODEL), F32),
                        pltpu.SemaphoreType.DMA((2,))],
    )
    return pl.pallas_call(
        functools.partial(_combine_kernel, first_k=first_k, kblock=kblock),
        grid_spec=grid_spec,
        out_shape=jax.ShapeDtypeStruct((t, D_MODEL), F32),
        compiler_params=pltpu.CompilerParams(
            dimension_semantics=("arbitrary",),
            vmem_limit_bytes=_vmem_limit(2 * max_rows * D_MODEL * 2 + 16 * tm * D_MODEL * 4)),
        name="combine",
    )(tile_start, tile_nch, x1, mod3, g_final, slot_t, y)


def _rope_tables(n):
    rows = n // GRID_W
    row = jnp.repeat(jnp.arange(rows, dtype=F32), GRID_W)
    col = jnp.tile(jnp.arange(GRID_W, dtype=F32), rows)
    inv = ROPE_THETA ** (-jnp.arange(ROPE_FREQS, dtype=F32) / ROPE_FREQS)
    ang_r = row[:, None] * inv
    ang_c = col[:, None] * inv
    zero = jnp.zeros_like(ang_r)
    cos = jnp.concatenate([jnp.cos(ang_r)] * 2 + [jnp.cos(ang_c)] * 2, axis=1)
    sin_a = jnp.concatenate([-jnp.sin(ang_r), zero, -jnp.sin(ang_c), zero], axis=1)
    sin_b = jnp.concatenate([zero, jnp.sin(ang_r), zero, jnp.sin(ang_c)], axis=1)
    return cos, sin_a, sin_b


def _token_group(x, mod3, mod_row, seq, weights, rope_tabs, ctx_kv):
    (norm_mix, w_in_bf, sink, pool_w, pool_scale, w_out_bf, norm_ffn, w_router,
     w_gate, w_up, w_down, norm_final) = weights
    b = x.shape[0]
    t = b * seq
    x2 = x.reshape(t, D_MODEL)
    q, k, v, p = _in_projection(x2, mod3, mod_row(PROJ_TILE), norm_mix, w_in_bf, rope_tabs)
    if ctx_kv is None:
        attn = _context_attention(q, k, v, sink, seq)
    else:
        attn = _latent_attention(q, k, v, ctx_kv[0], ctx_kv[1], sink, seq)
    pooled = _pool_mixer(p, pool_w, pool_scale, seq)
    x1, h_packed, aff = _out_projection(attn, pooled, x2, mod3, mod_row(PROJ_TILE), norm_ffn,
                                        w_out_bf, w_router)

    cap = EC_FACTOR * t // N_EXPERTS
    idx, slot3, off3 = _routing(aff, cap)
    table = h_packed.reshape(-1, LANES)
    per_range = N_EXPERTS // FFN_RANGES
    xs = []
    for r in range(FFN_RANGES):
        ids = _packed_row_ids(idx[r * per_range:(r + 1) * per_range], cap)
        xs.append(_gather_rows(table, ids).reshape(per_range, HQ_TILES, cap, LANES))

    chunks_per_tile = TOKEN_TILE // LANES
    tile_off = off3[:, ::chunks_per_tile, 0]
    tile_end = jnp.concatenate([tile_off[:, 1:], jnp.full((N_EXPERTS, 1), cap, I32)], axis=1)
    tile_start = (tile_off // SUBLANES) * SUBLANES
    tile_nch = jnp.where(tile_end > tile_off,
                         (tile_end - tile_start + COMBINE_CHUNK - 1) // COMBINE_CHUNK, 0)
    slot_t = slot3.reshape(N_EXPERTS, t).T

    def finish(y):
        out = _combine(x1, mod3, mod_row(TOKEN_TILE), norm_final, slot_t, tile_start.T,
                       tile_nch.T, y)
        return out.reshape(b, seq, D_MODEL)
    return xs, finish, k, v


def kernel(x_prompt, x_sample, c, cache_k, cache_v, c_ctx, w_ada, b_ada, norm_mix, w_in,
           sink_logits, pool_w, pool_scale, w_out, norm_ffn, w_router, w_gate, w_up, w_down,
           norm_final):
    n_b, seq, _ = x_prompt.shape
    n_db, n_lat, _ = x_sample.shape
    assert 1 + n_db <= MOD_ROWS and seq == TOKEN_TILE and n_lat % PROJ_TILE == 0
    assert (n_b * seq) % PROJ_TILE == 0

    cond = jnp.concatenate(
        [c_ctx[None, :], c, jnp.zeros((MOD_ROWS - 1 - n_db, D_MODEL), F32)], axis=0)
    mod3 = _modulation(cond, w_ada[0], b_ada[0]).reshape(MOD_ROWS, N_MOD, D_MODEL)

    w_router_bf = jnp.pad(w_router[0], ((0, 0), (0, LANES - N_EXPERTS))).astype(BF16)
    weights = (norm_mix[0][None, :], w_in[0].astype(BF16), sink_logits[0], pool_w[0],
               pool_scale[0][None, :], w_out[0].astype(BF16), norm_ffn[0][None, :], w_router_bf,
               w_gate.reshape(w_gate.shape[1:]), w_up.reshape(w_up.shape[1:]),
               w_down.reshape(w_down.shape[1:]), norm_final[None, :])

    xs_p, finish_p, k_p, v_p = _token_group(
        x_prompt, mod3, lambda tile: (lambda i: 0), seq, weights, None, None)

    past = cache_k.shape[2]
    ck = cache_k[:, 0].reshape(n_db, past, KV_WIDTH)
    cv = cache_v[:, 0].reshape(n_db, past, KV_WIDTH)
    xs_l, finish_l, _, _ = _token_group(
        x_sample, mod3, lambda tile: (lambda i: 1 + i // (n_lat // tile)), n_lat, weights,
        _rope_tables(n_lat), (ck, cv))

    ys = None
    for r in range(FFN_RANGES):
        ys = _expert_ffn([xs_p[r], xs_l[r]], weights[8], weights[9], weights[10],
                         r * (N_EXPERTS // FFN_RANGES), ys)
    y_prompt = finish_p(ys[0])
    y_sample = finish_l(ys[1])

    state_k = k_p.reshape(n_b, 1, seq, N_KV_HEADS, HEAD_DIM)
    state_v = v_p.reshape(n_b, 1, seq, N_KV_HEADS, HEAD_DIM)
    return (y_prompt, y_sample, state_k, state_v)
```

```python
import functools

import jax
import jax.numpy as jnp
import numpy as np
from jax import lax
from jax.experimental import pallas as pl
from jax.experimental.pallas import tpu as pltpu
from jax.experimental.pallas import tpu_sc as plsc

F32 = jnp.float32
BF16 = jnp.bfloat16
I32 = jnp.int32

D_MODEL = 2048
N_HEADS = 8
N_KV_HEADS = 2
HEAD_DIM = 128
Q_PER_KV = N_HEADS // N_KV_HEADS
ATTN_WIDTH = N_HEADS * HEAD_DIM
KV_WIDTH = N_KV_HEADS * HEAD_DIM
POOL_WIDTH = D_MODEL - ATTN_WIDTH
POOL_SIZES = (2, 4, 8, 16)
POOL_GROUP = POOL_WIDTH // len(POOL_SIZES)
IN_WIDTH = ATTN_WIDTH + 2 * KV_WIDTH + POOL_WIDTH
WINDOW = 128
BLOCK = 128
GRID_W = 64
ROPE_THETA = 10000.0
ROPE_FREQS = HEAD_DIM // 4
N_EXPERTS = 16
EC_FACTOR = 2
D_EXPERT = 1024
N_MOD = 6
EPS = 1e-6
NEG = -1e30
ATTN_SCALE = HEAD_DIM ** -0.5

LANES = 128
SUBLANES = 8
BF16_ROWS = 16
VMEM_CAP = 64 * 1024 * 1024
SC_CORES = 2
SC_SUBCORES = 16

MOD_ROWS = 8
TOKEN_TILE = 256
PROJ_TILE = 512
SUB_TILE = 256
HQ_TILES = D_MODEL // 2 // LANES + 1
GATHER_CHUNK = 128
FFN_RANGES = 2
POOL_HALO = 8
COMBINE_CHUNK = BF16_ROWS


def _vmem_limit(nbytes):
    return int(min(VMEM_CAP - (4 << 20), max(nbytes, 16 << 20)))


def _mod_kernel(c_ref, w_ref, b_ref, o_ref):
    c = c_ref[...]
    s = c * jax.nn.sigmoid(c)
    o_ref[...] = jnp.dot(s.astype(BF16), w_ref[...].astype(BF16),
                         preferred_element_type=F32) + b_ref[...]


def _modulation(cond, w_ada, b_ada):
    n = w_ada.shape[1]
    tn = 1024
    return pl.pallas_call(
        _mod_kernel,
        grid=(n // tn,),
        in_specs=[pl.BlockSpec((MOD_ROWS, D_MODEL), lambda j: (0, 0)),
                  pl.BlockSpec((D_MODEL, tn), lambda j: (0, j)),
                  pl.BlockSpec((1, tn), lambda j: (0, j))],
        out_specs=pl.BlockSpec((MOD_ROWS, tn), lambda j: (0, j)),
        out_shape=jax.ShapeDtypeStruct((MOD_ROWS, n), F32),
        compiler_params=pltpu.CompilerParams(
            vmem_limit_bytes=_vmem_limit(3 * D_MODEL * tn * 4)),
        name="modulation",
    )(cond, w_ada, b_ada.reshape(1, n))


def _norm_mod(x, g, shift, scale):
    ms = jnp.mean(x * x, axis=-1, keepdims=True)
    y = x * lax.rsqrt(ms + EPS)
    return (y * g) * (1.0 + scale) + shift


def _inproj_kernel(*refs, rope):
    if rope:
        x_ref, mod_ref, g_ref, w_ref, cos_ref, sa_ref, sb_ref, q_ref, k_ref, v_ref, p_ref = refs
    else:
        x_ref, mod_ref, g_ref, w_ref, q_ref, k_ref, v_ref, p_ref, ks_ref, vs_ref = refs
    mod = mod_ref[0]
    for s in range(x_ref.shape[0] // SUB_TILE):
        rows = slice(s * SUB_TILE, (s + 1) * SUB_TILE)
        h = _norm_mod(x_ref[rows, :], g_ref[...], mod[0:1], mod[1:2])
        u = jnp.dot(h.astype(BF16), w_ref[...], preferred_element_type=F32)

        def rot(xh, rows=rows):
            return (xh * cos_ref[rows, :] + pltpu.roll(xh, LANES - ROPE_FREQS, 1) * sa_ref[rows, :]
                    + pltpu.roll(xh, ROPE_FREQS, 1) * sb_ref[rows, :])

        for hd in range(N_HEADS):
            xh = u[:, hd * HEAD_DIM:(hd + 1) * HEAD_DIM] * ATTN_SCALE
            q_ref[rows, hd * HEAD_DIM:(hd + 1) * HEAD_DIM] = (rot(xh) if rope else xh).astype(BF16)
        for hd in range(N_KV_HEADS):
            lo = ATTN_WIDTH + hd * HEAD_DIM
            xh = u[:, lo:lo + HEAD_DIM]
            k_ref[rows, hd * HEAD_DIM:(hd + 1) * HEAD_DIM] = rot(xh) if rope else xh
        v_ref[rows, :] = u[:, ATTN_WIDTH + KV_WIDTH:ATTN_WIDTH + 2 * KV_WIDTH]
        p_ref[rows, :] = u[:, ATTN_WIDTH + 2 * KV_WIDTH:]
        if not rope:
            for hd in range(N_KV_HEADS):
                state_rows = pl.ds(s * SUB_TILE * N_KV_HEADS + hd, SUB_TILE, stride=N_KV_HEADS)
                lo = ATTN_WIDTH + hd * HEAD_DIM
                ks_ref[state_rows, :] = u[:, lo:lo + HEAD_DIM]
                vs_ref[state_rows, :] = u[:, lo + KV_WIDTH:lo + KV_WIDTH + HEAD_DIM]


def _in_projection(x2, mod3, mod_row, g, w_in_bf, rope_tabs):
    t = x2.shape[0]
    tm = PROJ_TILE
    rope = rope_tabs is not None
    row = lambda i: (i, 0)
    in_specs = [pl.BlockSpec((tm, D_MODEL), row),
                pl.BlockSpec((1, N_MOD, D_MODEL), lambda i: (mod_row(i), 0, 0)),
                pl.BlockSpec((1, D_MODEL), lambda i: (0, 0)),
                pl.BlockSpec((D_MODEL, IN_WIDTH), lambda i: (0, 0))]
    args = [x2, mod3, g, w_in_bf]
    if rope:
        n_seq = rope_tabs[0].shape[0]
        seq_blocks = n_seq // tm
        for tab in rope_tabs:
            in_specs.append(pl.BlockSpec((tm, HEAD_DIM), lambda i: (i % seq_blocks, 0)))
            args.append(tab)
    out_specs = [pl.BlockSpec((tm, ATTN_WIDTH), row),
                 pl.BlockSpec((tm, KV_WIDTH), row),
                 pl.BlockSpec((tm, KV_WIDTH), row),
                 pl.BlockSpec((tm, POOL_WIDTH), row)]
    out_shape = [jax.ShapeDtypeStruct((t, ATTN_WIDTH), BF16),
                 jax.ShapeDtypeStruct((t, KV_WIDTH), F32),
                 jax.ShapeDtypeStruct((t, KV_WIDTH), F32),
                 jax.ShapeDtypeStruct((t, POOL_WIDTH), F32)]
    if not rope:
        for _ in range(2):
            out_specs.append(pl.BlockSpec((tm * N_KV_HEADS, HEAD_DIM), row))
            out_shape.append(jax.ShapeDtypeStruct((t * N_KV_HEADS, HEAD_DIM), F32))
    return pl.pallas_call(
        functools.partial(_inproj_kernel, rope=rope),
        grid=(t // tm,),
        in_specs=in_specs,
        out_specs=out_specs,
        out_shape=out_shape,
        compiler_params=pltpu.CompilerParams(
            vmem_limit_bytes=_vmem_limit(2 * D_MODEL * IN_WIDTH * 2 + 24 * tm * D_MODEL * 4)),
        name="in_projection",
    )(*args)


def _softmax_pv(s_list, v_list, sink_col):
    m = sink_col
    for s in s_list:
        m = jnp.maximum(m, jnp.max(s, axis=-1, keepdims=True))
    denom = jnp.exp(sink_col - m)
    out = None
    for s, v in zip(s_list, v_list):
        e = jnp.exp(s - m)
        denom = denom + jnp.sum(e, axis=-1, keepdims=True)
        o = jnp.dot(e.astype(BF16), v, preferred_element_type=F32)
        out = o if out is None else out + o
    return out * (1.0 / denom)


def _stack_heads(q, kv):
    return jnp.concatenate(
        [q[:, (kv * Q_PER_KV + g) * HEAD_DIM:(kv * Q_PER_KV + g + 1) * HEAD_DIM]
         for g in range(Q_PER_KV)], axis=0)


def _sink_column(sink_ref, kv, rows):
    r = lax.broadcasted_iota(I32, (Q_PER_KV * rows, 1), 0)
    col = jnp.zeros((Q_PER_KV * rows, 1), F32)
    for g in range(Q_PER_KV):
        col = jnp.where((r >= g * rows) & (r < (g + 1) * rows), sink_ref[kv * Q_PER_KV + g], col)
    return col


def _qk(q, k):
    return lax.dot_general(q, k, (((1,), (1,)), ((), ())), preferred_element_type=F32)


def _ctx_attn_kernel(sink_ref, q_ref, k_ref, v_ref, o_ref):
    rows = q_ref.shape[0]
    q = q_ref[...]
    for kv in range(N_KV_HEADS):
        kh = k_ref[:, kv * HEAD_DIM:(kv + 1) * HEAD_DIM].astype(BF16)
        vh = v_ref[:, kv * HEAD_DIM:(kv + 1) * HEAD_DIM].astype(BF16)
        qs = _stack_heads(q, kv)
        s = _qk(qs, kh)
        o = _softmax_pv([s], [vh], _sink_column(sink_ref, kv, rows))
        for g in range(Q_PER_KV):
            hd = kv * Q_PER_KV + g
            o_ref[:, hd * HEAD_DIM:(hd + 1) * HEAD_DIM] = o[g * rows:(g + 1) * rows].astype(BF16)


def _context_attention(q, k, v, sink, seq):
    t = q.shape[0]
    row = lambda b: (b, 0)
    return pl.pallas_call(
        _ctx_attn_kernel,
        grid=(t // seq,),
        in_specs=[pl.BlockSpec(memory_space=pltpu.SMEM),
                  pl.BlockSpec((seq, ATTN_WIDTH), row),
                  pl.BlockSpec((seq, KV_WIDTH), row),
                  pl.BlockSpec((seq, KV_WIDTH), row)],
        out_specs=pl.BlockSpec((seq, ATTN_WIDTH), row),
        out_shape=jax.ShapeDtypeStruct((t, ATTN_WIDTH), BF16),
        name="context_attention",
    )(sink, q, k, v)


def _lat_attn_kernel(sink_ref, q_ref, k_ref, v_ref, ck_ref, cv_ref, o_ref, *, n_seq):
    i = pl.program_id(1)
    band = 3 * BLOCK
    start = pl.multiple_of(jnp.clip((i - 1) * BLOCK, 0, n_seq - band), BLOCK)
    rows = Q_PER_KV * BLOCK
    qpos = i * BLOCK + lax.broadcasted_iota(I32, (rows, band), 0) % BLOCK
    kpos = start + lax.broadcasted_iota(I32, (rows, band), 1)
    mask = jnp.abs(kpos - qpos) <= WINDOW
    q = q_ref[...]
    for kv in range(N_KV_HEADS):
        cols = slice(kv * HEAD_DIM, (kv + 1) * HEAD_DIM)
        kb = k_ref[pl.ds(start, band), cols].astype(BF16)
        vb = v_ref[pl.ds(start, band), cols].astype(BF16)
        ck = ck_ref[0, :, cols].astype(BF16)
        cv = cv_ref[0, :, cols].astype(BF16)
        qs = _stack_heads(q, kv)
        s_loc = jnp.where(mask, _qk(qs, kb), NEG)
        s_ctx = _qk(qs, ck)
        o = _softmax_pv([s_loc, s_ctx], [vb, cv], _sink_column(sink_ref, kv, BLOCK))
        for g in range(Q_PER_KV):
            hd = kv * Q_PER_KV + g
            o_ref[:, hd * HEAD_DIM:(hd + 1) * HEAD_DIM] = o[g * BLOCK:(g + 1) * BLOCK].astype(BF16)


def _latent_attention(q, k, v, ck, cv, sink, n_seq):
    t = q.shape[0]
    nb = n_seq // BLOCK
    past = ck.shape[1]
    return pl.pallas_call(
        functools.partial(_lat_attn_kernel, n_seq=n_seq),
        grid=(t // n_seq, nb),
        in_specs=[pl.BlockSpec(memory_space=pltpu.SMEM),
                  pl.BlockSpec((BLOCK, ATTN_WIDTH), lambda b, i: (b * nb + i, 0)),
                  pl.BlockSpec((n_seq, KV_WIDTH), lambda b, i: (b, 0)),
                  pl.BlockSpec((n_seq, KV_WIDTH), lambda b, i: (b, 0)),
                  pl.BlockSpec((1, past, KV_WIDTH), lambda b, i: (b, 0, 0)),
                  pl.BlockSpec((1, past, KV_WIDTH), lambda b, i: (b, 0, 0))],
        out_specs=pl.BlockSpec((BLOCK, ATTN_WIDTH), lambda b, i: (b * nb + i, 0)),
        out_shape=jax.ShapeDtypeStruct((t, ATTN_WIDTH), BF16),
        name="latent_attention",
    )(sink, q, k, v, ck, cv)


def _pool_subtile(p_ref, r0, seq, w_ref, s_ref):
    n = SUB_TILE
    rows = n + 2 * POOL_HALO
    static = isinstance(r0, int)
    t0 = r0 % seq
    t = t0 + lax.broadcasted_iota(I32, (n, 1), 0)
    has_top = t0 > 0
    has_bottom = t0 + n < seq
    zeros = jnp.zeros((POOL_HALO, POOL_GROUP), F32)
    outs = []
    for g, w in enumerate(POOL_SIZES):
        cols = slice(g * POOL_GROUP, (g + 1) * POOL_GROUP)
        pg = p_ref[pl.ds(r0, n), cols]
        if static:
            top = p_ref[r0 - POOL_HALO:r0, cols] if has_top else zeros
            bottom = p_ref[r0 + n:r0 + n + POOL_HALO, cols] if has_bottom else zeros
        else:
            top_row = pl.multiple_of(jnp.maximum(r0 - POOL_HALO, 0), POOL_HALO)
            bottom_row = pl.multiple_of(jnp.minimum(r0 + n, p_ref.shape[0] - POOL_HALO), POOL_HALO)
            top = jnp.where(has_top, p_ref[pl.ds(top_row, POOL_HALO), cols], 0.0)
            bottom = jnp.where(has_bottom, p_ref[pl.ds(bottom_row, POOL_HALO), cols], 0.0)
        x = jnp.concatenate([top, pg, bottom], axis=0)
        acc = x + pltpu.roll(x, 1, 0)
        step = 1
        while 2 * step < w:
            acc = pltpu.roll(acc, step, 0) + pltpu.roll(acc, rows - step, 0)
            step *= 2
        wsum = acc[POOL_HALO:POOL_HALO + n]
        lo = jnp.maximum(t - w // 2, 0)
        hi = jnp.minimum(t + w - w // 2, seq)
        inv_cnt = 1.0 / (hi - lo).astype(F32)
        mixed = wsum * inv_cnt - pg
        y = jnp.dot(mixed.astype(BF16), w_ref[g].astype(BF16), preferred_element_type=F32)
        outs.append((y * s_ref[:, cols]).astype(BF16))
    return jnp.concatenate(outs, axis=1)


def _pack_pair(lo, hi):
    return lax.bitcast_convert_type(pltpu.pack_elementwise([lo, hi], packed_dtype=BF16), I32)


def _unpack_pair(words):
    lo = pltpu.unpack_elementwise(words, index=0, packed_dtype=BF16, unpacked_dtype=F32)
    hi = pltpu.unpack_elementwise(words, index=1, packed_dtype=BF16, unpacked_dtype=F32)
    return lo, hi


def _outproj_kernel(a_ref, p_ref, x_ref, mod_ref, g_ref, wo_ref, wr_ref, pw_ref, ps_ref,
                    x1_ref, h_ref, aff_ref, *, seq):
    mod = mod_ref[0]
    groups = SUB_TILE // SUBLANES
    half = D_MODEL // 2
    tm = x_ref.shape[0]
    steps_per_p_block = p_ref.shape[0] // tm
    for s in range(tm // SUB_TILE):
        rows = slice(s * SUB_TILE, (s + 1) * SUB_TILE)
        grp = slice(s * groups, (s + 1) * groups)
        r0 = s * SUB_TILE
        if steps_per_p_block > 1:
            r0 = pl.multiple_of((pl.program_id(0) % steps_per_p_block) * tm + r0, SUB_TILE)
        pooled = _pool_subtile(p_ref, r0, seq, pw_ref, ps_ref)
        mix = (jnp.dot(a_ref[rows, :], wo_ref[0:ATTN_WIDTH, :], preferred_element_type=F32)
               + jnp.dot(pooled, wo_ref[ATTN_WIDTH:D_MODEL, :], preferred_element_type=F32))
        x1 = x_ref[rows, :] + mod[2:3] * mix
        x1_ref[rows, :] = x1
        h = _norm_mod(x1, g_ref[...], mod[3:4], mod[4:5])
        logits = jnp.dot(h.astype(BF16), wr_ref[...], preferred_element_type=F32)
        lane = lax.broadcasted_iota(I32, logits.shape, 1)
        logits = jnp.where(lane < N_EXPERTS, logits, -jnp.inf)
        m = jnp.max(logits, axis=-1, keepdims=True)
        e = jnp.exp(logits - m)
        aff = e / jnp.sum(e, axis=-1, keepdims=True)
        aff_ref[rows, :] = aff[:, 0:N_EXPERTS]
        for c in range(HQ_TILES - 1):
            words = _pack_pair(h[:, c * LANES:(c + 1) * LANES],
                               h[:, half + c * LANES:half + (c + 1) * LANES])
            h_ref[grp, c * SUBLANES:(c + 1) * SUBLANES, :] = (
                words.reshape(groups, SUBLANES, LANES))
        h_ref[grp, (HQ_TILES - 1) * SUBLANES:, :] = (
            lax.bitcast_convert_type(aff, I32).reshape(groups, SUBLANES, LANES))


def _out_projection(attn, p, x2, mod3, mod_row, g, w_out_bf, w_router, pool_w, pool_scale, seq):
    t = x2.shape[0]
    tm = PROJ_TILE
    row = lambda i: (i, 0)
    p_rows = max(tm, seq)
    steps_per_p_block = p_rows // tm
    return pl.pallas_call(
        functools.partial(_outproj_kernel, seq=seq),
        grid=(t // tm,),
        in_specs=[pl.BlockSpec((tm, ATTN_WIDTH), row),
                  pl.BlockSpec((p_rows, POOL_WIDTH), lambda i: (i // steps_per_p_block, 0)),
                  pl.BlockSpec((tm, D_MODEL), row),
                  pl.BlockSpec((1, N_MOD, D_MODEL), lambda i: (mod_row(i), 0, 0)),
                  pl.BlockSpec((1, D_MODEL), lambda i: (0, 0)),
                  pl.BlockSpec((D_MODEL, D_MODEL), lambda i: (0, 0)),
                  pl.BlockSpec((D_MODEL, LANES), lambda i: (0, 0)),
                  pl.BlockSpec((len(POOL_SIZES), POOL_GROUP, POOL_GROUP), lambda i: (0, 0, 0)),
                  pl.BlockSpec((1, POOL_WIDTH), lambda i: (0, 0))],
        out_specs=[pl.BlockSpec((tm, D_MODEL), row),
                   pl.BlockSpec((tm // SUBLANES, HQ_TILES * SUBLANES, LANES), lambda i: (i, 0, 0)),
                   pl.BlockSpec((tm, N_EXPERTS), row)],
        out_shape=[jax.ShapeDtypeStruct((t, D_MODEL), F32),
                   jax.ShapeDtypeStruct((t // SUBLANES, HQ_TILES * SUBLANES, LANES), I32),
                   jax.ShapeDtypeStruct((t, N_EXPERTS), F32)],
        compiler_params=pltpu.CompilerParams(
            vmem_limit_bytes=_vmem_limit(2 * D_MODEL * D_MODEL * 2 + 24 * tm * D_MODEL * 4)),
        name="out_projection",
    )(attn, p, x2, mod3, g, w_out_bf, w_router, pool_w, pool_scale)


def _route_kernel(a_ref, idx_ref, slot_ref, off_ref, *, cap):
    a = a_ref[...]
    n_e, n_c, _ = a.shape
    rows = n_e * n_c

    def bisect(it, thr):
        cand = thr | jnp.left_shift(jnp.int32(1), 30 - it)
        cand_f = lax.bitcast_convert_type(cand, F32)
        cnt = jnp.sum(jnp.sum((a >= cand_f).astype(F32), axis=2, keepdims=True),
                      axis=1, keepdims=True)
        return jnp.where(cnt >= cap, cand, thr)

    thr = lax.fori_loop(0, 31, bisect, jnp.zeros((n_e, 1, 1), I32))
    thr_f = lax.bitcast_convert_type(thr, F32)
    gt = (a > thr_f).astype(F32).reshape(rows, LANES)
    eq = (a == thr_f).astype(F32).reshape(rows, LANES)

    li = lax.broadcasted_iota(I32, (LANES, LANES), 0)
    lj = lax.broadcasted_iota(I32, (LANES, LANES), 1)
    upper_incl = (li <= lj).astype(BF16)
    ri = lax.broadcasted_iota(I32, (rows, rows), 0)
    rj = lax.broadcasted_iota(I32, (rows, rows), 1)
    same_expert = (ri // n_c) == (rj // n_c)
    before = (same_expert & (rj < ri)).astype(BF16)
    whole = same_expert.astype(BF16)

    def lane_bcast(col):
        return jnp.broadcast_to(col, (rows, LANES)).astype(BF16)

    def prefix(x):
        incl = jnp.dot(x.astype(BF16), upper_incl, preferred_element_type=F32)
        tot = incl[:, LANES - 1:LANES]
        off = jnp.dot(before, lane_bcast(tot), preferred_element_type=F32)
        return incl, tot, off

    n_gt = jnp.dot(whole, lane_bcast(jnp.sum(gt, axis=1, keepdims=True)),
                   preferred_element_type=F32)
    need = cap - n_gt
    incl_eq, _, off_eq = prefix(eq)
    rank_eq = off_eq + incl_eq - eq
    sel = jnp.where((eq > 0) & (rank_eq < need), 1.0, gt)
    incl, tot, off = prefix(sel)
    slot = off + incl - sel
    slot_ref[...] = jnp.where(sel > 0, slot, -1.0).astype(I32).reshape(n_e, n_c, LANES)
    off_ref[...] = off.astype(I32).reshape(n_e, n_c, LANES)

    s_lane = lax.broadcasted_iota(I32, (1, cap), 1).astype(F32)
    c_col = lax.broadcasted_iota(I32, (n_c, 1), 0).astype(F32)
    for e in range(n_e):
        r0 = e * n_c
        incl_e = incl[r0:r0 + n_c]
        off_e = off[r0:r0 + n_c, 0:1]
        tot_e = tot[r0:r0 + n_c]
        onehot = ((off_e <= s_lane) & (s_lane < off_e + tot_e)).astype(F32)
        counts = lax.dot_general(incl_e.astype(BF16), onehot.astype(BF16),
                                 (((0,), (0,)), ((), ())), preferred_element_type=F32)
        local = s_lane - jnp.sum(onehot * off_e, axis=0, keepdims=True)
        lane = jnp.sum((counts <= local).astype(F32), axis=0, keepdims=True)
        chunk = jnp.sum(onehot * c_col, axis=0, keepdims=True)
        idx_ref[e] = (chunk * LANES + lane).astype(I32)


def _routing(aff, cap):
    t = aff.shape[0]
    n_c = t // LANES
    a3 = aff.T.reshape(N_EXPERTS, n_c, LANES)
    return pl.pallas_call(
        functools.partial(_route_kernel, cap=cap),
        out_shape=[jax.ShapeDtypeStruct((N_EXPERTS, 1, cap), I32),
                   jax.ShapeDtypeStruct((N_EXPERTS, n_c, LANES), I32),
                   jax.ShapeDtypeStruct((N_EXPERTS, n_c, LANES), I32)],
        compiler_params=pltpu.CompilerParams(vmem_limit_bytes=_vmem_limit(48 << 20)),
        name="routing",
    )(a3)


def _gather_rows(table, row_ids):
    n_chunks = row_ids.shape[0]
    n_workers = SC_CORES * SC_SUBCORES
    per_worker = n_chunks // n_workers
    assert row_ids.shape[1] == GATHER_CHUNK and n_chunks % n_workers == 0
    mesh = plsc.VectorSubcoreMesh(core_axis_name="core", subcore_axis_name="subcore")

    @functools.partial(
        pl.kernel, mesh=mesh,
        out_type=jax.ShapeDtypeStruct((n_chunks * GATHER_CHUNK, LANES), I32),
        scratch_types=[pltpu.VMEM((GATHER_CHUNK,), I32),
                       pltpu.VMEM((GATHER_CHUNK, LANES), I32),
                       pltpu.SemaphoreType.DMA],
        name="gather_rows",
    )
    def gather(table_hbm, ids_hbm, out_hbm, ids_v, rows_v, sem):
        worker = lax.axis_index("subcore") * SC_CORES + lax.axis_index("core")

        @pl.loop(0, per_worker)
        def _(j):
            chunk = worker * per_worker + j
            pltpu.sync_copy(ids_hbm.at[chunk], ids_v)
            pltpu.async_copy(table_hbm.at[ids_v], rows_v, sem).wait()
            pltpu.sync_copy(
                rows_v, out_hbm.at[pl.ds(pl.multiple_of(chunk * GATHER_CHUNK, GATHER_CHUNK),
                                         GATHER_CHUNK)])

    return gather(table, row_ids)


def _packed_row_ids(idx, cap):
    tok = idx.reshape(-1, 1, cap // GATHER_CHUNK, GATHER_CHUNK)
    tile = jnp.arange(HQ_TILES, dtype=I32).reshape(1, HQ_TILES, 1, 1)
    ids = ((tok // SUBLANES) * HQ_TILES + tile) * SUBLANES + tok % SUBLANES
    return ids.reshape(-1, GATHER_CHUNK)


def _ffn_kernel(*refs, caps, n_f, row_chunk, first_expert, n_prior):
    n_g = len(caps)
    x_refs = refs[:n_g]
    wg_ref, wu_ref, wd_ref = refs[n_g:n_g + 3]
    y_refs = refs[n_g + 3 + n_prior:2 * n_g + 3 + n_prior]
    acc_ref = refs[2 * n_g + 3 + n_prior]
    e = pl.program_id(0)
    f = pl.program_id(1)

    @pl.when((e == 0) & (f == 0))
    def _():
        acc_ref[...] = jnp.zeros_like(acc_ref)

    first = f == 0
    base = 0
    for x_ref, cap in zip(x_refs, caps):
        for r in range(cap // row_chunk):
            rs = slice(r * row_chunk, (r + 1) * row_chunk)
            acc_rows = slice(base + r * row_chunk, base + (r + 1) * row_chunk)
            pairs = [_unpack_pair(x_ref[0, c, rs, :]) for c in range(HQ_TILES - 1)]
            x = jnp.concatenate([p[0] for p in pairs] + [p[1] for p in pairs], axis=1)
            gate_act = jnp.dot(x, wg_ref[0], preferred_element_type=F32)
            up = jnp.dot(x, wu_ref[0], preferred_element_type=F32)
            hid = (gate_act * jax.nn.sigmoid(gate_act)) * up
            part = jnp.dot(hid, wd_ref[0], preferred_element_type=F32)
            acc_ref[acc_rows, :] = part + jnp.where(first, 0.0, acc_ref[acc_rows, :])
        base += cap

    @pl.when(f == n_f - 1)
    def _():
        base = 0
        for x_ref, y_ref, cap in zip(x_refs, y_refs, caps):
            aff = lax.bitcast_convert_type(x_ref[0, HQ_TILES - 1], F32)
            lane = lax.broadcasted_iota(I32, aff.shape, 1)
            gates = jnp.sum(jnp.where(lane == e + first_expert, aff, 0.0), axis=1, keepdims=True)
            y = acc_ref[base:base + cap, :] * gates
            y_ref[0, 0:cap, :] = _pack_pair(y[:, 0:D_MODEL // 2], y[:, D_MODEL // 2:])
            y_ref[0, cap:, :] = jnp.zeros((COMBINE_CHUNK, D_MODEL // 2), I32)
            base += cap


def _expert_ffn(xs_groups, w_gate, w_up, w_down, first_expert, prior_outputs):
    n_f = 4
    tf = D_EXPERT // n_f
    row_chunk = 512
    n_e = xs_groups[0].shape[0]
    e0 = first_expert
    caps = tuple(xs.shape[2] for xs in xs_groups)
    rows = sum(caps)
    est = (2 * HQ_TILES * rows * LANES * 4 + rows * D_MODEL * (4 + 2 * 2) + 2 * 3 * D_MODEL * tf * 4
           + row_chunk * (3 * tf + 2 * D_MODEL) * 4)
    x_specs = [pl.BlockSpec((1, HQ_TILES, cap, LANES), lambda e, f: (e, 0, 0, 0)) for cap in caps]
    y_shapes = [(N_EXPERTS, cap + COMBINE_CHUNK, D_MODEL // 2) for cap in caps]
    y_specs = [pl.BlockSpec((1,) + s[1:], lambda e, f: (e + e0, 0, 0)) for s in y_shapes]
    prior = list(prior_outputs or [])
    n_in = len(caps) + 3
    return pl.pallas_call(
        functools.partial(_ffn_kernel, caps=caps, n_f=n_f, row_chunk=row_chunk,
                          first_expert=e0, n_prior=len(prior)),
        grid=(n_e, n_f),
        in_specs=(x_specs + [pl.BlockSpec((1, D_MODEL, tf), lambda e, f: (e + e0, 0, f)),
                             pl.BlockSpec((1, D_MODEL, tf), lambda e, f: (e + e0, 0, f)),
                             pl.BlockSpec((1, tf, D_MODEL), lambda e, f: (e + e0, f, 0))]
                  + [pl.BlockSpec(memory_space=pl.ANY) for _ in prior]),
        out_specs=y_specs,
        out_shape=[jax.ShapeDtypeStruct(s, I32) for s in y_shapes],
        input_output_aliases={n_in + j: j for j in range(len(prior))},
        scratch_shapes=[pltpu.VMEM((rows, D_MODEL), F32)],
        compiler_params=pltpu.CompilerParams(
            dimension_semantics=("arbitrary", "arbitrary"),
            vmem_limit_bytes=_vmem_limit(est + (4 << 20))),
        name="expert_ffn",
    )(*xs_groups, w_gate, w_up, w_down, *prior)


def _combine_kernel(start_ref, nch_ref, x1_ref, mod_ref, g_ref, slot_ref, y_hbm, o_ref,
                    ybuf_ref, cols_ref, ffn_ref, sem, *, first_k, kblock):
    i = pl.program_id(0)
    tm = x1_ref.shape[0]
    cur = i % 2

    def chunk_copy(e, src_row, buf, dst_row):
        return pltpu.make_async_copy(
            y_hbm.at[e, pl.ds(pl.multiple_of(src_row, SUBLANES), COMBINE_CHUNK)],
            ybuf_ref.at[buf, pl.ds(pl.multiple_of(dst_row, COMBINE_CHUNK), COMBINE_CHUNK)],
            sem.at[buf])

    def fetch(tile, buf):
        pos = jnp.int32(0)
        for e in range(N_EXPERTS):
            start = start_ref[tile, e]
            nch = nch_ref[tile, e]

            def issue(c, carry, e=e, start=start, pos=pos):
                chunk_copy(e, start + c * COMBINE_CHUNK, buf, pos + c * COMBINE_CHUNK).start()
                return carry
            lax.fori_loop(0, nch, issue, 0)
            pos = pos + nch * COMBINE_CHUNK

    @pl.when(i == 0)
    def _():
        ybuf_ref[...] = jnp.zeros_like(ybuf_ref)
        fetch(0, 0)

    @pl.when(i + 1 < pl.num_programs(0))
    def _():
        fetch(i + 1, 1 - cur)

    pos = jnp.int32(0)
    begins = []
    for e in range(N_EXPERTS):
        begins.append(pos)
        pos = pos + nch_ref[i, e] * COMBINE_CHUNK
    begins.append(pos)

    def drain(c, carry):
        chunk_copy(0, jnp.int32(0), cur, jnp.int32(0)).wait()
        return carry
    lax.fori_loop(0, pos // COMBINE_CHUNK, drain, 0)

    slots = slot_ref[...]
    for e in range(N_EXPERTS):
        s_e = slots[:, e:e + 1]
        col = jnp.where(s_e >= 0, s_e + (begins[e] - start_ref[i, e]), -1)
        cols_ref[e] = jnp.broadcast_to(col, (tm, LANES))

    lane = lax.broadcasted_iota(I32, (tm, LANES), 1)

    def selection(k0, width):
        halves = []
        for h in range(width // LANES):
            c0 = k0 + h * LANES
            target = lane + c0
            e_lo = jnp.int32(0)
            e_hi = jnp.int32(0)
            for e in range(N_EXPERTS):
                e_lo = e_lo + (begins[e + 1] <= c0).astype(I32)
                e_hi = e_hi + (begins[e] < c0 + LANES).astype(I32)

            def mark(e, hit, target=target):
                return jnp.where(cols_ref[e] == target, 1.0, hit)
            halves.append(lax.fori_loop(e_lo, e_hi, mark, jnp.zeros((tm, LANES), F32)))
        return jnp.concatenate(halves, axis=1)

    def block(k0, width):
        sel = selection(k0, width)
        lo, hi = _unpack_pair(ybuf_ref[cur, pl.ds(k0, width), :])
        return jnp.concatenate([jnp.dot(sel, lo, preferred_element_type=F32),
                                jnp.dot(sel, hi, preferred_element_type=F32)], axis=1)

    ffn_ref[...] = block(0, first_k)

    def kstep(kb, carry):
        ffn_ref[...] += block(pl.multiple_of(kb * kblock, kblock), kblock)
        return carry
    lax.fori_loop(first_k // kblock, (pos + kblock - 1) // kblock, kstep, 0)

    out = x1_ref[...] + mod_ref[0][5:6] * ffn_ref[...]
    ms = jnp.mean(out * out, axis=-1, keepdims=True)
    o_ref[...] = (out * lax.rsqrt(ms + EPS)) * g_ref[...]


def _combine(x1, mod3, mod_row, g_final, slot_t, tile_start, tile_nch, y):
    t = x1.shape[0]
    tm = TOKEN_TILE
    first_k = 2 * tm + N_EXPERTS * COMBINE_CHUNK
    kblock = 256
    max_rows = N_EXPERTS * (tm + 2 * COMBINE_CHUNK)
    max_rows = -(-max_rows // kblock) * kblock
    row = lambda i, *_: (i, 0)
    grid_spec = pltpu.PrefetchScalarGridSpec(
        num_scalar_prefetch=2,
        grid=(t // tm,),
        in_specs=[pl.BlockSpec((tm, D_MODEL), row),
                  pl.BlockSpec((1, N_MOD, D_MODEL), lambda i, *_: (mod_row(i), 0, 0)),
                  pl.BlockSpec((1, D_MODEL), lambda i, *_: (0, 0)),
                  pl.BlockSpec((tm, N_EXPERTS), row),
                  pl.BlockSpec(memory_space=pl.ANY)],
        out_specs=pl.BlockSpec((tm, D_MODEL), row),
        scratch_shapes=[pltpu.VMEM((2, max_rows, D_MODEL // 2), I32),
                        pltpu.VMEM((N_EXPERTS, tm, LANES), I32),
                        pltpu.VMEM((tm, D_MODEL), F32),
                        pltpu.SemaphoreType.DMA((2,))],
    )
    return pl.pallas_call(
        functools.partial(_combine_kernel, first_k=first_k, kblock=kblock),
        grid_spec=grid_spec,
        out_shape=jax.ShapeDtypeStruct((t, D_MODEL), F32),
        compiler_params=pltpu.CompilerParams(
            dimension_semantics=("arbitrary",),
            vmem_limit_bytes=_vmem_limit(2 * max_rows * D_MODEL * 2 + 16 * tm * D_MODEL * 4)),
        name="combine",
    )(tile_start, tile_nch, x1, mod3, g_final, slot_t, y)


def _rope_tables(n):
    rows = n // GRID_W
    row = jnp.repeat(jnp.arange(rows, dtype=F32), GRID_W)
    col = jnp.tile(jnp.arange(GRID_W, dtype=F32), rows)
    inv = ROPE_THETA ** (-jnp.arange(ROPE_FREQS, dtype=F32) / ROPE_FREQS)
    ang_r = row[:, None] * inv
    ang_c = col[:, None] * inv
    zero = jnp.zeros_like(ang_r)
    cos = jnp.concatenate([jnp.cos(ang_r)] * 2 + [jnp.cos(ang_c)] * 2, axis=1)
    sin_a = jnp.concatenate([-jnp.sin(ang_r), zero, -jnp.sin(ang_c), zero], axis=1)
    sin_b = jnp.concatenate([zero, jnp.sin(ang_r), zero, jnp.sin(ang_c)], axis=1)
    return cos, sin_a, sin_b


def _token_group(x, mod3, mod_row, seq, weights, rope_tabs, ctx_kv):
    (norm_mix, w_in_bf, sink, pool_w, pool_scale, w_out_bf, norm_ffn, w_router,
     w_gate, w_up, w_down, norm_final) = weights
    b = x.shape[0]
    t = b * seq
    x2 = x.reshape(t, D_MODEL)
    q, k, v, p, *state = _in_projection(x2, mod3, mod_row(PROJ_TILE), norm_mix, w_in_bf,
                                        rope_tabs)
    if ctx_kv is None:
        attn = _context_attention(q, k, v, sink, seq)
    else:
        attn = _latent_attention(q, k, v, ctx_kv[0], ctx_kv[1], sink, seq)
    x1, h_packed, aff = _out_projection(attn, p, x2, mod3, mod_row(PROJ_TILE), norm_ffn,
                                        w_out_bf, w_router, pool_w, pool_scale, seq)

    cap = EC_FACTOR * t // N_EXPERTS
    idx, slot3, off3 = _routing(aff, cap)
    table = h_packed.reshape(-1, LANES)
    per_range = N_EXPERTS // FFN_RANGES
    xs = []
    for r in range(FFN_RANGES):
        ids = _packed_row_ids(idx[r * per_range:(r + 1) * per_range], cap)
        xs.append(_gather_rows(table, ids).reshape(per_range, HQ_TILES, cap, LANES))

    chunks_per_tile = TOKEN_TILE // LANES
    tile_off = off3[:, ::chunks_per_tile, 0]
    tile_end = jnp.concatenate([tile_off[:, 1:], jnp.full((N_EXPERTS, 1), cap, I32)], axis=1)
    tile_start = (tile_off // SUBLANES) * SUBLANES
    tile_nch = jnp.where(tile_end > tile_off,
                         (tile_end - tile_start + COMBINE_CHUNK - 1) // COMBINE_CHUNK, 0)
    slot_t = slot3.reshape(N_EXPERTS, t).T

    def finish(y):
        out = _combine(x1, mod3, mod_row(TOKEN_TILE), norm_final, slot_t, tile_start.T,
                       tile_nch.T, y)
        return out.reshape(b, seq, D_MODEL)
    return xs, finish, state


def kernel(x_prompt, x_sample, c, cache_k, cache_v, c_ctx, w_ada, b_ada, norm_mix, w_in,
           sink_logits, pool_w, pool_scale, w_out, norm_ffn, w_router, w_gate, w_up, w_down,
           norm_final):
    n_b, seq, _ = x_prompt.shape
    n_db, n_lat, _ = x_sample.shape
    assert 1 + n_db <= MOD_ROWS and seq == TOKEN_TILE and n_lat % PROJ_TILE == 0
    assert (n_b * seq) % PROJ_TILE == 0

    cond = jnp.concatenate(
        [c_ctx[None, :], c, jnp.zeros((MOD_ROWS - 1 - n_db, D_MODEL), F32)], axis=0)
    mod3 = _modulation(cond, w_ada[0], b_ada[0]).reshape(MOD_ROWS, N_MOD, D_MODEL)

    w_router_bf = jnp.pad(w_router[0], ((0, 0), (0, LANES - N_EXPERTS))).astype(BF16)
    weights = (norm_mix[0][None, :], w_in[0].astype(BF16), sink_logits[0], pool_w[0],
               pool_scale[0][None, :], w_out[0].astype(BF16), norm_ffn[0][None, :], w_router_bf,
               w_gate.reshape(w_gate.shape[1:]), w_up.reshape(w_up.shape[1:]),
               w_down.reshape(w_down.shape[1:]), norm_final[None, :])

    xs_p, finish_p, (k_p, v_p) = _token_group(
        x_prompt, mod3, lambda tile: (lambda i: 0), seq, weights, None, None)

    past = cache_k.shape[2]
    ck = cache_k[:, 0].reshape(n_db, past, KV_WIDTH)
    cv = cache_v[:, 0].reshape(n_db, past, KV_WIDTH)
    xs_l, finish_l, _ = _token_group(
        x_sample, mod3, lambda tile: (lambda i: 1 + i // (n_lat // tile)), n_lat, weights,
        _rope_tables(n_lat), (ck, cv))

    ys = None
    for r in range(FFN_RANGES):
        ys = _expert_ffn([xs_p[r], xs_l[r]], weights[8], weights[9], weights[10],
                         r * (N_EXPERTS // FFN_RANGES), ys)
    y_prompt = finish_p(ys[0])
    y_sample = finish_l(ys[1])

    state_k = k_p.reshape(n_b, 1, seq, N_KV_HEADS, HEAD_DIM)
    state_v = v_p.reshape(n_b, 1, seq, N_KV_HEADS, HEAD_DIM)
    return (y_prompt, y_sample, state_k, state_v)
```

```python
import functools

import jax
import jax.numpy as jnp
import numpy as np
from jax import lax
from jax.experimental import pallas as pl
from jax.experimental.pallas import tpu as pltpu
from jax.experimental.pallas import tpu_sc as plsc

F32 = jnp.float32
BF16 = jnp.bfloat16
I32 = jnp.int32

D_MODEL = 2048
N_HEADS = 8
N_KV_HEADS = 2
HEAD_DIM = 128
Q_PER_KV = N_HEADS // N_KV_HEADS
ATTN_WIDTH = N_HEADS * HEAD_DIM
KV_WIDTH = N_KV_HEADS * HEAD_DIM
POOL_WIDTH = D_MODEL - ATTN_WIDTH
POOL_SIZES = (2, 4, 8, 16)
POOL_GROUP = POOL_WIDTH // len(POOL_SIZES)
IN_WIDTH = ATTN_WIDTH + 2 * KV_WIDTH + POOL_WIDTH
WINDOW = 128
BLOCK = 128
GRID_W = 64
ROPE_THETA = 10000.0
ROPE_FREQS = HEAD_DIM // 4
N_EXPERTS = 16
EC_FACTOR = 2
D_EXPERT = 1024
N_MOD = 6
EPS = 1e-6
NEG = -1e30
ATTN_SCALE = HEAD_DIM ** -0.5

LANES = 128
SUBLANES = 8
BF16_ROWS = 16
VMEM_CAP = 64 * 1024 * 1024
SC_CORES = 2
SC_SUBCORES = 16

MOD_ROWS = 8
TOKEN_TILE = 256
PROJ_TILE = 512
SUB_TILE = 256
HQ_TILES = D_MODEL // 2 // LANES + 1
GATHER_CHUNK = 128
FFN_RANGES = 2
POOL_HALO = 8
COMBINE_CHUNK = BF16_ROWS


def _vmem_limit(nbytes):
    return int(min(VMEM_CAP - (4 << 20), max(nbytes, 16 << 20)))


def _mod_kernel(c_ref, w_ref, b_ref, o_ref):
    c = c_ref[...]
    s = c * jax.nn.sigmoid(c)
    o_ref[...] = jnp.dot(s.astype(BF16), w_ref[...].astype(BF16),
                         preferred_element_type=F32) + b_ref[...]


def _modulation(cond, w_ada, b_ada):
    n = w_ada.shape[1]
    tn = 1024
    return pl.pallas_call(
        _mod_kernel,
        grid=(n // tn,),
        in_specs=[pl.BlockSpec((MOD_ROWS, D_MODEL), lambda j: (0, 0)),
                  pl.BlockSpec((D_MODEL, tn), lambda j: (0, j)),
                  pl.BlockSpec((1, tn), lambda j: (0, j))],
        out_specs=pl.BlockSpec((MOD_ROWS, tn), lambda j: (0, j)),
        out_shape=jax.ShapeDtypeStruct((MOD_ROWS, n), F32),
        compiler_params=pltpu.CompilerParams(
            vmem_limit_bytes=_vmem_limit(3 * D_MODEL * tn * 4)),
        name="modulation",
    )(cond, w_ada, b_ada.reshape(1, n))


def _norm_mod(x, g, shift, scale):
    ms = jnp.mean(x * x, axis=-1, keepdims=True)
    y = x * lax.rsqrt(ms + EPS)
    return (y * g) * (1.0 + scale) + shift


def _inproj_kernel(*refs, rope):
    if rope:
        x_ref, mod_ref, g_ref, w_ref, cos_ref, sa_ref, sb_ref, q_ref, k_ref, v_ref, p_ref = refs
    else:
        x_ref, mod_ref, g_ref, w_ref, q_ref, k_ref, v_ref, p_ref, ks_ref, vs_ref = refs
    mod = mod_ref[0]
    for s in range(x_ref.shape[0] // SUB_TILE):
        rows = slice(s * SUB_TILE, (s + 1) * SUB_TILE)
        h = _norm_mod(x_ref[rows, :], g_ref[...], mod[0:1], mod[1:2])
        u = jnp.dot(h.astype(BF16), w_ref[...], preferred_element_type=F32)

        def rot(xh, rows=rows):
            return (xh * cos_ref[rows, :] + pltpu.roll(xh, LANES - ROPE_FREQS, 1) * sa_ref[rows, :]
                    + pltpu.roll(xh, ROPE_FREQS, 1) * sb_ref[rows, :])

        for hd in range(N_HEADS):
            xh = u[:, hd * HEAD_DIM:(hd + 1) * HEAD_DIM] * ATTN_SCALE
            q_ref[rows, hd * HEAD_DIM:(hd + 1) * HEAD_DIM] = (rot(xh) if rope else xh).astype(BF16)
        for hd in range(N_KV_HEADS):
            lo = ATTN_WIDTH + hd * HEAD_DIM
            xh = u[:, lo:lo + HEAD_DIM]
            k_ref[rows, hd * HEAD_DIM:(hd + 1) * HEAD_DIM] = rot(xh) if rope else xh
        v_ref[rows, :] = u[:, ATTN_WIDTH + KV_WIDTH:ATTN_WIDTH + 2 * KV_WIDTH]
        p_ref[rows, :] = u[:, ATTN_WIDTH + 2 * KV_WIDTH:]
        if not rope:
            for hd in range(N_KV_HEADS):
                state_rows = pl.ds(s * SUB_TILE * N_KV_HEADS + hd, SUB_TILE, stride=N_KV_HEADS)
                lo = ATTN_WIDTH + hd * HEAD_DIM
                ks_ref[state_rows, :] = u[:, lo:lo + HEAD_DIM]
                vs_ref[state_rows, :] = u[:, lo + KV_WIDTH:lo + KV_WIDTH + HEAD_DIM]


def _in_projection(x2, mod3, mod_row, g, w_in_bf, rope_tabs):
    t = x2.shape[0]
    tm = PROJ_TILE
    rope = rope_tabs is not None
    row = lambda i: (i, 0)
    in_specs = [pl.BlockSpec((tm, D_MODEL), row),
                pl.BlockSpec((1, N_MOD, D_MODEL), lambda i: (mod_row(i), 0, 0)),
                pl.BlockSpec((1, D_MODEL), lambda i: (0, 0)),
                pl.BlockSpec((D_MODEL, IN_WIDTH), lambda i: (0, 0))]
    args = [x2, mod3, g, w_in_bf]
    if rope:
        n_seq = rope_tabs[0].shape[0]
        seq_blocks = n_seq // tm
        for tab in rope_tabs:
            in_specs.append(pl.BlockSpec((tm, HEAD_DIM), lambda i: (i % seq_blocks, 0)))
            args.append(tab)
    out_specs = [pl.BlockSpec((tm, ATTN_WIDTH), row),
                 pl.BlockSpec((tm, KV_WIDTH), row),
                 pl.BlockSpec((tm, KV_WIDTH), row),
                 pl.BlockSpec((tm, POOL_WIDTH), row)]
    out_shape = [jax.ShapeDtypeStruct((t, ATTN_WIDTH), BF16),
                 jax.ShapeDtypeStruct((t, KV_WIDTH), F32),
                 jax.ShapeDtypeStruct((t, KV_WIDTH), F32),
                 jax.ShapeDtypeStruct((t, POOL_WIDTH), F32)]
    if not rope:
        for _ in range(2):
            out_specs.append(pl.BlockSpec((tm * N_KV_HEADS, HEAD_DIM), row))
            out_shape.append(jax.ShapeDtypeStruct((t * N_KV_HEADS, HEAD_DIM), F32))
    return pl.pallas_call(
        functools.partial(_inproj_kernel, rope=rope),
        grid=(t // tm,),
        in_specs=in_specs,
        out_specs=out_specs,
        out_shape=out_shape,
        compiler_params=pltpu.CompilerParams(
            vmem_limit_bytes=_vmem_limit(2 * D_MODEL * IN_WIDTH * 2 + 24 * tm * D_MODEL * 4)),
        name="in_projection",
    )(*args)


def _softmax_pv(s_list, v_list, sink_col):
    m = sink_col
    for s in s_list:
        m = jnp.maximum(m, jnp.max(s, axis=-1, keepdims=True))
    denom = jnp.exp(sink_col - m)
    out = None
    for s, v in zip(s_list, v_list):
        e = jnp.exp(s - m)
        denom = denom + jnp.sum(e, axis=-1, keepdims=True)
        o = jnp.dot(e.astype(BF16), v, preferred_element_type=F32)
        out = o if out is None else out + o
    return out * (1.0 / denom)


def _stack_heads(q, kv):
    return jnp.concatenate(
        [q[:, (kv * Q_PER_KV + g) * HEAD_DIM:(kv * Q_PER_KV + g + 1) * HEAD_DIM]
         for g in range(Q_PER_KV)], axis=0)


def _sink_column(sink_ref, kv, rows):
    r = lax.broadcasted_iota(I32, (Q_PER_KV * rows, 1), 0)
    col = jnp.zeros((Q_PER_KV * rows, 1), F32)
    for g in range(Q_PER_KV):
        col = jnp.where((r >= g * rows) & (r < (g + 1) * rows), sink_ref[kv * Q_PER_KV + g], col)
    return col


def _qk(q, k):
    return lax.dot_general(q, k, (((1,), (1,)), ((), ())), preferred_element_type=F32)


def _ctx_attn_kernel(sink_ref, q_ref, k_ref, v_ref, o_ref):
    rows = q_ref.shape[0]
    q = q_ref[...]
    for kv in range(N_KV_HEADS):
        kh = k_ref[:, kv * HEAD_DIM:(kv + 1) * HEAD_DIM].astype(BF16)
        vh = v_ref[:, kv * HEAD_DIM:(kv + 1) * HEAD_DIM].astype(BF16)
        qs = _stack_heads(q, kv)
        s = _qk(qs, kh)
        o = _softmax_pv([s], [vh], _sink_column(sink_ref, kv, rows))
        for g in range(Q_PER_KV):
            hd = kv * Q_PER_KV + g
            o_ref[:, hd * HEAD_DIM:(hd + 1) * HEAD_DIM] = o[g * rows:(g + 1) * rows].astype(BF16)


def _context_attention(q, k, v, sink, seq):
    t = q.shape[0]
    row = lambda b: (b, 0)
    return pl.pallas_call(
        _ctx_attn_kernel,
        grid=(t // seq,),
        in_specs=[pl.BlockSpec(memory_space=pltpu.SMEM),
                  pl.BlockSpec((seq, ATTN_WIDTH), row),
                  pl.BlockSpec((seq, KV_WIDTH), row),
                  pl.BlockSpec((seq, KV_WIDTH), row)],
        out_specs=pl.BlockSpec((seq, ATTN_WIDTH), row),
        out_shape=jax.ShapeDtypeStruct((t, ATTN_WIDTH), BF16),
        name="context_attention",
    )(sink, q, k, v)


def _lat_attn_kernel(sink_ref, q_ref, k_ref, v_ref, ck_ref, cv_ref, o_ref, *, n_seq):
    i = pl.program_id(1)
    band = 3 * BLOCK
    start = pl.multiple_of(jnp.clip((i - 1) * BLOCK, 0, n_seq - band), BLOCK)
    rows = Q_PER_KV * BLOCK
    qpos = i * BLOCK + lax.broadcasted_iota(I32, (rows, band), 0) % BLOCK
    kpos = start + lax.broadcasted_iota(I32, (rows, band), 1)
    mask = jnp.abs(kpos - qpos) <= WINDOW
    q = q_ref[...]
    for kv in range(N_KV_HEADS):
        cols = slice(kv * HEAD_DIM, (kv + 1) * HEAD_DIM)
        kb = k_ref[pl.ds(start, band), cols].astype(BF16)
        vb = v_ref[pl.ds(start, band), cols].astype(BF16)
        ck = ck_ref[0, :, cols].astype(BF16)
        cv = cv_ref[0, :, cols].astype(BF16)
        qs = _stack_heads(q, kv)
        s_loc = jnp.where(mask, _qk(qs, kb), NEG)
        s_ctx = _qk(qs, ck)
        o = _softmax_pv([s_loc, s_ctx], [vb, cv], _sink_column(sink_ref, kv, BLOCK))
        for g in range(Q_PER_KV):
            hd = kv * Q_PER_KV + g
            o_ref[:, hd * HEAD_DIM:(hd + 1) * HEAD_DIM] = o[g * BLOCK:(g + 1) * BLOCK].astype(BF16)


def _latent_attention(q, k, v, ck, cv, sink, n_seq):
    t = q.shape[0]
    nb = n_seq // BLOCK
    past = ck.shape[1]
    return pl.pallas_call(
        functools.partial(_lat_attn_kernel, n_seq=n_seq),
        grid=(t // n_seq, nb),
        in_specs=[pl.BlockSpec(memory_space=pltpu.SMEM),
                  pl.BlockSpec((BLOCK, ATTN_WIDTH), lambda b, i: (b * nb + i, 0)),
                  pl.BlockSpec((n_seq, KV_WIDTH), lambda b, i: (b, 0)),
                  pl.BlockSpec((n_seq, KV_WIDTH), lambda b, i: (b, 0)),
                  pl.BlockSpec((1, past, KV_WIDTH), lambda b, i: (b, 0, 0)),
                  pl.BlockSpec((1, past, KV_WIDTH), lambda b, i: (b, 0, 0))],
        out_specs=pl.BlockSpec((BLOCK, ATTN_WIDTH), lambda b, i: (b * nb + i, 0)),
        out_shape=jax.ShapeDtypeStruct((t, ATTN_WIDTH), BF16),
        name="latent_attention",
    )(sink, q, k, v, ck, cv)


def _pool_subtile(p_ref, r0, seq, w_ref, s_ref):
    n = SUB_TILE
    rows = n + 2 * POOL_HALO
    static = isinstance(r0, int)
    t0 = r0 % seq
    t = t0 + lax.broadcasted_iota(I32, (n, 1), 0)
    has_top = t0 > 0
    has_bottom = t0 + n < seq
    zeros = jnp.zeros((POOL_HALO, POOL_GROUP), F32)
    outs = []
    for g, w in enumerate(POOL_SIZES):
        cols = slice(g * POOL_GROUP, (g + 1) * POOL_GROUP)
        pg = p_ref[pl.ds(r0, n), cols]
        if static:
            top = p_ref[r0 - POOL_HALO:r0, cols] if has_top else zeros
            bottom = p_ref[r0 + n:r0 + n + POOL_HALO, cols] if has_bottom else zeros
        else:
            top_row = pl.multiple_of(jnp.maximum(r0 - POOL_HALO, 0), POOL_HALO)
            bottom_row = pl.multiple_of(jnp.minimum(r0 + n, p_ref.shape[0] - POOL_HALO), POOL_HALO)
            top = jnp.where(has_top, p_ref[pl.ds(top_row, POOL_HALO), cols], 0.0)
            bottom = jnp.where(has_bottom, p_ref[pl.ds(bottom_row, POOL_HALO), cols], 0.0)
        x = jnp.concatenate([top, pg, bottom], axis=0)
        acc = x + pltpu.roll(x, 1, 0)
        step = 1
        while 2 * step < w:
            acc = pltpu.roll(acc, step, 0) + pltpu.roll(acc, rows - step, 0)
            step *= 2
        wsum = acc[POOL_HALO:POOL_HALO + n]
        lo = jnp.maximum(t - w // 2, 0)
        hi = jnp.minimum(t + w - w // 2, seq)
        inv_cnt = 1.0 / (hi - lo).astype(F32)
        mixed = wsum * inv_cnt - pg
        y = jnp.dot(mixed.astype(BF16), w_ref[g].astype(BF16), preferred_element_type=F32)
        outs.append((y * s_ref[:, cols]).astype(BF16))
    return jnp.concatenate(outs, axis=1)


def _pack_pair(lo, hi):
    return lax.bitcast_convert_type(pltpu.pack_elementwise([lo, hi], packed_dtype=BF16), I32)


def _unpack_pair(words):
    lo = pltpu.unpack_elementwise(words, index=0, packed_dtype=BF16, unpacked_dtype=F32)
    hi = pltpu.unpack_elementwise(words, index=1, packed_dtype=BF16, unpacked_dtype=F32)
    return lo, hi


def _outproj_kernel(a_ref, p_ref, x_ref, mod_ref, g_ref, wo_ref, wr_ref, pw_ref, ps_ref,
                    x1_ref, h_ref, aff_ref, *, seq):
    mod = mod_ref[0]
    groups = SUB_TILE // SUBLANES
    half = D_MODEL // 2
    tm = x_ref.shape[0]
    steps_per_p_block = p_ref.shape[0] // tm
    for s in range(tm // SUB_TILE):
        rows = slice(s * SUB_TILE, (s + 1) * SUB_TILE)
        grp = slice(s * groups, (s + 1) * groups)
        r0 = s * SUB_TILE
        if steps_per_p_block > 1:
            r0 = pl.multiple_of((pl.program_id(0) % steps_per_p_block) * tm + r0, SUB_TILE)
        pooled = _pool_subtile(p_ref, r0, seq, pw_ref, ps_ref)
        mix = (jnp.dot(a_ref[rows, :], wo_ref[0:ATTN_WIDTH, :], preferred_element_type=F32)
               + jnp.dot(pooled, wo_ref[ATTN_WIDTH:D_MODEL, :], preferred_element_type=F32))
        x1 = x_ref[rows, :] + mod[2:3] * mix
        x1_ref[rows, :] = x1
        h = _norm_mod(x1, g_ref[...], mod[3:4], mod[4:5])
        logits = jnp.dot(h.astype(BF16), wr_ref[...], preferred_element_type=F32)
        lane = lax.broadcasted_iota(I32, logits.shape, 1)
        logits = jnp.where(lane < N_EXPERTS, logits, -jnp.inf)
        m = jnp.max(logits, axis=-1, keepdims=True)
        e = jnp.exp(logits - m)
        aff = e / jnp.sum(e, axis=-1, keepdims=True)
        aff_ref[rows, :] = aff[:, 0:N_EXPERTS]
        for c in range(HQ_TILES - 1):
            words = _pack_pair(h[:, c * LANES:(c + 1) * LANES],
                               h[:, half + c * LANES:half + (c + 1) * LANES])
            h_ref[grp, c * SUBLANES:(c + 1) * SUBLANES, :] = (
                words.reshape(groups, SUBLANES, LANES))
        h_ref[grp, (HQ_TILES - 1) * SUBLANES:, :] = (
            lax.bitcast_convert_type(aff, I32).reshape(groups, SUBLANES, LANES))


def _out_projection(attn, p, x2, mod3, mod_row, g, w_out_bf, w_router, pool_w, pool_scale, seq):
    t = x2.shape[0]
    tm = PROJ_TILE
    row = lambda i: (i, 0)
    p_rows = max(tm, seq)
    steps_per_p_block = p_rows // tm
    return pl.pallas_call(
        functools.partial(_outproj_kernel, seq=seq),
        grid=(t // tm,),
        in_specs=[pl.BlockSpec((tm, ATTN_WIDTH), row),
                  pl.BlockSpec((p_rows, POOL_WIDTH), lambda i: (i // steps_per_p_block, 0)),
                  pl.BlockSpec((tm, D_MODEL), row),
                  pl.BlockSpec((1, N_MOD, D_MODEL), lambda i: (mod_row(i), 0, 0)),
                  pl.BlockSpec((1, D_MODEL), lambda i: (0, 0)),
                  pl.BlockSpec((D_MODEL, D_MODEL), lambda i: (0, 0)),
                  pl.BlockSpec((D_MODEL, LANES), lambda i: (0, 0)),
                  pl.BlockSpec((len(POOL_SIZES), POOL_GROUP, POOL_GROUP), lambda i: (0, 0, 0)),
                  pl.BlockSpec((1, POOL_WIDTH), lambda i: (0, 0))],
        out_specs=[pl.BlockSpec((tm, D_MODEL), row),
                   pl.BlockSpec((tm // SUBLANES, HQ_TILES * SUBLANES, LANES), lambda i: (i, 0, 0)),
                   pl.BlockSpec((tm, N_EXPERTS), row)],
        out_shape=[jax.ShapeDtypeStruct((t, D_MODEL), F32),
                   jax.ShapeDtypeStruct((t // SUBLANES, HQ_TILES * SUBLANES, LANES), I32),
                   jax.ShapeDtypeStruct((t, N_EXPERTS), F32)],
        compiler_params=pltpu.CompilerParams(
            vmem_limit_bytes=_vmem_limit(2 * D_MODEL * D_MODEL * 2 + 24 * tm * D_MODEL * 4)),
        name="out_projection",
    )(attn, p, x2, mod3, g, w_out_bf, w_router, pool_w, pool_scale)


def _route_kernel(a_ref, idx_ref, slot_ref, off_ref, *, cap):
    a = a_ref[...]
    n_e, n_c, _ = a.shape
    rows = n_e * n_c

    def bisect(it, thr):
        cand = thr | jnp.left_shift(jnp.int32(1), 30 - it)
        cand_f = lax.bitcast_convert_type(cand, F32)
        cnt = jnp.sum(jnp.sum((a >= cand_f).astype(F32), axis=2, keepdims=True),
                      axis=1, keepdims=True)
        return jnp.where(cnt >= cap, cand, thr)

    thr = lax.fori_loop(0, 31, bisect, jnp.zeros((n_e, 1, 1), I32))
    thr_f = lax.bitcast_convert_type(thr, F32)
    gt = (a > thr_f).astype(F32).reshape(rows, LANES)
    eq = (a == thr_f).astype(F32).reshape(rows, LANES)

    li = lax.broadcasted_iota(I32, (LANES, LANES), 0)
    lj = lax.broadcasted_iota(I32, (LANES, LANES), 1)
    upper_incl = (li <= lj).astype(BF16)
    ri = lax.broadcasted_iota(I32, (rows, rows), 0)
    rj = lax.broadcasted_iota(I32, (rows, rows), 1)
    same_expert = (ri // n_c) == (rj // n_c)
    before = (same_expert & (rj < ri)).astype(BF16)
    whole = same_expert.astype(BF16)

    def lane_bcast(col):
        return jnp.broadcast_to(col, (rows, LANES)).astype(BF16)

    def prefix(x):
        incl = jnp.dot(x.astype(BF16), upper_incl, preferred_element_type=F32)
        tot = incl[:, LANES - 1:LANES]
        off = jnp.dot(before, lane_bcast(tot), preferred_element_type=F32)
        return incl, tot, off

    n_gt = jnp.dot(whole, lane_bcast(jnp.sum(gt, axis=1, keepdims=True)),
                   preferred_element_type=F32)
    need = cap - n_gt
    incl_eq, _, off_eq = prefix(eq)
    rank_eq = off_eq + incl_eq - eq
    sel = jnp.where((eq > 0) & (rank_eq < need), 1.0, gt)
    incl, tot, off = prefix(sel)
    slot = off + incl - sel
    slot_ref[...] = jnp.where(sel > 0, slot, -1.0).astype(I32).reshape(n_e, n_c, LANES)
    off_ref[...] = off.astype(I32).reshape(n_e, n_c, LANES)

    s_lane = lax.broadcasted_iota(I32, (1, cap), 1).astype(F32)
    c_col = lax.broadcasted_iota(I32, (n_c, 1), 0).astype(F32)
    for e in range(n_e):
        r0 = e * n_c
        incl_e = incl[r0:r0 + n_c]
        off_e = off[r0:r0 + n_c, 0:1]
        tot_e = tot[r0:r0 + n_c]
        onehot = ((off_e <= s_lane) & (s_lane < off_e + tot_e)).astype(F32)
        counts = lax.dot_general(incl_e.astype(BF16), onehot.astype(BF16),
                                 (((0,), (0,)), ((), ())), preferred_element_type=F32)
        local = s_lane - jnp.sum(onehot * off_e, axis=0, keepdims=True)
        lane = jnp.sum((counts <= local).astype(F32), axis=0, keepdims=True)
        chunk = jnp.sum(onehot * c_col, axis=0, keepdims=True)
        idx_ref[e] = (chunk * LANES + lane).astype(I32)


def _routing(aff, cap):
    t = aff.shape[0]
    n_c = t // LANES
    a3 = aff.T.reshape(N_EXPERTS, n_c, LANES)
    return pl.pallas_call(
        functools.partial(_route_kernel, cap=cap),
        out_shape=[jax.ShapeDtypeStruct((N_EXPERTS, 1, cap), I32),
                   jax.ShapeDtypeStruct((N_EXPERTS, n_c, LANES), I32),
                   jax.ShapeDtypeStruct((N_EXPERTS, n_c, LANES), I32)],
        compiler_params=pltpu.CompilerParams(vmem_limit_bytes=_vmem_limit(48 << 20)),
        name="routing",
    )(a3)


def _gather_rows(table, row_ids):
    n_chunks = row_ids.shape[0]
    n_workers = SC_CORES * SC_SUBCORES
    per_worker = n_chunks // n_workers
    assert row_ids.shape[1] == GATHER_CHUNK and n_chunks % n_workers == 0
    mesh = plsc.VectorSubcoreMesh(core_axis_name="core", subcore_axis_name="subcore")

    @functools.partial(
        pl.kernel, mesh=mesh,
        out_type=jax.ShapeDtypeStruct((n_chunks * GATHER_CHUNK, LANES), I32),
        scratch_types=[pltpu.VMEM((per_worker, GATHER_CHUNK), I32),
                       pltpu.VMEM((2, GATHER_CHUNK, LANES), I32),
                       pltpu.SemaphoreType.DMA((2,)),
                       pltpu.SemaphoreType.DMA((2,))],
        name="gather_rows",
    )
    def gather(table_hbm, ids_hbm, out_hbm, ids_v, rows_v, gather_sem, store_sem):
        worker = lax.axis_index("subcore") * SC_CORES + lax.axis_index("core")
        first = worker * per_worker
        pltpu.sync_copy(ids_hbm.at[worker], ids_v)

        def fetch(j):
            return pltpu.make_async_copy(table_hbm.at[ids_v.at[j]], rows_v.at[j % 2],
                                         gather_sem.at[j % 2])

        def store(j):
            rows = pl.ds(pl.multiple_of((first + j) * GATHER_CHUNK, GATHER_CHUNK), GATHER_CHUNK)
            return pltpu.make_async_copy(rows_v.at[j % 2], out_hbm.at[rows], store_sem.at[j % 2])

        for j in range(per_worker):
            if j >= 2:
                store(j - 2).wait()
            fetch(j).start()
            if j >= 1:
                fetch(j - 1).wait()
                store(j - 1).start()
        fetch(per_worker - 1).wait()
        store(per_worker - 1).start()
        for j in range(max(per_worker - 2, 0), per_worker):
            store(j).wait()

    return gather(table, row_ids.reshape(n_workers, per_worker, GATHER_CHUNK))


def _packed_row_ids(idx, cap):
    tok = idx.reshape(-1, 1, cap // GATHER_CHUNK, GATHER_CHUNK)
    tile = jnp.arange(HQ_TILES, dtype=I32).reshape(1, HQ_TILES, 1, 1)
    ids = ((tok // SUBLANES) * HQ_TILES + tile) * SUBLANES + tok % SUBLANES
    return ids.reshape(-1, GATHER_CHUNK)


def _ffn_kernel(*refs, caps, n_f, row_chunk, first_expert, n_prior):
    n_g = len(caps)
    x_refs = refs[:n_g]
    wg_ref, wu_ref, wd_ref = refs[n_g:n_g + 3]
    y_refs = refs[n_g + 3 + n_prior:2 * n_g + 3 + n_prior]
    acc_ref = refs[2 * n_g + 3 + n_prior]
    e = pl.program_id(0)
    f = pl.program_id(1)

    @pl.when((e == 0) & (f == 0))
    def _():
        acc_ref[...] = jnp.zeros_like(acc_ref)

    first = f == 0
    base = 0
    for x_ref, cap in zip(x_refs, caps):
        for r in range(cap // row_chunk):
            rs = slice(r * row_chunk, (r + 1) * row_chunk)
            acc_rows = slice(base + r * row_chunk, base + (r + 1) * row_chunk)
            pairs = [_unpack_pair(x_ref[0, c, rs, :]) for c in range(HQ_TILES - 1)]
            x = jnp.concatenate([p[0] for p in pairs] + [p[1] for p in pairs], axis=1)
            gate_act = jnp.dot(x, wg_ref[0], preferred_element_type=F32)
            up = jnp.dot(x, wu_ref[0], preferred_element_type=F32)
            hid = (gate_act * jax.nn.sigmoid(gate_act)) * up
            part = jnp.dot(hid, wd_ref[0], preferred_element_type=F32)
            acc_ref[acc_rows, :] = part + jnp.where(first, 0.0, acc_ref[acc_rows, :])
        base += cap

    @pl.when(f == n_f - 1)
    def _():
        base = 0
        for x_ref, y_ref, cap in zip(x_refs, y_refs, caps):
            aff = lax.bitcast_convert_type(x_ref[0, HQ_TILES - 1], F32)
            lane = lax.broadcasted_iota(I32, aff.shape, 1)
            gates = jnp.sum(jnp.where(lane == e + first_expert, aff, 0.0), axis=1, keepdims=True)
            y = acc_ref[base:base + cap, :] * gates
            y_ref[0, 0:cap, :] = _pack_pair(y[:, 0:D_MODEL // 2], y[:, D_MODEL // 2:])
            y_ref[0, cap:, :] = jnp.zeros((COMBINE_CHUNK, D_MODEL // 2), I32)
            base += cap


def _expert_ffn(xs_groups, w_gate, w_up, w_down, first_expert, prior_outputs):
    n_f = 4
    tf = D_EXPERT // n_f
    row_chunk = 512
    n_e = xs_groups[0].shape[0]
    e0 = first_expert
    caps = tuple(xs.shape[2] for xs in xs_groups)
    rows = sum(caps)
    est = (2 * HQ_TILES * rows * LANES * 4 + rows * D_MODEL * (4 + 2 * 2) + 2 * 3 * D_MODEL * tf * 4
           + row_chunk * (3 * tf + 2 * D_MODEL) * 4)
    x_specs = [pl.BlockSpec((1, HQ_TILES, cap, LANES), lambda e, f: (e, 0, 0, 0)) for cap in caps]
    y_shapes = [(N_EXPERTS, cap + COMBINE_CHUNK, D_MODEL // 2) for cap in caps]
    y_specs = [pl.BlockSpec((1,) + s[1:], lambda e, f: (e + e0, 0, 0)) for s in y_shapes]
    prior = list(prior_outputs or [])
    n_in = len(caps) + 3
    return pl.pallas_call(
        functools.partial(_ffn_kernel, caps=caps, n_f=n_f, row_chunk=row_chunk,
                          first_expert=e0, n_prior=len(prior)),
        grid=(n_e, n_f),
        in_specs=(x_specs + [pl.BlockSpec((1, D_MODEL, tf), lambda e, f: (e + e0, 0, f)),
                             pl.BlockSpec((1, D_MODEL, tf), lambda e, f: (e + e0, 0, f)),
                             pl.BlockSpec((1, tf, D_MODEL), lambda e, f: (e + e0, f, 0))]
                  + [pl.BlockSpec(memory_space=pl.ANY) for _ in prior]),
        out_specs=y_specs,
        out_shape=[jax.ShapeDtypeStruct(s, I32) for s in y_shapes],
        input_output_aliases={n_in + j: j for j in range(len(prior))},
        scratch_shapes=[pltpu.VMEM((rows, D_MODEL), F32)],
        compiler_params=pltpu.CompilerParams(
            dimension_semantics=("arbitrary", "arbitrary"),
            vmem_limit_bytes=_vmem_limit(est + (4 << 20))),
        name="expert_ffn",
    )(*xs_groups, w_gate, w_up, w_down, *prior)


def _combine_kernel(start_ref, nch_ref, x1_ref, mod_ref, g_ref, slot_ref, y_hbm, o_ref,
                    ybuf_ref, cols_ref, ffn_ref, sem, *, first_k, kblock):
    i = pl.program_id(0)
    tm = x1_ref.shape[0]
    cur = i % 2

    def chunk_copy(e, src_row, buf, dst_row):
        return pltpu.make_async_copy(
            y_hbm.at[e, pl.ds(pl.multiple_of(src_row, SUBLANES), COMBINE_CHUNK)],
            ybuf_ref.at[buf, pl.ds(pl.multiple_of(dst_row, COMBINE_CHUNK), COMBINE_CHUNK)],
            sem.at[buf])

    def fetch(tile, buf):
        pos = jnp.int32(0)
        for e in range(N_EXPERTS):
            start = start_ref[tile, e]
            nch = nch_ref[tile, e]

            def issue(c, carry, e=e, start=start, pos=pos):
                chunk_copy(e, start + c * COMBINE_CHUNK, buf, pos + c * COMBINE_CHUNK).start()
                return carry
            lax.fori_loop(0, nch, issue, 0)
            pos = pos + nch * COMBINE_CHUNK

    @pl.when(i == 0)
    def _():
        ybuf_ref[...] = jnp.zeros_like(ybuf_ref)
        fetch(0, 0)

    @pl.when(i + 1 < pl.num_programs(0))
    def _():
        fetch(i + 1, 1 - cur)

    pos = jnp.int32(0)
    begins = []
    for e in range(N_EXPERTS):
        begins.append(pos)
        pos = pos + nch_ref[i, e] * COMBINE_CHUNK
    begins.append(pos)

    def drain(c, carry):
        chunk_copy(0, jnp.int32(0), cur, jnp.int32(0)).wait()
        return carry
    lax.fori_loop(0, pos // COMBINE_CHUNK, drain, 0)

    slots = slot_ref[...]
    for e in range(N_EXPERTS):
        s_e = slots[:, e:e + 1]
        col = jnp.where(s_e >= 0, s_e + (begins[e] - start_ref[i, e]), -1)
        cols_ref[e] = jnp.broadcast_to(col, (tm, LANES))

    lane = lax.broadcasted_iota(I32, (tm, LANES), 1)

    def selection(k0, width):
        halves = []
        for h in range(width // LANES):
            c0 = k0 + h * LANES
            target = lane + c0
            e_lo = jnp.int32(0)
            e_hi = jnp.int32(0)
            for e in range(N_EXPERTS):
                e_lo = e_lo + (begins[e + 1] <= c0).astype(I32)
                e_hi = e_hi + (begins[e] < c0 + LANES).astype(I32)

            def mark(e, hit, target=target):
                return jnp.where(cols_ref[e] == target, 1.0, hit)
            halves.append(lax.fori_loop(e_lo, e_hi, mark, jnp.zeros((tm, LANES), F32)))
        return jnp.concatenate(halves, axis=1)

    def block(k0, width):
        sel = selection(k0, width)
        lo, hi = _unpack_pair(ybuf_ref[cur, pl.ds(k0, width), :])
        return jnp.concatenate([jnp.dot(sel, lo, preferred_element_type=F32),
                                jnp.dot(sel, hi, preferred_element_type=F32)], axis=1)

    ffn_ref[...] = block(0, first_k)

    def kstep(kb, carry):
        ffn_ref[...] += block(pl.multiple_of(kb * kblock, kblock), kblock)
        return carry
    lax.fori_loop(first_k // kblock, (pos + kblock - 1) // kblock, kstep, 0)

    out = x1_ref[...] + mod_ref[0][5:6] * ffn_ref[...]
    ms = jnp.mean(out * out, axis=-1, keepdims=True)
    o_ref[...] = (out * lax.rsqrt(ms + EPS)) * g_ref[...]


def _combine(x1, mod3, mod_row, g_final, slot_t, tile_start, tile_nch, y):
    t = x1.shape[0]
    tm = TOKEN_TILE
    first_k = 2 * tm + N_EXPERTS * COMBINE_CHUNK
    kblock = 256
    max_rows = N_EXPERTS * (tm + 2 * COMBINE_CHUNK)
    max_rows = -(-max_rows // kblock) * kblock
    row = lambda i, *_: (i, 0)
    grid_spec = pltpu.PrefetchScalarGridSpec(
        num_scalar_prefetch=2,
        grid=(t // tm,),
        in_specs=[pl.BlockSpec((tm, D_MODEL), row),
                  pl.BlockSpec((1, N_MOD, D_MODEL), lambda i, *_: (mod_row(i), 0, 0)),
                  pl.BlockSpec((1, D_MODEL), lambda i, *_: (0, 0)),
                  pl.BlockSpec((tm, N_EXPERTS), row),
                  pl.BlockSpec(memory_space=pl.ANY)],
        out_specs=pl.BlockSpec((tm, D_MODEL), row),
        scratch_shapes=[pltpu.VMEM((2, max_rows, D_MODEL // 2), I32),
                        pltpu.VMEM((N_EXPERTS, tm, LANES), I32),
                        pltpu.VMEM((tm, D_MODEL), F32),
                        pltpu.SemaphoreType.DMA((2,))],
    )
    return pl.pallas_call(
        functools.partial(_combine_kernel, first_k=first_k, kblock=kblock),
        grid_spec=grid_spec,
        out_shape=jax.ShapeDtypeStruct((t, D_MODEL), F32),
        compiler_params=pltpu.CompilerParams(
            dimension_semantics=("arbitrary",),
            vmem_limit_bytes=_vmem_limit(2 * max_rows * D_MODEL * 2 + 16 * tm * D_MODEL * 4)),
        name="combine",
    )(tile_start, tile_nch, x1, mod3, g_final, slot_t, y)


def _rope_tables(n):
    rows = n // GRID_W
    row = jnp.repeat(jnp.arange(rows, dtype=F32), GRID_W)
    col = jnp.tile(jnp.arange(GRID_W, dtype=F32), rows)
    inv = ROPE_THETA ** (-jnp.arange(ROPE_FREQS, dtype=F32) / ROPE_FREQS)
    ang_r = row[:, None] * inv
    ang_c = col[:, None] * inv
    zero = jnp.zeros_like(ang_r)
    cos = jnp.concatenate([jnp.cos(ang_r)] * 2 + [jnp.cos(ang_c)] * 2, axis=1)
    sin_a = jnp.concatenate([-jnp.sin(ang_r), zero, -jnp.sin(ang_c), zero], axis=1)
    sin_b = jnp.concatenate([zero, jnp.sin(ang_r), zero, jnp.sin(ang_c)], axis=1)
    return cos, sin_a, sin_b


def _token_group(x, mod3, mod_row, seq, weights, rope_tabs, ctx_kv):
    (norm_mix, w_in_bf, sink, pool_w, pool_scale, w_out_bf, norm_ffn, w_router,
     w_gate, w_up, w_down, norm_final) = weights
    b = x.shape[0]
    t = b * seq
    x2 = x.reshape(t, D_MODEL)
    q, k, v, p, *state = _in_projection(x2, mod3, mod_row(PROJ_TILE), norm_mix, w_in_bf,
                                        rope_tabs)
    if ctx_kv is None:
        attn = _context_attention(q, k, v, sink, seq)
    else:
        attn = _latent_attention(q, k, v, ctx_kv[0], ctx_kv[1], sink, seq)
    x1, h_packed, aff = _out_projection(attn, p, x2, mod3, mod_row(PROJ_TILE), norm_ffn,
                                        w_out_bf, w_router, pool_w, pool_scale, seq)

    cap = EC_FACTOR * t // N_EXPERTS
    idx, slot3, off3 = _routing(aff, cap)
    table = h_packed.reshape(-1, LANES)
    per_range = N_EXPERTS // FFN_RANGES
    xs = []
    for r in range(FFN_RANGES):
        ids = _packed_row_ids(idx[r * per_range:(r + 1) * per_range], cap)
        xs.append(_gather_rows(table, ids).reshape(per_range, HQ_TILES, cap, LANES))

    chunks_per_tile = TOKEN_TILE // LANES
    tile_off = off3[:, ::chunks_per_tile, 0]
    tile_end = jnp.concatenate([tile_off[:, 1:], jnp.full((N_EXPERTS, 1), cap, I32)], axis=1)
    tile_start = (tile_off // SUBLANES) * SUBLANES
    tile_nch = jnp.where(tile_end > tile_off,
                         (tile_end - tile_start + COMBINE_CHUNK - 1) // COMBINE_CHUNK, 0)
    slot_t = slot3.reshape(N_EXPERTS, t).T

    def finish(y):
        out = _combine(x1, mod3, mod_row(TOKEN_TILE), norm_final, slot_t, tile_start.T,
                       tile_nch.T, y)
        return out.reshape(b, seq, D_MODEL)
    return xs, finish, state


def kernel(x_prompt, x_sample, c, cache_k, cache_v, c_ctx, w_ada, b_ada, norm_mix, w_in,
           sink_logits, pool_w, pool_scale, w_out, norm_ffn, w_router, w_gate, w_up, w_down,
           norm_final):
    n_b, seq, _ = x_prompt.shape
    n_db, n_lat, _ = x_sample.shape
    assert 1 + n_db <= MOD_ROWS and seq == TOKEN_TILE and n_lat % PROJ_TILE == 0
    assert (n_b * seq) % PROJ_TILE == 0

    cond = jnp.concatenate(
        [c_ctx[None, :], c, jnp.zeros((MOD_ROWS - 1 - n_db, D_MODEL), F32)], axis=0)
    mod3 = _modulation(cond, w_ada[0], b_ada[0]).reshape(MOD_ROWS, N_MOD, D_MODEL)

    w_router_bf = jnp.pad(w_router[0], ((0, 0), (0, LANES - N_EXPERTS))).astype(BF16)
    weights = (norm_mix[0][None, :], w_in[0].astype(BF16), sink_logits[0], pool_w[0],
               pool_scale[0][None, :], w_out[0].astype(BF16), norm_ffn[0][None, :], w_router_bf,
               w_gate.reshape(w_gate.shape[1:]), w_up.reshape(w_up.shape[1:]),
               w_down.reshape(w_down.shape[1:]), norm_final[None, :])

    xs_p, finish_p, (k_p, v_p) = _token_group(
        x_prompt, mod3, lambda tile: (lambda i: 0), seq, weights, None, None)

    past = cache_k.shape[2]
    ck = cache_k[:, 0].reshape(n_db, past, KV_WIDTH)
    cv = cache_v[:, 0].reshape(n_db, past, KV_WIDTH)
    xs_l, finish_l, _ = _token_group(
        x_sample, mod3, lambda tile: (lambda i: 1 + i // (n_lat // tile)), n_lat, weights,
        _rope_tables(n_lat), (ck, cv))

    ys = None
    for r in range(FFN_RANGES):
        ys = _expert_ffn([xs_p[r], xs_l[r]], weights[8], weights[9], weights[10],
                         r * (N_EXPERTS // FFN_RANGES), ys)
    y_prompt = finish_p(ys[0])
    y_sample = finish_l(ys[1])

    state_k = k_p.reshape(n_b, 1, seq, N_KV_HEADS, HEAD_DIM)
    state_v = v_p.reshape(n_b, 1, seq, N_KV_HEADS, HEAD_DIM)
    return (y_prompt, y_sample, state_k, state_v)
```

```python
import functools

import jax
import jax.numpy as jnp
import numpy as np
from jax import lax
from jax.experimental import pallas as pl
from jax.experimental.pallas import tpu as pltpu
from jax.experimental.pallas import tpu_sc as plsc

F32 = jnp.float32
BF16 = jnp.bfloat16
I32 = jnp.int32

D_MODEL = 2048
N_HEADS = 8
N_KV_HEADS = 2
HEAD_DIM = 128
Q_PER_KV = N_HEADS // N_KV_HEADS
ATTN_WIDTH = N_HEADS * HEAD_DIM
KV_WIDTH = N_KV_HEADS * HEAD_DIM
POOL_WIDTH = D_MODEL - ATTN_WIDTH
POOL_SIZES = (2, 4, 8, 16)
POOL_GROUP = POOL_WIDTH // len(POOL_SIZES)
IN_WIDTH = ATTN_WIDTH + 2 * KV_WIDTH + POOL_WIDTH
WINDOW = 128
BLOCK = 128
GRID_W = 64
ROPE_THETA = 10000.0
ROPE_FREQS = HEAD_DIM // 4
N_EXPERTS = 16
EC_FACTOR = 2
D_EXPERT = 1024
N_MOD = 6
EPS = 1e-6
NEG = -1e30
ATTN_SCALE = HEAD_DIM ** -0.5

LANES = 128
SUBLANES = 8
BF16_ROWS = 16
VMEM_CAP = 64 * 1024 * 1024
SC_CORES = 2
SC_SUBCORES = 16

MOD_ROWS = 8
TOKEN_TILE = 256
PROJ_TILE = 512
SUB_TILE = 256
HQ_TILES = D_MODEL // 2 // LANES + 1
GATHER_CHUNK = 128
FFN_RANGES = 2
POOL_HALO = 8
COMBINE_CHUNK = BF16_ROWS
COMBINE_WINDOW = 64


def _vmem_limit(nbytes):
    return int(min(VMEM_CAP - (4 << 20), max(nbytes, 16 << 20)))


def _mod_kernel(c_ref, w_ref, b_ref, o_ref):
    c = c_ref[...]
    s = c * jax.nn.sigmoid(c)
    o_ref[...] = jnp.dot(s.astype(BF16), w_ref[...].astype(BF16),
                         preferred_element_type=F32) + b_ref[...]


def _modulation(cond, w_ada, b_ada):
    n = w_ada.shape[1]
    tn = 1024
    return pl.pallas_call(
        _mod_kernel,
        grid=(n // tn,),
        in_specs=[pl.BlockSpec((MOD_ROWS, D_MODEL), lambda j: (0, 0)),
                  pl.BlockSpec((D_MODEL, tn), lambda j: (0, j)),
                  pl.BlockSpec((1, tn), lambda j: (0, j))],
        out_specs=pl.BlockSpec((MOD_ROWS, tn), lambda j: (0, j)),
        out_shape=jax.ShapeDtypeStruct((MOD_ROWS, n), F32),
        compiler_params=pltpu.CompilerParams(
            vmem_limit_bytes=_vmem_limit(3 * D_MODEL * tn * 4)),
        name="modulation",
    )(cond, w_ada, b_ada.reshape(1, n))


def _norm_mod(x, g, shift, scale):
    ms = jnp.mean(x * x, axis=-1, keepdims=True)
    y = x * lax.rsqrt(ms + EPS)
    return (y * g) * (1.0 + scale) + shift


def _inproj_kernel(*refs, rope):
    if rope:
        x_ref, mod_ref, g_ref, w_ref, cos_ref, sa_ref, sb_ref, q_ref, k_ref, v_ref, p_ref = refs
    else:
        x_ref, mod_ref, g_ref, w_ref, q_ref, k_ref, v_ref, p_ref, ks_ref, vs_ref = refs
    mod = mod_ref[0]
    for s in range(x_ref.shape[0] // SUB_TILE):
        rows = slice(s * SUB_TILE, (s + 1) * SUB_TILE)
        h = _norm_mod(x_ref[rows, :], g_ref[...], mod[0:1], mod[1:2])
        u = jnp.dot(h.astype(BF16), w_ref[...], preferred_element_type=F32)

        def rot(xh, rows=rows):
            return (xh * cos_ref[rows, :] + pltpu.roll(xh, LANES - ROPE_FREQS, 1) * sa_ref[rows, :]
                    + pltpu.roll(xh, ROPE_FREQS, 1) * sb_ref[rows, :])

        for hd in range(N_HEADS):
            xh = u[:, hd * HEAD_DIM:(hd + 1) * HEAD_DIM] * ATTN_SCALE
            q_ref[rows, hd * HEAD_DIM:(hd + 1) * HEAD_DIM] = (rot(xh) if rope else xh).astype(BF16)
        for hd in range(N_KV_HEADS):
            lo = ATTN_WIDTH + hd * HEAD_DIM
            xh = u[:, lo:lo + HEAD_DIM]
            k_ref[rows, hd * HEAD_DIM:(hd + 1) * HEAD_DIM] = rot(xh) if rope else xh
        v_ref[rows, :] = u[:, ATTN_WIDTH + KV_WIDTH:ATTN_WIDTH + 2 * KV_WIDTH]
        p_ref[rows, :] = u[:, ATTN_WIDTH + 2 * KV_WIDTH:]
        if not rope:
            for hd in range(N_KV_HEADS):
                state_rows = pl.ds(s * SUB_TILE * N_KV_HEADS + hd, SUB_TILE, stride=N_KV_HEADS)
                lo = ATTN_WIDTH + hd * HEAD_DIM
                ks_ref[state_rows, :] = u[:, lo:lo + HEAD_DIM]
                vs_ref[state_rows, :] = u[:, lo + KV_WIDTH:lo + KV_WIDTH + HEAD_DIM]


def _in_projection(x2, mod3, mod_row, g, w_in_bf, rope_tabs):
    t = x2.shape[0]
    tm = PROJ_TILE
    rope = rope_tabs is not None
    row = lambda i: (i, 0)
    in_specs = [pl.BlockSpec((tm, D_MODEL), row),
                pl.BlockSpec((1, N_MOD, D_MODEL), lambda i: (mod_row(i), 0, 0)),
                pl.BlockSpec((1, D_MODEL), lambda i: (0, 0)),
                pl.BlockSpec((D_MODEL, IN_WIDTH), lambda i: (0, 0))]
    args = [x2, mod3, g, w_in_bf]
    if rope:
        n_seq = rope_tabs[0].shape[0]
        seq_blocks = n_seq // tm
        for tab in rope_tabs:
            in_specs.append(pl.BlockSpec((tm, HEAD_DIM), lambda i: (i % seq_blocks, 0)))
            args.append(tab)
    out_specs = [pl.BlockSpec((tm, ATTN_WIDTH), row),
                 pl.BlockSpec((tm, KV_WIDTH), row),
                 pl.BlockSpec((tm, KV_WIDTH), row),
                 pl.BlockSpec((tm, POOL_WIDTH), row)]
    out_shape = [jax.ShapeDtypeStruct((t, ATTN_WIDTH), BF16),
                 jax.ShapeDtypeStruct((t, KV_WIDTH), F32),
                 jax.ShapeDtypeStruct((t, KV_WIDTH), F32),
                 jax.ShapeDtypeStruct((t, POOL_WIDTH), F32)]
    if not rope:
        for _ in range(2):
            out_specs.append(pl.BlockSpec((tm * N_KV_HEADS, HEAD_DIM), row))
            out_shape.append(jax.ShapeDtypeStruct((t * N_KV_HEADS, HEAD_DIM), F32))
    return pl.pallas_call(
        functools.partial(_inproj_kernel, rope=rope),
        grid=(t // tm,),
        in_specs=in_specs,
        out_specs=out_specs,
        out_shape=out_shape,
        compiler_params=pltpu.CompilerParams(
            vmem_limit_bytes=_vmem_limit(2 * D_MODEL * IN_WIDTH * 2 + 24 * tm * D_MODEL * 4)),
        name="in_projection",
    )(*args)


def _softmax_pv(s_list, v_list, sink_col):
    m = sink_col
    for s in s_list:
        m = jnp.maximum(m, jnp.max(s, axis=-1, keepdims=True))
    denom = jnp.exp(sink_col - m)
    out = None
    for s, v in zip(s_list, v_list):
        e = jnp.exp(s - m)
        denom = denom + jnp.sum(e, axis=-1, keepdims=True)
        o = jnp.dot(e.astype(BF16), v, preferred_element_type=F32)
        out = o if out is None else out + o
    return out * (1.0 / denom)


def _stack_heads(q, kv):
    return jnp.concatenate(
        [q[:, (kv * Q_PER_KV + g) * HEAD_DIM:(kv * Q_PER_KV + g + 1) * HEAD_DIM]
         for g in range(Q_PER_KV)], axis=0)


def _sink_column(sink_ref, kv, rows):
    r = lax.broadcasted_iota(I32, (Q_PER_KV * rows, 1), 0)
    col = jnp.zeros((Q_PER_KV * rows, 1), F32)
    for g in range(Q_PER_KV):
        col = jnp.where((r >= g * rows) & (r < (g + 1) * rows), sink_ref[kv * Q_PER_KV + g], col)
    return col


def _qk(q, k):
    return lax.dot_general(q, k, (((1,), (1,)), ((), ())), preferred_element_type=F32)


def _ctx_attn_kernel(sink_ref, q_ref, k_ref, v_ref, o_ref):
    rows = q_ref.shape[0]
    q = q_ref[...]
    for kv in range(N_KV_HEADS):
        kh = k_ref[:, kv * HEAD_DIM:(kv + 1) * HEAD_DIM].astype(BF16)
        vh = v_ref[:, kv * HEAD_DIM:(kv + 1) * HEAD_DIM].astype(BF16)
        qs = _stack_heads(q, kv)
        s = _qk(qs, kh)
        o = _softmax_pv([s], [vh], _sink_column(sink_ref, kv, rows))
        for g in range(Q_PER_KV):
            hd = kv * Q_PER_KV + g
            o_ref[:, hd * HEAD_DIM:(hd + 1) * HEAD_DIM] = o[g * rows:(g + 1) * rows].astype(BF16)


def _context_attention(q, k, v, sink, seq):
    t = q.shape[0]
    row = lambda b: (b, 0)
    return pl.pallas_call(
        _ctx_attn_kernel,
        grid=(t // seq,),
        in_specs=[pl.BlockSpec(memory_space=pltpu.SMEM),
                  pl.BlockSpec((seq, ATTN_WIDTH), row),
                  pl.BlockSpec((seq, KV_WIDTH), row),
                  pl.BlockSpec((seq, KV_WIDTH), row)],
        out_specs=pl.BlockSpec((seq, ATTN_WIDTH), row),
        out_shape=jax.ShapeDtypeStruct((t, ATTN_WIDTH), BF16),
        name="context_attention",
    )(sink, q, k, v)


def _lat_attn_kernel(sink_ref, q_ref, k_ref, v_ref, ck_ref, cv_ref, o_ref, *, n_seq):
    i = pl.program_id(1)
    band = 3 * BLOCK
    start = pl.multiple_of(jnp.clip((i - 1) * BLOCK, 0, n_seq - band), BLOCK)
    rows = Q_PER_KV * BLOCK
    qpos = i * BLOCK + lax.broadcasted_iota(I32, (rows, band), 0) % BLOCK
    kpos = start + lax.broadcasted_iota(I32, (rows, band), 1)
    mask = jnp.abs(kpos - qpos) <= WINDOW
    q = q_ref[...]
    for kv in range(N_KV_HEADS):
        cols = slice(kv * HEAD_DIM, (kv + 1) * HEAD_DIM)
        kb = k_ref[pl.ds(start, band), cols].astype(BF16)
        vb = v_ref[pl.ds(start, band), cols].astype(BF16)
        ck = ck_ref[0, :, cols].astype(BF16)
        cv = cv_ref[0, :, cols].astype(BF16)
        qs = _stack_heads(q, kv)
        s_loc = jnp.where(mask, _qk(qs, kb), NEG)
        s_ctx = _qk(qs, ck)
        o = _softmax_pv([s_loc, s_ctx], [vb, cv], _sink_column(sink_ref, kv, BLOCK))
        for g in range(Q_PER_KV):
            hd = kv * Q_PER_KV + g
            o_ref[:, hd * HEAD_DIM:(hd + 1) * HEAD_DIM] = o[g * BLOCK:(g + 1) * BLOCK].astype(BF16)


def _latent_attention(q, k, v, ck, cv, sink, n_seq):
    t = q.shape[0]
    nb = n_seq // BLOCK
    past = ck.shape[1]
    return pl.pallas_call(
        functools.partial(_lat_attn_kernel, n_seq=n_seq),
        grid=(t // n_seq, nb),
        in_specs=[pl.BlockSpec(memory_space=pltpu.SMEM),
                  pl.BlockSpec((BLOCK, ATTN_WIDTH), lambda b, i: (b * nb + i, 0)),
                  pl.BlockSpec((n_seq, KV_WIDTH), lambda b, i: (b, 0)),
                  pl.BlockSpec((n_seq, KV_WIDTH), lambda b, i: (b, 0)),
                  pl.BlockSpec((1, past, KV_WIDTH), lambda b, i: (b, 0, 0)),
                  pl.BlockSpec((1, past, KV_WIDTH), lambda b, i: (b, 0, 0))],
        out_specs=pl.BlockSpec((BLOCK, ATTN_WIDTH), lambda b, i: (b * nb + i, 0)),
        out_shape=jax.ShapeDtypeStruct((t, ATTN_WIDTH), BF16),
        name="latent_attention",
    )(sink, q, k, v, ck, cv)


def _pool_group(p_ref, r0, seq, w_ref, s_ref, g):
    n = SUB_TILE
    rows = n + 2 * POOL_HALO
    static = isinstance(r0, int)
    t0 = r0 % seq
    t = t0 + lax.broadcasted_iota(I32, (n, 1), 0)
    has_top = t0 > 0
    has_bottom = t0 + n < seq
    zeros = jnp.zeros((POOL_HALO, POOL_GROUP), F32)
    w = POOL_SIZES[g]
    cols = slice(g * POOL_GROUP, (g + 1) * POOL_GROUP)
    pg = p_ref[pl.ds(r0, n), cols]
    if static:
        top = p_ref[r0 - POOL_HALO:r0, cols] if has_top else zeros
        bottom = p_ref[r0 + n:r0 + n + POOL_HALO, cols] if has_bottom else zeros
    else:
        top_row = pl.multiple_of(jnp.maximum(r0 - POOL_HALO, 0), POOL_HALO)
        bottom_row = pl.multiple_of(jnp.minimum(r0 + n, p_ref.shape[0] - POOL_HALO), POOL_HALO)
        top = jnp.where(has_top, p_ref[pl.ds(top_row, POOL_HALO), cols], 0.0)
        bottom = jnp.where(has_bottom, p_ref[pl.ds(bottom_row, POOL_HALO), cols], 0.0)
    x = jnp.concatenate([top, pg, bottom], axis=0)
    fwd = x
    span = 1
    while span < w // 2:
        fwd = fwd + pltpu.roll(fwd, rows - span, 0)
        span *= 2
    if (w // 2) % SUBLANES == 0:
        wsum = fwd[POOL_HALO - w // 2:POOL_HALO - w // 2 + n] + fwd[POOL_HALO:POOL_HALO + n]
    else:
        wsum = (fwd + pltpu.roll(fwd, w // 2, 0))[POOL_HALO:POOL_HALO + n]
    lo = jnp.maximum(t - w // 2, 0)
    hi = jnp.minimum(t + w - w // 2, seq)
    inv_cnt = 1.0 / (hi - lo).astype(F32)
    mixed = wsum * inv_cnt - pg
    y = jnp.dot(mixed.astype(BF16), w_ref[g], preferred_element_type=F32)
    return (y * s_ref[:, cols]).astype(BF16)


def _pack_pair(lo, hi):
    return lax.bitcast_convert_type(pltpu.pack_elementwise([lo, hi], packed_dtype=BF16), I32)


def _unpack_pair(words):
    lo = pltpu.unpack_elementwise(words, index=0, packed_dtype=BF16, unpacked_dtype=F32)
    hi = pltpu.unpack_elementwise(words, index=1, packed_dtype=BF16, unpacked_dtype=F32)
    return lo, hi


def _outproj_kernel(a_ref, p_ref, x_ref, mod_ref, g_ref, wo_ref, wr_ref, pw_ref, ps_ref,
                    x1_ref, h_ref, aff_ref, *, seq):
    mod = mod_ref[0]
    groups = SUB_TILE // SUBLANES
    half = D_MODEL // 2
    tm = x_ref.shape[0]
    steps_per_p_block = p_ref.shape[0] // tm
    for s in range(tm // SUB_TILE):
        rows = slice(s * SUB_TILE, (s + 1) * SUB_TILE)
        grp = slice(s * groups, (s + 1) * groups)
        r0 = s * SUB_TILE
        if steps_per_p_block > 1:
            r0 = pl.multiple_of((pl.program_id(0) % steps_per_p_block) * tm + r0, SUB_TILE)
        mix = jnp.dot(a_ref[rows, :], wo_ref[0:ATTN_WIDTH, :], preferred_element_type=F32)
        for g in range(len(POOL_SIZES)):
            lo = ATTN_WIDTH + g * POOL_GROUP
            mix = mix + jnp.dot(_pool_group(p_ref, r0, seq, pw_ref, ps_ref, g),
                                wo_ref[lo:lo + POOL_GROUP, :], preferred_element_type=F32)
        x1 = x_ref[rows, :] + mod[2:3] * mix
        x1_ref[rows, :] = x1
        h = _norm_mod(x1, g_ref[...], mod[3:4], mod[4:5])
        logits = jnp.dot(h.astype(BF16), wr_ref[...], preferred_element_type=F32)
        lane = lax.broadcasted_iota(I32, logits.shape, 1)
        logits = jnp.where(lane < N_EXPERTS, logits, -jnp.inf)
        m = jnp.max(logits, axis=-1, keepdims=True)
        e = jnp.exp(logits - m)
        aff = e / jnp.sum(e, axis=-1, keepdims=True)
        aff_ref[rows, :] = aff[:, 0:N_EXPERTS]
        for c in range(HQ_TILES - 1):
            words = _pack_pair(h[:, c * LANES:(c + 1) * LANES],
                               h[:, half + c * LANES:half + (c + 1) * LANES])
            h_ref[grp, c * SUBLANES:(c + 1) * SUBLANES, :] = (
                words.reshape(groups, SUBLANES, LANES))
        h_ref[grp, (HQ_TILES - 1) * SUBLANES:, :] = (
            lax.bitcast_convert_type(aff, I32).reshape(groups, SUBLANES, LANES))


def _out_projection(attn, p, x2, mod3, mod_row, g, w_out_bf, w_router, pool_w, pool_scale, seq):
    t = x2.shape[0]
    tm = PROJ_TILE
    row = lambda i: (i, 0)
    p_rows = max(tm, seq)
    steps_per_p_block = p_rows // tm
    return pl.pallas_call(
        functools.partial(_outproj_kernel, seq=seq),
        grid=(t // tm,),
        in_specs=[pl.BlockSpec((tm, ATTN_WIDTH), row),
                  pl.BlockSpec((p_rows, POOL_WIDTH), lambda i: (i // steps_per_p_block, 0)),
                  pl.BlockSpec((tm, D_MODEL), row),
                  pl.BlockSpec((1, N_MOD, D_MODEL), lambda i: (mod_row(i), 0, 0)),
                  pl.BlockSpec((1, D_MODEL), lambda i: (0, 0)),
                  pl.BlockSpec((D_MODEL, D_MODEL), lambda i: (0, 0)),
                  pl.BlockSpec((D_MODEL, LANES), lambda i: (0, 0)),
                  pl.BlockSpec((len(POOL_SIZES), POOL_GROUP, POOL_GROUP), lambda i: (0, 0, 0)),
                  pl.BlockSpec((1, POOL_WIDTH), lambda i: (0, 0))],
        out_specs=[pl.BlockSpec((tm, D_MODEL), row),
                   pl.BlockSpec((tm // SUBLANES, HQ_TILES * SUBLANES, LANES), lambda i: (i, 0, 0)),
                   pl.BlockSpec((tm, N_EXPERTS), row)],
        out_shape=[jax.ShapeDtypeStruct((t, D_MODEL), F32),
                   jax.ShapeDtypeStruct((t // SUBLANES, HQ_TILES * SUBLANES, LANES), I32),
                   jax.ShapeDtypeStruct((t, N_EXPERTS), F32)],
        compiler_params=pltpu.CompilerParams(
            vmem_limit_bytes=_vmem_limit(2 * D_MODEL * D_MODEL * 2 + 24 * tm * D_MODEL * 4)),
        name="out_projection",
    )(attn, p, x2, mod3, g, w_out_bf, w_router, pool_w, pool_scale)


def _route_kernel(a_ref, idx_ref, slot_ref, off_ref, *, cap):
    a = a_ref[...]
    n_e, n_c, _ = a.shape
    rows = n_e * n_c

    def bisect(it, thr):
        cand = thr | jnp.left_shift(jnp.int32(1), 30 - it)
        cand_f = lax.bitcast_convert_type(cand, F32)
        cnt = jnp.sum(jnp.sum((a >= cand_f).astype(F32), axis=2, keepdims=True),
                      axis=1, keepdims=True)
        return jnp.where(cnt >= cap, cand, thr)

    thr = lax.fori_loop(0, 31, bisect, jnp.zeros((n_e, 1, 1), I32))
    thr_f = lax.bitcast_convert_type(thr, F32)
    gt = (a > thr_f).astype(F32).reshape(rows, LANES)
    eq = (a == thr_f).astype(F32).reshape(rows, LANES)

    li = lax.broadcasted_iota(I32, (LANES, LANES), 0)
    lj = lax.broadcasted_iota(I32, (LANES, LANES), 1)
    upper_incl = (li <= lj).astype(BF16)
    ri = lax.broadcasted_iota(I32, (rows, rows), 0)
    rj = lax.broadcasted_iota(I32, (rows, rows), 1)
    same_expert = (ri // n_c) == (rj // n_c)
    before = (same_expert & (rj < ri)).astype(BF16)
    whole = same_expert.astype(BF16)

    def lane_bcast(col):
        return jnp.broadcast_to(col, (rows, LANES)).astype(BF16)

    def prefix(x):
        incl = jnp.dot(x.astype(BF16), upper_incl, preferred_element_type=F32)
        tot = incl[:, LANES - 1:LANES]
        off = jnp.dot(before, lane_bcast(tot), preferred_element_type=F32)
        return incl, tot, off

    n_gt = jnp.dot(whole, lane_bcast(jnp.sum(gt, axis=1, keepdims=True)),
                   preferred_element_type=F32)
    need = cap - n_gt
    incl_eq, _, off_eq = prefix(eq)
    rank_eq = off_eq + incl_eq - eq
    sel = jnp.where((eq > 0) & (rank_eq < need), 1.0, gt)
    incl, tot, off = prefix(sel)
    slot = off + incl - sel
    slot_ref[...] = jnp.where(sel > 0, slot, -1.0).astype(I32).reshape(n_e, n_c, LANES)
    off_ref[...] = off.astype(I32).reshape(n_e, n_c, LANES)

    s_lane = lax.broadcasted_iota(I32, (1, cap), 1).astype(F32)
    c_col = lax.broadcasted_iota(I32, (n_c, 1), 0).astype(F32)
    for e in range(n_e):
        r0 = e * n_c
        incl_e = incl[r0:r0 + n_c]
        off_e = off[r0:r0 + n_c, 0:1]
        tot_e = tot[r0:r0 + n_c]
        onehot = ((off_e <= s_lane) & (s_lane < off_e + tot_e)).astype(F32)
        counts = lax.dot_general(incl_e.astype(BF16), onehot.astype(BF16),
                                 (((0,), (0,)), ((), ())), preferred_element_type=F32)
        local = s_lane - jnp.sum(onehot * off_e, axis=0, keepdims=True)
        lane = jnp.sum((counts <= local).astype(F32), axis=0, keepdims=True)
        chunk = jnp.sum(onehot * c_col, axis=0, keepdims=True)
        idx_ref[e] = (chunk * LANES + lane).astype(I32)


def _routing(aff, cap):
    t = aff.shape[0]
    n_c = t // LANES
    a3 = aff.T.reshape(N_EXPERTS, n_c, LANES)
    return pl.pallas_call(
        functools.partial(_route_kernel, cap=cap),
        out_shape=[jax.ShapeDtypeStruct((N_EXPERTS, 1, cap), I32),
                   jax.ShapeDtypeStruct((N_EXPERTS, n_c, LANES), I32),
                   jax.ShapeDtypeStruct((N_EXPERTS, n_c, LANES), I32)],
        compiler_params=pltpu.CompilerParams(vmem_limit_bytes=_vmem_limit(48 << 20)),
        name="routing",
    )(a3)


def _gather_rows(table, row_ids):
    n_chunks = row_ids.shape[0]
    n_workers = SC_CORES * SC_SUBCORES
    per_worker = n_chunks // n_workers
    assert row_ids.shape[1] == GATHER_CHUNK and n_chunks % n_workers == 0
    mesh = plsc.VectorSubcoreMesh(core_axis_name="core", subcore_axis_name="subcore")

    @functools.partial(
        pl.kernel, mesh=mesh,
        out_type=jax.ShapeDtypeStruct((n_chunks * GATHER_CHUNK, LANES), I32),
        scratch_types=[pltpu.VMEM((per_worker, GATHER_CHUNK), I32),
                       pltpu.VMEM((2, GATHER_CHUNK, LANES), I32),
                       pltpu.SemaphoreType.DMA((2,)),
                       pltpu.SemaphoreType.DMA((2,))],
        name="gather_rows",
    )
    def gather(table_hbm, ids_hbm, out_hbm, ids_v, rows_v, gather_sem, store_sem):
        worker = lax.axis_index("subcore") * SC_CORES + lax.axis_index("core")
        first = worker * per_worker
        pltpu.sync_copy(ids_hbm.at[worker], ids_v)

        def fetch(j):
            return pltpu.make_async_copy(table_hbm.at[ids_v.at[j]], rows_v.at[j % 2],
                                         gather_sem.at[j % 2])

        def store(j):
            rows = pl.ds(pl.multiple_of((first + j) * GATHER_CHUNK, GATHER_CHUNK), GATHER_CHUNK)
            return pltpu.make_async_copy(rows_v.at[j % 2], out_hbm.at[rows], store_sem.at[j % 2])

        for j in range(per_worker):
            if j >= 2:
                store(j - 2).wait()
            fetch(j).start()
            if j >= 1:
                fetch(j - 1).wait()
                store(j - 1).start()
        fetch(per_worker - 1).wait()
        store(per_worker - 1).start()
        for j in range(max(per_worker - 2, 0), per_worker):
            store(j).wait()

    return gather(table, row_ids.reshape(n_workers, per_worker, GATHER_CHUNK))


def _packed_row_ids(idx, cap):
    tok = idx.reshape(-1, 1, cap // GATHER_CHUNK, GATHER_CHUNK)
    tile = jnp.arange(HQ_TILES, dtype=I32).reshape(1, HQ_TILES, 1, 1)
    ids = ((tok // SUBLANES) * HQ_TILES + tile) * SUBLANES + tok % SUBLANES
    return ids.reshape(-1, GATHER_CHUNK)


def _ffn_kernel(*refs, caps, n_f, row_chunk, first_expert, n_prior):
    n_g = len(caps)
    x_refs = refs[:n_g]
    wg_ref, wu_ref, wd_ref = refs[n_g:n_g + 3]
    y_refs = refs[n_g + 3 + n_prior:2 * n_g + 3 + n_prior]
    acc_ref = refs[2 * n_g + 3 + n_prior]
    e = pl.program_id(0)
    f = pl.program_id(1)

    @pl.when((e == 0) & (f == 0))
    def _():
        acc_ref[...] = jnp.zeros_like(acc_ref)

    def step(last):
        first = f == 0
        base = 0
        for x_ref, y_ref, cap in zip(x_refs, y_refs, caps):
            if last:
                aff = lax.bitcast_convert_type(x_ref[0, HQ_TILES - 1], F32)
                lane = lax.broadcasted_iota(I32, aff.shape, 1)
                gates = jnp.sum(jnp.where(lane == e + first_expert, aff, 0.0), axis=1,
                                keepdims=True)
                y_ref[0, cap:, :] = jnp.zeros((COMBINE_WINDOW, D_MODEL // 2), I32)
            for r in range(cap // row_chunk):
                rs = slice(r * row_chunk, (r + 1) * row_chunk)
                acc_rows = slice(base + r * row_chunk, base + (r + 1) * row_chunk)
                pairs = [_unpack_pair(x_ref[0, c, rs, :]) for c in range(HQ_TILES - 1)]
                x = jnp.concatenate([p[0] for p in pairs] + [p[1] for p in pairs], axis=1)
                gate_act = jnp.dot(x, wg_ref[0], preferred_element_type=F32)
                up = jnp.dot(x, wu_ref[0], preferred_element_type=F32)
                hid = (gate_act * jax.nn.sigmoid(gate_act)) * up
                part = jnp.dot(hid, wd_ref[0], preferred_element_type=F32)
                if last:
                    y = (part + acc_ref[acc_rows, :]) * gates[rs]
                    y_ref[0, rs, :] = _pack_pair(y[:, 0:D_MODEL // 2], y[:, D_MODEL // 2:])
                else:
                    acc_ref[acc_rows, :] = part + jnp.where(first, 0.0, acc_ref[acc_rows, :])
            base += cap

    @pl.when(f < n_f - 1)
    def _():
        step(False)

    @pl.when(f == n_f - 1)
    def _():
        step(True)


def _expert_ffn(xs_groups, w_gate, w_up, w_down, first_expert, prior_outputs):
    n_f = 4
    tf = D_EXPERT // n_f
    row_chunk = 512
    n_e = xs_groups[0].shape[0]
    e0 = first_expert
    caps = tuple(xs.shape[2] for xs in xs_groups)
    rows = sum(caps)
    est = (2 * HQ_TILES * rows * LANES * 4 + rows * D_MODEL * (4 + 2 * 2) + 2 * 3 * D_MODEL * tf * 4
           + row_chunk * (3 * tf + 2 * D_MODEL) * 4)
    x_specs = [pl.BlockSpec((1, HQ_TILES, cap, LANES), lambda e, f: (e, 0, 0, 0)) for cap in caps]
    y_shapes = [(N_EXPERTS, cap + COMBINE_WINDOW, D_MODEL // 2) for cap in caps]
    y_specs = [pl.BlockSpec((1,) + s[1:], lambda e, f: (e + e0, 0, 0)) for s in y_shapes]
    prior = list(prior_outputs or [])
    n_in = len(caps) + 3
    return pl.pallas_call(
        functools.partial(_ffn_kernel, caps=caps, n_f=n_f, row_chunk=row_chunk,
                          first_expert=e0, n_prior=len(prior)),
        grid=(n_e, n_f),
        in_specs=(x_specs + [pl.BlockSpec((1, D_MODEL, tf), lambda e, f: (e + e0, 0, f)),
                             pl.BlockSpec((1, D_MODEL, tf), lambda e, f: (e + e0, 0, f)),
                             pl.BlockSpec((1, tf, D_MODEL), lambda e, f: (e + e0, f, 0))]
                  + [pl.BlockSpec(memory_space=pl.ANY) for _ in prior]),
        out_specs=y_specs,
        out_shape=[jax.ShapeDtypeStruct(s, I32) for s in y_shapes],
        input_output_aliases={n_in + j: j for j in range(len(prior))},
        scratch_shapes=[pltpu.VMEM((rows, D_MODEL), F32)],
        compiler_params=pltpu.CompilerParams(
            dimension_semantics=("arbitrary", "arbitrary"),
            vmem_limit_bytes=_vmem_limit(est + (4 << 20))),
        name="expert_ffn",
    )(*xs_groups, w_gate, w_up, w_down, *prior)


def _combine_kernel(start_ref, nch_ref, wide_ref, x1_ref, mod_ref, g_ref, slot_ref, y_hbm, o_ref,
                    ybuf_ref, cols_ref, ffn_ref, sem, *, first_k, kblock):
    i = pl.program_id(0)
    tm = x1_ref.shape[0]
    cur = i % 2
    n_window_rows = N_EXPERTS * COMBINE_WINDOW

    def window_copy(e, src_row, buf):
        return pltpu.make_async_copy(
            y_hbm.at[e, pl.ds(pl.multiple_of(src_row, SUBLANES), COMBINE_WINDOW)],
            ybuf_ref.at[buf, e * COMBINE_WINDOW:(e + 1) * COMBINE_WINDOW], sem.at[buf])

    def chunk_copy(e, src_row, buf, dst_row):
        return pltpu.make_async_copy(
            y_hbm.at[e, pl.ds(pl.multiple_of(src_row, SUBLANES), COMBINE_CHUNK)],
            ybuf_ref.at[buf, pl.ds(pl.multiple_of(dst_row, COMBINE_CHUNK), COMBINE_CHUNK)],
            sem.at[buf])

    def fetch(tile, buf):
        @pl.when(wide_ref[tile] == 0)
        def _():
            for e in range(N_EXPERTS):
                window_copy(e, start_ref[tile, e], buf).start()

        @pl.when(wide_ref[tile] != 0)
        def _():
            pos = jnp.int32(0)
            for e in range(N_EXPERTS):
                start = start_ref[tile, e]
                nch = nch_ref[tile, e]

                def issue(c, carry, e=e, start=start, pos=pos):
                    chunk_copy(e, start + c * COMBINE_CHUNK, buf, pos + c * COMBINE_CHUNK).start()
                    return carry
                lax.fori_loop(0, nch, issue, 0)
                pos = pos + nch * COMBINE_CHUNK

    @pl.when(i == 0)
    def _():
        ybuf_ref[...] = jnp.zeros_like(ybuf_ref)
        fetch(0, 0)

    @pl.when(i + 1 < pl.num_programs(0))
    def _():
        fetch(i + 1, 1 - cur)

    def finish(ffn):
        out = x1_ref[...] + mod_ref[0][5:6] * ffn
        ms = jnp.mean(out * out, axis=-1, keepdims=True)
        o_ref[...] = (out * lax.rsqrt(ms + EPS)) * g_ref[...]

    def apply_selection(sel, k0, width):
        lo, hi = _unpack_pair(ybuf_ref[cur, pl.ds(k0, width), :])
        return jnp.concatenate([jnp.dot(sel, lo, preferred_element_type=F32),
                                jnp.dot(sel, hi, preferred_element_type=F32)], axis=1)

    @pl.when(wide_ref[i] == 0)
    def _():
        for e in range(N_EXPERTS):
            window_copy(e, jnp.int32(0), cur).wait()
        expert_lane = lax.broadcasted_iota(I32, (1, N_EXPERTS), 1)
        starts = jnp.zeros((1, N_EXPERTS), I32)
        for e in range(N_EXPERTS):
            starts = jnp.where(expert_lane == e, start_ref[i, e], starts)
        slots = slot_ref[...]
        rows_in_window = jnp.where(slots >= 0, (slots - starts).astype(F32), -1.0)
        window_of_lane = lax.broadcasted_iota(I32, (N_EXPERTS, n_window_rows), 1) // COMBINE_WINDOW
        spread = (window_of_lane == lax.broadcasted_iota(I32, (N_EXPERTS, n_window_rows), 0))
        target = jnp.dot(rows_in_window.astype(BF16), spread.astype(BF16),
                         preferred_element_type=F32)
        lane_row = (lax.broadcasted_iota(I32, (tm, n_window_rows), 1) % COMBINE_WINDOW).astype(F32)
        sel = jnp.where(target == lane_row, 1.0, 0.0)
        ffn = None
        for k0 in range(0, n_window_rows, kblock):
            part = apply_selection(sel[:, k0:k0 + kblock], k0, kblock)
            ffn = part if ffn is None else ffn + part
        finish(ffn)

    @pl.when(wide_ref[i] != 0)
    def _():
        pos = jnp.int32(0)
        begins = []
        for e in range(N_EXPERTS):
            begins.append(pos)
            pos = pos + nch_ref[i, e] * COMBINE_CHUNK
        begins.append(pos)

        def drain(c, carry):
            chunk_copy(0, jnp.int32(0), cur, jnp.int32(0)).wait()
            return carry
        lax.fori_loop(0, pos // COMBINE_CHUNK, drain, 0)

        slots = slot_ref[...]
        for e in range(N_EXPERTS):
            s_e = slots[:, e:e + 1]
            col = jnp.where(s_e >= 0, s_e + (begins[e] - start_ref[i, e]), -1)
            cols_ref[e] = jnp.broadcast_to(col, (tm, LANES))

        lane = lax.broadcasted_iota(I32, (tm, LANES), 1)

        def selection(k0, width):
            halves = []
            for h in range(width // LANES):
                c0 = k0 + h * LANES
                target = lane + c0
                e_lo = jnp.int32(0)
                e_hi = jnp.int32(0)
                for e in range(N_EXPERTS):
                    e_lo = e_lo + (begins[e + 1] <= c0).astype(I32)
                    e_hi = e_hi + (begins[e] < c0 + LANES).astype(I32)

                def mark(e, hit, target=target):
                    return jnp.where(cols_ref[e] == target, 1.0, hit)
                halves.append(lax.fori_loop(e_lo, e_hi, mark, jnp.zeros((tm, LANES), F32)))
            return jnp.concatenate(halves, axis=1)

        ffn_ref[...] = apply_selection(selection(0, first_k), 0, first_k)

        def kstep(kb, carry):
            k0 = pl.multiple_of(kb * kblock, kblock)
            ffn_ref[...] += apply_selection(selection(k0, kblock), k0, kblock)
            return carry
        lax.fori_loop(first_k // kblock, (pos + kblock - 1) // kblock, kstep, 0)
        finish(ffn_ref[...])


def _combine(x1, mod3, mod_row, g_final, slot_t, tile_start, tile_nch, tile_wide, y):
    t = x1.shape[0]
    tm = TOKEN_TILE
    first_k = 2 * tm + N_EXPERTS * COMBINE_CHUNK
    kblock = 256
    max_rows = N_EXPERTS * (tm + 2 * COMBINE_CHUNK)
    max_rows = -(-max_rows // kblock) * kblock
    row = lambda i, *_: (i, 0)
    grid_spec = pltpu.PrefetchScalarGridSpec(
        num_scalar_prefetch=3,
        grid=(t // tm,),
        in_specs=[pl.BlockSpec((tm, D_MODEL), row),
                  pl.BlockSpec((1, N_MOD, D_MODEL), lambda i, *_: (mod_row(i), 0, 0)),
                  pl.BlockSpec((1, D_MODEL), lambda i, *_: (0, 0)),
                  pl.BlockSpec((tm, N_EXPERTS), row),
                  pl.BlockSpec(memory_space=pl.ANY)],
        out_specs=pl.BlockSpec((tm, D_MODEL), row),
        scratch_shapes=[pltpu.VMEM((2, max_rows, D_MODEL // 2), I32),
                        pltpu.VMEM((N_EXPERTS, tm, LANES), I32),
                        pltpu.VMEM((tm, D_MODEL), F32),
                        pltpu.SemaphoreType.DMA((2,))],
    )
    return pl.pallas_call(
        functools.partial(_combine_kernel, first_k=first_k, kblock=kblock),
        grid_spec=grid_spec,
        out_shape=jax.ShapeDtypeStruct((t, D_MODEL), F32),
        compiler_params=pltpu.CompilerParams(
            dimension_semantics=("arbitrary",),
            vmem_limit_bytes=_vmem_limit(2 * max_rows * D_MODEL * 2 + 16 * tm * D_MODEL * 4)),
        name="combine",
    )(tile_start, tile_nch, tile_wide, x1, mod3, g_final, slot_t, y)


def _rope_tables(n):
    rows = n // GRID_W
    row = jnp.repeat(jnp.arange(rows, dtype=F32), GRID_W)
    col = jnp.tile(jnp.arange(GRID_W, dtype=F32), rows)
    inv = ROPE_THETA ** (-jnp.arange(ROPE_FREQS, dtype=F32) / ROPE_FREQS)
    ang_r = row[:, None] * inv
    ang_c = col[:, None] * inv
    zero = jnp.zeros_like(ang_r)
    cos = jnp.concatenate([jnp.cos(ang_r)] * 2 + [jnp.cos(ang_c)] * 2, axis=1)
    sin_a = jnp.concatenate([-jnp.sin(ang_r), zero, -jnp.sin(ang_c), zero], axis=1)
    sin_b = jnp.concatenate([zero, jnp.sin(ang_r), zero, jnp.sin(ang_c)], axis=1)
    return cos, sin_a, sin_b


def _token_group(x, mod3, mod_row, seq, weights, rope_tabs, ctx_kv):
    (norm_mix, w_in_bf, sink, pool_w, pool_scale, w_out_bf, norm_ffn, w_router,
     w_gate, w_up, w_down, norm_final) = weights
    b = x.shape[0]
    t = b * seq
    x2 = x.reshape(t, D_MODEL)
    q, k, v, p, *state = _in_projection(x2, mod3, mod_row(PROJ_TILE), norm_mix, w_in_bf,
                                        rope_tabs)
    if ctx_kv is None:
        attn = _context_attention(q, k, v, sink, seq)
    else:
        attn = _latent_attention(q, k, v, ctx_kv[0], ctx_kv[1], sink, seq)
    x1, h_packed, aff = _out_projection(attn, p, x2, mod3, mod_row(PROJ_TILE), norm_ffn,
                                        w_out_bf, w_router, pool_w, pool_scale, seq)

    cap = EC_FACTOR * t // N_EXPERTS
    idx, slot3, off3 = _routing(aff, cap)
    table = h_packed.reshape(-1, LANES)
    per_range = N_EXPERTS // FFN_RANGES
    xs = []
    for r in range(FFN_RANGES):
        ids = _packed_row_ids(idx[r * per_range:(r + 1) * per_range], cap)
        xs.append(_gather_rows(table, ids).reshape(per_range, HQ_TILES, cap, LANES))

    chunks_per_tile = TOKEN_TILE // LANES
    tile_off = off3[:, ::chunks_per_tile, 0]
    tile_end = jnp.concatenate([tile_off[:, 1:], jnp.full((N_EXPERTS, 1), cap, I32)], axis=1)
    tile_start = (tile_off // SUBLANES) * SUBLANES
    tile_nch = jnp.where(tile_end > tile_off,
                         (tile_end - tile_start + COMBINE_CHUNK - 1) // COMBINE_CHUNK, 0)
    tile_wide = jnp.any(tile_end - tile_start > COMBINE_WINDOW, axis=0).astype(I32)
    slot_t = slot3.reshape(N_EXPERTS, t).T

    def finish(y):
        out = _combine(x1, mod3, mod_row(TOKEN_TILE), norm_final, slot_t, tile_start.T,
                       tile_nch.T, tile_wide, y)
        return out.reshape(b, seq, D_MODEL)
    return xs, finish, state


def kernel(x_prompt, x_sample, c, cache_k, cache_v, c_ctx, w_ada, b_ada, norm_mix, w_in,
           sink_logits, pool_w, pool_scale, w_out, norm_ffn, w_router, w_gate, w_up, w_down,
           norm_final):
    n_b, seq, _ = x_prompt.shape
    n_db, n_lat, _ = x_sample.shape
    assert 1 + n_db <= MOD_ROWS and seq == TOKEN_TILE and n_lat % PROJ_TILE == 0
    assert (n_b * seq) % PROJ_TILE == 0

    cond = jnp.concatenate(
        [c_ctx[None, :], c, jnp.zeros((MOD_ROWS - 1 - n_db, D_MODEL), F32)], axis=0)
    mod3 = _modulation(cond, w_ada[0], b_ada[0]).reshape(MOD_ROWS, N_MOD, D_MODEL)

    w_router_bf = jnp.pad(w_router[0], ((0, 0), (0, LANES - N_EXPERTS))).astype(BF16)
    weights = (norm_mix[0][None, :], w_in[0].astype(BF16), sink_logits[0], pool_w[0].astype(BF16),
               pool_scale[0][None, :], w_out[0].astype(BF16), norm_ffn[0][None, :], w_router_bf,
               w_gate.reshape(w_gate.shape[1:]), w_up.reshape(w_up.shape[1:]),
               w_down.reshape(w_down.shape[1:]), norm_final[None, :])

    xs_p, finish_p, (k_p, v_p) = _token_group(
        x_prompt, mod3, lambda tile: (lambda i: 0), seq, weights, None, None)

    past = cache_k.shape[2]
    ck = cache_k[:, 0].reshape(n_db, past, KV_WIDTH)
    cv = cache_v[:, 0].reshape(n_db, past, KV_WIDTH)
    xs_l, finish_l, _ = _token_group(
        x_sample, mod3, lambda tile: (lambda i: 1 + i // (n_lat // tile)), n_lat, weights,
        _rope_tables(n_lat), (ck, cv))

    ys = None
    for r in range(FFN_RANGES):
        ys = _expert_ffn([xs_p[r], xs_l[r]], weights[8], weights[9], weights[10],
                         r * (N_EXPERTS // FFN_RANGES), ys)
    y_prompt = finish_p(ys[0])
    y_sample = finish_l(ys[1])

    state_k = k_p.reshape(n_b, 1, seq, N_KV_HEADS, HEAD_DIM)
    state_v = v_p.reshape(n_b, 1, seq, N_KV_HEADS, HEAD_DIM)
    return (y_prompt, y_sample, state_k, state_v)
```

```python
import functools

import jax
import jax.numpy as jnp
import numpy as np
from jax import lax
from jax.experimental import pallas as pl
from jax.experimental.pallas import tpu as pltpu
from jax.experimental.pallas import tpu_sc as plsc

F32 = jnp.float32
BF16 = jnp.bfloat16
I32 = jnp.int32

D_MODEL = 2048
N_HEADS = 8
N_KV_HEADS = 2
HEAD_DIM = 128
Q_PER_KV = N_HEADS // N_KV_HEADS
ATTN_WIDTH = N_HEADS * HEAD_DIM
KV_WIDTH = N_KV_HEADS * HEAD_DIM
POOL_WIDTH = D_MODEL - ATTN_WIDTH
POOL_SIZES = (2, 4, 8, 16)
POOL_GROUP = POOL_WIDTH // len(POOL_SIZES)
IN_WIDTH = ATTN_WIDTH + 2 * KV_WIDTH + POOL_WIDTH
WINDOW = 128
BLOCK = 128
GRID_W = 64
ROPE_THETA = 10000.0
ROPE_FREQS = HEAD_DIM // 4
N_EXPERTS = 16
EC_FACTOR = 2
D_EXPERT = 1024
N_MOD = 6
EPS = 1e-6
NEG = -1e30
LOG2_E = 1.4426950408889634
ATTN_SCALE = HEAD_DIM ** -0.5 * LOG2_E

LANES = 128
SUBLANES = 8
BF16_ROWS = 16
VMEM_CAP = 64 * 1024 * 1024
SC_CORES = 2
SC_SUBCORES = 16

MOD_ROWS = 8
TOKEN_TILE = 256
PROJ_TILE = 512
SUB_TILE = 256
HQ_TILES = D_MODEL // 2 // LANES + 1
GATHER_CHUNK = 128
FFN_RANGES = 2
POOL_HALO = 8
COMBINE_CHUNK = BF16_ROWS
COMBINE_WINDOW = 64


def _vmem_limit(nbytes):
    return int(min(VMEM_CAP - (4 << 20), max(nbytes, 16 << 20)))


def _mod_kernel(c_ref, w_ref, b_ref, o_ref):
    c = c_ref[...]
    s = c * jax.nn.sigmoid(c)
    o_ref[...] = jnp.dot(s.astype(BF16), w_ref[...].astype(BF16),
                         preferred_element_type=F32) + b_ref[...]


def _modulation(cond, w_ada, b_ada):
    n = w_ada.shape[1]
    tn = 1024
    return pl.pallas_call(
        _mod_kernel,
        grid=(n // tn,),
        in_specs=[pl.BlockSpec((MOD_ROWS, D_MODEL), lambda j: (0, 0)),
                  pl.BlockSpec((D_MODEL, tn), lambda j: (0, j)),
                  pl.BlockSpec((1, tn), lambda j: (0, j))],
        out_specs=pl.BlockSpec((MOD_ROWS, tn), lambda j: (0, j)),
        out_shape=jax.ShapeDtypeStruct((MOD_ROWS, n), F32),
        compiler_params=pltpu.CompilerParams(
            vmem_limit_bytes=_vmem_limit(3 * D_MODEL * tn * 4)),
        name="modulation",
    )(cond, w_ada, b_ada.reshape(1, n))


def _norm_mod(x, g, shift, scale):
    ms = jnp.mean(x * x, axis=-1, keepdims=True)
    y = x * lax.rsqrt(ms + EPS)
    return (y * g) * (1.0 + scale) + shift


def _inproj_kernel(*refs, rope):
    if rope:
        x_ref, mod_ref, g_ref, w_ref, cos_ref, sa_ref, sb_ref, q_ref, k_ref, v_ref, p_ref = refs
    else:
        x_ref, mod_ref, g_ref, w_ref, q_ref, k_ref, v_ref, p_ref, ks_ref, vs_ref = refs
    mod = mod_ref[0]
    for s in range(x_ref.shape[0] // SUB_TILE):
        rows = slice(s * SUB_TILE, (s + 1) * SUB_TILE)
        h = _norm_mod(x_ref[rows, :], g_ref[...], mod[0:1], mod[1:2])
        u = jnp.dot(h.astype(BF16), w_ref[...], preferred_element_type=F32)

        def rot(xh, rows=rows):
            return (xh * cos_ref[rows, :] + pltpu.roll(xh, LANES - ROPE_FREQS, 1) * sa_ref[rows, :]
                    + pltpu.roll(xh, ROPE_FREQS, 1) * sb_ref[rows, :])

        for hd in range(N_HEADS):
            xh = u[:, hd * HEAD_DIM:(hd + 1) * HEAD_DIM] * ATTN_SCALE
            q_ref[rows, hd * HEAD_DIM:(hd + 1) * HEAD_DIM] = (rot(xh) if rope else xh).astype(BF16)
        for hd in range(N_KV_HEADS):
            lo = ATTN_WIDTH + hd * HEAD_DIM
            xh = u[:, lo:lo + HEAD_DIM]
            k_ref[rows, hd * HEAD_DIM:(hd + 1) * HEAD_DIM] = rot(xh) if rope else xh
        v_ref[rows, :] = u[:, ATTN_WIDTH + KV_WIDTH:ATTN_WIDTH + 2 * KV_WIDTH]
        p_ref[rows, :] = u[:, ATTN_WIDTH + 2 * KV_WIDTH:]
        if not rope:
            for hd in range(N_KV_HEADS):
                state_rows = pl.ds(s * SUB_TILE * N_KV_HEADS + hd, SUB_TILE, stride=N_KV_HEADS)
                lo = ATTN_WIDTH + hd * HEAD_DIM
                ks_ref[state_rows, :] = u[:, lo:lo + HEAD_DIM]
                vs_ref[state_rows, :] = u[:, lo + KV_WIDTH:lo + KV_WIDTH + HEAD_DIM]


def _in_projection(x2, mod3, mod_row, g, w_in_bf, rope_tabs):
    t = x2.shape[0]
    tm = PROJ_TILE
    rope = rope_tabs is not None
    row = lambda i: (i, 0)
    in_specs = [pl.BlockSpec((tm, D_MODEL), row),
                pl.BlockSpec((1, N_MOD, D_MODEL), lambda i: (mod_row(i), 0, 0)),
                pl.BlockSpec((1, D_MODEL), lambda i: (0, 0)),
                pl.BlockSpec((D_MODEL, IN_WIDTH), lambda i: (0, 0))]
    args = [x2, mod3, g, w_in_bf]
    if rope:
        n_seq = rope_tabs[0].shape[0]
        seq_blocks = n_seq // tm
        for tab in rope_tabs:
            in_specs.append(pl.BlockSpec((tm, HEAD_DIM), lambda i: (i % seq_blocks, 0)))
            args.append(tab)
    out_specs = [pl.BlockSpec((tm, ATTN_WIDTH), row),
                 pl.BlockSpec((tm, KV_WIDTH), row),
                 pl.BlockSpec((tm, KV_WIDTH), row),
                 pl.BlockSpec((tm, POOL_WIDTH), row)]
    out_shape = [jax.ShapeDtypeStruct((t, ATTN_WIDTH), BF16),
                 jax.ShapeDtypeStruct((t, KV_WIDTH), F32),
                 jax.ShapeDtypeStruct((t, KV_WIDTH), F32),
                 jax.ShapeDtypeStruct((t, POOL_WIDTH), F32)]
    if not rope:
        for _ in range(2):
            out_specs.append(pl.BlockSpec((tm * N_KV_HEADS, HEAD_DIM), row))
            out_shape.append(jax.ShapeDtypeStruct((t * N_KV_HEADS, HEAD_DIM), F32))
    return pl.pallas_call(
        functools.partial(_inproj_kernel, rope=rope),
        grid=(t // tm,),
        in_specs=in_specs,
        out_specs=out_specs,
        out_shape=out_shape,
        compiler_params=pltpu.CompilerParams(
            vmem_limit_bytes=_vmem_limit(2 * D_MODEL * IN_WIDTH * 2 + 24 * tm * D_MODEL * 4)),
        name="in_projection",
    )(*args)


def _softmax_pv(s_list, v_list, sink_col):
    m = sink_col
    for s in s_list:
        m = jnp.maximum(m, jnp.max(s, axis=-1, keepdims=True))
    denom = jnp.exp2(sink_col - m)
    out = None
    for s, v in zip(s_list, v_list):
        e = jnp.exp2(s - m)
        if v.shape[1] == HEAD_DIM:
            denom = denom + jnp.sum(e, axis=-1, keepdims=True)
        o = jnp.dot(e.astype(BF16), v, preferred_element_type=F32)
        out = o if out is None else out + o
    if out.shape[1] > HEAD_DIM:
        denom = denom + out[:, HEAD_DIM:HEAD_DIM + 1]
    return out[:, 0:HEAD_DIM] * (1.0 / denom)


def _with_ones(v):
    return jnp.concatenate([v.astype(BF16), jnp.ones(v.shape, BF16)], axis=1)


def _stack_heads(q, kv):
    return jnp.concatenate(
        [q[:, (kv * Q_PER_KV + g) * HEAD_DIM:(kv * Q_PER_KV + g + 1) * HEAD_DIM]
         for g in range(Q_PER_KV)], axis=0)


def _sink_column(sink_ref, kv, rows):
    r = lax.broadcasted_iota(I32, (Q_PER_KV * rows, 1), 0)
    col = jnp.zeros((Q_PER_KV * rows, 1), F32)
    for g in range(Q_PER_KV):
        col = jnp.where((r >= g * rows) & (r < (g + 1) * rows), sink_ref[kv * Q_PER_KV + g], col)
    return col * LOG2_E


def _qk(q, k):
    return lax.dot_general(q, k, (((1,), (1,)), ((), ())), preferred_element_type=F32)


def _ctx_attn_kernel(sink_ref, q_ref, k_ref, v_ref, o_ref):
    rows = q_ref.shape[0]
    q = q_ref[...]
    for kv in range(N_KV_HEADS):
        kh = k_ref[:, kv * HEAD_DIM:(kv + 1) * HEAD_DIM].astype(BF16)
        vh = v_ref[:, kv * HEAD_DIM:(kv + 1) * HEAD_DIM].astype(BF16)
        qs = _stack_heads(q, kv)
        s = _qk(qs, kh)
        o = _softmax_pv([s], [vh], _sink_column(sink_ref, kv, rows))
        for g in range(Q_PER_KV):
            hd = kv * Q_PER_KV + g
            o_ref[:, hd * HEAD_DIM:(hd + 1) * HEAD_DIM] = o[g * rows:(g + 1) * rows].astype(BF16)


def _context_attention(q, k, v, sink, seq):
    t = q.shape[0]
    row = lambda b: (b, 0)
    return pl.pallas_call(
        _ctx_attn_kernel,
        grid=(t // seq,),
        in_specs=[pl.BlockSpec(memory_space=pltpu.SMEM),
                  pl.BlockSpec((seq, ATTN_WIDTH), row),
                  pl.BlockSpec((seq, KV_WIDTH), row),
                  pl.BlockSpec((seq, KV_WIDTH), row)],
        out_specs=pl.BlockSpec((seq, ATTN_WIDTH), row),
        out_shape=jax.ShapeDtypeStruct((t, ATTN_WIDTH), BF16),
        name="context_attention",
    )(sink, q, k, v)


def _lat_attn_kernel(sink_ref, q_ref, k_ref, v_ref, ck_ref, cv_ref, o_ref, *, n_seq):
    i = pl.program_id(1)
    band = 3 * BLOCK
    start = pl.multiple_of(jnp.clip((i - 1) * BLOCK, 0, n_seq - band), BLOCK)
    rows = Q_PER_KV * BLOCK
    qpos = i * BLOCK + lax.broadcasted_iota(I32, (rows, band), 0) % BLOCK
    kpos = start + lax.broadcasted_iota(I32, (rows, band), 1)
    mask = jnp.abs(kpos - qpos) <= WINDOW
    q = q_ref[...]
    for kv in range(N_KV_HEADS):
        cols = slice(kv * HEAD_DIM, (kv + 1) * HEAD_DIM)
        kb = k_ref[pl.ds(start, band), cols].astype(BF16)
        vb = _with_ones(v_ref[pl.ds(start, band), cols])
        ck = ck_ref[0, :, cols].astype(BF16)
        cv = _with_ones(cv_ref[0, :, cols])
        qs = _stack_heads(q, kv)
        s_loc = jnp.where(mask, _qk(qs, kb), NEG)
        s_ctx = _qk(qs, ck)
        o = _softmax_pv([s_loc, s_ctx], [vb, cv], _sink_column(sink_ref, kv, BLOCK))
        for g in range(Q_PER_KV):
            hd = kv * Q_PER_KV + g
            o_ref[:, hd * HEAD_DIM:(hd + 1) * HEAD_DIM] = o[g * BLOCK:(g + 1) * BLOCK].astype(BF16)


def _latent_attention(q, k, v, ck, cv, sink, n_seq):
    t = q.shape[0]
    nb = n_seq // BLOCK
    past = ck.shape[1]
    return pl.pallas_call(
        functools.partial(_lat_attn_kernel, n_seq=n_seq),
        grid=(t // n_seq, nb),
        in_specs=[pl.BlockSpec(memory_space=pltpu.SMEM),
                  pl.BlockSpec((BLOCK, ATTN_WIDTH), lambda b, i: (b * nb + i, 0)),
                  pl.BlockSpec((n_seq, KV_WIDTH), lambda b, i: (b, 0)),
                  pl.BlockSpec((n_seq, KV_WIDTH), lambda b, i: (b, 0)),
                  pl.BlockSpec((1, past, KV_WIDTH), lambda b, i: (b, 0, 0)),
                  pl.BlockSpec((1, past, KV_WIDTH), lambda b, i: (b, 0, 0))],
        out_specs=pl.BlockSpec((BLOCK, ATTN_WIDTH), lambda b, i: (b * nb + i, 0)),
        out_shape=jax.ShapeDtypeStruct((t, ATTN_WIDTH), BF16),
        name="latent_attention",
    )(sink, q, k, v, ck, cv)


def _pool_group(p_ref, r0, seq, w_ref, s_ref, g):
    n = SUB_TILE
    rows = n + 2 * POOL_HALO
    static = isinstance(r0, int)
    t0 = r0 % seq
    t = t0 + lax.broadcasted_iota(I32, (n, 1), 0)
    has_top = t0 > 0
    has_bottom = t0 + n < seq
    zeros = jnp.zeros((POOL_HALO, POOL_GROUP), F32)
    w = POOL_SIZES[g]
    cols = slice(g * POOL_GROUP, (g + 1) * POOL_GROUP)
    pg = p_ref[pl.ds(r0, n), cols]
    if static:
        top = p_ref[r0 - POOL_HALO:r0, cols] if has_top else zeros
        bottom = p_ref[r0 + n:r0 + n + POOL_HALO, cols] if has_bottom else zeros
    else:
        top_row = pl.multiple_of(jnp.maximum(r0 - POOL_HALO, 0), POOL_HALO)
        bottom_row = pl.multiple_of(jnp.minimum(r0 + n, p_ref.shape[0] - POOL_HALO), POOL_HALO)
        top = jnp.where(has_top, p_ref[pl.ds(top_row, POOL_HALO), cols], 0.0)
        bottom = jnp.where(has_bottom, p_ref[pl.ds(bottom_row, POOL_HALO), cols], 0.0)
    x = jnp.concatenate([top, pg, bottom], axis=0)
    fwd = x
    span = 1
    while span < w // 2:
        fwd = fwd + pltpu.roll(fwd, rows - span, 0)
        span *= 2
    if (w // 2) % SUBLANES == 0:
        wsum = fwd[POOL_HALO - w // 2:POOL_HALO - w // 2 + n] + fwd[POOL_HALO:POOL_HALO + n]
    else:
        wsum = (fwd + pltpu.roll(fwd, w // 2, 0))[POOL_HALO:POOL_HALO + n]
    lo = jnp.maximum(t - w // 2, 0)
    hi = jnp.minimum(t + w - w // 2, seq)
    inv_cnt = 1.0 / (hi - lo).astype(F32)
    mixed = wsum * inv_cnt - pg
    y = jnp.dot(mixed.astype(BF16), w_ref[g], preferred_element_type=F32)
    return (y * s_ref[:, cols]).astype(BF16)


def _pack_pair(lo, hi):
    return lax.bitcast_convert_type(pltpu.pack_elementwise([lo, hi], packed_dtype=BF16), I32)


def _unpack_pair(words):
    lo = pltpu.unpack_elementwise(words, index=0, packed_dtype=BF16, unpacked_dtype=F32)
    hi = pltpu.unpack_elementwise(words, index=1, packed_dtype=BF16, unpacked_dtype=F32)
    return lo, hi


def _outproj_kernel(a_ref, p_ref, x_ref, mod_ref, g_ref, wo_ref, wr_ref, pw_ref, ps_ref,
                    x1_ref, h_ref, aff_ref, *, seq):
    mod = mod_ref[0]
    groups = SUB_TILE // SUBLANES
    half = D_MODEL // 2
    tm = x_ref.shape[0]
    steps_per_p_block = p_ref.shape[0] // tm
    for s in range(tm // SUB_TILE):
        rows = slice(s * SUB_TILE, (s + 1) * SUB_TILE)
        grp = slice(s * groups, (s + 1) * groups)
        r0 = s * SUB_TILE
        if steps_per_p_block > 1:
            r0 = pl.multiple_of((pl.program_id(0) % steps_per_p_block) * tm + r0, SUB_TILE)
        mix = jnp.dot(a_ref[rows, :], wo_ref[0:ATTN_WIDTH, :], preferred_element_type=F32)
        for g in range(len(POOL_SIZES)):
            lo = ATTN_WIDTH + g * POOL_GROUP
            mix = mix + jnp.dot(_pool_group(p_ref, r0, seq, pw_ref, ps_ref, g),
                                wo_ref[lo:lo + POOL_GROUP, :], preferred_element_type=F32)
        x1 = x_ref[rows, :] + mod[2:3] * mix
        x1_ref[rows, :] = x1
        h = _norm_mod(x1, g_ref[...], mod[3:4], mod[4:5])
        logits = jnp.dot(h.astype(BF16), wr_ref[...], preferred_element_type=F32)
        lane = lax.broadcasted_iota(I32, logits.shape, 1)
        logits = jnp.where(lane < N_EXPERTS, logits, -jnp.inf)
        m = jnp.max(logits, axis=-1, keepdims=True)
        e = jnp.exp(logits - m)
        aff = e / jnp.sum(e, axis=-1, keepdims=True)
        aff_ref[rows, :] = aff[:, 0:N_EXPERTS]
        for c in range(HQ_TILES - 1):
            words = _pack_pair(h[:, c * LANES:(c + 1) * LANES],
                               h[:, half + c * LANES:half + (c + 1) * LANES])
            h_ref[grp, c * SUBLANES:(c + 1) * SUBLANES, :] = (
                words.reshape(groups, SUBLANES, LANES))
        h_ref[grp, (HQ_TILES - 1) * SUBLANES:, :] = (
            lax.bitcast_convert_type(aff, I32).reshape(groups, SUBLANES, LANES))


def _out_projection(attn, p, x2, mod3, mod_row, g, w_out_bf, w_router, pool_w, pool_scale, seq):
    t = x2.shape[0]
    tm = PROJ_TILE
    row = lambda i: (i, 0)
    p_rows = max(tm, seq)
    steps_per_p_block = p_rows // tm
    return pl.pallas_call(
        functools.partial(_outproj_kernel, seq=seq),
        grid=(t // tm,),
        in_specs=[pl.BlockSpec((tm, ATTN_WIDTH), row),
                  pl.BlockSpec((p_rows, POOL_WIDTH), lambda i: (i // steps_per_p_block, 0)),
                  pl.BlockSpec((tm, D_MODEL), row),
                  pl.BlockSpec((1, N_MOD, D_MODEL), lambda i: (mod_row(i), 0, 0)),
                  pl.BlockSpec((1, D_MODEL), lambda i: (0, 0)),
                  pl.BlockSpec((D_MODEL, D_MODEL), lambda i: (0, 0)),
                  pl.BlockSpec((D_MODEL, LANES), lambda i: (0, 0)),
                  pl.BlockSpec((len(POOL_SIZES), POOL_GROUP, POOL_GROUP), lambda i: (0, 0, 0)),
                  pl.BlockSpec((1, POOL_WIDTH), lambda i: (0, 0))],
        out_specs=[pl.BlockSpec((tm, D_MODEL), row),
                   pl.BlockSpec((tm // SUBLANES, HQ_TILES * SUBLANES, LANES), lambda i: (i, 0, 0)),
                   pl.BlockSpec((tm, N_EXPERTS), row)],
        out_shape=[jax.ShapeDtypeStruct((t, D_MODEL), F32),
                   jax.ShapeDtypeStruct((t // SUBLANES, HQ_TILES * SUBLANES, LANES), I32),
                   jax.ShapeDtypeStruct((t, N_EXPERTS), F32)],
        compiler_params=pltpu.CompilerParams(
            vmem_limit_bytes=_vmem_limit(2 * D_MODEL * D_MODEL * 2 + 24 * tm * D_MODEL * 4)),
        name="out_projection",
    )(attn, p, x2, mod3, g, w_out_bf, w_router, pool_w, pool_scale)


def _route_kernel(a_ref, idx_ref, slot_ref, off_ref, *, cap):
    a = a_ref[...]
    n_e, n_c, _ = a.shape
    rows = n_e * n_c

    def bisect(it, thr):
        cand = thr | jnp.left_shift(jnp.int32(1), 30 - it)
        cand_f = lax.bitcast_convert_type(cand, F32)
        cnt = jnp.sum(jnp.sum((a >= cand_f).astype(F32), axis=2, keepdims=True),
                      axis=1, keepdims=True)
        return jnp.where(cnt >= cap, cand, thr)

    thr = lax.fori_loop(0, 31, bisect, jnp.zeros((n_e, 1, 1), I32))
    thr_f = lax.bitcast_convert_type(thr, F32)
    gt = (a > thr_f).astype(F32).reshape(rows, LANES)
    eq = (a == thr_f).astype(F32).reshape(rows, LANES)

    li = lax.broadcasted_iota(I32, (LANES, LANES), 0)
    lj = lax.broadcasted_iota(I32, (LANES, LANES), 1)
    upper_incl = (li <= lj).astype(BF16)
    ri = lax.broadcasted_iota(I32, (rows, rows), 0)
    rj = lax.broadcasted_iota(I32, (rows, rows), 1)
    same_expert = (ri // n_c) == (rj // n_c)
    before = (same_expert & (rj < ri)).astype(BF16)
    whole = same_expert.astype(BF16)

    def lane_bcast(col):
        return jnp.broadcast_to(col, (rows, LANES)).astype(BF16)

    def prefix(x):
        incl = jnp.dot(x.astype(BF16), upper_incl, preferred_element_type=F32)
        tot = incl[:, LANES - 1:LANES]
        off = jnp.dot(before, lane_bcast(tot), preferred_element_type=F32)
        return incl, tot, off

    n_gt = jnp.dot(whole, lane_bcast(jnp.sum(gt, axis=1, keepdims=True)),
                   preferred_element_type=F32)
    need = cap - n_gt
    incl_eq, _, off_eq = prefix(eq)
    rank_eq = off_eq + incl_eq - eq
    sel = jnp.where((eq > 0) & (rank_eq < need), 1.0, gt)
    incl, tot, off = prefix(sel)
    slot = off + incl - sel
    slot_ref[...] = jnp.where(sel > 0, slot, -1.0).astype(I32).reshape(n_e, n_c, LANES)
    off_ref[...] = off.astype(I32).reshape(n_e, n_c, LANES)

    s_lane = lax.broadcasted_iota(I32, (1, cap), 1).astype(F32)
    c_col = lax.broadcasted_iota(I32, (n_c, 1), 0).astype(F32)
    for e in range(n_e):
        r0 = e * n_c
        incl_e = incl[r0:r0 + n_c]
        off_e = off[r0:r0 + n_c, 0:1]
        tot_e = tot[r0:r0 + n_c]
        onehot = ((off_e <= s_lane) & (s_lane < off_e + tot_e)).astype(F32)
        counts = lax.dot_general(incl_e.astype(BF16), onehot.astype(BF16),
                                 (((0,), (0,)), ((), ())), preferred_element_type=F32)
        local = s_lane - jnp.sum(onehot * off_e, axis=0, keepdims=True)
        lane = jnp.sum((counts <= local).astype(F32), axis=0, keepdims=True)
        chunk = jnp.sum(onehot * c_col, axis=0, keepdims=True)
        idx_ref[e] = (chunk * LANES + lane).astype(I32)


def _routing(aff, cap):
    t = aff.shape[0]
    n_c = t // LANES
    a3 = aff.T.reshape(N_EXPERTS, n_c, LANES)
    return pl.pallas_call(
        functools.partial(_route_kernel, cap=cap),
        out_shape=[jax.ShapeDtypeStruct((N_EXPERTS, 1, cap), I32),
                   jax.ShapeDtypeStruct((N_EXPERTS, n_c, LANES), I32),
                   jax.ShapeDtypeStruct((N_EXPERTS, n_c, LANES), I32)],
        compiler_params=pltpu.CompilerParams(vmem_limit_bytes=_vmem_limit(48 << 20)),
        name="routing",
    )(a3)


def _gather_rows(table, row_ids):
    n_chunks = row_ids.shape[0]
    n_workers = SC_CORES * SC_SUBCORES
    per_worker = n_chunks // n_workers
    assert row_ids.shape[1] == GATHER_CHUNK and n_chunks % n_workers == 0
    mesh = plsc.VectorSubcoreMesh(core_axis_name="core", subcore_axis_name="subcore")

    @functools.partial(
        pl.kernel, mesh=mesh,
        out_type=jax.ShapeDtypeStruct((n_chunks * GATHER_CHUNK, LANES), I32),
        scratch_types=[pltpu.VMEM((per_worker, GATHER_CHUNK), I32),
                       pltpu.VMEM((2, GATHER_CHUNK, LANES), I32),
                       pltpu.SemaphoreType.DMA((2,)),
                       pltpu.SemaphoreType.DMA((2,))],
        name="gather_rows",
    )
    def gather(table_hbm, ids_hbm, out_hbm, ids_v, rows_v, gather_sem, store_sem):
        worker = lax.axis_index("subcore") * SC_CORES + lax.axis_index("core")
        first = worker * per_worker
        pltpu.sync_copy(ids_hbm.at[worker], ids_v)

        def fetch(j):
            return pltpu.make_async_copy(table_hbm.at[ids_v.at[j]], rows_v.at[j % 2],
                                         gather_sem.at[j % 2])

        def store(j):
            rows = pl.ds(pl.multiple_of((first + j) * GATHER_CHUNK, GATHER_CHUNK), GATHER_CHUNK)
            return pltpu.make_async_copy(rows_v.at[j % 2], out_hbm.at[rows], store_sem.at[j % 2])

        for j in range(per_worker):
            if j >= 2:
                store(j - 2).wait()
            fetch(j).start()
            if j >= 1:
                fetch(j - 1).wait()
                store(j - 1).start()
        fetch(per_worker - 1).wait()
        store(per_worker - 1).start()
        for j in range(max(per_worker - 2, 0), per_worker):
            store(j).wait()

    return gather(table, row_ids.reshape(n_workers, per_worker, GATHER_CHUNK))


def _packed_row_ids(idx, cap):
    tok = idx.reshape(-1, 1, cap // GATHER_CHUNK, GATHER_CHUNK)
    tile = jnp.arange(HQ_TILES, dtype=I32).reshape(1, HQ_TILES, 1, 1)
    ids = ((tok // SUBLANES) * HQ_TILES + tile) * SUBLANES + tok % SUBLANES
    return ids.reshape(-1, GATHER_CHUNK)


def _ffn_kernel(*refs, caps, n_f, row_chunk, first_expert, n_prior):
    n_g = len(caps)
    x_refs = refs[:n_g]
    wg_ref, wu_ref, wd_ref = refs[n_g:n_g + 3]
    y_refs = refs[n_g + 3 + n_prior:2 * n_g + 3 + n_prior]
    acc_ref = refs[2 * n_g + 3 + n_prior]
    e = pl.program_id(0)
    f = pl.program_id(1)

    @pl.when((e == 0) & (f == 0))
    def _():
        acc_ref[...] = jnp.zeros_like(acc_ref)

    def step(last):
        first = f == 0
        base = 0
        for x_ref, y_ref, cap in zip(x_refs, y_refs, caps):
            if last:
                aff = lax.bitcast_convert_type(x_ref[0, HQ_TILES - 1], F32)
                lane = lax.broadcasted_iota(I32, aff.shape, 1)
                gates = jnp.sum(jnp.where(lane == e + first_expert, aff, 0.0), axis=1,
                                keepdims=True)
                y_ref[0, cap:, :] = jnp.zeros((COMBINE_WINDOW, D_MODEL // 2), I32)
            for r in range(cap // row_chunk):
                rs = slice(r * row_chunk, (r + 1) * row_chunk)
                acc_rows = slice(base + r * row_chunk, base + (r + 1) * row_chunk)
                pairs = [_unpack_pair(x_ref[0, c, rs, :]) for c in range(HQ_TILES - 1)]
                x = jnp.concatenate([p[0] for p in pairs] + [p[1] for p in pairs], axis=1)
                gate_act = jnp.dot(x, wg_ref[0], preferred_element_type=F32)
                up = jnp.dot(x, wu_ref[0], preferred_element_type=F32)
                hid = (gate_act * jax.nn.sigmoid(gate_act)) * up
                part = jnp.dot(hid, wd_ref[0], preferred_element_type=F32)
                if last:
                    y = (part + acc_ref[acc_rows, :]) * gates[rs]
                    y_ref[0, rs, :] = _pack_pair(y[:, 0:D_MODEL // 2], y[:, D_MODEL // 2:])
                else:
                    acc_ref[acc_rows, :] = part + jnp.where(first, 0.0, acc_ref[acc_rows, :])
            base += cap

    @pl.when(f < n_f - 1)
    def _():
        step(False)

    @pl.when(f == n_f - 1)
    def _():
        step(True)


def _expert_ffn(xs_groups, w_gate, w_up, w_down, first_expert, prior_outputs):
    n_f = 4
    tf = D_EXPERT // n_f
    row_chunk = 512
    n_e = xs_groups[0].shape[0]
    e0 = first_expert
    caps = tuple(xs.shape[2] for xs in xs_groups)
    rows = sum(caps)
    est = (2 * HQ_TILES * rows * LANES * 4 + rows * D_MODEL * (4 + 2 * 2) + 2 * 3 * D_MODEL * tf * 4
           + row_chunk * (3 * tf + 2 * D_MODEL) * 4)
    x_specs = [pl.BlockSpec((1, HQ_TILES, cap, LANES), lambda e, f: (e, 0, 0, 0)) for cap in caps]
    y_shapes = [(N_EXPERTS, cap + COMBINE_WINDOW, D_MODEL // 2) for cap in caps]
    y_specs = [pl.BlockSpec((1,) + s[1:], lambda e, f: (e + e0, 0, 0)) for s in y_shapes]
    prior = list(prior_outputs or [])
    n_in = len(caps) + 3
    return pl.pallas_call(
        functools.partial(_ffn_kernel, caps=caps, n_f=n_f, row_chunk=row_chunk,
                          first_expert=e0, n_prior=len(prior)),
        grid=(n_e, n_f),
        in_specs=(x_specs + [pl.BlockSpec((1, D_MODEL, tf), lambda e, f: (e + e0, 0, f)),
                             pl.BlockSpec((1, D_MODEL, tf), lambda e, f: (e + e0, 0, f)),
                             pl.BlockSpec((1, tf, D_MODEL), lambda e, f: (e + e0, f, 0))]
                  + [pl.BlockSpec(memory_space=pl.ANY) for _ in prior]),
        out_specs=y_specs,
        out_shape=[jax.ShapeDtypeStruct(s, I32) for s in y_shapes],
        input_output_aliases={n_in + j: j for j in range(len(prior))},
        scratch_shapes=[pltpu.VMEM((rows, D_MODEL), F32)],
        compiler_params=pltpu.CompilerParams(
            dimension_semantics=("arbitrary", "arbitrary"),
            vmem_limit_bytes=_vmem_limit(est + (4 << 20))),
        name="expert_ffn",
    )(*xs_groups, w_gate, w_up, w_down, *prior)


def _combine_kernel(start_ref, nch_ref, wide_ref, x1_ref, mod_ref, g_ref, slot_ref, y_hbm, o_ref,
                    ybuf_ref, cols_ref, ffn_ref, sem, *, first_k, kblock):
    i = pl.program_id(0)
    tm = x1_ref.shape[0]
    cur = i % 2
    n_window_rows = N_EXPERTS * COMBINE_WINDOW

    def window_copy(e, src_row, buf):
        return pltpu.make_async_copy(
            y_hbm.at[e, pl.ds(pl.multiple_of(src_row, SUBLANES), COMBINE_WINDOW)],
            ybuf_ref.at[buf, e * COMBINE_WINDOW:(e + 1) * COMBINE_WINDOW], sem.at[buf])

    def chunk_copy(e, src_row, buf, dst_row):
        return pltpu.make_async_copy(
            y_hbm.at[e, pl.ds(pl.multiple_of(src_row, SUBLANES), COMBINE_CHUNK)],
            ybuf_ref.at[buf, pl.ds(pl.multiple_of(dst_row, COMBINE_CHUNK), COMBINE_CHUNK)],
            sem.at[buf])

    def fetch(tile, buf):
        @pl.when(wide_ref[tile] == 0)
        def _():
            for e in range(N_EXPERTS):
                window_copy(e, start_ref[tile, e], buf).start()

        @pl.when(wide_ref[tile] != 0)
        def _():
            pos = jnp.int32(0)
            for e in range(N_EXPERTS):
                start = start_ref[tile, e]
                nch = nch_ref[tile, e]

                def issue(c, carry, e=e, start=start, pos=pos):
                    chunk_copy(e, start + c * COMBINE_CHUNK, buf, pos + c * COMBINE_CHUNK).start()
                    return carry
                lax.fori_loop(0, nch, issue, 0)
                pos = pos + nch * COMBINE_CHUNK

    @pl.when(i == 0)
    def _():
        ybuf_ref[...] = jnp.zeros_like(ybuf_ref)
        fetch(0, 0)

    @pl.when(i + 1 < pl.num_programs(0))
    def _():
        fetch(i + 1, 1 - cur)

    def finish(ffn):
        out = x1_ref[...] + mod_ref[0][5:6] * ffn
        ms = jnp.mean(out * out, axis=-1, keepdims=True)
        o_ref[...] = (out * lax.rsqrt(ms + EPS)) * g_ref[...]

    def apply_selection(sel, k0, width):
        lo, hi = _unpack_pair(ybuf_ref[cur, pl.ds(k0, width), :])
        return jnp.concatenate([jnp.dot(sel, lo, preferred_element_type=F32),
                                jnp.dot(sel, hi, preferred_element_type=F32)], axis=1)

    @pl.when(wide_ref[i] == 0)
    def _():
        for e in range(N_EXPERTS):
            window_copy(e, jnp.int32(0), cur).wait()
        expert_lane = lax.broadcasted_iota(I32, (1, N_EXPERTS), 1)
        starts = jnp.zeros((1, N_EXPERTS), I32)
        for e in range(N_EXPERTS):
            starts = jnp.where(expert_lane == e, start_ref[i, e], starts)
        slots = slot_ref[...]
        rows_in_window = jnp.where(slots >= 0, (slots - starts).astype(F32), -1.0)
        window_of_lane = lax.broadcasted_iota(I32, (N_EXPERTS, n_window_rows), 1) // COMBINE_WINDOW
        spread = (window_of_lane == lax.broadcasted_iota(I32, (N_EXPERTS, n_window_rows), 0))
        target = jnp.dot(rows_in_window.astype(BF16), spread.astype(BF16),
                         preferred_element_type=F32)
        lane_row = (lax.broadcasted_iota(I32, (tm, n_window_rows), 1) % COMBINE_WINDOW).astype(F32)
        sel = jnp.where(target == lane_row, 1.0, 0.0)
        ffn = None
        for k0 in range(0, n_window_rows, kblock):
            part = apply_selection(sel[:, k0:k0 + kblock], k0, kblock)
            ffn = part if ffn is None else ffn + part
        finish(ffn)

    @pl.when(wide_ref[i] != 0)
    def _():
        pos = jnp.int32(0)
        begins = []
        for e in range(N_EXPERTS):
            begins.append(pos)
            pos = pos + nch_ref[i, e] * COMBINE_CHUNK
        begins.append(pos)

        def drain(c, carry):
            chunk_copy(0, jnp.int32(0), cur, jnp.int32(0)).wait()
            return carry
        lax.fori_loop(0, pos // COMBINE_CHUNK, drain, 0)

        slots = slot_ref[...]
        for e in range(N_EXPERTS):
            s_e = slots[:, e:e + 1]
            col = jnp.where(s_e >= 0, s_e + (begins[e] - start_ref[i, e]), -1)
            cols_ref[e] = jnp.broadcast_to(col, (tm, LANES))

        lane = lax.broadcasted_iota(I32, (tm, LANES), 1)

        def selection(k0, width):
            halves = []
            for h in range(width // LANES):
                c0 = k0 + h * LANES
                target = lane + c0
                e_lo = jnp.int32(0)
                e_hi = jnp.int32(0)
                for e in range(N_EXPERTS):
                    e_lo = e_lo + (begins[e + 1] <= c0).astype(I32)
                    e_hi = e_hi + (begins[e] < c0 + LANES).astype(I32)

                def mark(e, hit, target=target):
                    return jnp.where(cols_ref[e] == target, 1.0, hit)
                halves.append(lax.fori_loop(e_lo, e_hi, mark, jnp.zeros((tm, LANES), F32)))
            return jnp.concatenate(halves, axis=1)

        ffn_ref[...] = apply_selection(selection(0, first_k), 0, first_k)

        def kstep(kb, carry):
            k0 = pl.multiple_of(kb * kblock, kblock)
            ffn_ref[...] += apply_selection(selection(k0, kblock), k0, kblock)
            return carry
        lax.fori_loop(first_k // kblock, (pos + kblock - 1) // kblock, kstep, 0)
        finish(ffn_ref[...])


def _combine(x1, mod3, mod_row, g_final, slot_t, tile_start, tile_nch, tile_wide, y):
    t = x1.shape[0]
    tm = TOKEN_TILE
    first_k = 2 * tm + N_EXPERTS * COMBINE_CHUNK
    kblock = 256
    max_rows = N_EXPERTS * (tm + 2 * COMBINE_CHUNK)
    max_rows = -(-max_rows // kblock) * kblock
    row = lambda i, *_: (i, 0)
    grid_spec = pltpu.PrefetchScalarGridSpec(
        num_scalar_prefetch=3,
        grid=(t // tm,),
        in_specs=[pl.BlockSpec((tm, D_MODEL), row),
                  pl.BlockSpec((1, N_MOD, D_MODEL), lambda i, *_: (mod_row(i), 0, 0)),
                  pl.BlockSpec((1, D_MODEL), lambda i, *_: (0, 0)),
                  pl.BlockSpec((tm, N_EXPERTS), row),
                  pl.BlockSpec(memory_space=pl.ANY)],
        out_specs=pl.BlockSpec((tm, D_MODEL), row),
        scratch_shapes=[pltpu.VMEM((2, max_rows, D_MODEL // 2), I32),
                        pltpu.VMEM((N_EXPERTS, tm, LANES), I32),
                        pltpu.VMEM((tm, D_MODEL), F32),
                        pltpu.SemaphoreType.DMA((2,))],
    )
    return pl.pallas_call(
        functools.partial(_combine_kernel, first_k=first_k, kblock=kblock),
        grid_spec=grid_spec,
        out_shape=jax.ShapeDtypeStruct((t, D_MODEL), F32),
        compiler_params=pltpu.CompilerParams(
            dimension_semantics=("arbitrary",),
            vmem_limit_bytes=_vmem_limit(2 * max_rows * D_MODEL * 2 + 16 * tm * D_MODEL * 4)),
        name="combine",
    )(tile_start, tile_nch, tile_wide, x1, mod3, g_final, slot_t, y)


def _rope_tables(n):
    rows = n // GRID_W
    row = jnp.repeat(jnp.arange(rows, dtype=F32), GRID_W)
    col = jnp.tile(jnp.arange(GRID_W, dtype=F32), rows)
    inv = ROPE_THETA ** (-jnp.arange(ROPE_FREQS, dtype=F32) / ROPE_FREQS)
    ang_r = row[:, None] * inv
    ang_c = col[:, None] * inv
    zero = jnp.zeros_like(ang_r)
    cos = jnp.concatenate([jnp.cos(ang_r)] * 2 + [jnp.cos(ang_c)] * 2, axis=1)
    sin_a = jnp.concatenate([-jnp.sin(ang_r), zero, -jnp.sin(ang_c), zero], axis=1)
    sin_b = jnp.concatenate([zero, jnp.sin(ang_r), zero, jnp.sin(ang_c)], axis=1)
    return cos, sin_a, sin_b


def _token_group(x, mod3, mod_row, seq, weights, rope_tabs, ctx_kv):
    (norm_mix, w_in_bf, sink, pool_w, pool_scale, w_out_bf, norm_ffn, w_router,
     w_gate, w_up, w_down, norm_final) = weights
    b = x.shape[0]
    t = b * seq
    x2 = x.reshape(t, D_MODEL)
    q, k, v, p, *state = _in_projection(x2, mod3, mod_row(PROJ_TILE), norm_mix, w_in_bf,
                                        rope_tabs)
    if ctx_kv is None:
        attn = _context_attention(q, k, v, sink, seq)
    else:
        attn = _latent_attention(q, k, v, ctx_kv[0], ctx_kv[1], sink, seq)
    x1, h_packed, aff = _out_projection(attn, p, x2, mod3, mod_row(PROJ_TILE), norm_ffn,
                                        w_out_bf, w_router, pool_w, pool_scale, seq)

    cap = EC_FACTOR * t // N_EXPERTS
    idx, slot3, off3 = _routing(aff, cap)
    table = h_packed.reshape(-1, LANES)
    per_range = N_EXPERTS // FFN_RANGES
    xs = []
    for r in range(FFN_RANGES):
        ids = _packed_row_ids(idx[r * per_range:(r + 1) * per_range], cap)
        xs.append(_gather_rows(table, ids).reshape(per_range, HQ_TILES, cap, LANES))

    chunks_per_tile = TOKEN_TILE // LANES
    tile_off = off3[:, ::chunks_per_tile, 0]
    tile_end = jnp.concatenate([tile_off[:, 1:], jnp.full((N_EXPERTS, 1), cap, I32)], axis=1)
    tile_start = (tile_off // SUBLANES) * SUBLANES
    tile_nch = jnp.where(tile_end > tile_off,
                         (tile_end - tile_start + COMBINE_CHUNK - 1) // COMBINE_CHUNK, 0)
    tile_wide = jnp.any(tile_end - tile_start > COMBINE_WINDOW, axis=0).astype(I32)
    slot_t = slot3.reshape(N_EXPERTS, t).T

    def finish(y):
        out = _combine(x1, mod3, mod_row(TOKEN_TILE), norm_final, slot_t, tile_start.T,
                       tile_nch.T, tile_wide, y)
        return out.reshape(b, seq, D_MODEL)
    return xs, finish, state


def kernel(x_prompt, x_sample, c, cache_k, cache_v, c_ctx, w_ada, b_ada, norm_mix, w_in,
           sink_logits, pool_w, pool_scale, w_out, norm_ffn, w_router, w_gate, w_up, w_down,
           norm_final):
    n_b, seq, _ = x_prompt.shape
    n_db, n_lat, _ = x_sample.shape
    assert 1 + n_db <= MOD_ROWS and seq == TOKEN_TILE and n_lat % PROJ_TILE == 0
    assert (n_b * seq) % PROJ_TILE == 0

    cond = jnp.concatenate(
        [c_ctx[None, :], c, jnp.zeros((MOD_ROWS - 1 - n_db, D_MODEL), F32)], axis=0)
    mod3 = _modulation(cond, w_ada[0], b_ada[0]).reshape(MOD_ROWS, N_MOD, D_MODEL)

    w_router_bf = jnp.pad(w_router[0], ((0, 0), (0, LANES - N_EXPERTS))).astype(BF16)
    weights = (norm_mix[0][None, :], w_in[0].astype(BF16), sink_logits[0], pool_w[0].astype(BF16),
               pool_scale[0][None, :], w_out[0].astype(BF16), norm_ffn[0][None, :], w_router_bf,
               w_gate.reshape(w_gate.shape[1:]), w_up.reshape(w_up.shape[1:]),
               w_down.reshape(w_down.shape[1:]), norm_final[None, :])

    xs_p, finish_p, (k_p, v_p) = _token_group(
        x_prompt, mod3, lambda tile: (lambda i: 0), seq, weights, None, None)

    past = cache_k.shape[2]
    ck = cache_k[:, 0].reshape(n_db, past, KV_WIDTH)
    cv = cache_v[:, 0].reshape(n_db, past, KV_WIDTH)
    xs_l, finish_l, _ = _token_group(
        x_sample, mod3, lambda tile: (lambda i: 1 + i // (n_lat // tile)), n_lat, weights,
        _rope_tables(n_lat), (ck, cv))

    ys = None
    for r in range(FFN_RANGES):
        ys = _expert_ffn([xs_p[r], xs_l[r]], weights[8], weights[9], weights[10],
                         r * (N_EXPERTS // FFN_RANGES), ys)
    y_prompt = finish_p(ys[0])
    y_sample = finish_l(ys[1])

    state_k = k_p.reshape(n_b, 1, seq, N_KV_HEADS, HEAD_DIM)
    state_v = v_p.reshape(n_b, 1, seq, N_KV_HEADS, HEAD_DIM)
    return (y_prompt, y_sample, state_k, state_v)
```

```python
import functools

import jax
import jax.numpy as jnp
import numpy as np
from jax import lax
from jax.experimental import pallas as pl
from jax.experimental.pallas import tpu as pltpu
from jax.experimental.pallas import tpu_sc as plsc

F32 = jnp.float32
BF16 = jnp.bfloat16
I32 = jnp.int32

D_MODEL = 2048
N_HEADS = 8
N_KV_HEADS = 2
HEAD_DIM = 128
Q_PER_KV = N_HEADS // N_KV_HEADS
ATTN_WIDTH = N_HEADS * HEAD_DIM
KV_WIDTH = N_KV_HEADS * HEAD_DIM
POOL_WIDTH = D_MODEL - ATTN_WIDTH
POOL_SIZES = (2, 4, 8, 16)
POOL_GROUP = POOL_WIDTH // len(POOL_SIZES)
IN_WIDTH = ATTN_WIDTH + 2 * KV_WIDTH + POOL_WIDTH
WINDOW = 128
BLOCK = 128
GRID_W = 64
ROPE_THETA = 10000.0
ROPE_FREQS = HEAD_DIM // 4
N_EXPERTS = 16
EC_FACTOR = 2
D_EXPERT = 1024
N_MOD = 6
EPS = 1e-6
NEG = -1e30
LOG2_E = 1.4426950408889634
ATTN_SCALE = HEAD_DIM ** -0.5 * LOG2_E

LANES = 128
SUBLANES = 8
BF16_ROWS = 16
VMEM_CAP = 64 * 1024 * 1024
SC_CORES = 2
SC_SUBCORES = 16

MOD_ROWS = 8
TOKEN_TILE = 256
PROJ_TILE = 512
SUB_TILE = 256
HQ_TILES = D_MODEL // 2 // LANES + 1
GATHER_CHUNK = 128
FFN_RANGES = 2
POOL_HALO = 8
COMBINE_CHUNK = BF16_ROWS
COMBINE_WINDOW = 64


def _vmem_limit(nbytes):
    return int(min(VMEM_CAP - (4 << 20), max(nbytes, 16 << 20)))


def _mod_kernel(c_ref, w_ref, b_ref, o_ref):
    c = c_ref[...]
    s = c * jax.nn.sigmoid(c)
    o_ref[...] = jnp.dot(s.astype(BF16), w_ref[...].astype(BF16),
                         preferred_element_type=F32) + b_ref[...]


def _modulation(cond, w_ada, b_ada):
    n = w_ada.shape[1]
    tn = 1024
    return pl.pallas_call(
        _mod_kernel,
        grid=(n // tn,),
        in_specs=[pl.BlockSpec((MOD_ROWS, D_MODEL), lambda j: (0, 0)),
                  pl.BlockSpec((D_MODEL, tn), lambda j: (0, j)),
                  pl.BlockSpec((1, tn), lambda j: (0, j))],
        out_specs=pl.BlockSpec((MOD_ROWS, tn), lambda j: (0, j)),
        out_shape=jax.ShapeDtypeStruct((MOD_ROWS, n), F32),
        compiler_params=pltpu.CompilerParams(
            vmem_limit_bytes=_vmem_limit(3 * D_MODEL * tn * 4)),
        name="modulation",
    )(cond, w_ada, b_ada.reshape(1, n))


def _norm_mod(x, g, shift, scale):
    ms = jnp.mean(x * x, axis=-1, keepdims=True)
    y = x * lax.rsqrt(ms + EPS)
    return (y * g) * (1.0 + scale) + shift


def _inproj_kernel(*refs, rope):
    if rope:
        x_ref, mod_ref, g_ref, w_ref, cos_ref, sa_ref, sb_ref, q_ref, k_ref, v_ref, p_ref = refs
    else:
        x_ref, mod_ref, g_ref, w_ref, q_ref, k_ref, v_ref, p_ref, ks_ref, vs_ref = refs
    mod = mod_ref[0]
    for s in range(x_ref.shape[0] // SUB_TILE):
        rows = slice(s * SUB_TILE, (s + 1) * SUB_TILE)
        h = _norm_mod(x_ref[rows, :], g_ref[...], mod[0:1], mod[1:2])
        u = jnp.dot(h.astype(BF16), w_ref[...], preferred_element_type=F32)

        def rot(xh, rows=rows):
            return (xh * cos_ref[rows, :] + pltpu.roll(xh, LANES - ROPE_FREQS, 1) * sa_ref[rows, :]
                    + pltpu.roll(xh, ROPE_FREQS, 1) * sb_ref[rows, :])

        for hd in range(N_HEADS):
            xh = u[:, hd * HEAD_DIM:(hd + 1) * HEAD_DIM] * ATTN_SCALE
            q_ref[rows, hd * HEAD_DIM:(hd + 1) * HEAD_DIM] = (rot(xh) if rope else xh).astype(BF16)
        for hd in range(N_KV_HEADS):
            lo = ATTN_WIDTH + hd * HEAD_DIM
            xh = u[:, lo:lo + HEAD_DIM]
            k_ref[rows, hd * HEAD_DIM:(hd + 1) * HEAD_DIM] = rot(xh) if rope else xh
        v_ref[rows, :] = u[:, ATTN_WIDTH + KV_WIDTH:ATTN_WIDTH + 2 * KV_WIDTH]
        p_ref[rows, :] = u[:, ATTN_WIDTH + 2 * KV_WIDTH:]
        if not rope:
            for hd in range(N_KV_HEADS):
                state_rows = pl.ds(s * SUB_TILE * N_KV_HEADS + hd, SUB_TILE, stride=N_KV_HEADS)
                lo = ATTN_WIDTH + hd * HEAD_DIM
                ks_ref[state_rows, :] = u[:, lo:lo + HEAD_DIM]
                vs_ref[state_rows, :] = u[:, lo + KV_WIDTH:lo + KV_WIDTH + HEAD_DIM]


def _in_projection(x2, mod3, mod_row, g, w_in_bf, rope_tabs):
    t = x2.shape[0]
    tm = PROJ_TILE
    rope = rope_tabs is not None
    row = lambda i: (i, 0)
    in_specs = [pl.BlockSpec((tm, D_MODEL), row),
                pl.BlockSpec((1, N_MOD, D_MODEL), lambda i: (mod_row(i), 0, 0)),
                pl.BlockSpec((1, D_MODEL), lambda i: (0, 0)),
                pl.BlockSpec((D_MODEL, IN_WIDTH), lambda i: (0, 0))]
    args = [x2, mod3, g, w_in_bf]
    if rope:
        n_seq = rope_tabs[0].shape[0]
        seq_blocks = n_seq // tm
        for tab in rope_tabs:
            in_specs.append(pl.BlockSpec((tm, HEAD_DIM), lambda i: (i % seq_blocks, 0)))
            args.append(tab)
    out_specs = [pl.BlockSpec((tm, ATTN_WIDTH), row),
                 pl.BlockSpec((tm, KV_WIDTH), row),
                 pl.BlockSpec((tm, KV_WIDTH), row),
                 pl.BlockSpec((tm, POOL_WIDTH), row)]
    out_shape = [jax.ShapeDtypeStruct((t, ATTN_WIDTH), BF16),
                 jax.ShapeDtypeStruct((t, KV_WIDTH), F32),
                 jax.ShapeDtypeStruct((t, KV_WIDTH), F32),
                 jax.ShapeDtypeStruct((t, POOL_WIDTH), F32)]
    if not rope:
        for _ in range(2):
            out_specs.append(pl.BlockSpec((tm * N_KV_HEADS, HEAD_DIM), row))
            out_shape.append(jax.ShapeDtypeStruct((t * N_KV_HEADS, HEAD_DIM), F32))
    return pl.pallas_call(
        functools.partial(_inproj_kernel, rope=rope),
        grid=(t // tm,),
        in_specs=in_specs,
        out_specs=out_specs,
        out_shape=out_shape,
        compiler_params=pltpu.CompilerParams(
            vmem_limit_bytes=_vmem_limit(2 * D_MODEL * IN_WIDTH * 2 + 24 * tm * D_MODEL * 4)),
        name="in_projection",
    )(*args)


def _softmax_pv(s_list, v_list, sink_col):
    m = sink_col
    for s in s_list:
        m = jnp.maximum(m, jnp.max(s, axis=-1, keepdims=True))
    denom = jnp.exp2(sink_col - m)
    out = None
    for s, v in zip(s_list, v_list):
        e = jnp.exp2(s - m)
        if v.shape[1] == HEAD_DIM:
            denom = denom + jnp.sum(e, axis=-1, keepdims=True)
        o = jnp.dot(e.astype(BF16), v, preferred_element_type=F32)
        out = o if out is None else out + o
    if out.shape[1] > HEAD_DIM:
        denom = denom + out[:, HEAD_DIM:HEAD_DIM + 1]
    return out[:, 0:HEAD_DIM] * (1.0 / denom)


def _with_ones(v):
    return jnp.concatenate([v.astype(BF16), jnp.ones(v.shape, BF16)], axis=1)


def _stack_heads(q, kv):
    return jnp.concatenate(
        [q[:, (kv * Q_PER_KV + g) * HEAD_DIM:(kv * Q_PER_KV + g + 1) * HEAD_DIM]
         for g in range(Q_PER_KV)], axis=0)


def _sink_column(sink_ref, kv, rows):
    r = lax.broadcasted_iota(I32, (Q_PER_KV * rows, 1), 0)
    col = jnp.zeros((Q_PER_KV * rows, 1), F32)
    for g in range(Q_PER_KV):
        col = jnp.where((r >= g * rows) & (r < (g + 1) * rows), sink_ref[kv * Q_PER_KV + g], col)
    return col * LOG2_E


def _qk(q, k):
    return lax.dot_general(q, k, (((1,), (1,)), ((), ())), preferred_element_type=F32)


def _ctx_attn_kernel(sink_ref, q_ref, k_ref, v_ref, o_ref):
    rows = q_ref.shape[0]
    q = q_ref[...]
    for kv in range(N_KV_HEADS):
        kh = k_ref[:, kv * HEAD_DIM:(kv + 1) * HEAD_DIM].astype(BF16)
        vh = v_ref[:, kv * HEAD_DIM:(kv + 1) * HEAD_DIM].astype(BF16)
        qs = _stack_heads(q, kv)
        s = _qk(qs, kh)
        o = _softmax_pv([s], [vh], _sink_column(sink_ref, kv, rows))
        for g in range(Q_PER_KV):
            hd = kv * Q_PER_KV + g
            o_ref[:, hd * HEAD_DIM:(hd + 1) * HEAD_DIM] = o[g * rows:(g + 1) * rows].astype(BF16)


def _context_attention(q, k, v, sink, seq):
    t = q.shape[0]
    row = lambda b: (b, 0)
    return pl.pallas_call(
        _ctx_attn_kernel,
        grid=(t // seq,),
        in_specs=[pl.BlockSpec(memory_space=pltpu.SMEM),
                  pl.BlockSpec((seq, ATTN_WIDTH), row),
                  pl.BlockSpec((seq, KV_WIDTH), row),
                  pl.BlockSpec((seq, KV_WIDTH), row)],
        out_specs=pl.BlockSpec((seq, ATTN_WIDTH), row),
        out_shape=jax.ShapeDtypeStruct((t, ATTN_WIDTH), BF16),
        name="context_attention",
    )(sink, q, k, v)


def _lat_attn_kernel(sink_ref, q_ref, k_ref, v_ref, ck_ref, cv_ref, o_ref, *, n_seq):
    i = pl.program_id(1)
    band = 3 * BLOCK
    start = pl.multiple_of(jnp.clip((i - 1) * BLOCK, 0, n_seq - band), BLOCK)
    rows = Q_PER_KV * BLOCK
    qpos = i * BLOCK + lax.broadcasted_iota(I32, (rows, band), 0) % BLOCK
    kpos = start + lax.broadcasted_iota(I32, (rows, band), 1)
    mask = jnp.abs(kpos - qpos) <= WINDOW
    q = q_ref[...]
    for kv in range(N_KV_HEADS):
        cols = slice(kv * HEAD_DIM, (kv + 1) * HEAD_DIM)
        kb = k_ref[pl.ds(start, band), cols].astype(BF16)
        vb = _with_ones(v_ref[pl.ds(start, band), cols])
        ck = ck_ref[0, :, cols].astype(BF16)
        cv = _with_ones(cv_ref[0, :, cols])
        qs = _stack_heads(q, kv)
        s_loc = jnp.where(mask, _qk(qs, kb), NEG)
        s_ctx = _qk(qs, ck)
        o = _softmax_pv([s_loc, s_ctx], [vb, cv], _sink_column(sink_ref, kv, BLOCK))
        for g in range(Q_PER_KV):
            hd = kv * Q_PER_KV + g
            o_ref[:, hd * HEAD_DIM:(hd + 1) * HEAD_DIM] = o[g * BLOCK:(g + 1) * BLOCK].astype(BF16)


def _latent_attention(q, k, v, ck, cv, sink, n_seq):
    t = q.shape[0]
    nb = n_seq // BLOCK
    past = ck.shape[1]
    return pl.pallas_call(
        functools.partial(_lat_attn_kernel, n_seq=n_seq),
        grid=(t // n_seq, nb),
        in_specs=[pl.BlockSpec(memory_space=pltpu.SMEM),
                  pl.BlockSpec((BLOCK, ATTN_WIDTH), lambda b, i: (b * nb + i, 0)),
                  pl.BlockSpec((n_seq, KV_WIDTH), lambda b, i: (b, 0)),
                  pl.BlockSpec((n_seq, KV_WIDTH), lambda b, i: (b, 0)),
                  pl.BlockSpec((1, past, KV_WIDTH), lambda b, i: (b, 0, 0)),
                  pl.BlockSpec((1, past, KV_WIDTH), lambda b, i: (b, 0, 0))],
        out_specs=pl.BlockSpec((BLOCK, ATTN_WIDTH), lambda b, i: (b * nb + i, 0)),
        out_shape=jax.ShapeDtypeStruct((t, ATTN_WIDTH), BF16),
        name="latent_attention",
    )(sink, q, k, v, ck, cv)


def _pool_group(p_ref, r0, seq, w_ref, s_ref, g):
    n = SUB_TILE
    rows = n + 2 * POOL_HALO
    static = isinstance(r0, int)
    t0 = r0 % seq
    t = t0 + lax.broadcasted_iota(I32, (n, 1), 0)
    has_top = t0 > 0
    has_bottom = t0 + n < seq
    zeros = jnp.zeros((POOL_HALO, POOL_GROUP), F32)
    w = POOL_SIZES[g]
    cols = slice(g * POOL_GROUP, (g + 1) * POOL_GROUP)
    pg = p_ref[pl.ds(r0, n), cols]
    if static:
        top = p_ref[r0 - POOL_HALO:r0, cols] if has_top else zeros
        bottom = p_ref[r0 + n:r0 + n + POOL_HALO, cols] if has_bottom else zeros
    else:
        top_row = pl.multiple_of(jnp.maximum(r0 - POOL_HALO, 0), POOL_HALO)
        bottom_row = pl.multiple_of(jnp.minimum(r0 + n, p_ref.shape[0] - POOL_HALO), POOL_HALO)
        top = jnp.where(has_top, p_ref[pl.ds(top_row, POOL_HALO), cols], 0.0)
        bottom = jnp.where(has_bottom, p_ref[pl.ds(bottom_row, POOL_HALO), cols], 0.0)
    x = jnp.concatenate([top, pg, bottom], axis=0)
    fwd = x
    span = 1
    while span < w // 2:
        fwd = fwd + pltpu.roll(fwd, rows - span, 0)
        span *= 2
    if (w // 2) % SUBLANES == 0:
        wsum = fwd[POOL_HALO - w // 2:POOL_HALO - w // 2 + n] + fwd[POOL_HALO:POOL_HALO + n]
    else:
        wsum = (fwd + pltpu.roll(fwd, w // 2, 0))[POOL_HALO:POOL_HALO + n]
    lo = jnp.maximum(t - w // 2, 0)
    hi = jnp.minimum(t + w - w // 2, seq)
    inv_cnt = 1.0 / (hi - lo).astype(F32)
    mixed = wsum * inv_cnt - pg
    y = jnp.dot(mixed.astype(BF16), w_ref[g], preferred_element_type=F32)
    return (y * s_ref[:, cols]).astype(BF16)


def _pack_pair(lo, hi):
    return lax.bitcast_convert_type(pltpu.pack_elementwise([lo, hi], packed_dtype=BF16), I32)


def _unpack_pair(words):
    lo = pltpu.unpack_elementwise(words, index=0, packed_dtype=BF16, unpacked_dtype=F32)
    hi = pltpu.unpack_elementwise(words, index=1, packed_dtype=BF16, unpacked_dtype=F32)
    return lo, hi


def _outproj_kernel(a_ref, p_ref, x_ref, mod_ref, g_ref, wo_ref, wr_ref, pw_ref, ps_ref,
                    x1_ref, h_ref, aff_ref, *, seq):
    mod = mod_ref[0]
    groups = SUB_TILE // SUBLANES
    half = D_MODEL // 2
    tm = x_ref.shape[0]
    steps_per_p_block = p_ref.shape[0] // tm
    for s in range(tm // SUB_TILE):
        rows = slice(s * SUB_TILE, (s + 1) * SUB_TILE)
        grp = slice(s * groups, (s + 1) * groups)
        r0 = s * SUB_TILE
        if steps_per_p_block > 1:
            r0 = pl.multiple_of((pl.program_id(0) % steps_per_p_block) * tm + r0, SUB_TILE)
        pooled = jnp.concatenate([_pool_group(p_ref, r0, seq, pw_ref, ps_ref, g)
                                  for g in range(len(POOL_SIZES))], axis=1)
        mix = (jnp.dot(a_ref[rows, :], wo_ref[0:ATTN_WIDTH, :], preferred_element_type=F32)
               + jnp.dot(pooled, wo_ref[ATTN_WIDTH:D_MODEL, :], preferred_element_type=F32))
        x1 = x_ref[rows, :] + mod[2:3] * mix
        x1_ref[rows, :] = x1
        h = _norm_mod(x1, g_ref[...], mod[3:4], mod[4:5])
        logits = jnp.dot(h.astype(BF16), wr_ref[...], preferred_element_type=F32)
        lane = lax.broadcasted_iota(I32, logits.shape, 1)
        logits = jnp.where(lane < N_EXPERTS, logits, -jnp.inf)
        m = jnp.max(logits, axis=-1, keepdims=True)
        e = jnp.exp(logits - m)
        aff = e / jnp.sum(e, axis=-1, keepdims=True)
        aff_ref[rows, :] = aff[:, 0:N_EXPERTS]
        for c in range(HQ_TILES - 1):
            words = _pack_pair(h[:, c * LANES:(c + 1) * LANES],
                               h[:, half + c * LANES:half + (c + 1) * LANES])
            h_ref[grp, c * SUBLANES:(c + 1) * SUBLANES, :] = (
                words.reshape(groups, SUBLANES, LANES))
        h_ref[grp, (HQ_TILES - 1) * SUBLANES:, :] = (
            lax.bitcast_convert_type(aff, I32).reshape(groups, SUBLANES, LANES))


def _out_projection(attn, p, x2, mod3, mod_row, g, w_out_bf, w_router, pool_w, pool_scale, seq):
    t = x2.shape[0]
    tm = PROJ_TILE
    row = lambda i: (i, 0)
    p_rows = max(tm, seq)
    steps_per_p_block = p_rows // tm
    return pl.pallas_call(
        functools.partial(_outproj_kernel, seq=seq),
        grid=(t // tm,),
        in_specs=[pl.BlockSpec((tm, ATTN_WIDTH), row),
                  pl.BlockSpec((p_rows, POOL_WIDTH), lambda i: (i // steps_per_p_block, 0)),
                  pl.BlockSpec((tm, D_MODEL), row),
                  pl.BlockSpec((1, N_MOD, D_MODEL), lambda i: (mod_row(i), 0, 0)),
                  pl.BlockSpec((1, D_MODEL), lambda i: (0, 0)),
                  pl.BlockSpec((D_MODEL, D_MODEL), lambda i: (0, 0)),
                  pl.BlockSpec((D_MODEL, LANES), lambda i: (0, 0)),
                  pl.BlockSpec((len(POOL_SIZES), POOL_GROUP, POOL_GROUP), lambda i: (0, 0, 0)),
                  pl.BlockSpec((1, POOL_WIDTH), lambda i: (0, 0))],
        out_specs=[pl.BlockSpec((tm, D_MODEL), row),
                   pl.BlockSpec((tm // SUBLANES, HQ_TILES * SUBLANES, LANES), lambda i: (i, 0, 0)),
                   pl.BlockSpec((tm, N_EXPERTS), row)],
        out_shape=[jax.ShapeDtypeStruct((t, D_MODEL), F32),
                   jax.ShapeDtypeStruct((t // SUBLANES, HQ_TILES * SUBLANES, LANES), I32),
                   jax.ShapeDtypeStruct((t, N_EXPERTS), F32)],
        compiler_params=pltpu.CompilerParams(
            vmem_limit_bytes=_vmem_limit(2 * D_MODEL * D_MODEL * 2 + 24 * tm * D_MODEL * 4)),
        name="out_projection",
    )(attn, p, x2, mod3, g, w_out_bf, w_router, pool_w, pool_scale)


def _route_kernel(a_ref, idx_ref, slot_ref, off_ref, *, cap):
    a = a_ref[...]
    n_e, n_c, _ = a.shape
    rows = n_e * n_c

    def enough(cand):
        cand_f = lax.bitcast_convert_type(cand, F32)
        cnt = jnp.sum(jnp.sum((a >= cand_f).astype(F32), axis=2, keepdims=True),
                      axis=1, keepdims=True)
        return cnt >= cap

    def two_bits(it, thr):
        low = 28 - 2 * it
        for setting in (1, 2, 3):
            cand = thr | jnp.left_shift(jnp.int32(setting), low)
            best = jnp.where(enough(cand), cand, thr if setting == 1 else best)
        return best

    top = jnp.full((n_e, 1, 1), 1 << 30, I32)
    thr = jnp.where(enough(top), top, 0)
    thr = lax.fori_loop(0, 15, two_bits, thr)
    thr_f = lax.bitcast_convert_type(thr, F32)
    gt = (a > thr_f).astype(F32).reshape(rows, LANES)
    eq = (a == thr_f).astype(F32).reshape(rows, LANES)

    li = lax.broadcasted_iota(I32, (LANES, LANES), 0)
    lj = lax.broadcasted_iota(I32, (LANES, LANES), 1)
    upper_incl = (li <= lj).astype(BF16)
    ri = lax.broadcasted_iota(I32, (rows, rows), 0)
    rj = lax.broadcasted_iota(I32, (rows, rows), 1)
    same_expert = (ri // n_c) == (rj // n_c)
    before = (same_expert & (rj < ri)).astype(BF16)
    whole = same_expert.astype(BF16)

    def lane_bcast(col):
        return jnp.broadcast_to(col, (rows, LANES)).astype(BF16)

    def prefix(x):
        incl = jnp.dot(x.astype(BF16), upper_incl, preferred_element_type=F32)
        tot = incl[:, LANES - 1:LANES]
        off = jnp.dot(before, lane_bcast(tot), preferred_element_type=F32)
        return incl, tot, off

    n_gt = jnp.dot(whole, lane_bcast(jnp.sum(gt, axis=1, keepdims=True)),
                   preferred_element_type=F32)
    need = cap - n_gt
    incl_eq, _, off_eq = prefix(eq)
    rank_eq = off_eq + incl_eq - eq
    sel = jnp.where((eq > 0) & (rank_eq < need), 1.0, gt)
    incl, tot, off = prefix(sel)
    slot = off + incl - sel
    slot_ref[...] = jnp.where(sel > 0, slot, -1.0).astype(I32).reshape(n_e, n_c, LANES)
    off_ref[...] = off.astype(I32).reshape(n_e, n_c, LANES)

    s_lane = lax.broadcasted_iota(I32, (1, cap), 1).astype(F32)
    c_col = lax.broadcasted_iota(I32, (n_c, 1), 0).astype(F32)
    for e in range(n_e):
        r0 = e * n_c
        incl_e = incl[r0:r0 + n_c]
        off_e = off[r0:r0 + n_c, 0:1]
        tot_e = tot[r0:r0 + n_c]
        onehot = ((off_e <= s_lane) & (s_lane < off_e + tot_e)).astype(F32)
        counts = lax.dot_general(incl_e.astype(BF16), onehot.astype(BF16),
                                 (((0,), (0,)), ((), ())), preferred_element_type=F32)
        local = s_lane - jnp.sum(onehot * off_e, axis=0, keepdims=True)
        lane = jnp.sum((counts <= local).astype(F32), axis=0, keepdims=True)
        chunk = jnp.sum(onehot * c_col, axis=0, keepdims=True)
        idx_ref[e] = (chunk * LANES + lane).astype(I32)


def _routing(aff, cap):
    t = aff.shape[0]
    n_c = t // LANES
    a3 = aff.T.reshape(N_EXPERTS, n_c, LANES)
    return pl.pallas_call(
        functools.partial(_route_kernel, cap=cap),
        out_shape=[jax.ShapeDtypeStruct((N_EXPERTS, 1, cap), I32),
                   jax.ShapeDtypeStruct((N_EXPERTS, n_c, LANES), I32),
                   jax.ShapeDtypeStruct((N_EXPERTS, n_c, LANES), I32)],
        compiler_params=pltpu.CompilerParams(vmem_limit_bytes=_vmem_limit(48 << 20)),
        name="routing",
    )(a3)


def _gather_rows(table, row_ids):
    n_chunks = row_ids.shape[0]
    n_workers = SC_CORES * SC_SUBCORES
    per_worker = n_chunks // n_workers
    assert row_ids.shape[1] == GATHER_CHUNK and n_chunks % n_workers == 0
    mesh = plsc.VectorSubcoreMesh(core_axis_name="core", subcore_axis_name="subcore")

    @functools.partial(
        pl.kernel, mesh=mesh,
        out_type=jax.ShapeDtypeStruct((n_chunks * GATHER_CHUNK, LANES), I32),
        scratch_types=[pltpu.VMEM((per_worker, GATHER_CHUNK), I32),
                       pltpu.VMEM((2, GATHER_CHUNK, LANES), I32),
                       pltpu.SemaphoreType.DMA((2,)),
                       pltpu.SemaphoreType.DMA((2,))],
        name="gather_rows",
    )
    def gather(table_hbm, ids_hbm, out_hbm, ids_v, rows_v, gather_sem, store_sem):
        worker = lax.axis_index("subcore") * SC_CORES + lax.axis_index("core")
        first = worker * per_worker
        pltpu.sync_copy(ids_hbm.at[worker], ids_v)

        def fetch(j):
            return pltpu.make_async_copy(table_hbm.at[ids_v.at[j]], rows_v.at[j % 2],
                                         gather_sem.at[j % 2])

        def store(j):
            rows = pl.ds(pl.multiple_of((first + j) * GATHER_CHUNK, GATHER_CHUNK), GATHER_CHUNK)
            return pltpu.make_async_copy(rows_v.at[j % 2], out_hbm.at[rows], store_sem.at[j % 2])

        for j in range(per_worker):
            if j >= 2:
                store(j - 2).wait()
            fetch(j).start()
            if j >= 1:
                fetch(j - 1).wait()
                store(j - 1).start()
        fetch(per_worker - 1).wait()
        store(per_worker - 1).start()
        for j in range(max(per_worker - 2, 0), per_worker):
            store(j).wait()

    return gather(table, row_ids.reshape(n_workers, per_worker, GATHER_CHUNK))


def _packed_row_ids(idx, cap):
    tok = idx.reshape(-1, 1, cap // GATHER_CHUNK, GATHER_CHUNK)
    tile = jnp.arange(HQ_TILES, dtype=I32).reshape(1, HQ_TILES, 1, 1)
    ids = ((tok // SUBLANES) * HQ_TILES + tile) * SUBLANES + tok % SUBLANES
    return ids.reshape(-1, GATHER_CHUNK)


def _ffn_kernel(*refs, caps, n_f, row_chunk, first_expert, n_prior):
    n_g = len(caps)
    x_refs = refs[:n_g]
    wg_ref, wu_ref, wd_ref = refs[n_g:n_g + 3]
    y_refs = refs[n_g + 3 + n_prior:2 * n_g + 3 + n_prior]
    acc_ref = refs[2 * n_g + 3 + n_prior]
    e = pl.program_id(0)
    f = pl.program_id(1)

    @pl.when((e == 0) & (f == 0))
    def _():
        acc_ref[...] = jnp.zeros_like(acc_ref)

    def step(last):
        first = f == 0
        base = 0
        for x_ref, y_ref, cap in zip(x_refs, y_refs, caps):
            if last:
                aff = lax.bitcast_convert_type(x_ref[0, HQ_TILES - 1], F32)
                lane = lax.broadcasted_iota(I32, aff.shape, 1)
                gates = jnp.sum(jnp.where(lane == e + first_expert, aff, 0.0), axis=1,
                                keepdims=True)
                y_ref[0, cap:, :] = jnp.zeros((COMBINE_WINDOW, D_MODEL // 2), I32)
            for r in range(cap // row_chunk):
                rs = slice(r * row_chunk, (r + 1) * row_chunk)
                acc_rows = slice(base + r * row_chunk, base + (r + 1) * row_chunk)
                pairs = [_unpack_pair(x_ref[0, c, rs, :]) for c in range(HQ_TILES - 1)]
                x = jnp.concatenate([p[0] for p in pairs] + [p[1] for p in pairs], axis=1)
                gate_act = jnp.dot(x, wg_ref[0], preferred_element_type=F32)
                up = jnp.dot(x, wu_ref[0], preferred_element_type=F32)
                hid = (gate_act * jax.nn.sigmoid(gate_act)) * up
                part = jnp.dot(hid, wd_ref[0], preferred_element_type=F32)
                if last:
                    y = (part + acc_ref[acc_rows, :]) * gates[rs]
                    y_ref[0, rs, :] = _pack_pair(y[:, 0:D_MODEL // 2], y[:, D_MODEL // 2:])
                else:
                    acc_ref[acc_rows, :] = part + jnp.where(first, 0.0, acc_ref[acc_rows, :])
            base += cap

    @pl.when(f < n_f - 1)
    def _():
        step(False)

    @pl.when(f == n_f - 1)
    def _():
        step(True)


def _expert_ffn(xs_groups, w_gate, w_up, w_down, first_expert, prior_outputs):
    n_f = 4
    tf = D_EXPERT // n_f
    row_chunk = 512
    n_e = xs_groups[0].shape[0]
    e0 = first_expert
    caps = tuple(xs.shape[2] for xs in xs_groups)
    rows = sum(caps)
    est = (2 * HQ_TILES * rows * LANES * 4 + rows * D_MODEL * (4 + 2 * 2) + 2 * 3 * D_MODEL * tf * 4
           + row_chunk * (3 * tf + 2 * D_MODEL) * 4)
    x_specs = [pl.BlockSpec((1, HQ_TILES, cap, LANES), lambda e, f: (e, 0, 0, 0)) for cap in caps]
    y_shapes = [(N_EXPERTS, cap + COMBINE_WINDOW, D_MODEL // 2) for cap in caps]
    y_specs = [pl.BlockSpec((1,) + s[1:], lambda e, f: (e + e0, 0, 0)) for s in y_shapes]
    prior = list(prior_outputs or [])
    n_in = len(caps) + 3
    return pl.pallas_call(
        functools.partial(_ffn_kernel, caps=caps, n_f=n_f, row_chunk=row_chunk,
                          first_expert=e0, n_prior=len(prior)),
        grid=(n_e, n_f),
        in_specs=(x_specs + [pl.BlockSpec((1, D_MODEL, tf), lambda e, f: (e + e0, 0, f)),
                             pl.BlockSpec((1, D_MODEL, tf), lambda e, f: (e + e0, 0, f)),
                             pl.BlockSpec((1, tf, D_MODEL), lambda e, f: (e + e0, f, 0))]
                  + [pl.BlockSpec(memory_space=pl.ANY) for _ in prior]),
        out_specs=y_specs,
        out_shape=[jax.ShapeDtypeStruct(s, I32) for s in y_shapes],
        input_output_aliases={n_in + j: j for j in range(len(prior))},
        scratch_shapes=[pltpu.VMEM((rows, D_MODEL), F32)],
        compiler_params=pltpu.CompilerParams(
            dimension_semantics=("arbitrary", "arbitrary"),
            vmem_limit_bytes=_vmem_limit(est + (4 << 20))),
        name="expert_ffn",
    )(*xs_groups, w_gate, w_up, w_down, *prior)


def _combine_kernel(start_ref, nch_ref, wide_ref, x1_ref, mod_ref, g_ref, slot_ref, y_hbm, o_ref,
                    ybuf_ref, cols_ref, ffn_ref, sem, *, first_k, kblock):
    i = pl.program_id(0)
    tm = x1_ref.shape[0]
    cur = i % 2
    n_window_rows = N_EXPERTS * COMBINE_WINDOW

    def window_copy(e, src_row, buf):
        return pltpu.make_async_copy(
            y_hbm.at[e, pl.ds(pl.multiple_of(src_row, SUBLANES), COMBINE_WINDOW)],
            ybuf_ref.at[buf, e * COMBINE_WINDOW:(e + 1) * COMBINE_WINDOW], sem.at[buf])

    def chunk_copy(e, src_row, buf, dst_row):
        return pltpu.make_async_copy(
            y_hbm.at[e, pl.ds(pl.multiple_of(src_row, SUBLANES), COMBINE_CHUNK)],
            ybuf_ref.at[buf, pl.ds(pl.multiple_of(dst_row, COMBINE_CHUNK), COMBINE_CHUNK)],
            sem.at[buf])

    def fetch(tile, buf):
        @pl.when(wide_ref[tile] == 0)
        def _():
            for e in range(N_EXPERTS):
                window_copy(e, start_ref[tile, e], buf).start()

        @pl.when(wide_ref[tile] != 0)
        def _():
            pos = jnp.int32(0)
            for e in range(N_EXPERTS):
                start = start_ref[tile, e]
                nch = nch_ref[tile, e]

                def issue(c, carry, e=e, start=start, pos=pos):
                    chunk_copy(e, start + c * COMBINE_CHUNK, buf, pos + c * COMBINE_CHUNK).start()
                    return carry
                lax.fori_loop(0, nch, issue, 0)
                pos = pos + nch * COMBINE_CHUNK

    @pl.when(i == 0)
    def _():
        ybuf_ref[...] = jnp.zeros_like(ybuf_ref)
        fetch(0, 0)

    @pl.when(i + 1 < pl.num_programs(0))
    def _():
        fetch(i + 1, 1 - cur)

    def finish(ffn):
        out = x1_ref[...] + mod_ref[0][5:6] * ffn
        ms = jnp.mean(out * out, axis=-1, keepdims=True)
        o_ref[...] = (out * lax.rsqrt(ms + EPS)) * g_ref[...]

    def apply_selection(sel, k0, width):
        lo, hi = _unpack_pair(ybuf_ref[cur, pl.ds(k0, width), :])
        return jnp.concatenate([jnp.dot(sel, lo, preferred_element_type=F32),
                                jnp.dot(sel, hi, preferred_element_type=F32)], axis=1)

    @pl.when(wide_ref[i] == 0)
    def _():
        for e in range(N_EXPERTS):
            window_copy(e, jnp.int32(0), cur).wait()
        expert_lane = lax.broadcasted_iota(I32, (1, N_EXPERTS), 1)
        starts = jnp.zeros((1, N_EXPERTS), I32)
        for e in range(N_EXPERTS):
            starts = jnp.where(expert_lane == e, start_ref[i, e], starts)
        slots = slot_ref[...]
        rows_in_window = jnp.where(slots >= 0, (slots - starts).astype(F32), -1.0)
        window_of_lane = lax.broadcasted_iota(I32, (N_EXPERTS, n_window_rows), 1) // COMBINE_WINDOW
        spread = (window_of_lane == lax.broadcasted_iota(I32, (N_EXPERTS, n_window_rows), 0))
        target = jnp.dot(rows_in_window.astype(BF16), spread.astype(BF16),
                         preferred_element_type=F32)
        lane_row = (lax.broadcasted_iota(I32, (tm, n_window_rows), 1) % COMBINE_WINDOW).astype(F32)
        sel = jnp.where(target == lane_row, 1.0, 0.0)
        ffn = None
        for k0 in range(0, n_window_rows, kblock):
            part = apply_selection(sel[:, k0:k0 + kblock], k0, kblock)
            ffn = part if ffn is None else ffn + part
        finish(ffn)

    @pl.when(wide_ref[i] != 0)
    def _():
        pos = jnp.int32(0)
        begins = []
        for e in range(N_EXPERTS):
            begins.append(pos)
            pos = pos + nch_ref[i, e] * COMBINE_CHUNK
        begins.append(pos)

        def drain(c, carry):
            chunk_copy(0, jnp.int32(0), cur, jnp.int32(0)).wait()
            return carry
        lax.fori_loop(0, pos // COMBINE_CHUNK, drain, 0)

        slots = slot_ref[...]
        for e in range(N_EXPERTS):
            s_e = slots[:, e:e + 1]
            col = jnp.where(s_e >= 0, s_e + (begins[e] - start_ref[i, e]), -1)
            cols_ref[e] = jnp.broadcast_to(col, (tm, LANES))

        lane = lax.broadcasted_iota(I32, (tm, LANES), 1)

        def selection(k0, width):
            halves = []
            for h in range(width // LANES):
                c0 = k0 + h * LANES
                target = lane + c0
                e_lo = jnp.int32(0)
                e_hi = jnp.int32(0)
                for e in range(N_EXPERTS):
                    e_lo = e_lo + (begins[e + 1] <= c0).astype(I32)
                    e_hi = e_hi + (begins[e] < c0 + LANES).astype(I32)

                def mark(e, hit, target=target):
                    return jnp.where(cols_ref[e] == target, 1.0, hit)
                halves.append(lax.fori_loop(e_lo, e_hi, mark, jnp.zeros((tm, LANES), F32)))
            return jnp.concatenate(halves, axis=1)

        ffn_ref[...] = apply_selection(selection(0, first_k), 0, first_k)

        def kstep(kb, carry):
            k0 = pl.multiple_of(kb * kblock, kblock)
            ffn_ref[...] += apply_selection(selection(k0, kblock), k0, kblock)
            return carry
        lax.fori_loop(first_k // kblock, (pos + kblock - 1) // kblock, kstep, 0)
        finish(ffn_ref[...])


def _combine(x1, mod3, mod_row, g_final, slot_t, tile_start, tile_nch, tile_wide, y):
    t = x1.shape[0]
    tm = TOKEN_TILE
    first_k = 2 * tm + N_EXPERTS * COMBINE_CHUNK
    kblock = 256
    max_rows = N_EXPERTS * (tm + 2 * COMBINE_CHUNK)
    max_rows = -(-max_rows // kblock) * kblock
    row = lambda i, *_: (i, 0)
    grid_spec = pltpu.PrefetchScalarGridSpec(
        num_scalar_prefetch=3,
        grid=(t // tm,),
        in_specs=[pl.BlockSpec((tm, D_MODEL), row),
                  pl.BlockSpec((1, N_MOD, D_MODEL), lambda i, *_: (mod_row(i), 0, 0)),
                  pl.BlockSpec((1, D_MODEL), lambda i, *_: (0, 0)),
                  pl.BlockSpec((tm, N_EXPERTS), row),
                  pl.BlockSpec(memory_space=pl.ANY)],
        out_specs=pl.BlockSpec((tm, D_MODEL), row),
        scratch_shapes=[pltpu.VMEM((2, max_rows, D_MODEL // 2), I32),
                        pltpu.VMEM((N_EXPERTS, tm, LANES), I32),
                        pltpu.VMEM((tm, D_MODEL), F32),
                        pltpu.SemaphoreType.DMA((2,))],
    )
    return pl.pallas_call(
        functools.partial(_combine_kernel, first_k=first_k, kblock=kblock),
        grid_spec=grid_spec,
        out_shape=jax.ShapeDtypeStruct((t, D_MODEL), F32),
        compiler_params=pltpu.CompilerParams(
            dimension_semantics=("arbitrary",),
            vmem_limit_bytes=_vmem_limit(2 * max_rows * D_MODEL * 2 + 16 * tm * D_MODEL * 4)),
        name="combine",
    )(tile_start, tile_nch, tile_wide, x1, mod3, g_final, slot_t, y)


def _rope_tables(n):
    rows = n // GRID_W
    row = jnp.repeat(jnp.arange(rows, dtype=F32), GRID_W)
    col = jnp.tile(jnp.arange(GRID_W, dtype=F32), rows)
    inv = ROPE_THETA ** (-jnp.arange(ROPE_FREQS, dtype=F32) / ROPE_FREQS)
    ang_r = row[:, None] * inv
    ang_c = col[:, None] * inv
    zero = jnp.zeros_like(ang_r)
    cos = jnp.concatenate([jnp.cos(ang_r)] * 2 + [jnp.cos(ang_c)] * 2, axis=1)
    sin_a = jnp.concatenate([-jnp.sin(ang_r), zero, -jnp.sin(ang_c), zero], axis=1)
    sin_b = jnp.concatenate([zero, jnp.sin(ang_r), zero, jnp.sin(ang_c)], axis=1)
    return cos, sin_a, sin_b


def _token_group(x, mod3, mod_row, seq, weights, rope_tabs, ctx_kv):
    (norm_mix, w_in_bf, sink, pool_w, pool_scale, w_out_bf, norm_ffn, w_router,
     w_gate, w_up, w_down, norm_final) = weights
    b = x.shape[0]
    t = b * seq
    x2 = x.reshape(t, D_MODEL)
    q, k, v, p, *state = _in_projection(x2, mod3, mod_row(PROJ_TILE), norm_mix, w_in_bf,
                                        rope_tabs)
    if ctx_kv is None:
        attn = _context_attention(q, k, v, sink, seq)
    else:
        attn = _latent_attention(q, k, v, ctx_kv[0], ctx_kv[1], sink, seq)
    x1, h_packed, aff = _out_projection(attn, p, x2, mod3, mod_row(PROJ_TILE), norm_ffn,
                                        w_out_bf, w_router, pool_w, pool_scale, seq)

    cap = EC_FACTOR * t // N_EXPERTS
    idx, slot3, off3 = _routing(aff, cap)
    table = h_packed.reshape(-1, LANES)
    per_range = N_EXPERTS // FFN_RANGES
    xs = []
    for r in range(FFN_RANGES):
        ids = _packed_row_ids(idx[r * per_range:(r + 1) * per_range], cap)
        xs.append(_gather_rows(table, ids).reshape(per_range, HQ_TILES, cap, LANES))

    chunks_per_tile = TOKEN_TILE // LANES
    tile_off = off3[:, ::chunks_per_tile, 0]
    tile_end = jnp.concatenate([tile_off[:, 1:], jnp.full((N_EXPERTS, 1), cap, I32)], axis=1)
    tile_start = (tile_off // SUBLANES) * SUBLANES
    tile_nch = jnp.where(tile_end > tile_off,
                         (tile_end - tile_start + COMBINE_CHUNK - 1) // COMBINE_CHUNK, 0)
    tile_wide = jnp.any(tile_end - tile_start > COMBINE_WINDOW, axis=0).astype(I32)
    slot_t = slot3.reshape(N_EXPERTS, t).T

    def finish(y):
        out = _combine(x1, mod3, mod_row(TOKEN_TILE), norm_final, slot_t, tile_start.T,
                       tile_nch.T, tile_wide, y)
        return out.reshape(b, seq, D_MODEL)
    return xs, finish, state


def kernel(x_prompt, x_sample, c, cache_k, cache_v, c_ctx, w_ada, b_ada, norm_mix, w_in,
           sink_logits, pool_w, pool_scale, w_out, norm_ffn, w_router, w_gate, w_up, w_down,
           norm_final):
    n_b, seq, _ = x_prompt.shape
    n_db, n_lat, _ = x_sample.shape
    assert 1 + n_db <= MOD_ROWS and seq == TOKEN_TILE and n_lat % PROJ_TILE == 0
    assert (n_b * seq) % PROJ_TILE == 0

    cond = jnp.concatenate(
        [c_ctx[None, :], c, jnp.zeros((MOD_ROWS - 1 - n_db, D_MODEL), F32)], axis=0)
    mod3 = _modulation(cond, w_ada[0], b_ada[0]).reshape(MOD_ROWS, N_MOD, D_MODEL)

    w_router_bf = jnp.pad(w_router[0], ((0, 0), (0, LANES - N_EXPERTS))).astype(BF16)
    weights = (norm_mix[0][None, :], w_in[0].astype(BF16), sink_logits[0], pool_w[0].astype(BF16),
               pool_scale[0][None, :], w_out[0].astype(BF16), norm_ffn[0][None, :], w_router_bf,
               w_gate.reshape(w_gate.shape[1:]), w_up.reshape(w_up.shape[1:]),
               w_down.reshape(w_down.shape[1:]), norm_final[None, :])

    xs_p, finish_p, (k_p, v_p) = _token_group(
        x_prompt, mod3, lambda tile: (lambda i: 0), seq, weights, None, None)

    past = cache_k.shape[2]
    ck = cache_k[:, 0].reshape(n_db, past, KV_WIDTH)
    cv = cache_v[:, 0].reshape(n_db, past, KV_WIDTH)
    xs_l, finish_l, _ = _token_group(
        x_sample, mod3, lambda tile: (lambda i: 1 + i // (n_lat // tile)), n_lat, weights,
        _rope_tables(n_lat), (ck, cv))

    ys = None
    for r in range(FFN_RANGES):
        ys = _expert_ffn([xs_p[r], xs_l[r]], weights[8], weights[9], weights[10],
                         r * (N_EXPERTS // FFN_RANGES), ys)
    y_prompt = finish_p(ys[0])
    y_sample = finish_l(ys[1])

    state_k = k_p.reshape(n_b, 1, seq, N_KV_HEADS, HEAD_DIM)
    state_v = v_p.reshape(n_b, 1, seq, N_KV_HEADS, HEAD_DIM)
    return (y_prompt, y_sample, state_k, state_v)
```

```python
import functools

import jax
import jax.numpy as jnp
import numpy as np
from jax import lax
from jax.experimental import pallas as pl
from jax.experimental.pallas import tpu as pltpu
from jax.experimental.pallas import tpu_sc as plsc

F32 = jnp.float32
BF16 = jnp.bfloat16
I32 = jnp.int32

D_MODEL = 2048
N_HEADS = 8
N_KV_HEADS = 2
HEAD_DIM = 128
Q_PER_KV = N_HEADS // N_KV_HEADS
ATTN_WIDTH = N_HEADS * HEAD_DIM
KV_WIDTH = N_KV_HEADS * HEAD_DIM
POOL_WIDTH = D_MODEL - ATTN_WIDTH
POOL_SIZES = (2, 4, 8, 16)
POOL_GROUP = POOL_WIDTH // len(POOL_SIZES)
IN_WIDTH = ATTN_WIDTH + 2 * KV_WIDTH + POOL_WIDTH
WINDOW = 128
BLOCK = 128
GRID_W = 64
ROPE_THETA = 10000.0
ROPE_FREQS = HEAD_DIM // 4
N_EXPERTS = 16
EC_FACTOR = 2
D_EXPERT = 1024
N_MOD = 6
EPS = 1e-6
NEG = -1e30
LOG2_E = 1.4426950408889634
ATTN_SCALE = HEAD_DIM ** -0.5 * LOG2_E

LANES = 128
SUBLANES = 8
BF16_ROWS = 16
VMEM_CAP = 64 * 1024 * 1024
SC_CORES = 2
SC_SUBCORES = 16

MOD_ROWS = 8
TOKEN_TILE = 256
PROJ_TILE = 512
SUB_TILE = 256
HQ_TILES = D_MODEL // 2 // LANES + 1
GATHER_CHUNK = 128
FFN_RANGES = 2
POOL_HALO = 8
COMBINE_CHUNK = BF16_ROWS
COMBINE_WINDOW = 64


def _vmem_limit(nbytes):
    return int(min(VMEM_CAP - (4 << 20), max(nbytes, 16 << 20)))


def _mod_kernel(c_ref, w_ref, b_ref, o_ref):
    c = c_ref[...]
    s = c * jax.nn.sigmoid(c)
    o_ref[...] = jnp.dot(s.astype(BF16), w_ref[...].astype(BF16),
                         preferred_element_type=F32) + b_ref[...]


def _modulation(cond, w_ada, b_ada):
    n = w_ada.shape[1]
    tn = 1024
    return pl.pallas_call(
        _mod_kernel,
        grid=(n // tn,),
        in_specs=[pl.BlockSpec((MOD_ROWS, D_MODEL), lambda j: (0, 0)),
                  pl.BlockSpec((D_MODEL, tn), lambda j: (0, j)),
                  pl.BlockSpec((1, tn), lambda j: (0, j))],
        out_specs=pl.BlockSpec((MOD_ROWS, tn), lambda j: (0, j)),
        out_shape=jax.ShapeDtypeStruct((MOD_ROWS, n), F32),
        compiler_params=pltpu.CompilerParams(
            vmem_limit_bytes=_vmem_limit(3 * D_MODEL * tn * 4)),
        name="modulation",
    )(cond, w_ada, b_ada.reshape(1, n))


def _norm_mod(x, g, shift, scale):
    ms = jnp.mean(x * x, axis=-1, keepdims=True)
    y = x * lax.rsqrt(ms + EPS)
    return (y * g) * (1.0 + scale) + shift


def _inproj_kernel(*refs, rope):
    if rope:
        x_ref, mod_ref, g_ref, w_ref, cos_ref, sa_ref, sb_ref, q_ref, k_ref, v_ref, p_ref = refs
    else:
        x_ref, mod_ref, g_ref, w_ref, q_ref, k_ref, v_ref, p_ref, ks_ref, vs_ref = refs
    mod = mod_ref[0]
    for s in range(x_ref.shape[0] // SUB_TILE):
        rows = slice(s * SUB_TILE, (s + 1) * SUB_TILE)
        h = _norm_mod(x_ref[rows, :], g_ref[...], mod[0:1], mod[1:2])
        u = jnp.dot(h.astype(BF16), w_ref[...], preferred_element_type=F32)

        def rot(xh, rows=rows):
            return (xh * cos_ref[rows, :] + pltpu.roll(xh, LANES - ROPE_FREQS, 1) * sa_ref[rows, :]
                    + pltpu.roll(xh, ROPE_FREQS, 1) * sb_ref[rows, :])

        for hd in range(N_HEADS):
            xh = u[:, hd * HEAD_DIM:(hd + 1) * HEAD_DIM] * ATTN_SCALE
            q_ref[rows, hd * HEAD_DIM:(hd + 1) * HEAD_DIM] = (rot(xh) if rope else xh).astype(BF16)
        for hd in range(N_KV_HEADS):
            lo = ATTN_WIDTH + hd * HEAD_DIM
            xh = u[:, lo:lo + HEAD_DIM]
            k_ref[rows, hd * HEAD_DIM:(hd + 1) * HEAD_DIM] = rot(xh) if rope else xh
        v_ref[rows, :] = u[:, ATTN_WIDTH + KV_WIDTH:ATTN_WIDTH + 2 * KV_WIDTH]
        p_ref[rows, :] = u[:, ATTN_WIDTH + 2 * KV_WIDTH:]
        if not rope:
            for hd in range(N_KV_HEADS):
                state_rows = pl.ds(s * SUB_TILE * N_KV_HEADS + hd, SUB_TILE, stride=N_KV_HEADS)
                lo = ATTN_WIDTH + hd * HEAD_DIM
                ks_ref[state_rows, :] = u[:, lo:lo + HEAD_DIM]
                vs_ref[state_rows, :] = u[:, lo + KV_WIDTH:lo + KV_WIDTH + HEAD_DIM]


def _in_projection(x2, mod3, mod_row, g, w_in_bf, rope_tabs):
    t = x2.shape[0]
    tm = PROJ_TILE
    rope = rope_tabs is not None
    row = lambda i: (i, 0)
    in_specs = [pl.BlockSpec((tm, D_MODEL), row),
                pl.BlockSpec((1, N_MOD, D_MODEL), lambda i: (mod_row(i), 0, 0)),
                pl.BlockSpec((1, D_MODEL), lambda i: (0, 0)),
                pl.BlockSpec((D_MODEL, IN_WIDTH), lambda i: (0, 0))]
    args = [x2, mod3, g, w_in_bf]
    if rope:
        n_seq = rope_tabs[0].shape[0]
        seq_blocks = n_seq // tm
        for tab in rope_tabs:
            in_specs.append(pl.BlockSpec((tm, HEAD_DIM), lambda i: (i % seq_blocks, 0)))
            args.append(tab)
    out_specs = [pl.BlockSpec((tm, ATTN_WIDTH), row),
                 pl.BlockSpec((tm, KV_WIDTH), row),
                 pl.BlockSpec((tm, KV_WIDTH), row),
                 pl.BlockSpec((tm, POOL_WIDTH), row)]
    out_shape = [jax.ShapeDtypeStruct((t, ATTN_WIDTH), BF16),
                 jax.ShapeDtypeStruct((t, KV_WIDTH), F32),
                 jax.ShapeDtypeStruct((t, KV_WIDTH), F32),
                 jax.ShapeDtypeStruct((t, POOL_WIDTH), F32)]
    if not rope:
        for _ in range(2):
            out_specs.append(pl.BlockSpec((tm * N_KV_HEADS, HEAD_DIM), row))
            out_shape.append(jax.ShapeDtypeStruct((t * N_KV_HEADS, HEAD_DIM), F32))
    return pl.pallas_call(
        functools.partial(_inproj_kernel, rope=rope),
        grid=(t // tm,),
        in_specs=in_specs,
        out_specs=out_specs,
        out_shape=out_shape,
        compiler_params=pltpu.CompilerParams(
            vmem_limit_bytes=_vmem_limit(2 * D_MODEL * IN_WIDTH * 2 + 24 * tm * D_MODEL * 4)),
        name="in_projection",
    )(*args)


def _softmax_pv(s_list, v_list, sink_col):
    m = sink_col
    for s in s_list:
        m = jnp.maximum(m, jnp.max(s, axis=-1, keepdims=True))
    denom = jnp.exp2(sink_col - m)
    out = None
    for s, v in zip(s_list, v_list):
        e = jnp.exp2(s - m)
        if v.shape[1] == HEAD_DIM:
            denom = denom + jnp.sum(e, axis=-1, keepdims=True)
        o = jnp.dot(e.astype(BF16), v, preferred_element_type=F32)
        out = o if out is None else out + o
    if out.shape[1] > HEAD_DIM:
        denom = denom + out[:, HEAD_DIM:HEAD_DIM + 1]
    return out[:, 0:HEAD_DIM] * (1.0 / denom)


def _with_ones(v):
    return jnp.concatenate([v.astype(BF16), jnp.ones(v.shape, BF16)], axis=1)


def _stack_heads(q, kv):
    return jnp.concatenate(
        [q[:, (kv * Q_PER_KV + g) * HEAD_DIM:(kv * Q_PER_KV + g + 1) * HEAD_DIM]
         for g in range(Q_PER_KV)], axis=0)


def _sink_column(sink_ref, kv, rows):
    r = lax.broadcasted_iota(I32, (Q_PER_KV * rows, 1), 0)
    col = jnp.zeros((Q_PER_KV * rows, 1), F32)
    for g in range(Q_PER_KV):
        col = jnp.where((r >= g * rows) & (r < (g + 1) * rows), sink_ref[kv * Q_PER_KV + g], col)
    return col * LOG2_E


def _qk(q, k):
    return lax.dot_general(q, k, (((1,), (1,)), ((), ())), preferred_element_type=F32)


def _ctx_attn_kernel(sink_ref, q_ref, k_ref, v_ref, o_ref):
    rows = q_ref.shape[0]
    q = q_ref[...]
    for kv in range(N_KV_HEADS):
        kh = k_ref[:, kv * HEAD_DIM:(kv + 1) * HEAD_DIM].astype(BF16)
        vh = v_ref[:, kv * HEAD_DIM:(kv + 1) * HEAD_DIM].astype(BF16)
        qs = _stack_heads(q, kv)
        s = _qk(qs, kh)
        o = _softmax_pv([s], [vh], _sink_column(sink_ref, kv, rows))
        for g in range(Q_PER_KV):
            hd = kv * Q_PER_KV + g
            o_ref[:, hd * HEAD_DIM:(hd + 1) * HEAD_DIM] = o[g * rows:(g + 1) * rows].astype(BF16)


def _context_attention(q, k, v, sink, seq):
    t = q.shape[0]
    row = lambda b: (b, 0)
    return pl.pallas_call(
        _ctx_attn_kernel,
        grid=(t // seq,),
        in_specs=[pl.BlockSpec(memory_space=pltpu.SMEM),
                  pl.BlockSpec((seq, ATTN_WIDTH), row),
                  pl.BlockSpec((seq, KV_WIDTH), row),
                  pl.BlockSpec((seq, KV_WIDTH), row)],
        out_specs=pl.BlockSpec((seq, ATTN_WIDTH), row),
        out_shape=jax.ShapeDtypeStruct((t, ATTN_WIDTH), BF16),
        name="context_attention",
    )(sink, q, k, v)


def _lat_attn_kernel(sink_ref, q_ref, k_ref, v_ref, ck_ref, cv_ref, o_ref, *, n_seq):
    i = pl.program_id(1)
    band = 3 * BLOCK
    start = pl.multiple_of(jnp.clip((i - 1) * BLOCK, 0, n_seq - band), BLOCK)
    rows = Q_PER_KV * BLOCK
    qpos = i * BLOCK + lax.broadcasted_iota(I32, (rows, band), 0) % BLOCK
    kpos = start + lax.broadcasted_iota(I32, (rows, band), 1)
    mask = jnp.abs(kpos - qpos) <= WINDOW
    q = q_ref[...]
    for kv in range(N_KV_HEADS):
        cols = slice(kv * HEAD_DIM, (kv + 1) * HEAD_DIM)
        kb = k_ref[pl.ds(start, band), cols].astype(BF16)
        vb = _with_ones(v_ref[pl.ds(start, band), cols])
        ck = ck_ref[0, :, cols].astype(BF16)
        cv = _with_ones(cv_ref[0, :, cols])
        qs = _stack_heads(q, kv)
        s_loc = jnp.where(mask, _qk(qs, kb), NEG)
        s_ctx = _qk(qs, ck)
        o = _softmax_pv([s_loc, s_ctx], [vb, cv], _sink_column(sink_ref, kv, BLOCK))
        for g in range(Q_PER_KV):
            hd = kv * Q_PER_KV + g
            o_ref[:, hd * HEAD_DIM:(hd + 1) * HEAD_DIM] = o[g * BLOCK:(g + 1) * BLOCK].astype(BF16)


def _latent_attention(q, k, v, ck, cv, sink, n_seq):
    t = q.shape[0]
    nb = n_seq // BLOCK
    past = ck.shape[1]
    return pl.pallas_call(
        functools.partial(_lat_attn_kernel, n_seq=n_seq),
        grid=(t // n_seq, nb),
        in_specs=[pl.BlockSpec(memory_space=pltpu.SMEM),
                  pl.BlockSpec((BLOCK, ATTN_WIDTH), lambda b, i: (b * nb + i, 0)),
                  pl.BlockSpec((n_seq, KV_WIDTH), lambda b, i: (b, 0)),
                  pl.BlockSpec((n_seq, KV_WIDTH), lambda b, i: (b, 0)),
                  pl.BlockSpec((1, past, KV_WIDTH), lambda b, i: (b, 0, 0)),
                  pl.BlockSpec((1, past, KV_WIDTH), lambda b, i: (b, 0, 0))],
        out_specs=pl.BlockSpec((BLOCK, ATTN_WIDTH), lambda b, i: (b * nb + i, 0)),
        out_shape=jax.ShapeDtypeStruct((t, ATTN_WIDTH), BF16),
        name="latent_attention",
    )(sink, q, k, v, ck, cv)


def _pool_group(p_ref, r0, seq, w_ref, s_ref, g):
    n = SUB_TILE
    rows = n + 2 * POOL_HALO
    static = isinstance(r0, int)
    t0 = r0 % seq
    t = t0 + lax.broadcasted_iota(I32, (n, 1), 0)
    has_top = t0 > 0
    has_bottom = t0 + n < seq
    zeros = jnp.zeros((POOL_HALO, POOL_GROUP), F32)
    w = POOL_SIZES[g]
    cols = slice(g * POOL_GROUP, (g + 1) * POOL_GROUP)
    pg = p_ref[pl.ds(r0, n), cols]
    if static:
        top = p_ref[r0 - POOL_HALO:r0, cols] if has_top else zeros
        bottom = p_ref[r0 + n:r0 + n + POOL_HALO, cols] if has_bottom else zeros
    else:
        top_row = pl.multiple_of(jnp.maximum(r0 - POOL_HALO, 0), POOL_HALO)
        bottom_row = pl.multiple_of(jnp.minimum(r0 + n, p_ref.shape[0] - POOL_HALO), POOL_HALO)
        top = jnp.where(has_top, p_ref[pl.ds(top_row, POOL_HALO), cols], 0.0)
        bottom = jnp.where(has_bottom, p_ref[pl.ds(bottom_row, POOL_HALO), cols], 0.0)
    x = jnp.concatenate([top, pg, bottom], axis=0)
    fwd = x
    span = 1
    while span < w // 2:
        fwd = fwd + pltpu.roll(fwd, rows - span, 0)
        span *= 2
    if (w // 2) % SUBLANES == 0:
        wsum = fwd[POOL_HALO - w // 2:POOL_HALO - w // 2 + n] + fwd[POOL_HALO:POOL_HALO + n]
    else:
        wsum = (fwd + pltpu.roll(fwd, w // 2, 0))[POOL_HALO:POOL_HALO + n]
    lo = jnp.maximum(t - w // 2, 0)
    hi = jnp.minimum(t + w - w // 2, seq)
    inv_cnt = 1.0 / (hi - lo).astype(F32)
    mixed = wsum * inv_cnt - pg
    y = jnp.dot(mixed.astype(BF16), w_ref[g], preferred_element_type=F32)
    return (y * s_ref[:, cols]).astype(BF16)


def _pack_pair(lo, hi):
    return lax.bitcast_convert_type(pltpu.pack_elementwise([lo, hi], packed_dtype=BF16), I32)


def _unpack_pair(words):
    lo = pltpu.unpack_elementwise(words, index=0, packed_dtype=BF16, unpacked_dtype=F32)
    hi = pltpu.unpack_elementwise(words, index=1, packed_dtype=BF16, unpacked_dtype=F32)
    return lo, hi


def _outproj_kernel(a_ref, p_ref, x_ref, mod_ref, g_ref, wo_ref, wr_ref, pw_ref, ps_ref,
                    x1_ref, h_ref, aff_ref, *, seq):
    mod = mod_ref[0]
    groups = SUB_TILE // SUBLANES
    half = D_MODEL // 2
    tm = x_ref.shape[0]
    steps_per_p_block = p_ref.shape[0] // tm
    for s in range(tm // SUB_TILE):
        rows = slice(s * SUB_TILE, (s + 1) * SUB_TILE)
        grp = slice(s * groups, (s + 1) * groups)
        r0 = s * SUB_TILE
        if steps_per_p_block > 1:
            r0 = pl.multiple_of((pl.program_id(0) % steps_per_p_block) * tm + r0, SUB_TILE)
        pooled = jnp.concatenate([_pool_group(p_ref, r0, seq, pw_ref, ps_ref, g)
                                  for g in range(len(POOL_SIZES))], axis=1)
        mix = (jnp.dot(a_ref[rows, :], wo_ref[0:ATTN_WIDTH, :], preferred_element_type=F32)
               + jnp.dot(pooled, wo_ref[ATTN_WIDTH:D_MODEL, :], preferred_element_type=F32))
        x1 = x_ref[rows, :] + mod[2:3] * mix
        x1_ref[rows, :] = x1
        h = _norm_mod(x1, g_ref[...], mod[3:4], mod[4:5])
        logits = jnp.dot(h.astype(BF16), wr_ref[...], preferred_element_type=F32)
        lane = lax.broadcasted_iota(I32, logits.shape, 1)
        logits = jnp.where(lane < N_EXPERTS, logits, -jnp.inf)
        m = jnp.max(logits, axis=-1, keepdims=True)
        e = jnp.exp(logits - m)
        aff = e / jnp.sum(e, axis=-1, keepdims=True)
        aff_ref[rows, :] = aff[:, 0:N_EXPERTS]
        for c in range(HQ_TILES - 1):
            words = _pack_pair(h[:, c * LANES:(c + 1) * LANES],
                               h[:, half + c * LANES:half + (c + 1) * LANES])
            h_ref[grp, c * SUBLANES:(c + 1) * SUBLANES, :] = (
                words.reshape(groups, SUBLANES, LANES))
        h_ref[grp, (HQ_TILES - 1) * SUBLANES:, :] = (
            lax.bitcast_convert_type(aff, I32).reshape(groups, SUBLANES, LANES))


def _out_projection(attn, p, x2, mod3, mod_row, g, w_out_bf, w_router, pool_w, pool_scale, seq):
    t = x2.shape[0]
    tm = PROJ_TILE
    row = lambda i: (i, 0)
    p_rows = max(tm, seq)
    steps_per_p_block = p_rows // tm
    return pl.pallas_call(
        functools.partial(_outproj_kernel, seq=seq),
        grid=(t // tm,),
        in_specs=[pl.BlockSpec((tm, ATTN_WIDTH), row),
                  pl.BlockSpec((p_rows, POOL_WIDTH), lambda i: (i // steps_per_p_block, 0)),
                  pl.BlockSpec((tm, D_MODEL), row),
                  pl.BlockSpec((1, N_MOD, D_MODEL), lambda i: (mod_row(i), 0, 0)),
                  pl.BlockSpec((1, D_MODEL), lambda i: (0, 0)),
                  pl.BlockSpec((D_MODEL, D_MODEL), lambda i: (0, 0)),
                  pl.BlockSpec((D_MODEL, LANES), lambda i: (0, 0)),
                  pl.BlockSpec((len(POOL_SIZES), POOL_GROUP, POOL_GROUP), lambda i: (0, 0, 0)),
                  pl.BlockSpec((1, POOL_WIDTH), lambda i: (0, 0))],
        out_specs=[pl.BlockSpec((tm, D_MODEL), row),
                   pl.BlockSpec((tm // SUBLANES, HQ_TILES * SUBLANES, LANES), lambda i: (i, 0, 0)),
                   pl.BlockSpec((tm, N_EXPERTS), row)],
        out_shape=[jax.ShapeDtypeStruct((t, D_MODEL), F32),
                   jax.ShapeDtypeStruct((t // SUBLANES, HQ_TILES * SUBLANES, LANES), I32),
                   jax.ShapeDtypeStruct((t, N_EXPERTS), F32)],
        compiler_params=pltpu.CompilerParams(
            vmem_limit_bytes=_vmem_limit(2 * D_MODEL * D_MODEL * 2 + 24 * tm * D_MODEL * 4)),
        name="out_projection",
    )(attn, p, x2, mod3, g, w_out_bf, w_router, pool_w, pool_scale)


def _route_kernel(a_ref, idx_ref, slot_ref, off_ref, *, cap):
    a = a_ref[...]
    n_e, n_c, _ = a.shape
    rows = n_e * n_c

    def enough(cand):
        cand_f = lax.bitcast_convert_type(cand, F32)
        cnt = jnp.sum(jnp.sum((a >= cand_f).astype(F32), axis=1, keepdims=True),
                      axis=2, keepdims=True)
        return cnt >= cap

    def two_bits(it, thr):
        low = 28 - 2 * it
        for setting in (1, 2, 3):
            cand = thr | jnp.left_shift(jnp.int32(setting), low)
            best = jnp.where(enough(cand), cand, thr if setting == 1 else best)
        return best

    top = jnp.full((n_e, 1, 1), 1 << 30, I32)
    thr = jnp.where(enough(top), top, 0)
    thr = lax.fori_loop(0, 15, two_bits, thr)
    thr_f = lax.bitcast_convert_type(thr, F32)
    gt = (a > thr_f).astype(F32).reshape(rows, LANES)
    eq = (a == thr_f).astype(F32).reshape(rows, LANES)

    li = lax.broadcasted_iota(I32, (LANES, LANES), 0)
    lj = lax.broadcasted_iota(I32, (LANES, LANES), 1)
    upper_incl = (li <= lj).astype(BF16)
    ci = lax.broadcasted_iota(I32, (n_c, n_c), 0)
    cj = lax.broadcasted_iota(I32, (n_c, n_c), 1)
    before = (cj < ci).astype(BF16)
    whole = jnp.ones((n_c, n_c), BF16)

    def over_chunks(mat, col):
        wide = jnp.broadcast_to(col, (rows, LANES)).astype(BF16)
        side_by_side = jnp.concatenate([wide[e * n_c:(e + 1) * n_c] for e in range(n_e)], axis=1)
        res = jnp.dot(mat, side_by_side, preferred_element_type=F32)
        return jnp.concatenate([res[:, e * LANES:(e + 1) * LANES] for e in range(n_e)], axis=0)

    def prefix(x):
        incl = jnp.dot(x.astype(BF16), upper_incl, preferred_element_type=F32)
        tot = incl[:, LANES - 1:LANES]
        return incl, tot, over_chunks(before, tot)

    n_gt = over_chunks(whole, jnp.sum(gt, axis=1, keepdims=True))
    need = cap - n_gt
    incl_eq, _, off_eq = prefix(eq)
    rank_eq = off_eq + incl_eq - eq
    sel = jnp.where((eq > 0) & (rank_eq < need), 1.0, gt)
    incl, tot, off = prefix(sel)
    slot = off + incl - sel
    slot_ref[...] = jnp.where(sel > 0, slot, -1.0).astype(I32).reshape(n_e, n_c, LANES)
    off_ref[...] = off.astype(I32).reshape(n_e, n_c, LANES)

    s_lane = lax.broadcasted_iota(I32, (1, cap), 1).astype(F32)
    c_col = lax.broadcasted_iota(I32, (n_c, 1), 0).astype(F32)
    for e in range(n_e):
        r0 = e * n_c
        incl_e = incl[r0:r0 + n_c]
        off_e = off[r0:r0 + n_c, 0:1]
        tot_e = tot[r0:r0 + n_c]
        onehot = ((off_e <= s_lane) & (s_lane < off_e + tot_e)).astype(F32)
        counts = lax.dot_general(incl_e.astype(BF16), onehot.astype(BF16),
                                 (((0,), (0,)), ((), ())), preferred_element_type=F32)
        local = s_lane - jnp.sum(onehot * off_e, axis=0, keepdims=True)
        lane = jnp.sum((counts <= local).astype(F32), axis=0, keepdims=True)
        chunk = jnp.sum(onehot * c_col, axis=0, keepdims=True)
        idx_ref[e] = (chunk * LANES + lane).astype(I32)


def _routing(aff, cap):
    t = aff.shape[0]
    n_c = t // LANES
    a3 = aff.T.reshape(N_EXPERTS, n_c, LANES)
    return pl.pallas_call(
        functools.partial(_route_kernel, cap=cap),
        out_shape=[jax.ShapeDtypeStruct((N_EXPERTS, 1, cap), I32),
                   jax.ShapeDtypeStruct((N_EXPERTS, n_c, LANES), I32),
                   jax.ShapeDtypeStruct((N_EXPERTS, n_c, LANES), I32)],
        compiler_params=pltpu.CompilerParams(vmem_limit_bytes=_vmem_limit(48 << 20)),
        name="routing",
    )(a3)


def _gather_rows(table, row_ids):
    n_chunks = row_ids.shape[0]
    n_workers = SC_CORES * SC_SUBCORES
    per_worker = n_chunks // n_workers
    assert row_ids.shape[1] == GATHER_CHUNK and n_chunks % n_workers == 0
    mesh = plsc.VectorSubcoreMesh(core_axis_name="core", subcore_axis_name="subcore")

    @functools.partial(
        pl.kernel, mesh=mesh,
        out_type=jax.ShapeDtypeStruct((n_chunks * GATHER_CHUNK, LANES), I32),
        scratch_types=[pltpu.VMEM((per_worker, GATHER_CHUNK), I32),
                       pltpu.VMEM((2, GATHER_CHUNK, LANES), I32),
                       pltpu.SemaphoreType.DMA((2,)),
                       pltpu.SemaphoreType.DMA((2,))],
        name="gather_rows",
    )
    def gather(table_hbm, ids_hbm, out_hbm, ids_v, rows_v, gather_sem, store_sem):
        worker = lax.axis_index("subcore") * SC_CORES + lax.axis_index("core")
        first = worker * per_worker
        pltpu.sync_copy(ids_hbm.at[worker], ids_v)

        def fetch(j):
            return pltpu.make_async_copy(table_hbm.at[ids_v.at[j]], rows_v.at[j % 2],
                                         gather_sem.at[j % 2])

        def store(j):
            rows = pl.ds(pl.multiple_of((first + j) * GATHER_CHUNK, GATHER_CHUNK), GATHER_CHUNK)
            return pltpu.make_async_copy(rows_v.at[j % 2], out_hbm.at[rows], store_sem.at[j % 2])

        for j in range(per_worker):
            if j >= 2:
                store(j - 2).wait()
            fetch(j).start()
            if j >= 1:
                fetch(j - 1).wait()
                store(j - 1).start()
        fetch(per_worker - 1).wait()
        store(per_worker - 1).start()
        for j in range(max(per_worker - 2, 0), per_worker):
            store(j).wait()

    return gather(table, row_ids.reshape(n_workers, per_worker, GATHER_CHUNK))


def _packed_row_ids(idx, cap):
    tok = idx.reshape(-1, 1, cap // GATHER_CHUNK, GATHER_CHUNK)
    tile = jnp.arange(HQ_TILES, dtype=I32).reshape(1, HQ_TILES, 1, 1)
    ids = ((tok // SUBLANES) * HQ_TILES + tile) * SUBLANES + tok % SUBLANES
    return ids.reshape(-1, GATHER_CHUNK)


def _ffn_kernel(*refs, caps, n_f, row_chunk, first_expert, n_prior):
    n_g = len(caps)
    x_refs = refs[:n_g]
    wg_ref, wu_ref, wd_ref = refs[n_g:n_g + 3]
    y_refs = refs[n_g + 3 + n_prior:2 * n_g + 3 + n_prior]
    acc_ref = refs[2 * n_g + 3 + n_prior]
    e = pl.program_id(0)
    f = pl.program_id(1)

    @pl.when((e == 0) & (f == 0))
    def _():
        acc_ref[...] = jnp.zeros_like(acc_ref)

    def step(last):
        first = f == 0
        base = 0
        for x_ref, y_ref, cap in zip(x_refs, y_refs, caps):
            if last:
                aff = lax.bitcast_convert_type(x_ref[0, HQ_TILES - 1], F32)
                lane = lax.broadcasted_iota(I32, aff.shape, 1)
                gates = jnp.sum(jnp.where(lane == e + first_expert, aff, 0.0), axis=1,
                                keepdims=True)
                y_ref[0, cap:, :] = jnp.zeros((COMBINE_WINDOW, D_MODEL // 2), I32)
            for r in range(cap // row_chunk):
                rs = slice(r * row_chunk, (r + 1) * row_chunk)
                acc_rows = slice(base + r * row_chunk, base + (r + 1) * row_chunk)
                pairs = [_unpack_pair(x_ref[0, c, rs, :]) for c in range(HQ_TILES - 1)]
                x = jnp.concatenate([p[0] for p in pairs] + [p[1] for p in pairs], axis=1)
                gate_act = jnp.dot(x, wg_ref[0], preferred_element_type=F32)
                up = jnp.dot(x, wu_ref[0], preferred_element_type=F32)
                hid = (gate_act * jax.nn.sigmoid(gate_act)) * up
                part = jnp.dot(hid, wd_ref[0], preferred_element_type=F32)
                if last:
                    y = (part + acc_ref[acc_rows, :]) * gates[rs]
                    y_ref[0, rs, :] = _pack_pair(y[:, 0:D_MODEL // 2], y[:, D_MODEL // 2:])
                else:
                    acc_ref[acc_rows, :] = part + jnp.where(first, 0.0, acc_ref[acc_rows, :])
            base += cap

    @pl.when(f < n_f - 1)
    def _():
        step(False)

    @pl.when(f == n_f - 1)
    def _():
        step(True)


def _expert_ffn(xs_groups, w_gate, w_up, w_down, first_expert, prior_outputs):
    n_f = 4
    tf = D_EXPERT // n_f
    row_chunk = 512
    n_e = xs_groups[0].shape[0]
    e0 = first_expert
    caps = tuple(xs.shape[2] for xs in xs_groups)
    rows = sum(caps)
    est = (2 * HQ_TILES * rows * LANES * 4 + rows * D_MODEL * (4 + 2 * 2) + 2 * 3 * D_MODEL * tf * 4
           + row_chunk * (3 * tf + 2 * D_MODEL) * 4)
    x_specs = [pl.BlockSpec((1, HQ_TILES, cap, LANES), lambda e, f: (e, 0, 0, 0)) for cap in caps]
    y_shapes = [(N_EXPERTS, cap + COMBINE_WINDOW, D_MODEL // 2) for cap in caps]
    y_specs = [pl.BlockSpec((1,) + s[1:], lambda e, f: (e + e0, 0, 0)) for s in y_shapes]
    prior = list(prior_outputs or [])
    n_in = len(caps) + 3
    return pl.pallas_call(
        functools.partial(_ffn_kernel, caps=caps, n_f=n_f, row_chunk=row_chunk,
                          first_expert=e0, n_prior=len(prior)),
        grid=(n_e, n_f),
        in_specs=(x_specs + [pl.BlockSpec((1, D_MODEL, tf), lambda e, f: (e + e0, 0, f)),
                             pl.BlockSpec((1, D_MODEL, tf), lambda e, f: (e + e0, 0, f)),
                             pl.BlockSpec((1, tf, D_MODEL), lambda e, f: (e + e0, f, 0))]
                  + [pl.BlockSpec(memory_space=pl.ANY) for _ in prior]),
        out_specs=y_specs,
        out_shape=[jax.ShapeDtypeStruct(s, I32) for s in y_shapes],
        input_output_aliases={n_in + j: j for j in range(len(prior))},
        scratch_shapes=[pltpu.VMEM((rows, D_MODEL), F32)],
        compiler_params=pltpu.CompilerParams(
            dimension_semantics=("arbitrary", "arbitrary"),
            vmem_limit_bytes=_vmem_limit(est + (4 << 20))),
        name="expert_ffn",
    )(*xs_groups, w_gate, w_up, w_down, *prior)


def _combine_kernel(start_ref, nch_ref, wide_ref, x1_ref, mod_ref, g_ref, slot_ref, y_hbm, o_ref,
                    ybuf_ref, cols_ref, ffn_ref, sem, *, first_k, kblock):
    i = pl.program_id(0)
    tm = x1_ref.shape[0]
    cur = i % 2
    n_window_rows = N_EXPERTS * COMBINE_WINDOW

    def window_copy(e, src_row, buf):
        return pltpu.make_async_copy(
            y_hbm.at[e, pl.ds(pl.multiple_of(src_row, SUBLANES), COMBINE_WINDOW)],
            ybuf_ref.at[buf, e * COMBINE_WINDOW:(e + 1) * COMBINE_WINDOW], sem.at[buf])

    def chunk_copy(e, src_row, buf, dst_row):
        return pltpu.make_async_copy(
            y_hbm.at[e, pl.ds(pl.multiple_of(src_row, SUBLANES), COMBINE_CHUNK)],
            ybuf_ref.at[buf, pl.ds(pl.multiple_of(dst_row, COMBINE_CHUNK), COMBINE_CHUNK)],
            sem.at[buf])

    def fetch(tile, buf):
        @pl.when(wide_ref[tile] == 0)
        def _():
            for e in range(N_EXPERTS):
                window_copy(e, start_ref[tile, e], buf).start()

        @pl.when(wide_ref[tile] != 0)
        def _():
            pos = jnp.int32(0)
            for e in range(N_EXPERTS):
                start = start_ref[tile, e]
                nch = nch_ref[tile, e]

                def issue(c, carry, e=e, start=start, pos=pos):
                    chunk_copy(e, start + c * COMBINE_CHUNK, buf, pos + c * COMBINE_CHUNK).start()
                    return carry
                lax.fori_loop(0, nch, issue, 0)
                pos = pos + nch * COMBINE_CHUNK

    @pl.when(i == 0)
    def _():
        ybuf_ref[...] = jnp.zeros_like(ybuf_ref)
        fetch(0, 0)

    @pl.when(i + 1 < pl.num_programs(0))
    def _():
        fetch(i + 1, 1 - cur)

    def finish(ffn):
        out = x1_ref[...] + mod_ref[0][5:6] * ffn
        ms = jnp.mean(out * out, axis=-1, keepdims=True)
        o_ref[...] = (out * lax.rsqrt(ms + EPS)) * g_ref[...]

    def apply_selection(sel, k0, width):
        lo, hi = _unpack_pair(ybuf_ref[cur, pl.ds(k0, width), :])
        return jnp.concatenate([jnp.dot(sel, lo, preferred_element_type=F32),
                                jnp.dot(sel, hi, preferred_element_type=F32)], axis=1)

    @pl.when(wide_ref[i] == 0)
    def _():
        for e in range(N_EXPERTS):
            window_copy(e, jnp.int32(0), cur).wait()
        expert_lane = lax.broadcasted_iota(I32, (1, N_EXPERTS), 1)
        starts = jnp.zeros((1, N_EXPERTS), I32)
        for e in range(N_EXPERTS):
            starts = jnp.where(expert_lane == e, start_ref[i, e], starts)
        slots = slot_ref[...]
        rows_in_window = jnp.where(slots >= 0, (slots - starts).astype(F32), -1.0)
        window_of_lane = lax.broadcasted_iota(I32, (N_EXPERTS, n_window_rows), 1) // COMBINE_WINDOW
        spread = (window_of_lane == lax.broadcasted_iota(I32, (N_EXPERTS, n_window_rows), 0))
        target = jnp.dot(rows_in_window.astype(BF16), spread.astype(BF16),
                         preferred_element_type=F32)
        lane_row = (lax.broadcasted_iota(I32, (tm, n_window_rows), 1) % COMBINE_WINDOW).astype(F32)
        sel = jnp.where(target == lane_row, 1.0, 0.0)
        ffn = None
        for k0 in range(0, n_window_rows, kblock):
            part = apply_selection(sel[:, k0:k0 + kblock], k0, kblock)
            ffn = part if ffn is None else ffn + part
        finish(ffn)

    @pl.when(wide_ref[i] != 0)
    def _():
        pos = jnp.int32(0)
        begins = []
        for e in range(N_EXPERTS):
            begins.append(pos)
            pos = pos + nch_ref[i, e] * COMBINE_CHUNK
        begins.append(pos)

        def drain(c, carry):
            chunk_copy(0, jnp.int32(0), cur, jnp.int32(0)).wait()
            return carry
        lax.fori_loop(0, pos // COMBINE_CHUNK, drain, 0)

        slots = slot_ref[...]
        for e in range(N_EXPERTS):
            s_e = slots[:, e:e + 1]
            col = jnp.where(s_e >= 0, s_e + (begins[e] - start_ref[i, e]), -1)
            cols_ref[e] = jnp.broadcast_to(col, (tm, LANES))

        lane = lax.broadcasted_iota(I32, (tm, LANES), 1)

        def selection(k0, width):
            halves = []
            for h in range(width // LANES):
                c0 = k0 + h * LANES
                target = lane + c0
                e_lo = jnp.int32(0)
                e_hi = jnp.int32(0)
                for e in range(N_EXPERTS):
                    e_lo = e_lo + (begins[e + 1] <= c0).astype(I32)
                    e_hi = e_hi + (begins[e] < c0 + LANES).astype(I32)

                def mark(e, hit, target=target):
                    return jnp.where(cols_ref[e] == target, 1.0, hit)
                halves.append(lax.fori_loop(e_lo, e_hi, mark, jnp.zeros((tm, LANES), F32)))
            return jnp.concatenate(halves, axis=1)

        ffn_ref[...] = apply_selection(selection(0, first_k), 0, first_k)

        def kstep(kb, carry):
            k0 = pl.multiple_of(kb * kblock, kblock)
            ffn_ref[...] += apply_selection(selection(k0, kblock), k0, kblock)
            return carry
        lax.fori_loop(first_k // kblock, (pos + kblock - 1) // kblock, kstep, 0)
        finish(ffn_ref[...])


def _combine(x1, mod3, mod_row, g_final, slot_t, tile_start, tile_nch, tile_wide, y):
    t = x1.shape[0]
    tm = TOKEN_TILE
    first_k = 2 * tm + N_EXPERTS * COMBINE_CHUNK
    kblock = 256
    max_rows = N_EXPERTS * (tm + 2 * COMBINE_CHUNK)
    max_rows = -(-max_rows // kblock) * kblock
    row = lambda i, *_: (i, 0)
    grid_spec = pltpu.PrefetchScalarGridSpec(
        num_scalar_prefetch=3,
        grid=(t // tm,),
        in_specs=[pl.BlockSpec((tm, D_MODEL), row),
                  pl.BlockSpec((1, N_MOD, D_MODEL), lambda i, *_: (mod_row(i), 0, 0)),
                  pl.BlockSpec((1, D_MODEL), lambda i, *_: (0, 0)),
                  pl.BlockSpec((tm, N_EXPERTS), row),
                  pl.BlockSpec(memory_space=pl.ANY)],
        out_specs=pl.BlockSpec((tm, D_MODEL), row),
        scratch_shapes=[pltpu.VMEM((2, max_rows, D_MODEL // 2), I32),
                        pltpu.VMEM((N_EXPERTS, tm, LANES), I32),
                        pltpu.VMEM((tm, D_MODEL), F32),
                        pltpu.SemaphoreType.DMA((2,))],
    )
    return pl.pallas_call(
        functools.partial(_combine_kernel, first_k=first_k, kblock=kblock),
        grid_spec=grid_spec,
        out_shape=jax.ShapeDtypeStruct((t, D_MODEL), F32),
        compiler_params=pltpu.CompilerParams(
            dimension_semantics=("arbitrary",),
            vmem_limit_bytes=_vmem_limit(2 * max_rows * D_MODEL * 2 + 16 * tm * D_MODEL * 4)),
        name="combine",
    )(tile_start, tile_nch, tile_wide, x1, mod3, g_final, slot_t, y)


def _rope_tables(n):
    rows = n // GRID_W
    row = jnp.repeat(jnp.arange(rows, dtype=F32), GRID_W)
    col = jnp.tile(jnp.arange(GRID_W, dtype=F32), rows)
    inv = ROPE_THETA ** (-jnp.arange(ROPE_FREQS, dtype=F32) / ROPE_FREQS)
    ang_r = row[:, None] * inv
    ang_c = col[:, None] * inv
    zero = jnp.zeros_like(ang_r)
    cos = jnp.concatenate([jnp.cos(ang_r)] * 2 + [jnp.cos(ang_c)] * 2, axis=1)
    sin_a = jnp.concatenate([-jnp.sin(ang_r), zero, -jnp.sin(ang_c), zero], axis=1)
    sin_b = jnp.concatenate([zero, jnp.sin(ang_r), zero, jnp.sin(ang_c)], axis=1)
    return cos, sin_a, sin_b


def _token_group(x, mod3, mod_row, seq, weights, rope_tabs, ctx_kv):
    (norm_mix, w_in_bf, sink, pool_w, pool_scale, w_out_bf, norm_ffn, w_router,
     w_gate, w_up, w_down, norm_final) = weights
    b = x.shape[0]
    t = b * seq
    x2 = x.reshape(t, D_MODEL)
    q, k, v, p, *state = _in_projection(x2, mod3, mod_row(PROJ_TILE), norm_mix, w_in_bf,
                                        rope_tabs)
    if ctx_kv is None:
        attn = _context_attention(q, k, v, sink, seq)
    else:
        attn = _latent_attention(q, k, v, ctx_kv[0], ctx_kv[1], sink, seq)
    x1, h_packed, aff = _out_projection(attn, p, x2, mod3, mod_row(PROJ_TILE), norm_ffn,
                                        w_out_bf, w_router, pool_w, pool_scale, seq)

    cap = EC_FACTOR * t // N_EXPERTS
    idx, slot3, off3 = _routing(aff, cap)
    table = h_packed.reshape(-1, LANES)
    per_range = N_EXPERTS // FFN_RANGES
    xs = []
    for r in range(FFN_RANGES):
        ids = _packed_row_ids(idx[r * per_range:(r + 1) * per_range], cap)
        xs.append(_gather_rows(table, ids).reshape(per_range, HQ_TILES, cap, LANES))

    chunks_per_tile = TOKEN_TILE // LANES
    tile_off = off3[:, ::chunks_per_tile, 0]
    tile_end = jnp.concatenate([tile_off[:, 1:], jnp.full((N_EXPERTS, 1), cap, I32)], axis=1)
    tile_start = (tile_off // SUBLANES) * SUBLANES
    tile_nch = jnp.where(tile_end > tile_off,
                         (tile_end - tile_start + COMBINE_CHUNK - 1) // COMBINE_CHUNK, 0)
    tile_wide = jnp.any(tile_end - tile_start > COMBINE_WINDOW, axis=0).astype(I32)
    slot_t = slot3.reshape(N_EXPERTS, t).T

    def finish(y):
        out = _combine(x1, mod3, mod_row(TOKEN_TILE), norm_final, slot_t, tile_start.T,
                       tile_nch.T, tile_wide, y)
        return out.reshape(b, seq, D_MODEL)
    return xs, finish, state


def kernel(x_prompt, x_sample, c, cache_k, cache_v, c_ctx, w_ada, b_ada, norm_mix, w_in,
           sink_logits, pool_w, pool_scale, w_out, norm_ffn, w_router, w_gate, w_up, w_down,
           norm_final):
    n_b, seq, _ = x_prompt.shape
    n_db, n_lat, _ = x_sample.shape
    assert 1 + n_db <= MOD_ROWS and seq == TOKEN_TILE and n_lat % PROJ_TILE == 0
    assert (n_b * seq) % PROJ_TILE == 0

    cond = jnp.concatenate(
        [c_ctx[None, :], c, jnp.zeros((MOD_ROWS - 1 - n_db, D_MODEL), F32)], axis=0)
    mod3 = _modulation(cond, w_ada[0], b_ada[0]).reshape(MOD_ROWS, N_MOD, D_MODEL)

    w_router_bf = jnp.pad(w_router[0], ((0, 0), (0, LANES - N_EXPERTS))).astype(BF16)
    weights = (norm_mix[0][None, :], w_in[0].astype(BF16), sink_logits[0], pool_w[0].astype(BF16),
               pool_scale[0][None, :], w_out[0].astype(BF16), norm_ffn[0][None, :], w_router_bf,
               w_gate.reshape(w_gate.shape[1:]), w_up.reshape(w_up.shape[1:]),
               w_down.reshape(w_down.shape[1:]), norm_final[None, :])

    xs_p, finish_p, (k_p, v_p) = _token_group(
        x_prompt, mod3, lambda tile: (lambda i: 0), seq, weights, None, None)

    past = cache_k.shape[2]
    ck = cache_k[:, 0].reshape(n_db, past, KV_WIDTH)
    cv = cache_v[:, 0].reshape(n_db, past, KV_WIDTH)
    xs_l, finish_l, _ = _token_group(
        x_sample, mod3, lambda tile: (lambda i: 1 + i // (n_lat // tile)), n_lat, weights,
        _rope_tables(n_lat), (ck, cv))

    ys = None
    for r in range(FFN_RANGES):
        ys = _expert_ffn([xs_p[r], xs_l[r]], weights[8], weights[9], weights[10],
                         r * (N_EXPERTS // FFN_RANGES), ys)
    y_prompt = finish_p(ys[0])
    y_sample = finish_l(ys[1])

    state_k = k_p.reshape(n_b, 1, seq, N_KV_HEADS, HEAD_DIM)
    state_v = v_p.reshape(n_b, 1, seq, N_KV_HEADS, HEAD_DIM)
    return (y_prompt, y_sample, state_k, state_v)
```

```python
import functools

import jax
import jax.numpy as jnp
import numpy as np
from jax import lax
from jax.experimental import pallas as pl
from jax.experimental.pallas import tpu as pltpu
from jax.experimental.pallas import tpu_sc as plsc

F32 = jnp.float32
BF16 = jnp.bfloat16
I32 = jnp.int32

D_MODEL = 2048
N_HEADS = 8
N_KV_HEADS = 2
HEAD_DIM = 128
Q_PER_KV = N_HEADS // N_KV_HEADS
ATTN_WIDTH = N_HEADS * HEAD_DIM
KV_WIDTH = N_KV_HEADS * HEAD_DIM
POOL_WIDTH = D_MODEL - ATTN_WIDTH
POOL_SIZES = (2, 4, 8, 16)
POOL_GROUP = POOL_WIDTH // len(POOL_SIZES)
IN_WIDTH = ATTN_WIDTH + 2 * KV_WIDTH + POOL_WIDTH
WINDOW = 128
BLOCK = 128
GRID_W = 64
ROPE_THETA = 10000.0
ROPE_FREQS = HEAD_DIM // 4
N_EXPERTS = 16
EC_FACTOR = 2
D_EXPERT = 1024
N_MOD = 6
EPS = 1e-6
NEG = -1e30
LOG2_E = 1.4426950408889634
ATTN_SCALE = HEAD_DIM ** -0.5 * LOG2_E

LANES = 128
SUBLANES = 8
BF16_ROWS = 16
VMEM_CAP = 64 * 1024 * 1024
SC_CORES = 2
SC_SUBCORES = 16

MOD_ROWS = 8
TOKEN_TILE = 256
PROJ_TILE = 512
SUB_TILE = 256
HQ_TILES = D_MODEL // 2 // LANES + 1
GATHER_CHUNK = 128
FFN_RANGES = 2
POOL_HALO = 8
COMBINE_CHUNK = BF16_ROWS
COMBINE_WINDOW = 64


def _vmem_limit(nbytes):
    return int(min(VMEM_CAP - (4 << 20), max(nbytes, 16 << 20)))


def _mod_kernel(c_ref, w_ref, b_ref, o_ref):
    c = c_ref[...]
    s = c * jax.nn.sigmoid(c)
    o_ref[...] = jnp.dot(s.astype(BF16), w_ref[...].astype(BF16),
                         preferred_element_type=F32) + b_ref[...]


def _modulation(cond, w_ada, b_ada):
    n = w_ada.shape[1]
    tn = 1024
    return pl.pallas_call(
        _mod_kernel,
        grid=(n // tn,),
        in_specs=[pl.BlockSpec((MOD_ROWS, D_MODEL), lambda j: (0, 0)),
                  pl.BlockSpec((D_MODEL, tn), lambda j: (0, j)),
                  pl.BlockSpec((1, tn), lambda j: (0, j))],
        out_specs=pl.BlockSpec((MOD_ROWS, tn), lambda j: (0, j)),
        out_shape=jax.ShapeDtypeStruct((MOD_ROWS, n), F32),
        compiler_params=pltpu.CompilerParams(
            vmem_limit_bytes=_vmem_limit(3 * D_MODEL * tn * 4)),
        name="modulation",
    )(cond, w_ada, b_ada.reshape(1, n))


def _norm_mod(x, g, shift, scale):
    ms = jnp.mean(x * x, axis=-1, keepdims=True)
    y = x * lax.rsqrt(ms + EPS)
    return (y * g) * (1.0 + scale) + shift


def _inproj_kernel(*refs, rope):
    if rope:
        x_ref, mod_ref, g_ref, w_ref, cos_ref, sa_ref, sb_ref, q_ref, k_ref, v_ref, p_ref = refs
    else:
        x_ref, mod_ref, g_ref, w_ref, q_ref, k_ref, v_ref, p_ref, ks_ref, vs_ref = refs
    mod = mod_ref[0]
    for s in range(x_ref.shape[0] // SUB_TILE):
        rows = slice(s * SUB_TILE, (s + 1) * SUB_TILE)
        h = _norm_mod(x_ref[rows, :], g_ref[...], mod[0:1], mod[1:2])
        u = jnp.dot(h.astype(BF16), w_ref[...], preferred_element_type=F32)

        def rot(xh, rows=rows):
            return (xh * cos_ref[rows, :] + pltpu.roll(xh, LANES - ROPE_FREQS, 1) * sa_ref[rows, :]
                    + pltpu.roll(xh, ROPE_FREQS, 1) * sb_ref[rows, :])

        for hd in range(N_HEADS):
            xh = u[:, hd * HEAD_DIM:(hd + 1) * HEAD_DIM] * ATTN_SCALE
            q_ref[rows, hd * HEAD_DIM:(hd + 1) * HEAD_DIM] = (rot(xh) if rope else xh).astype(BF16)
        for hd in range(N_KV_HEADS):
            lo = ATTN_WIDTH + hd * HEAD_DIM
            xh = u[:, lo:lo + HEAD_DIM]
            k_ref[rows, hd * HEAD_DIM:(hd + 1) * HEAD_DIM] = rot(xh) if rope else xh
        v_ref[rows, :] = u[:, ATTN_WIDTH + KV_WIDTH:ATTN_WIDTH + 2 * KV_WIDTH]
        p_ref[rows, :] = u[:, ATTN_WIDTH + 2 * KV_WIDTH:]
        if not rope:
            for hd in range(N_KV_HEADS):
                state_rows = pl.ds(s * SUB_TILE * N_KV_HEADS + hd, SUB_TILE, stride=N_KV_HEADS)
                lo = ATTN_WIDTH + hd * HEAD_DIM
                ks_ref[state_rows, :] = u[:, lo:lo + HEAD_DIM]
                vs_ref[state_rows, :] = u[:, lo + KV_WIDTH:lo + KV_WIDTH + HEAD_DIM]


def _in_projection(x2, mod3, mod_row, g, w_in_bf, rope_tabs):
    t = x2.shape[0]
    tm = PROJ_TILE
    rope = rope_tabs is not None
    row = lambda i: (i, 0)
    in_specs = [pl.BlockSpec((tm, D_MODEL), row),
                pl.BlockSpec((1, N_MOD, D_MODEL), lambda i: (mod_row(i), 0, 0)),
                pl.BlockSpec((1, D_MODEL), lambda i: (0, 0)),
                pl.BlockSpec((D_MODEL, IN_WIDTH), lambda i: (0, 0))]
    args = [x2, mod3, g, w_in_bf]
    if rope:
        n_seq = rope_tabs[0].shape[0]
        seq_blocks = n_seq // tm
        for tab in rope_tabs:
            in_specs.append(pl.BlockSpec((tm, HEAD_DIM), lambda i: (i % seq_blocks, 0)))
            args.append(tab)
    out_specs = [pl.BlockSpec((tm, ATTN_WIDTH), row),
                 pl.BlockSpec((tm, KV_WIDTH), row),
                 pl.BlockSpec((tm, KV_WIDTH), row),
                 pl.BlockSpec((tm, POOL_WIDTH), row)]
    out_shape = [jax.ShapeDtypeStruct((t, ATTN_WIDTH), BF16),
                 jax.ShapeDtypeStruct((t, KV_WIDTH), F32),
                 jax.ShapeDtypeStruct((t, KV_WIDTH), F32),
                 jax.ShapeDtypeStruct((t, POOL_WIDTH), F32)]
    if not rope:
        for _ in range(2):
            out_specs.append(pl.BlockSpec((tm * N_KV_HEADS, HEAD_DIM), row))
            out_shape.append(jax.ShapeDtypeStruct((t * N_KV_HEADS, HEAD_DIM), F32))
    return pl.pallas_call(
        functools.partial(_inproj_kernel, rope=rope),
        grid=(t // tm,),
        in_specs=in_specs,
        out_specs=out_specs,
        out_shape=out_shape,
        compiler_params=pltpu.CompilerParams(
            vmem_limit_bytes=_vmem_limit(2 * D_MODEL * IN_WIDTH * 2 + 24 * tm * D_MODEL * 4)),
        name="in_projection",
    )(*args)


def _softmax_pv(s_list, v_list, sink_col):
    m = sink_col
    for s in s_list:
        m = jnp.maximum(m, jnp.max(s, axis=-1, keepdims=True))
    denom = jnp.exp2(sink_col - m)
    out = None
    for s, v in zip(s_list, v_list):
        e = jnp.exp2(s - m)
        if v.shape[1] == HEAD_DIM:
            denom = denom + jnp.sum(e, axis=-1, keepdims=True)
        o = jnp.dot(e.astype(BF16), v, preferred_element_type=F32)
        out = o if out is None else out + o
    if out.shape[1] > HEAD_DIM:
        denom = denom + out[:, HEAD_DIM:HEAD_DIM + 1]
    return out[:, 0:HEAD_DIM] * (1.0 / denom)


def _with_ones(v):
    return jnp.concatenate([v.astype(BF16), jnp.ones(v.shape, BF16)], axis=1)


def _stack_heads(q, kv):
    return jnp.concatenate(
        [q[:, (kv * Q_PER_KV + g) * HEAD_DIM:(kv * Q_PER_KV + g + 1) * HEAD_DIM]
         for g in range(Q_PER_KV)], axis=0)


def _sink_column(sink_ref, kv, rows):
    r = lax.broadcasted_iota(I32, (Q_PER_KV * rows, 1), 0)
    col = jnp.zeros((Q_PER_KV * rows, 1), F32)
    for g in range(Q_PER_KV):
        col = jnp.where((r >= g * rows) & (r < (g + 1) * rows), sink_ref[kv * Q_PER_KV + g], col)
    return col * LOG2_E


def _qk(q, k):
    return lax.dot_general(q, k, (((1,), (1,)), ((), ())), preferred_element_type=F32)


def _ctx_attn_kernel(sink_ref, q_ref, k_ref, v_ref, o_ref):
    rows = q_ref.shape[0]
    q = q_ref[...]
    for kv in range(N_KV_HEADS):
        kh = k_ref[:, kv * HEAD_DIM:(kv + 1) * HEAD_DIM].astype(BF16)
        vh = v_ref[:, kv * HEAD_DIM:(kv + 1) * HEAD_DIM].astype(BF16)
        qs = _stack_heads(q, kv)
        s = _qk(qs, kh)
        o = _softmax_pv([s], [vh], _sink_column(sink_ref, kv, rows))
        for g in range(Q_PER_KV):
            hd = kv * Q_PER_KV + g
            o_ref[:, hd * HEAD_DIM:(hd + 1) * HEAD_DIM] = o[g * rows:(g + 1) * rows].astype(BF16)


def _context_attention(q, k, v, sink, seq):
    t = q.shape[0]
    row = lambda b: (b, 0)
    return pl.pallas_call(
        _ctx_attn_kernel,
        grid=(t // seq,),
        in_specs=[pl.BlockSpec(memory_space=pltpu.SMEM),
                  pl.BlockSpec((seq, ATTN_WIDTH), row),
                  pl.BlockSpec((seq, KV_WIDTH), row),
                  pl.BlockSpec((seq, KV_WIDTH), row)],
        out_specs=pl.BlockSpec((seq, ATTN_WIDTH), row),
        out_shape=jax.ShapeDtypeStruct((t, ATTN_WIDTH), BF16),
        name="context_attention",
    )(sink, q, k, v)


def _lat_attn_kernel(sink_ref, q_ref, k_ref, v_ref, ck_ref, cv_ref, o_ref, *, n_seq):
    i = pl.program_id(1)
    band = 3 * BLOCK
    start = pl.multiple_of(jnp.clip((i - 1) * BLOCK, 0, n_seq - band), BLOCK)
    rows = Q_PER_KV * BLOCK
    qpos = i * BLOCK + lax.broadcasted_iota(I32, (rows, band), 0) % BLOCK
    kpos = start + lax.broadcasted_iota(I32, (rows, band), 1)
    mask = jnp.abs(kpos - qpos) <= WINDOW
    q = q_ref[...]
    for kv in range(N_KV_HEADS):
        cols = slice(kv * HEAD_DIM, (kv + 1) * HEAD_DIM)
        kb = k_ref[pl.ds(start, band), cols].astype(BF16)
        vb = _with_ones(v_ref[pl.ds(start, band), cols])
        ck = ck_ref[0, :, cols].astype(BF16)
        cv = _with_ones(cv_ref[0, :, cols])
        qs = _stack_heads(q, kv)
        s_loc = jnp.where(mask, _qk(qs, kb), NEG)
        s_ctx = _qk(qs, ck)
        o = _softmax_pv([s_loc, s_ctx], [vb, cv], _sink_column(sink_ref, kv, BLOCK))
        for g in range(Q_PER_KV):
            hd = kv * Q_PER_KV + g
            o_ref[:, hd * HEAD_DIM:(hd + 1) * HEAD_DIM] = o[g * BLOCK:(g + 1) * BLOCK].astype(BF16)


def _latent_attention(q, k, v, ck, cv, sink, n_seq):
    t = q.shape[0]
    nb = n_seq // BLOCK
    past = ck.shape[1]
    return pl.pallas_call(
        functools.partial(_lat_attn_kernel, n_seq=n_seq),
        grid=(t // n_seq, nb),
        in_specs=[pl.BlockSpec(memory_space=pltpu.SMEM),
                  pl.BlockSpec((BLOCK, ATTN_WIDTH), lambda b, i: (b * nb + i, 0)),
                  pl.BlockSpec((n_seq, KV_WIDTH), lambda b, i: (b, 0)),
                  pl.BlockSpec((n_seq, KV_WIDTH), lambda b, i: (b, 0)),
                  pl.BlockSpec((1, past, KV_WIDTH), lambda b, i: (b, 0, 0)),
                  pl.BlockSpec((1, past, KV_WIDTH), lambda b, i: (b, 0, 0))],
        out_specs=pl.BlockSpec((BLOCK, ATTN_WIDTH), lambda b, i: (b * nb + i, 0)),
        out_shape=jax.ShapeDtypeStruct((t, ATTN_WIDTH), BF16),
        name="latent_attention",
    )(sink, q, k, v, ck, cv)


def _pool_group(p_ref, r0, seq, w_ref, s_ref, g):
    n = SUB_TILE
    rows = n + 2 * POOL_HALO
    static = isinstance(r0, int)
    t0 = r0 % seq
    t = t0 + lax.broadcasted_iota(I32, (n, 1), 0)
    has_top = t0 > 0
    has_bottom = t0 + n < seq
    zeros = jnp.zeros((POOL_HALO, POOL_GROUP), F32)
    w = POOL_SIZES[g]
    cols = slice(g * POOL_GROUP, (g + 1) * POOL_GROUP)
    pg = p_ref[pl.ds(r0, n), cols]
    if static:
        top = p_ref[r0 - POOL_HALO:r0, cols] if has_top else zeros
        bottom = p_ref[r0 + n:r0 + n + POOL_HALO, cols] if has_bottom else zeros
    else:
        top_row = pl.multiple_of(jnp.maximum(r0 - POOL_HALO, 0), POOL_HALO)
        bottom_row = pl.multiple_of(jnp.minimum(r0 + n, p_ref.shape[0] - POOL_HALO), POOL_HALO)
        top = jnp.where(has_top, p_ref[pl.ds(top_row, POOL_HALO), cols], 0.0)
        bottom = jnp.where(has_bottom, p_ref[pl.ds(bottom_row, POOL_HALO), cols], 0.0)
    x = jnp.concatenate([top, pg, bottom], axis=0)
    fwd = x
    span = 1
    while span < w // 2:
        fwd = fwd + pltpu.roll(fwd, rows - span, 0)
        span *= 2
    if (w // 2) % SUBLANES == 0:
        wsum = fwd[POOL_HALO - w // 2:POOL_HALO - w // 2 + n] + fwd[POOL_HALO:POOL_HALO + n]
    else:
        wsum = (fwd + pltpu.roll(fwd, w // 2, 0))[POOL_HALO:POOL_HALO + n]
    lo = jnp.maximum(t - w // 2, 0)
    hi = jnp.minimum(t + w - w // 2, seq)
    inv_cnt = 1.0 / (hi - lo).astype(F32)
    mixed = wsum * inv_cnt - pg
    y = jnp.dot(mixed.astype(BF16), w_ref[g], preferred_element_type=F32)
    return (y * s_ref[:, cols]).astype(BF16)


def _pack_pair(lo, hi):
    return lax.bitcast_convert_type(pltpu.pack_elementwise([lo, hi], packed_dtype=BF16), I32)


def _unpack_pair(words):
    lo = pltpu.unpack_elementwise(words, index=0, packed_dtype=BF16, unpacked_dtype=F32)
    hi = pltpu.unpack_elementwise(words, index=1, packed_dtype=BF16, unpacked_dtype=F32)
    return lo, hi


def _outproj_kernel(a_ref, p_ref, x_ref, mod_ref, g_ref, wo_ref, wr_ref, pw_ref, ps_ref,
                    x1_ref, h_ref, aff_ref, *, seq):
    mod = mod_ref[0]
    groups = SUB_TILE // SUBLANES
    half = D_MODEL // 2
    tm = x_ref.shape[0]
    steps_per_p_block = p_ref.shape[0] // tm
    for s in range(tm // SUB_TILE):
        rows = slice(s * SUB_TILE, (s + 1) * SUB_TILE)
        grp = slice(s * groups, (s + 1) * groups)
        r0 = s * SUB_TILE
        if steps_per_p_block > 1:
            r0 = pl.multiple_of((pl.program_id(0) % steps_per_p_block) * tm + r0, SUB_TILE)
        pooled = jnp.concatenate([_pool_group(p_ref, r0, seq, pw_ref, ps_ref, g)
                                  for g in range(len(POOL_SIZES))], axis=1)
        mix = (jnp.dot(a_ref[rows, :], wo_ref[0:ATTN_WIDTH, :], preferred_element_type=F32)
               + jnp.dot(pooled, wo_ref[ATTN_WIDTH:D_MODEL, :], preferred_element_type=F32))
        x1 = x_ref[rows, :] + mod[2:3] * mix
        x1_ref[rows, :] = x1
        h = _norm_mod(x1, g_ref[...], mod[3:4], mod[4:5])
        logits = jnp.dot(h.astype(BF16), wr_ref[...], preferred_element_type=F32)
        lane = lax.broadcasted_iota(I32, logits.shape, 1)
        logits = jnp.where(lane < N_EXPERTS, logits, -jnp.inf)
        m = jnp.max(logits, axis=-1, keepdims=True)
        e = jnp.exp(logits - m)
        aff = e / jnp.sum(e, axis=-1, keepdims=True)
        aff_ref[rows, :] = aff[:, 0:N_EXPERTS]
        for c in range(HQ_TILES - 1):
            words = _pack_pair(h[:, c * LANES:(c + 1) * LANES],
                               h[:, half + c * LANES:half + (c + 1) * LANES])
            h_ref[grp, c * SUBLANES:(c + 1) * SUBLANES, :] = (
                words.reshape(groups, SUBLANES, LANES))
        h_ref[grp, (HQ_TILES - 1) * SUBLANES:, :] = (
            lax.bitcast_convert_type(aff, I32).reshape(groups, SUBLANES, LANES))


def _out_projection(attn, p, x2, mod3, mod_row, g, w_out_bf, w_router, pool_w, pool_scale, seq):
    t = x2.shape[0]
    tm = PROJ_TILE
    row = lambda i: (i, 0)
    p_rows = max(tm, seq)
    steps_per_p_block = p_rows // tm
    return pl.pallas_call(
        functools.partial(_outproj_kernel, seq=seq),
        grid=(t // tm,),
        in_specs=[pl.BlockSpec((tm, ATTN_WIDTH), row),
                  pl.BlockSpec((p_rows, POOL_WIDTH), lambda i: (i // steps_per_p_block, 0)),
                  pl.BlockSpec((tm, D_MODEL), row),
                  pl.BlockSpec((1, N_MOD, D_MODEL), lambda i: (mod_row(i), 0, 0)),
                  pl.BlockSpec((1, D_MODEL), lambda i: (0, 0)),
                  pl.BlockSpec((D_MODEL, D_MODEL), lambda i: (0, 0)),
                  pl.BlockSpec((D_MODEL, LANES), lambda i: (0, 0)),
                  pl.BlockSpec((len(POOL_SIZES), POOL_GROUP, POOL_GROUP), lambda i: (0, 0, 0)),
                  pl.BlockSpec((1, POOL_WIDTH), lambda i: (0, 0))],
        out_specs=[pl.BlockSpec((tm, D_MODEL), row),
                   pl.BlockSpec((tm // SUBLANES, HQ_TILES * SUBLANES, LANES), lambda i: (i, 0, 0)),
                   pl.BlockSpec((tm, N_EXPERTS), row)],
        out_shape=[jax.ShapeDtypeStruct((t, D_MODEL), F32),
                   jax.ShapeDtypeStruct((t // SUBLANES, HQ_TILES * SUBLANES, LANES), I32),
                   jax.ShapeDtypeStruct((t, N_EXPERTS), F32)],
        compiler_params=pltpu.CompilerParams(
            vmem_limit_bytes=_vmem_limit(2 * D_MODEL * D_MODEL * 2 + 24 * tm * D_MODEL * 4)),
        name="out_projection",
    )(attn, p, x2, mod3, g, w_out_bf, w_router, pool_w, pool_scale)


def _route_kernel(a_ref, idx_ref, slot_ref, off_ref, slot_scr, *, cap):
    a = a_ref[...]
    n_e, n_c, _ = a.shape
    rows = n_e * n_c

    def enough(cand):
        cand_f = lax.bitcast_convert_type(cand, F32)
        cnt = jnp.sum(jnp.sum((a >= cand_f).astype(F32), axis=1, keepdims=True),
                      axis=2, keepdims=True)
        return cnt >= cap

    def two_bits(it, thr):
        low = 28 - 2 * it
        for setting in (1, 2, 3):
            cand = thr | jnp.left_shift(jnp.int32(setting), low)
            best = jnp.where(enough(cand), cand, thr if setting == 1 else best)
        return best

    top = jnp.full((n_e, 1, 1), 1 << 30, I32)
    thr = jnp.where(enough(top), top, 0)
    thr = lax.fori_loop(0, 15, two_bits, thr)
    thr_f = lax.bitcast_convert_type(thr, F32)
    gt = (a > thr_f).astype(F32).reshape(rows, LANES)
    eq = (a == thr_f).astype(F32).reshape(rows, LANES)

    li = lax.broadcasted_iota(I32, (LANES, LANES), 0)
    lj = lax.broadcasted_iota(I32, (LANES, LANES), 1)
    upper_incl = (li <= lj).astype(BF16)
    ci = lax.broadcasted_iota(I32, (n_c, n_c), 0)
    cj = lax.broadcasted_iota(I32, (n_c, n_c), 1)
    before = (cj < ci).astype(BF16)
    whole = jnp.ones((n_c, n_c), BF16)

    def over_chunks(mat, col):
        wide = jnp.broadcast_to(col, (rows, LANES)).astype(BF16)
        side_by_side = jnp.concatenate([wide[e * n_c:(e + 1) * n_c] for e in range(n_e)], axis=1)
        res = jnp.dot(mat, side_by_side, preferred_element_type=F32)
        return jnp.concatenate([res[:, e * LANES:(e + 1) * LANES] for e in range(n_e)], axis=0)

    def prefix(x):
        incl = jnp.dot(x.astype(BF16), upper_incl, preferred_element_type=F32)
        tot = incl[:, LANES - 1:LANES]
        return incl, tot, over_chunks(before, tot)

    n_gt = over_chunks(whole, jnp.sum(gt, axis=1, keepdims=True))
    need = cap - n_gt
    incl_eq, _, off_eq = prefix(eq)
    rank_eq = off_eq + incl_eq - eq
    sel = jnp.where((eq > 0) & (rank_eq < need), 1.0, gt)
    incl, tot, off = prefix(sel)
    slot = off + incl - sel
    slot_scr[...] = jnp.where(sel > 0, slot, -1.0)
    for c in range(n_c):
        per_expert = slot_scr[pl.ds(c, n_e, stride=n_c), :]
        slot_ref[c * LANES:(c + 1) * LANES, :] = per_expert.T.astype(I32)
    off_ref[...] = off.astype(I32).reshape(n_e, n_c, LANES)

    s_lane = lax.broadcasted_iota(I32, (1, cap), 1).astype(F32)
    c_col = lax.broadcasted_iota(I32, (n_c, 1), 0).astype(F32)
    for e in range(n_e):
        r0 = e * n_c
        incl_e = incl[r0:r0 + n_c]
        off_e = off[r0:r0 + n_c, 0:1]
        tot_e = tot[r0:r0 + n_c]
        onehot = ((off_e <= s_lane) & (s_lane < off_e + tot_e)).astype(F32)
        counts = lax.dot_general(incl_e.astype(BF16), onehot.astype(BF16),
                                 (((0,), (0,)), ((), ())), preferred_element_type=F32)
        local = s_lane - jnp.sum(onehot * off_e, axis=0, keepdims=True)
        lane = jnp.sum((counts <= local).astype(F32), axis=0, keepdims=True)
        chunk = jnp.sum(onehot * c_col, axis=0, keepdims=True)
        idx_ref[e] = (chunk * LANES + lane).astype(I32)


def _routing(aff, cap):
    t = aff.shape[0]
    n_c = t // LANES
    a3 = aff.T.reshape(N_EXPERTS, n_c, LANES)
    return pl.pallas_call(
        functools.partial(_route_kernel, cap=cap),
        out_shape=[jax.ShapeDtypeStruct((N_EXPERTS, 1, cap), I32),
                   jax.ShapeDtypeStruct((t, N_EXPERTS), I32),
                   jax.ShapeDtypeStruct((N_EXPERTS, n_c, LANES), I32)],
        scratch_shapes=[pltpu.VMEM((N_EXPERTS * n_c, LANES), F32)],
        compiler_params=pltpu.CompilerParams(vmem_limit_bytes=_vmem_limit(48 << 20)),
        name="routing",
    )(a3)


def _gather_rows(table, row_ids):
    n_chunks = row_ids.shape[0]
    n_workers = SC_CORES * SC_SUBCORES
    per_worker = n_chunks // n_workers
    assert row_ids.shape[1] == GATHER_CHUNK and n_chunks % n_workers == 0
    mesh = plsc.VectorSubcoreMesh(core_axis_name="core", subcore_axis_name="subcore")

    @functools.partial(
        pl.kernel, mesh=mesh,
        out_type=jax.ShapeDtypeStruct((n_chunks * GATHER_CHUNK, LANES), I32),
        scratch_types=[pltpu.VMEM((per_worker, GATHER_CHUNK), I32),
                       pltpu.VMEM((2, GATHER_CHUNK, LANES), I32),
                       pltpu.SemaphoreType.DMA((2,)),
                       pltpu.SemaphoreType.DMA((2,))],
        name="gather_rows",
    )
    def gather(table_hbm, ids_hbm, out_hbm, ids_v, rows_v, gather_sem, store_sem):
        worker = lax.axis_index("subcore") * SC_CORES + lax.axis_index("core")
        first = worker * per_worker
        pltpu.sync_copy(ids_hbm.at[worker], ids_v)

        def fetch(j):
            return pltpu.make_async_copy(table_hbm.at[ids_v.at[j]], rows_v.at[j % 2],
                                         gather_sem.at[j % 2])

        def store(j):
            rows = pl.ds(pl.multiple_of((first + j) * GATHER_CHUNK, GATHER_CHUNK), GATHER_CHUNK)
            return pltpu.make_async_copy(rows_v.at[j % 2], out_hbm.at[rows], store_sem.at[j % 2])

        for j in range(per_worker):
            if j >= 2:
                store(j - 2).wait()
            fetch(j).start()
            if j >= 1:
                fetch(j - 1).wait()
                store(j - 1).start()
        fetch(per_worker - 1).wait()
        store(per_worker - 1).start()
        for j in range(max(per_worker - 2, 0), per_worker):
            store(j).wait()

    return gather(table, row_ids.reshape(n_workers, per_worker, GATHER_CHUNK))


def _packed_row_ids(idx, cap):
    tok = idx.reshape(-1, 1, cap // GATHER_CHUNK, GATHER_CHUNK)
    tile = jnp.arange(HQ_TILES, dtype=I32).reshape(1, HQ_TILES, 1, 1)
    ids = ((tok // SUBLANES) * HQ_TILES + tile) * SUBLANES + tok % SUBLANES
    return ids.reshape(-1, GATHER_CHUNK)


def _ffn_kernel(*refs, caps, n_f, row_chunk, first_expert, n_prior):
    n_g = len(caps)
    x_refs = refs[:n_g]
    wg_ref, wu_ref, wd_ref = refs[n_g:n_g + 3]
    y_refs = refs[n_g + 3 + n_prior:2 * n_g + 3 + n_prior]
    acc_ref = refs[2 * n_g + 3 + n_prior]
    e = pl.program_id(0)
    f = pl.program_id(1)

    @pl.when((e == 0) & (f == 0))
    def _():
        acc_ref[...] = jnp.zeros_like(acc_ref)

    def step(last):
        first = f == 0
        base = 0
        for x_ref, y_ref, cap in zip(x_refs, y_refs, caps):
            if last:
                aff = lax.bitcast_convert_type(x_ref[0, HQ_TILES - 1], F32)
                lane = lax.broadcasted_iota(I32, aff.shape, 1)
                gates = jnp.sum(jnp.where(lane == e + first_expert, aff, 0.0), axis=1,
                                keepdims=True)
                y_ref[0, cap:, :] = jnp.zeros((COMBINE_WINDOW, D_MODEL // 2), I32)
            for r in range(cap // row_chunk):
                rs = slice(r * row_chunk, (r + 1) * row_chunk)
                acc_rows = slice(base + r * row_chunk, base + (r + 1) * row_chunk)
                pairs = [_unpack_pair(x_ref[0, c, rs, :]) for c in range(HQ_TILES - 1)]
                x = jnp.concatenate([p[0] for p in pairs] + [p[1] for p in pairs], axis=1)
                gate_act = jnp.dot(x, wg_ref[0], preferred_element_type=F32)
                up = jnp.dot(x, wu_ref[0], preferred_element_type=F32)
                hid = (gate_act * jax.nn.sigmoid(gate_act)) * up
                part = jnp.dot(hid, wd_ref[0], preferred_element_type=F32)
                if last:
                    y = (part + acc_ref[acc_rows, :]) * gates[rs]
                    y_ref[0, rs, :] = _pack_pair(y[:, 0:D_MODEL // 2], y[:, D_MODEL // 2:])
                else:
                    acc_ref[acc_rows, :] = part + jnp.where(first, 0.0, acc_ref[acc_rows, :])
            base += cap

    @pl.when(f < n_f - 1)
    def _():
        step(False)

    @pl.when(f == n_f - 1)
    def _():
        step(True)


def _expert_ffn(xs_groups, w_gate, w_up, w_down, first_expert, prior_outputs):
    n_f = 4
    tf = D_EXPERT // n_f
    row_chunk = 512
    n_e = xs_groups[0].shape[0]
    e0 = first_expert
    caps = tuple(xs.shape[2] for xs in xs_groups)
    rows = sum(caps)
    est = (2 * HQ_TILES * rows * LANES * 4 + rows * D_MODEL * (4 + 2 * 2) + 2 * 3 * D_MODEL * tf * 4
           + row_chunk * (3 * tf + 2 * D_MODEL) * 4)
    x_specs = [pl.BlockSpec((1, HQ_TILES, cap, LANES), lambda e, f: (e, 0, 0, 0)) for cap in caps]
    y_shapes = [(N_EXPERTS, cap + COMBINE_WINDOW, D_MODEL // 2) for cap in caps]
    y_specs = [pl.BlockSpec((1,) + s[1:], lambda e, f: (e + e0, 0, 0)) for s in y_shapes]
    prior = list(prior_outputs or [])
    n_in = len(caps) + 3
    return pl.pallas_call(
        functools.partial(_ffn_kernel, caps=caps, n_f=n_f, row_chunk=row_chunk,
                          first_expert=e0, n_prior=len(prior)),
        grid=(n_e, n_f),
        in_specs=(x_specs + [pl.BlockSpec((1, D_MODEL, tf), lambda e, f: (e + e0, 0, f)),
                             pl.BlockSpec((1, D_MODEL, tf), lambda e, f: (e + e0, 0, f)),
                             pl.BlockSpec((1, tf, D_MODEL), lambda e, f: (e + e0, f, 0))]
                  + [pl.BlockSpec(memory_space=pl.ANY) for _ in prior]),
        out_specs=y_specs,
        out_shape=[jax.ShapeDtypeStruct(s, I32) for s in y_shapes],
        input_output_aliases={n_in + j: j for j in range(len(prior))},
        scratch_shapes=[pltpu.VMEM((rows, D_MODEL), F32)],
        compiler_params=pltpu.CompilerParams(
            dimension_semantics=("arbitrary", "arbitrary"),
            vmem_limit_bytes=_vmem_limit(est + (4 << 20))),
        name="expert_ffn",
    )(*xs_groups, w_gate, w_up, w_down, *prior)


def _combine_kernel(start_ref, nch_ref, wide_ref, x1_ref, mod_ref, g_ref, slot_ref, y_hbm, o_ref,
                    ybuf_ref, cols_ref, ffn_ref, sem, *, first_k, kblock):
    i = pl.program_id(0)
    tm = x1_ref.shape[0]
    cur = i % 2
    n_window_rows = N_EXPERTS * COMBINE_WINDOW

    def window_copy(e, src_row, buf):
        return pltpu.make_async_copy(
            y_hbm.at[e, pl.ds(pl.multiple_of(src_row, SUBLANES), COMBINE_WINDOW)],
            ybuf_ref.at[buf, e * COMBINE_WINDOW:(e + 1) * COMBINE_WINDOW], sem.at[buf])

    def chunk_copy(e, src_row, buf, dst_row):
        return pltpu.make_async_copy(
            y_hbm.at[e, pl.ds(pl.multiple_of(src_row, SUBLANES), COMBINE_CHUNK)],
            ybuf_ref.at[buf, pl.ds(pl.multiple_of(dst_row, COMBINE_CHUNK), COMBINE_CHUNK)],
            sem.at[buf])

    def fetch(tile, buf):
        @pl.when(wide_ref[tile] == 0)
        def _():
            for e in range(N_EXPERTS):
                window_copy(e, start_ref[tile, e], buf).start()

        @pl.when(wide_ref[tile] != 0)
        def _():
            pos = jnp.int32(0)
            for e in range(N_EXPERTS):
                start = start_ref[tile, e]
                nch = nch_ref[tile, e]

                def issue(c, carry, e=e, start=start, pos=pos):
                    chunk_copy(e, start + c * COMBINE_CHUNK, buf, pos + c * COMBINE_CHUNK).start()
                    return carry
                lax.fori_loop(0, nch, issue, 0)
                pos = pos + nch * COMBINE_CHUNK

    @pl.when(i == 0)
    def _():
        ybuf_ref[...] = jnp.zeros_like(ybuf_ref)
        fetch(0, 0)

    @pl.when(i + 1 < pl.num_programs(0))
    def _():
        fetch(i + 1, 1 - cur)

    def finish(ffn):
        out = x1_ref[...] + mod_ref[0][5:6] * ffn
        ms = jnp.mean(out * out, axis=-1, keepdims=True)
        o_ref[...] = (out * lax.rsqrt(ms + EPS)) * g_ref[...]

    def apply_selection(sel, k0, width):
        lo, hi = _unpack_pair(ybuf_ref[cur, pl.ds(k0, width), :])
        return jnp.concatenate([jnp.dot(sel, lo, preferred_element_type=F32),
                                jnp.dot(sel, hi, preferred_element_type=F32)], axis=1)

    @pl.when(wide_ref[i] == 0)
    def _():
        for e in range(N_EXPERTS):
            window_copy(e, jnp.int32(0), cur).wait()
        expert_lane = lax.broadcasted_iota(I32, (1, N_EXPERTS), 1)
        starts = jnp.zeros((1, N_EXPERTS), I32)
        for e in range(N_EXPERTS):
            starts = jnp.where(expert_lane == e, start_ref[i, e], starts)
        slots = slot_ref[...]
        rows_in_window = jnp.where(slots >= 0, (slots - starts).astype(F32), -1.0)
        window_of_lane = lax.broadcasted_iota(I32, (N_EXPERTS, n_window_rows), 1) // COMBINE_WINDOW
        spread = (window_of_lane == lax.broadcasted_iota(I32, (N_EXPERTS, n_window_rows), 0))
        target = jnp.dot(rows_in_window.astype(BF16), spread.astype(BF16),
                         preferred_element_type=F32)
        lane_row = (lax.broadcasted_iota(I32, (tm, n_window_rows), 1) % COMBINE_WINDOW).astype(F32)
        sel = jnp.where(target == lane_row, 1.0, 0.0)
        ffn = None
        for k0 in range(0, n_window_rows, kblock):
            part = apply_selection(sel[:, k0:k0 + kblock], k0, kblock)
            ffn = part if ffn is None else ffn + part
        finish(ffn)

    @pl.when(wide_ref[i] != 0)
    def _():
        pos = jnp.int32(0)
        begins = []
        for e in range(N_EXPERTS):
            begins.append(pos)
            pos = pos + nch_ref[i, e] * COMBINE_CHUNK
        begins.append(pos)

        def drain(c, carry):
            chunk_copy(0, jnp.int32(0), cur, jnp.int32(0)).wait()
            return carry
        lax.fori_loop(0, pos // COMBINE_CHUNK, drain, 0)

        slots = slot_ref[...]
        for e in range(N_EXPERTS):
            s_e = slots[:, e:e + 1]
            col = jnp.where(s_e >= 0, s_e + (begins[e] - start_ref[i, e]), -1)
            cols_ref[e] = jnp.broadcast_to(col, (tm, LANES))

        lane = lax.broadcasted_iota(I32, (tm, LANES), 1)

        def selection(k0, width):
            halves = []
            for h in range(width // LANES):
                c0 = k0 + h * LANES
                target = lane + c0
                e_lo = jnp.int32(0)
                e_hi = jnp.int32(0)
                for e in range(N_EXPERTS):
                    e_lo = e_lo + (begins[e + 1] <= c0).astype(I32)
                    e_hi = e_hi + (begins[e] < c0 + LANES).astype(I32)

                def mark(e, hit, target=target):
                    return jnp.where(cols_ref[e] == target, 1.0, hit)
                halves.append(lax.fori_loop(e_lo, e_hi, mark, jnp.zeros((tm, LANES), F32)))
            return jnp.concatenate(halves, axis=1)

        ffn_ref[...] = apply_selection(selection(0, first_k), 0, first_k)

        def kstep(kb, carry):
            k0 = pl.multiple_of(kb * kblock, kblock)
            ffn_ref[...] += apply_selection(selection(k0, kblock), k0, kblock)
            return carry
        lax.fori_loop(first_k // kblock, (pos + kblock - 1) // kblock, kstep, 0)
        finish(ffn_ref[...])


def _combine(x1, mod3, mod_row, g_final, slot_t, tile_start, tile_nch, tile_wide, y):
    t = x1.shape[0]
    tm = TOKEN_TILE
    first_k = 2 * tm + N_EXPERTS * COMBINE_CHUNK
    kblock = 256
    max_rows = N_EXPERTS * (tm + 2 * COMBINE_CHUNK)
    max_rows = -(-max_rows // kblock) * kblock
    row = lambda i, *_: (i, 0)
    grid_spec = pltpu.PrefetchScalarGridSpec(
        num_scalar_prefetch=3,
        grid=(t // tm,),
        in_specs=[pl.BlockSpec((tm, D_MODEL), row),
                  pl.BlockSpec((1, N_MOD, D_MODEL), lambda i, *_: (mod_row(i), 0, 0)),
                  pl.BlockSpec((1, D_MODEL), lambda i, *_: (0, 0)),
                  pl.BlockSpec((tm, N_EXPERTS), row),
                  pl.BlockSpec(memory_space=pl.ANY)],
        out_specs=pl.BlockSpec((tm, D_MODEL), row),
        scratch_shapes=[pltpu.VMEM((2, max_rows, D_MODEL // 2), I32),
                        pltpu.VMEM((N_EXPERTS, tm, LANES), I32),
                        pltpu.VMEM((tm, D_MODEL), F32),
                        pltpu.SemaphoreType.DMA((2,))],
    )
    return pl.pallas_call(
        functools.partial(_combine_kernel, first_k=first_k, kblock=kblock),
        grid_spec=grid_spec,
        out_shape=jax.ShapeDtypeStruct((t, D_MODEL), F32),
        compiler_params=pltpu.CompilerParams(
            dimension_semantics=("arbitrary",),
            vmem_limit_bytes=_vmem_limit(2 * max_rows * D_MODEL * 2 + 16 * tm * D_MODEL * 4)),
        name="combine",
    )(tile_start, tile_nch, tile_wide, x1, mod3, g_final, slot_t, y)


def _rope_tables(n):
    rows = n // GRID_W
    row = jnp.repeat(jnp.arange(rows, dtype=F32), GRID_W)
    col = jnp.tile(jnp.arange(GRID_W, dtype=F32), rows)
    inv = ROPE_THETA ** (-jnp.arange(ROPE_FREQS, dtype=F32) / ROPE_FREQS)
    ang_r = row[:, None] * inv
    ang_c = col[:, None] * inv
    zero = jnp.zeros_like(ang_r)
    cos = jnp.concatenate([jnp.cos(ang_r)] * 2 + [jnp.cos(ang_c)] * 2, axis=1)
    sin_a = jnp.concatenate([-jnp.sin(ang_r), zero, -jnp.sin(ang_c), zero], axis=1)
    sin_b = jnp.concatenate([zero, jnp.sin(ang_r), zero, jnp.sin(ang_c)], axis=1)
    return cos, sin_a, sin_b


def _token_group(x, mod3, mod_row, seq, weights, rope_tabs, ctx_kv):
    (norm_mix, w_in_bf, sink, pool_w, pool_scale, w_out_bf, norm_ffn, w_router,
     w_gate, w_up, w_down, norm_final) = weights
    b = x.shape[0]
    t = b * seq
    x2 = x.reshape(t, D_MODEL)
    q, k, v, p, *state = _in_projection(x2, mod3, mod_row(PROJ_TILE), norm_mix, w_in_bf,
                                        rope_tabs)
    if ctx_kv is None:
        attn = _context_attention(q, k, v, sink, seq)
    else:
        attn = _latent_attention(q, k, v, ctx_kv[0], ctx_kv[1], sink, seq)
    x1, h_packed, aff = _out_projection(attn, p, x2, mod3, mod_row(PROJ_TILE), norm_ffn,
                                        w_out_bf, w_router, pool_w, pool_scale, seq)

    cap = EC_FACTOR * t // N_EXPERTS
    idx, slot_t, off3 = _routing(aff, cap)
    table = h_packed.reshape(-1, LANES)
    per_range = N_EXPERTS // FFN_RANGES
    xs = []
    for r in range(FFN_RANGES):
        ids = _packed_row_ids(idx[r * per_range:(r + 1) * per_range], cap)
        xs.append(_gather_rows(table, ids).reshape(per_range, HQ_TILES, cap, LANES))

    chunks_per_tile = TOKEN_TILE // LANES
    tile_off = off3[:, ::chunks_per_tile, 0]
    tile_end = jnp.concatenate([tile_off[:, 1:], jnp.full((N_EXPERTS, 1), cap, I32)], axis=1)
    tile_start = (tile_off // SUBLANES) * SUBLANES
    tile_nch = jnp.where(tile_end > tile_off,
                         (tile_end - tile_start + COMBINE_CHUNK - 1) // COMBINE_CHUNK, 0)
    tile_wide = jnp.any(tile_end - tile_start > COMBINE_WINDOW, axis=0).astype(I32)

    def finish(y):
        out = _combine(x1, mod3, mod_row(TOKEN_TILE), norm_final, slot_t, tile_start.T,
                       tile_nch.T, tile_wide, y)
        return out.reshape(b, seq, D_MODEL)
    return xs, finish, state


def kernel(x_prompt, x_sample, c, cache_k, cache_v, c_ctx, w_ada, b_ada, norm_mix, w_in,
           sink_logits, pool_w, pool_scale, w_out, norm_ffn, w_router, w_gate, w_up, w_down,
           norm_final):
    n_b, seq, _ = x_prompt.shape
    n_db, n_lat, _ = x_sample.shape
    assert 1 + n_db <= MOD_ROWS and seq == TOKEN_TILE and n_lat % PROJ_TILE == 0
    assert (n_b * seq) % PROJ_TILE == 0

    cond = jnp.concatenate(
        [c_ctx[None, :], c, jnp.zeros((MOD_ROWS - 1 - n_db, D_MODEL), F32)], axis=0)
    mod3 = _modulation(cond, w_ada[0], b_ada[0]).reshape(MOD_ROWS, N_MOD, D_MODEL)

    w_router_bf = jnp.pad(w_router[0], ((0, 0), (0, LANES - N_EXPERTS))).astype(BF16)
    weights = (norm_mix[0][None, :], w_in[0].astype(BF16), sink_logits[0], pool_w[0].astype(BF16),
               pool_scale[0][None, :], w_out[0].astype(BF16), norm_ffn[0][None, :], w_router_bf,
               w_gate.reshape(w_gate.shape[1:]), w_up.reshape(w_up.shape[1:]),
               w_down.reshape(w_down.shape[1:]), norm_final[None, :])

    xs_p, finish_p, (k_p, v_p) = _token_group(
        x_prompt, mod3, lambda tile: (lambda i: 0), seq, weights, None, None)

    past = cache_k.shape[2]
    ck = cache_k[:, 0].reshape(n_db, past, KV_WIDTH)
    cv = cache_v[:, 0].reshape(n_db, past, KV_WIDTH)
    xs_l, finish_l, _ = _token_group(
        x_sample, mod3, lambda tile: (lambda i: 1 + i // (n_lat // tile)), n_lat, weights,
        _rope_tables(n_lat), (ck, cv))

    ys = None
    for r in range(FFN_RANGES):
        ys = _expert_ffn([xs_p[r], xs_l[r]], weights[8], weights[9], weights[10],
                         r * (N_EXPERTS // FFN_RANGES), ys)
    y_prompt = finish_p(ys[0])
    y_sample = finish_l(ys[1])

    state_k = k_p.reshape(n_b, 1, seq, N_KV_HEADS, HEAD_DIM)
    state_v = v_p.reshape(n_b, 1, seq, N_KV_HEADS, HEAD_DIM)
    return (y_prompt, y_sample, state_k, state_v)
```

```python
import functools

import jax
import jax.numpy as jnp
from jax import lax
from jax.experimental import pallas as pl
from jax.experimental.pallas import tpu as pltpu
from jax.experimental.pallas import tpu_sc as plsc

F32 = jnp.float32
BF16 = jnp.bfloat16
I32 = jnp.int32

D_MODEL = 2048
N_HEADS = 8
N_KV_HEADS = 2
HEAD_DIM = 128
Q_PER_KV = N_HEADS // N_KV_HEADS
ATTN_WIDTH = N_HEADS * HEAD_DIM
KV_WIDTH = N_KV_HEADS * HEAD_DIM
POOL_WIDTH = D_MODEL - ATTN_WIDTH
POOL_SIZES = (2, 4, 8, 16)
POOL_GROUP = POOL_WIDTH // len(POOL_SIZES)
IN_WIDTH = ATTN_WIDTH + 2 * KV_WIDTH + POOL_WIDTH
WINDOW = 128
BLOCK = 128
GRID_W = 64
ROPE_THETA = 10000.0
ROPE_FREQS = HEAD_DIM // 4
N_EXPERTS = 16
EC_FACTOR = 2
D_EXPERT = 1024
N_MOD = 6
EPS = 1e-6
NEG = -1e30
LOG2_E = 1.4426950408889634
ATTN_SCALE = HEAD_DIM ** -0.5 * LOG2_E

LANES = 128
SUBLANES = 8
BF16_ROWS = 16
VMEM_CAP = 64 * 1024 * 1024
SC_CORES = 2
SC_SUBCORES = 16

MOD_ROWS = 8
TOKEN_TILE = 256
PROJ_TILE = 512
SUB_TILE = 256
HQ_TILES = D_MODEL // 2 // LANES + 1
GATHER_CHUNK = 128
FFN_RANGES = 2
POOL_HALO = 8
COMBINE_CHUNK = BF16_ROWS
COMBINE_WINDOW = 64


def _vmem_limit(nbytes):
    return int(min(VMEM_CAP - (4 << 20), max(nbytes, 16 << 20)))


def _mod_kernel(c_ref, w_ref, b_ref, o_ref):
    c = c_ref[...]
    s = c * jax.nn.sigmoid(c)
    o_ref[...] = jnp.dot(s.astype(BF16), w_ref[...].astype(BF16),
                         preferred_element_type=F32) + b_ref[...]


def _modulation(cond, w_ada, b_ada):
    n = w_ada.shape[1]
    tn = 1024
    return pl.pallas_call(
        _mod_kernel,
        grid=(n // tn,),
        in_specs=[pl.BlockSpec((MOD_ROWS, D_MODEL), lambda j: (0, 0)),
                  pl.BlockSpec((D_MODEL, tn), lambda j: (0, j)),
                  pl.BlockSpec((1, tn), lambda j: (0, j))],
        out_specs=pl.BlockSpec((MOD_ROWS, tn), lambda j: (0, j)),
        out_shape=jax.ShapeDtypeStruct((MOD_ROWS, n), F32),
        compiler_params=pltpu.CompilerParams(
            vmem_limit_bytes=_vmem_limit(3 * D_MODEL * tn * 4)),
        name="modulation",
    )(cond, w_ada, b_ada.reshape(1, n))


def _norm_mod(x, g, shift, scale):
    ms = jnp.mean(x * x, axis=-1, keepdims=True)
    y = x * lax.rsqrt(ms + EPS)
    return (y * g) * (1.0 + scale) + shift


def _inproj_kernel(*refs, rope):
    if rope:
        x_ref, mod_ref, g_ref, w_ref, cos_ref, sa_ref, sb_ref, q_ref, k_ref, v_ref, p_ref = refs
    else:
        x_ref, mod_ref, g_ref, w_ref, q_ref, k_ref, v_ref, p_ref, ks_ref, vs_ref = refs
    mod = mod_ref[0]
    for s in range(x_ref.shape[0] // SUB_TILE):
        rows = slice(s * SUB_TILE, (s + 1) * SUB_TILE)
        h = _norm_mod(x_ref[rows, :], g_ref[...], mod[0:1], mod[1:2])
        u = jnp.dot(h.astype(BF16), w_ref[...], preferred_element_type=F32)

        def rot(xh, rows=rows):
            return (xh * cos_ref[rows, :] + pltpu.roll(xh, LANES - ROPE_FREQS, 1) * sa_ref[rows, :]
                    + pltpu.roll(xh, ROPE_FREQS, 1) * sb_ref[rows, :])

        for hd in range(N_HEADS):
            xh = u[:, hd * HEAD_DIM:(hd + 1) * HEAD_DIM] * ATTN_SCALE
            q_ref[rows, hd * HEAD_DIM:(hd + 1) * HEAD_DIM] = (rot(xh) if rope else xh).astype(BF16)
        for hd in range(N_KV_HEADS):
            lo = ATTN_WIDTH + hd * HEAD_DIM
            xh = u[:, lo:lo + HEAD_DIM]
            k_ref[rows, hd * HEAD_DIM:(hd + 1) * HEAD_DIM] = rot(xh) if rope else xh
        v_ref[rows, :] = u[:, ATTN_WIDTH + KV_WIDTH:ATTN_WIDTH + 2 * KV_WIDTH]
        p_ref[rows, :] = u[:, ATTN_WIDTH + 2 * KV_WIDTH:]
        if not rope:
            for hd in range(N_KV_HEADS):
                state_rows = pl.ds(s * SUB_TILE * N_KV_HEADS + hd, SUB_TILE, stride=N_KV_HEADS)
                lo = ATTN_WIDTH + hd * HEAD_DIM
                ks_ref[state_rows, :] = u[:, lo:lo + HEAD_DIM]
                vs_ref[state_rows, :] = u[:, lo + KV_WIDTH:lo + KV_WIDTH + HEAD_DIM]


def _in_projection(x2, mod3, mod_row, g, w_in_bf, rope_tabs):
    t = x2.shape[0]
    tm = PROJ_TILE
    rope = rope_tabs is not None
    row = lambda i: (i, 0)
    in_specs = [pl.BlockSpec((tm, D_MODEL), row),
                pl.BlockSpec((1, N_MOD, D_MODEL), lambda i: (mod_row(i), 0, 0)),
                pl.BlockSpec((1, D_MODEL), lambda i: (0, 0)),
                pl.BlockSpec((D_MODEL, IN_WIDTH), lambda i: (0, 0))]
    args = [x2, mod3, g, w_in_bf]
    if rope:
        n_seq = rope_tabs[0].shape[0]
        seq_blocks = n_seq // tm
        for tab in rope_tabs:
            in_specs.append(pl.BlockSpec((tm, HEAD_DIM), lambda i: (i % seq_blocks, 0)))
            args.append(tab)
    out_specs = [pl.BlockSpec((tm, ATTN_WIDTH), row),
                 pl.BlockSpec((tm, KV_WIDTH), row),
                 pl.BlockSpec((tm, KV_WIDTH), row),
                 pl.BlockSpec((tm, POOL_WIDTH), row)]
    out_shape = [jax.ShapeDtypeStruct((t, ATTN_WIDTH), BF16),
                 jax.ShapeDtypeStruct((t, KV_WIDTH), F32),
                 jax.ShapeDtypeStruct((t, KV_WIDTH), F32),
                 jax.ShapeDtypeStruct((t, POOL_WIDTH), F32)]
    if not rope:
        for _ in range(2):
            out_specs.append(pl.BlockSpec((tm * N_KV_HEADS, HEAD_DIM), row))
            out_shape.append(jax.ShapeDtypeStruct((t * N_KV_HEADS, HEAD_DIM), F32))
    return pl.pallas_call(
        functools.partial(_inproj_kernel, rope=rope),
        grid=(t // tm,),
        in_specs=in_specs,
        out_specs=out_specs,
        out_shape=out_shape,
        compiler_params=pltpu.CompilerParams(
            vmem_limit_bytes=_vmem_limit(2 * D_MODEL * IN_WIDTH * 2 + 24 * tm * D_MODEL * 4)),
        name="in_projection",
    )(*args)


def _softmax_pv(s_list, v_list, sink_col):
    m = sink_col
    for s in s_list:
        m = jnp.maximum(m, jnp.max(s, axis=-1, keepdims=True))
    denom = jnp.exp2(sink_col - m)
    out = None
    for s, v in zip(s_list, v_list):
        e = jnp.exp2(s - m)
        if v.shape[1] == HEAD_DIM:
            denom = denom + jnp.sum(e, axis=-1, keepdims=True)
        o = jnp.dot(e.astype(BF16), v, preferred_element_type=F32)
        out = o if out is None else out + o
    if out.shape[1] > HEAD_DIM:
        denom = denom + out[:, HEAD_DIM:HEAD_DIM + 1]
    return out[:, 0:HEAD_DIM] * (1.0 / denom)


def _with_ones(v):
    return jnp.concatenate([v.astype(BF16), jnp.ones(v.shape, BF16)], axis=1)


def _stack_heads(q, kv):
    return jnp.concatenate(
        [q[:, (kv * Q_PER_KV + g) * HEAD_DIM:(kv * Q_PER_KV + g + 1) * HEAD_DIM]
         for g in range(Q_PER_KV)], axis=0)


def _sink_column(sink_ref, kv, rows):
    r = lax.broadcasted_iota(I32, (Q_PER_KV * rows, 1), 0)
    col = jnp.zeros((Q_PER_KV * rows, 1), F32)
    for g in range(Q_PER_KV):
        col = jnp.where((r >= g * rows) & (r < (g + 1) * rows), sink_ref[kv * Q_PER_KV + g], col)
    return col * LOG2_E


def _qk(q, k):
    return lax.dot_general(q, k, (((1,), (1,)), ((), ())), preferred_element_type=F32)


def _ctx_attn_kernel(sink_ref, q_ref, k_ref, v_ref, o_ref):
    rows = q_ref.shape[0]
    q = q_ref[...]
    for kv in range(N_KV_HEADS):
        kh = k_ref[:, kv * HEAD_DIM:(kv + 1) * HEAD_DIM].astype(BF16)
        vh = v_ref[:, kv * HEAD_DIM:(kv + 1) * HEAD_DIM].astype(BF16)
        qs = _stack_heads(q, kv)
        s = _qk(qs, kh)
        o = _softmax_pv([s], [vh], _sink_column(sink_ref, kv, rows))
        for g in range(Q_PER_KV):
            hd = kv * Q_PER_KV + g
            o_ref[:, hd * HEAD_DIM:(hd + 1) * HEAD_DIM] = o[g * rows:(g + 1) * rows].astype(BF16)


def _context_attention(q, k, v, sink, seq):
    t = q.shape[0]
    row = lambda b: (b, 0)
    return pl.pallas_call(
        _ctx_attn_kernel,
        grid=(t // seq,),
        in_specs=[pl.BlockSpec(memory_space=pltpu.SMEM),
                  pl.BlockSpec((seq, ATTN_WIDTH), row),
                  pl.BlockSpec((seq, KV_WIDTH), row),
                  pl.BlockSpec((seq, KV_WIDTH), row)],
        out_specs=pl.BlockSpec((seq, ATTN_WIDTH), row),
        out_shape=jax.ShapeDtypeStruct((t, ATTN_WIDTH), BF16),
        name="context_attention",
    )(sink, q, k, v)


def _lat_attn_kernel(sink_ref, q_ref, k_ref, v_ref, ck_ref, cv_ref, o_ref, *, n_seq):
    i = pl.program_id(1)
    band = 3 * BLOCK
    start = pl.multiple_of(jnp.clip((i - 1) * BLOCK, 0, n_seq - band), BLOCK)
    rows = Q_PER_KV * BLOCK
    qpos = i * BLOCK + lax.broadcasted_iota(I32, (rows, band), 0) % BLOCK
    kpos = start + lax.broadcasted_iota(I32, (rows, band), 1)
    mask = jnp.abs(kpos - qpos) <= WINDOW
    q = q_ref[...]
    for kv in range(N_KV_HEADS):
        cols = slice(kv * HEAD_DIM, (kv + 1) * HEAD_DIM)
        kb = k_ref[pl.ds(start, band), cols].astype(BF16)
        vb = _with_ones(v_ref[pl.ds(start, band), cols])
        ck = ck_ref[0, :, cols].astype(BF16)
        cv = _with_ones(cv_ref[0, :, cols])
        qs = _stack_heads(q, kv)
        s_loc = jnp.where(mask, _qk(qs, kb), NEG)
        s_ctx = _qk(qs, ck)
        o = _softmax_pv([s_loc, s_ctx], [vb, cv], _sink_column(sink_ref, kv, BLOCK))
        for g in range(Q_PER_KV):
            hd = kv * Q_PER_KV + g
            o_ref[:, hd * HEAD_DIM:(hd + 1) * HEAD_DIM] = o[g * BLOCK:(g + 1) * BLOCK].astype(BF16)


def _latent_attention(q, k, v, ck, cv, sink, n_seq):
    t = q.shape[0]
    nb = n_seq // BLOCK
    past = ck.shape[1]
    return pl.pallas_call(
        functools.partial(_lat_attn_kernel, n_seq=n_seq),
        grid=(t // n_seq, nb),
        in_specs=[pl.BlockSpec(memory_space=pltpu.SMEM),
                  pl.BlockSpec((BLOCK, ATTN_WIDTH), lambda b, i: (b * nb + i, 0)),
                  pl.BlockSpec((n_seq, KV_WIDTH), lambda b, i: (b, 0)),
                  pl.BlockSpec((n_seq, KV_WIDTH), lambda b, i: (b, 0)),
                  pl.BlockSpec((1, past, KV_WIDTH), lambda b, i: (b, 0, 0)),
                  pl.BlockSpec((1, past, KV_WIDTH), lambda b, i: (b, 0, 0))],
        out_specs=pl.BlockSpec((BLOCK, ATTN_WIDTH), lambda b, i: (b * nb + i, 0)),
        out_shape=jax.ShapeDtypeStruct((t, ATTN_WIDTH), BF16),
        name="latent_attention",
    )(sink, q, k, v, ck, cv)


def _pool_group(p_ref, r0, seq, w_ref, s_ref, g):
    n = SUB_TILE
    rows = n + 2 * POOL_HALO
    static = isinstance(r0, int)
    t0 = r0 % seq
    t = t0 + lax.broadcasted_iota(I32, (n, 1), 0)
    has_top = t0 > 0
    has_bottom = t0 + n < seq
    zeros = jnp.zeros((POOL_HALO, POOL_GROUP), F32)
    w = POOL_SIZES[g]
    cols = slice(g * POOL_GROUP, (g + 1) * POOL_GROUP)
    pg = p_ref[pl.ds(r0, n), cols]
    if static:
        top = p_ref[r0 - POOL_HALO:r0, cols] if has_top else zeros
        bottom = p_ref[r0 + n:r0 + n + POOL_HALO, cols] if has_bottom else zeros
    else:
        top_row = pl.multiple_of(jnp.maximum(r0 - POOL_HALO, 0), POOL_HALO)
        bottom_row = pl.multiple_of(jnp.minimum(r0 + n, p_ref.shape[0] - POOL_HALO), POOL_HALO)
        top = jnp.where(has_top, p_ref[pl.ds(top_row, POOL_HALO), cols], 0.0)
        bottom = jnp.where(has_bottom, p_ref[pl.ds(bottom_row, POOL_HALO), cols], 0.0)
    x = jnp.concatenate([top, pg, bottom], axis=0)
    fwd = x
    span = 1
    while span < w // 2:
        fwd = fwd + pltpu.roll(fwd, rows - span, 0)
        span *= 2
    if (w // 2) % SUBLANES == 0:
        wsum = fwd[POOL_HALO - w // 2:POOL_HALO - w // 2 + n] + fwd[POOL_HALO:POOL_HALO + n]
    else:
        wsum = (fwd + pltpu.roll(fwd, w // 2, 0))[POOL_HALO:POOL_HALO + n]
    lo = jnp.maximum(t - w // 2, 0)
    hi = jnp.minimum(t + w - w // 2, seq)
    inv_cnt = 1.0 / (hi - lo).astype(F32)
    mixed = wsum * inv_cnt - pg
    y = jnp.dot(mixed.astype(BF16), w_ref[g], preferred_element_type=F32)
    return (y * s_ref[:, cols]).astype(BF16)


def _pack_pair(lo, hi):
    return lax.bitcast_convert_type(pltpu.pack_elementwise([lo, hi], packed_dtype=BF16), I32)


def _unpack_pair(words):
    lo = pltpu.unpack_elementwise(words, index=0, packed_dtype=BF16, unpacked_dtype=F32)
    hi = pltpu.unpack_elementwise(words, index=1, packed_dtype=BF16, unpacked_dtype=F32)
    return lo, hi


def _outproj_kernel(a_ref, p_ref, x_ref, mod_ref, g_ref, wo_ref, wr_ref, pw_ref, ps_ref,
                    x1_ref, h_ref, aff_ref, *, seq):
    mod = mod_ref[0]
    groups = SUB_TILE // SUBLANES
    half = D_MODEL // 2
    tm = x_ref.shape[0]
    steps_per_p_block = p_ref.shape[0] // tm
    for s in range(tm // SUB_TILE):
        rows = slice(s * SUB_TILE, (s + 1) * SUB_TILE)
        grp = slice(s * groups, (s + 1) * groups)
        r0 = s * SUB_TILE
        if steps_per_p_block > 1:
            r0 = pl.multiple_of((pl.program_id(0) % steps_per_p_block) * tm + r0, SUB_TILE)
        pooled = jnp.concatenate([_pool_group(p_ref, r0, seq, pw_ref, ps_ref, g)
                                  for g in range(len(POOL_SIZES))], axis=1)
        mix = (jnp.dot(a_ref[rows, :], wo_ref[0:ATTN_WIDTH, :], preferred_element_type=F32)
               + jnp.dot(pooled, wo_ref[ATTN_WIDTH:D_MODEL, :], preferred_element_type=F32))
        x1 = x_ref[rows, :] + mod[2:3] * mix
        x1_ref[rows, :] = x1
        h = _norm_mod(x1, g_ref[...], mod[3:4], mod[4:5])
        logits = jnp.dot(h.astype(BF16), wr_ref[...], preferred_element_type=F32)
        lane = lax.broadcasted_iota(I32, logits.shape, 1)
        logits = jnp.where(lane < N_EXPERTS, logits, -jnp.inf)
        m = jnp.max(logits, axis=-1, keepdims=True)
        e = jnp.exp(logits - m)
        aff = e / jnp.sum(e, axis=-1, keepdims=True)
        aff_ref[rows, :] = aff[:, 0:N_EXPERTS]
        for c in range(HQ_TILES - 1):
            words = _pack_pair(h[:, c * LANES:(c + 1) * LANES],
                               h[:, half + c * LANES:half + (c + 1) * LANES])
            h_ref[grp, c * SUBLANES:(c + 1) * SUBLANES, :] = (
                words.reshape(groups, SUBLANES, LANES))
        h_ref[grp, (HQ_TILES - 1) * SUBLANES:, :] = (
            lax.bitcast_convert_type(aff, I32).reshape(groups, SUBLANES, LANES))


def _out_projection(attn, p, x2, mod3, mod_row, g, w_out_bf, w_router, pool_w, pool_scale, seq):
    t = x2.shape[0]
    tm = PROJ_TILE
    row = lambda i: (i, 0)
    p_rows = max(tm, seq)
    steps_per_p_block = p_rows // tm
    return pl.pallas_call(
        functools.partial(_outproj_kernel, seq=seq),
        grid=(t // tm,),
        in_specs=[pl.BlockSpec((tm, ATTN_WIDTH), row),
                  pl.BlockSpec((p_rows, POOL_WIDTH), lambda i: (i // steps_per_p_block, 0)),
                  pl.BlockSpec((tm, D_MODEL), row),
                  pl.BlockSpec((1, N_MOD, D_MODEL), lambda i: (mod_row(i), 0, 0)),
                  pl.BlockSpec((1, D_MODEL), lambda i: (0, 0)),
                  pl.BlockSpec((D_MODEL, D_MODEL), lambda i: (0, 0)),
                  pl.BlockSpec((D_MODEL, LANES), lambda i: (0, 0)),
                  pl.BlockSpec((len(POOL_SIZES), POOL_GROUP, POOL_GROUP), lambda i: (0, 0, 0)),
                  pl.BlockSpec((1, POOL_WIDTH), lambda i: (0, 0))],
        out_specs=[pl.BlockSpec((tm, D_MODEL), row),
                   pl.BlockSpec((tm // SUBLANES, HQ_TILES * SUBLANES, LANES), lambda i: (i, 0, 0)),
                   pl.BlockSpec((tm, N_EXPERTS), row)],
        out_shape=[jax.ShapeDtypeStruct((t, D_MODEL), F32),
                   jax.ShapeDtypeStruct((t // SUBLANES, HQ_TILES * SUBLANES, LANES), I32),
                   jax.ShapeDtypeStruct((t, N_EXPERTS), F32)],
        compiler_params=pltpu.CompilerParams(
            vmem_limit_bytes=_vmem_limit(2 * D_MODEL * D_MODEL * 2 + 24 * tm * D_MODEL * 4)),
        name="out_projection",
    )(attn, p, x2, mod3, g, w_out_bf, w_router, pool_w, pool_scale)


def _route_kernel(a_ref, idx_ref, slot_ref, off_ref, slot_scr, *, cap):
    a = a_ref[...]
    n_e, n_c, _ = a.shape
    rows = n_e * n_c

    def enough(cand):
        cand_f = lax.bitcast_convert_type(cand, F32)
        cnt = jnp.sum(jnp.sum((a >= cand_f).astype(F32), axis=1, keepdims=True),
                      axis=2, keepdims=True)
        return cnt >= cap

    def two_bits(it, thr):
        low = 28 - 2 * it
        for setting in (1, 2, 3):
            cand = thr | jnp.left_shift(jnp.int32(setting), low)
            best = jnp.where(enough(cand), cand, thr if setting == 1 else best)
        return best

    top = jnp.full((n_e, 1, 1), 1 << 30, I32)
    thr = jnp.where(enough(top), top, 0)
    thr = lax.fori_loop(0, 15, two_bits, thr)
    thr_f = lax.bitcast_convert_type(thr, F32)
    gt = (a > thr_f).astype(F32).reshape(rows, LANES)
    eq = (a == thr_f).astype(F32).reshape(rows, LANES)

    li = lax.broadcasted_iota(I32, (LANES, LANES), 0)
    lj = lax.broadcasted_iota(I32, (LANES, LANES), 1)
    upper_incl = (li <= lj).astype(BF16)
    ci = lax.broadcasted_iota(I32, (n_c, n_c), 0)
    cj = lax.broadcasted_iota(I32, (n_c, n_c), 1)
    before = (cj < ci).astype(BF16)
    whole = jnp.ones((n_c, n_c), BF16)

    def over_chunks(mat, col):
        wide = jnp.broadcast_to(col, (rows, LANES)).astype(BF16)
        side_by_side = jnp.concatenate([wide[e * n_c:(e + 1) * n_c] for e in range(n_e)], axis=1)
        res = jnp.dot(mat, side_by_side, preferred_element_type=F32)
        return jnp.concatenate([res[:, e * LANES:(e + 1) * LANES] for e in range(n_e)], axis=0)

    def prefix(x):
        incl = jnp.dot(x.astype(BF16), upper_incl, preferred_element_type=F32)
        tot = incl[:, LANES - 1:LANES]
        return incl, tot, over_chunks(before, tot)

    n_gt = over_chunks(whole, jnp.sum(gt, axis=1, keepdims=True))
    need = cap - n_gt
    incl_eq, _, off_eq = prefix(eq)
    rank_eq = off_eq + incl_eq - eq
    sel = jnp.where((eq > 0) & (rank_eq < need), 1.0, gt)
    incl, tot, off = prefix(sel)
    slot = off + incl - sel
    slot_scr[...] = jnp.where(sel > 0, slot, -1.0)
    for c in range(n_c):
        per_expert = slot_scr[pl.ds(c, n_e, stride=n_c), :]
        slot_ref[c * LANES:(c + 1) * LANES, :] = per_expert.T.astype(I32)
    off_ref[...] = off.astype(I32).reshape(n_e, n_c, LANES)

    s_lane = lax.broadcasted_iota(I32, (1, cap), 1).astype(F32)
    c_col = lax.broadcasted_iota(I32, (n_c, 1), 0).astype(F32)
    for e in range(n_e):
        r0 = e * n_c
        incl_e = incl[r0:r0 + n_c]
        off_e = off[r0:r0 + n_c, 0:1]
        tot_e = tot[r0:r0 + n_c]
        onehot = ((off_e <= s_lane) & (s_lane < off_e + tot_e)).astype(F32)
        counts = lax.dot_general(incl_e.astype(BF16), onehot.astype(BF16),
                                 (((0,), (0,)), ((), ())), preferred_element_type=F32)
        local = s_lane - jnp.sum(onehot * off_e, axis=0, keepdims=True)
        lane = jnp.sum((counts <= local).astype(F32), axis=0, keepdims=True)
        chunk = jnp.sum(onehot * c_col, axis=0, keepdims=True)
        idx_ref[e] = (chunk * LANES + lane).astype(I32)


def _routing(aff, cap):
    t = aff.shape[0]
    n_c = t // LANES
    a3 = aff.T.reshape(N_EXPERTS, n_c, LANES)
    return pl.pallas_call(
        functools.partial(_route_kernel, cap=cap),
        out_shape=[jax.ShapeDtypeStruct((N_EXPERTS, 1, cap), I32),
                   jax.ShapeDtypeStruct((t, N_EXPERTS), I32),
                   jax.ShapeDtypeStruct((N_EXPERTS, n_c, LANES), I32)],
        scratch_shapes=[pltpu.VMEM((N_EXPERTS * n_c, LANES), F32)],
        compiler_params=pltpu.CompilerParams(vmem_limit_bytes=_vmem_limit(48 << 20)),
        name="routing",
    )(a3)


def _gather_rows(table, row_ids):
    n_chunks = row_ids.shape[0]
    n_workers = SC_CORES * SC_SUBCORES
    per_worker = n_chunks // n_workers
    assert row_ids.shape[1] == GATHER_CHUNK and n_chunks % n_workers == 0
    mesh = plsc.VectorSubcoreMesh(core_axis_name="core", subcore_axis_name="subcore")

    @functools.partial(
        pl.kernel, mesh=mesh,
        out_type=jax.ShapeDtypeStruct((n_chunks * GATHER_CHUNK, LANES), I32),
        scratch_types=[pltpu.VMEM((per_worker, GATHER_CHUNK), I32),
                       pltpu.VMEM((2, GATHER_CHUNK, LANES), I32),
                       pltpu.SemaphoreType.DMA((2,)),
                       pltpu.SemaphoreType.DMA((2,))],
        name="gather_rows",
    )
    def gather(table_hbm, ids_hbm, out_hbm, ids_v, rows_v, gather_sem, store_sem):
        worker = lax.axis_index("subcore") * SC_CORES + lax.axis_index("core")
        first = worker * per_worker
        pltpu.sync_copy(ids_hbm.at[worker], ids_v)

        def fetch(j):
            return pltpu.make_async_copy(table_hbm.at[ids_v.at[j]], rows_v.at[j % 2],
                                         gather_sem.at[j % 2])

        def store(j):
            rows = pl.ds(pl.multiple_of((first + j) * GATHER_CHUNK, GATHER_CHUNK), GATHER_CHUNK)
            return pltpu.make_async_copy(rows_v.at[j % 2], out_hbm.at[rows], store_sem.at[j % 2])

        for j in range(per_worker):
            if j >= 2:
                store(j - 2).wait()
            fetch(j).start()
            if j >= 1:
                fetch(j - 1).wait()
                store(j - 1).start()
        fetch(per_worker - 1).wait()
        store(per_worker - 1).start()
        for j in range(max(per_worker - 2, 0), per_worker):
            store(j).wait()

    return gather(table, row_ids.reshape(n_workers, per_worker, GATHER_CHUNK))


def _packed_row_ids(idx, cap):
    tok = idx.reshape(-1, 1, cap // GATHER_CHUNK, GATHER_CHUNK)
    tile = jnp.arange(HQ_TILES, dtype=I32).reshape(1, HQ_TILES, 1, 1)
    ids = ((tok // SUBLANES) * HQ_TILES + tile) * SUBLANES + tok % SUBLANES
    return ids.reshape(-1, GATHER_CHUNK)


def _ffn_kernel(*refs, caps, n_f, row_chunk, first_expert, n_prior):
    n_g = len(caps)
    x_refs = refs[:n_g]
    wg_ref, wu_ref, wd_ref = refs[n_g:n_g + 3]
    y_refs = refs[n_g + 3 + n_prior:2 * n_g + 3 + n_prior]
    acc_ref = refs[2 * n_g + 3 + n_prior]
    e = pl.program_id(0)
    f = pl.program_id(1)

    @pl.when((e == 0) & (f == 0))
    def _():
        acc_ref[...] = jnp.zeros_like(acc_ref)

    def step(last):
        first = f == 0
        base = 0
        for x_ref, y_ref, cap in zip(x_refs, y_refs, caps):
            if last:
                aff = lax.bitcast_convert_type(x_ref[0, HQ_TILES - 1], F32)
                lane = lax.broadcasted_iota(I32, aff.shape, 1)
                gates = jnp.sum(jnp.where(lane == e + first_expert, aff, 0.0), axis=1,
                                keepdims=True)
                y_ref[0, cap:, :] = jnp.zeros((COMBINE_WINDOW, D_MODEL // 2), I32)
            for r in range(cap // row_chunk):
                rs = slice(r * row_chunk, (r + 1) * row_chunk)
                acc_rows = slice(base + r * row_chunk, base + (r + 1) * row_chunk)
                pairs = [_unpack_pair(x_ref[0, c, rs, :]) for c in range(HQ_TILES - 1)]
                x = jnp.concatenate([p[0] for p in pairs] + [p[1] for p in pairs], axis=1)
                gate_act = jnp.dot(x, wg_ref[0], preferred_element_type=F32)
                up = jnp.dot(x, wu_ref[0], preferred_element_type=F32)
                hid = (gate_act * jax.nn.sigmoid(gate_act)) * up
                part = jnp.dot(hid, wd_ref[0], preferred_element_type=F32)
                if last:
                    y = (part + acc_ref[acc_rows, :]) * gates[rs]
                    y_ref[0, rs, :] = _pack_pair(y[:, 0:D_MODEL // 2], y[:, D_MODEL // 2:])
                else:
                    acc_ref[acc_rows, :] = part + jnp.where(first, 0.0, acc_ref[acc_rows, :])
            base += cap

    @pl.when(f < n_f - 1)
    def _():
        step(False)

    @pl.when(f == n_f - 1)
    def _():
        step(True)


def _expert_ffn(xs_groups, w_gate, w_up, w_down, first_expert, prior_outputs):
    n_f = 4
    tf = D_EXPERT // n_f
    row_chunk = 512
    n_e = xs_groups[0].shape[0]
    e0 = first_expert
    caps = tuple(xs.shape[2] for xs in xs_groups)
    rows = sum(caps)
    est = (2 * HQ_TILES * rows * LANES * 4 + rows * D_MODEL * (4 + 2 * 2) + 2 * 3 * D_MODEL * tf * 4
           + row_chunk * (3 * tf + 2 * D_MODEL) * 4)
    x_specs = [pl.BlockSpec((1, HQ_TILES, cap, LANES), lambda e, f: (e, 0, 0, 0)) for cap in caps]
    y_shapes = [(N_EXPERTS, cap + COMBINE_WINDOW, D_MODEL // 2) for cap in caps]
    y_specs = [pl.BlockSpec((1,) + s[1:], lambda e, f: (e + e0, 0, 0)) for s in y_shapes]
    prior = list(prior_outputs or [])
    n_in = len(caps) + 3
    return pl.pallas_call(
        functools.partial(_ffn_kernel, caps=caps, n_f=n_f, row_chunk=row_chunk,
                          first_expert=e0, n_prior=len(prior)),
        grid=(n_e, n_f),
        in_specs=(x_specs + [pl.BlockSpec((1, D_MODEL, tf), lambda e, f: (e + e0, 0, f)),
                             pl.BlockSpec((1, D_MODEL, tf), lambda e, f: (e + e0, 0, f)),
                             pl.BlockSpec((1, tf, D_MODEL), lambda e, f: (e + e0, f, 0))]
                  + [pl.BlockSpec(memory_space=pl.ANY) for _ in prior]),
        out_specs=y_specs,
        out_shape=[jax.ShapeDtypeStruct(s, I32) for s in y_shapes],
        input_output_aliases={n_in + j: j for j in range(len(prior))},
        scratch_shapes=[pltpu.VMEM((rows, D_MODEL), F32)],
        compiler_params=pltpu.CompilerParams(
            dimension_semantics=("arbitrary", "arbitrary"),
            vmem_limit_bytes=_vmem_limit(est + (4 << 20))),
        name="expert_ffn",
    )(*xs_groups, w_gate, w_up, w_down, *prior)


def _combine_kernel(start_ref, nch_ref, wide_ref, x1_ref, mod_ref, g_ref, slot_ref, y_hbm, o_ref,
                    ybuf_ref, cols_ref, ffn_ref, sem, *, first_k, kblock):
    i = pl.program_id(0)
    tm = x1_ref.shape[0]
    cur = i % 2
    n_window_rows = N_EXPERTS * COMBINE_WINDOW

    def window_copy(e, src_row, buf):
        return pltpu.make_async_copy(
            y_hbm.at[e, pl.ds(pl.multiple_of(src_row, SUBLANES), COMBINE_WINDOW)],
            ybuf_ref.at[buf, e * COMBINE_WINDOW:(e + 1) * COMBINE_WINDOW], sem.at[buf])

    def chunk_copy(e, src_row, buf, dst_row):
        return pltpu.make_async_copy(
            y_hbm.at[e, pl.ds(pl.multiple_of(src_row, SUBLANES), COMBINE_CHUNK)],
            ybuf_ref.at[buf, pl.ds(pl.multiple_of(dst_row, COMBINE_CHUNK), COMBINE_CHUNK)],
            sem.at[buf])

    def fetch(tile, buf):
        @pl.when(wide_ref[tile] == 0)
        def _():
            for e in range(N_EXPERTS):
                window_copy(e, start_ref[tile, e], buf).start()

        @pl.when(wide_ref[tile] != 0)
        def _():
            pos = jnp.int32(0)
            for e in range(N_EXPERTS):
                start = start_ref[tile, e]
                nch = nch_ref[tile, e]

                def issue(c, carry, e=e, start=start, pos=pos):
                    chunk_copy(e, start + c * COMBINE_CHUNK, buf, pos + c * COMBINE_CHUNK).start()
                    return carry
                lax.fori_loop(0, nch, issue, 0)
                pos = pos + nch * COMBINE_CHUNK

    @pl.when(i == 0)
    def _():
        ybuf_ref[...] = jnp.zeros_like(ybuf_ref)
        fetch(0, 0)

    @pl.when(i + 1 < pl.num_programs(0))
    def _():
        fetch(i + 1, 1 - cur)

    def finish(ffn):
        out = x1_ref[...] + mod_ref[0][5:6] * ffn
        ms = jnp.mean(out * out, axis=-1, keepdims=True)
        o_ref[...] = (out * lax.rsqrt(ms + EPS)) * g_ref[...]

    def apply_selection(sel, k0, width):
        lo, hi = _unpack_pair(ybuf_ref[cur, pl.ds(k0, width), :])
        return jnp.concatenate([jnp.dot(sel, lo, preferred_element_type=F32),
                                jnp.dot(sel, hi, preferred_element_type=F32)], axis=1)

    @pl.when(wide_ref[i] == 0)
    def _():
        for e in range(N_EXPERTS):
            window_copy(e, jnp.int32(0), cur).wait()
        expert_lane = lax.broadcasted_iota(I32, (1, N_EXPERTS), 1)
        starts = jnp.zeros((1, N_EXPERTS), I32)
        for e in range(N_EXPERTS):
            starts = jnp.where(expert_lane == e, start_ref[i, e], starts)
        slots = slot_ref[...]
        rows_in_window = jnp.where(slots >= 0, (slots - starts).astype(F32), -1.0)
        window_of_lane = lax.broadcasted_iota(I32, (N_EXPERTS, n_window_rows), 1) // COMBINE_WINDOW
        spread = (window_of_lane == lax.broadcasted_iota(I32, (N_EXPERTS, n_window_rows), 0))
        target = jnp.dot(rows_in_window.astype(BF16), spread.astype(BF16),
                         preferred_element_type=F32)
        lane_row = (lax.broadcasted_iota(I32, (tm, n_window_rows), 1) % COMBINE_WINDOW).astype(F32)
        sel = jnp.where(target == lane_row, 1.0, 0.0)
        ffn = None
        for k0 in range(0, n_window_rows, kblock):
            part = apply_selection(sel[:, k0:k0 + kblock], k0, kblock)
            ffn = part if ffn is None else ffn + part
        finish(ffn)

    @pl.when(wide_ref[i] != 0)
    def _():
        pos = jnp.int32(0)
        begins = []
        for e in range(N_EXPERTS):
            begins.append(pos)
            pos = pos + nch_ref[i, e] * COMBINE_CHUNK
        begins.append(pos)

        def drain(c, carry):
            chunk_copy(0, jnp.int32(0), cur, jnp.int32(0)).wait()
            return carry
        lax.fori_loop(0, pos // COMBINE_CHUNK, drain, 0)

        slots = slot_ref[...]
        for e in range(N_EXPERTS):
            s_e = slots[:, e:e + 1]
            col = jnp.where(s_e >= 0, s_e + (begins[e] - start_ref[i, e]), -1)
            cols_ref[e] = jnp.broadcast_to(col, (tm, LANES))

        lane = lax.broadcasted_iota(I32, (tm, LANES), 1)

        def selection(k0, width):
            halves = []
            for h in range(width // LANES):
                c0 = k0 + h * LANES
                target = lane + c0
                e_lo = jnp.int32(0)
                e_hi = jnp.int32(0)
                for e in range(N_EXPERTS):
                    e_lo = e_lo + (begins[e + 1] <= c0).astype(I32)
                    e_hi = e_hi + (begins[e] < c0 + LANES).astype(I32)

                def mark(e, hit, target=target):
                    return jnp.where(cols_ref[e] == target, 1.0, hit)
                halves.append(lax.fori_loop(e_lo, e_hi, mark, jnp.zeros((tm, LANES), F32)))
            return jnp.concatenate(halves, axis=1)

        ffn_ref[...] = apply_selection(selection(0, first_k), 0, first_k)

        def kstep(kb, carry):
            k0 = pl.multiple_of(kb * kblock, kblock)
            ffn_ref[...] += apply_selection(selection(k0, kblock), k0, kblock)
            return carry
        lax.fori_loop(first_k // kblock, (pos + kblock - 1) // kblock, kstep, 0)
        finish(ffn_ref[...])


def _combine(x1, mod3, mod_row, g_final, slot_t, tile_start, tile_nch, tile_wide, y):
    t = x1.shape[0]
    tm = TOKEN_TILE
    first_k = 2 * tm + N_EXPERTS * COMBINE_CHUNK
    kblock = 256
    max_rows = N_EXPERTS * (tm + 2 * COMBINE_CHUNK)
    max_rows = -(-max_rows // kblock) * kblock
    row = lambda i, *_: (i, 0)
    grid_spec = pltpu.PrefetchScalarGridSpec(
        num_scalar_prefetch=3,
        grid=(t // tm,),
        in_specs=[pl.BlockSpec((tm, D_MODEL), row),
                  pl.BlockSpec((1, N_MOD, D_MODEL), lambda i, *_: (mod_row(i), 0, 0)),
                  pl.BlockSpec((1, D_MODEL), lambda i, *_: (0, 0)),
                  pl.BlockSpec((tm, N_EXPERTS), row),
                  pl.BlockSpec(memory_space=pl.ANY)],
        out_specs=pl.BlockSpec((tm, D_MODEL), row),
        scratch_shapes=[pltpu.VMEM((2, max_rows, D_MODEL // 2), I32),
                        pltpu.VMEM((N_EXPERTS, tm, LANES), I32),
                        pltpu.VMEM((tm, D_MODEL), F32),
                        pltpu.SemaphoreType.DMA((2,))],
    )
    return pl.pallas_call(
        functools.partial(_combine_kernel, first_k=first_k, kblock=kblock),
        grid_spec=grid_spec,
        out_shape=jax.ShapeDtypeStruct((t, D_MODEL), F32),
        compiler_params=pltpu.CompilerParams(
            dimension_semantics=("arbitrary",),
            vmem_limit_bytes=_vmem_limit(2 * max_rows * D_MODEL * 2 + 16 * tm * D_MODEL * 4)),
        name="combine",
    )(tile_start, tile_nch, tile_wide, x1, mod3, g_final, slot_t, y)


def _rope_tables(n):
    rows = n // GRID_W
    row = jnp.repeat(jnp.arange(rows, dtype=F32), GRID_W)
    col = jnp.tile(jnp.arange(GRID_W, dtype=F32), rows)
    inv = ROPE_THETA ** (-jnp.arange(ROPE_FREQS, dtype=F32) / ROPE_FREQS)
    ang_r = row[:, None] * inv
    ang_c = col[:, None] * inv
    zero = jnp.zeros_like(ang_r)
    cos = jnp.concatenate([jnp.cos(ang_r)] * 2 + [jnp.cos(ang_c)] * 2, axis=1)
    sin_a = jnp.concatenate([-jnp.sin(ang_r), zero, -jnp.sin(ang_c), zero], axis=1)
    sin_b = jnp.concatenate([zero, jnp.sin(ang_r), zero, jnp.sin(ang_c)], axis=1)
    return cos, sin_a, sin_b


def _token_group(x, mod3, mod_row, seq, weights, rope_tabs, ctx_kv):
    (norm_mix, w_in_bf, sink, pool_w, pool_scale, w_out_bf, norm_ffn, w_router,
     w_gate, w_up, w_down, norm_final) = weights
    b = x.shape[0]
    t = b * seq
    x2 = x.reshape(t, D_MODEL)
    q, k, v, p, *state = _in_projection(x2, mod3, mod_row(PROJ_TILE), norm_mix, w_in_bf,
                                        rope_tabs)
    if ctx_kv is None:
        attn = _context_attention(q, k, v, sink, seq)
    else:
        attn = _latent_attention(q, k, v, ctx_kv[0], ctx_kv[1], sink, seq)
    x1, h_packed, aff = _out_projection(attn, p, x2, mod3, mod_row(PROJ_TILE), norm_ffn,
                                        w_out_bf, w_router, pool_w, pool_scale, seq)

    cap = EC_FACTOR * t // N_EXPERTS
    idx, slot_t, off3 = _routing(aff, cap)
    table = h_packed.reshape(-1, LANES)
    per_range = N_EXPERTS // FFN_RANGES
    xs = []
    for r in range(FFN_RANGES):
        ids = _packed_row_ids(idx[r * per_range:(r + 1) * per_range], cap)
        xs.append(_gather_rows(table, ids).reshape(per_range, HQ_TILES, cap, LANES))

    chunks_per_tile = TOKEN_TILE // LANES
    tile_off = off3[:, ::chunks_per_tile, 0]
    tile_end = jnp.concatenate([tile_off[:, 1:], jnp.full((N_EXPERTS, 1), cap, I32)], axis=1)
    tile_start = (tile_off // SUBLANES) * SUBLANES
    tile_nch = jnp.where(tile_end > tile_off,
                         (tile_end - tile_start + COMBINE_CHUNK - 1) // COMBINE_CHUNK, 0)
    tile_wide = jnp.any(tile_end - tile_start > COMBINE_WINDOW, axis=0).astype(I32)

    def finish(y):
        out = _combine(x1, mod3, mod_row(TOKEN_TILE), norm_final, slot_t, tile_start.T,
                       tile_nch.T, tile_wide, y)
        return out.reshape(b, seq, D_MODEL)
    return xs, finish, state


def kernel(x_prompt, x_sample, c, cache_k, cache_v, c_ctx, w_ada, b_ada, norm_mix, w_in,
           sink_logits, pool_w, pool_scale, w_out, norm_ffn, w_router, w_gate, w_up, w_down,
           norm_final):
    n_b, seq, _ = x_prompt.shape
    n_db, n_lat, _ = x_sample.shape
    assert 1 + n_db <= MOD_ROWS and seq == TOKEN_TILE and n_lat % PROJ_TILE == 0
    assert (n_b * seq) % PROJ_TILE == 0

    cond = jnp.concatenate(
        [c_ctx[None, :], c, jnp.zeros((MOD_ROWS - 1 - n_db, D_MODEL), F32)], axis=0)
    mod3 = _modulation(cond, w_ada[0], b_ada[0]).reshape(MOD_ROWS, N_MOD, D_MODEL)

    w_router_bf = jnp.pad(w_router[0], ((0, 0), (0, LANES - N_EXPERTS))).astype(BF16)
    weights = (norm_mix[0][None, :], w_in[0].astype(BF16), sink_logits[0], pool_w[0].astype(BF16),
               pool_scale[0][None, :], w_out[0].astype(BF16), norm_ffn[0][None, :], w_router_bf,
               w_gate.reshape(w_gate.shape[1:]), w_up.reshape(w_up.shape[1:]),
               w_down.reshape(w_down.shape[1:]), norm_final[None, :])

    xs_p, finish_p, (k_p, v_p) = _token_group(
        x_prompt, mod3, lambda tile: (lambda i: 0), seq, weights, None, None)

    past = cache_k.shape[2]
    ck = cache_k[:, 0].reshape(n_db, past, KV_WIDTH)
    cv = cache_v[:, 0].reshape(n_db, past, KV_WIDTH)
    xs_l, finish_l, _ = _token_group(
        x_sample, mod3, lambda tile: (lambda i: 1 + i // (n_lat // tile)), n_lat, weights,
        _rope_tables(n_lat), (ck, cv))

    ys = None
    for r in range(FFN_RANGES):
        ys = _expert_ffn([xs_p[r], xs_l[r]], weights[8], weights[9], weights[10],
                         r * (N_EXPERTS // FFN_RANGES), ys)
    y_prompt = finish_p(ys[0])
    y_sample = finish_l(ys[1])

    state_k = k_p.reshape(n_b, 1, seq, N_KV_HEADS, HEAD_DIM)
    state_v = v_p.reshape(n_b, 1, seq, N_KV_HEADS, HEAD_DIM)
    return (y_prompt, y_sample, state_k, state_v)
```

```python
import functools

import jax
import jax.numpy as jnp
from jax import lax
from jax.experimental import pallas as pl
from jax.experimental.pallas import tpu as pltpu
from jax.experimental.pallas import tpu_sc as plsc

F32 = jnp.float32
BF16 = jnp.bfloat16
I32 = jnp.int32

D_MODEL = 2048
N_HEADS = 8
N_KV_HEADS = 2
HEAD_DIM = 128
Q_PER_KV = N_HEADS // N_KV_HEADS
ATTN_WIDTH = N_HEADS * HEAD_DIM
KV_WIDTH = N_KV_HEADS * HEAD_DIM
POOL_WIDTH = D_MODEL - ATTN_WIDTH
POOL_SIZES = (2, 4, 8, 16)
POOL_GROUP = POOL_WIDTH // len(POOL_SIZES)
IN_WIDTH = ATTN_WIDTH + 2 * KV_WIDTH + POOL_WIDTH
WINDOW = 128
BLOCK = 128
GRID_W = 64
ROPE_THETA = 10000.0
ROPE_FREQS = HEAD_DIM // 4
N_EXPERTS = 16
EC_FACTOR = 2
D_EXPERT = 1024
N_MOD = 6
EPS = 1e-6
NEG = -1e30
LOG2_E = 1.4426950408889634
ATTN_SCALE = HEAD_DIM ** -0.5 * LOG2_E

LANES = 128
SUBLANES = 8
BF16_ROWS = 16
VMEM_CAP = 64 * 1024 * 1024
SC_CORES = 2
SC_SUBCORES = 16

MOD_ROWS = 8
TOKEN_TILE = 256
PROJ_TILE = 512
SUB_TILE = 256
HQ_TILES = D_MODEL // 2 // LANES + 1
GATHER_CHUNK = 128
FFN_RANGES = 2
POOL_HALO = 8
COMBINE_CHUNK = BF16_ROWS
COMBINE_WINDOW = 64


def _vmem_limit(nbytes):
    return int(min(VMEM_CAP - (4 << 20), max(nbytes, 16 << 20)))


def _mod_kernel(c_ref, w_ref, b_ref, o_ref):
    c = c_ref[...]
    s = c * jax.nn.sigmoid(c)
    o_ref[...] = jnp.dot(s.astype(BF16), w_ref[...].astype(BF16),
                         preferred_element_type=F32) + b_ref[...]


def _modulation(cond, w_ada, b_ada):
    n = w_ada.shape[1]
    tn = 1024
    return pl.pallas_call(
        _mod_kernel,
        grid=(n // tn,),
        in_specs=[pl.BlockSpec((MOD_ROWS, D_MODEL), lambda j: (0, 0)),
                  pl.BlockSpec((D_MODEL, tn), lambda j: (0, j)),
                  pl.BlockSpec((1, tn), lambda j: (0, j))],
        out_specs=pl.BlockSpec((MOD_ROWS, tn), lambda j: (0, j)),
        out_shape=jax.ShapeDtypeStruct((MOD_ROWS, n), F32),
        compiler_params=pltpu.CompilerParams(
            vmem_limit_bytes=_vmem_limit(3 * D_MODEL * tn * 4)),
        name="modulation",
    )(cond, w_ada, b_ada.reshape(1, n))


def _norm_mod(x, g, shift, scale):
    ms = jnp.mean(x * x, axis=-1, keepdims=True)
    y = x * lax.rsqrt(ms + EPS)
    return (y * g) * (1.0 + scale) + shift


def _inproj_kernel(*refs, rope):
    if rope:
        x_ref, mod_ref, g_ref, w_ref, cos_ref, sa_ref, sb_ref, q_ref, k_ref, v_ref, p_ref = refs
    else:
        x_ref, mod_ref, g_ref, w_ref, q_ref, k_ref, v_ref, p_ref, ks_ref, vs_ref = refs
    mod = mod_ref[0]
    for s in range(x_ref.shape[0] // SUB_TILE):
        rows = slice(s * SUB_TILE, (s + 1) * SUB_TILE)
        h = _norm_mod(x_ref[rows, :], g_ref[...], mod[0:1], mod[1:2])
        u = jnp.dot(h.astype(BF16), w_ref[...], preferred_element_type=F32)

        def rot(xh, rows=rows):
            return (xh * cos_ref[rows, :] + pltpu.roll(xh, LANES - ROPE_FREQS, 1) * sa_ref[rows, :]
                    + pltpu.roll(xh, ROPE_FREQS, 1) * sb_ref[rows, :])

        for hd in range(N_HEADS):
            xh = u[:, hd * HEAD_DIM:(hd + 1) * HEAD_DIM] * ATTN_SCALE
            q_ref[rows, hd * HEAD_DIM:(hd + 1) * HEAD_DIM] = (rot(xh) if rope else xh).astype(BF16)
        for hd in range(N_KV_HEADS):
            lo = ATTN_WIDTH + hd * HEAD_DIM
            xh = u[:, lo:lo + HEAD_DIM]
            k_ref[rows, hd * HEAD_DIM:(hd + 1) * HEAD_DIM] = rot(xh) if rope else xh
        v_ref[rows, :] = u[:, ATTN_WIDTH + KV_WIDTH:ATTN_WIDTH + 2 * KV_WIDTH]
        p_ref[rows, :] = u[:, ATTN_WIDTH + 2 * KV_WIDTH:]
        if not rope:
            for hd in range(N_KV_HEADS):
                state_rows = pl.ds(s * SUB_TILE * N_KV_HEADS + hd, SUB_TILE, stride=N_KV_HEADS)
                lo = ATTN_WIDTH + hd * HEAD_DIM
                ks_ref[state_rows, :] = u[:, lo:lo + HEAD_DIM]
                vs_ref[state_rows, :] = u[:, lo + KV_WIDTH:lo + KV_WIDTH + HEAD_DIM]


def _in_projection(x2, mod3, mod_row, g, w_in_bf, rope_tabs):
    t = x2.shape[0]
    tm = PROJ_TILE
    rope = rope_tabs is not None
    row = lambda i: (i, 0)
    in_specs = [pl.BlockSpec((tm, D_MODEL), row),
                pl.BlockSpec((1, N_MOD, D_MODEL), lambda i: (mod_row(i), 0, 0)),
                pl.BlockSpec((1, D_MODEL), lambda i: (0, 0)),
                pl.BlockSpec((D_MODEL, IN_WIDTH), lambda i: (0, 0))]
    args = [x2, mod3, g, w_in_bf]
    if rope:
        n_seq = rope_tabs[0].shape[0]
        seq_blocks = n_seq // tm
        for tab in rope_tabs:
            in_specs.append(pl.BlockSpec((tm, HEAD_DIM), lambda i: (i % seq_blocks, 0)))
            args.append(tab)
    out_specs = [pl.BlockSpec((tm, ATTN_WIDTH), row),
                 pl.BlockSpec((tm, KV_WIDTH), row),
                 pl.BlockSpec((tm, KV_WIDTH), row),
                 pl.BlockSpec((tm, POOL_WIDTH), row)]
    out_shape = [jax.ShapeDtypeStruct((t, ATTN_WIDTH), BF16),
                 jax.ShapeDtypeStruct((t, KV_WIDTH), F32),
                 jax.ShapeDtypeStruct((t, KV_WIDTH), F32),
                 jax.ShapeDtypeStruct((t, POOL_WIDTH), F32)]
    if not rope:
        for _ in range(2):
            out_specs.append(pl.BlockSpec((tm * N_KV_HEADS, HEAD_DIM), row))
            out_shape.append(jax.ShapeDtypeStruct((t * N_KV_HEADS, HEAD_DIM), F32))
    return pl.pallas_call(
        functools.partial(_inproj_kernel, rope=rope),
        grid=(t // tm,),
        in_specs=in_specs,
        out_specs=out_specs,
        out_shape=out_shape,
        compiler_params=pltpu.CompilerParams(
            vmem_limit_bytes=_vmem_limit(2 * D_MODEL * IN_WIDTH * 2 + 24 * tm * D_MODEL * 4)),
        name="in_projection",
    )(*args)


def _softmax_pv(s_list, v_list, sink_col):
    m = sink_col
    for s in s_list:
        m = jnp.maximum(m, jnp.max(s, axis=-1, keepdims=True))
    denom = jnp.exp2(sink_col - m)
    out = None
    for s, v in zip(s_list, v_list):
        e = jnp.exp2(s - m)
        if v.shape[1] == HEAD_DIM:
            denom = denom + jnp.sum(e, axis=-1, keepdims=True)
        o = jnp.dot(e.astype(BF16), v, preferred_element_type=F32)
        out = o if out is None else out + o
    if out.shape[1] > HEAD_DIM:
        denom = denom + out[:, HEAD_DIM:HEAD_DIM + 1]
    return out[:, 0:HEAD_DIM] * (1.0 / denom)


def _with_ones(v):
    return jnp.concatenate([v.astype(BF16), jnp.ones(v.shape, BF16)], axis=1)


def _stack_heads(q, kv):
    return jnp.concatenate(
        [q[:, (kv * Q_PER_KV + g) * HEAD_DIM:(kv * Q_PER_KV + g + 1) * HEAD_DIM]
         for g in range(Q_PER_KV)], axis=0)


def _sink_column(sink_ref, kv, rows):
    r = lax.broadcasted_iota(I32, (Q_PER_KV * rows, 1), 0)
    col = jnp.zeros((Q_PER_KV * rows, 1), F32)
    for g in range(Q_PER_KV):
        col = jnp.where((r >= g * rows) & (r < (g + 1) * rows), sink_ref[kv * Q_PER_KV + g], col)
    return col * LOG2_E


def _qk(q, k):
    return lax.dot_general(q, k, (((1,), (1,)), ((), ())), preferred_element_type=F32)


def _ctx_attn_kernel(sink_ref, q_ref, k_ref, v_ref, o_ref):
    rows = q_ref.shape[0]
    q = q_ref[...]
    for kv in range(N_KV_HEADS):
        kh = k_ref[:, kv * HEAD_DIM:(kv + 1) * HEAD_DIM].astype(BF16)
        vh = v_ref[:, kv * HEAD_DIM:(kv + 1) * HEAD_DIM].astype(BF16)
        qs = _stack_heads(q, kv)
        s = _qk(qs, kh)
        o = _softmax_pv([s], [vh], _sink_column(sink_ref, kv, rows))
        for g in range(Q_PER_KV):
            hd = kv * Q_PER_KV + g
            o_ref[:, hd * HEAD_DIM:(hd + 1) * HEAD_DIM] = o[g * rows:(g + 1) * rows].astype(BF16)


def _context_attention(q, k, v, sink, seq):
    t = q.shape[0]
    row = lambda b: (b, 0)
    return pl.pallas_call(
        _ctx_attn_kernel,
        grid=(t // seq,),
        in_specs=[pl.BlockSpec(memory_space=pltpu.SMEM),
                  pl.BlockSpec((seq, ATTN_WIDTH), row),
                  pl.BlockSpec((seq, KV_WIDTH), row),
                  pl.BlockSpec((seq, KV_WIDTH), row)],
        out_specs=pl.BlockSpec((seq, ATTN_WIDTH), row),
        out_shape=jax.ShapeDtypeStruct((t, ATTN_WIDTH), BF16),
        name="context_attention",
    )(sink, q, k, v)


def _lat_attn_kernel(sink_ref, q_ref, k_ref, v_ref, ck_ref, cv_ref, o_ref, *, n_seq):
    i = pl.program_id(1)
    band = 3 * BLOCK
    start = pl.multiple_of(jnp.clip((i - 1) * BLOCK, 0, n_seq - band), BLOCK)
    rows = Q_PER_KV * BLOCK
    qpos = i * BLOCK + lax.broadcasted_iota(I32, (rows, band), 0) % BLOCK
    kpos = start + lax.broadcasted_iota(I32, (rows, band), 1)
    mask = jnp.abs(kpos - qpos) <= WINDOW
    q = q_ref[...]
    for kv in range(N_KV_HEADS):
        cols = slice(kv * HEAD_DIM, (kv + 1) * HEAD_DIM)
        kb = k_ref[pl.ds(start, band), cols].astype(BF16)
        vb = _with_ones(v_ref[pl.ds(start, band), cols])
        ck = ck_ref[0, :, cols].astype(BF16)
        cv = _with_ones(cv_ref[0, :, cols])
        qs = _stack_heads(q, kv)
        s_loc = jnp.where(mask, _qk(qs, kb), NEG)
        s_ctx = _qk(qs, ck)
        o = _softmax_pv([s_loc, s_ctx], [vb, cv], _sink_column(sink_ref, kv, BLOCK))
        for g in range(Q_PER_KV):
            hd = kv * Q_PER_KV + g
            o_ref[:, hd * HEAD_DIM:(hd + 1) * HEAD_DIM] = o[g * BLOCK:(g + 1) * BLOCK].astype(BF16)


def _latent_attention(q, k, v, ck, cv, sink, n_seq):
    t = q.shape[0]
    nb = n_seq // BLOCK
    past = ck.shape[1]
    return pl.pallas_call(
        functools.partial(_lat_attn_kernel, n_seq=n_seq),
        grid=(t // n_seq, nb),
        in_specs=[pl.BlockSpec(memory_space=pltpu.SMEM),
                  pl.BlockSpec((BLOCK, ATTN_WIDTH), lambda b, i: (b * nb + i, 0)),
                  pl.BlockSpec((n_seq, KV_WIDTH), lambda b, i: (b, 0)),
                  pl.BlockSpec((n_seq, KV_WIDTH), lambda b, i: (b, 0)),
                  pl.BlockSpec((1, past, KV_WIDTH), lambda b, i: (b, 0, 0)),
                  pl.BlockSpec((1, past, KV_WIDTH), lambda b, i: (b, 0, 0))],
        out_specs=pl.BlockSpec((BLOCK, ATTN_WIDTH), lambda b, i: (b * nb + i, 0)),
        out_shape=jax.ShapeDtypeStruct((t, ATTN_WIDTH), BF16),
        name="latent_attention",
    )(sink, q, k, v, ck, cv)


def _pool_group(p_ref, r0, seq, w_ref, s_ref, g):
    n = SUB_TILE
    rows = n + 2 * POOL_HALO
    static = isinstance(r0, int)
    t0 = r0 % seq
    t = t0 + lax.broadcasted_iota(I32, (n, 1), 0)
    has_top = t0 > 0
    has_bottom = t0 + n < seq
    zeros = jnp.zeros((POOL_HALO, POOL_GROUP), F32)
    w = POOL_SIZES[g]
    cols = slice(g * POOL_GROUP, (g + 1) * POOL_GROUP)
    pg = p_ref[pl.ds(r0, n), cols]
    if static:
        top = p_ref[r0 - POOL_HALO:r0, cols] if has_top else zeros
        bottom = p_ref[r0 + n:r0 + n + POOL_HALO, cols] if has_bottom else zeros
    else:
        top_row = pl.multiple_of(jnp.maximum(r0 - POOL_HALO, 0), POOL_HALO)
        bottom_row = pl.multiple_of(jnp.minimum(r0 + n, p_ref.shape[0] - POOL_HALO), POOL_HALO)
        top = jnp.where(has_top, p_ref[pl.ds(top_row, POOL_HALO), cols], 0.0)
        bottom = jnp.where(has_bottom, p_ref[pl.ds(bottom_row, POOL_HALO), cols], 0.0)
    x = jnp.concatenate([top, pg, bottom], axis=0)
    fwd = x
    span = 1
    while span < w // 2:
        fwd = fwd + pltpu.roll(fwd, rows - span, 0)
        span *= 2
    if (w // 2) % SUBLANES == 0:
        wsum = fwd[POOL_HALO - w // 2:POOL_HALO - w // 2 + n] + fwd[POOL_HALO:POOL_HALO + n]
    else:
        wsum = (fwd + pltpu.roll(fwd, w // 2, 0))[POOL_HALO:POOL_HALO + n]
    lo = jnp.maximum(t - w // 2, 0)
    hi = jnp.minimum(t + w - w // 2, seq)
    inv_cnt = 1.0 / (hi - lo).astype(F32)
    mixed = wsum * inv_cnt - pg
    y = jnp.dot(mixed.astype(BF16), w_ref[g], preferred_element_type=F32)
    return (y * s_ref[:, cols]).astype(BF16)


def _pack_pair(lo, hi):
    return lax.bitcast_convert_type(pltpu.pack_elementwise([lo, hi], packed_dtype=BF16), I32)


def _unpack_pair(words):
    lo = pltpu.unpack_elementwise(words, index=0, packed_dtype=BF16, unpacked_dtype=F32)
    hi = pltpu.unpack_elementwise(words, index=1, packed_dtype=BF16, unpacked_dtype=F32)
    return lo, hi


def _outproj_kernel(a_ref, p_ref, x_ref, mod_ref, g_ref, wo_ref, wr_ref, pw_ref, ps_ref,
                    x1_ref, h_ref, aff_ref, *, seq):
    mod = mod_ref[0]
    groups = SUB_TILE // SUBLANES
    half = D_MODEL // 2
    tm = x_ref.shape[0]
    steps_per_p_block = p_ref.shape[0] // tm
    for s in range(tm // SUB_TILE):
        rows = slice(s * SUB_TILE, (s + 1) * SUB_TILE)
        grp = slice(s * groups, (s + 1) * groups)
        r0 = s * SUB_TILE
        if steps_per_p_block > 1:
            r0 = pl.multiple_of((pl.program_id(0) % steps_per_p_block) * tm + r0, SUB_TILE)
        pooled = jnp.concatenate([_pool_group(p_ref, r0, seq, pw_ref, ps_ref, g)
                                  for g in range(len(POOL_SIZES))], axis=1)
        mix = (jnp.dot(a_ref[rows, :], wo_ref[0:ATTN_WIDTH, :], preferred_element_type=F32)
               + jnp.dot(pooled, wo_ref[ATTN_WIDTH:D_MODEL, :], preferred_element_type=F32))
        x1 = x_ref[rows, :] + mod[2:3] * mix
        x1_ref[rows, :] = x1
        h = _norm_mod(x1, g_ref[...], mod[3:4], mod[4:5])
        logits = jnp.dot(h.astype(BF16), wr_ref[...], preferred_element_type=F32)
        lane = lax.broadcasted_iota(I32, logits.shape, 1)
        logits = jnp.where(lane < N_EXPERTS, logits, -jnp.inf)
        m = jnp.max(logits, axis=-1, keepdims=True)
        e = jnp.exp(logits - m)
        aff = e / jnp.sum(e, axis=-1, keepdims=True)
        aff_ref[rows, :] = aff[:, 0:N_EXPERTS]
        for c in range(HQ_TILES - 1):
            words = _pack_pair(h[:, c * LANES:(c + 1) * LANES],
                               h[:, half + c * LANES:half + (c + 1) * LANES])
            h_ref[grp, c * SUBLANES:(c + 1) * SUBLANES, :] = (
                words.reshape(groups, SUBLANES, LANES))
        h_ref[grp, (HQ_TILES - 1) * SUBLANES:, :] = (
            lax.bitcast_convert_type(aff, I32).reshape(groups, SUBLANES, LANES))


def _out_projection(attn, p, x2, mod3, mod_row, g, w_out_bf, w_router, pool_w, pool_scale, seq):
    t = x2.shape[0]
    tm = PROJ_TILE
    row = lambda i: (i, 0)
    p_rows = max(tm, seq)
    steps_per_p_block = p_rows // tm
    return pl.pallas_call(
        functools.partial(_outproj_kernel, seq=seq),
        grid=(t // tm,),
        in_specs=[pl.BlockSpec((tm, ATTN_WIDTH), row),
                  pl.BlockSpec((p_rows, POOL_WIDTH), lambda i: (i // steps_per_p_block, 0)),
                  pl.BlockSpec((tm, D_MODEL), row),
                  pl.BlockSpec((1, N_MOD, D_MODEL), lambda i: (mod_row(i), 0, 0)),
                  pl.BlockSpec((1, D_MODEL), lambda i: (0, 0)),
                  pl.BlockSpec((D_MODEL, D_MODEL), lambda i: (0, 0)),
                  pl.BlockSpec((D_MODEL, LANES), lambda i: (0, 0)),
                  pl.BlockSpec((len(POOL_SIZES), POOL_GROUP, POOL_GROUP), lambda i: (0, 0, 0)),
                  pl.BlockSpec((1, POOL_WIDTH), lambda i: (0, 0))],
        out_specs=[pl.BlockSpec((tm, D_MODEL), row),
                   pl.BlockSpec((tm // SUBLANES, HQ_TILES * SUBLANES, LANES), lambda i: (i, 0, 0)),
                   pl.BlockSpec((tm, N_EXPERTS), row)],
        out_shape=[jax.ShapeDtypeStruct((t, D_MODEL), F32),
                   jax.ShapeDtypeStruct((t // SUBLANES, HQ_TILES * SUBLANES, LANES), I32),
                   jax.ShapeDtypeStruct((t, N_EXPERTS), F32)],
        compiler_params=pltpu.CompilerParams(
            vmem_limit_bytes=_vmem_limit(2 * D_MODEL * D_MODEL * 2 + 24 * tm * D_MODEL * 4)),
        name="out_projection",
    )(attn, p, x2, mod3, g, w_out_bf, w_router, pool_w, pool_scale)


def _route_kernel(a_ref, idx_ref, slot_ref, off_ref, slot_scr, *, cap):
    a = a_ref[...]
    n_e, n_c, _ = a.shape
    rows = n_e * n_c

    def enough(cand):
        cand_f = lax.bitcast_convert_type(cand, F32)
        cnt = jnp.sum(jnp.sum((a >= cand_f).astype(F32), axis=1, keepdims=True),
                      axis=2, keepdims=True)
        return cnt >= cap

    def two_bits(it, thr):
        low = 28 - 2 * it
        for setting in (1, 2, 3):
            cand = thr | jnp.left_shift(jnp.int32(setting), low)
            best = jnp.where(enough(cand), cand, thr if setting == 1 else best)
        return best

    top = jnp.full((n_e, 1, 1), 1 << 30, I32)
    thr = jnp.where(enough(top), top, 0)
    thr = lax.fori_loop(0, 15, two_bits, thr)
    thr_f = lax.bitcast_convert_type(thr, F32)
    gt = (a > thr_f).astype(F32).reshape(rows, LANES)
    eq = (a == thr_f).astype(F32).reshape(rows, LANES)

    li = lax.broadcasted_iota(I32, (LANES, LANES), 0)
    lj = lax.broadcasted_iota(I32, (LANES, LANES), 1)
    upper_incl = (li <= lj).astype(BF16)
    ci = lax.broadcasted_iota(I32, (n_c, n_c), 0)
    cj = lax.broadcasted_iota(I32, (n_c, n_c), 1)
    before = (cj < ci).astype(BF16)
    whole = jnp.ones((n_c, n_c), BF16)

    def over_chunks(mat, col):
        wide = jnp.broadcast_to(col, (rows, LANES)).astype(BF16)
        side_by_side = jnp.concatenate([wide[e * n_c:(e + 1) * n_c] for e in range(n_e)], axis=1)
        res = jnp.dot(mat, side_by_side, preferred_element_type=F32)
        return jnp.concatenate([res[:, e * LANES:(e + 1) * LANES] for e in range(n_e)], axis=0)

    def prefix(x):
        incl = jnp.dot(x.astype(BF16), upper_incl, preferred_element_type=F32)
        tot = incl[:, LANES - 1:LANES]
        return incl, tot, over_chunks(before, tot)

    n_gt = over_chunks(whole, jnp.sum(gt, axis=1, keepdims=True))
    need = cap - n_gt
    incl_eq, _, off_eq = prefix(eq)
    rank_eq = off_eq + incl_eq - eq
    sel = jnp.where((eq > 0) & (rank_eq < need), 1.0, gt)
    incl, tot, off = prefix(sel)
    slot = off + incl - sel
    slot_scr[...] = jnp.where(sel > 0, slot, -1.0)
    for c in range(n_c):
        per_expert = slot_scr[pl.ds(c, n_e, stride=n_c), :]
        slot_ref[c * LANES:(c + 1) * LANES, :] = per_expert.T.astype(I32)
    off_ref[...] = off.astype(I32).reshape(n_e, n_c, LANES)

    s_lane = lax.broadcasted_iota(I32, (1, cap), 1).astype(F32)
    c_col = lax.broadcasted_iota(I32, (n_c, 1), 0).astype(F32)
    for e in range(n_e):
        r0 = e * n_c
        incl_e = incl[r0:r0 + n_c]
        off_e = off[r0:r0 + n_c, 0:1]
        tot_e = tot[r0:r0 + n_c]
        onehot = ((off_e <= s_lane) & (s_lane < off_e + tot_e)).astype(F32)
        counts = lax.dot_general(incl_e.astype(BF16), onehot.astype(BF16),
                                 (((0,), (0,)), ((), ())), preferred_element_type=F32)
        local = s_lane - jnp.sum(onehot * off_e, axis=0, keepdims=True)
        lane = jnp.sum((counts <= local).astype(F32), axis=0, keepdims=True)
        chunk = jnp.sum(onehot * c_col, axis=0, keepdims=True)
        idx_ref[e] = (chunk * LANES + lane).astype(I32)


def _routing(aff, cap):
    t = aff.shape[0]
    n_c = t // LANES
    a3 = aff.T.reshape(N_EXPERTS, n_c, LANES)
    return pl.pallas_call(
        functools.partial(_route_kernel, cap=cap),
        out_shape=[jax.ShapeDtypeStruct((N_EXPERTS, 1, cap), I32),
                   jax.ShapeDtypeStruct((t, N_EXPERTS), I32),
                   jax.ShapeDtypeStruct((N_EXPERTS, n_c, LANES), I32)],
        scratch_shapes=[pltpu.VMEM((N_EXPERTS * n_c, LANES), F32)],
        compiler_params=pltpu.CompilerParams(vmem_limit_bytes=_vmem_limit(48 << 20)),
        name="routing",
    )(a3)


def _gather_rows(table, row_ids):
    n_chunks = row_ids.shape[0]
    n_workers = SC_CORES * SC_SUBCORES
    per_worker = n_chunks // n_workers
    assert row_ids.shape[1] == GATHER_CHUNK and n_chunks % n_workers == 0
    mesh = plsc.VectorSubcoreMesh(core_axis_name="core", subcore_axis_name="subcore")

    @functools.partial(
        pl.kernel, mesh=mesh,
        out_type=jax.ShapeDtypeStruct((n_chunks * GATHER_CHUNK, LANES), I32),
        scratch_types=[pltpu.VMEM((per_worker, GATHER_CHUNK), I32),
                       pltpu.VMEM((2, GATHER_CHUNK, LANES), I32),
                       pltpu.SemaphoreType.DMA((2,)),
                       pltpu.SemaphoreType.DMA((2,))],
        name="gather_rows",
    )
    def gather(table_hbm, ids_hbm, out_hbm, ids_v, rows_v, gather_sem, store_sem):
        worker = lax.axis_index("subcore") * SC_CORES + lax.axis_index("core")
        first = worker * per_worker
        pltpu.sync_copy(ids_hbm.at[worker], ids_v)

        def fetch(j):
            return pltpu.make_async_copy(table_hbm.at[ids_v.at[j]], rows_v.at[j % 2],
                                         gather_sem.at[j % 2])

        def store(j):
            rows = pl.ds(pl.multiple_of((first + j) * GATHER_CHUNK, GATHER_CHUNK), GATHER_CHUNK)
            return pltpu.make_async_copy(rows_v.at[j % 2], out_hbm.at[rows], store_sem.at[j % 2])

        for j in range(per_worker):
            if j >= 2:
                store(j - 2).wait()
            fetch(j).start()
            if j >= 1:
                fetch(j - 1).wait()
                store(j - 1).start()
        fetch(per_worker - 1).wait()
        store(per_worker - 1).start()
        for j in range(max(per_worker - 2, 0), per_worker):
            store(j).wait()

    return gather(table, row_ids.reshape(n_workers, per_worker, GATHER_CHUNK))


def _packed_row_ids(idx, cap):
    tok = idx.reshape(-1, 1, cap // GATHER_CHUNK, GATHER_CHUNK)
    tile = jnp.arange(HQ_TILES, dtype=I32).reshape(1, HQ_TILES, 1, 1)
    ids = ((tok // SUBLANES) * HQ_TILES + tile) * SUBLANES + tok % SUBLANES
    return ids.reshape(-1, GATHER_CHUNK)


def _ffn_kernel(*refs, caps, n_f, row_chunk, first_expert, n_prior):
    n_g = len(caps)
    x_refs = refs[:n_g]
    wg_ref, wu_ref, wd_ref = refs[n_g:n_g + 3]
    y_refs = refs[n_g + 3 + n_prior:2 * n_g + 3 + n_prior]
    acc_ref = refs[2 * n_g + 3 + n_prior]
    e = pl.program_id(0)
    f = pl.program_id(1)

    @pl.when((e == 0) & (f == 0))
    def _():
        acc_ref[...] = jnp.zeros_like(acc_ref)

    def step(last):
        first = f == 0
        base = 0
        for x_ref, y_ref, cap in zip(x_refs, y_refs, caps):
            if last:
                aff = lax.bitcast_convert_type(x_ref[0, HQ_TILES - 1], F32)
                lane = lax.broadcasted_iota(I32, aff.shape, 1)
                gates = jnp.sum(jnp.where(lane == e + first_expert, aff, 0.0), axis=1,
                                keepdims=True)
                y_ref[0, cap:, :] = jnp.zeros((COMBINE_WINDOW, D_MODEL // 2), I32)
            for r in range(cap // row_chunk):
                rs = slice(r * row_chunk, (r + 1) * row_chunk)
                acc_rows = slice(base + r * row_chunk, base + (r + 1) * row_chunk)
                pairs = [_unpack_pair(x_ref[0, c, rs, :]) for c in range(HQ_TILES - 1)]
                x = jnp.concatenate([p[0] for p in pairs] + [p[1] for p in pairs], axis=1)
                gate_act = jnp.dot(x, wg_ref[0], preferred_element_type=F32)
                up = jnp.dot(x, wu_ref[0], preferred_element_type=F32)
                hid = (gate_act * jax.nn.sigmoid(gate_act)) * up
                part = jnp.dot(hid, wd_ref[0], preferred_element_type=F32)
                if last:
                    y = (part + acc_ref[acc_rows, :]) * gates[rs]
                    y_ref[0, rs, :] = _pack_pair(y[:, 0:D_MODEL // 2], y[:, D_MODEL // 2:])
                else:
                    acc_ref[acc_rows, :] = part + jnp.where(first, 0.0, acc_ref[acc_rows, :])
            base += cap

    @pl.when(f < n_f - 1)
    def _():
        step(False)

    @pl.when(f == n_f - 1)
    def _():
        step(True)


def _expert_ffn(xs_groups, w_gate, w_up, w_down, first_expert, prior_outputs):
    n_f = 4
    tf = D_EXPERT // n_f
    row_chunk = 512
    n_e = xs_groups[0].shape[0]
    e0 = first_expert
    caps = tuple(xs.shape[2] for xs in xs_groups)
    rows = sum(caps)
    est = (2 * HQ_TILES * rows * LANES * 4 + rows * D_MODEL * (4 + 2 * 2) + 2 * 3 * D_MODEL * tf * 4
           + row_chunk * (3 * tf + 2 * D_MODEL) * 4)
    x_specs = [pl.BlockSpec((1, HQ_TILES, cap, LANES), lambda e, f: (e, 0, 0, 0)) for cap in caps]
    y_shapes = [(N_EXPERTS, cap + COMBINE_WINDOW, D_MODEL // 2) for cap in caps]
    y_specs = [pl.BlockSpec((1,) + s[1:], lambda e, f: (e + e0, 0, 0)) for s in y_shapes]
    prior = list(prior_outputs or [])
    n_in = len(caps) + 3
    return pl.pallas_call(
        functools.partial(_ffn_kernel, caps=caps, n_f=n_f, row_chunk=row_chunk,
                          first_expert=e0, n_prior=len(prior)),
        grid=(n_e, n_f),
        in_specs=(x_specs + [pl.BlockSpec((1, D_MODEL, tf), lambda e, f: (e + e0, 0, f)),
                             pl.BlockSpec((1, D_MODEL, tf), lambda e, f: (e + e0, 0, f)),
                             pl.BlockSpec((1, tf, D_MODEL), lambda e, f: (e + e0, f, 0))]
                  + [pl.BlockSpec(memory_space=pl.ANY) for _ in prior]),
        out_specs=y_specs,
        out_shape=[jax.ShapeDtypeStruct(s, I32) for s in y_shapes],
        input_output_aliases={n_in + j: j for j in range(len(prior))},
        scratch_shapes=[pltpu.VMEM((rows, D_MODEL), F32)],
        compiler_params=pltpu.CompilerParams(
            dimension_semantics=("arbitrary", "arbitrary"),
            vmem_limit_bytes=_vmem_limit(est + (4 << 20))),
        name="expert_ffn",
    )(*xs_groups, w_gate, w_up, w_down, *prior)


def _combine_kernel(start_ref, nch_ref, wide_ref, x1_ref, mod_ref, g_ref, slot_ref, y_hbm, o_ref,
                    ybuf_ref, cols_ref, ffn_ref, sem, *, first_k, kblock):
    i = pl.program_id(0)
    tm = x1_ref.shape[0]
    cur = i % 2
    n_window_rows = N_EXPERTS * COMBINE_WINDOW

    def window_copy(e, src_row, buf):
        return pltpu.make_async_copy(
            y_hbm.at[e, pl.ds(pl.multiple_of(src_row, SUBLANES), COMBINE_WINDOW)],
            ybuf_ref.at[buf, e * COMBINE_WINDOW:(e + 1) * COMBINE_WINDOW], sem.at[buf])

    def chunk_copy(e, src_row, buf, dst_row):
        return pltpu.make_async_copy(
            y_hbm.at[e, pl.ds(pl.multiple_of(src_row, SUBLANES), COMBINE_CHUNK)],
            ybuf_ref.at[buf, pl.ds(pl.multiple_of(dst_row, COMBINE_CHUNK), COMBINE_CHUNK)],
            sem.at[buf])

    def fetch(tile, buf):
        @pl.when(wide_ref[tile] == 0)
        def _():
            for e in range(N_EXPERTS):
                window_copy(e, start_ref[tile, e], buf).start()

        @pl.when(wide_ref[tile] != 0)
        def _():
            pos = jnp.int32(0)
            for e in range(N_EXPERTS):
                start = start_ref[tile, e]
                nch = nch_ref[tile, e]

                def issue(c, carry, e=e, start=start, pos=pos):
                    chunk_copy(e, start + c * COMBINE_CHUNK, buf, pos + c * COMBINE_CHUNK).start()
                    return carry
                lax.fori_loop(0, nch, issue, 0)
                pos = pos + nch * COMBINE_CHUNK

    @pl.when(i == 0)
    def _():
        ybuf_ref[...] = jnp.zeros_like(ybuf_ref)
        fetch(0, 0)

    @pl.when(i + 1 < pl.num_programs(0))
    def _():
        fetch(i + 1, 1 - cur)

    def finish(ffn):
        out = x1_ref[...] + mod_ref[0][5:6] * ffn
        ms = jnp.mean(out * out, axis=-1, keepdims=True)
        o_ref[...] = (out * lax.rsqrt(ms + EPS)) * g_ref[...]

    def apply_selection(sel, k0, width):
        lo, hi = _unpack_pair(ybuf_ref[cur, pl.ds(k0, width), :])
        return jnp.concatenate([jnp.dot(sel, lo, preferred_element_type=F32),
                                jnp.dot(sel, hi, preferred_element_type=F32)], axis=1)

    @pl.when(wide_ref[i] == 0)
    def _():
        for e in range(N_EXPERTS):
            window_copy(e, jnp.int32(0), cur).wait()
        expert_lane = lax.broadcasted_iota(I32, (1, N_EXPERTS), 1)
        starts = jnp.zeros((1, N_EXPERTS), I32)
        for e in range(N_EXPERTS):
            starts = jnp.where(expert_lane == e, start_ref[i, e], starts)
        slots = slot_ref[...]
        rows_in_window = jnp.where(slots >= 0, (slots - starts).astype(F32), -1.0)
        window_of_lane = lax.broadcasted_iota(I32, (N_EXPERTS, n_window_rows), 1) // COMBINE_WINDOW
        spread = (window_of_lane == lax.broadcasted_iota(I32, (N_EXPERTS, n_window_rows), 0))
        target = jnp.dot(rows_in_window.astype(BF16), spread.astype(BF16),
                         preferred_element_type=F32)
        lane_row = (lax.broadcasted_iota(I32, (tm, n_window_rows), 1) % COMBINE_WINDOW).astype(F32)
        sel = jnp.where(target == lane_row, 1.0, 0.0)
        ffn = None
        for k0 in range(0, n_window_rows, kblock):
            part = apply_selection(sel[:, k0:k0 + kblock], k0, kblock)
            ffn = part if ffn is None else ffn + part
        finish(ffn)

    @pl.when(wide_ref[i] != 0)
    def _():
        pos = jnp.int32(0)
        begins = []
        for e in range(N_EXPERTS):
            begins.append(pos)
            pos = pos + nch_ref[i, e] * COMBINE_CHUNK
        begins.append(pos)

        def drain(c, carry):
            chunk_copy(0, jnp.int32(0), cur, jnp.int32(0)).wait()
            return carry
        lax.fori_loop(0, pos // COMBINE_CHUNK, drain, 0)

        slots = slot_ref[...]
        for e in range(N_EXPERTS):
            s_e = slots[:, e:e + 1]
            col = jnp.where(s_e >= 0, s_e + (begins[e] - start_ref[i, e]), -1)
            cols_ref[e] = jnp.broadcast_to(col, (tm, LANES))

        lane = lax.broadcasted_iota(I32, (tm, LANES), 1)

        def selection(k0, width):
            halves = []
            for h in range(width // LANES):
                c0 = k0 + h * LANES
                target = lane + c0
                e_lo = jnp.int32(0)
                e_hi = jnp.int32(0)
                for e in range(N_EXPERTS):
                    e_lo = e_lo + (begins[e + 1] <= c0).astype(I32)
                    e_hi = e_hi + (begins[e] < c0 + LANES).astype(I32)

                def mark(e, hit, target=target):
                    return jnp.where(cols_ref[e] == target, 1.0, hit)
                halves.append(lax.fori_loop(e_lo, e_hi, mark, jnp.zeros((tm, LANES), F32)))
            return jnp.concatenate(halves, axis=1)

        ffn_ref[...] = apply_selection(selection(0, first_k), 0, first_k)

        def kstep(kb, carry):
            k0 = pl.multiple_of(kb * kblock, kblock)
            ffn_ref[...] += apply_selection(selection(k0, kblock), k0, kblock)
            return carry
        lax.fori_loop(first_k // kblock, (pos + kblock - 1) // kblock, kstep, 0)
        finish(ffn_ref[...])


def _combine(x1, mod3, mod_row, g_final, slot_t, tile_start, tile_nch, tile_wide, y):
    t = x1.shape[0]
    tm = TOKEN_TILE
    first_k = 2 * tm + N_EXPERTS * COMBINE_CHUNK
    kblock = 256
    max_rows = N_EXPERTS * (tm + 2 * COMBINE_CHUNK)
    max_rows = -(-max_rows // kblock) * kblock
    row = lambda i, *_: (i, 0)
    grid_spec = pltpu.PrefetchScalarGridSpec(
        num_scalar_prefetch=3,
        grid=(t // tm,),
        in_specs=[pl.BlockSpec((tm, D_MODEL), row),
                  pl.BlockSpec((1, N_MOD, D_MODEL), lambda i, *_: (mod_row(i), 0, 0)),
                  pl.BlockSpec((1, D_MODEL), lambda i, *_: (0, 0)),
                  pl.BlockSpec((tm, N_EXPERTS), row),
                  pl.BlockSpec(memory_space=pl.ANY)],
        out_specs=pl.BlockSpec((tm, D_MODEL), row),
        scratch_shapes=[pltpu.VMEM((2, max_rows, D_MODEL // 2), I32),
                        pltpu.VMEM((N_EXPERTS, tm, LANES), I32),
                        pltpu.VMEM((tm, D_MODEL), F32),
                        pltpu.SemaphoreType.DMA((2,))],
    )
    return pl.pallas_call(
        functools.partial(_combine_kernel, first_k=first_k, kblock=kblock),
        grid_spec=grid_spec,
        out_shape=jax.ShapeDtypeStruct((t, D_MODEL), F32),
        compiler_params=pltpu.CompilerParams(
            dimension_semantics=("arbitrary",),
            vmem_limit_bytes=_vmem_limit(2 * max_rows * D_MODEL * 2 + 16 * tm * D_MODEL * 4)),
        name="combine",
    )(tile_start, tile_nch, tile_wide, x1, mod3, g_final, slot_t, y)


def _rope_tables(n):
    rows = n // GRID_W
    row = jnp.repeat(jnp.arange(rows, dtype=F32), GRID_W)
    col = jnp.tile(jnp.arange(GRID_W, dtype=F32), rows)
    inv = ROPE_THETA ** (-jnp.arange(ROPE_FREQS, dtype=F32) / ROPE_FREQS)
    ang_r = row[:, None] * inv
    ang_c = col[:, None] * inv
    zero = jnp.zeros_like(ang_r)
    cos = jnp.concatenate([jnp.cos(ang_r)] * 2 + [jnp.cos(ang_c)] * 2, axis=1)
    sin_a = jnp.concatenate([-jnp.sin(ang_r), zero, -jnp.sin(ang_c), zero], axis=1)
    sin_b = jnp.concatenate([zero, jnp.sin(ang_r), zero, jnp.sin(ang_c)], axis=1)
    return cos, sin_a, sin_b


def _token_group(x, mod3, mod_row, seq, weights, rope_tabs, ctx_kv):
    (norm_mix, w_in_bf, sink, pool_w, pool_scale, w_out_bf, norm_ffn, w_router,
     w_gate, w_up, w_down, norm_final) = weights
    b = x.shape[0]
    t = b * seq
    x2 = x.reshape(t, D_MODEL)
    q, k, v, p, *state = _in_projection(x2, mod3, mod_row(PROJ_TILE), norm_mix, w_in_bf,
                                        rope_tabs)
    if ctx_kv is None:
        attn = _context_attention(q, k, v, sink, seq)
    else:
        attn = _latent_attention(q, k, v, ctx_kv[0], ctx_kv[1], sink, seq)
    x1, h_packed, aff = _out_projection(attn, p, x2, mod3, mod_row(PROJ_TILE), norm_ffn,
                                        w_out_bf, w_router, pool_w, pool_scale, seq)

    cap = EC_FACTOR * t // N_EXPERTS
    idx, slot_t, off3 = _routing(aff, cap)
    table = h_packed.reshape(-1, LANES)
    per_range = N_EXPERTS // FFN_RANGES
    xs = []
    for r in range(FFN_RANGES):
        ids = _packed_row_ids(idx[r * per_range:(r + 1) * per_range], cap)
        xs.append(_gather_rows(table, ids).reshape(per_range, HQ_TILES, cap, LANES))

    chunks_per_tile = TOKEN_TILE // LANES
    tile_off = off3[:, ::chunks_per_tile, 0]
    tile_end = jnp.concatenate([tile_off[:, 1:], jnp.full((N_EXPERTS, 1), cap, I32)], axis=1)
    tile_start = (tile_off // SUBLANES) * SUBLANES
    tile_nch = jnp.where(tile_end > tile_off,
                         (tile_end - tile_start + COMBINE_CHUNK - 1) // COMBINE_CHUNK, 0)
    tile_wide = jnp.any(tile_end - tile_start > COMBINE_WINDOW, axis=0).astype(I32)

    def finish(y):
        out = _combine(x1, mod3, mod_row(TOKEN_TILE), norm_final, slot_t, tile_start.T,
                       tile_nch.T, tile_wide, y)
        return out.reshape(b, seq, D_MODEL)
    return xs, finish, state, idx


def kernel(x_prompt, x_sample, c, cache_k, cache_v, c_ctx, w_ada, b_ada, norm_mix, w_in,
           sink_logits, pool_w, pool_scale, w_out, norm_ffn, w_router, w_gate, w_up, w_down,
           norm_final):
    n_b, seq, _ = x_prompt.shape
    n_db, n_lat, _ = x_sample.shape
    assert 1 + n_db <= MOD_ROWS and seq == TOKEN_TILE and n_lat % PROJ_TILE == 0
    assert (n_b * seq) % PROJ_TILE == 0

    cond = jnp.concatenate(
        [c_ctx[None, :], c, jnp.zeros((MOD_ROWS - 1 - n_db, D_MODEL), F32)], axis=0)
    mod3 = _modulation(cond, w_ada[0], b_ada[0]).reshape(MOD_ROWS, N_MOD, D_MODEL)

    w_router_bf = jnp.pad(w_router[0], ((0, 0), (0, LANES - N_EXPERTS))).astype(BF16)
    weights = (norm_mix[0][None, :], w_in[0].astype(BF16), sink_logits[0], pool_w[0].astype(BF16),
               pool_scale[0][None, :], w_out[0].astype(BF16), norm_ffn[0][None, :], w_router_bf,
               w_gate.reshape(w_gate.shape[1:]), w_up.reshape(w_up.shape[1:]),
               w_down.reshape(w_down.shape[1:]), norm_final[None, :])

    xs_p, finish_p, (k_p, v_p), routed_p = _token_group(
        x_prompt, mod3, lambda tile: (lambda i: 0), seq, weights, None, None)
    x_sample, _ = lax.optimization_barrier((x_sample, routed_p))

    past = cache_k.shape[2]
    ck = cache_k[:, 0].reshape(n_db, past, KV_WIDTH)
    cv = cache_v[:, 0].reshape(n_db, past, KV_WIDTH)
    xs_l, finish_l, _, _ = _token_group(
        x_sample, mod3, lambda tile: (lambda i: 1 + i // (n_lat // tile)), n_lat, weights,
        _rope_tables(n_lat), (ck, cv))

    ys = None
    for r in range(FFN_RANGES):
        ys = _expert_ffn([xs_p[r], xs_l[r]], weights[8], weights[9], weights[10],
                         r * (N_EXPERTS // FFN_RANGES), ys)
    y_prompt = finish_p(ys[0])
    y_sample = finish_l(ys[1])

    state_k = k_p.reshape(n_b, 1, seq, N_KV_HEADS, HEAD_DIM)
    state_v = v_p.reshape(n_b, 1, seq, N_KV_HEADS, HEAD_DIM)
    return (y_prompt, y_sample, state_k, state_v)
```

```python
import functools

import jax
import jax.numpy as jnp
from jax import lax
from jax.experimental import pallas as pl
from jax.experimental.pallas import tpu as pltpu
from jax.experimental.pallas import tpu_sc as plsc

F32 = jnp.float32
BF16 = jnp.bfloat16
I32 = jnp.int32

D_MODEL = 2048
N_HEADS = 8
N_KV_HEADS = 2
HEAD_DIM = 128
Q_PER_KV = N_HEADS // N_KV_HEADS
ATTN_WIDTH = N_HEADS * HEAD_DIM
KV_WIDTH = N_KV_HEADS * HEAD_DIM
POOL_WIDTH = D_MODEL - ATTN_WIDTH
POOL_SIZES = (2, 4, 8, 16)
POOL_GROUP = POOL_WIDTH // len(POOL_SIZES)
IN_WIDTH = ATTN_WIDTH + 2 * KV_WIDTH + POOL_WIDTH
WINDOW = 128
BLOCK = 128
GRID_W = 64
ROPE_THETA = 10000.0
ROPE_FREQS = HEAD_DIM // 4
N_EXPERTS = 16
EC_FACTOR = 2
D_EXPERT = 1024
N_MOD = 6
EPS = 1e-6
NEG = -1e30
LOG2_E = 1.4426950408889634
ATTN_SCALE = HEAD_DIM ** -0.5 * LOG2_E

LANES = 128
SUBLANES = 8
BF16_ROWS = 16
VMEM_CAP = 64 * 1024 * 1024
SC_CORES = 2
SC_SUBCORES = 16

MOD_ROWS = 8
TOKEN_TILE = 256
PROJ_TILE = 512
SUB_TILE = 256
HQ_TILES = D_MODEL // 2 // LANES + 1
GATHER_CHUNK = 128
GATHER_SPLIT = 8
FFN_RANGES = 4
POOL_HALO = 8
COMBINE_CHUNK = BF16_ROWS
COMBINE_WINDOW = 64


def _vmem_limit(nbytes):
    return int(min(VMEM_CAP - (4 << 20), max(nbytes, 16 << 20)))


def _mod_kernel(c_ref, w_ref, b_ref, o_ref):
    c = c_ref[...]
    s = c * jax.nn.sigmoid(c)
    o_ref[...] = jnp.dot(s.astype(BF16), w_ref[...].astype(BF16),
                         preferred_element_type=F32) + b_ref[...]


def _modulation(cond, w_ada, b_ada):
    n = w_ada.shape[1]
    tn = 1024
    return pl.pallas_call(
        _mod_kernel,
        grid=(n // tn,),
        in_specs=[pl.BlockSpec((MOD_ROWS, D_MODEL), lambda j: (0, 0)),
                  pl.BlockSpec((D_MODEL, tn), lambda j: (0, j)),
                  pl.BlockSpec((1, tn), lambda j: (0, j))],
        out_specs=pl.BlockSpec((MOD_ROWS, tn), lambda j: (0, j)),
        out_shape=jax.ShapeDtypeStruct((MOD_ROWS, n), F32),
        compiler_params=pltpu.CompilerParams(
            vmem_limit_bytes=_vmem_limit(3 * D_MODEL * tn * 4)),
        name="modulation",
    )(cond, w_ada, b_ada.reshape(1, n))


def _norm_mod(x, g, shift, scale):
    ms = jnp.mean(x * x, axis=-1, keepdims=True)
    y = x * lax.rsqrt(ms + EPS)
    return (y * g) * (1.0 + scale) + shift


def _inproj_kernel(*refs, rope):
    if rope:
        x_ref, mod_ref, g_ref, w_ref, cos_ref, sa_ref, sb_ref, q_ref, k_ref, v_ref, p_ref = refs
    else:
        x_ref, mod_ref, g_ref, w_ref, q_ref, k_ref, v_ref, p_ref, ks_ref, vs_ref = refs
    mod = mod_ref[0]
    for s in range(x_ref.shape[0] // SUB_TILE):
        rows = slice(s * SUB_TILE, (s + 1) * SUB_TILE)
        h = _norm_mod(x_ref[rows, :], g_ref[...], mod[0:1], mod[1:2])
        u = jnp.dot(h.astype(BF16), w_ref[...], preferred_element_type=F32)

        def rot(xh, rows=rows):
            return (xh * cos_ref[rows, :] + pltpu.roll(xh, LANES - ROPE_FREQS, 1) * sa_ref[rows, :]
                    + pltpu.roll(xh, ROPE_FREQS, 1) * sb_ref[rows, :])

        for hd in range(N_HEADS):
            xh = u[:, hd * HEAD_DIM:(hd + 1) * HEAD_DIM] * ATTN_SCALE
            q_ref[rows, hd * HEAD_DIM:(hd + 1) * HEAD_DIM] = (rot(xh) if rope else xh).astype(BF16)
        for hd in range(N_KV_HEADS):
            lo = ATTN_WIDTH + hd * HEAD_DIM
            xh = u[:, lo:lo + HEAD_DIM]
            k_ref[rows, hd * HEAD_DIM:(hd + 1) * HEAD_DIM] = rot(xh) if rope else xh
        v_ref[rows, :] = u[:, ATTN_WIDTH + KV_WIDTH:ATTN_WIDTH + 2 * KV_WIDTH]
        p_ref[rows, :] = u[:, ATTN_WIDTH + 2 * KV_WIDTH:]
        if not rope:
            for hd in range(N_KV_HEADS):
                state_rows = pl.ds(s * SUB_TILE * N_KV_HEADS + hd, SUB_TILE, stride=N_KV_HEADS)
                lo = ATTN_WIDTH + hd * HEAD_DIM
                ks_ref[state_rows, :] = u[:, lo:lo + HEAD_DIM]
                vs_ref[state_rows, :] = u[:, lo + KV_WIDTH:lo + KV_WIDTH + HEAD_DIM]


def _in_projection(x2, mod3, mod_row, g, w_in_bf, rope_tabs):
    t = x2.shape[0]
    tm = PROJ_TILE
    rope = rope_tabs is not None
    row = lambda i: (i, 0)
    in_specs = [pl.BlockSpec((tm, D_MODEL), row),
                pl.BlockSpec((1, N_MOD, D_MODEL), lambda i: (mod_row(i), 0, 0)),
                pl.BlockSpec((1, D_MODEL), lambda i: (0, 0)),
                pl.BlockSpec((D_MODEL, IN_WIDTH), lambda i: (0, 0))]
    args = [x2, mod3, g, w_in_bf]
    if rope:
        n_seq = rope_tabs[0].shape[0]
        seq_blocks = n_seq // tm
        for tab in rope_tabs:
            in_specs.append(pl.BlockSpec((tm, HEAD_DIM), lambda i: (i % seq_blocks, 0)))
            args.append(tab)
    out_specs = [pl.BlockSpec((tm, ATTN_WIDTH), row),
                 pl.BlockSpec((tm, KV_WIDTH), row),
                 pl.BlockSpec((tm, KV_WIDTH), row),
                 pl.BlockSpec((tm, POOL_WIDTH), row)]
    out_shape = [jax.ShapeDtypeStruct((t, ATTN_WIDTH), BF16),
                 jax.ShapeDtypeStruct((t, KV_WIDTH), F32),
                 jax.ShapeDtypeStruct((t, KV_WIDTH), F32),
                 jax.ShapeDtypeStruct((t, POOL_WIDTH), F32)]
    if not rope:
        for _ in range(2):
            out_specs.append(pl.BlockSpec((tm * N_KV_HEADS, HEAD_DIM), row))
            out_shape.append(jax.ShapeDtypeStruct((t * N_KV_HEADS, HEAD_DIM), F32))
    return pl.pallas_call(
        functools.partial(_inproj_kernel, rope=rope),
        grid=(t // tm,),
        in_specs=in_specs,
        out_specs=out_specs,
        out_shape=out_shape,
        compiler_params=pltpu.CompilerParams(
            vmem_limit_bytes=_vmem_limit(2 * D_MODEL * IN_WIDTH * 2 + 24 * tm * D_MODEL * 4)),
        name="in_projection",
    )(*args)


def _softmax_pv(s_list, v_list, sink_col):
    m = sink_col
    for s in s_list:
        m = jnp.maximum(m, jnp.max(s, axis=-1, keepdims=True))
    denom = jnp.exp2(sink_col - m)
    out = None
    for s, v in zip(s_list, v_list):
        e = jnp.exp2(s - m)
        if v.shape[1] == HEAD_DIM:
            denom = denom + jnp.sum(e, axis=-1, keepdims=True)
        o = jnp.dot(e.astype(BF16), v, preferred_element_type=F32)
        out = o if out is None else out + o
    if out.shape[1] > HEAD_DIM:
        denom = denom + out[:, HEAD_DIM:HEAD_DIM + 1]
    return out[:, 0:HEAD_DIM] * (1.0 / denom)


def _with_ones(v):
    return jnp.concatenate([v.astype(BF16), jnp.ones(v.shape, BF16)], axis=1)


def _stack_heads(q, kv):
    return jnp.concatenate(
        [q[:, (kv * Q_PER_KV + g) * HEAD_DIM:(kv * Q_PER_KV + g + 1) * HEAD_DIM]
         for g in range(Q_PER_KV)], axis=0)


def _sink_column(sink_ref, kv, rows):
    r = lax.broadcasted_iota(I32, (Q_PER_KV * rows, 1), 0)
    col = jnp.zeros((Q_PER_KV * rows, 1), F32)
    for g in range(Q_PER_KV):
        col = jnp.where((r >= g * rows) & (r < (g + 1) * rows), sink_ref[kv * Q_PER_KV + g], col)
    return col * LOG2_E


def _qk(q, k):
    return lax.dot_general(q, k, (((1,), (1,)), ((), ())), preferred_element_type=F32)


def _ctx_attn_kernel(sink_ref, q_ref, k_ref, v_ref, o_ref):
    rows = q_ref.shape[0]
    q = q_ref[...]
    for kv in range(N_KV_HEADS):
        kh = k_ref[:, kv * HEAD_DIM:(kv + 1) * HEAD_DIM].astype(BF16)
        vh = v_ref[:, kv * HEAD_DIM:(kv + 1) * HEAD_DIM].astype(BF16)
        qs = _stack_heads(q, kv)
        s = _qk(qs, kh)
        o = _softmax_pv([s], [vh], _sink_column(sink_ref, kv, rows))
        for g in range(Q_PER_KV):
            hd = kv * Q_PER_KV + g
            o_ref[:, hd * HEAD_DIM:(hd + 1) * HEAD_DIM] = o[g * rows:(g + 1) * rows].astype(BF16)


def _context_attention(q, k, v, sink, seq):
    t = q.shape[0]
    row = lambda b: (b, 0)
    return pl.pallas_call(
        _ctx_attn_kernel,
        grid=(t // seq,),
        in_specs=[pl.BlockSpec(memory_space=pltpu.SMEM),
                  pl.BlockSpec((seq, ATTN_WIDTH), row),
                  pl.BlockSpec((seq, KV_WIDTH), row),
                  pl.BlockSpec((seq, KV_WIDTH), row)],
        out_specs=pl.BlockSpec((seq, ATTN_WIDTH), row),
        out_shape=jax.ShapeDtypeStruct((t, ATTN_WIDTH), BF16),
        name="context_attention",
    )(sink, q, k, v)


def _lat_attn_kernel(sink_ref, q_ref, k_ref, v_ref, ck_ref, cv_ref, o_ref, *, n_seq):
    i = pl.program_id(1)
    band = 3 * BLOCK
    start = pl.multiple_of(jnp.clip((i - 1) * BLOCK, 0, n_seq - band), BLOCK)
    rows = Q_PER_KV * BLOCK
    qpos = i * BLOCK + lax.broadcasted_iota(I32, (rows, band), 0) % BLOCK
    kpos = start + lax.broadcasted_iota(I32, (rows, band), 1)
    mask = jnp.abs(kpos - qpos) <= WINDOW
    q = q_ref[...]
    for kv in range(N_KV_HEADS):
        cols = slice(kv * HEAD_DIM, (kv + 1) * HEAD_DIM)
        kb = k_ref[pl.ds(start, band), cols].astype(BF16)
        vb = _with_ones(v_ref[pl.ds(start, band), cols])
        ck = ck_ref[0, :, cols].astype(BF16)
        cv = _with_ones(cv_ref[0, :, cols])
        qs = _stack_heads(q, kv)
        s_loc = jnp.where(mask, _qk(qs, kb), NEG)
        s_ctx = _qk(qs, ck)
        o = _softmax_pv([s_loc, s_ctx], [vb, cv], _sink_column(sink_ref, kv, BLOCK))
        for g in range(Q_PER_KV):
            hd = kv * Q_PER_KV + g
            o_ref[:, hd * HEAD_DIM:(hd + 1) * HEAD_DIM] = o[g * BLOCK:(g + 1) * BLOCK].astype(BF16)


def _latent_attention(q, k, v, ck, cv, sink, n_seq):
    t = q.shape[0]
    nb = n_seq // BLOCK
    past = ck.shape[1]
    return pl.pallas_call(
        functools.partial(_lat_attn_kernel, n_seq=n_seq),
        grid=(t // n_seq, nb),
        in_specs=[pl.BlockSpec(memory_space=pltpu.SMEM),
                  pl.BlockSpec((BLOCK, ATTN_WIDTH), lambda b, i: (b * nb + i, 0)),
                  pl.BlockSpec((n_seq, KV_WIDTH), lambda b, i: (b, 0)),
                  pl.BlockSpec((n_seq, KV_WIDTH), lambda b, i: (b, 0)),
                  pl.BlockSpec((1, past, KV_WIDTH), lambda b, i: (b, 0, 0)),
                  pl.BlockSpec((1, past, KV_WIDTH), lambda b, i: (b, 0, 0))],
        out_specs=pl.BlockSpec((BLOCK, ATTN_WIDTH), lambda b, i: (b * nb + i, 0)),
        out_shape=jax.ShapeDtypeStruct((t, ATTN_WIDTH), BF16),
        name="latent_attention",
    )(sink, q, k, v, ck, cv)


def _pool_group(p_ref, r0, seq, w_ref, s_ref, g):
    n = SUB_TILE
    rows = n + 2 * POOL_HALO
    static = isinstance(r0, int)
    t0 = r0 % seq
    t = t0 + lax.broadcasted_iota(I32, (n, 1), 0)
    has_top = t0 > 0
    has_bottom = t0 + n < seq
    zeros = jnp.zeros((POOL_HALO, POOL_GROUP), F32)
    w = POOL_SIZES[g]
    cols = slice(g * POOL_GROUP, (g + 1) * POOL_GROUP)
    pg = p_ref[pl.ds(r0, n), cols]
    if static:
        top = p_ref[r0 - POOL_HALO:r0, cols] if has_top else zeros
        bottom = p_ref[r0 + n:r0 + n + POOL_HALO, cols] if has_bottom else zeros
    else:
        top_row = pl.multiple_of(jnp.maximum(r0 - POOL_HALO, 0), POOL_HALO)
        bottom_row = pl.multiple_of(jnp.minimum(r0 + n, p_ref.shape[0] - POOL_HALO), POOL_HALO)
        top = jnp.where(has_top, p_ref[pl.ds(top_row, POOL_HALO), cols], 0.0)
        bottom = jnp.where(has_bottom, p_ref[pl.ds(bottom_row, POOL_HALO), cols], 0.0)
    x = jnp.concatenate([top, pg, bottom], axis=0)
    fwd = x
    span = 1
    while span < w // 2:
        fwd = fwd + pltpu.roll(fwd, rows - span, 0)
        span *= 2
    if (w // 2) % SUBLANES == 0:
        wsum = fwd[POOL_HALO - w // 2:POOL_HALO - w // 2 + n] + fwd[POOL_HALO:POOL_HALO + n]
    else:
        wsum = (fwd + pltpu.roll(fwd, w // 2, 0))[POOL_HALO:POOL_HALO + n]
    lo = jnp.maximum(t - w // 2, 0)
    hi = jnp.minimum(t + w - w // 2, seq)
    inv_cnt = 1.0 / (hi - lo).astype(F32)
    mixed = wsum * inv_cnt - pg
    y = jnp.dot(mixed.astype(BF16), w_ref[g], preferred_element_type=F32)
    return (y * s_ref[:, cols]).astype(BF16)


def _pack_pair(lo, hi):
    return lax.bitcast_convert_type(pltpu.pack_elementwise([lo, hi], packed_dtype=BF16), I32)


def _unpack_pair(words):
    lo = pltpu.unpack_elementwise(words, index=0, packed_dtype=BF16, unpacked_dtype=F32)
    hi = pltpu.unpack_elementwise(words, index=1, packed_dtype=BF16, unpacked_dtype=F32)
    return lo, hi


def _outproj_kernel(a_ref, p_ref, x_ref, mod_ref, g_ref, wo_ref, wr_ref, pw_ref, ps_ref,
                    x1_ref, h_ref, aff_ref, *, seq):
    mod = mod_ref[0]
    groups = SUB_TILE // SUBLANES
    half = D_MODEL // 2
    tm = x_ref.shape[0]
    steps_per_p_block = p_ref.shape[0] // tm
    for s in range(tm // SUB_TILE):
        rows = slice(s * SUB_TILE, (s + 1) * SUB_TILE)
        grp = slice(s * groups, (s + 1) * groups)
        r0 = s * SUB_TILE
        if steps_per_p_block > 1:
            r0 = pl.multiple_of((pl.program_id(0) % steps_per_p_block) * tm + r0, SUB_TILE)
        pooled = jnp.concatenate([_pool_group(p_ref, r0, seq, pw_ref, ps_ref, g)
                                  for g in range(len(POOL_SIZES))], axis=1)
        mix = (jnp.dot(a_ref[rows, :], wo_ref[0:ATTN_WIDTH, :], preferred_element_type=F32)
               + jnp.dot(pooled, wo_ref[ATTN_WIDTH:D_MODEL, :], preferred_element_type=F32))
        x1 = x_ref[rows, :] + mod[2:3] * mix
        x1_ref[rows, :] = x1
        h = _norm_mod(x1, g_ref[...], mod[3:4], mod[4:5])
        logits = jnp.dot(h.astype(BF16), wr_ref[...], preferred_element_type=F32)
        lane = lax.broadcasted_iota(I32, logits.shape, 1)
        logits = jnp.where(lane < N_EXPERTS, logits, -jnp.inf)
        m = jnp.max(logits, axis=-1, keepdims=True)
        e = jnp.exp(logits - m)
        aff = e / jnp.sum(e, axis=-1, keepdims=True)
        aff_ref[rows, :] = aff[:, 0:N_EXPERTS]
        for c in range(HQ_TILES - 1):
            words = _pack_pair(h[:, c * LANES:(c + 1) * LANES],
                               h[:, half + c * LANES:half + (c + 1) * LANES])
            h_ref[grp, c * SUBLANES:(c + 1) * SUBLANES, :] = (
                words.reshape(groups, SUBLANES, LANES))
        h_ref[grp, (HQ_TILES - 1) * SUBLANES:, :] = (
            lax.bitcast_convert_type(aff, I32).reshape(groups, SUBLANES, LANES))


def _out_projection(attn, p, x2, mod3, mod_row, g, w_out_bf, w_router, pool_w, pool_scale, seq):
    t = x2.shape[0]
    tm = PROJ_TILE
    row = lambda i: (i, 0)
    p_rows = max(tm, seq)
    steps_per_p_block = p_rows // tm
    return pl.pallas_call(
        functools.partial(_outproj_kernel, seq=seq),
        grid=(t // tm,),
        in_specs=[pl.BlockSpec((tm, ATTN_WIDTH), row),
                  pl.BlockSpec((p_rows, POOL_WIDTH), lambda i: (i // steps_per_p_block, 0)),
                  pl.BlockSpec((tm, D_MODEL), row),
                  pl.BlockSpec((1, N_MOD, D_MODEL), lambda i: (mod_row(i), 0, 0)),
                  pl.BlockSpec((1, D_MODEL), lambda i: (0, 0)),
                  pl.BlockSpec((D_MODEL, D_MODEL), lambda i: (0, 0)),
                  pl.BlockSpec((D_MODEL, LANES), lambda i: (0, 0)),
                  pl.BlockSpec((len(POOL_SIZES), POOL_GROUP, POOL_GROUP), lambda i: (0, 0, 0)),
                  pl.BlockSpec((1, POOL_WIDTH), lambda i: (0, 0))],
        out_specs=[pl.BlockSpec((tm, D_MODEL), row),
                   pl.BlockSpec((tm // SUBLANES, HQ_TILES * SUBLANES, LANES), lambda i: (i, 0, 0)),
                   pl.BlockSpec((tm, N_EXPERTS), row)],
        out_shape=[jax.ShapeDtypeStruct((t, D_MODEL), F32),
                   jax.ShapeDtypeStruct((t // SUBLANES, HQ_TILES * SUBLANES, LANES), I32),
                   jax.ShapeDtypeStruct((t, N_EXPERTS), F32)],
        compiler_params=pltpu.CompilerParams(
            vmem_limit_bytes=_vmem_limit(2 * D_MODEL * D_MODEL * 2 + 24 * tm * D_MODEL * 4)),
        name="out_projection",
    )(attn, p, x2, mod3, g, w_out_bf, w_router, pool_w, pool_scale)


def _route_kernel(a_ref, idx_ref, slot_ref, off_ref, slot_scr, *, cap):
    a = a_ref[...]
    n_e, n_c, _ = a.shape
    rows = n_e * n_c

    def enough(cand):
        cand_f = lax.bitcast_convert_type(cand, F32)
        cnt = jnp.sum(jnp.sum((a >= cand_f).astype(F32), axis=1, keepdims=True),
                      axis=2, keepdims=True)
        return cnt >= cap

    def two_bits(it, thr):
        low = 28 - 2 * it
        for setting in (1, 2, 3):
            cand = thr | jnp.left_shift(jnp.int32(setting), low)
            best = jnp.where(enough(cand), cand, thr if setting == 1 else best)
        return best

    top = jnp.full((n_e, 1, 1), 1 << 30, I32)
    thr = jnp.where(enough(top), top, 0)
    thr = lax.fori_loop(0, 15, two_bits, thr)
    thr_f = lax.bitcast_convert_type(thr, F32)
    gt = (a > thr_f).astype(F32).reshape(rows, LANES)
    eq = (a == thr_f).astype(F32).reshape(rows, LANES)

    li = lax.broadcasted_iota(I32, (LANES, LANES), 0)
    lj = lax.broadcasted_iota(I32, (LANES, LANES), 1)
    upper_incl = (li <= lj).astype(BF16)
    ci = lax.broadcasted_iota(I32, (n_c, n_c), 0)
    cj = lax.broadcasted_iota(I32, (n_c, n_c), 1)
    before = (cj < ci).astype(BF16)
    whole = jnp.ones((n_c, n_c), BF16)

    def over_chunks(mat, col):
        wide = jnp.broadcast_to(col, (rows, LANES)).astype(BF16)
        side_by_side = jnp.concatenate([wide[e * n_c:(e + 1) * n_c] for e in range(n_e)], axis=1)
        res = jnp.dot(mat, side_by_side, preferred_element_type=F32)
        return jnp.concatenate([res[:, e * LANES:(e + 1) * LANES] for e in range(n_e)], axis=0)

    def prefix(x):
        incl = jnp.dot(x.astype(BF16), upper_incl, preferred_element_type=F32)
        tot = incl[:, LANES - 1:LANES]
        return incl, tot, over_chunks(before, tot)

    n_gt = over_chunks(whole, jnp.sum(gt, axis=1, keepdims=True))
    need = cap - n_gt
    incl_eq, _, off_eq = prefix(eq)
    rank_eq = off_eq + incl_eq - eq
    sel = jnp.where((eq > 0) & (rank_eq < need), 1.0, gt)
    incl, tot, off = prefix(sel)
    slot = off + incl - sel
    slot_scr[...] = jnp.where(sel > 0, slot, -1.0)
    for c in range(n_c):
        per_expert = slot_scr[pl.ds(c, n_e, stride=n_c), :]
        slot_ref[c * LANES:(c + 1) * LANES, :] = per_expert.T.astype(I32)
    off_ref[...] = off.astype(I32).reshape(n_e, n_c, LANES)

    s_lane = lax.broadcasted_iota(I32, (1, cap), 1).astype(F32)
    c_col = lax.broadcasted_iota(I32, (n_c, 1), 0).astype(F32)
    for e in range(n_e):
        r0 = e * n_c
        incl_e = incl[r0:r0 + n_c]
        off_e = off[r0:r0 + n_c, 0:1]
        tot_e = tot[r0:r0 + n_c]
        onehot = ((off_e <= s_lane) & (s_lane < off_e + tot_e)).astype(F32)
        counts = lax.dot_general(incl_e.astype(BF16), onehot.astype(BF16),
                                 (((0,), (0,)), ((), ())), preferred_element_type=F32)
        local = s_lane - jnp.sum(onehot * off_e, axis=0, keepdims=True)
        lane = jnp.sum((counts <= local).astype(F32), axis=0, keepdims=True)
        chunk = jnp.sum(onehot * c_col, axis=0, keepdims=True)
        idx_ref[e] = (chunk * LANES + lane).astype(I32)


def _routing(aff, cap):
    t = aff.shape[0]
    n_c = t // LANES
    a3 = aff.T.reshape(N_EXPERTS, n_c, LANES)
    return pl.pallas_call(
        functools.partial(_route_kernel, cap=cap),
        out_shape=[jax.ShapeDtypeStruct((N_EXPERTS, 1, cap), I32),
                   jax.ShapeDtypeStruct((t, N_EXPERTS), I32),
                   jax.ShapeDtypeStruct((N_EXPERTS, n_c, LANES), I32)],
        scratch_shapes=[pltpu.VMEM((N_EXPERTS * n_c, LANES), F32)],
        compiler_params=pltpu.CompilerParams(vmem_limit_bytes=_vmem_limit(48 << 20)),
        name="routing",
    )(a3)


def _gather_rows(table, row_ids):
    n_chunks, chunk = row_ids.shape
    n_workers = SC_CORES * SC_SUBCORES
    per_worker = n_chunks // n_workers
    assert chunk <= GATHER_CHUNK and chunk % SUBLANES == 0 and n_chunks % n_workers == 0
    mesh = plsc.VectorSubcoreMesh(core_axis_name="core", subcore_axis_name="subcore")

    @functools.partial(
        pl.kernel, mesh=mesh,
        out_type=jax.ShapeDtypeStruct((n_chunks * chunk, LANES), I32),
        scratch_types=[pltpu.VMEM((per_worker, chunk), I32),
                       pltpu.VMEM((2, chunk, LANES), I32),
                       pltpu.SemaphoreType.DMA((2,)),
                       pltpu.SemaphoreType.DMA((2,))],
        name="gather_rows",
    )
    def gather(table_hbm, ids_hbm, out_hbm, ids_v, rows_v, gather_sem, store_sem):
        worker = lax.axis_index("subcore") * SC_CORES + lax.axis_index("core")
        first = worker * per_worker
        pltpu.sync_copy(ids_hbm.at[worker], ids_v)

        def fetch(j):
            return pltpu.make_async_copy(table_hbm.at[ids_v.at[j]], rows_v.at[j % 2],
                                         gather_sem.at[j % 2])

        def store(j):
            rows = pl.ds(pl.multiple_of((first + j) * chunk, chunk), chunk)
            return pltpu.make_async_copy(rows_v.at[j % 2], out_hbm.at[rows], store_sem.at[j % 2])

        for j in range(per_worker):
            if j >= 2:
                store(j - 2).wait()
            fetch(j).start()
            if j >= 1:
                fetch(j - 1).wait()
                store(j - 1).start()
        fetch(per_worker - 1).wait()
        store(per_worker - 1).start()
        for j in range(max(per_worker - 2, 0), per_worker):
            store(j).wait()

    return gather(table, row_ids.reshape(n_workers, per_worker, chunk))


def _packed_row_ids(idx, cap):
    chunk = min(GATHER_CHUNK, cap // GATHER_SPLIT)
    tok = idx.reshape(-1, 1, cap // chunk, chunk)
    tile = jnp.arange(HQ_TILES, dtype=I32).reshape(1, HQ_TILES, 1, 1)
    ids = ((tok // SUBLANES) * HQ_TILES + tile) * SUBLANES + tok % SUBLANES
    return ids.reshape(-1, chunk)


def _ffn_kernel(*refs, caps, n_f, row_chunk, first_expert, n_prior):
    n_g = len(caps)
    x_refs = refs[:n_g]
    wg_ref, wu_ref, wd_ref = refs[n_g:n_g + 3]
    y_refs = refs[n_g + 3 + n_prior:2 * n_g + 3 + n_prior]
    acc_ref = refs[2 * n_g + 3 + n_prior]
    e = pl.program_id(0)
    f = pl.program_id(1)

    @pl.when((e == 0) & (f == 0))
    def _():
        acc_ref[...] = jnp.zeros_like(acc_ref)

    def step(last):
        first = f == 0
        base = 0
        for x_ref, y_ref, cap in zip(x_refs, y_refs, caps):
            if last:
                aff = lax.bitcast_convert_type(x_ref[0, HQ_TILES - 1], F32)
                lane = lax.broadcasted_iota(I32, aff.shape, 1)
                gates = jnp.sum(jnp.where(lane == e + first_expert, aff, 0.0), axis=1,
                                keepdims=True)
                y_ref[0, cap:, :] = jnp.zeros((COMBINE_WINDOW, D_MODEL // 2), I32)
            for r in range(cap // row_chunk):
                rs = slice(r * row_chunk, (r + 1) * row_chunk)
                acc_rows = slice(base + r * row_chunk, base + (r + 1) * row_chunk)
                pairs = [_unpack_pair(x_ref[0, c, rs, :]) for c in range(HQ_TILES - 1)]
                x = jnp.concatenate([p[0] for p in pairs] + [p[1] for p in pairs], axis=1)
                gate_act = jnp.dot(x, wg_ref[0], preferred_element_type=F32)
                up = jnp.dot(x, wu_ref[0], preferred_element_type=F32)
                hid = (gate_act * jax.nn.sigmoid(gate_act)) * up
                part = jnp.dot(hid, wd_ref[0], preferred_element_type=F32)
                if last:
                    y = (part + acc_ref[acc_rows, :]) * gates[rs]
                    y_ref[0, rs, :] = _pack_pair(y[:, 0:D_MODEL // 2], y[:, D_MODEL // 2:])
                else:
                    acc_ref[acc_rows, :] = part + jnp.where(first, 0.0, acc_ref[acc_rows, :])
            base += cap

    @pl.when(f < n_f - 1)
    def _():
        step(False)

    @pl.when(f == n_f - 1)
    def _():
        step(True)


def _expert_ffn(xs_groups, w_gate, w_up, w_down, first_expert, prior_outputs):
    n_f = 4
    tf = D_EXPERT // n_f
    row_chunk = 512
    n_e = xs_groups[0].shape[0]
    e0 = first_expert
    caps = tuple(xs.shape[2] for xs in xs_groups)
    rows = sum(caps)
    est = (2 * HQ_TILES * rows * LANES * 4 + rows * D_MODEL * (4 + 2 * 2) + 2 * 3 * D_MODEL * tf * 4
           + row_chunk * (3 * tf + 2 * D_MODEL) * 4)
    x_specs = [pl.BlockSpec((1, HQ_TILES, cap, LANES), lambda e, f: (e, 0, 0, 0)) for cap in caps]
    y_shapes = [(N_EXPERTS, cap + COMBINE_WINDOW, D_MODEL // 2) for cap in caps]
    y_specs = [pl.BlockSpec((1,) + s[1:], lambda e, f: (e + e0, 0, 0)) for s in y_shapes]
    prior = list(prior_outputs or [])
    n_in = len(caps) + 3
    return pl.pallas_call(
        functools.partial(_ffn_kernel, caps=caps, n_f=n_f, row_chunk=row_chunk,
                          first_expert=e0, n_prior=len(prior)),
        grid=(n_e, n_f),
        in_specs=(x_specs + [pl.BlockSpec((1, D_MODEL, tf), lambda e, f: (e + e0, 0, f)),
                             pl.BlockSpec((1, D_MODEL, tf), lambda e, f: (e + e0, 0, f)),
                             pl.BlockSpec((1, tf, D_MODEL), lambda e, f: (e + e0, f, 0))]
                  + [pl.BlockSpec(memory_space=pl.ANY) for _ in prior]),
        out_specs=y_specs,
        out_shape=[jax.ShapeDtypeStruct(s, I32) for s in y_shapes],
        input_output_aliases={n_in + j: j for j in range(len(prior))},
        scratch_shapes=[pltpu.VMEM((rows, D_MODEL), F32)],
        compiler_params=pltpu.CompilerParams(
            dimension_semantics=("arbitrary", "arbitrary"),
            vmem_limit_bytes=_vmem_limit(est + (4 << 20))),
        name="expert_ffn",
    )(*xs_groups, w_gate, w_up, w_down, *prior)


def _combine_kernel(start_ref, nch_ref, wide_ref, x1_ref, mod_ref, g_ref, slot_ref, y_hbm, o_ref,
                    ybuf_ref, cols_ref, ffn_ref, sem, *, first_k, kblock):
    i = pl.program_id(0)
    tm = x1_ref.shape[0]
    cur = i % 2
    n_window_rows = N_EXPERTS * COMBINE_WINDOW

    def window_copy(e, src_row, buf):
        return pltpu.make_async_copy(
            y_hbm.at[e, pl.ds(pl.multiple_of(src_row, SUBLANES), COMBINE_WINDOW)],
            ybuf_ref.at[buf, e * COMBINE_WINDOW:(e + 1) * COMBINE_WINDOW], sem.at[buf])

    def chunk_copy(e, src_row, buf, dst_row):
        return pltpu.make_async_copy(
            y_hbm.at[e, pl.ds(pl.multiple_of(src_row, SUBLANES), COMBINE_CHUNK)],
            ybuf_ref.at[buf, pl.ds(pl.multiple_of(dst_row, COMBINE_CHUNK), COMBINE_CHUNK)],
            sem.at[buf])

    def fetch(tile, buf):
        @pl.when(wide_ref[tile] == 0)
        def _():
            for e in range(N_EXPERTS):
                window_copy(e, start_ref[tile, e], buf).start()

        @pl.when(wide_ref[tile] != 0)
        def _():
            pos = jnp.int32(0)
            for e in range(N_EXPERTS):
                start = start_ref[tile, e]
                nch = nch_ref[tile, e]

                def issue(c, carry, e=e, start=start, pos=pos):
                    chunk_copy(e, start + c * COMBINE_CHUNK, buf, pos + c * COMBINE_CHUNK).start()
                    return carry
                lax.fori_loop(0, nch, issue, 0)
                pos = pos + nch * COMBINE_CHUNK

    @pl.when(i == 0)
    def _():
        ybuf_ref[...] = jnp.zeros_like(ybuf_ref)
        fetch(0, 0)

    @pl.when(i + 1 < pl.num_programs(0))
    def _():
        fetch(i + 1, 1 - cur)

    def finish(ffn):
        out = x1_ref[...] + mod_ref[0][5:6] * ffn
        ms = jnp.mean(out * out, axis=-1, keepdims=True)
        o_ref[...] = (out * lax.rsqrt(ms + EPS)) * g_ref[...]

    def apply_selection(sel, k0, width):
        lo, hi = _unpack_pair(ybuf_ref[cur, pl.ds(k0, width), :])
        return jnp.concatenate([jnp.dot(sel, lo, preferred_element_type=F32),
                                jnp.dot(sel, hi, preferred_element_type=F32)], axis=1)

    @pl.when(wide_ref[i] == 0)
    def _():
        for e in range(N_EXPERTS):
            window_copy(e, jnp.int32(0), cur).wait()
        expert_lane = lax.broadcasted_iota(I32, (1, N_EXPERTS), 1)
        starts = jnp.zeros((1, N_EXPERTS), I32)
        for e in range(N_EXPERTS):
            starts = jnp.where(expert_lane == e, start_ref[i, e], starts)
        slots = slot_ref[...]
        rows_in_window = jnp.where(slots >= 0, (slots - starts).astype(F32), -1.0)
        window_of_lane = lax.broadcasted_iota(I32, (N_EXPERTS, n_window_rows), 1) // COMBINE_WINDOW
        spread = (window_of_lane == lax.broadcasted_iota(I32, (N_EXPERTS, n_window_rows), 0))
        target = jnp.dot(rows_in_window.astype(BF16), spread.astype(BF16),
                         preferred_element_type=F32)
        lane_row = (lax.broadcasted_iota(I32, (tm, n_window_rows), 1) % COMBINE_WINDOW).astype(F32)
        sel = jnp.where(target == lane_row, 1.0, 0.0)
        ffn = None
        for k0 in range(0, n_window_rows, kblock):
            part = apply_selection(sel[:, k0:k0 + kblock], k0, kblock)
            ffn = part if ffn is None else ffn + part
        finish(ffn)

    @pl.when(wide_ref[i] != 0)
    def _():
        pos = jnp.int32(0)
        begins = []
        for e in range(N_EXPERTS):
            begins.append(pos)
            pos = pos + nch_ref[i, e] * COMBINE_CHUNK
        begins.append(pos)

        def drain(c, carry):
            chunk_copy(0, jnp.int32(0), cur, jnp.int32(0)).wait()
            return carry
        lax.fori_loop(0, pos // COMBINE_CHUNK, drain, 0)

        slots = slot_ref[...]
        for e in range(N_EXPERTS):
            s_e = slots[:, e:e + 1]
            col = jnp.where(s_e >= 0, s_e + (begins[e] - start_ref[i, e]), -1)
            cols_ref[e] = jnp.broadcast_to(col, (tm, LANES))

        lane = lax.broadcasted_iota(I32, (tm, LANES), 1)

        def selection(k0, width):
            halves = []
            for h in range(width // LANES):
                c0 = k0 + h * LANES
                target = lane + c0
                e_lo = jnp.int32(0)
                e_hi = jnp.int32(0)
                for e in range(N_EXPERTS):
                    e_lo = e_lo + (begins[e + 1] <= c0).astype(I32)
                    e_hi = e_hi + (begins[e] < c0 + LANES).astype(I32)

                def mark(e, hit, target=target):
                    return jnp.where(cols_ref[e] == target, 1.0, hit)
                halves.append(lax.fori_loop(e_lo, e_hi, mark, jnp.zeros((tm, LANES), F32)))
            return jnp.concatenate(halves, axis=1)

        ffn_ref[...] = apply_selection(selection(0, first_k), 0, first_k)

        def kstep(kb, carry):
            k0 = pl.multiple_of(kb * kblock, kblock)
            ffn_ref[...] += apply_selection(selection(k0, kblock), k0, kblock)
            return carry
        lax.fori_loop(first_k // kblock, (pos + kblock - 1) // kblock, kstep, 0)
        finish(ffn_ref[...])


def _combine(x1, mod3, mod_row, g_final, slot_t, tile_start, tile_nch, tile_wide, y):
    t = x1.shape[0]
    tm = TOKEN_TILE
    first_k = 2 * tm + N_EXPERTS * COMBINE_CHUNK
    kblock = 256
    max_rows = N_EXPERTS * (tm + 2 * COMBINE_CHUNK)
    max_rows = -(-max_rows // kblock) * kblock
    row = lambda i, *_: (i, 0)
    grid_spec = pltpu.PrefetchScalarGridSpec(
        num_scalar_prefetch=3,
        grid=(t // tm,),
        in_specs=[pl.BlockSpec((tm, D_MODEL), row),
                  pl.BlockSpec((1, N_MOD, D_MODEL), lambda i, *_: (mod_row(i), 0, 0)),
                  pl.BlockSpec((1, D_MODEL), lambda i, *_: (0, 0)),
                  pl.BlockSpec((tm, N_EXPERTS), row),
                  pl.BlockSpec(memory_space=pl.ANY)],
        out_specs=pl.BlockSpec((tm, D_MODEL), row),
        scratch_shapes=[pltpu.VMEM((2, max_rows, D_MODEL // 2), I32),
                        pltpu.VMEM((N_EXPERTS, tm, LANES), I32),
                        pltpu.VMEM((tm, D_MODEL), F32),
                        pltpu.SemaphoreType.DMA((2,))],
    )
    return pl.pallas_call(
        functools.partial(_combine_kernel, first_k=first_k, kblock=kblock),
        grid_spec=grid_spec,
        out_shape=jax.ShapeDtypeStruct((t, D_MODEL), F32),
        compiler_params=pltpu.CompilerParams(
            dimension_semantics=("arbitrary",),
            vmem_limit_bytes=_vmem_limit(2 * max_rows * D_MODEL * 2 + 16 * tm * D_MODEL * 4)),
        name="combine",
    )(tile_start, tile_nch, tile_wide, x1, mod3, g_final, slot_t, y)


def _rope_tables(n):
    rows = n // GRID_W
    row = jnp.repeat(jnp.arange(rows, dtype=F32), GRID_W)
    col = jnp.tile(jnp.arange(GRID_W, dtype=F32), rows)
    inv = ROPE_THETA ** (-jnp.arange(ROPE_FREQS, dtype=F32) / ROPE_FREQS)
    ang_r = row[:, None] * inv
    ang_c = col[:, None] * inv
    zero = jnp.zeros_like(ang_r)
    cos = jnp.concatenate([jnp.cos(ang_r)] * 2 + [jnp.cos(ang_c)] * 2, axis=1)
    sin_a = jnp.concatenate([-jnp.sin(ang_r), zero, -jnp.sin(ang_c), zero], axis=1)
    sin_b = jnp.concatenate([zero, jnp.sin(ang_r), zero, jnp.sin(ang_c)], axis=1)
    return cos, sin_a, sin_b


def _token_group(x, mod3, mod_row, seq, weights, rope_tabs, ctx_kv):
    (norm_mix, w_in_bf, sink, pool_w, pool_scale, w_out_bf, norm_ffn, w_router,
     w_gate, w_up, w_down, norm_final) = weights
    b = x.shape[0]
    t = b * seq
    x2 = x.reshape(t, D_MODEL)
    q, k, v, p, *state = _in_projection(x2, mod3, mod_row(PROJ_TILE), norm_mix, w_in_bf,
                                        rope_tabs)
    if ctx_kv is None:
        attn = _context_attention(q, k, v, sink, seq)
    else:
        attn = _latent_attention(q, k, v, ctx_kv[0], ctx_kv[1], sink, seq)
    x1, h_packed, aff = _out_projection(attn, p, x2, mod3, mod_row(PROJ_TILE), norm_ffn,
                                        w_out_bf, w_router, pool_w, pool_scale, seq)

    cap = EC_FACTOR * t // N_EXPERTS
    idx, slot_t, off3 = _routing(aff, cap)
    table = h_packed.reshape(-1, LANES)
    per_range = N_EXPERTS // FFN_RANGES
    xs = []
    for r in range(FFN_RANGES):
        ids = _packed_row_ids(idx[r * per_range:(r + 1) * per_range], cap)
        xs.append(_gather_rows(table, ids).reshape(per_range, HQ_TILES, cap, LANES))

    chunks_per_tile = TOKEN_TILE // LANES
    tile_off = off3[:, ::chunks_per_tile, 0]
    tile_end = jnp.concatenate([tile_off[:, 1:], jnp.full((N_EXPERTS, 1), cap, I32)], axis=1)
    tile_start = (tile_off // SUBLANES) * SUBLANES
    tile_nch = jnp.where(tile_end > tile_off,
                         (tile_end - tile_start + COMBINE_CHUNK - 1) // COMBINE_CHUNK, 0)
    tile_wide = jnp.any(tile_end - tile_start > COMBINE_WINDOW, axis=0).astype(I32)

    def finish(y):
        out = _combine(x1, mod3, mod_row(TOKEN_TILE), norm_final, slot_t, tile_start.T,
                       tile_nch.T, tile_wide, y)
        return out.reshape(b, seq, D_MODEL)
    return xs, finish, state


def kernel(x_prompt, x_sample, c, cache_k, cache_v, c_ctx, w_ada, b_ada, norm_mix, w_in,
           sink_logits, pool_w, pool_scale, w_out, norm_ffn, w_router, w_gate, w_up, w_down,
           norm_final):
    n_b, seq, _ = x_prompt.shape
    n_db, n_lat, _ = x_sample.shape
    assert 1 + n_db <= MOD_ROWS and seq == TOKEN_TILE and n_lat % PROJ_TILE == 0
    assert (n_b * seq) % PROJ_TILE == 0

    cond = jnp.concatenate(
        [c_ctx[None, :], c, jnp.zeros((MOD_ROWS - 1 - n_db, D_MODEL), F32)], axis=0)
    mod3 = _modulation(cond, w_ada[0], b_ada[0]).reshape(MOD_ROWS, N_MOD, D_MODEL)

    w_router_bf = jnp.pad(w_router[0], ((0, 0), (0, LANES - N_EXPERTS))).astype(BF16)
    weights = (norm_mix[0][None, :], w_in[0].astype(BF16), sink_logits[0], pool_w[0].astype(BF16),
               pool_scale[0][None, :], w_out[0].astype(BF16), norm_ffn[0][None, :], w_router_bf,
               w_gate.reshape(w_gate.shape[1:]), w_up.reshape(w_up.shape[1:]),
               w_down.reshape(w_down.shape[1:]), norm_final[None, :])

    xs_p, finish_p, (k_p, v_p) = _token_group(
        x_prompt, mod3, lambda tile: (lambda i: 0), seq, weights, None, None)

    past = cache_k.shape[2]
    ck = cache_k[:, 0].reshape(n_db, past, KV_WIDTH)
    cv = cache_v[:, 0].reshape(n_db, past, KV_WIDTH)
    xs_l, finish_l, _ = _token_group(
        x_sample, mod3, lambda tile: (lambda i: 1 + i // (n_lat // tile)), n_lat, weights,
        _rope_tables(n_lat), (ck, cv))

    ys = None
    for r in range(FFN_RANGES):
        ys = _expert_ffn([xs_p[r], xs_l[r]], weights[8], weights[9], weights[10],
                         r * (N_EXPERTS // FFN_RANGES), ys)
    y_prompt = finish_p(ys[0])
    y_sample = finish_l(ys[1])

    state_k = k_p.reshape(n_b, 1, seq, N_KV_HEADS, HEAD_DIM)
    state_v = v_p.reshape(n_b, 1, seq, N_KV_HEADS, HEAD_DIM)
    return (y_prompt, y_sample, state_k, state_v)
```

```python
import functools

import jax
import jax.numpy as jnp
from jax import lax
from jax.experimental import pallas as pl
from jax.experimental.pallas import tpu as pltpu
from jax.experimental.pallas import tpu_sc as plsc

F32 = jnp.float32
BF16 = jnp.bfloat16
I32 = jnp.int32

D_MODEL = 2048
N_HEADS = 8
N_KV_HEADS = 2
HEAD_DIM = 128
Q_PER_KV = N_HEADS // N_KV_HEADS
ATTN_WIDTH = N_HEADS * HEAD_DIM
KV_WIDTH = N_KV_HEADS * HEAD_DIM
POOL_WIDTH = D_MODEL - ATTN_WIDTH
POOL_SIZES = (2, 4, 8, 16)
POOL_GROUP = POOL_WIDTH // len(POOL_SIZES)
IN_WIDTH = ATTN_WIDTH + 2 * KV_WIDTH + POOL_WIDTH
WINDOW = 128
BLOCK = 128
GRID_W = 64
ROPE_THETA = 10000.0
ROPE_FREQS = HEAD_DIM // 4
N_EXPERTS = 16
EC_FACTOR = 2
D_EXPERT = 1024
N_MOD = 6
EPS = 1e-6
NEG = -1e30
LOG2_E = 1.4426950408889634
ATTN_SCALE = HEAD_DIM ** -0.5 * LOG2_E

LANES = 128
SUBLANES = 8
BF16_ROWS = 16
VMEM_CAP = 64 * 1024 * 1024
SC_CORES = 2
SC_SUBCORES = 16

MOD_ROWS = 8
TOKEN_TILE = 256
PROJ_TILE = 512
SUB_TILE = 256
HQ_TILES = D_MODEL // 2 // LANES + 1
GATHER_CHUNK = 128
FFN_RANGES = 2
POOL_HALO = 8
COMBINE_CHUNK = BF16_ROWS
COMBINE_WINDOW = 64


def _vmem_limit(nbytes):
    return int(min(VMEM_CAP - (4 << 20), max(nbytes, 16 << 20)))


def _mod_kernel(c_ref, w_ref, b_ref, o_ref):
    c = c_ref[...]
    s = c * jax.nn.sigmoid(c)
    o_ref[...] = jnp.dot(s.astype(BF16), w_ref[...].astype(BF16),
                         preferred_element_type=F32) + b_ref[...]


def _modulation(cond, w_ada, b_ada):
    n = w_ada.shape[1]
    tn = 1024
    return pl.pallas_call(
        _mod_kernel,
        grid=(n // tn,),
        in_specs=[pl.BlockSpec((MOD_ROWS, D_MODEL), lambda j: (0, 0)),
                  pl.BlockSpec((D_MODEL, tn), lambda j: (0, j)),
                  pl.BlockSpec((1, tn), lambda j: (0, j))],
        out_specs=pl.BlockSpec((MOD_ROWS, tn), lambda j: (0, j)),
        out_shape=jax.ShapeDtypeStruct((MOD_ROWS, n), F32),
        compiler_params=pltpu.CompilerParams(
            vmem_limit_bytes=_vmem_limit(3 * D_MODEL * tn * 4)),
        name="modulation",
    )(cond, w_ada, b_ada.reshape(1, n))


def _norm_mod(x, g, shift, scale):
    ms = jnp.mean(x * x, axis=-1, keepdims=True)
    y = x * lax.rsqrt(ms + EPS)
    return (y * g) * (1.0 + scale) + shift


def _inproj_kernel(*refs, rope):
    if rope:
        x_ref, mod_ref, g_ref, w_ref, cos_ref, sa_ref, sb_ref, q_ref, k_ref, v_ref, p_ref = refs
    else:
        x_ref, mod_ref, g_ref, w_ref, q_ref, k_ref, v_ref, p_ref, ks_ref, vs_ref = refs
    mod = mod_ref[0]
    for s in range(x_ref.shape[0] // SUB_TILE):
        rows = slice(s * SUB_TILE, (s + 1) * SUB_TILE)
        h = _norm_mod(x_ref[rows, :], g_ref[...], mod[0:1], mod[1:2])
        u = jnp.dot(h.astype(BF16), w_ref[...], preferred_element_type=F32)

        def rot(xh, rows=rows):
            return (xh * cos_ref[rows, :] + pltpu.roll(xh, LANES - ROPE_FREQS, 1) * sa_ref[rows, :]
                    + pltpu.roll(xh, ROPE_FREQS, 1) * sb_ref[rows, :])

        for hd in range(N_HEADS):
            xh = u[:, hd * HEAD_DIM:(hd + 1) * HEAD_DIM] * ATTN_SCALE
            q_ref[rows, hd * HEAD_DIM:(hd + 1) * HEAD_DIM] = (rot(xh) if rope else xh).astype(BF16)
        for hd in range(N_KV_HEADS):
            lo = ATTN_WIDTH + hd * HEAD_DIM
            xh = u[:, lo:lo + HEAD_DIM]
            k_ref[rows, hd * HEAD_DIM:(hd + 1) * HEAD_DIM] = rot(xh) if rope else xh
        v_ref[rows, :] = u[:, ATTN_WIDTH + KV_WIDTH:ATTN_WIDTH + 2 * KV_WIDTH]
        p_ref[rows, :] = u[:, ATTN_WIDTH + 2 * KV_WIDTH:]
        if not rope:
            for hd in range(N_KV_HEADS):
                state_rows = pl.ds(s * SUB_TILE * N_KV_HEADS + hd, SUB_TILE, stride=N_KV_HEADS)
                lo = ATTN_WIDTH + hd * HEAD_DIM
                ks_ref[state_rows, :] = u[:, lo:lo + HEAD_DIM]
                vs_ref[state_rows, :] = u[:, lo + KV_WIDTH:lo + KV_WIDTH + HEAD_DIM]


def _in_projection(x2, mod3, mod_row, g, w_in_bf, rope_tabs):
    t = x2.shape[0]
    tm = PROJ_TILE
    rope = rope_tabs is not None
    row = lambda i: (i, 0)
    in_specs = [pl.BlockSpec((tm, D_MODEL), row),
                pl.BlockSpec((1, N_MOD, D_MODEL), lambda i: (mod_row(i), 0, 0)),
                pl.BlockSpec((1, D_MODEL), lambda i: (0, 0)),
                pl.BlockSpec((D_MODEL, IN_WIDTH), lambda i: (0, 0))]
    args = [x2, mod3, g, w_in_bf]
    if rope:
        n_seq = rope_tabs[0].shape[0]
        seq_blocks = n_seq // tm
        for tab in rope_tabs:
            in_specs.append(pl.BlockSpec((tm, HEAD_DIM), lambda i: (i % seq_blocks, 0)))
            args.append(tab)
    out_specs = [pl.BlockSpec((tm, ATTN_WIDTH), row),
                 pl.BlockSpec((tm, KV_WIDTH), row),
                 pl.BlockSpec((tm, KV_WIDTH), row),
                 pl.BlockSpec((tm, POOL_WIDTH), row)]
    out_shape = [jax.ShapeDtypeStruct((t, ATTN_WIDTH), BF16),
                 jax.ShapeDtypeStruct((t, KV_WIDTH), F32),
                 jax.ShapeDtypeStruct((t, KV_WIDTH), F32),
                 jax.ShapeDtypeStruct((t, POOL_WIDTH), F32)]
    if not rope:
        for _ in range(2):
            out_specs.append(pl.BlockSpec((tm * N_KV_HEADS, HEAD_DIM), row))
            out_shape.append(jax.ShapeDtypeStruct((t * N_KV_HEADS, HEAD_DIM), F32))
    return pl.pallas_call(
        functools.partial(_inproj_kernel, rope=rope),
        grid=(t // tm,),
        in_specs=in_specs,
        out_specs=out_specs,
        out_shape=out_shape,
        compiler_params=pltpu.CompilerParams(
            vmem_limit_bytes=_vmem_limit(2 * D_MODEL * IN_WIDTH * 2 + 24 * tm * D_MODEL * 4)),
        name="in_projection",
    )(*args)


def _softmax_pv(s_list, v_list, sink_col):
    m = sink_col
    for s in s_list:
        m = jnp.maximum(m, jnp.max(s, axis=-1, keepdims=True))
    denom = jnp.exp2(sink_col - m)
    out = None
    for s, v in zip(s_list, v_list):
        e = jnp.exp2(s - m)
        if v.shape[1] == HEAD_DIM:
            denom = denom + jnp.sum(e, axis=-1, keepdims=True)
        o = jnp.dot(e.astype(BF16), v, preferred_element_type=F32)
        out = o if out is None else out + o
    if out.shape[1] > HEAD_DIM:
        denom = denom + out[:, HEAD_DIM:HEAD_DIM + 1]
    return out[:, 0:HEAD_DIM] * (1.0 / denom)


def _with_ones(v):
    return jnp.concatenate([v.astype(BF16), jnp.ones(v.shape, BF16)], axis=1)


def _stack_heads(q, kv):
    return jnp.concatenate(
        [q[:, (kv * Q_PER_KV + g) * HEAD_DIM:(kv * Q_PER_KV + g + 1) * HEAD_DIM]
         for g in range(Q_PER_KV)], axis=0)


def _sink_column(sink_ref, kv, rows):
    r = lax.broadcasted_iota(I32, (Q_PER_KV * rows, 1), 0)
    col = jnp.zeros((Q_PER_KV * rows, 1), F32)
    for g in range(Q_PER_KV):
        col = jnp.where((r >= g * rows) & (r < (g + 1) * rows), sink_ref[kv * Q_PER_KV + g], col)
    return col * LOG2_E


def _qk(q, k):
    return lax.dot_general(q, k, (((1,), (1,)), ((), ())), preferred_element_type=F32)


def _ctx_attn_kernel(sink_ref, q_ref, k_ref, v_ref, o_ref):
    rows = q_ref.shape[0]
    q = q_ref[...]
    for kv in range(N_KV_HEADS):
        kh = k_ref[:, kv * HEAD_DIM:(kv + 1) * HEAD_DIM].astype(BF16)
        vh = v_ref[:, kv * HEAD_DIM:(kv + 1) * HEAD_DIM].astype(BF16)
        qs = _stack_heads(q, kv)
        s = _qk(qs, kh)
        o = _softmax_pv([s], [vh], _sink_column(sink_ref, kv, rows))
        for g in range(Q_PER_KV):
            hd = kv * Q_PER_KV + g
            o_ref[:, hd * HEAD_DIM:(hd + 1) * HEAD_DIM] = o[g * rows:(g + 1) * rows].astype(BF16)


def _context_attention(q, k, v, sink, seq):
    t = q.shape[0]
    row = lambda b: (b, 0)
    return pl.pallas_call(
        _ctx_attn_kernel,
        grid=(t // seq,),
        in_specs=[pl.BlockSpec(memory_space=pltpu.SMEM),
                  pl.BlockSpec((seq, ATTN_WIDTH), row),
                  pl.BlockSpec((seq, KV_WIDTH), row),
                  pl.BlockSpec((seq, KV_WIDTH), row)],
        out_specs=pl.BlockSpec((seq, ATTN_WIDTH), row),
        out_shape=jax.ShapeDtypeStruct((t, ATTN_WIDTH), BF16),
        name="context_attention",
    )(sink, q, k, v)


def _lat_attn_kernel(sink_ref, q_ref, k_ref, v_ref, ck_ref, cv_ref, o_ref, *, n_seq):
    i = pl.program_id(1)
    band = 3 * BLOCK
    start = pl.multiple_of(jnp.clip((i - 1) * BLOCK, 0, n_seq - band), BLOCK)
    rows = Q_PER_KV * BLOCK
    qpos = i * BLOCK + lax.broadcasted_iota(I32, (rows, band), 0) % BLOCK
    kpos = start + lax.broadcasted_iota(I32, (rows, band), 1)
    mask = jnp.abs(kpos - qpos) <= WINDOW
    q = q_ref[...]
    for kv in range(N_KV_HEADS):
        cols = slice(kv * HEAD_DIM, (kv + 1) * HEAD_DIM)
        kb = k_ref[pl.ds(start, band), cols].astype(BF16)
        vb = _with_ones(v_ref[pl.ds(start, band), cols])
        ck = ck_ref[0, :, cols].astype(BF16)
        cv = _with_ones(cv_ref[0, :, cols])
        qs = _stack_heads(q, kv)
        s_loc = jnp.where(mask, _qk(qs, kb), NEG)
        s_ctx = _qk(qs, ck)
        o = _softmax_pv([s_loc, s_ctx], [vb, cv], _sink_column(sink_ref, kv, BLOCK))
        for g in range(Q_PER_KV):
            hd = kv * Q_PER_KV + g
            o_ref[:, hd * HEAD_DIM:(hd + 1) * HEAD_DIM] = o[g * BLOCK:(g + 1) * BLOCK].astype(BF16)


def _latent_attention(q, k, v, ck, cv, sink, n_seq):
    t = q.shape[0]
    nb = n_seq // BLOCK
    past = ck.shape[1]
    return pl.pallas_call(
        functools.partial(_lat_attn_kernel, n_seq=n_seq),
        grid=(t // n_seq, nb),
        in_specs=[pl.BlockSpec(memory_space=pltpu.SMEM),
                  pl.BlockSpec((BLOCK, ATTN_WIDTH), lambda b, i: (b * nb + i, 0)),
                  pl.BlockSpec((n_seq, KV_WIDTH), lambda b, i: (b, 0)),
                  pl.BlockSpec((n_seq, KV_WIDTH), lambda b, i: (b, 0)),
                  pl.BlockSpec((1, past, KV_WIDTH), lambda b, i: (b, 0, 0)),
                  pl.BlockSpec((1, past, KV_WIDTH), lambda b, i: (b, 0, 0))],
        out_specs=pl.BlockSpec((BLOCK, ATTN_WIDTH), lambda b, i: (b * nb + i, 0)),
        out_shape=jax.ShapeDtypeStruct((t, ATTN_WIDTH), BF16),
        name="latent_attention",
    )(sink, q, k, v, ck, cv)


def _pool_group(p_ref, r0, seq, w_ref, s_ref, g):
    n = SUB_TILE
    rows = n + 2 * POOL_HALO
    static = isinstance(r0, int)
    t0 = r0 % seq
    t = t0 + lax.broadcasted_iota(I32, (n, 1), 0)
    has_top = t0 > 0
    has_bottom = t0 + n < seq
    zeros = jnp.zeros((POOL_HALO, POOL_GROUP), F32)
    w = POOL_SIZES[g]
    cols = slice(g * POOL_GROUP, (g + 1) * POOL_GROUP)
    pg = p_ref[pl.ds(r0, n), cols]
    if static:
        top = p_ref[r0 - POOL_HALO:r0, cols] if has_top else zeros
        bottom = p_ref[r0 + n:r0 + n + POOL_HALO, cols] if has_bottom else zeros
    else:
        top_row = pl.multiple_of(jnp.maximum(r0 - POOL_HALO, 0), POOL_HALO)
        bottom_row = pl.multiple_of(jnp.minimum(r0 + n, p_ref.shape[0] - POOL_HALO), POOL_HALO)
        top = jnp.where(has_top, p_ref[pl.ds(top_row, POOL_HALO), cols], 0.0)
        bottom = jnp.where(has_bottom, p_ref[pl.ds(bottom_row, POOL_HALO), cols], 0.0)
    x = jnp.concatenate([top, pg, bottom], axis=0)
    fwd = x
    span = 1
    while span < w // 2:
        fwd = fwd + pltpu.roll(fwd, rows - span, 0)
        span *= 2
    if (w // 2) % SUBLANES == 0:
        wsum = fwd[POOL_HALO - w // 2:POOL_HALO - w // 2 + n] + fwd[POOL_HALO:POOL_HALO + n]
    else:
        wsum = (fwd + pltpu.roll(fwd, w // 2, 0))[POOL_HALO:POOL_HALO + n]
    lo = jnp.maximum(t - w // 2, 0)
    hi = jnp.minimum(t + w - w // 2, seq)
    inv_cnt = 1.0 / (hi - lo).astype(F32)
    mixed = wsum * inv_cnt - pg
    y = jnp.dot(mixed.astype(BF16), w_ref[g], preferred_element_type=F32)
    return (y * s_ref[:, cols]).astype(BF16)


def _pack_pair(lo, hi):
    return lax.bitcast_convert_type(pltpu.pack_elementwise([lo, hi], packed_dtype=BF16), I32)


def _unpack_pair(words):
    lo = pltpu.unpack_elementwise(words, index=0, packed_dtype=BF16, unpacked_dtype=F32)
    hi = pltpu.unpack_elementwise(words, index=1, packed_dtype=BF16, unpacked_dtype=F32)
    return lo, hi


def _outproj_kernel(a_ref, p_ref, x_ref, mod_ref, g_ref, wo_ref, wr_ref, pw_ref, ps_ref,
                    x1_ref, h_ref, aff_ref, *, seq):
    mod = mod_ref[0]
    groups = SUB_TILE // SUBLANES
    half = D_MODEL // 2
    tm = x_ref.shape[0]
    steps_per_p_block = p_ref.shape[0] // tm
    for s in range(tm // SUB_TILE):
        rows = slice(s * SUB_TILE, (s + 1) * SUB_TILE)
        grp = slice(s * groups, (s + 1) * groups)
        r0 = s * SUB_TILE
        if steps_per_p_block > 1:
            r0 = pl.multiple_of((pl.program_id(0) % steps_per_p_block) * tm + r0, SUB_TILE)
        pooled = jnp.concatenate([_pool_group(p_ref, r0, seq, pw_ref, ps_ref, g)
                                  for g in range(len(POOL_SIZES))], axis=1)
        mix = (jnp.dot(a_ref[rows, :], wo_ref[0:ATTN_WIDTH, :], preferred_element_type=F32)
               + jnp.dot(pooled, wo_ref[ATTN_WIDTH:D_MODEL, :], preferred_element_type=F32))
        x1 = x_ref[rows, :] + mod[2:3] * mix
        x1_ref[rows, :] = x1
        h = _norm_mod(x1, g_ref[...], mod[3:4], mod[4:5])
        logits = jnp.dot(h.astype(BF16), wr_ref[...], preferred_element_type=F32)
        lane = lax.broadcasted_iota(I32, logits.shape, 1)
        logits = jnp.where(lane < N_EXPERTS, logits, -jnp.inf)
        m = jnp.max(logits, axis=-1, keepdims=True)
        e = jnp.exp(logits - m)
        aff = e / jnp.sum(e, axis=-1, keepdims=True)
        aff_ref[rows, :] = aff[:, 0:N_EXPERTS]
        for c in range(HQ_TILES - 1):
            words = _pack_pair(h[:, c * LANES:(c + 1) * LANES],
                               h[:, half + c * LANES:half + (c + 1) * LANES])
            h_ref[grp, c * SUBLANES:(c + 1) * SUBLANES, :] = (
                words.reshape(groups, SUBLANES, LANES))
        h_ref[grp, (HQ_TILES - 1) * SUBLANES:, :] = (
            lax.bitcast_convert_type(aff, I32).reshape(groups, SUBLANES, LANES))


def _out_projection(attn, p, x2, mod3, mod_row, g, w_out_bf, w_router, pool_w, pool_scale, seq):
    t = x2.shape[0]
    tm = PROJ_TILE
    row = lambda i: (i, 0)
    p_rows = max(tm, seq)
    steps_per_p_block = p_rows // tm
    return pl.pallas_call(
        functools.partial(_outproj_kernel, seq=seq),
        grid=(t // tm,),
        in_specs=[pl.BlockSpec((tm, ATTN_WIDTH), row),
                  pl.BlockSpec((p_rows, POOL_WIDTH), lambda i: (i // steps_per_p_block, 0)),
                  pl.BlockSpec((tm, D_MODEL), row),
                  pl.BlockSpec((1, N_MOD, D_MODEL), lambda i: (mod_row(i), 0, 0)),
                  pl.BlockSpec((1, D_MODEL), lambda i: (0, 0)),
                  pl.BlockSpec((D_MODEL, D_MODEL), lambda i: (0, 0)),
                  pl.BlockSpec((D_MODEL, LANES), lambda i: (0, 0)),
                  pl.BlockSpec((len(POOL_SIZES), POOL_GROUP, POOL_GROUP), lambda i: (0, 0, 0)),
                  pl.BlockSpec((1, POOL_WIDTH), lambda i: (0, 0))],
        out_specs=[pl.BlockSpec((tm, D_MODEL), row),
                   pl.BlockSpec((tm // SUBLANES, HQ_TILES * SUBLANES, LANES), lambda i: (i, 0, 0)),
                   pl.BlockSpec((tm, N_EXPERTS), row)],
        out_shape=[jax.ShapeDtypeStruct((t, D_MODEL), F32),
                   jax.ShapeDtypeStruct((t // SUBLANES, HQ_TILES * SUBLANES, LANES), I32),
                   jax.ShapeDtypeStruct((t, N_EXPERTS), F32)],
        compiler_params=pltpu.CompilerParams(
            vmem_limit_bytes=_vmem_limit(2 * D_MODEL * D_MODEL * 2 + 24 * tm * D_MODEL * 4)),
        name="out_projection",
    )(attn, p, x2, mod3, g, w_out_bf, w_router, pool_w, pool_scale)


def _route_kernel(a_ref, idx_ref, slot_ref, off_ref, slot_scr, *, cap):
    a = a_ref[...]
    n_e, n_c, _ = a.shape
    rows = n_e * n_c

    def enough(cand):
        cand_f = lax.bitcast_convert_type(cand, F32)
        cnt = jnp.sum(jnp.sum((a >= cand_f).astype(F32), axis=1, keepdims=True),
                      axis=2, keepdims=True)
        return cnt >= cap

    def two_bits(it, thr):
        low = 28 - 2 * it
        for setting in (1, 2, 3):
            cand = thr | jnp.left_shift(jnp.int32(setting), low)
            best = jnp.where(enough(cand), cand, thr if setting == 1 else best)
        return best

    top = jnp.full((n_e, 1, 1), 1 << 30, I32)
    thr = jnp.where(enough(top), top, 0)
    thr = lax.fori_loop(0, 15, two_bits, thr)
    thr_f = lax.bitcast_convert_type(thr, F32)
    gt = (a > thr_f).astype(F32).reshape(rows, LANES)
    eq = (a == thr_f).astype(F32).reshape(rows, LANES)

    li = lax.broadcasted_iota(I32, (LANES, LANES), 0)
    lj = lax.broadcasted_iota(I32, (LANES, LANES), 1)
    upper_incl = (li <= lj).astype(BF16)
    ci = lax.broadcasted_iota(I32, (n_c, n_c), 0)
    cj = lax.broadcasted_iota(I32, (n_c, n_c), 1)
    before = (cj < ci).astype(BF16)
    whole = jnp.ones((n_c, n_c), BF16)

    def over_chunks(mat, col):
        wide = jnp.broadcast_to(col, (rows, LANES)).astype(BF16)
        side_by_side = jnp.concatenate([wide[e * n_c:(e + 1) * n_c] for e in range(n_e)], axis=1)
        res = jnp.dot(mat, side_by_side, preferred_element_type=F32)
        return jnp.concatenate([res[:, e * LANES:(e + 1) * LANES] for e in range(n_e)], axis=0)

    def prefix(x):
        incl = jnp.dot(x.astype(BF16), upper_incl, preferred_element_type=F32)
        tot = incl[:, LANES - 1:LANES]
        return incl, tot, over_chunks(before, tot)

    n_gt = over_chunks(whole, jnp.sum(gt, axis=1, keepdims=True))
    need = cap - n_gt
    incl_eq, _, off_eq = prefix(eq)
    rank_eq = off_eq + incl_eq - eq
    sel = jnp.where((eq > 0) & (rank_eq < need), 1.0, gt)
    incl, tot, off = prefix(sel)
    slot = off + incl - sel
    slot_scr[...] = jnp.where(sel > 0, slot, -1.0)
    for c in range(n_c):
        per_expert = slot_scr[pl.ds(c, n_e, stride=n_c), :]
        slot_ref[c * LANES:(c + 1) * LANES, :] = per_expert.T.astype(I32)
    off_ref[...] = off.astype(I32).reshape(n_e, n_c, LANES)

    s_lane = lax.broadcasted_iota(I32, (1, cap), 1).astype(F32)
    c_col = lax.broadcasted_iota(I32, (n_c, 1), 0).astype(F32)
    for e in range(n_e):
        r0 = e * n_c
        incl_e = incl[r0:r0 + n_c]
        off_e = off[r0:r0 + n_c, 0:1]
        tot_e = tot[r0:r0 + n_c]
        onehot = ((off_e <= s_lane) & (s_lane < off_e + tot_e)).astype(F32)
        counts = lax.dot_general(incl_e.astype(BF16), onehot.astype(BF16),
                                 (((0,), (0,)), ((), ())), preferred_element_type=F32)
        local = s_lane - jnp.sum(onehot * off_e, axis=0, keepdims=True)
        lane = jnp.sum((counts <= local).astype(F32), axis=0, keepdims=True)
        chunk = jnp.sum(onehot * c_col, axis=0, keepdims=True)
        idx_ref[e] = (chunk * LANES + lane).astype(I32)


def _routing(aff, cap):
    t = aff.shape[0]
    n_c = t // LANES
    a3 = aff.T.reshape(N_EXPERTS, n_c, LANES)
    return pl.pallas_call(
        functools.partial(_route_kernel, cap=cap),
        out_shape=[jax.ShapeDtypeStruct((N_EXPERTS, 1, cap), I32),
                   jax.ShapeDtypeStruct((t, N_EXPERTS), I32),
                   jax.ShapeDtypeStruct((N_EXPERTS, n_c, LANES), I32)],
        scratch_shapes=[pltpu.VMEM((N_EXPERTS * n_c, LANES), F32)],
        compiler_params=pltpu.CompilerParams(vmem_limit_bytes=_vmem_limit(48 << 20)),
        name="routing",
    )(a3)


def _gather_rows(table, row_ids):
    n_chunks = row_ids.shape[0]
    n_workers = SC_CORES * SC_SUBCORES
    per_worker = n_chunks // n_workers
    assert row_ids.shape[1] == GATHER_CHUNK and n_chunks % n_workers == 0
    mesh = plsc.VectorSubcoreMesh(core_axis_name="core", subcore_axis_name="subcore")

    @functools.partial(
        pl.kernel, mesh=mesh,
        out_type=jax.ShapeDtypeStruct((n_chunks * GATHER_CHUNK, LANES), I32),
        scratch_types=[pltpu.VMEM((per_worker, GATHER_CHUNK), I32),
                       pltpu.VMEM((2, GATHER_CHUNK, LANES), I32),
                       pltpu.SemaphoreType.DMA((2,)),
                       pltpu.SemaphoreType.DMA((2,))],
        name="gather_rows",
    )
    def gather(table_hbm, ids_hbm, out_hbm, ids_v, rows_v, gather_sem, store_sem):
        worker = lax.axis_index("subcore") * SC_CORES + lax.axis_index("core")
        first = worker * per_worker
        pltpu.sync_copy(ids_hbm.at[worker], ids_v)

        def fetch(j):
            return pltpu.make_async_copy(table_hbm.at[ids_v.at[j]], rows_v.at[j % 2],
                                         gather_sem.at[j % 2])

        def store(j):
            rows = pl.ds(pl.multiple_of((first + j) * GATHER_CHUNK, GATHER_CHUNK), GATHER_CHUNK)
            return pltpu.make_async_copy(rows_v.at[j % 2], out_hbm.at[rows], store_sem.at[j % 2])

        for j in range(per_worker):
            if j >= 2:
                store(j - 2).wait()
            fetch(j).start()
            if j >= 1:
                fetch(j - 1).wait()
                store(j - 1).start()
        fetch(per_worker - 1).wait()
        store(per_worker - 1).start()
        for j in range(max(per_worker - 2, 0), per_worker):
            store(j).wait()

    return gather(table, row_ids.reshape(n_workers, per_worker, GATHER_CHUNK))


def _packed_row_ids(idx, cap):
    tok = idx.reshape(-1, 1, cap // GATHER_CHUNK, GATHER_CHUNK)
    tile = jnp.arange(HQ_TILES, dtype=I32).reshape(1, HQ_TILES, 1, 1)
    ids = ((tok // SUBLANES) * HQ_TILES + tile) * SUBLANES + tok % SUBLANES
    return ids.reshape(-1, GATHER_CHUNK)


def _ffn_kernel(*refs, caps, n_f, row_chunk, first_expert, n_prior):
    n_g = len(caps)
    x_refs = refs[:n_g]
    wg_ref, wu_ref, wd_ref = refs[n_g:n_g + 3]
    y_refs = refs[n_g + 3 + n_prior:2 * n_g + 3 + n_prior]
    acc_ref = refs[2 * n_g + 3 + n_prior]
    e = pl.program_id(0)
    f = pl.program_id(1)

    @pl.when((e == 0) & (f == 0))
    def _():
        acc_ref[...] = jnp.zeros_like(acc_ref)

    def step(last):
        first = f == 0
        base = 0
        for x_ref, y_ref, cap in zip(x_refs, y_refs, caps):
            if last:
                aff = lax.bitcast_convert_type(x_ref[0, HQ_TILES - 1], F32)
                lane = lax.broadcasted_iota(I32, aff.shape, 1)
                gates = jnp.sum(jnp.where(lane == e + first_expert, aff, 0.0), axis=1,
                                keepdims=True)
                y_ref[0, cap:, :] = jnp.zeros((COMBINE_WINDOW, D_MODEL // 2), I32)
            for r in range(cap // row_chunk):
                rs = slice(r * row_chunk, (r + 1) * row_chunk)
                acc_rows = slice(base + r * row_chunk, base + (r + 1) * row_chunk)
                pairs = [_unpack_pair(x_ref[0, c, rs, :]) for c in range(HQ_TILES - 1)]
                x = jnp.concatenate([p[0] for p in pairs] + [p[1] for p in pairs], axis=1)
                gate_act = jnp.dot(x, wg_ref[0], preferred_element_type=F32)
                up = jnp.dot(x, wu_ref[0], preferred_element_type=F32)
                hid = (gate_act * jax.nn.sigmoid(gate_act)) * up
                part = jnp.dot(hid, wd_ref[0], preferred_element_type=F32)
                if last:
                    y = (part + acc_ref[acc_rows, :]) * gates[rs]
                    y_ref[0, rs, :] = _pack_pair(y[:, 0:D_MODEL // 2], y[:, D_MODEL // 2:])
                else:
                    acc_ref[acc_rows, :] = part + jnp.where(first, 0.0, acc_ref[acc_rows, :])
            base += cap

    @pl.when(f < n_f - 1)
    def _():
        step(False)

    @pl.when(f == n_f - 1)
    def _():
        step(True)


def _expert_ffn(xs_groups, w_gate, w_up, w_down, first_expert, prior_outputs):
    n_f = 4
    tf = D_EXPERT // n_f
    row_chunk = 512
    n_e = xs_groups[0].shape[0]
    e0 = first_expert
    caps = tuple(xs.shape[2] for xs in xs_groups)
    rows = sum(caps)
    est = (2 * HQ_TILES * rows * LANES * 4 + rows * D_MODEL * (4 + 2 * 2) + 2 * 3 * D_MODEL * tf * 4
           + row_chunk * (3 * tf + 2 * D_MODEL) * 4)
    x_specs = [pl.BlockSpec((1, HQ_TILES, cap, LANES), lambda e, f: (e, 0, 0, 0)) for cap in caps]
    y_shapes = [(N_EXPERTS, cap + COMBINE_WINDOW, D_MODEL // 2) for cap in caps]
    y_specs = [pl.BlockSpec((1,) + s[1:], lambda e, f: (e + e0, 0, 0)) for s in y_shapes]
    prior = list(prior_outputs or [])
    n_in = len(caps) + 3
    return pl.pallas_call(
        functools.partial(_ffn_kernel, caps=caps, n_f=n_f, row_chunk=row_chunk,
                          first_expert=e0, n_prior=len(prior)),
        grid=(n_e, n_f),
        in_specs=(x_specs + [pl.BlockSpec((1, D_MODEL, tf), lambda e, f: (e + e0, 0, f)),
                             pl.BlockSpec((1, D_MODEL, tf), lambda e, f: (e + e0, 0, f)),
                             pl.BlockSpec((1, tf, D_MODEL), lambda e, f: (e + e0, f, 0))]
                  + [pl.BlockSpec(memory_space=pl.ANY) for _ in prior]),
        out_specs=y_specs,
        out_shape=[jax.ShapeDtypeStruct(s, I32) for s in y_shapes],
        input_output_aliases={n_in + j: j for j in range(len(prior))},
        scratch_shapes=[pltpu.VMEM((rows, D_MODEL), F32)],
        compiler_params=pltpu.CompilerParams(
            dimension_semantics=("arbitrary", "arbitrary"),
            vmem_limit_bytes=_vmem_limit(est + (4 << 20))),
        name="expert_ffn",
    )(*xs_groups, w_gate, w_up, w_down, *prior)


def _combine_kernel(start_ref, nch_ref, wide_ref, x1_ref, mod_ref, g_ref, slot_ref, y_hbm, o_ref,
                    ybuf_ref, cols_ref, ffn_ref, sem, *, first_k, kblock):
    i = pl.program_id(0)
    tm = x1_ref.shape[0]
    cur = i % 2
    n_window_rows = N_EXPERTS * COMBINE_WINDOW

    window_head = COMBINE_WINDOW - COMBINE_CHUNK

    def window_copy(e, src_row, buf):
        return pltpu.make_async_copy(
            y_hbm.at[e, pl.ds(pl.multiple_of(src_row, SUBLANES), window_head)],
            ybuf_ref.at[buf, e * COMBINE_WINDOW:e * COMBINE_WINDOW + window_head], sem.at[buf])

    def window_tail_copy(e, src_row, buf):
        return chunk_copy(e, src_row + window_head, buf,
                          jnp.int32(e * COMBINE_WINDOW + window_head))

    def needs_tail(tile, e):
        return nch_ref[tile, e] * COMBINE_CHUNK > window_head

    def chunk_copy(e, src_row, buf, dst_row):
        return pltpu.make_async_copy(
            y_hbm.at[e, pl.ds(pl.multiple_of(src_row, SUBLANES), COMBINE_CHUNK)],
            ybuf_ref.at[buf, pl.ds(pl.multiple_of(dst_row, COMBINE_CHUNK), COMBINE_CHUNK)],
            sem.at[buf])

    def fetch(tile, buf):
        @pl.when(wide_ref[tile] == 0)
        def _():
            for e in range(N_EXPERTS):
                window_copy(e, start_ref[tile, e], buf).start()

                @pl.when(needs_tail(tile, e))
                def _(e=e):
                    window_tail_copy(e, start_ref[tile, e], buf).start()

        @pl.when(wide_ref[tile] != 0)
        def _():
            pos = jnp.int32(0)
            for e in range(N_EXPERTS):
                start = start_ref[tile, e]
                nch = nch_ref[tile, e]

                def issue(c, carry, e=e, start=start, pos=pos):
                    chunk_copy(e, start + c * COMBINE_CHUNK, buf, pos + c * COMBINE_CHUNK).start()
                    return carry
                lax.fori_loop(0, nch, issue, 0)
                pos = pos + nch * COMBINE_CHUNK

    @pl.when(i == 0)
    def _():
        ybuf_ref[...] = jnp.zeros_like(ybuf_ref)
        fetch(0, 0)

    @pl.when(i + 1 < pl.num_programs(0))
    def _():
        fetch(i + 1, 1 - cur)

    def finish(ffn):
        out = x1_ref[...] + mod_ref[0][5:6] * ffn
        ms = jnp.mean(out * out, axis=-1, keepdims=True)
        o_ref[...] = (out * lax.rsqrt(ms + EPS)) * g_ref[...]

    def apply_selection(sel, k0, width):
        lo, hi = _unpack_pair(ybuf_ref[cur, pl.ds(k0, width), :])
        return jnp.concatenate([jnp.dot(sel, lo, preferred_element_type=F32),
                                jnp.dot(sel, hi, preferred_element_type=F32)], axis=1)

    @pl.when(wide_ref[i] == 0)
    def _():
        for e in range(N_EXPERTS):
            window_copy(e, jnp.int32(0), cur).wait()

            @pl.when(needs_tail(i, e))
            def _(e=e):
                window_tail_copy(e, jnp.int32(0), cur).wait()
        expert_lane = lax.broadcasted_iota(I32, (1, N_EXPERTS), 1)
        starts = jnp.zeros((1, N_EXPERTS), I32)
        for e in range(N_EXPERTS):
            starts = jnp.where(expert_lane == e, start_ref[i, e], starts)
        slots = slot_ref[...]
        rows_in_window = jnp.where(slots >= 0, (slots - starts).astype(F32), -1.0)
        window_of_lane = lax.broadcasted_iota(I32, (N_EXPERTS, n_window_rows), 1) // COMBINE_WINDOW
        spread = (window_of_lane == lax.broadcasted_iota(I32, (N_EXPERTS, n_window_rows), 0))
        target = jnp.dot(rows_in_window.astype(BF16), spread.astype(BF16),
                         preferred_element_type=F32)
        lane_row = (lax.broadcasted_iota(I32, (tm, n_window_rows), 1) % COMBINE_WINDOW).astype(F32)
        sel = jnp.where(target == lane_row, 1.0, 0.0)
        ffn = None
        for k0 in range(0, n_window_rows, kblock):
            part = apply_selection(sel[:, k0:k0 + kblock], k0, kblock)
            ffn = part if ffn is None else ffn + part
        finish(ffn)

    @pl.when(wide_ref[i] != 0)
    def _():
        pos = jnp.int32(0)
        begins = []
        for e in range(N_EXPERTS):
            begins.append(pos)
            pos = pos + nch_ref[i, e] * COMBINE_CHUNK
        begins.append(pos)

        def drain(c, carry):
            chunk_copy(0, jnp.int32(0), cur, jnp.int32(0)).wait()
            return carry
        lax.fori_loop(0, pos // COMBINE_CHUNK, drain, 0)

        slots = slot_ref[...]
        for e in range(N_EXPERTS):
            s_e = slots[:, e:e + 1]
            col = jnp.where(s_e >= 0, s_e + (begins[e] - start_ref[i, e]), -1)
            cols_ref[e] = jnp.broadcast_to(col, (tm, LANES))

        lane = lax.broadcasted_iota(I32, (tm, LANES), 1)

        def selection(k0, width):
            halves = []
            for h in range(width // LANES):
                c0 = k0 + h * LANES
                target = lane + c0
                e_lo = jnp.int32(0)
                e_hi = jnp.int32(0)
                for e in range(N_EXPERTS):
                    e_lo = e_lo + (begins[e + 1] <= c0).astype(I32)
                    e_hi = e_hi + (begins[e] < c0 + LANES).astype(I32)

                def mark(e, hit, target=target):
                    return jnp.where(cols_ref[e] == target, 1.0, hit)
                halves.append(lax.fori_loop(e_lo, e_hi, mark, jnp.zeros((tm, LANES), F32)))
            return jnp.concatenate(halves, axis=1)

        ffn_ref[...] = apply_selection(selection(0, first_k), 0, first_k)

        def kstep(kb, carry):
            k0 = pl.multiple_of(kb * kblock, kblock)
            ffn_ref[...] += apply_selection(selection(k0, kblock), k0, kblock)
            return carry
        lax.fori_loop(first_k // kblock, (pos + kblock - 1) // kblock, kstep, 0)
        finish(ffn_ref[...])


def _combine(x1, mod3, mod_row, g_final, slot_t, tile_start, tile_nch, tile_wide, y):
    t = x1.shape[0]
    tm = TOKEN_TILE
    first_k = 2 * tm + N_EXPERTS * COMBINE_CHUNK
    kblock = 256
    max_rows = N_EXPERTS * (tm + 2 * COMBINE_CHUNK)
    max_rows = -(-max_rows // kblock) * kblock
    row = lambda i, *_: (i, 0)
    grid_spec = pltpu.PrefetchScalarGridSpec(
        num_scalar_prefetch=3,
        grid=(t // tm,),
        in_specs=[pl.BlockSpec((tm, D_MODEL), row),
                  pl.BlockSpec((1, N_MOD, D_MODEL), lambda i, *_: (mod_row(i), 0, 0)),
                  pl.BlockSpec((1, D_MODEL), lambda i, *_: (0, 0)),
                  pl.BlockSpec((tm, N_EXPERTS), row),
                  pl.BlockSpec(memory_space=pl.ANY)],
        out_specs=pl.BlockSpec((tm, D_MODEL), row),
        scratch_shapes=[pltpu.VMEM((2, max_rows, D_MODEL // 2), I32),
                        pltpu.VMEM((N_EXPERTS, tm, LANES), I32),
                        pltpu.VMEM((tm, D_MODEL), F32),
                        pltpu.SemaphoreType.DMA((2,))],
    )
    return pl.pallas_call(
        functools.partial(_combine_kernel, first_k=first_k, kblock=kblock),
        grid_spec=grid_spec,
        out_shape=jax.ShapeDtypeStruct((t, D_MODEL), F32),
        compiler_params=pltpu.CompilerParams(
            dimension_semantics=("arbitrary",),
            vmem_limit_bytes=_vmem_limit(2 * max_rows * D_MODEL * 2 + 16 * tm * D_MODEL * 4)),
        name="combine",
    )(tile_start, tile_nch, tile_wide, x1, mod3, g_final, slot_t, y)


def _rope_tables(n):
    rows = n // GRID_W
    row = jnp.repeat(jnp.arange(rows, dtype=F32), GRID_W)
    col = jnp.tile(jnp.arange(GRID_W, dtype=F32), rows)
    inv = ROPE_THETA ** (-jnp.arange(ROPE_FREQS, dtype=F32) / ROPE_FREQS)
    ang_r = row[:, None] * inv
    ang_c = col[:, None] * inv
    zero = jnp.zeros_like(ang_r)
    cos = jnp.concatenate([jnp.cos(ang_r)] * 2 + [jnp.cos(ang_c)] * 2, axis=1)
    sin_a = jnp.concatenate([-jnp.sin(ang_r), zero, -jnp.sin(ang_c), zero], axis=1)
    sin_b = jnp.concatenate([zero, jnp.sin(ang_r), zero, jnp.sin(ang_c)], axis=1)
    return cos, sin_a, sin_b


def _token_group(x, mod3, mod_row, seq, weights, rope_tabs, ctx_kv):
    (norm_mix, w_in_bf, sink, pool_w, pool_scale, w_out_bf, norm_ffn, w_router,
     w_gate, w_up, w_down, norm_final) = weights
    b = x.shape[0]
    t = b * seq
    x2 = x.reshape(t, D_MODEL)
    q, k, v, p, *state = _in_projection(x2, mod3, mod_row(PROJ_TILE), norm_mix, w_in_bf,
                                        rope_tabs)
    if ctx_kv is None:
        attn = _context_attention(q, k, v, sink, seq)
    else:
        attn = _latent_attention(q, k, v, ctx_kv[0], ctx_kv[1], sink, seq)
    x1, h_packed, aff = _out_projection(attn, p, x2, mod3, mod_row(PROJ_TILE), norm_ffn,
                                        w_out_bf, w_router, pool_w, pool_scale, seq)

    cap = EC_FACTOR * t // N_EXPERTS
    idx, slot_t, off3 = _routing(aff, cap)
    table = h_packed.reshape(-1, LANES)
    per_range = N_EXPERTS // FFN_RANGES
    xs = []
    for r in range(FFN_RANGES):
        ids = _packed_row_ids(idx[r * per_range:(r + 1) * per_range], cap)
        xs.append(_gather_rows(table, ids).reshape(per_range, HQ_TILES, cap, LANES))

    chunks_per_tile = TOKEN_TILE // LANES
    tile_off = off3[:, ::chunks_per_tile, 0]
    tile_end = jnp.concatenate([tile_off[:, 1:], jnp.full((N_EXPERTS, 1), cap, I32)], axis=1)
    tile_start = (tile_off // SUBLANES) * SUBLANES
    tile_nch = jnp.where(tile_end > tile_off,
                         (tile_end - tile_start + COMBINE_CHUNK - 1) // COMBINE_CHUNK, 0)
    tile_wide = jnp.any(tile_end - tile_start > COMBINE_WINDOW, axis=0).astype(I32)

    def finish(y):
        out = _combine(x1, mod3, mod_row(TOKEN_TILE), norm_final, slot_t, tile_start.T,
                       tile_nch.T, tile_wide, y)
        return out.reshape(b, seq, D_MODEL)
    return xs, finish, state


def kernel(x_prompt, x_sample, c, cache_k, cache_v, c_ctx, w_ada, b_ada, norm_mix, w_in,
           sink_logits, pool_w, pool_scale, w_out, norm_ffn, w_router, w_gate, w_up, w_down,
           norm_final):
    n_b, seq, _ = x_prompt.shape
    n_db, n_lat, _ = x_sample.shape
    assert 1 + n_db <= MOD_ROWS and seq == TOKEN_TILE and n_lat % PROJ_TILE == 0
    assert (n_b * seq) % PROJ_TILE == 0

    cond = jnp.concatenate(
        [c_ctx[None, :], c, jnp.zeros((MOD_ROWS - 1 - n_db, D_MODEL), F32)], axis=0)
    mod3 = _modulation(cond, w_ada[0], b_ada[0]).reshape(MOD_ROWS, N_MOD, D_MODEL)

    w_router_bf = jnp.pad(w_router[0], ((0, 0), (0, LANES - N_EXPERTS))).astype(BF16)
    weights = (norm_mix[0][None, :], w_in[0].astype(BF16), sink_logits[0], pool_w[0].astype(BF16),
               pool_scale[0][None, :], w_out[0].astype(BF16), norm_ffn[0][None, :], w_router_bf,
               w_gate.reshape(w_gate.shape[1:]), w_up.reshape(w_up.shape[1:]),
               w_down.reshape(w_down.shape[1:]), norm_final[None, :])

    xs_p, finish_p, (k_p, v_p) = _token_group(
        x_prompt, mod3, lambda tile: (lambda i: 0), seq, weights, None, None)

    past = cache_k.shape[2]
    ck = cache_k[:, 0].reshape(n_db, past, KV_WIDTH)
    cv = cache_v[:, 0].reshape(n_db, past, KV_WIDTH)
    xs_l, finish_l, _ = _token_group(
        x_sample, mod3, lambda tile: (lambda i: 1 + i // (n_lat // tile)), n_lat, weights,
        _rope_tables(n_lat), (ck, cv))

    ys = None
    for r in range(FFN_RANGES):
        ys = _expert_ffn([xs_p[r], xs_l[r]], weights[8], weights[9], weights[10],
                         r * (N_EXPERTS // FFN_RANGES), ys)
    y_prompt = finish_p(ys[0])
    y_sample = finish_l(ys[1])

    state_k = k_p.reshape(n_b, 1, seq, N_KV_HEADS, HEAD_DIM)
    state_v = v_p.reshape(n_b, 1, seq, N_KV_HEADS, HEAD_DIM)
    return (y_prompt, y_sample, state_k, state_v)
```

```python
import functools

import jax
import jax.numpy as jnp
import numpy as np
from jax import lax
from jax.experimental import pallas as pl
from jax.experimental.pallas import tpu as pltpu
from jax.experimental.pallas import tpu_sc as plsc

F32 = jnp.float32
BF16 = jnp.bfloat16
I32 = jnp.int32

D_MODEL = 2048
N_HEADS = 8
N_KV_HEADS = 2
HEAD_DIM = 128
Q_PER_KV = N_HEADS // N_KV_HEADS
ATTN_WIDTH = N_HEADS * HEAD_DIM
KV_WIDTH = N_KV_HEADS * HEAD_DIM
POOL_WIDTH = D_MODEL - ATTN_WIDTH
POOL_SIZES = (2, 4, 8, 16)
POOL_GROUP = POOL_WIDTH // len(POOL_SIZES)
IN_WIDTH = ATTN_WIDTH + 2 * KV_WIDTH + POOL_WIDTH
WINDOW = 128
BLOCK = 128
GRID_W = 64
ROPE_THETA = 10000.0
ROPE_FREQS = HEAD_DIM // 4
N_EXPERTS = 16
EC_FACTOR = 2
D_EXPERT = 1024
N_MOD = 6
EPS = 1e-6
NEG = -1e30
LOG2_E = 1.4426950408889634
ATTN_SCALE = HEAD_DIM ** -0.5 * LOG2_E

LANES = 128
SUBLANES = 8
BF16_ROWS = 16
VMEM_CAP = 64 * 1024 * 1024
SC_CORES = 2
SC_SUBCORES = 16

MOD_ROWS = 8
TOKEN_TILE = 256
PROJ_TILE = 512
SUB_TILE = 256
HQ_TILES = D_MODEL // 2 // LANES + 1
GATHER_CHUNK = 128
FFN_RANGES = 2
POOL_HALO = 8
COMBINE_CHUNK = BF16_ROWS
COMBINE_WINDOW = 64


def _vmem_limit(nbytes):
    return int(min(VMEM_CAP - (4 << 20), max(nbytes, 16 << 20)))


def _mod_kernel(c_ref, w_ref, b_ref, o_ref):
    c = c_ref[...]
    s = c * jax.nn.sigmoid(c)
    o_ref[...] = jnp.dot(s.astype(BF16), w_ref[...].astype(BF16),
                         preferred_element_type=F32) + b_ref[...]


def _modulation(cond, w_ada, b_ada):
    n = w_ada.shape[1]
    tn = 1024
    return pl.pallas_call(
        _mod_kernel,
        grid=(n // tn,),
        in_specs=[pl.BlockSpec((MOD_ROWS, D_MODEL), lambda j: (0, 0)),
                  pl.BlockSpec((D_MODEL, tn), lambda j: (0, j)),
                  pl.BlockSpec((1, tn), lambda j: (0, j))],
        out_specs=pl.BlockSpec((MOD_ROWS, tn), lambda j: (0, j)),
        out_shape=jax.ShapeDtypeStruct((MOD_ROWS, n), F32),
        compiler_params=pltpu.CompilerParams(
            vmem_limit_bytes=_vmem_limit(3 * D_MODEL * tn * 4)),
        name="modulation",
    )(cond, w_ada, b_ada.reshape(1, n))


def _norm_mod(x, g, shift, scale):
    ms = jnp.mean(x * x, axis=-1, keepdims=True)
    y = x * lax.rsqrt(ms + EPS)
    return (y * g) * (1.0 + scale) + shift


def _inproj_kernel(*refs, rope):
    if rope:
        x_ref, mod_ref, g_ref, w_ref, cos_ref, sa_ref, sb_ref, q_ref, k_ref, v_ref, p_ref = refs
    else:
        x_ref, mod_ref, g_ref, w_ref, q_ref, k_ref, v_ref, p_ref, ks_ref, vs_ref = refs
    mod = mod_ref[0]
    for s in range(x_ref.shape[0] // SUB_TILE):
        rows = slice(s * SUB_TILE, (s + 1) * SUB_TILE)
        h = _norm_mod(x_ref[rows, :], g_ref[...], mod[0:1], mod[1:2])
        u = jnp.dot(h.astype(BF16), w_ref[...], preferred_element_type=F32)

        def rot(xh, rows=rows):
            return (xh * cos_ref[rows, :] + pltpu.roll(xh, LANES - ROPE_FREQS, 1) * sa_ref[rows, :]
                    + pltpu.roll(xh, ROPE_FREQS, 1) * sb_ref[rows, :])

        for hd in range(N_HEADS):
            xh = u[:, hd * HEAD_DIM:(hd + 1) * HEAD_DIM] * ATTN_SCALE
            q_ref[rows, hd * HEAD_DIM:(hd + 1) * HEAD_DIM] = (rot(xh) if rope else xh).astype(BF16)
        for hd in range(N_KV_HEADS):
            lo = ATTN_WIDTH + hd * HEAD_DIM
            xh = u[:, lo:lo + HEAD_DIM]
            k_ref[rows, hd * HEAD_DIM:(hd + 1) * HEAD_DIM] = rot(xh) if rope else xh
        v_ref[rows, :] = u[:, ATTN_WIDTH + KV_WIDTH:ATTN_WIDTH + 2 * KV_WIDTH]
        p_ref[rows, :] = u[:, ATTN_WIDTH + 2 * KV_WIDTH:]
        if not rope:
            for hd in range(N_KV_HEADS):
                state_rows = pl.ds(s * SUB_TILE * N_KV_HEADS + hd, SUB_TILE, stride=N_KV_HEADS)
                lo = ATTN_WIDTH + hd * HEAD_DIM
                ks_ref[state_rows, :] = u[:, lo:lo + HEAD_DIM]
                vs_ref[state_rows, :] = u[:, lo + KV_WIDTH:lo + KV_WIDTH + HEAD_DIM]


def _in_projection(x2, mod3, mod_row, g, w_in_bf, rope_tabs):
    t = x2.shape[0]
    tm = PROJ_TILE
    rope = rope_tabs is not None
    row = lambda i: (i, 0)
    in_specs = [pl.BlockSpec((tm, D_MODEL), row),
                pl.BlockSpec((1, N_MOD, D_MODEL), lambda i: (mod_row(i), 0, 0)),
                pl.BlockSpec((1, D_MODEL), lambda i: (0, 0)),
                pl.BlockSpec((D_MODEL, IN_WIDTH), lambda i: (0, 0))]
    args = [x2, mod3, g, w_in_bf]
    if rope:
        n_seq = rope_tabs[0].shape[0]
        seq_blocks = n_seq // tm
        for tab in rope_tabs:
            in_specs.append(pl.BlockSpec((tm, HEAD_DIM), lambda i: (i % seq_blocks, 0)))
            args.append(tab)
    out_specs = [pl.BlockSpec((tm, ATTN_WIDTH), row),
                 pl.BlockSpec((tm, KV_WIDTH), row),
                 pl.BlockSpec((tm, KV_WIDTH), row),
                 pl.BlockSpec((tm, POOL_WIDTH), row)]
    out_shape = [jax.ShapeDtypeStruct((t, ATTN_WIDTH), BF16),
                 jax.ShapeDtypeStruct((t, KV_WIDTH), F32),
                 jax.ShapeDtypeStruct((t, KV_WIDTH), F32),
                 jax.ShapeDtypeStruct((t, POOL_WIDTH), F32)]
    if not rope:
        for _ in range(2):
            out_specs.append(pl.BlockSpec((tm * N_KV_HEADS, HEAD_DIM), row))
            out_shape.append(jax.ShapeDtypeStruct((t * N_KV_HEADS, HEAD_DIM), F32))
    return pl.pallas_call(
        functools.partial(_inproj_kernel, rope=rope),
        grid=(t // tm,),
        in_specs=in_specs,
        out_specs=out_specs,
        out_shape=out_shape,
        compiler_params=pltpu.CompilerParams(
            vmem_limit_bytes=_vmem_limit(2 * D_MODEL * IN_WIDTH * 2 + 24 * tm * D_MODEL * 4)),
        name="in_projection",
    )(*args)


def _softmax_pv(s_list, v_list, sink_col):
    m = sink_col
    for s in s_list:
        m = jnp.maximum(m, jnp.max(s, axis=-1, keepdims=True))
    denom = jnp.exp2(sink_col - m)
    out = None
    for s, v in zip(s_list, v_list):
        e = jnp.exp2(s - m)
        if v.shape[1] == HEAD_DIM:
            denom = denom + jnp.sum(e, axis=-1, keepdims=True)
        o = jnp.dot(e.astype(BF16), v, preferred_element_type=F32)
        out = o if out is None else out + o
    if out.shape[1] > HEAD_DIM:
        denom = denom + out[:, HEAD_DIM:HEAD_DIM + 1]
    return out[:, 0:HEAD_DIM] * (1.0 / denom)


def _with_ones(v):
    return jnp.concatenate([v.astype(BF16), jnp.ones(v.shape, BF16)], axis=1)


def _stack_heads(q, kv):
    return jnp.concatenate(
        [q[:, (kv * Q_PER_KV + g) * HEAD_DIM:(kv * Q_PER_KV + g + 1) * HEAD_DIM]
         for g in range(Q_PER_KV)], axis=0)


def _sink_column(sink_ref, kv, rows):
    r = lax.broadcasted_iota(I32, (Q_PER_KV * rows, 1), 0)
    col = jnp.zeros((Q_PER_KV * rows, 1), F32)
    for g in range(Q_PER_KV):
        col = jnp.where((r >= g * rows) & (r < (g + 1) * rows), sink_ref[kv * Q_PER_KV + g], col)
    return col * LOG2_E


def _qk(q, k):
    return lax.dot_general(q, k, (((1,), (1,)), ((), ())), preferred_element_type=F32)


def _ctx_attn_kernel(sink_ref, q_ref, k_ref, v_ref, o_ref):
    rows = q_ref.shape[0]
    q = q_ref[...]
    for kv in range(N_KV_HEADS):
        kh = k_ref[:, kv * HEAD_DIM:(kv + 1) * HEAD_DIM].astype(BF16)
        vh = v_ref[:, kv * HEAD_DIM:(kv + 1) * HEAD_DIM].astype(BF16)
        qs = _stack_heads(q, kv)
        s = _qk(qs, kh)
        o = _softmax_pv([s], [vh], _sink_column(sink_ref, kv, rows))
        for g in range(Q_PER_KV):
            hd = kv * Q_PER_KV + g
            o_ref[:, hd * HEAD_DIM:(hd + 1) * HEAD_DIM] = o[g * rows:(g + 1) * rows].astype(BF16)


def _context_attention(q, k, v, sink, seq):
    t = q.shape[0]
    row = lambda b: (b, 0)
    return pl.pallas_call(
        _ctx_attn_kernel,
        grid=(t // seq,),
        in_specs=[pl.BlockSpec(memory_space=pltpu.SMEM),
                  pl.BlockSpec((seq, ATTN_WIDTH), row),
                  pl.BlockSpec((seq, KV_WIDTH), row),
                  pl.BlockSpec((seq, KV_WIDTH), row)],
        out_specs=pl.BlockSpec((seq, ATTN_WIDTH), row),
        out_shape=jax.ShapeDtypeStruct((t, ATTN_WIDTH), BF16),
        name="context_attention",
    )(sink, q, k, v)


def _lat_attn_kernel(sink_ref, q_ref, k_ref, v_ref, ck_ref, cv_ref, o_ref, *, n_seq):
    i = pl.program_id(1)
    band = 3 * BLOCK
    start = pl.multiple_of(jnp.clip((i - 1) * BLOCK, 0, n_seq - band), BLOCK)
    rows = Q_PER_KV * BLOCK
    qpos = i * BLOCK + lax.broadcasted_iota(I32, (rows, band), 0) % BLOCK
    kpos = start + lax.broadcasted_iota(I32, (rows, band), 1)
    mask = jnp.abs(kpos - qpos) <= WINDOW
    q = q_ref[...]
    for kv in range(N_KV_HEADS):
        cols = slice(kv * HEAD_DIM, (kv + 1) * HEAD_DIM)
        kb = k_ref[pl.ds(start, band), cols].astype(BF16)
        vb = _with_ones(v_ref[pl.ds(start, band), cols])
        cached = pl.ds(kv, ck_ref.shape[0] // N_KV_HEADS, stride=N_KV_HEADS)
        ck = ck_ref[cached, :].astype(BF16)
        cv = _with_ones(cv_ref[cached, :])
        qs = _stack_heads(q, kv)
        s_loc = jnp.where(mask, _qk(qs, kb), NEG)
        s_ctx = _qk(qs, ck)
        o = _softmax_pv([s_loc, s_ctx], [vb, cv], _sink_column(sink_ref, kv, BLOCK))
        for g in range(Q_PER_KV):
            hd = kv * Q_PER_KV + g
            o_ref[:, hd * HEAD_DIM:(hd + 1) * HEAD_DIM] = o[g * BLOCK:(g + 1) * BLOCK].astype(BF16)


def _latent_attention(q, k, v, ck, cv, sink, n_seq):
    t = q.shape[0]
    nb = n_seq // BLOCK
    cache_rows = ck.shape[0] // (t // n_seq)
    return pl.pallas_call(
        functools.partial(_lat_attn_kernel, n_seq=n_seq),
        grid=(t // n_seq, nb),
        in_specs=[pl.BlockSpec(memory_space=pltpu.SMEM),
                  pl.BlockSpec((BLOCK, ATTN_WIDTH), lambda b, i: (b * nb + i, 0)),
                  pl.BlockSpec((n_seq, KV_WIDTH), lambda b, i: (b, 0)),
                  pl.BlockSpec((n_seq, KV_WIDTH), lambda b, i: (b, 0)),
                  pl.BlockSpec((cache_rows, HEAD_DIM), lambda b, i: (b, 0)),
                  pl.BlockSpec((cache_rows, HEAD_DIM), lambda b, i: (b, 0))],
        out_specs=pl.BlockSpec((BLOCK, ATTN_WIDTH), lambda b, i: (b * nb + i, 0)),
        out_shape=jax.ShapeDtypeStruct((t, ATTN_WIDTH), BF16),
        name="latent_attention",
    )(sink, q, k, v, ck, cv)


def _pool_group(p_ref, r0, seq, w_ref, s_ref, g):
    n = SUB_TILE
    rows = n + 2 * POOL_HALO
    static = isinstance(r0, int)
    t0 = r0 % seq
    t = t0 + lax.broadcasted_iota(I32, (n, 1), 0)
    has_top = t0 > 0
    has_bottom = t0 + n < seq
    zeros = jnp.zeros((POOL_HALO, POOL_GROUP), F32)
    w = POOL_SIZES[g]
    cols = slice(g * POOL_GROUP, (g + 1) * POOL_GROUP)
    pg = p_ref[pl.ds(r0, n), cols]
    if static:
        top = p_ref[r0 - POOL_HALO:r0, cols] if has_top else zeros
        bottom = p_ref[r0 + n:r0 + n + POOL_HALO, cols] if has_bottom else zeros
    else:
        top_row = pl.multiple_of(jnp.maximum(r0 - POOL_HALO, 0), POOL_HALO)
        bottom_row = pl.multiple_of(jnp.minimum(r0 + n, p_ref.shape[0] - POOL_HALO), POOL_HALO)
        top = jnp.where(has_top, p_ref[pl.ds(top_row, POOL_HALO), cols], 0.0)
        bottom = jnp.where(has_bottom, p_ref[pl.ds(bottom_row, POOL_HALO), cols], 0.0)
    x = jnp.concatenate([top, pg, bottom], axis=0)
    fwd = x
    span = 1
    while span < w // 2:
        fwd = fwd + pltpu.roll(fwd, rows - span, 0)
        span *= 2
    if (w // 2) % SUBLANES == 0:
        wsum = fwd[POOL_HALO - w // 2:POOL_HALO - w // 2 + n] + fwd[POOL_HALO:POOL_HALO + n]
    else:
        wsum = (fwd + pltpu.roll(fwd, w // 2, 0))[POOL_HALO:POOL_HALO + n]
    lo = jnp.maximum(t - w // 2, 0)
    hi = jnp.minimum(t + w - w // 2, seq)
    inv_cnt = 1.0 / (hi - lo).astype(F32)
    mixed = wsum * inv_cnt - pg
    y = jnp.dot(mixed.astype(BF16), w_ref[g], preferred_element_type=F32)
    return (y * s_ref[:, cols]).astype(BF16)


def _pack_pair(lo, hi):
    return lax.bitcast_convert_type(pltpu.pack_elementwise([lo, hi], packed_dtype=BF16), I32)


def _unpack_pair(words):
    lo = pltpu.unpack_elementwise(words, index=0, packed_dtype=BF16, unpacked_dtype=F32)
    hi = pltpu.unpack_elementwise(words, index=1, packed_dtype=BF16, unpacked_dtype=F32)
    return lo, hi


def _outproj_kernel(a_ref, p_ref, x_ref, mod_ref, g_ref, wo_ref, wr_ref, pw_ref, ps_ref,
                    x1_ref, h_ref, aff_ref, *, seq):
    mod = mod_ref[0]
    groups = SUB_TILE // SUBLANES
    half = D_MODEL // 2
    tm = x_ref.shape[0]
    steps_per_p_block = p_ref.shape[0] // tm
    for s in range(tm // SUB_TILE):
        rows = slice(s * SUB_TILE, (s + 1) * SUB_TILE)
        grp = slice(s * groups, (s + 1) * groups)
        r0 = s * SUB_TILE
        if steps_per_p_block > 1:
            r0 = pl.multiple_of((pl.program_id(0) % steps_per_p_block) * tm + r0, SUB_TILE)
        pooled = jnp.concatenate([_pool_group(p_ref, r0, seq, pw_ref, ps_ref, g)
                                  for g in range(len(POOL_SIZES))], axis=1)
        mix = (jnp.dot(a_ref[rows, :], wo_ref[0:ATTN_WIDTH, :], preferred_element_type=F32)
               + jnp.dot(pooled, wo_ref[ATTN_WIDTH:D_MODEL, :], preferred_element_type=F32))
        x1 = x_ref[rows, :] + mod[2:3] * mix
        x1_ref[rows, :] = x1
        h = _norm_mod(x1, g_ref[...], mod[3:4], mod[4:5])
        logits = jnp.dot(h.astype(BF16), wr_ref[...], preferred_element_type=F32)
        lane = lax.broadcasted_iota(I32, logits.shape, 1)
        logits = jnp.where(lane < N_EXPERTS, logits, -jnp.inf)
        m = jnp.max(logits, axis=-1, keepdims=True)
        e = jnp.exp(logits - m)
        aff = e / jnp.sum(e, axis=-1, keepdims=True)
        aff_ref[rows, :] = aff[:, 0:N_EXPERTS]
        for c in range(HQ_TILES - 1):
            words = _pack_pair(h[:, c * LANES:(c + 1) * LANES],
                               h[:, half + c * LANES:half + (c + 1) * LANES])
            h_ref[grp, c * SUBLANES:(c + 1) * SUBLANES, :] = (
                words.reshape(groups, SUBLANES, LANES))
        h_ref[grp, (HQ_TILES - 1) * SUBLANES:, :] = (
            lax.bitcast_convert_type(aff, I32).reshape(groups, SUBLANES, LANES))


def _out_projection(attn, p, x2, mod3, mod_row, g, w_out_bf, w_router, pool_w, pool_scale, seq):
    t = x2.shape[0]
    tm = PROJ_TILE
    row = lambda i: (i, 0)
    p_rows = max(tm, seq)
    steps_per_p_block = p_rows // tm
    return pl.pallas_call(
        functools.partial(_outproj_kernel, seq=seq),
        grid=(t // tm,),
        in_specs=[pl.BlockSpec((tm, ATTN_WIDTH), row),
                  pl.BlockSpec((p_rows, POOL_WIDTH), lambda i: (i // steps_per_p_block, 0)),
                  pl.BlockSpec((tm, D_MODEL), row),
                  pl.BlockSpec((1, N_MOD, D_MODEL), lambda i: (mod_row(i), 0, 0)),
                  pl.BlockSpec((1, D_MODEL), lambda i: (0, 0)),
                  pl.BlockSpec((D_MODEL, D_MODEL), lambda i: (0, 0)),
                  pl.BlockSpec((D_MODEL, LANES), lambda i: (0, 0)),
                  pl.BlockSpec((len(POOL_SIZES), POOL_GROUP, POOL_GROUP), lambda i: (0, 0, 0)),
                  pl.BlockSpec((1, POOL_WIDTH), lambda i: (0, 0))],
        out_specs=[pl.BlockSpec((tm, D_MODEL), row),
                   pl.BlockSpec((tm // SUBLANES, HQ_TILES * SUBLANES, LANES), lambda i: (i, 0, 0)),
                   pl.BlockSpec((tm, N_EXPERTS), row)],
        out_shape=[jax.ShapeDtypeStruct((t, D_MODEL), F32),
                   jax.ShapeDtypeStruct((t // SUBLANES, HQ_TILES * SUBLANES, LANES), I32),
                   jax.ShapeDtypeStruct((t, N_EXPERTS), F32)],
        compiler_params=pltpu.CompilerParams(
            vmem_limit_bytes=_vmem_limit(2 * D_MODEL * D_MODEL * 2 + 24 * tm * D_MODEL * 4)),
        name="out_projection",
    )(attn, p, x2, mod3, g, w_out_bf, w_router, pool_w, pool_scale)


def _route_kernel(a_ref, idx_ref, slot_ref, off_ref, slot_scr, *, cap):
    a = a_ref[...]
    n_e, n_c, _ = a.shape
    rows = n_e * n_c

    def enough(cand):
        cand_f = lax.bitcast_convert_type(cand, F32)
        cnt = jnp.sum(jnp.sum((a >= cand_f).astype(F32), axis=1, keepdims=True),
                      axis=2, keepdims=True)
        return cnt >= cap

    def two_bits(it, thr):
        low = 28 - 2 * it
        for setting in (1, 2, 3):
            cand = thr | jnp.left_shift(jnp.int32(setting), low)
            best = jnp.where(enough(cand), cand, thr if setting == 1 else best)
        return best

    top = jnp.full((n_e, 1, 1), 1 << 30, I32)
    thr = jnp.where(enough(top), top, 0)
    thr = lax.fori_loop(0, 15, two_bits, thr)
    thr_f = lax.bitcast_convert_type(thr, F32)
    gt = (a > thr_f).astype(F32).reshape(rows, LANES)
    eq = (a == thr_f).astype(F32).reshape(rows, LANES)

    li = lax.broadcasted_iota(I32, (LANES, LANES), 0)
    lj = lax.broadcasted_iota(I32, (LANES, LANES), 1)
    upper_incl = (li <= lj).astype(BF16)
    ci = lax.broadcasted_iota(I32, (n_c, n_c), 0)
    cj = lax.broadcasted_iota(I32, (n_c, n_c), 1)
    before = (cj < ci).astype(BF16)
    whole = jnp.ones((n_c, n_c), BF16)

    def over_chunks(mat, col):
        wide = jnp.broadcast_to(col, (rows, LANES)).astype(BF16)
        side_by_side = jnp.concatenate([wide[e * n_c:(e + 1) * n_c] for e in range(n_e)], axis=1)
        res = jnp.dot(mat, side_by_side, preferred_element_type=F32)
        return jnp.concatenate([res[:, e * LANES:(e + 1) * LANES] for e in range(n_e)], axis=0)

    def prefix(x):
        incl = jnp.dot(x.astype(BF16), upper_incl, preferred_element_type=F32)
        tot = incl[:, LANES - 1:LANES]
        return incl, tot, over_chunks(before, tot)

    n_gt = over_chunks(whole, jnp.sum(gt, axis=1, keepdims=True))
    need = cap - n_gt
    incl_eq, _, off_eq = prefix(eq)
    rank_eq = off_eq + incl_eq - eq
    sel = jnp.where((eq > 0) & (rank_eq < need), 1.0, gt)
    incl, tot, off = prefix(sel)
    slot = off + incl - sel
    slot_scr[...] = jnp.where(sel > 0, slot, -1.0)
    for c in range(n_c):
        per_expert = slot_scr[pl.ds(c, n_e, stride=n_c), :]
        slot_ref[c * LANES:(c + 1) * LANES, :] = per_expert.T.astype(I32)
    off_ref[...] = off.astype(I32).reshape(n_e, n_c, LANES)

    s_lane = lax.broadcasted_iota(I32, (1, cap), 1).astype(F32)
    c_col = lax.broadcasted_iota(I32, (n_c, 1), 0).astype(F32)
    for e in range(n_e):
        r0 = e * n_c
        incl_e = incl[r0:r0 + n_c]
        off_e = off[r0:r0 + n_c, 0:1]
        tot_e = tot[r0:r0 + n_c]
        onehot = ((off_e <= s_lane) & (s_lane < off_e + tot_e)).astype(F32)
        counts = lax.dot_general(incl_e.astype(BF16), onehot.astype(BF16),
                                 (((0,), (0,)), ((), ())), preferred_element_type=F32)
        local = s_lane - jnp.sum(onehot * off_e, axis=0, keepdims=True)
        lane = jnp.sum((counts <= local).astype(F32), axis=0, keepdims=True)
        chunk = jnp.sum(onehot * c_col, axis=0, keepdims=True)
        idx_ref[e] = (chunk * LANES + lane).astype(I32)


def _routing(aff, cap):
    t = aff.shape[0]
    n_c = t // LANES
    a3 = aff.T.reshape(N_EXPERTS, n_c, LANES)
    return pl.pallas_call(
        functools.partial(_route_kernel, cap=cap),
        out_shape=[jax.ShapeDtypeStruct((N_EXPERTS, 1, cap), I32),
                   jax.ShapeDtypeStruct((t, N_EXPERTS), I32),
                   jax.ShapeDtypeStruct((N_EXPERTS, n_c, LANES), I32)],
        scratch_shapes=[pltpu.VMEM((N_EXPERTS * n_c, LANES), F32)],
        compiler_params=pltpu.CompilerParams(vmem_limit_bytes=_vmem_limit(48 << 20)),
        name="routing",
    )(a3)


def _gather_rows(table, row_ids):
    n_chunks = row_ids.shape[0]
    n_workers = SC_CORES * SC_SUBCORES
    per_worker = n_chunks // n_workers
    assert row_ids.shape[1] == GATHER_CHUNK and n_chunks % n_workers == 0
    mesh = plsc.VectorSubcoreMesh(core_axis_name="core", subcore_axis_name="subcore")

    @functools.partial(
        pl.kernel, mesh=mesh,
        out_type=jax.ShapeDtypeStruct((n_chunks * GATHER_CHUNK, LANES), I32),
        scratch_types=[pltpu.VMEM((per_worker, GATHER_CHUNK), I32),
                       pltpu.VMEM((2, GATHER_CHUNK, LANES), I32),
                       pltpu.SemaphoreType.DMA((2,)),
                       pltpu.SemaphoreType.DMA((2,))],
        name="gather_rows",
    )
    def gather(table_hbm, ids_hbm, out_hbm, ids_v, rows_v, gather_sem, store_sem):
        worker = lax.axis_index("subcore") * SC_CORES + lax.axis_index("core")
        first = worker * per_worker
        pltpu.sync_copy(ids_hbm.at[worker], ids_v)

        def fetch(j):
            return pltpu.make_async_copy(table_hbm.at[ids_v.at[j]], rows_v.at[j % 2],
                                         gather_sem.at[j % 2])

        def store(j):
            rows = pl.ds(pl.multiple_of((first + j) * GATHER_CHUNK, GATHER_CHUNK), GATHER_CHUNK)
            return pltpu.make_async_copy(rows_v.at[j % 2], out_hbm.at[rows], store_sem.at[j % 2])

        for j in range(per_worker):
            if j >= 2:
                store(j - 2).wait()
            fetch(j).start()
            if j >= 1:
                fetch(j - 1).wait()
                store(j - 1).start()
        fetch(per_worker - 1).wait()
        store(per_worker - 1).start()
        for j in range(max(per_worker - 2, 0), per_worker):
            store(j).wait()

    return gather(table, row_ids.reshape(n_workers, per_worker, GATHER_CHUNK))


def _packed_row_ids(idx, cap):
    tok = idx.reshape(-1, 1, cap // GATHER_CHUNK, GATHER_CHUNK)
    tile = jnp.arange(HQ_TILES, dtype=I32).reshape(1, HQ_TILES, 1, 1)
    ids = ((tok // SUBLANES) * HQ_TILES + tile) * SUBLANES + tok % SUBLANES
    return ids.reshape(-1, GATHER_CHUNK)


def _ffn_kernel(*refs, caps, n_f, row_chunk, first_expert, n_prior):
    n_g = len(caps)
    x_refs = refs[:n_g]
    wg_ref, wu_ref, wd_ref = refs[n_g:n_g + 3]
    y_refs = refs[n_g + 3 + n_prior:2 * n_g + 3 + n_prior]
    acc_ref = refs[2 * n_g + 3 + n_prior]
    e = pl.program_id(0)
    f = pl.program_id(1)

    @pl.when((e == 0) & (f == 0))
    def _():
        acc_ref[...] = jnp.zeros_like(acc_ref)

    def step(last):
        first = f == 0
        base = 0
        for x_ref, y_ref, cap in zip(x_refs, y_refs, caps):
            if last:
                aff = lax.bitcast_convert_type(x_ref[0, HQ_TILES - 1], F32)
                lane = lax.broadcasted_iota(I32, aff.shape, 1)
                gates = jnp.sum(jnp.where(lane == e + first_expert, aff, 0.0), axis=1,
                                keepdims=True)
                y_ref[0, cap:, :] = jnp.zeros((COMBINE_WINDOW, D_MODEL // 2), I32)
            for r in range(cap // row_chunk):
                rs = slice(r * row_chunk, (r + 1) * row_chunk)
                acc_rows = slice(base + r * row_chunk, base + (r + 1) * row_chunk)
                pairs = [_unpack_pair(x_ref[0, c, rs, :]) for c in range(HQ_TILES - 1)]
                x = jnp.concatenate([p[0] for p in pairs] + [p[1] for p in pairs], axis=1)
                gate_act = jnp.dot(x, wg_ref[0], preferred_element_type=F32)
                up = jnp.dot(x, wu_ref[0], preferred_element_type=F32)
                hid = (gate_act * jax.nn.sigmoid(gate_act)) * up
                part = jnp.dot(hid, wd_ref[0], preferred_element_type=F32)
                if last:
                    y = (part + acc_ref[acc_rows, :]) * gates[rs]
                    y_ref[0, rs, :] = _pack_pair(y[:, 0:D_MODEL // 2], y[:, D_MODEL // 2:])
                else:
                    acc_ref[acc_rows, :] = part + jnp.where(first, 0.0, acc_ref[acc_rows, :])
            base += cap

    @pl.when(f < n_f - 1)
    def _():
        step(False)

    @pl.when(f == n_f - 1)
    def _():
        step(True)


def _expert_ffn(xs_groups, w_gate, w_up, w_down, first_expert, prior_outputs):
    n_f = 4
    tf = D_EXPERT // n_f
    row_chunk = 512
    n_e = xs_groups[0].shape[0]
    e0 = first_expert
    caps = tuple(xs.shape[2] for xs in xs_groups)
    rows = sum(caps)
    est = (2 * HQ_TILES * rows * LANES * 4 + rows * D_MODEL * (4 + 2 * 2) + 2 * 3 * D_MODEL * tf * 4
           + row_chunk * (3 * tf + 2 * D_MODEL) * 4)
    x_specs = [pl.BlockSpec((1, HQ_TILES, cap, LANES), lambda e, f: (e, 0, 0, 0)) for cap in caps]
    y_shapes = [(N_EXPERTS, cap + COMBINE_WINDOW, D_MODEL // 2) for cap in caps]
    y_specs = [pl.BlockSpec((1,) + s[1:], lambda e, f: (e + e0, 0, 0)) for s in y_shapes]
    prior = list(prior_outputs or [])
    n_in = len(caps) + 3
    return pl.pallas_call(
        functools.partial(_ffn_kernel, caps=caps, n_f=n_f, row_chunk=row_chunk,
                          first_expert=e0, n_prior=len(prior)),
        grid=(n_e, n_f),
        in_specs=(x_specs + [pl.BlockSpec((1, D_MODEL, tf), lambda e, f: (e + e0, 0, f)),
                             pl.BlockSpec((1, D_MODEL, tf), lambda e, f: (e + e0, 0, f)),
                             pl.BlockSpec((1, tf, D_MODEL), lambda e, f: (e + e0, f, 0))]
                  + [pl.BlockSpec(memory_space=pl.ANY) for _ in prior]),
        out_specs=y_specs,
        out_shape=[jax.ShapeDtypeStruct(s, I32) for s in y_shapes],
        input_output_aliases={n_in + j: j for j in range(len(prior))},
        scratch_shapes=[pltpu.VMEM((rows, D_MODEL), F32)],
        compiler_params=pltpu.CompilerParams(
            dimension_semantics=("arbitrary", "arbitrary"),
            vmem_limit_bytes=_vmem_limit(est + (4 << 20))),
        name="expert_ffn",
    )(*xs_groups, w_gate, w_up, w_down, *prior)


def _combine_kernel(start_ref, nch_ref, wide_ref, x1_ref, mod_ref, g_ref, slot_ref, y_hbm, o_ref,
                    ybuf_ref, cols_ref, ffn_ref, sem, *, first_k, kblock):
    i = pl.program_id(0)
    tm = x1_ref.shape[0]
    cur = i % 2
    n_window_rows = N_EXPERTS * COMBINE_WINDOW

    window_head = COMBINE_WINDOW - COMBINE_CHUNK

    def window_copy(e, src_row, buf):
        return pltpu.make_async_copy(
            y_hbm.at[e, pl.ds(pl.multiple_of(src_row, SUBLANES), window_head)],
            ybuf_ref.at[buf, e * COMBINE_WINDOW:e * COMBINE_WINDOW + window_head], sem.at[buf])

    def window_tail_copy(e, src_row, buf):
        return chunk_copy(e, src_row + window_head, buf,
                          jnp.int32(e * COMBINE_WINDOW + window_head))

    def needs_tail(tile, e):
        return nch_ref[tile, e] * COMBINE_CHUNK > window_head

    def chunk_copy(e, src_row, buf, dst_row):
        return pltpu.make_async_copy(
            y_hbm.at[e, pl.ds(pl.multiple_of(src_row, SUBLANES), COMBINE_CHUNK)],
            ybuf_ref.at[buf, pl.ds(pl.multiple_of(dst_row, COMBINE_CHUNK), COMBINE_CHUNK)],
            sem.at[buf])

    def fetch(tile, buf):
        @pl.when(wide_ref[tile] == 0)
        def _():
            for e in range(N_EXPERTS):
                window_copy(e, start_ref[tile, e], buf).start()

                @pl.when(needs_tail(tile, e))
                def _(e=e):
                    window_tail_copy(e, start_ref[tile, e], buf).start()

        @pl.when(wide_ref[tile] != 0)
        def _():
            pos = jnp.int32(0)
            for e in range(N_EXPERTS):
                start = start_ref[tile, e]
                nch = nch_ref[tile, e]

                def issue(c, carry, e=e, start=start, pos=pos):
                    chunk_copy(e, start + c * COMBINE_CHUNK, buf, pos + c * COMBINE_CHUNK).start()
                    return carry
                lax.fori_loop(0, nch, issue, 0)
                pos = pos + nch * COMBINE_CHUNK

    @pl.when(i == 0)
    def _():
        ybuf_ref[...] = jnp.zeros_like(ybuf_ref)
        fetch(0, 0)

    @pl.when(i + 1 < pl.num_programs(0))
    def _():
        fetch(i + 1, 1 - cur)

    def finish(ffn):
        out = x1_ref[...] + mod_ref[0][5:6] * ffn
        ms = jnp.mean(out * out, axis=-1, keepdims=True)
        o_ref[...] = (out * lax.rsqrt(ms + EPS)) * g_ref[...]

    def apply_selection(sel, k0, width):
        lo, hi = _unpack_pair(ybuf_ref[cur, pl.ds(k0, width), :])
        return jnp.concatenate([jnp.dot(sel, lo, preferred_element_type=F32),
                                jnp.dot(sel, hi, preferred_element_type=F32)], axis=1)

    @pl.when(wide_ref[i] == 0)
    def _():
        for e in range(N_EXPERTS):
            window_copy(e, jnp.int32(0), cur).wait()

            @pl.when(needs_tail(i, e))
            def _(e=e):
                window_tail_copy(e, jnp.int32(0), cur).wait()
        expert_lane = lax.broadcasted_iota(I32, (1, N_EXPERTS), 1)
        starts = jnp.zeros((1, N_EXPERTS), I32)
        for e in range(N_EXPERTS):
            starts = jnp.where(expert_lane == e, start_ref[i, e], starts)
        slots = slot_ref[...]
        rows_in_window = jnp.where(slots >= 0, (slots - starts).astype(F32), -1.0)
        window_of_lane = lax.broadcasted_iota(I32, (N_EXPERTS, n_window_rows), 1) // COMBINE_WINDOW
        spread = (window_of_lane == lax.broadcasted_iota(I32, (N_EXPERTS, n_window_rows), 0))
        target = jnp.dot(rows_in_window.astype(BF16), spread.astype(BF16),
                         preferred_element_type=F32)
        lane_row = (lax.broadcasted_iota(I32, (tm, n_window_rows), 1) % COMBINE_WINDOW).astype(F32)
        sel = jnp.where(target == lane_row, 1.0, 0.0)
        ffn = None
        for k0 in range(0, n_window_rows, kblock):
            part = apply_selection(sel[:, k0:k0 + kblock], k0, kblock)
            ffn = part if ffn is None else ffn + part
        finish(ffn)

    @pl.when(wide_ref[i] != 0)
    def _():
        pos = jnp.int32(0)
        begins = []
        for e in range(N_EXPERTS):
            begins.append(pos)
            pos = pos + nch_ref[i, e] * COMBINE_CHUNK
        begins.append(pos)

        def drain(c, carry):
            chunk_copy(0, jnp.int32(0), cur, jnp.int32(0)).wait()
            return carry
        lax.fori_loop(0, pos // COMBINE_CHUNK, drain, 0)

        slots = slot_ref[...]
        for e in range(N_EXPERTS):
            s_e = slots[:, e:e + 1]
            col = jnp.where(s_e >= 0, s_e + (begins[e] - start_ref[i, e]), -1)
            cols_ref[e] = jnp.broadcast_to(col, (tm, LANES))

        lane = lax.broadcasted_iota(I32, (tm, LANES), 1)

        def selection(k0, width):
            halves = []
            for h in range(width // LANES):
                c0 = k0 + h * LANES
                target = lane + c0
                e_lo = jnp.int32(0)
                e_hi = jnp.int32(0)
                for e in range(N_EXPERTS):
                    e_lo = e_lo + (begins[e + 1] <= c0).astype(I32)
                    e_hi = e_hi + (begins[e] < c0 + LANES).astype(I32)

                def mark(e, hit, target=target):
                    return jnp.where(cols_ref[e] == target, 1.0, hit)
                halves.append(lax.fori_loop(e_lo, e_hi, mark, jnp.zeros((tm, LANES), F32)))
            return jnp.concatenate(halves, axis=1)

        ffn_ref[...] = apply_selection(selection(0, first_k), 0, first_k)

        def kstep(kb, carry):
            k0 = pl.multiple_of(kb * kblock, kblock)
            ffn_ref[...] += apply_selection(selection(k0, kblock), k0, kblock)
            return carry
        lax.fori_loop(first_k // kblock, (pos + kblock - 1) // kblock, kstep, 0)
        finish(ffn_ref[...])


def _combine(x1, mod3, mod_row, g_final, slot_t, tile_start, tile_nch, tile_wide, y):
    t = x1.shape[0]
    tm = TOKEN_TILE
    first_k = 2 * tm + N_EXPERTS * COMBINE_CHUNK
    kblock = 256
    max_rows = N_EXPERTS * (tm + 2 * COMBINE_CHUNK)
    max_rows = -(-max_rows // kblock) * kblock
    row = lambda i, *_: (i, 0)
    grid_spec = pltpu.PrefetchScalarGridSpec(
        num_scalar_prefetch=3,
        grid=(t // tm,),
        in_specs=[pl.BlockSpec((tm, D_MODEL), row),
                  pl.BlockSpec((1, N_MOD, D_MODEL), lambda i, *_: (mod_row(i), 0, 0)),
                  pl.BlockSpec((1, D_MODEL), lambda i, *_: (0, 0)),
                  pl.BlockSpec((tm, N_EXPERTS), row),
                  pl.BlockSpec(memory_space=pl.ANY)],
        out_specs=pl.BlockSpec((tm, D_MODEL), row),
        scratch_shapes=[pltpu.VMEM((2, max_rows, D_MODEL // 2), I32),
                        pltpu.VMEM((N_EXPERTS, tm, LANES), I32),
                        pltpu.VMEM((tm, D_MODEL), F32),
                        pltpu.SemaphoreType.DMA((2,))],
    )
    return pl.pallas_call(
        functools.partial(_combine_kernel, first_k=first_k, kblock=kblock),
        grid_spec=grid_spec,
        out_shape=jax.ShapeDtypeStruct((t, D_MODEL), F32),
        compiler_params=pltpu.CompilerParams(
            dimension_semantics=("arbitrary",),
            vmem_limit_bytes=_vmem_limit(2 * max_rows * D_MODEL * 2 + 16 * tm * D_MODEL * 4)),
        name="combine",
    )(tile_start, tile_nch, tile_wide, x1, mod3, g_final, slot_t, y)


def _rope_tables(n):
    f32 = np.float32
    rows = n // GRID_W
    row = np.repeat(np.arange(rows, dtype=f32), GRID_W)
    col = np.tile(np.arange(GRID_W, dtype=f32), rows)
    inv = (f32(ROPE_THETA) ** (-np.arange(ROPE_FREQS, dtype=f32) / f32(ROPE_FREQS))).astype(f32)
    ang_r = row[:, None] * inv
    ang_c = col[:, None] * inv
    zero = np.zeros_like(ang_r)
    cos = np.concatenate([np.cos(ang_r)] * 2 + [np.cos(ang_c)] * 2, axis=1)
    sin_a = np.concatenate([-np.sin(ang_r), zero, -np.sin(ang_c), zero], axis=1)
    sin_b = np.concatenate([zero, np.sin(ang_r), zero, np.sin(ang_c)], axis=1)
    return tuple(jnp.asarray(t, F32) for t in (cos, sin_a, sin_b))


def _token_group(x, mod3, mod_row, seq, weights, rope_tabs, ctx_kv):
    (norm_mix, w_in_bf, sink, pool_w, pool_scale, w_out_bf, norm_ffn, w_router,
     w_gate, w_up, w_down, norm_final) = weights
    b = x.shape[0]
    t = b * seq
    x2 = x.reshape(t, D_MODEL)
    q, k, v, p, *state = _in_projection(x2, mod3, mod_row(PROJ_TILE), norm_mix, w_in_bf,
                                        rope_tabs)
    if ctx_kv is None:
        attn = _context_attention(q, k, v, sink, seq)
    else:
        attn = _latent_attention(q, k, v, ctx_kv[0], ctx_kv[1], sink, seq)
    x1, h_packed, aff = _out_projection(attn, p, x2, mod3, mod_row(PROJ_TILE), norm_ffn,
                                        w_out_bf, w_router, pool_w, pool_scale, seq)

    cap = EC_FACTOR * t // N_EXPERTS
    idx, slot_t, off3 = _routing(aff, cap)
    table = h_packed.reshape(-1, LANES)
    per_range = N_EXPERTS // FFN_RANGES
    xs = []
    for r in range(FFN_RANGES):
        ids = _packed_row_ids(idx[r * per_range:(r + 1) * per_range], cap)
        xs.append(_gather_rows(table, ids).reshape(per_range, HQ_TILES, cap, LANES))

    chunks_per_tile = TOKEN_TILE // LANES
    tile_off = off3[:, ::chunks_per_tile, 0]
    tile_end = jnp.concatenate([tile_off[:, 1:], jnp.full((N_EXPERTS, 1), cap, I32)], axis=1)
    tile_start = (tile_off // SUBLANES) * SUBLANES
    tile_nch = jnp.where(tile_end > tile_off,
                         (tile_end - tile_start + COMBINE_CHUNK - 1) // COMBINE_CHUNK, 0)
    tile_wide = jnp.any(tile_end - tile_start > COMBINE_WINDOW, axis=0).astype(I32)

    def finish(y):
        out = _combine(x1, mod3, mod_row(TOKEN_TILE), norm_final, slot_t, tile_start.T,
                       tile_nch.T, tile_wide, y)
        return out.reshape(b, seq, D_MODEL)
    return xs, finish, state


def kernel(x_prompt, x_sample, c, cache_k, cache_v, c_ctx, w_ada, b_ada, norm_mix, w_in,
           sink_logits, pool_w, pool_scale, w_out, norm_ffn, w_router, w_gate, w_up, w_down,
           norm_final):
    n_b, seq, _ = x_prompt.shape
    n_db, n_lat, _ = x_sample.shape
    assert 1 + n_db <= MOD_ROWS and seq == TOKEN_TILE and n_lat % PROJ_TILE == 0
    assert w_in.shape[0] == 1 and cache_k.shape[1] == 1
    assert (n_b * seq) % PROJ_TILE == 0

    cond = jnp.concatenate(
        [c_ctx[None, :], c, jnp.zeros((MOD_ROWS - 1 - n_db, D_MODEL), F32)], axis=0)
    mod3 = _modulation(cond, w_ada[0], b_ada[0]).reshape(MOD_ROWS, N_MOD, D_MODEL)

    w_router_bf = jnp.pad(w_router[0], ((0, 0), (0, LANES - N_EXPERTS))).astype(BF16)
    weights = (norm_mix[0][None, :], w_in[0].astype(BF16), sink_logits[0], pool_w[0].astype(BF16),
               pool_scale[0][None, :], w_out[0].astype(BF16), norm_ffn[0][None, :], w_router_bf,
               w_gate.reshape(w_gate.shape[1:]), w_up.reshape(w_up.shape[1:]),
               w_down.reshape(w_down.shape[1:]), norm_final[None, :])

    xs_p, finish_p, (k_p, v_p) = _token_group(
        x_prompt, mod3, lambda tile: (lambda i: 0), seq, weights, None, None)

    ck = cache_k.reshape(-1, HEAD_DIM)
    cv = cache_v.reshape(-1, HEAD_DIM)
    xs_l, finish_l, _ = _token_group(
        x_sample, mod3, lambda tile: (lambda i: 1 + i // (n_lat // tile)), n_lat, weights,
        _rope_tables(n_lat), (ck, cv))

    ys = None
    for r in range(FFN_RANGES):
        ys = _expert_ffn([xs_p[r], xs_l[r]], weights[8], weights[9], weights[10],
                         r * (N_EXPERTS // FFN_RANGES), ys)
    y_prompt = finish_p(ys[0])
    y_sample = finish_l(ys[1])

    state_k = k_p.reshape(n_b, 1, seq, N_KV_HEADS, HEAD_DIM)
    state_v = v_p.reshape(n_b, 1, seq, N_KV_HEADS, HEAD_DIM)
    return (y_prompt, y_sample, state_k, state_v)
```

```python
import functools

import jax
import jax.numpy as jnp
import numpy as np
from jax import lax
from jax.experimental import pallas as pl
from jax.experimental.pallas import tpu as pltpu
from jax.experimental.pallas import tpu_sc as plsc

F32 = jnp.float32
BF16 = jnp.bfloat16
I32 = jnp.int32

D_MODEL = 2048
N_HEADS = 8
N_KV_HEADS = 2
HEAD_DIM = 128
Q_PER_KV = N_HEADS // N_KV_HEADS
ATTN_WIDTH = N_HEADS * HEAD_DIM
KV_WIDTH = N_KV_HEADS * HEAD_DIM
POOL_WIDTH = D_MODEL - ATTN_WIDTH
POOL_SIZES = (2, 4, 8, 16)
POOL_GROUP = POOL_WIDTH // len(POOL_SIZES)
IN_WIDTH = ATTN_WIDTH + 2 * KV_WIDTH + POOL_WIDTH
WINDOW = 128
BLOCK = 128
GRID_W = 64
ROPE_THETA = 10000.0
ROPE_FREQS = HEAD_DIM // 4
N_EXPERTS = 16
EC_FACTOR = 2
D_EXPERT = 1024
N_MOD = 6
EPS = 1e-6
NEG = -1e30
LOG2_E = 1.4426950408889634
ATTN_SCALE = HEAD_DIM ** -0.5 * LOG2_E

LANES = 128
SUBLANES = 8
BF16_ROWS = 16
VMEM_CAP = 64 * 1024 * 1024
SC_CORES = 2
SC_SUBCORES = 16

MOD_ROWS = 8
TOKEN_TILE = 256
PROJ_TILE = 512
SUB_TILE = 256
HQ_TILES = D_MODEL // 2 // LANES + 1
GATHER_CHUNK = 128
GATHER_SPLIT = 8
EXPERT_RANGES = ((0, 4), (4, 12))
POOL_HALO = 8
COMBINE_CHUNK = BF16_ROWS
COMBINE_WINDOW = 64


def _vmem_limit(nbytes):
    return int(min(VMEM_CAP - (4 << 20), max(nbytes, 16 << 20)))


def _mod_kernel(c_ref, w_ref, b_ref, o_ref):
    c = c_ref[...]
    s = c * jax.nn.sigmoid(c)
    o_ref[...] = jnp.dot(s.astype(BF16), w_ref[...].astype(BF16),
                         preferred_element_type=F32) + b_ref[...]


def _modulation(cond, w_ada, b_ada):
    n = w_ada.shape[1]
    tn = 1024
    return pl.pallas_call(
        _mod_kernel,
        grid=(n // tn,),
        in_specs=[pl.BlockSpec((MOD_ROWS, D_MODEL), lambda j: (0, 0)),
                  pl.BlockSpec((D_MODEL, tn), lambda j: (0, j)),
                  pl.BlockSpec((1, tn), lambda j: (0, j))],
        out_specs=pl.BlockSpec((MOD_ROWS, tn), lambda j: (0, j)),
        out_shape=jax.ShapeDtypeStruct((MOD_ROWS, n), F32),
        compiler_params=pltpu.CompilerParams(
            vmem_limit_bytes=_vmem_limit(3 * D_MODEL * tn * 4)),
        name="modulation",
    )(cond, w_ada, b_ada.reshape(1, n))


def _norm_mod(x, g, shift, scale):
    ms = jnp.mean(x * x, axis=-1, keepdims=True)
    y = x * lax.rsqrt(ms + EPS)
    return (y * g) * (1.0 + scale) + shift


def _inproj_kernel(*refs, rope):
    if rope:
        x_ref, mod_ref, g_ref, w_ref, cos_ref, sa_ref, sb_ref, q_ref, k_ref, v_ref, p_ref = refs
    else:
        x_ref, mod_ref, g_ref, w_ref, q_ref, k_ref, v_ref, p_ref, ks_ref, vs_ref = refs
    mod = mod_ref[0]
    for s in range(x_ref.shape[0] // SUB_TILE):
        rows = slice(s * SUB_TILE, (s + 1) * SUB_TILE)
        h = _norm_mod(x_ref[rows, :], g_ref[...], mod[0:1], mod[1:2])
        u = jnp.dot(h.astype(BF16), w_ref[...], preferred_element_type=F32)

        def rot(xh, rows=rows):
            return (xh * cos_ref[rows, :] + pltpu.roll(xh, LANES - ROPE_FREQS, 1) * sa_ref[rows, :]
                    + pltpu.roll(xh, ROPE_FREQS, 1) * sb_ref[rows, :])

        for hd in range(N_HEADS):
            xh = u[:, hd * HEAD_DIM:(hd + 1) * HEAD_DIM] * ATTN_SCALE
            q_ref[rows, hd * HEAD_DIM:(hd + 1) * HEAD_DIM] = (rot(xh) if rope else xh).astype(BF16)
        for hd in range(N_KV_HEADS):
            lo = ATTN_WIDTH + hd * HEAD_DIM
            xh = u[:, lo:lo + HEAD_DIM]
            k_ref[rows, hd * HEAD_DIM:(hd + 1) * HEAD_DIM] = rot(xh) if rope else xh
        v_ref[rows, :] = u[:, ATTN_WIDTH + KV_WIDTH:ATTN_WIDTH + 2 * KV_WIDTH]
        p_ref[rows, :] = u[:, ATTN_WIDTH + 2 * KV_WIDTH:]
        if not rope:
            for hd in range(N_KV_HEADS):
                state_rows = pl.ds(s * SUB_TILE * N_KV_HEADS + hd, SUB_TILE, stride=N_KV_HEADS)
                lo = ATTN_WIDTH + hd * HEAD_DIM
                ks_ref[state_rows, :] = u[:, lo:lo + HEAD_DIM]
                vs_ref[state_rows, :] = u[:, lo + KV_WIDTH:lo + KV_WIDTH + HEAD_DIM]


def _in_projection(x2, mod3, mod_row, g, w_in_bf, rope_tabs):
    t = x2.shape[0]
    tm = PROJ_TILE
    rope = rope_tabs is not None
    row = lambda i: (i, 0)
    in_specs = [pl.BlockSpec((tm, D_MODEL), row),
                pl.BlockSpec((1, N_MOD, D_MODEL), lambda i: (mod_row(i), 0, 0)),
                pl.BlockSpec((1, D_MODEL), lambda i: (0, 0)),
                pl.BlockSpec((D_MODEL, IN_WIDTH), lambda i: (0, 0))]
    args = [x2, mod3, g, w_in_bf]
    if rope:
        n_seq = rope_tabs[0].shape[0]
        seq_blocks = n_seq // tm
        for tab in rope_tabs:
            in_specs.append(pl.BlockSpec((tm, HEAD_DIM), lambda i: (i % seq_blocks, 0)))
            args.append(tab)
    out_specs = [pl.BlockSpec((tm, ATTN_WIDTH), row),
                 pl.BlockSpec((tm, KV_WIDTH), row),
                 pl.BlockSpec((tm, KV_WIDTH), row),
                 pl.BlockSpec((tm, POOL_WIDTH), row)]
    out_shape = [jax.ShapeDtypeStruct((t, ATTN_WIDTH), BF16),
                 jax.ShapeDtypeStruct((t, KV_WIDTH), F32),
                 jax.ShapeDtypeStruct((t, KV_WIDTH), F32),
                 jax.ShapeDtypeStruct((t, POOL_WIDTH), F32)]
    if not rope:
        for _ in range(2):
            out_specs.append(pl.BlockSpec((tm * N_KV_HEADS, HEAD_DIM), row))
            out_shape.append(jax.ShapeDtypeStruct((t * N_KV_HEADS, HEAD_DIM), F32))
    return pl.pallas_call(
        functools.partial(_inproj_kernel, rope=rope),
        grid=(t // tm,),
        in_specs=in_specs,
        out_specs=out_specs,
        out_shape=out_shape,
        compiler_params=pltpu.CompilerParams(
            vmem_limit_bytes=_vmem_limit(2 * D_MODEL * IN_WIDTH * 2 + 24 * tm * D_MODEL * 4)),
        name="in_projection",
    )(*args)


def _softmax_pv(s_list, v_list, sink_col):
    m = sink_col
    for s in s_list:
        m = jnp.maximum(m, jnp.max(s, axis=-1, keepdims=True))
    denom = jnp.exp2(sink_col - m)
    out = None
    for s, v in zip(s_list, v_list):
        e = jnp.exp2(s - m)
        if v.shape[1] == HEAD_DIM:
            denom = denom + jnp.sum(e, axis=-1, keepdims=True)
        o = jnp.dot(e.astype(BF16), v, preferred_element_type=F32)
        out = o if out is None else out + o
    if out.shape[1] > HEAD_DIM:
        denom = denom + out[:, HEAD_DIM:HEAD_DIM + 1]
    return out[:, 0:HEAD_DIM] * (1.0 / denom)


def _with_ones(v):
    return jnp.concatenate([v.astype(BF16), jnp.ones(v.shape, BF16)], axis=1)


def _stack_heads(q, kv):
    return jnp.concatenate(
        [q[:, (kv * Q_PER_KV + g) * HEAD_DIM:(kv * Q_PER_KV + g + 1) * HEAD_DIM]
         for g in range(Q_PER_KV)], axis=0)


def _sink_column(sink_ref, kv, rows):
    r = lax.broadcasted_iota(I32, (Q_PER_KV * rows, 1), 0)
    col = jnp.zeros((Q_PER_KV * rows, 1), F32)
    for g in range(Q_PER_KV):
        col = jnp.where((r >= g * rows) & (r < (g + 1) * rows), sink_ref[kv * Q_PER_KV + g], col)
    return col * LOG2_E


def _qk(q, k):
    return lax.dot_general(q, k, (((1,), (1,)), ((), ())), preferred_element_type=F32)


def _ctx_attn_kernel(sink_ref, q_ref, k_ref, v_ref, o_ref):
    rows = q_ref.shape[0]
    q = q_ref[...]
    for kv in range(N_KV_HEADS):
        kh = k_ref[:, kv * HEAD_DIM:(kv + 1) * HEAD_DIM].astype(BF16)
        vh = v_ref[:, kv * HEAD_DIM:(kv + 1) * HEAD_DIM].astype(BF16)
        qs = _stack_heads(q, kv)
        s = _qk(qs, kh)
        o = _softmax_pv([s], [vh], _sink_column(sink_ref, kv, rows))
        for g in range(Q_PER_KV):
            hd = kv * Q_PER_KV + g
            o_ref[:, hd * HEAD_DIM:(hd + 1) * HEAD_DIM] = o[g * rows:(g + 1) * rows].astype(BF16)


def _context_attention(q, k, v, sink, seq):
    t = q.shape[0]
    row = lambda b: (b, 0)
    return pl.pallas_call(
        _ctx_attn_kernel,
        grid=(t // seq,),
        in_specs=[pl.BlockSpec(memory_space=pltpu.SMEM),
                  pl.BlockSpec((seq, ATTN_WIDTH), row),
                  pl.BlockSpec((seq, KV_WIDTH), row),
                  pl.BlockSpec((seq, KV_WIDTH), row)],
        out_specs=pl.BlockSpec((seq, ATTN_WIDTH), row),
        out_shape=jax.ShapeDtypeStruct((t, ATTN_WIDTH), BF16),
        name="context_attention",
    )(sink, q, k, v)


def _lat_attn_kernel(sink_ref, q_ref, k_ref, v_ref, ck_ref, cv_ref, o_ref, *, n_seq):
    i = pl.program_id(1)
    band = 3 * BLOCK
    start = pl.multiple_of(jnp.clip((i - 1) * BLOCK, 0, n_seq - band), BLOCK)
    rows = Q_PER_KV * BLOCK
    qpos = i * BLOCK + lax.broadcasted_iota(I32, (rows, band), 0) % BLOCK
    kpos = start + lax.broadcasted_iota(I32, (rows, band), 1)
    mask = jnp.abs(kpos - qpos) <= WINDOW
    q = q_ref[...]
    for kv in range(N_KV_HEADS):
        cols = slice(kv * HEAD_DIM, (kv + 1) * HEAD_DIM)
        kb = k_ref[pl.ds(start, band), cols].astype(BF16)
        vb = _with_ones(v_ref[pl.ds(start, band), cols])
        cached = pl.ds(kv, ck_ref.shape[0] // N_KV_HEADS, stride=N_KV_HEADS)
        ck = ck_ref[cached, :].astype(BF16)
        cv = _with_ones(cv_ref[cached, :])
        qs = _stack_heads(q, kv)
        s_loc = jnp.where(mask, _qk(qs, kb), NEG)
        s_ctx = _qk(qs, ck)
        o = _softmax_pv([s_loc, s_ctx], [vb, cv], _sink_column(sink_ref, kv, BLOCK))
        for g in range(Q_PER_KV):
            hd = kv * Q_PER_KV + g
            o_ref[:, hd * HEAD_DIM:(hd + 1) * HEAD_DIM] = o[g * BLOCK:(g + 1) * BLOCK].astype(BF16)


def _latent_attention(q, k, v, ck, cv, sink, n_seq):
    t = q.shape[0]
    nb = n_seq // BLOCK
    cache_rows = ck.shape[0] // (t // n_seq)
    return pl.pallas_call(
        functools.partial(_lat_attn_kernel, n_seq=n_seq),
        grid=(t // n_seq, nb),
        in_specs=[pl.BlockSpec(memory_space=pltpu.SMEM),
                  pl.BlockSpec((BLOCK, ATTN_WIDTH), lambda b, i: (b * nb + i, 0)),
                  pl.BlockSpec((n_seq, KV_WIDTH), lambda b, i: (b, 0)),
                  pl.BlockSpec((n_seq, KV_WIDTH), lambda b, i: (b, 0)),
                  pl.BlockSpec((cache_rows, HEAD_DIM), lambda b, i: (b, 0)),
                  pl.BlockSpec((cache_rows, HEAD_DIM), lambda b, i: (b, 0))],
        out_specs=pl.BlockSpec((BLOCK, ATTN_WIDTH), lambda b, i: (b * nb + i, 0)),
        out_shape=jax.ShapeDtypeStruct((t, ATTN_WIDTH), BF16),
        name="latent_attention",
    )(sink, q, k, v, ck, cv)


def _pool_group(p_ref, r0, seq, w_ref, s_ref, g):
    n = SUB_TILE
    rows = n + 2 * POOL_HALO
    static = isinstance(r0, int)
    t0 = r0 % seq
    t = t0 + lax.broadcasted_iota(I32, (n, 1), 0)
    has_top = t0 > 0
    has_bottom = t0 + n < seq
    zeros = jnp.zeros((POOL_HALO, POOL_GROUP), F32)
    w = POOL_SIZES[g]
    cols = slice(g * POOL_GROUP, (g + 1) * POOL_GROUP)
    pg = p_ref[pl.ds(r0, n), cols]
    if static:
        top = p_ref[r0 - POOL_HALO:r0, cols] if has_top else zeros
        bottom = p_ref[r0 + n:r0 + n + POOL_HALO, cols] if has_bottom else zeros
    else:
        top_row = pl.multiple_of(jnp.maximum(r0 - POOL_HALO, 0), POOL_HALO)
        bottom_row = pl.multiple_of(jnp.minimum(r0 + n, p_ref.shape[0] - POOL_HALO), POOL_HALO)
        top = jnp.where(has_top, p_ref[pl.ds(top_row, POOL_HALO), cols], 0.0)
        bottom = jnp.where(has_bottom, p_ref[pl.ds(bottom_row, POOL_HALO), cols], 0.0)
    x = jnp.concatenate([top, pg, bottom], axis=0)
    fwd = x
    span = 1
    while span < w // 2:
        fwd = fwd + pltpu.roll(fwd, rows - span, 0)
        span *= 2
    if (w // 2) % SUBLANES == 0:
        wsum = fwd[POOL_HALO - w // 2:POOL_HALO - w // 2 + n] + fwd[POOL_HALO:POOL_HALO + n]
    else:
        wsum = (fwd + pltpu.roll(fwd, w // 2, 0))[POOL_HALO:POOL_HALO + n]
    lo = jnp.maximum(t - w // 2, 0)
    hi = jnp.minimum(t + w - w // 2, seq)
    inv_cnt = 1.0 / (hi - lo).astype(F32)
    mixed = wsum * inv_cnt - pg
    y = jnp.dot(mixed.astype(BF16), w_ref[g], preferred_element_type=F32)
    return (y * s_ref[:, cols]).astype(BF16)


def _pack_pair(lo, hi):
    return lax.bitcast_convert_type(pltpu.pack_elementwise([lo, hi], packed_dtype=BF16), I32)


def _unpack_pair(words):
    lo = pltpu.unpack_elementwise(words, index=0, packed_dtype=BF16, unpacked_dtype=F32)
    hi = pltpu.unpack_elementwise(words, index=1, packed_dtype=BF16, unpacked_dtype=F32)
    return lo, hi


def _outproj_kernel(a_ref, p_ref, x_ref, mod_ref, g_ref, wo_ref, wr_ref, pw_ref, ps_ref,
                    x1_ref, h_ref, aff_ref, *, seq):
    mod = mod_ref[0]
    groups = SUB_TILE // SUBLANES
    half = D_MODEL // 2
    tm = x_ref.shape[0]
    steps_per_p_block = p_ref.shape[0] // tm
    for s in range(tm // SUB_TILE):
        rows = slice(s * SUB_TILE, (s + 1) * SUB_TILE)
        grp = slice(s * groups, (s + 1) * groups)
        r0 = s * SUB_TILE
        if steps_per_p_block > 1:
            r0 = pl.multiple_of((pl.program_id(0) % steps_per_p_block) * tm + r0, SUB_TILE)
        pooled = jnp.concatenate([_pool_group(p_ref, r0, seq, pw_ref, ps_ref, g)
                                  for g in range(len(POOL_SIZES))], axis=1)
        mix = (jnp.dot(a_ref[rows, :], wo_ref[0:ATTN_WIDTH, :], preferred_element_type=F32)
               + jnp.dot(pooled, wo_ref[ATTN_WIDTH:D_MODEL, :], preferred_element_type=F32))
        x1 = x_ref[rows, :] + mod[2:3] * mix
        x1_ref[rows, :] = x1
        h = _norm_mod(x1, g_ref[...], mod[3:4], mod[4:5])
        logits = jnp.dot(h.astype(BF16), wr_ref[...], preferred_element_type=F32)
        lane = lax.broadcasted_iota(I32, logits.shape, 1)
        logits = jnp.where(lane < N_EXPERTS, logits, -jnp.inf)
        m = jnp.max(logits, axis=-1, keepdims=True)
        e = jnp.exp(logits - m)
        aff = e / jnp.sum(e, axis=-1, keepdims=True)
        aff_ref[rows, :] = aff[:, 0:N_EXPERTS]
        for c in range(HQ_TILES - 1):
            words = _pack_pair(h[:, c * LANES:(c + 1) * LANES],
                               h[:, half + c * LANES:half + (c + 1) * LANES])
            h_ref[grp, c * SUBLANES:(c + 1) * SUBLANES, :] = (
                words.reshape(groups, SUBLANES, LANES))
        h_ref[grp, (HQ_TILES - 1) * SUBLANES:, :] = (
            lax.bitcast_convert_type(aff, I32).reshape(groups, SUBLANES, LANES))


def _out_projection(attn, p, x2, mod3, mod_row, g, w_out_bf, w_router, pool_w, pool_scale, seq):
    t = x2.shape[0]
    tm = PROJ_TILE
    row = lambda i: (i, 0)
    p_rows = max(tm, seq)
    steps_per_p_block = p_rows // tm
    return pl.pallas_call(
        functools.partial(_outproj_kernel, seq=seq),
        grid=(t // tm,),
        in_specs=[pl.BlockSpec((tm, ATTN_WIDTH), row),
                  pl.BlockSpec((p_rows, POOL_WIDTH), lambda i: (i // steps_per_p_block, 0)),
                  pl.BlockSpec((tm, D_MODEL), row),
                  pl.BlockSpec((1, N_MOD, D_MODEL), lambda i: (mod_row(i), 0, 0)),
                  pl.BlockSpec((1, D_MODEL), lambda i: (0, 0)),
                  pl.BlockSpec((D_MODEL, D_MODEL), lambda i: (0, 0)),
                  pl.BlockSpec((D_MODEL, LANES), lambda i: (0, 0)),
                  pl.BlockSpec((len(POOL_SIZES), POOL_GROUP, POOL_GROUP), lambda i: (0, 0, 0)),
                  pl.BlockSpec((1, POOL_WIDTH), lambda i: (0, 0))],
        out_specs=[pl.BlockSpec((tm, D_MODEL), row),
                   pl.BlockSpec((tm // SUBLANES, HQ_TILES * SUBLANES, LANES), lambda i: (i, 0, 0)),
                   pl.BlockSpec((tm, N_EXPERTS), row)],
        out_shape=[jax.ShapeDtypeStruct((t, D_MODEL), F32),
                   jax.ShapeDtypeStruct((t // SUBLANES, HQ_TILES * SUBLANES, LANES), I32),
                   jax.ShapeDtypeStruct((t, N_EXPERTS), F32)],
        compiler_params=pltpu.CompilerParams(
            vmem_limit_bytes=_vmem_limit(2 * D_MODEL * D_MODEL * 2 + 24 * tm * D_MODEL * 4)),
        name="out_projection",
    )(attn, p, x2, mod3, g, w_out_bf, w_router, pool_w, pool_scale)


def _route_kernel(a_ref, idx_ref, slot_ref, off_ref, slot_scr, *, cap):
    a = a_ref[...]
    n_e, n_c, _ = a.shape
    rows = n_e * n_c

    def enough(cand):
        cand_f = lax.bitcast_convert_type(cand, F32)
        cnt = jnp.sum(jnp.sum((a >= cand_f).astype(F32), axis=1, keepdims=True),
                      axis=2, keepdims=True)
        return cnt >= cap

    def two_bits(it, thr):
        low = 28 - 2 * it
        for setting in (1, 2, 3):
            cand = thr | jnp.left_shift(jnp.int32(setting), low)
            best = jnp.where(enough(cand), cand, thr if setting == 1 else best)
        return best

    top = jnp.full((n_e, 1, 1), 1 << 30, I32)
    thr = jnp.where(enough(top), top, 0)
    thr = lax.fori_loop(0, 15, two_bits, thr)
    thr_f = lax.bitcast_convert_type(thr, F32)
    gt = (a > thr_f).astype(F32).reshape(rows, LANES)
    eq = (a == thr_f).astype(F32).reshape(rows, LANES)

    li = lax.broadcasted_iota(I32, (LANES, LANES), 0)
    lj = lax.broadcasted_iota(I32, (LANES, LANES), 1)
    upper_incl = (li <= lj).astype(BF16)
    ci = lax.broadcasted_iota(I32, (n_c, n_c), 0)
    cj = lax.broadcasted_iota(I32, (n_c, n_c), 1)
    before = (cj < ci).astype(BF16)
    whole = jnp.ones((n_c, n_c), BF16)

    def over_chunks(mat, col):
        wide = jnp.broadcast_to(col, (rows, LANES)).astype(BF16)
        side_by_side = jnp.concatenate([wide[e * n_c:(e + 1) * n_c] for e in range(n_e)], axis=1)
        res = jnp.dot(mat, side_by_side, preferred_element_type=F32)
        return jnp.concatenate([res[:, e * LANES:(e + 1) * LANES] for e in range(n_e)], axis=0)

    def prefix(x):
        incl = jnp.dot(x.astype(BF16), upper_incl, preferred_element_type=F32)
        tot = incl[:, LANES - 1:LANES]
        return incl, tot, over_chunks(before, tot)

    n_gt = over_chunks(whole, jnp.sum(gt, axis=1, keepdims=True))
    need = cap - n_gt
    incl_eq, _, off_eq = prefix(eq)
    rank_eq = off_eq + incl_eq - eq
    sel = jnp.where((eq > 0) & (rank_eq < need), 1.0, gt)
    incl, tot, off = prefix(sel)
    slot = off + incl - sel
    slot_scr[...] = jnp.where(sel > 0, slot, -1.0)
    for c in range(n_c):
        per_expert = slot_scr[pl.ds(c, n_e, stride=n_c), :]
        slot_ref[c * LANES:(c + 1) * LANES, :] = per_expert.T.astype(I32)
    off_ref[...] = off.astype(I32).reshape(n_e, n_c, LANES)

    s_lane = lax.broadcasted_iota(I32, (1, cap), 1).astype(F32)
    c_col = lax.broadcasted_iota(I32, (n_c, 1), 0).astype(F32)
    for e in range(n_e):
        r0 = e * n_c
        incl_e = incl[r0:r0 + n_c]
        off_e = off[r0:r0 + n_c, 0:1]
        tot_e = tot[r0:r0 + n_c]
        onehot = ((off_e <= s_lane) & (s_lane < off_e + tot_e)).astype(F32)
        counts = lax.dot_general(incl_e.astype(BF16), onehot.astype(BF16),
                                 (((0,), (0,)), ((), ())), preferred_element_type=F32)
        local = s_lane - jnp.sum(onehot * off_e, axis=0, keepdims=True)
        lane = jnp.sum((counts <= local).astype(F32), axis=0, keepdims=True)
        chunk = jnp.sum(onehot * c_col, axis=0, keepdims=True)
        idx_ref[e] = (chunk * LANES + lane).astype(I32)


def _routing(aff, cap):
    t = aff.shape[0]
    n_c = t // LANES
    a3 = aff.T.reshape(N_EXPERTS, n_c, LANES)
    return pl.pallas_call(
        functools.partial(_route_kernel, cap=cap),
        out_shape=[jax.ShapeDtypeStruct((N_EXPERTS, 1, cap), I32),
                   jax.ShapeDtypeStruct((t, N_EXPERTS), I32),
                   jax.ShapeDtypeStruct((N_EXPERTS, n_c, LANES), I32)],
        scratch_shapes=[pltpu.VMEM((N_EXPERTS * n_c, LANES), F32)],
        compiler_params=pltpu.CompilerParams(vmem_limit_bytes=_vmem_limit(48 << 20)),
        name="routing",
    )(a3)


def _gather_rows(table, row_ids):
    n_chunks, chunk = row_ids.shape
    n_workers = SC_CORES * SC_SUBCORES
    per_worker = n_chunks // n_workers
    assert chunk <= GATHER_CHUNK and chunk % SUBLANES == 0 and n_chunks % n_workers == 0
    mesh = plsc.VectorSubcoreMesh(core_axis_name="core", subcore_axis_name="subcore")

    @functools.partial(
        pl.kernel, mesh=mesh,
        out_type=jax.ShapeDtypeStruct((n_chunks * chunk, LANES), I32),
        scratch_types=[pltpu.VMEM((per_worker, chunk), I32),
                       pltpu.VMEM((2, chunk, LANES), I32),
                       pltpu.SemaphoreType.DMA((2,)),
                       pltpu.SemaphoreType.DMA((2,))],
        name="gather_rows",
    )
    def gather(table_hbm, ids_hbm, out_hbm, ids_v, rows_v, gather_sem, store_sem):
        worker = lax.axis_index("subcore") * SC_CORES + lax.axis_index("core")
        first = worker * per_worker
        pltpu.sync_copy(ids_hbm.at[worker], ids_v)

        def fetch(j):
            return pltpu.make_async_copy(table_hbm.at[ids_v.at[j]], rows_v.at[j % 2],
                                         gather_sem.at[j % 2])

        def store(j):
            rows = pl.ds(pl.multiple_of((first + j) * chunk, chunk), chunk)
            return pltpu.make_async_copy(rows_v.at[j % 2], out_hbm.at[rows], store_sem.at[j % 2])

        for j in range(per_worker):
            if j >= 2:
                store(j - 2).wait()
            fetch(j).start()
            if j >= 1:
                fetch(j - 1).wait()
                store(j - 1).start()
        fetch(per_worker - 1).wait()
        store(per_worker - 1).start()
        for j in range(max(per_worker - 2, 0), per_worker):
            store(j).wait()

    return gather(table, row_ids.reshape(n_workers, per_worker, chunk))


def _packed_row_ids(idx, cap):
    chunk = min(GATHER_CHUNK, cap // GATHER_SPLIT)
    tok = idx.reshape(-1, 1, cap // chunk, chunk)
    tile = jnp.arange(HQ_TILES, dtype=I32).reshape(1, HQ_TILES, 1, 1)
    ids = ((tok // SUBLANES) * HQ_TILES + tile) * SUBLANES + tok % SUBLANES
    return ids.reshape(-1, chunk)


def _ffn_kernel(*refs, caps, n_f, row_chunk, first_expert, n_prior):
    n_g = len(caps)
    x_refs = refs[:n_g]
    wg_ref, wu_ref, wd_ref = refs[n_g:n_g + 3]
    y_refs = refs[n_g + 3 + n_prior:2 * n_g + 3 + n_prior]
    acc_ref = refs[2 * n_g + 3 + n_prior]
    e = pl.program_id(0)
    f = pl.program_id(1)

    @pl.when((e == 0) & (f == 0))
    def _():
        acc_ref[...] = jnp.zeros_like(acc_ref)

    def step(last):
        first = f == 0
        base = 0
        for x_ref, y_ref, cap in zip(x_refs, y_refs, caps):
            if last:
                aff = lax.bitcast_convert_type(x_ref[0, HQ_TILES - 1], F32)
                lane = lax.broadcasted_iota(I32, aff.shape, 1)
                gates = jnp.sum(jnp.where(lane == e + first_expert, aff, 0.0), axis=1,
                                keepdims=True)
                y_ref[0, cap:, :] = jnp.zeros((COMBINE_WINDOW, D_MODEL // 2), I32)
            for r in range(cap // row_chunk):
                rs = slice(r * row_chunk, (r + 1) * row_chunk)
                acc_rows = slice(base + r * row_chunk, base + (r + 1) * row_chunk)
                pairs = [_unpack_pair(x_ref[0, c, rs, :]) for c in range(HQ_TILES - 1)]
                x = jnp.concatenate([p[0] for p in pairs] + [p[1] for p in pairs], axis=1)
                gate_act = jnp.dot(x, wg_ref[0], preferred_element_type=F32)
                up = jnp.dot(x, wu_ref[0], preferred_element_type=F32)
                hid = (gate_act * jax.nn.sigmoid(gate_act)) * up
                part = jnp.dot(hid, wd_ref[0], preferred_element_type=F32)
                if last:
                    y = (part + acc_ref[acc_rows, :]) * gates[rs]
                    y_ref[0, rs, :] = _pack_pair(y[:, 0:D_MODEL // 2], y[:, D_MODEL // 2:])
                else:
                    acc_ref[acc_rows, :] = part + jnp.where(first, 0.0, acc_ref[acc_rows, :])
            base += cap

    @pl.when(f < n_f - 1)
    def _():
        step(False)

    @pl.when(f == n_f - 1)
    def _():
        step(True)


def _expert_ffn(xs_groups, w_gate, w_up, w_down, first_expert, prior_outputs):
    n_f = 4
    tf = D_EXPERT // n_f
    row_chunk = 512
    n_e = xs_groups[0].shape[0]
    e0 = first_expert
    caps = tuple(xs.shape[2] for xs in xs_groups)
    rows = sum(caps)
    est = (2 * HQ_TILES * rows * LANES * 4 + rows * D_MODEL * (4 + 2 * 2) + 2 * 3 * D_MODEL * tf * 4
           + row_chunk * (3 * tf + 2 * D_MODEL) * 4)
    x_specs = [pl.BlockSpec((1, HQ_TILES, cap, LANES), lambda e, f: (e, 0, 0, 0)) for cap in caps]
    y_shapes = [(N_EXPERTS, cap + COMBINE_WINDOW, D_MODEL // 2) for cap in caps]
    y_specs = [pl.BlockSpec((1,) + s[1:], lambda e, f: (e + e0, 0, 0)) for s in y_shapes]
    prior = list(prior_outputs or [])
    n_in = len(caps) + 3
    return pl.pallas_call(
        functools.partial(_ffn_kernel, caps=caps, n_f=n_f, row_chunk=row_chunk,
                          first_expert=e0, n_prior=len(prior)),
        grid=(n_e, n_f),
        in_specs=(x_specs + [pl.BlockSpec((1, D_MODEL, tf), lambda e, f: (e + e0, 0, f)),
                             pl.BlockSpec((1, D_MODEL, tf), lambda e, f: (e + e0, 0, f)),
                             pl.BlockSpec((1, tf, D_MODEL), lambda e, f: (e + e0, f, 0))]
                  + [pl.BlockSpec(memory_space=pl.ANY) for _ in prior]),
        out_specs=y_specs,
        out_shape=[jax.ShapeDtypeStruct(s, I32) for s in y_shapes],
        input_output_aliases={n_in + j: j for j in range(len(prior))},
        scratch_shapes=[pltpu.VMEM((rows, D_MODEL), F32)],
        compiler_params=pltpu.CompilerParams(
            dimension_semantics=("arbitrary", "arbitrary"),
            vmem_limit_bytes=_vmem_limit(est + (4 << 20))),
        name="expert_ffn",
    )(*xs_groups, w_gate, w_up, w_down, *prior)


def _combine_kernel(start_ref, nch_ref, wide_ref, x1_ref, mod_ref, g_ref, slot_ref, y_hbm, o_ref,
                    ybuf_ref, cols_ref, ffn_ref, sem, *, first_k, kblock):
    i = pl.program_id(0)
    tm = x1_ref.shape[0]
    cur = i % 2
    n_window_rows = N_EXPERTS * COMBINE_WINDOW

    window_head = COMBINE_WINDOW - COMBINE_CHUNK

    def window_copy(e, src_row, buf):
        return pltpu.make_async_copy(
            y_hbm.at[e, pl.ds(pl.multiple_of(src_row, SUBLANES), window_head)],
            ybuf_ref.at[buf, e * COMBINE_WINDOW:e * COMBINE_WINDOW + window_head], sem.at[buf])

    def window_tail_copy(e, src_row, buf):
        return chunk_copy(e, src_row + window_head, buf,
                          jnp.int32(e * COMBINE_WINDOW + window_head))

    def needs_tail(tile, e):
        return nch_ref[tile, e] * COMBINE_CHUNK > window_head

    def chunk_copy(e, src_row, buf, dst_row):
        return pltpu.make_async_copy(
            y_hbm.at[e, pl.ds(pl.multiple_of(src_row, SUBLANES), COMBINE_CHUNK)],
            ybuf_ref.at[buf, pl.ds(pl.multiple_of(dst_row, COMBINE_CHUNK), COMBINE_CHUNK)],
            sem.at[buf])

    def fetch(tile, buf):
        @pl.when(wide_ref[tile] == 0)
        def _():
            for e in range(N_EXPERTS):
                window_copy(e, start_ref[tile, e], buf).start()

                @pl.when(needs_tail(tile, e))
                def _(e=e):
                    window_tail_copy(e, start_ref[tile, e], buf).start()

        @pl.when(wide_ref[tile] != 0)
        def _():
            pos = jnp.int32(0)
            for e in range(N_EXPERTS):
                start = start_ref[tile, e]
                nch = nch_ref[tile, e]

                def issue(c, carry, e=e, start=start, pos=pos):
                    chunk_copy(e, start + c * COMBINE_CHUNK, buf, pos + c * COMBINE_CHUNK).start()
                    return carry
                lax.fori_loop(0, nch, issue, 0)
                pos = pos + nch * COMBINE_CHUNK

    @pl.when(i == 0)
    def _():
        ybuf_ref[...] = jnp.zeros_like(ybuf_ref)
        fetch(0, 0)

    @pl.when(i + 1 < pl.num_programs(0))
    def _():
        fetch(i + 1, 1 - cur)

    def finish(ffn):
        out = x1_ref[...] + mod_ref[0][5:6] * ffn
        ms = jnp.mean(out * out, axis=-1, keepdims=True)
        o_ref[...] = (out * lax.rsqrt(ms + EPS)) * g_ref[...]

    def apply_selection(sel, k0, width):
        lo, hi = _unpack_pair(ybuf_ref[cur, pl.ds(k0, width), :])
        return jnp.concatenate([jnp.dot(sel, lo, preferred_element_type=F32),
                                jnp.dot(sel, hi, preferred_element_type=F32)], axis=1)

    @pl.when(wide_ref[i] == 0)
    def _():
        for e in range(N_EXPERTS):
            window_copy(e, jnp.int32(0), cur).wait()

            @pl.when(needs_tail(i, e))
            def _(e=e):
                window_tail_copy(e, jnp.int32(0), cur).wait()
        expert_lane = lax.broadcasted_iota(I32, (1, N_EXPERTS), 1)
        starts = jnp.zeros((1, N_EXPERTS), I32)
        for e in range(N_EXPERTS):
            starts = jnp.where(expert_lane == e, start_ref[i, e], starts)
        slots = slot_ref[...]
        rows_in_window = jnp.where(slots >= 0, (slots - starts).astype(F32), -1.0)
        window_of_lane = lax.broadcasted_iota(I32, (N_EXPERTS, n_window_rows), 1) // COMBINE_WINDOW
        spread = (window_of_lane == lax.broadcasted_iota(I32, (N_EXPERTS, n_window_rows), 0))
        target = jnp.dot(rows_in_window.astype(BF16), spread.astype(BF16),
                         preferred_element_type=F32)
        lane_row = (lax.broadcasted_iota(I32, (tm, n_window_rows), 1) % COMBINE_WINDOW).astype(F32)
        sel = jnp.where(target == lane_row, 1.0, 0.0)
        ffn = None
        for k0 in range(0, n_window_rows, kblock):
            part = apply_selection(sel[:, k0:k0 + kblock], k0, kblock)
            ffn = part if ffn is None else ffn + part
        finish(ffn)

    @pl.when(wide_ref[i] != 0)
    def _():
        pos = jnp.int32(0)
        begins = []
        for e in range(N_EXPERTS):
            begins.append(pos)
            pos = pos + nch_ref[i, e] * COMBINE_CHUNK
        begins.append(pos)

        def drain(c, carry):
            chunk_copy(0, jnp.int32(0), cur, jnp.int32(0)).wait()
            return carry
        lax.fori_loop(0, pos // COMBINE_CHUNK, drain, 0)

        slots = slot_ref[...]
        for e in range(N_EXPERTS):
            s_e = slots[:, e:e + 1]
            col = jnp.where(s_e >= 0, s_e + (begins[e] - start_ref[i, e]), -1)
            cols_ref[e] = jnp.broadcast_to(col, (tm, LANES))

        lane = lax.broadcasted_iota(I32, (tm, LANES), 1)

        def selection(k0, width):
            halves = []
            for h in range(width // LANES):
                c0 = k0 + h * LANES
                target = lane + c0
                e_lo = jnp.int32(0)
                e_hi = jnp.int32(0)
                for e in range(N_EXPERTS):
                    e_lo = e_lo + (begins[e + 1] <= c0).astype(I32)
                    e_hi = e_hi + (begins[e] < c0 + LANES).astype(I32)

                def mark(e, hit, target=target):
                    return jnp.where(cols_ref[e] == target, 1.0, hit)
                halves.append(lax.fori_loop(e_lo, e_hi, mark, jnp.zeros((tm, LANES), F32)))
            return jnp.concatenate(halves, axis=1)

        ffn_ref[...] = apply_selection(selection(0, first_k), 0, first_k)

        def kstep(kb, carry):
            k0 = pl.multiple_of(kb * kblock, kblock)
            ffn_ref[...] += apply_selection(selection(k0, kblock), k0, kblock)
            return carry
        lax.fori_loop(first_k // kblock, (pos + kblock - 1) // kblock, kstep, 0)
        finish(ffn_ref[...])


def _combine(x1, mod3, mod_row, g_final, slot_t, tile_start, tile_nch, tile_wide, y):
    t = x1.shape[0]
    tm = TOKEN_TILE
    first_k = 2 * tm + N_EXPERTS * COMBINE_CHUNK
    kblock = 256
    max_rows = N_EXPERTS * (tm + 2 * COMBINE_CHUNK)
    max_rows = -(-max_rows // kblock) * kblock
    row = lambda i, *_: (i, 0)
    grid_spec = pltpu.PrefetchScalarGridSpec(
        num_scalar_prefetch=3,
        grid=(t // tm,),
        in_specs=[pl.BlockSpec((tm, D_MODEL), row),
                  pl.BlockSpec((1, N_MOD, D_MODEL), lambda i, *_: (mod_row(i), 0, 0)),
                  pl.BlockSpec((1, D_MODEL), lambda i, *_: (0, 0)),
                  pl.BlockSpec((tm, N_EXPERTS), row),
                  pl.BlockSpec(memory_space=pl.ANY)],
        out_specs=pl.BlockSpec((tm, D_MODEL), row),
        scratch_shapes=[pltpu.VMEM((2, max_rows, D_MODEL // 2), I32),
                        pltpu.VMEM((N_EXPERTS, tm, LANES), I32),
                        pltpu.VMEM((tm, D_MODEL), F32),
                        pltpu.SemaphoreType.DMA((2,))],
    )
    return pl.pallas_call(
        functools.partial(_combine_kernel, first_k=first_k, kblock=kblock),
        grid_spec=grid_spec,
        out_shape=jax.ShapeDtypeStruct((t, D_MODEL), F32),
        compiler_params=pltpu.CompilerParams(
            dimension_semantics=("arbitrary",),
            vmem_limit_bytes=_vmem_limit(2 * max_rows * D_MODEL * 2 + 16 * tm * D_MODEL * 4)),
        name="combine",
    )(tile_start, tile_nch, tile_wide, x1, mod3, g_final, slot_t, y)


def _rope_tables(n):
    f32 = np.float32
    rows = n // GRID_W
    row = np.repeat(np.arange(rows, dtype=f32), GRID_W)
    col = np.tile(np.arange(GRID_W, dtype=f32), rows)
    inv = (f32(ROPE_THETA) ** (-np.arange(ROPE_FREQS, dtype=f32) / f32(ROPE_FREQS))).astype(f32)
    ang_r = row[:, None] * inv
    ang_c = col[:, None] * inv
    zero = np.zeros_like(ang_r)
    cos = np.concatenate([np.cos(ang_r)] * 2 + [np.cos(ang_c)] * 2, axis=1)
    sin_a = np.concatenate([-np.sin(ang_r), zero, -np.sin(ang_c), zero], axis=1)
    sin_b = np.concatenate([zero, np.sin(ang_r), zero, np.sin(ang_c)], axis=1)
    return tuple(jnp.asarray(t, F32) for t in (cos, sin_a, sin_b))


def _token_group(x, mod3, mod_row, seq, weights, rope_tabs, ctx_kv):
    (norm_mix, w_in_bf, sink, pool_w, pool_scale, w_out_bf, norm_ffn, w_router,
     w_gate, w_up, w_down, norm_final) = weights
    b = x.shape[0]
    t = b * seq
    x2 = x.reshape(t, D_MODEL)
    q, k, v, p, *state = _in_projection(x2, mod3, mod_row(PROJ_TILE), norm_mix, w_in_bf,
                                        rope_tabs)
    if ctx_kv is None:
        attn = _context_attention(q, k, v, sink, seq)
    else:
        attn = _latent_attention(q, k, v, ctx_kv[0], ctx_kv[1], sink, seq)
    x1, h_packed, aff = _out_projection(attn, p, x2, mod3, mod_row(PROJ_TILE), norm_ffn,
                                        w_out_bf, w_router, pool_w, pool_scale, seq)

    cap = EC_FACTOR * t // N_EXPERTS
    idx, slot_t, off3 = _routing(aff, cap)
    table = h_packed.reshape(-1, LANES)
    xs = []
    for first, count in EXPERT_RANGES:
        ids = _packed_row_ids(idx[first:first + count], cap)
        xs.append(_gather_rows(table, ids).reshape(count, HQ_TILES, cap, LANES))

    chunks_per_tile = TOKEN_TILE // LANES
    tile_off = off3[:, ::chunks_per_tile, 0]
    tile_end = jnp.concatenate([tile_off[:, 1:], jnp.full((N_EXPERTS, 1), cap, I32)], axis=1)
    tile_start = (tile_off // SUBLANES) * SUBLANES
    tile_nch = jnp.where(tile_end > tile_off,
                         (tile_end - tile_start + COMBINE_CHUNK - 1) // COMBINE_CHUNK, 0)
    tile_wide = jnp.any(tile_end - tile_start > COMBINE_WINDOW, axis=0).astype(I32)

    def finish(y):
        out = _combine(x1, mod3, mod_row(TOKEN_TILE), norm_final, slot_t, tile_start.T,
                       tile_nch.T, tile_wide, y)
        return out.reshape(b, seq, D_MODEL)
    return xs, finish, state


def kernel(x_prompt, x_sample, c, cache_k, cache_v, c_ctx, w_ada, b_ada, norm_mix, w_in,
           sink_logits, pool_w, pool_scale, w_out, norm_ffn, w_router, w_gate, w_up, w_down,
           norm_final):
    n_b, seq, _ = x_prompt.shape
    n_db, n_lat, _ = x_sample.shape
    assert 1 + n_db <= MOD_ROWS and seq == TOKEN_TILE and n_lat % PROJ_TILE == 0
    assert w_in.shape[0] == 1 and cache_k.shape[1] == 1
    assert (n_b * seq) % PROJ_TILE == 0

    cond = jnp.concatenate(
        [c_ctx[None, :], c, jnp.zeros((MOD_ROWS - 1 - n_db, D_MODEL), F32)], axis=0)
    mod3 = _modulation(cond, w_ada[0], b_ada[0]).reshape(MOD_ROWS, N_MOD, D_MODEL)

    w_router_bf = jnp.pad(w_router[0], ((0, 0), (0, LANES - N_EXPERTS))).astype(BF16)
    weights = (norm_mix[0][None, :], w_in[0].astype(BF16), sink_logits[0], pool_w[0].astype(BF16),
               pool_scale[0][None, :], w_out[0].astype(BF16), norm_ffn[0][None, :], w_router_bf,
               w_gate.reshape(w_gate.shape[1:]), w_up.reshape(w_up.shape[1:]),
               w_down.reshape(w_down.shape[1:]), norm_final[None, :])

    xs_p, finish_p, (k_p, v_p) = _token_group(
        x_prompt, mod3, lambda tile: (lambda i: 0), seq, weights, None, None)

    ck = cache_k.reshape(-1, HEAD_DIM)
    cv = cache_v.reshape(-1, HEAD_DIM)
    xs_l, finish_l, _ = _token_group(
        x_sample, mod3, lambda tile: (lambda i: 1 + i // (n_lat // tile)), n_lat, weights,
        _rope_tables(n_lat), (ck, cv))

    ys = None
    for r, (first, _) in enumerate(EXPERT_RANGES):
        ys = _expert_ffn([xs_p[r], xs_l[r]], weights[8], weights[9], weights[10], first, ys)
    y_prompt = finish_p(ys[0])
    y_sample = finish_l(ys[1])

    state_k = k_p.reshape(n_b, 1, seq, N_KV_HEADS, HEAD_DIM)
    state_v = v_p.reshape(n_b, 1, seq, N_KV_HEADS, HEAD_DIM)
    return (y_prompt, y_sample, state_k, state_v)
```

```python
import functools

import jax
import jax.numpy as jnp
import numpy as np
from jax import lax
from jax.experimental import pallas as pl
from jax.experimental.pallas import tpu as pltpu
from jax.experimental.pallas import tpu_sc as plsc

F32 = jnp.float32
BF16 = jnp.bfloat16
I32 = jnp.int32

D_MODEL = 2048
N_HEADS = 8
N_KV_HEADS = 2
HEAD_DIM = 128
Q_PER_KV = N_HEADS // N_KV_HEADS
ATTN_WIDTH = N_HEADS * HEAD_DIM
KV_WIDTH = N_KV_HEADS * HEAD_DIM
POOL_WIDTH = D_MODEL - ATTN_WIDTH
POOL_SIZES = (2, 4, 8, 16)
POOL_GROUP = POOL_WIDTH // len(POOL_SIZES)
IN_WIDTH = ATTN_WIDTH + 2 * KV_WIDTH + POOL_WIDTH
WINDOW = 128
BLOCK = 128
GRID_W = 64
ROPE_THETA = 10000.0
ROPE_FREQS = HEAD_DIM // 4
N_EXPERTS = 16
EC_FACTOR = 2
D_EXPERT = 1024
N_MOD = 6
EPS = 1e-6
NEG = -1e30
LOG2_E = 1.4426950408889634
ATTN_SCALE = HEAD_DIM ** -0.5 * LOG2_E

LANES = 128
SUBLANES = 8
BF16_ROWS = 16
VMEM_CAP = 64 * 1024 * 1024
SC_CORES = 2
SC_SUBCORES = 16

MOD_ROWS = 8
TOKEN_TILE = 256
PROJ_TILE = 512
SUB_TILE = 256
HQ_TILES = D_MODEL // 2 // LANES + 1
GATHER_CHUNK = 128
FFN_RANGES = 2
POOL_HALO = 8
COMBINE_CHUNK = BF16_ROWS
COMBINE_WINDOW = 64
ROW_SPLIT = 32


def _vmem_limit(nbytes):
    return int(min(VMEM_CAP - (4 << 20), max(nbytes, 16 << 20)))


def _mod_kernel(c_ref, w_ref, b_ref, o_ref):
    c = c_ref[...]
    s = c * jax.nn.sigmoid(c)
    o_ref[...] = jnp.dot(s.astype(BF16), w_ref[...].astype(BF16),
                         preferred_element_type=F32) + b_ref[...]


def _modulation(cond, w_ada, b_ada):
    n = w_ada.shape[1]
    tn = 1024
    return pl.pallas_call(
        _mod_kernel,
        grid=(n // tn,),
        in_specs=[pl.BlockSpec((MOD_ROWS, D_MODEL), lambda j: (0, 0)),
                  pl.BlockSpec((D_MODEL, tn), lambda j: (0, j)),
                  pl.BlockSpec((1, tn), lambda j: (0, j))],
        out_specs=pl.BlockSpec((MOD_ROWS, tn), lambda j: (0, j)),
        out_shape=jax.ShapeDtypeStruct((MOD_ROWS, n), F32),
        compiler_params=pltpu.CompilerParams(
            vmem_limit_bytes=_vmem_limit(3 * D_MODEL * tn * 4)),
        name="modulation",
    )(cond, w_ada, b_ada.reshape(1, n))


def _norm_mod(x, g, shift, scale):
    ms = jnp.mean(x * x, axis=-1, keepdims=True)
    y = x * lax.rsqrt(ms + EPS)
    return (y * g) * (1.0 + scale) + shift


def _inproj_kernel(*refs, rope):
    if rope:
        x_ref, mod_ref, g_ref, w_ref, cos_ref, sa_ref, sb_ref, q_ref, k_ref, v_ref, p_ref = refs
    else:
        x_ref, mod_ref, g_ref, w_ref, q_ref, k_ref, v_ref, p_ref, ks_ref, vs_ref = refs
    mod = mod_ref[0]
    for s in range(x_ref.shape[0] // SUB_TILE):
        rows = slice(s * SUB_TILE, (s + 1) * SUB_TILE)
        h = _norm_mod(x_ref[rows, :], g_ref[...], mod[0:1], mod[1:2])
        u = jnp.dot(h.astype(BF16), w_ref[...], preferred_element_type=F32)

        def rot(xh, rows=rows):
            return (xh * cos_ref[rows, :] + pltpu.roll(xh, LANES - ROPE_FREQS, 1) * sa_ref[rows, :]
                    + pltpu.roll(xh, ROPE_FREQS, 1) * sb_ref[rows, :])

        for hd in range(N_HEADS):
            xh = u[:, hd * HEAD_DIM:(hd + 1) * HEAD_DIM] * ATTN_SCALE
            q_ref[rows, hd * HEAD_DIM:(hd + 1) * HEAD_DIM] = (rot(xh) if rope else xh).astype(BF16)
        for hd in range(N_KV_HEADS):
            lo = ATTN_WIDTH + hd * HEAD_DIM
            xh = u[:, lo:lo + HEAD_DIM]
            k_ref[rows, hd * HEAD_DIM:(hd + 1) * HEAD_DIM] = rot(xh) if rope else xh
        v_ref[rows, :] = u[:, ATTN_WIDTH + KV_WIDTH:ATTN_WIDTH + 2 * KV_WIDTH]
        p_ref[rows, :] = u[:, ATTN_WIDTH + 2 * KV_WIDTH:]
        if not rope:
            for hd in range(N_KV_HEADS):
                state_rows = pl.ds(s * SUB_TILE * N_KV_HEADS + hd, SUB_TILE, stride=N_KV_HEADS)
                lo = ATTN_WIDTH + hd * HEAD_DIM
                ks_ref[state_rows, :] = u[:, lo:lo + HEAD_DIM]
                vs_ref[state_rows, :] = u[:, lo + KV_WIDTH:lo + KV_WIDTH + HEAD_DIM]


def _in_projection(x2, mod3, mod_row, g, w_in_bf, rope_tabs):
    t = x2.shape[0]
    tm = PROJ_TILE
    rope = rope_tabs is not None
    row = lambda i: (i, 0)
    in_specs = [pl.BlockSpec((tm, D_MODEL), row),
                pl.BlockSpec((1, N_MOD, D_MODEL), lambda i: (mod_row(i), 0, 0)),
                pl.BlockSpec((1, D_MODEL), lambda i: (0, 0)),
                pl.BlockSpec((D_MODEL, IN_WIDTH), lambda i: (0, 0))]
    args = [x2, mod3, g, w_in_bf]
    if rope:
        n_seq = rope_tabs[0].shape[0]
        seq_blocks = n_seq // tm
        for tab in rope_tabs:
            in_specs.append(pl.BlockSpec((tm, HEAD_DIM), lambda i: (i % seq_blocks, 0)))
            args.append(tab)
    out_specs = [pl.BlockSpec((tm, ATTN_WIDTH), row),
                 pl.BlockSpec((tm, KV_WIDTH), row),
                 pl.BlockSpec((tm, KV_WIDTH), row),
                 pl.BlockSpec((tm, POOL_WIDTH), row)]
    out_shape = [jax.ShapeDtypeStruct((t, ATTN_WIDTH), BF16),
                 jax.ShapeDtypeStruct((t, KV_WIDTH), F32),
                 jax.ShapeDtypeStruct((t, KV_WIDTH), F32),
                 jax.ShapeDtypeStruct((t, POOL_WIDTH), F32)]
    if not rope:
        for _ in range(2):
            out_specs.append(pl.BlockSpec((tm * N_KV_HEADS, HEAD_DIM), row))
            out_shape.append(jax.ShapeDtypeStruct((t * N_KV_HEADS, HEAD_DIM), F32))
    return pl.pallas_call(
        functools.partial(_inproj_kernel, rope=rope),
        grid=(t // tm,),
        in_specs=in_specs,
        out_specs=out_specs,
        out_shape=out_shape,
        compiler_params=pltpu.CompilerParams(
            vmem_limit_bytes=_vmem_limit(2 * D_MODEL * IN_WIDTH * 2 + 24 * tm * D_MODEL * 4)),
        name="in_projection",
    )(*args)


def _softmax_pv(s_list, v_list, sink_col):
    m = sink_col
    for s in s_list:
        m = jnp.maximum(m, jnp.max(s, axis=-1, keepdims=True))
    denom = jnp.exp2(sink_col - m)
    out = None
    for s, v in zip(s_list, v_list):
        e = jnp.exp2(s - m)
        if v.shape[1] == HEAD_DIM:
            denom = denom + jnp.sum(e, axis=-1, keepdims=True)
        o = jnp.dot(e.astype(BF16), v, preferred_element_type=F32)
        out = o if out is None else out + o
    if out.shape[1] > HEAD_DIM:
        denom = denom + out[:, HEAD_DIM:HEAD_DIM + 1]
    return out[:, 0:HEAD_DIM] * (1.0 / denom)


def _with_ones(v):
    return jnp.concatenate([v.astype(BF16), jnp.ones(v.shape, BF16)], axis=1)


def _stack_heads(q, kv):
    return jnp.concatenate(
        [q[:, (kv * Q_PER_KV + g) * HEAD_DIM:(kv * Q_PER_KV + g + 1) * HEAD_DIM]
         for g in range(Q_PER_KV)], axis=0)


def _sink_column(sink_ref, kv, rows):
    r = lax.broadcasted_iota(I32, (Q_PER_KV * rows, 1), 0)
    col = jnp.zeros((Q_PER_KV * rows, 1), F32)
    for g in range(Q_PER_KV):
        col = jnp.where((r >= g * rows) & (r < (g + 1) * rows), sink_ref[kv * Q_PER_KV + g], col)
    return col * LOG2_E


def _qk(q, k):
    return lax.dot_general(q, k, (((1,), (1,)), ((), ())), preferred_element_type=F32)


def _ctx_attn_kernel(sink_ref, q_ref, k_ref, v_ref, o_ref):
    rows = q_ref.shape[0]
    q = q_ref[...]
    for kv in range(N_KV_HEADS):
        kh = k_ref[:, kv * HEAD_DIM:(kv + 1) * HEAD_DIM].astype(BF16)
        vh = v_ref[:, kv * HEAD_DIM:(kv + 1) * HEAD_DIM].astype(BF16)
        qs = _stack_heads(q, kv)
        s = _qk(qs, kh)
        o = _softmax_pv([s], [vh], _sink_column(sink_ref, kv, rows))
        for g in range(Q_PER_KV):
            hd = kv * Q_PER_KV + g
            o_ref[:, hd * HEAD_DIM:(hd + 1) * HEAD_DIM] = o[g * rows:(g + 1) * rows].astype(BF16)


def _context_attention(q, k, v, sink, seq):
    t = q.shape[0]
    row = lambda b: (b, 0)
    return pl.pallas_call(
        _ctx_attn_kernel,
        grid=(t // seq,),
        in_specs=[pl.BlockSpec(memory_space=pltpu.SMEM),
                  pl.BlockSpec((seq, ATTN_WIDTH), row),
                  pl.BlockSpec((seq, KV_WIDTH), row),
                  pl.BlockSpec((seq, KV_WIDTH), row)],
        out_specs=pl.BlockSpec((seq, ATTN_WIDTH), row),
        out_shape=jax.ShapeDtypeStruct((t, ATTN_WIDTH), BF16),
        name="context_attention",
    )(sink, q, k, v)


def _lat_attn_kernel(sink_ref, q_ref, k_ref, v_ref, ck_ref, cv_ref, o_ref, *, n_seq):
    i = pl.program_id(1)
    band = 3 * BLOCK
    start = pl.multiple_of(jnp.clip((i - 1) * BLOCK, 0, n_seq - band), BLOCK)
    rows = Q_PER_KV * BLOCK
    qpos = i * BLOCK + lax.broadcasted_iota(I32, (rows, band), 0) % BLOCK
    kpos = start + lax.broadcasted_iota(I32, (rows, band), 1)
    mask = jnp.abs(kpos - qpos) <= WINDOW
    q = q_ref[...]
    for kv in range(N_KV_HEADS):
        cols = slice(kv * HEAD_DIM, (kv + 1) * HEAD_DIM)
        kb = k_ref[pl.ds(start, band), cols].astype(BF16)
        vb = _with_ones(v_ref[pl.ds(start, band), cols])
        cached = pl.ds(kv, ck_ref.shape[0] // N_KV_HEADS, stride=N_KV_HEADS)
        ck = ck_ref[cached, :].astype(BF16)
        cv = _with_ones(cv_ref[cached, :])
        qs = _stack_heads(q, kv)
        s_loc = jnp.where(mask, _qk(qs, kb), NEG)
        s_ctx = _qk(qs, ck)
        o = _softmax_pv([s_loc, s_ctx], [vb, cv], _sink_column(sink_ref, kv, BLOCK))
        for g in range(Q_PER_KV):
            hd = kv * Q_PER_KV + g
            o_ref[:, hd * HEAD_DIM:(hd + 1) * HEAD_DIM] = o[g * BLOCK:(g + 1) * BLOCK].astype(BF16)


def _latent_attention(q, k, v, ck, cv, sink, n_seq):
    t = q.shape[0]
    nb = n_seq // BLOCK
    cache_rows = ck.shape[0] // (t // n_seq)
    return pl.pallas_call(
        functools.partial(_lat_attn_kernel, n_seq=n_seq),
        grid=(t // n_seq, nb),
        in_specs=[pl.BlockSpec(memory_space=pltpu.SMEM),
                  pl.BlockSpec((BLOCK, ATTN_WIDTH), lambda b, i: (b * nb + i, 0)),
                  pl.BlockSpec((n_seq, KV_WIDTH), lambda b, i: (b, 0)),
                  pl.BlockSpec((n_seq, KV_WIDTH), lambda b, i: (b, 0)),
                  pl.BlockSpec((cache_rows, HEAD_DIM), lambda b, i: (b, 0)),
                  pl.BlockSpec((cache_rows, HEAD_DIM), lambda b, i: (b, 0))],
        out_specs=pl.BlockSpec((BLOCK, ATTN_WIDTH), lambda b, i: (b * nb + i, 0)),
        out_shape=jax.ShapeDtypeStruct((t, ATTN_WIDTH), BF16),
        name="latent_attention",
    )(sink, q, k, v, ck, cv)


def _pool_group(p_ref, r0, seq, w_ref, s_ref, g):
    n = SUB_TILE
    rows = n + 2 * POOL_HALO
    static = isinstance(r0, int)
    t0 = r0 % seq
    t = t0 + lax.broadcasted_iota(I32, (n, 1), 0)
    has_top = t0 > 0
    has_bottom = t0 + n < seq
    zeros = jnp.zeros((POOL_HALO, POOL_GROUP), F32)
    w = POOL_SIZES[g]
    cols = slice(g * POOL_GROUP, (g + 1) * POOL_GROUP)
    pg = p_ref[pl.ds(r0, n), cols]
    if static:
        top = p_ref[r0 - POOL_HALO:r0, cols] if has_top else zeros
        bottom = p_ref[r0 + n:r0 + n + POOL_HALO, cols] if has_bottom else zeros
    else:
        top_row = pl.multiple_of(jnp.maximum(r0 - POOL_HALO, 0), POOL_HALO)
        bottom_row = pl.multiple_of(jnp.minimum(r0 + n, p_ref.shape[0] - POOL_HALO), POOL_HALO)
        top = jnp.where(has_top, p_ref[pl.ds(top_row, POOL_HALO), cols], 0.0)
        bottom = jnp.where(has_bottom, p_ref[pl.ds(bottom_row, POOL_HALO), cols], 0.0)
    x = jnp.concatenate([top, pg, bottom], axis=0)
    fwd = x
    span = 1
    while span < w // 2:
        fwd = fwd + pltpu.roll(fwd, rows - span, 0)
        span *= 2
    if (w // 2) % SUBLANES == 0:
        wsum = fwd[POOL_HALO - w // 2:POOL_HALO - w // 2 + n] + fwd[POOL_HALO:POOL_HALO + n]
    else:
        wsum = (fwd + pltpu.roll(fwd, w // 2, 0))[POOL_HALO:POOL_HALO + n]
    lo = jnp.maximum(t - w // 2, 0)
    hi = jnp.minimum(t + w - w // 2, seq)
    inv_cnt = 1.0 / (hi - lo).astype(F32)
    mixed = wsum * inv_cnt - pg
    y = jnp.dot(mixed.astype(BF16), w_ref[g], preferred_element_type=F32)
    return (y * s_ref[:, cols]).astype(BF16)


def _pack_pair(lo, hi):
    return lax.bitcast_convert_type(pltpu.pack_elementwise([lo, hi], packed_dtype=BF16), I32)


def _unpack_pair(words):
    lo = pltpu.unpack_elementwise(words, index=0, packed_dtype=BF16, unpacked_dtype=F32)
    hi = pltpu.unpack_elementwise(words, index=1, packed_dtype=BF16, unpacked_dtype=F32)
    return lo, hi


def _outproj_kernel(a_ref, p_ref, x_ref, mod_ref, g_ref, wo_ref, wr_ref, pw_ref, ps_ref,
                    x1_ref, h_ref, aff_ref, *, seq):
    mod = mod_ref[0]
    groups = SUB_TILE // SUBLANES
    half = D_MODEL // 2
    tm = x_ref.shape[0]
    steps_per_p_block = p_ref.shape[0] // tm
    for s in range(tm // SUB_TILE):
        rows = slice(s * SUB_TILE, (s + 1) * SUB_TILE)
        grp = slice(s * groups, (s + 1) * groups)
        r0 = s * SUB_TILE
        if steps_per_p_block > 1:
            r0 = pl.multiple_of((pl.program_id(0) % steps_per_p_block) * tm + r0, SUB_TILE)
        pooled = jnp.concatenate([_pool_group(p_ref, r0, seq, pw_ref, ps_ref, g)
                                  for g in range(len(POOL_SIZES))], axis=1)
        mix = (jnp.dot(a_ref[rows, :], wo_ref[0:ATTN_WIDTH, :], preferred_element_type=F32)
               + jnp.dot(pooled, wo_ref[ATTN_WIDTH:D_MODEL, :], preferred_element_type=F32))
        x1 = x_ref[rows, :] + mod[2:3] * mix
        x1_ref[rows, :] = x1
        h = _norm_mod(x1, g_ref[...], mod[3:4], mod[4:5])
        logits = jnp.dot(h.astype(BF16), wr_ref[...], preferred_element_type=F32)
        lane = lax.broadcasted_iota(I32, logits.shape, 1)
        logits = jnp.where(lane < N_EXPERTS, logits, -jnp.inf)
        m = jnp.max(logits, axis=-1, keepdims=True)
        e = jnp.exp(logits - m)
        aff = e / jnp.sum(e, axis=-1, keepdims=True)
        aff_ref[rows, :] = aff[:, 0:N_EXPERTS]
        for c in range(HQ_TILES - 1):
            words = _pack_pair(h[:, c * LANES:(c + 1) * LANES],
                               h[:, half + c * LANES:half + (c + 1) * LANES])
            h_ref[grp, c * SUBLANES:(c + 1) * SUBLANES, :] = (
                words.reshape(groups, SUBLANES, LANES))
        h_ref[grp, (HQ_TILES - 1) * SUBLANES:, :] = (
            lax.bitcast_convert_type(aff, I32).reshape(groups, SUBLANES, LANES))


def _out_projection(attn, p, x2, mod3, mod_row, g, w_out_bf, w_router, pool_w, pool_scale, seq):
    t = x2.shape[0]
    tm = PROJ_TILE
    row = lambda i: (i, 0)
    p_rows = max(tm, seq)
    steps_per_p_block = p_rows // tm
    return pl.pallas_call(
        functools.partial(_outproj_kernel, seq=seq),
        grid=(t // tm,),
        in_specs=[pl.BlockSpec((tm, ATTN_WIDTH), row),
                  pl.BlockSpec((p_rows, POOL_WIDTH), lambda i: (i // steps_per_p_block, 0)),
                  pl.BlockSpec((tm, D_MODEL), row),
                  pl.BlockSpec((1, N_MOD, D_MODEL), lambda i: (mod_row(i), 0, 0)),
                  pl.BlockSpec((1, D_MODEL), lambda i: (0, 0)),
                  pl.BlockSpec((D_MODEL, D_MODEL), lambda i: (0, 0)),
                  pl.BlockSpec((D_MODEL, LANES), lambda i: (0, 0)),
                  pl.BlockSpec((len(POOL_SIZES), POOL_GROUP, POOL_GROUP), lambda i: (0, 0, 0)),
                  pl.BlockSpec((1, POOL_WIDTH), lambda i: (0, 0))],
        out_specs=[pl.BlockSpec((tm, D_MODEL), row),
                   pl.BlockSpec((tm // SUBLANES, HQ_TILES * SUBLANES, LANES), lambda i: (i, 0, 0)),
                   pl.BlockSpec((tm, N_EXPERTS), row)],
        out_shape=[jax.ShapeDtypeStruct((t, D_MODEL), F32),
                   jax.ShapeDtypeStruct((t // SUBLANES, HQ_TILES * SUBLANES, LANES), I32),
                   jax.ShapeDtypeStruct((t, N_EXPERTS), F32)],
        compiler_params=pltpu.CompilerParams(
            vmem_limit_bytes=_vmem_limit(2 * D_MODEL * D_MODEL * 2 + 24 * tm * D_MODEL * 4)),
        name="out_projection",
    )(attn, p, x2, mod3, g, w_out_bf, w_router, pool_w, pool_scale)


def _route_kernel(a_ref, idx_ref, slot_ref, off_ref, slot_scr, *, cap):
    a = a_ref[...]
    n_e, n_c, _ = a.shape
    rows = n_e * n_c

    def enough(cand):
        cand_f = lax.bitcast_convert_type(cand, F32)
        cnt = jnp.sum(jnp.sum((a >= cand_f).astype(F32), axis=1, keepdims=True),
                      axis=2, keepdims=True)
        return cnt >= cap

    def two_bits(it, thr):
        low = 28 - 2 * it
        for setting in (1, 2, 3):
            cand = thr | jnp.left_shift(jnp.int32(setting), low)
            best = jnp.where(enough(cand), cand, thr if setting == 1 else best)
        return best

    top = jnp.full((n_e, 1, 1), 1 << 30, I32)
    thr = jnp.where(enough(top), top, 0)
    thr = lax.fori_loop(0, 15, two_bits, thr)
    thr_f = lax.bitcast_convert_type(thr, F32)
    gt = (a > thr_f).astype(F32).reshape(rows, LANES)
    eq = (a == thr_f).astype(F32).reshape(rows, LANES)

    li = lax.broadcasted_iota(I32, (LANES, LANES), 0)
    lj = lax.broadcasted_iota(I32, (LANES, LANES), 1)
    upper_incl = (li <= lj).astype(BF16)
    ci = lax.broadcasted_iota(I32, (n_c, n_c), 0)
    cj = lax.broadcasted_iota(I32, (n_c, n_c), 1)
    before = (cj < ci).astype(BF16)
    whole = jnp.ones((n_c, n_c), BF16)

    def over_chunks(mat, col):
        wide = jnp.broadcast_to(col, (rows, LANES)).astype(BF16)
        side_by_side = jnp.concatenate([wide[e * n_c:(e + 1) * n_c] for e in range(n_e)], axis=1)
        res = jnp.dot(mat, side_by_side, preferred_element_type=F32)
        return jnp.concatenate([res[:, e * LANES:(e + 1) * LANES] for e in range(n_e)], axis=0)

    def prefix(x):
        incl = jnp.dot(x.astype(BF16), upper_incl, preferred_element_type=F32)
        tot = incl[:, LANES - 1:LANES]
        return incl, tot, over_chunks(before, tot)

    n_gt = over_chunks(whole, jnp.sum(gt, axis=1, keepdims=True))
    need = cap - n_gt
    incl_eq, _, off_eq = prefix(eq)
    rank_eq = off_eq + incl_eq - eq
    sel = jnp.where((eq > 0) & (rank_eq < need), 1.0, gt)
    incl, tot, off = prefix(sel)
    slot = off + incl - sel
    slot_scr[...] = jnp.where(sel > 0, slot, -1.0)
    for c in range(n_c):
        per_expert = slot_scr[pl.ds(c, n_e, stride=n_c), :]
        slot_ref[c * LANES:(c + 1) * LANES, :] = per_expert.T.astype(I32)
    off_ref[...] = off.astype(I32).reshape(n_e, n_c, LANES)

    s_lane = lax.broadcasted_iota(I32, (1, cap), 1).astype(F32)
    c_col = lax.broadcasted_iota(I32, (n_c, 1), 0).astype(F32)
    for e in range(n_e):
        r0 = e * n_c
        incl_e = incl[r0:r0 + n_c]
        off_e = off[r0:r0 + n_c, 0:1]
        tot_e = tot[r0:r0 + n_c]
        onehot = ((off_e <= s_lane) & (s_lane < off_e + tot_e)).astype(F32)
        counts = lax.dot_general(incl_e.astype(BF16), onehot.astype(BF16),
                                 (((0,), (0,)), ((), ())), preferred_element_type=F32)
        local = s_lane - jnp.sum(onehot * off_e, axis=0, keepdims=True)
        lane = jnp.sum((counts <= local).astype(F32), axis=0, keepdims=True)
        chunk = jnp.sum(onehot * c_col, axis=0, keepdims=True)
        idx_ref[e] = (chunk * LANES + lane).astype(I32)


def _routing(aff, cap):
    t = aff.shape[0]
    n_c = t // LANES
    a3 = aff.T.reshape(N_EXPERTS, n_c, LANES)
    return pl.pallas_call(
        functools.partial(_route_kernel, cap=cap),
        out_shape=[jax.ShapeDtypeStruct((N_EXPERTS, 1, cap), I32),
                   jax.ShapeDtypeStruct((t, N_EXPERTS), I32),
                   jax.ShapeDtypeStruct((N_EXPERTS, n_c, LANES), I32)],
        scratch_shapes=[pltpu.VMEM((N_EXPERTS * n_c, LANES), F32)],
        compiler_params=pltpu.CompilerParams(vmem_limit_bytes=_vmem_limit(48 << 20)),
        name="routing",
    )(a3)


def _gather_rows(table, row_ids):
    n_chunks = row_ids.shape[0]
    n_workers = SC_CORES * SC_SUBCORES
    per_worker = n_chunks // n_workers
    assert row_ids.shape[1] == GATHER_CHUNK and n_chunks % n_workers == 0
    mesh = plsc.VectorSubcoreMesh(core_axis_name="core", subcore_axis_name="subcore")

    @functools.partial(
        pl.kernel, mesh=mesh,
        out_type=jax.ShapeDtypeStruct((n_chunks * GATHER_CHUNK, LANES), I32),
        scratch_types=[pltpu.VMEM((per_worker, GATHER_CHUNK), I32),
                       pltpu.VMEM((2, GATHER_CHUNK, LANES), I32),
                       pltpu.SemaphoreType.DMA((2,)),
                       pltpu.SemaphoreType.DMA((2,))],
        name="gather_rows",
    )
    def gather(table_hbm, ids_hbm, out_hbm, ids_v, rows_v, gather_sem, store_sem):
        worker = lax.axis_index("subcore") * SC_CORES + lax.axis_index("core")
        first = worker * per_worker
        pltpu.sync_copy(ids_hbm.at[worker], ids_v)

        def fetch(j):
            return pltpu.make_async_copy(table_hbm.at[ids_v.at[j]], rows_v.at[j % 2],
                                         gather_sem.at[j % 2])

        def store(j):
            rows = pl.ds(pl.multiple_of((first + j) * GATHER_CHUNK, GATHER_CHUNK), GATHER_CHUNK)
            return pltpu.make_async_copy(rows_v.at[j % 2], out_hbm.at[rows], store_sem.at[j % 2])

        for j in range(per_worker):
            if j >= 2:
                store(j - 2).wait()
            fetch(j).start()
            if j >= 1:
                fetch(j - 1).wait()
                store(j - 1).start()
        fetch(per_worker - 1).wait()
        store(per_worker - 1).start()
        for j in range(max(per_worker - 2, 0), per_worker):
            store(j).wait()

    return gather(table, row_ids.reshape(n_workers, per_worker, GATHER_CHUNK))


def _packed_row_ids(idx, cap):
    tok = idx.reshape(-1, 1, cap // GATHER_CHUNK, GATHER_CHUNK)
    tile = jnp.arange(HQ_TILES, dtype=I32).reshape(1, HQ_TILES, 1, 1)
    ids = ((tok // SUBLANES) * HQ_TILES + tile) * SUBLANES + tok % SUBLANES
    return ids.reshape(-1, GATHER_CHUNK)


def _ffn_kernel(*refs, caps, n_f, row_chunk, first_expert, n_prior):
    n_g = len(caps)
    x_refs = refs[:n_g]
    wg_ref, wu_ref, wd_ref = refs[n_g:n_g + 3]
    y_refs = refs[n_g + 3 + n_prior:2 * n_g + 3 + n_prior]
    acc_ref = refs[2 * n_g + 3 + n_prior]
    e = pl.program_id(0)
    f = pl.program_id(1)

    @pl.when((e == 0) & (f == 0))
    def _():
        acc_ref[...] = jnp.zeros_like(acc_ref)

    def step(last):
        first = f == 0
        base = 0
        for x_ref, y_ref, cap in zip(x_refs, y_refs, caps):
            if last:
                aff = lax.bitcast_convert_type(x_ref[0, HQ_TILES - 1], F32)
                lane = lax.broadcasted_iota(I32, aff.shape, 1)
                gates = jnp.sum(jnp.where(lane == e + first_expert, aff, 0.0), axis=1,
                                keepdims=True)
                y_ref[0, cap:, :] = jnp.zeros((COMBINE_WINDOW, D_MODEL // 2), I32)
            for r in range(cap // row_chunk):
                rs = slice(r * row_chunk, (r + 1) * row_chunk)
                acc_rows = slice(base + r * row_chunk, base + (r + 1) * row_chunk)
                pairs = [_unpack_pair(x_ref[0, c, rs, :]) for c in range(HQ_TILES - 1)]
                x = jnp.concatenate([p[0] for p in pairs] + [p[1] for p in pairs], axis=1)
                gate_act = jnp.dot(x, wg_ref[0], preferred_element_type=F32)
                up = jnp.dot(x, wu_ref[0], preferred_element_type=F32)
                hid = (gate_act * jax.nn.sigmoid(gate_act)) * up
                part = jnp.dot(hid, wd_ref[0], preferred_element_type=F32)
                if last:
                    y = (part + acc_ref[acc_rows, :]) * gates[rs]
                    y_ref[0, rs, :] = _pack_pair(y[:, 0:D_MODEL // 2], y[:, D_MODEL // 2:])
                else:
                    acc_ref[acc_rows, :] = part + jnp.where(first, 0.0, acc_ref[acc_rows, :])
            base += cap

    @pl.when(f < n_f - 1)
    def _():
        step(False)

    @pl.when(f == n_f - 1)
    def _():
        step(True)


def _expert_ffn(xs_groups, w_gate, w_up, w_down, first_expert, prior_outputs):
    n_f = 4
    tf = D_EXPERT // n_f
    row_chunk = 512
    n_e = xs_groups[0].shape[0]
    e0 = first_expert
    caps = tuple(xs.shape[2] for xs in xs_groups)
    rows = sum(caps)
    est = (2 * HQ_TILES * rows * LANES * 4 + rows * D_MODEL * (4 + 2 * 2) + 2 * 3 * D_MODEL * tf * 4
           + row_chunk * (3 * tf + 2 * D_MODEL) * 4)
    x_specs = [pl.BlockSpec((1, HQ_TILES, cap, LANES), lambda e, f: (e, 0, 0, 0)) for cap in caps]
    y_shapes = [(N_EXPERTS, cap + COMBINE_WINDOW, D_MODEL // 2) for cap in caps]
    y_specs = [pl.BlockSpec((1,) + s[1:], lambda e, f: (e + e0, 0, 0)) for s in y_shapes]
    prior = list(prior_outputs or [])
    n_in = len(caps) + 3
    return pl.pallas_call(
        functools.partial(_ffn_kernel, caps=caps, n_f=n_f, row_chunk=row_chunk,
                          first_expert=e0, n_prior=len(prior)),
        grid=(n_e, n_f),
        in_specs=(x_specs + [pl.BlockSpec((1, D_MODEL, tf), lambda e, f: (e + e0, 0, f)),
                             pl.BlockSpec((1, D_MODEL, tf), lambda e, f: (e + e0, 0, f)),
                             pl.BlockSpec((1, tf, D_MODEL), lambda e, f: (e + e0, f, 0))]
                  + [pl.BlockSpec(memory_space=pl.ANY) for _ in prior]),
        out_specs=y_specs,
        out_shape=[jax.ShapeDtypeStruct(s, I32) for s in y_shapes],
        input_output_aliases={n_in + j: j for j in range(len(prior))},
        scratch_shapes=[pltpu.VMEM((rows, D_MODEL), F32)],
        compiler_params=pltpu.CompilerParams(
            dimension_semantics=("arbitrary", "arbitrary"),
            vmem_limit_bytes=_vmem_limit(est + (4 << 20))),
        name="expert_ffn",
    )(*xs_groups, w_gate, w_up, w_down, *prior)


def _combine_kernel(start_ref, nch_ref, wide_ref, x1_ref, mod_ref, g_ref, slot_ref, y_hbm, o_ref,
                    ybuf_ref, ffn_ref, sem, *, first_k, kblock):
    i = pl.program_id(0)
    tm = x1_ref.shape[0]
    cur = i % 2
    n_window_rows = N_EXPERTS * COMBINE_WINDOW

    window_head = COMBINE_WINDOW - COMBINE_CHUNK

    def window_copy(e, src_row, buf):
        return pltpu.make_async_copy(
            y_hbm.at[e, pl.ds(pl.multiple_of(src_row, SUBLANES), window_head)],
            ybuf_ref.at[buf, e * COMBINE_WINDOW:e * COMBINE_WINDOW + window_head], sem.at[buf])

    def window_tail_copy(e, src_row, buf):
        return chunk_copy(e, src_row + window_head, buf,
                          jnp.int32(e * COMBINE_WINDOW + window_head))

    def needs_tail(tile, e):
        return nch_ref[tile, e] * COMBINE_CHUNK > window_head

    def chunk_copy(e, src_row, buf, dst_row):
        return pltpu.make_async_copy(
            y_hbm.at[e, pl.ds(pl.multiple_of(src_row, SUBLANES), COMBINE_CHUNK)],
            ybuf_ref.at[buf, pl.ds(pl.multiple_of(dst_row, COMBINE_CHUNK), COMBINE_CHUNK)],
            sem.at[buf])

    def fetch(tile, buf):
        @pl.when(wide_ref[tile] == 0)
        def _():
            for e in range(N_EXPERTS):
                window_copy(e, start_ref[tile, e], buf).start()

                @pl.when(needs_tail(tile, e))
                def _(e=e):
                    window_tail_copy(e, start_ref[tile, e], buf).start()

        @pl.when(wide_ref[tile] != 0)
        def _():
            pos = jnp.int32(0)
            for e in range(N_EXPERTS):
                start = start_ref[tile, e]
                nch = nch_ref[tile, e]

                def issue(c, carry, e=e, start=start, pos=pos):
                    chunk_copy(e, start + c * COMBINE_CHUNK, buf, pos + c * COMBINE_CHUNK).start()
                    return carry
                lax.fori_loop(0, nch, issue, 0)
                pos = pos + nch * COMBINE_CHUNK

    @pl.when(i == 0)
    def _():
        ybuf_ref[...] = jnp.zeros_like(ybuf_ref)
        fetch(0, 0)

    @pl.when(i + 1 < pl.num_programs(0))
    def _():
        fetch(i + 1, 1 - cur)

    def finish(ffn):
        out = x1_ref[...] + mod_ref[0][5:6] * ffn
        ms = jnp.mean(out * out, axis=-1, keepdims=True)
        o_ref[...] = (out * lax.rsqrt(ms + EPS)) * g_ref[...]

    def apply_selection(sel, k0, width):
        lo, hi = _unpack_pair(ybuf_ref[cur, pl.ds(k0, width), :])
        return jnp.concatenate([jnp.dot(sel, lo, preferred_element_type=F32),
                                jnp.dot(sel, hi, preferred_element_type=F32)], axis=1)

    @pl.when(wide_ref[i] == 0)
    def _():
        for e in range(N_EXPERTS):
            window_copy(e, jnp.int32(0), cur).wait()

            @pl.when(needs_tail(i, e))
            def _(e=e):
                window_tail_copy(e, jnp.int32(0), cur).wait()
        expert_lane = lax.broadcasted_iota(I32, (1, N_EXPERTS), 1)
        starts = jnp.zeros((1, N_EXPERTS), I32)
        for e in range(N_EXPERTS):
            starts = jnp.where(expert_lane == e, start_ref[i, e], starts)
        slots = slot_ref[...]
        rows_in_window = jnp.where(slots >= 0, (slots - starts).astype(F32), -1.0)
        window_of_lane = lax.broadcasted_iota(I32, (N_EXPERTS, n_window_rows), 1) // COMBINE_WINDOW
        spread = (window_of_lane == lax.broadcasted_iota(I32, (N_EXPERTS, n_window_rows), 0))
        target = jnp.dot(rows_in_window.astype(BF16), spread.astype(BF16),
                         preferred_element_type=F32)
        lane_row = (lax.broadcasted_iota(I32, (tm, n_window_rows), 1) % COMBINE_WINDOW).astype(F32)
        sel = jnp.where(target == lane_row, 1.0, 0.0)
        ffn = None
        for k0 in range(0, n_window_rows, kblock):
            part = apply_selection(sel[:, k0:k0 + kblock], k0, kblock)
            ffn = part if ffn is None else ffn + part
        finish(ffn)

    @pl.when(wide_ref[i] != 0)
    def _():
        pos = jnp.int32(0)
        begins = []
        for e in range(N_EXPERTS):
            begins.append(pos)
            pos = pos + nch_ref[i, e] * COMBINE_CHUNK
        begins.append(pos)

        def drain(c, carry):
            chunk_copy(0, jnp.int32(0), cur, jnp.int32(0)).wait()
            return carry
        lax.fori_loop(0, pos // COMBINE_CHUNK, drain, 0)

        expert_lane = lax.broadcasted_iota(I32, (1, N_EXPERTS), 1)
        expert_row = lax.broadcasted_iota(I32, (N_EXPERTS, 1), 0)
        shift = jnp.zeros((1, N_EXPERTS), I32)
        range_lo = jnp.zeros((N_EXPERTS, 1), I32)
        range_hi = jnp.zeros((N_EXPERTS, 1), I32)
        for e in range(N_EXPERTS):
            shift = jnp.where(expert_lane == e, begins[e] - start_ref[i, e], shift)
            range_lo = jnp.where(expert_row == e, begins[e], range_lo)
            range_hi = jnp.where(expert_row == e, begins[e + 1], range_hi)
        slots = slot_ref[...]
        packed_row = slots + shift
        row_hi = jnp.where(slots >= 0, packed_row // ROW_SPLIT, -1).astype(F32).astype(BF16)
        row_lo = jnp.where(slots >= 0, packed_row % ROW_SPLIT, 0).astype(F32).astype(BF16)

        def selection(k0, width):
            col = k0 + lax.broadcasted_iota(I32, (N_EXPERTS, width), 1)
            spread = ((col >= range_lo) & (col < range_hi)).astype(BF16)
            want_hi = jnp.dot(row_hi, spread, preferred_element_type=F32)
            want_lo = jnp.dot(row_lo, spread, preferred_element_type=F32)
            here = k0 + lax.broadcasted_iota(I32, (1, width), 1)
            hit = ((want_hi == (here // ROW_SPLIT).astype(F32))
                   & (want_lo == (here % ROW_SPLIT).astype(F32)) & (here < pos))
            return jnp.where(hit, 1.0, 0.0)

        ffn_ref[...] = apply_selection(selection(0, first_k), 0, first_k)

        def kstep(kb, carry):
            k0 = pl.multiple_of(kb * kblock, kblock)
            ffn_ref[...] += apply_selection(selection(k0, kblock), k0, kblock)
            return carry
        lax.fori_loop(first_k // kblock, (pos + kblock - 1) // kblock, kstep, 0)
        finish(ffn_ref[...])


def _combine(x1, mod3, mod_row, g_final, slot_t, tile_start, tile_nch, tile_wide, y):
    t = x1.shape[0]
    tm = TOKEN_TILE
    first_k = 2 * tm + N_EXPERTS * COMBINE_CHUNK
    kblock = 256
    max_rows = N_EXPERTS * (tm + 2 * COMBINE_CHUNK)
    max_rows = -(-max_rows // kblock) * kblock
    row = lambda i, *_: (i, 0)
    grid_spec = pltpu.PrefetchScalarGridSpec(
        num_scalar_prefetch=3,
        grid=(t // tm,),
        in_specs=[pl.BlockSpec((tm, D_MODEL), row),
                  pl.BlockSpec((1, N_MOD, D_MODEL), lambda i, *_: (mod_row(i), 0, 0)),
                  pl.BlockSpec((1, D_MODEL), lambda i, *_: (0, 0)),
                  pl.BlockSpec((tm, N_EXPERTS), row),
                  pl.BlockSpec(memory_space=pl.ANY)],
        out_specs=pl.BlockSpec((tm, D_MODEL), row),
        scratch_shapes=[pltpu.VMEM((2, max_rows, D_MODEL // 2), I32),
                        pltpu.VMEM((tm, D_MODEL), F32),
                        pltpu.SemaphoreType.DMA((2,))],
    )
    return pl.pallas_call(
        functools.partial(_combine_kernel, first_k=first_k, kblock=kblock),
        grid_spec=grid_spec,
        out_shape=jax.ShapeDtypeStruct((t, D_MODEL), F32),
        compiler_params=pltpu.CompilerParams(
            dimension_semantics=("arbitrary",),
            vmem_limit_bytes=_vmem_limit(2 * max_rows * D_MODEL * 2 + 16 * tm * D_MODEL * 4)),
        name="combine",
    )(tile_start, tile_nch, tile_wide, x1, mod3, g_final, slot_t, y)


def _rope_tables(n):
    f32 = np.float32
    rows = n // GRID_W
    row = np.repeat(np.arange(rows, dtype=f32), GRID_W)
    col = np.tile(np.arange(GRID_W, dtype=f32), rows)
    inv = (f32(ROPE_THETA) ** (-np.arange(ROPE_FREQS, dtype=f32) / f32(ROPE_FREQS))).astype(f32)
    ang_r = row[:, None] * inv
    ang_c = col[:, None] * inv
    zero = np.zeros_like(ang_r)
    cos = np.concatenate([np.cos(ang_r)] * 2 + [np.cos(ang_c)] * 2, axis=1)
    sin_a = np.concatenate([-np.sin(ang_r), zero, -np.sin(ang_c), zero], axis=1)
    sin_b = np.concatenate([zero, np.sin(ang_r), zero, np.sin(ang_c)], axis=1)
    return tuple(jnp.asarray(t, F32) for t in (cos, sin_a, sin_b))


def _token_group(x, mod3, mod_row, seq, weights, rope_tabs, ctx_kv):
    (norm_mix, w_in_bf, sink, pool_w, pool_scale, w_out_bf, norm_ffn, w_router,
     w_gate, w_up, w_down, norm_final) = weights
    b = x.shape[0]
    t = b * seq
    x2 = x.reshape(t, D_MODEL)
    q, k, v, p, *state = _in_projection(x2, mod3, mod_row(PROJ_TILE), norm_mix, w_in_bf,
                                        rope_tabs)
    if ctx_kv is None:
        attn = _context_attention(q, k, v, sink, seq)
    else:
        attn = _latent_attention(q, k, v, ctx_kv[0], ctx_kv[1], sink, seq)
    x1, h_packed, aff = _out_projection(attn, p, x2, mod3, mod_row(PROJ_TILE), norm_ffn,
                                        w_out_bf, w_router, pool_w, pool_scale, seq)

    cap = EC_FACTOR * t // N_EXPERTS
    idx, slot_t, off3 = _routing(aff, cap)
    table = h_packed.reshape(-1, LANES)
    per_range = N_EXPERTS // FFN_RANGES
    xs = []
    for r in range(FFN_RANGES):
        ids = _packed_row_ids(idx[r * per_range:(r + 1) * per_range], cap)
        xs.append(_gather_rows(table, ids).reshape(per_range, HQ_TILES, cap, LANES))

    chunks_per_tile = TOKEN_TILE // LANES
    tile_off = off3[:, ::chunks_per_tile, 0]
    tile_end = jnp.concatenate([tile_off[:, 1:], jnp.full((N_EXPERTS, 1), cap, I32)], axis=1)
    tile_start = (tile_off // SUBLANES) * SUBLANES
    tile_nch = jnp.where(tile_end > tile_off,
                         (tile_end - tile_start + COMBINE_CHUNK - 1) // COMBINE_CHUNK, 0)
    tile_wide = jnp.any(tile_end - tile_start > COMBINE_WINDOW, axis=0).astype(I32)

    def finish(y):
        out = _combine(x1, mod3, mod_row(TOKEN_TILE), norm_final, slot_t, tile_start.T,
                       tile_nch.T, tile_wide, y)
        return out.reshape(b, seq, D_MODEL)
    return xs, finish, state


def kernel(x_prompt, x_sample, c, cache_k, cache_v, c_ctx, w_ada, b_ada, norm_mix, w_in,
           sink_logits, pool_w, pool_scale, w_out, norm_ffn, w_router, w_gate, w_up, w_down,
           norm_final):
    n_b, seq, _ = x_prompt.shape
    n_db, n_lat, _ = x_sample.shape
    assert 1 + n_db <= MOD_ROWS and seq == TOKEN_TILE and n_lat % PROJ_TILE == 0
    assert w_in.shape[0] == 1 and cache_k.shape[1] == 1
    assert (n_b * seq) % PROJ_TILE == 0

    cond = jnp.concatenate(
        [c_ctx[None, :], c, jnp.zeros((MOD_ROWS - 1 - n_db, D_MODEL), F32)], axis=0)
    mod3 = _modulation(cond, w_ada[0], b_ada[0]).reshape(MOD_ROWS, N_MOD, D_MODEL)

    w_router_bf = jnp.pad(w_router[0], ((0, 0), (0, LANES - N_EXPERTS))).astype(BF16)
    weights = (norm_mix[0][None, :], w_in[0].astype(BF16), sink_logits[0], pool_w[0].astype(BF16),
               pool_scale[0][None, :], w_out[0].astype(BF16), norm_ffn[0][None, :], w_router_bf,
               w_gate.reshape(w_gate.shape[1:]), w_up.reshape(w_up.shape[1:]),
               w_down.reshape(w_down.shape[1:]), norm_final[None, :])

    xs_p, finish_p, (k_p, v_p) = _token_group(
        x_prompt, mod3, lambda tile: (lambda i: 0), seq, weights, None, None)

    ck = cache_k.reshape(-1, HEAD_DIM)
    cv = cache_v.reshape(-1, HEAD_DIM)
    xs_l, finish_l, _ = _token_group(
        x_sample, mod3, lambda tile: (lambda i: 1 + i // (n_lat // tile)), n_lat, weights,
        _rope_tables(n_lat), (ck, cv))

    ys = None
    for r in range(FFN_RANGES):
        ys = _expert_ffn([xs_p[r], xs_l[r]], weights[8], weights[9], weights[10],
                         r * (N_EXPERTS // FFN_RANGES), ys)
    y_prompt = finish_p(ys[0])
    y_sample = finish_l(ys[1])

    state_k = k_p.reshape(n_b, 1, seq, N_KV_HEADS, HEAD_DIM)
    state_v = v_p.reshape(n_b, 1, seq, N_KV_HEADS, HEAD_DIM)
    return (y_prompt, y_sample, state_k, state_v)
```

```python
import functools

import jax
import jax.numpy as jnp
import numpy as np
from jax import lax
from jax.experimental import pallas as pl
from jax.experimental.pallas import tpu as pltpu
from jax.experimental.pallas import tpu_sc as plsc

F32 = jnp.float32
BF16 = jnp.bfloat16
I32 = jnp.int32

D_MODEL = 2048
N_HEADS = 8
N_KV_HEADS = 2
HEAD_DIM = 128
Q_PER_KV = N_HEADS // N_KV_HEADS
ATTN_WIDTH = N_HEADS * HEAD_DIM
KV_WIDTH = N_KV_HEADS * HEAD_DIM
POOL_WIDTH = D_MODEL - ATTN_WIDTH
POOL_SIZES = (2, 4, 8, 16)
POOL_GROUP = POOL_WIDTH // len(POOL_SIZES)
IN_WIDTH = ATTN_WIDTH + 2 * KV_WIDTH + POOL_WIDTH
WINDOW = 128
BLOCK = 128
GRID_W = 64
ROPE_THETA = 10000.0
ROPE_FREQS = HEAD_DIM // 4
N_EXPERTS = 16
EC_FACTOR = 2
D_EXPERT = 1024
N_MOD = 6
EPS = 1e-6
NEG = -1e30
LOG2_E = 1.4426950408889634
ATTN_SCALE = HEAD_DIM ** -0.5 * LOG2_E

LANES = 128
SUBLANES = 8
BF16_ROWS = 16
VMEM_CAP = 64 * 1024 * 1024
SC_CORES = 2
SC_SUBCORES = 16

MOD_ROWS = 8
TOKEN_TILE = 256
PROJ_TILE = 512
SUB_TILE = 256
HQ_TILES = D_MODEL // 2 // LANES + 1
GATHER_CHUNK = 128
FFN_RANGES = 2
POOL_HALO = 8
COMBINE_CHUNK = BF16_ROWS
COMBINE_WINDOW = 64
ROW_SPLIT = 32


def _vmem_limit(nbytes):
    return int(min(VMEM_CAP - (4 << 20), max(nbytes, 16 << 20)))


def _mod_kernel(c_ref, w_ref, b_ref, o_ref):
    c = c_ref[...]
    s = c * jax.nn.sigmoid(c)
    o_ref[...] = jnp.dot(s.astype(BF16), w_ref[...].astype(BF16),
                         preferred_element_type=F32) + b_ref[...]


def _modulation(cond, w_ada, b_ada):
    n = w_ada.shape[1]
    tn = 1024
    return pl.pallas_call(
        _mod_kernel,
        grid=(n // tn,),
        in_specs=[pl.BlockSpec((MOD_ROWS, D_MODEL), lambda j: (0, 0)),
                  pl.BlockSpec((D_MODEL, tn), lambda j: (0, j)),
                  pl.BlockSpec((1, tn), lambda j: (0, j))],
        out_specs=pl.BlockSpec((MOD_ROWS, tn), lambda j: (0, j)),
        out_shape=jax.ShapeDtypeStruct((MOD_ROWS, n), F32),
        compiler_params=pltpu.CompilerParams(
            vmem_limit_bytes=_vmem_limit(3 * D_MODEL * tn * 4)),
        name="modulation",
    )(cond, w_ada, b_ada.reshape(1, n))


def _norm_mod(x, g, shift, scale):
    ms = jnp.mean(x * x, axis=-1, keepdims=True)
    y = x * lax.rsqrt(ms + EPS)
    return (y * g) * (1.0 + scale) + shift


def _inproj_kernel(*refs, rope):
    if rope:
        x_ref, mod_ref, g_ref, w_ref, cos_ref, sa_ref, sb_ref, q_ref, k_ref, v_ref, p_ref = refs
    else:
        (x_ref, mod_ref, g_ref, w_ref, slab_ref,
         q_ref, k_ref, v_ref, p_ref, ks_ref, vs_ref, slab_bf_ref) = refs
        slab_bf_ref[...] = slab_ref[...].astype(BF16)
    mod = mod_ref[0]
    for s in range(x_ref.shape[0] // SUB_TILE):
        rows = slice(s * SUB_TILE, (s + 1) * SUB_TILE)
        h = _norm_mod(x_ref[rows, :], g_ref[...], mod[0:1], mod[1:2])
        u = jnp.dot(h.astype(BF16), w_ref[...], preferred_element_type=F32)

        def rot(xh, rows=rows):
            return (xh * cos_ref[rows, :] + pltpu.roll(xh, LANES - ROPE_FREQS, 1) * sa_ref[rows, :]
                    + pltpu.roll(xh, ROPE_FREQS, 1) * sb_ref[rows, :])

        for hd in range(N_HEADS):
            xh = u[:, hd * HEAD_DIM:(hd + 1) * HEAD_DIM] * ATTN_SCALE
            q_ref[rows, hd * HEAD_DIM:(hd + 1) * HEAD_DIM] = (rot(xh) if rope else xh).astype(BF16)
        for hd in range(N_KV_HEADS):
            lo = ATTN_WIDTH + hd * HEAD_DIM
            xh = u[:, lo:lo + HEAD_DIM]
            k_ref[rows, hd * HEAD_DIM:(hd + 1) * HEAD_DIM] = rot(xh) if rope else xh
        v_ref[rows, :] = u[:, ATTN_WIDTH + KV_WIDTH:ATTN_WIDTH + 2 * KV_WIDTH]
        p_ref[rows, :] = u[:, ATTN_WIDTH + 2 * KV_WIDTH:]
        if not rope:
            for hd in range(N_KV_HEADS):
                state_rows = pl.ds(s * SUB_TILE * N_KV_HEADS + hd, SUB_TILE, stride=N_KV_HEADS)
                lo = ATTN_WIDTH + hd * HEAD_DIM
                ks_ref[state_rows, :] = u[:, lo:lo + HEAD_DIM]
                vs_ref[state_rows, :] = u[:, lo + KV_WIDTH:lo + KV_WIDTH + HEAD_DIM]


def _in_projection(x2, mod3, mod_row, g, w_in_bf, rope_tabs, w_out):
    t = x2.shape[0]
    tm = PROJ_TILE
    rope = rope_tabs is not None
    row = lambda i: (i, 0)
    in_specs = [pl.BlockSpec((tm, D_MODEL), row),
                pl.BlockSpec((1, N_MOD, D_MODEL), lambda i: (mod_row(i), 0, 0)),
                pl.BlockSpec((1, D_MODEL), lambda i: (0, 0)),
                pl.BlockSpec((D_MODEL, IN_WIDTH), lambda i: (0, 0))]
    args = [x2, mod3, g, w_in_bf]
    if rope:
        n_seq = rope_tabs[0].shape[0]
        seq_blocks = n_seq // tm
        for tab in rope_tabs:
            in_specs.append(pl.BlockSpec((tm, HEAD_DIM), lambda i: (i % seq_blocks, 0)))
            args.append(tab)
    else:
        slab = w_out.shape[0] // (t // tm)
        assert slab * (t // tm) == w_out.shape[0] and slab % BF16_ROWS == 0
        in_specs.append(pl.BlockSpec((slab, w_out.shape[1]), row))
        args.append(w_out)
    out_specs = [pl.BlockSpec((tm, ATTN_WIDTH), row),
                 pl.BlockSpec((tm, KV_WIDTH), row),
                 pl.BlockSpec((tm, KV_WIDTH), row),
                 pl.BlockSpec((tm, POOL_WIDTH), row)]
    out_shape = [jax.ShapeDtypeStruct((t, ATTN_WIDTH), BF16),
                 jax.ShapeDtypeStruct((t, KV_WIDTH), F32),
                 jax.ShapeDtypeStruct((t, KV_WIDTH), F32),
                 jax.ShapeDtypeStruct((t, POOL_WIDTH), F32)]
    if not rope:
        for _ in range(2):
            out_specs.append(pl.BlockSpec((tm * N_KV_HEADS, HEAD_DIM), row))
            out_shape.append(jax.ShapeDtypeStruct((t * N_KV_HEADS, HEAD_DIM), F32))
        out_specs.append(pl.BlockSpec((slab, w_out.shape[1]), row))
        out_shape.append(jax.ShapeDtypeStruct(w_out.shape, BF16))
    return pl.pallas_call(
        functools.partial(_inproj_kernel, rope=rope),
        grid=(t // tm,),
        in_specs=in_specs,
        out_specs=out_specs,
        out_shape=out_shape,
        compiler_params=pltpu.CompilerParams(
            vmem_limit_bytes=_vmem_limit(2 * D_MODEL * IN_WIDTH * 2 + 24 * tm * D_MODEL * 4)),
        name="in_projection",
    )(*args)


def _softmax_pv(s_list, v_list, sink_col):
    m = sink_col
    for s in s_list:
        m = jnp.maximum(m, jnp.max(s, axis=-1, keepdims=True))
    denom = jnp.exp2(sink_col - m)
    out = None
    for s, v in zip(s_list, v_list):
        e = jnp.exp2(s - m)
        if v.shape[1] == HEAD_DIM:
            denom = denom + jnp.sum(e, axis=-1, keepdims=True)
        o = jnp.dot(e.astype(BF16), v, preferred_element_type=F32)
        out = o if out is None else out + o
    if out.shape[1] > HEAD_DIM:
        denom = denom + out[:, HEAD_DIM:HEAD_DIM + 1]
    return out[:, 0:HEAD_DIM] * (1.0 / denom)


def _with_ones(v):
    return jnp.concatenate([v.astype(BF16), jnp.ones(v.shape, BF16)], axis=1)


def _stack_heads(q, kv):
    return jnp.concatenate(
        [q[:, (kv * Q_PER_KV + g) * HEAD_DIM:(kv * Q_PER_KV + g + 1) * HEAD_DIM]
         for g in range(Q_PER_KV)], axis=0)


def _sink_column(sink_ref, kv, rows):
    r = lax.broadcasted_iota(I32, (Q_PER_KV * rows, 1), 0)
    col = jnp.zeros((Q_PER_KV * rows, 1), F32)
    for g in range(Q_PER_KV):
        col = jnp.where((r >= g * rows) & (r < (g + 1) * rows), sink_ref[kv * Q_PER_KV + g], col)
    return col * LOG2_E


def _qk(q, k):
    return lax.dot_general(q, k, (((1,), (1,)), ((), ())), preferred_element_type=F32)


def _ctx_attn_kernel(sink_ref, q_ref, k_ref, v_ref, o_ref):
    rows = q_ref.shape[0]
    q = q_ref[...]
    for kv in range(N_KV_HEADS):
        kh = k_ref[:, kv * HEAD_DIM:(kv + 1) * HEAD_DIM].astype(BF16)
        vh = v_ref[:, kv * HEAD_DIM:(kv + 1) * HEAD_DIM].astype(BF16)
        qs = _stack_heads(q, kv)
        s = _qk(qs, kh)
        o = _softmax_pv([s], [vh], _sink_column(sink_ref, kv, rows))
        for g in range(Q_PER_KV):
            hd = kv * Q_PER_KV + g
            o_ref[:, hd * HEAD_DIM:(hd + 1) * HEAD_DIM] = o[g * rows:(g + 1) * rows].astype(BF16)


def _context_attention(q, k, v, sink, seq):
    t = q.shape[0]
    row = lambda b: (b, 0)
    return pl.pallas_call(
        _ctx_attn_kernel,
        grid=(t // seq,),
        in_specs=[pl.BlockSpec(memory_space=pltpu.SMEM),
                  pl.BlockSpec((seq, ATTN_WIDTH), row),
                  pl.BlockSpec((seq, KV_WIDTH), row),
                  pl.BlockSpec((seq, KV_WIDTH), row)],
        out_specs=pl.BlockSpec((seq, ATTN_WIDTH), row),
        out_shape=jax.ShapeDtypeStruct((t, ATTN_WIDTH), BF16),
        name="context_attention",
    )(sink, q, k, v)


def _lat_attn_kernel(sink_ref, q_ref, k_ref, v_ref, ck_ref, cv_ref, o_ref, *, n_seq):
    i = pl.program_id(1)
    band = 3 * BLOCK
    start = pl.multiple_of(jnp.clip((i - 1) * BLOCK, 0, n_seq - band), BLOCK)
    rows = Q_PER_KV * BLOCK
    qpos = i * BLOCK + lax.broadcasted_iota(I32, (rows, band), 0) % BLOCK
    kpos = start + lax.broadcasted_iota(I32, (rows, band), 1)
    mask = jnp.abs(kpos - qpos) <= WINDOW
    q = q_ref[...]
    for kv in range(N_KV_HEADS):
        cols = slice(kv * HEAD_DIM, (kv + 1) * HEAD_DIM)
        kb = k_ref[pl.ds(start, band), cols].astype(BF16)
        vb = _with_ones(v_ref[pl.ds(start, band), cols])
        cached = pl.ds(kv, ck_ref.shape[0] // N_KV_HEADS, stride=N_KV_HEADS)
        ck = ck_ref[cached, :].astype(BF16)
        cv = _with_ones(cv_ref[cached, :])
        qs = _stack_heads(q, kv)
        s_loc = jnp.where(mask, _qk(qs, kb), NEG)
        s_ctx = _qk(qs, ck)
        o = _softmax_pv([s_loc, s_ctx], [vb, cv], _sink_column(sink_ref, kv, BLOCK))
        for g in range(Q_PER_KV):
            hd = kv * Q_PER_KV + g
            o_ref[:, hd * HEAD_DIM:(hd + 1) * HEAD_DIM] = o[g * BLOCK:(g + 1) * BLOCK].astype(BF16)


def _latent_attention(q, k, v, ck, cv, sink, n_seq):
    t = q.shape[0]
    nb = n_seq // BLOCK
    cache_rows = ck.shape[0] // (t // n_seq)
    return pl.pallas_call(
        functools.partial(_lat_attn_kernel, n_seq=n_seq),
        grid=(t // n_seq, nb),
        in_specs=[pl.BlockSpec(memory_space=pltpu.SMEM),
                  pl.BlockSpec((BLOCK, ATTN_WIDTH), lambda b, i: (b * nb + i, 0)),
                  pl.BlockSpec((n_seq, KV_WIDTH), lambda b, i: (b, 0)),
                  pl.BlockSpec((n_seq, KV_WIDTH), lambda b, i: (b, 0)),
                  pl.BlockSpec((cache_rows, HEAD_DIM), lambda b, i: (b, 0)),
                  pl.BlockSpec((cache_rows, HEAD_DIM), lambda b, i: (b, 0))],
        out_specs=pl.BlockSpec((BLOCK, ATTN_WIDTH), lambda b, i: (b * nb + i, 0)),
        out_shape=jax.ShapeDtypeStruct((t, ATTN_WIDTH), BF16),
        name="latent_attention",
    )(sink, q, k, v, ck, cv)


def _pool_group(p_ref, r0, seq, w_ref, s_ref, g):
    n = SUB_TILE
    rows = n + 2 * POOL_HALO
    static = isinstance(r0, int)
    t0 = r0 % seq
    t = t0 + lax.broadcasted_iota(I32, (n, 1), 0)
    has_top = t0 > 0
    has_bottom = t0 + n < seq
    zeros = jnp.zeros((POOL_HALO, POOL_GROUP), F32)
    w = POOL_SIZES[g]
    cols = slice(g * POOL_GROUP, (g + 1) * POOL_GROUP)
    pg = p_ref[pl.ds(r0, n), cols]
    if static:
        top = p_ref[r0 - POOL_HALO:r0, cols] if has_top else zeros
        bottom = p_ref[r0 + n:r0 + n + POOL_HALO, cols] if has_bottom else zeros
    else:
        top_row = pl.multiple_of(jnp.maximum(r0 - POOL_HALO, 0), POOL_HALO)
        bottom_row = pl.multiple_of(jnp.minimum(r0 + n, p_ref.shape[0] - POOL_HALO), POOL_HALO)
        top = jnp.where(has_top, p_ref[pl.ds(top_row, POOL_HALO), cols], 0.0)
        bottom = jnp.where(has_bottom, p_ref[pl.ds(bottom_row, POOL_HALO), cols], 0.0)
    x = jnp.concatenate([top, pg, bottom], axis=0)
    fwd = x
    span = 1
    while span < w // 2:
        fwd = fwd + pltpu.roll(fwd, rows - span, 0)
        span *= 2
    if (w // 2) % SUBLANES == 0:
        wsum = fwd[POOL_HALO - w // 2:POOL_HALO - w // 2 + n] + fwd[POOL_HALO:POOL_HALO + n]
    else:
        wsum = (fwd + pltpu.roll(fwd, w // 2, 0))[POOL_HALO:POOL_HALO + n]
    lo = jnp.maximum(t - w // 2, 0)
    hi = jnp.minimum(t + w - w // 2, seq)
    inv_cnt = 1.0 / (hi - lo).astype(F32)
    mixed = wsum * inv_cnt - pg
    y = jnp.dot(mixed.astype(BF16), w_ref[g], preferred_element_type=F32)
    return (y * s_ref[:, cols]).astype(BF16)


def _pack_pair(lo, hi):
    return lax.bitcast_convert_type(pltpu.pack_elementwise([lo, hi], packed_dtype=BF16), I32)


def _unpack_pair(words):
    lo = pltpu.unpack_elementwise(words, index=0, packed_dtype=BF16, unpacked_dtype=F32)
    hi = pltpu.unpack_elementwise(words, index=1, packed_dtype=BF16, unpacked_dtype=F32)
    return lo, hi


def _outproj_kernel(a_ref, p_ref, x_ref, mod_ref, g_ref, wo_ref, wr_ref, pw_ref, ps_ref,
                    x1_ref, h_ref, aff_ref, *, seq):
    mod = mod_ref[0]
    groups = SUB_TILE // SUBLANES
    half = D_MODEL // 2
    tm = x_ref.shape[0]
    steps_per_p_block = p_ref.shape[0] // tm
    for s in range(tm // SUB_TILE):
        rows = slice(s * SUB_TILE, (s + 1) * SUB_TILE)
        grp = slice(s * groups, (s + 1) * groups)
        r0 = s * SUB_TILE
        if steps_per_p_block > 1:
            r0 = pl.multiple_of((pl.program_id(0) % steps_per_p_block) * tm + r0, SUB_TILE)
        pooled = jnp.concatenate([_pool_group(p_ref, r0, seq, pw_ref, ps_ref, g)
                                  for g in range(len(POOL_SIZES))], axis=1)
        mix = (jnp.dot(a_ref[rows, :], wo_ref[0:ATTN_WIDTH, :], preferred_element_type=F32)
               + jnp.dot(pooled, wo_ref[ATTN_WIDTH:D_MODEL, :], preferred_element_type=F32))
        x1 = x_ref[rows, :] + mod[2:3] * mix
        x1_ref[rows, :] = x1
        h = _norm_mod(x1, g_ref[...], mod[3:4], mod[4:5])
        logits = jnp.dot(h.astype(BF16), wr_ref[...], preferred_element_type=F32)
        lane = lax.broadcasted_iota(I32, logits.shape, 1)
        logits = jnp.where(lane < N_EXPERTS, logits, -jnp.inf)
        m = jnp.max(logits, axis=-1, keepdims=True)
        e = jnp.exp(logits - m)
        aff = e / jnp.sum(e, axis=-1, keepdims=True)
        aff_ref[rows, :] = aff[:, 0:N_EXPERTS]
        for c in range(HQ_TILES - 1):
            words = _pack_pair(h[:, c * LANES:(c + 1) * LANES],
                               h[:, half + c * LANES:half + (c + 1) * LANES])
            h_ref[grp, c * SUBLANES:(c + 1) * SUBLANES, :] = (
                words.reshape(groups, SUBLANES, LANES))
        h_ref[grp, (HQ_TILES - 1) * SUBLANES:, :] = (
            lax.bitcast_convert_type(aff, I32).reshape(groups, SUBLANES, LANES))


def _out_projection(attn, p, x2, mod3, mod_row, g, w_out_bf, w_router, pool_w, pool_scale, seq):
    t = x2.shape[0]
    tm = PROJ_TILE
    row = lambda i: (i, 0)
    p_rows = max(tm, seq)
    steps_per_p_block = p_rows // tm
    return pl.pallas_call(
        functools.partial(_outproj_kernel, seq=seq),
        grid=(t // tm,),
        in_specs=[pl.BlockSpec((tm, ATTN_WIDTH), row),
                  pl.BlockSpec((p_rows, POOL_WIDTH), lambda i: (i // steps_per_p_block, 0)),
                  pl.BlockSpec((tm, D_MODEL), row),
                  pl.BlockSpec((1, N_MOD, D_MODEL), lambda i: (mod_row(i), 0, 0)),
                  pl.BlockSpec((1, D_MODEL), lambda i: (0, 0)),
                  pl.BlockSpec((D_MODEL, D_MODEL), lambda i: (0, 0)),
                  pl.BlockSpec((D_MODEL, LANES), lambda i: (0, 0)),
                  pl.BlockSpec((len(POOL_SIZES), POOL_GROUP, POOL_GROUP), lambda i: (0, 0, 0)),
                  pl.BlockSpec((1, POOL_WIDTH), lambda i: (0, 0))],
        out_specs=[pl.BlockSpec((tm, D_MODEL), row),
                   pl.BlockSpec((tm // SUBLANES, HQ_TILES * SUBLANES, LANES), lambda i: (i, 0, 0)),
                   pl.BlockSpec((tm, N_EXPERTS), row)],
        out_shape=[jax.ShapeDtypeStruct((t, D_MODEL), F32),
                   jax.ShapeDtypeStruct((t // SUBLANES, HQ_TILES * SUBLANES, LANES), I32),
                   jax.ShapeDtypeStruct((t, N_EXPERTS), F32)],
        compiler_params=pltpu.CompilerParams(
            vmem_limit_bytes=_vmem_limit(2 * D_MODEL * D_MODEL * 2 + 24 * tm * D_MODEL * 4)),
        name="out_projection",
    )(attn, p, x2, mod3, g, w_out_bf, w_router, pool_w, pool_scale)


def _route_kernel(a_ref, idx_ref, slot_ref, off_ref, slot_scr, *, cap):
    a = a_ref[...]
    n_e, n_c, _ = a.shape
    rows = n_e * n_c

    def enough(cand):
        cand_f = lax.bitcast_convert_type(cand, F32)
        cnt = jnp.sum(jnp.sum((a >= cand_f).astype(F32), axis=1, keepdims=True),
                      axis=2, keepdims=True)
        return cnt >= cap

    def two_bits(it, thr):
        low = 28 - 2 * it
        for setting in (1, 2, 3):
            cand = thr | jnp.left_shift(jnp.int32(setting), low)
            best = jnp.where(enough(cand), cand, thr if setting == 1 else best)
        return best

    top = jnp.full((n_e, 1, 1), 1 << 30, I32)
    thr = jnp.where(enough(top), top, 0)
    thr = lax.fori_loop(0, 15, two_bits, thr)
    thr_f = lax.bitcast_convert_type(thr, F32)
    gt = (a > thr_f).astype(F32).reshape(rows, LANES)
    eq = (a == thr_f).astype(F32).reshape(rows, LANES)

    li = lax.broadcasted_iota(I32, (LANES, LANES), 0)
    lj = lax.broadcasted_iota(I32, (LANES, LANES), 1)
    upper_incl = (li <= lj).astype(BF16)
    ci = lax.broadcasted_iota(I32, (n_c, n_c), 0)
    cj = lax.broadcasted_iota(I32, (n_c, n_c), 1)
    before = (cj < ci).astype(BF16)
    whole = jnp.ones((n_c, n_c), BF16)

    def over_chunks(mat, col):
        wide = jnp.broadcast_to(col, (rows, LANES)).astype(BF16)
        side_by_side = jnp.concatenate([wide[e * n_c:(e + 1) * n_c] for e in range(n_e)], axis=1)
        res = jnp.dot(mat, side_by_side, preferred_element_type=F32)
        return jnp.concatenate([res[:, e * LANES:(e + 1) * LANES] for e in range(n_e)], axis=0)

    def prefix(x):
        incl = jnp.dot(x.astype(BF16), upper_incl, preferred_element_type=F32)
        tot = incl[:, LANES - 1:LANES]
        return incl, tot, over_chunks(before, tot)

    n_gt = over_chunks(whole, jnp.sum(gt, axis=1, keepdims=True))
    need = cap - n_gt
    incl_eq, _, off_eq = prefix(eq)
    rank_eq = off_eq + incl_eq - eq
    sel = jnp.where((eq > 0) & (rank_eq < need), 1.0, gt)
    incl, tot, off = prefix(sel)
    slot = off + incl - sel
    slot_scr[...] = jnp.where(sel > 0, slot, -1.0)
    for c in range(n_c):
        per_expert = slot_scr[pl.ds(c, n_e, stride=n_c), :]
        slot_ref[c * LANES:(c + 1) * LANES, :] = per_expert.T.astype(I32)
    off_ref[...] = off.astype(I32).reshape(n_e, n_c, LANES)

    s_lane = lax.broadcasted_iota(I32, (1, cap), 1).astype(F32)
    c_col = lax.broadcasted_iota(I32, (n_c, 1), 0).astype(F32)
    for e in range(n_e):
        r0 = e * n_c
        incl_e = incl[r0:r0 + n_c]
        off_e = off[r0:r0 + n_c, 0:1]
        tot_e = tot[r0:r0 + n_c]
        onehot = ((off_e <= s_lane) & (s_lane < off_e + tot_e)).astype(F32)
        counts = lax.dot_general(incl_e.astype(BF16), onehot.astype(BF16),
                                 (((0,), (0,)), ((), ())), preferred_element_type=F32)
        local = s_lane - jnp.sum(onehot * off_e, axis=0, keepdims=True)
        lane = jnp.sum((counts <= local).astype(F32), axis=0, keepdims=True)
        chunk = jnp.sum(onehot * c_col, axis=0, keepdims=True)
        idx_ref[e] = (chunk * LANES + lane).astype(I32)


def _routing(aff, cap):
    t = aff.shape[0]
    n_c = t // LANES
    a3 = aff.T.reshape(N_EXPERTS, n_c, LANES)
    return pl.pallas_call(
        functools.partial(_route_kernel, cap=cap),
        out_shape=[jax.ShapeDtypeStruct((N_EXPERTS, 1, cap), I32),
                   jax.ShapeDtypeStruct((t, N_EXPERTS), I32),
                   jax.ShapeDtypeStruct((N_EXPERTS, n_c, LANES), I32)],
        scratch_shapes=[pltpu.VMEM((N_EXPERTS * n_c, LANES), F32)],
        compiler_params=pltpu.CompilerParams(vmem_limit_bytes=_vmem_limit(48 << 20)),
        name="routing",
    )(a3)


def _gather_rows(table, row_ids):
    n_chunks = row_ids.shape[0]
    n_workers = SC_CORES * SC_SUBCORES
    per_worker = n_chunks // n_workers
    assert row_ids.shape[1] == GATHER_CHUNK and n_chunks % n_workers == 0
    mesh = plsc.VectorSubcoreMesh(core_axis_name="core", subcore_axis_name="subcore")

    @functools.partial(
        pl.kernel, mesh=mesh,
        out_type=jax.ShapeDtypeStruct((n_chunks * GATHER_CHUNK, LANES), I32),
        scratch_types=[pltpu.VMEM((per_worker, GATHER_CHUNK), I32),
                       pltpu.VMEM((2, GATHER_CHUNK, LANES), I32),
                       pltpu.SemaphoreType.DMA((2,)),
                       pltpu.SemaphoreType.DMA((2,))],
        name="gather_rows",
    )
    def gather(table_hbm, ids_hbm, out_hbm, ids_v, rows_v, gather_sem, store_sem):
        worker = lax.axis_index("subcore") * SC_CORES + lax.axis_index("core")
        first = worker * per_worker
        pltpu.sync_copy(ids_hbm.at[worker], ids_v)

        def fetch(j):
            return pltpu.make_async_copy(table_hbm.at[ids_v.at[j]], rows_v.at[j % 2],
                                         gather_sem.at[j % 2])

        def store(j):
            rows = pl.ds(pl.multiple_of((first + j) * GATHER_CHUNK, GATHER_CHUNK), GATHER_CHUNK)
            return pltpu.make_async_copy(rows_v.at[j % 2], out_hbm.at[rows], store_sem.at[j % 2])

        for j in range(per_worker):
            if j >= 2:
                store(j - 2).wait()
            fetch(j).start()
            if j >= 1:
                fetch(j - 1).wait()
                store(j - 1).start()
        fetch(per_worker - 1).wait()
        store(per_worker - 1).start()
        for j in range(max(per_worker - 2, 0), per_worker):
            store(j).wait()

    return gather(table, row_ids.reshape(n_workers, per_worker, GATHER_CHUNK))


def _packed_row_ids(idx, cap):
    tok = idx.reshape(-1, 1, cap // GATHER_CHUNK, GATHER_CHUNK)
    tile = jnp.arange(HQ_TILES, dtype=I32).reshape(1, HQ_TILES, 1, 1)
    ids = ((tok // SUBLANES) * HQ_TILES + tile) * SUBLANES + tok % SUBLANES
    return ids.reshape(-1, GATHER_CHUNK)


def _ffn_kernel(*refs, caps, n_f, row_chunk, first_expert, n_prior):
    n_g = len(caps)
    x_refs = refs[:n_g]
    wg_ref, wu_ref, wd_ref = refs[n_g:n_g + 3]
    y_refs = refs[n_g + 3 + n_prior:2 * n_g + 3 + n_prior]
    acc_ref = refs[2 * n_g + 3 + n_prior]
    e = pl.program_id(0)
    f = pl.program_id(1)

    @pl.when((e == 0) & (f == 0))
    def _():
        acc_ref[...] = jnp.zeros_like(acc_ref)

    def step(last):
        first = f == 0
        base = 0
        for x_ref, y_ref, cap in zip(x_refs, y_refs, caps):
            if last:
                aff = lax.bitcast_convert_type(x_ref[0, HQ_TILES - 1], F32)
                lane = lax.broadcasted_iota(I32, aff.shape, 1)
                gates = jnp.sum(jnp.where(lane == e + first_expert, aff, 0.0), axis=1,
                                keepdims=True)
                y_ref[0, cap:, :] = jnp.zeros((COMBINE_WINDOW, D_MODEL // 2), I32)
            for r in range(cap // row_chunk):
                rs = slice(r * row_chunk, (r + 1) * row_chunk)
                acc_rows = slice(base + r * row_chunk, base + (r + 1) * row_chunk)
                pairs = [_unpack_pair(x_ref[0, c, rs, :]) for c in range(HQ_TILES - 1)]
                x = jnp.concatenate([p[0] for p in pairs] + [p[1] for p in pairs], axis=1)
                gate_act = jnp.dot(x, wg_ref[0], preferred_element_type=F32)
                up = jnp.dot(x, wu_ref[0], preferred_element_type=F32)
                hid = (gate_act * jax.nn.sigmoid(gate_act)) * up
                part = jnp.dot(hid, wd_ref[0], preferred_element_type=F32)
                if last:
                    y = (part + acc_ref[acc_rows, :]) * gates[rs]
                    y_ref[0, rs, :] = _pack_pair(y[:, 0:D_MODEL // 2], y[:, D_MODEL // 2:])
                else:
                    acc_ref[acc_rows, :] = part + jnp.where(first, 0.0, acc_ref[acc_rows, :])
            base += cap

    @pl.when(f < n_f - 1)
    def _():
        step(False)

    @pl.when(f == n_f - 1)
    def _():
        step(True)


def _expert_ffn(xs_groups, w_gate, w_up, w_down, first_expert, prior_outputs):
    n_f = 4
    tf = D_EXPERT // n_f
    row_chunk = 512
    n_e = xs_groups[0].shape[0]
    e0 = first_expert
    caps = tuple(xs.shape[2] for xs in xs_groups)
    rows = sum(caps)
    est = (2 * HQ_TILES * rows * LANES * 4 + rows * D_MODEL * (4 + 2 * 2) + 2 * 3 * D_MODEL * tf * 4
           + row_chunk * (3 * tf + 2 * D_MODEL) * 4)
    x_specs = [pl.BlockSpec((1, HQ_TILES, cap, LANES), lambda e, f: (e, 0, 0, 0)) for cap in caps]
    y_shapes = [(N_EXPERTS, cap + COMBINE_WINDOW, D_MODEL // 2) for cap in caps]
    y_specs = [pl.BlockSpec((1,) + s[1:], lambda e, f: (e + e0, 0, 0)) for s in y_shapes]
    prior = list(prior_outputs or [])
    n_in = len(caps) + 3
    return pl.pallas_call(
        functools.partial(_ffn_kernel, caps=caps, n_f=n_f, row_chunk=row_chunk,
                          first_expert=e0, n_prior=len(prior)),
        grid=(n_e, n_f),
        in_specs=(x_specs + [pl.BlockSpec((1, D_MODEL, tf), lambda e, f: (e + e0, 0, f)),
                             pl.BlockSpec((1, D_MODEL, tf), lambda e, f: (e + e0, 0, f)),
                             pl.BlockSpec((1, tf, D_MODEL), lambda e, f: (e + e0, f, 0))]
                  + [pl.BlockSpec(memory_space=pl.ANY) for _ in prior]),
        out_specs=y_specs,
        out_shape=[jax.ShapeDtypeStruct(s, I32) for s in y_shapes],
        input_output_aliases={n_in + j: j for j in range(len(prior))},
        scratch_shapes=[pltpu.VMEM((rows, D_MODEL), F32)],
        compiler_params=pltpu.CompilerParams(
            dimension_semantics=("arbitrary", "arbitrary"),
            vmem_limit_bytes=_vmem_limit(est + (4 << 20))),
        name="expert_ffn",
    )(*xs_groups, w_gate, w_up, w_down, *prior)


def _combine_kernel(start_ref, nch_ref, wide_ref, x1_ref, mod_ref, g_ref, slot_ref, y_hbm, o_ref,
                    ybuf_ref, ffn_ref, sem, *, first_k, kblock):
    i = pl.program_id(0)
    tm = x1_ref.shape[0]
    cur = i % 2
    n_window_rows = N_EXPERTS * COMBINE_WINDOW

    window_head = COMBINE_WINDOW - COMBINE_CHUNK

    def window_copy(e, src_row, buf):
        return pltpu.make_async_copy(
            y_hbm.at[e, pl.ds(pl.multiple_of(src_row, SUBLANES), window_head)],
            ybuf_ref.at[buf, e * COMBINE_WINDOW:e * COMBINE_WINDOW + window_head], sem.at[buf])

    def window_tail_copy(e, src_row, buf):
        return chunk_copy(e, src_row + window_head, buf,
                          jnp.int32(e * COMBINE_WINDOW + window_head))

    def needs_tail(tile, e):
        return nch_ref[tile, e] * COMBINE_CHUNK > window_head

    def chunk_copy(e, src_row, buf, dst_row):
        return pltpu.make_async_copy(
            y_hbm.at[e, pl.ds(pl.multiple_of(src_row, SUBLANES), COMBINE_CHUNK)],
            ybuf_ref.at[buf, pl.ds(pl.multiple_of(dst_row, COMBINE_CHUNK), COMBINE_CHUNK)],
            sem.at[buf])

    def fetch(tile, buf):
        @pl.when(wide_ref[tile] == 0)
        def _():
            for e in range(N_EXPERTS):
                window_copy(e, start_ref[tile, e], buf).start()

                @pl.when(needs_tail(tile, e))
                def _(e=e):
                    window_tail_copy(e, start_ref[tile, e], buf).start()

        @pl.when(wide_ref[tile] != 0)
        def _():
            pos = jnp.int32(0)
            for e in range(N_EXPERTS):
                start = start_ref[tile, e]
                nch = nch_ref[tile, e]

                def issue(c, carry, e=e, start=start, pos=pos):
                    chunk_copy(e, start + c * COMBINE_CHUNK, buf, pos + c * COMBINE_CHUNK).start()
                    return carry
                lax.fori_loop(0, nch, issue, 0)
                pos = pos + nch * COMBINE_CHUNK

    @pl.when(i == 0)
    def _():
        ybuf_ref[...] = jnp.zeros_like(ybuf_ref)
        fetch(0, 0)

    @pl.when(i + 1 < pl.num_programs(0))
    def _():
        fetch(i + 1, 1 - cur)

    def finish(ffn):
        out = x1_ref[...] + mod_ref[0][5:6] * ffn
        ms = jnp.mean(out * out, axis=-1, keepdims=True)
        o_ref[...] = (out * lax.rsqrt(ms + EPS)) * g_ref[...]

    def apply_selection(sel, k0, width):
        lo, hi = _unpack_pair(ybuf_ref[cur, pl.ds(k0, width), :])
        return jnp.concatenate([jnp.dot(sel, lo, preferred_element_type=F32),
                                jnp.dot(sel, hi, preferred_element_type=F32)], axis=1)

    @pl.when(wide_ref[i] == 0)
    def _():
        for e in range(N_EXPERTS):
            window_copy(e, jnp.int32(0), cur).wait()

            @pl.when(needs_tail(i, e))
            def _(e=e):
                window_tail_copy(e, jnp.int32(0), cur).wait()
        expert_lane = lax.broadcasted_iota(I32, (1, N_EXPERTS), 1)
        starts = jnp.zeros((1, N_EXPERTS), I32)
        for e in range(N_EXPERTS):
            starts = jnp.where(expert_lane == e, start_ref[i, e], starts)
        slots = slot_ref[...]
        rows_in_window = jnp.where(slots >= 0, (slots - starts).astype(F32), -1.0)
        window_of_lane = lax.broadcasted_iota(I32, (N_EXPERTS, n_window_rows), 1) // COMBINE_WINDOW
        spread = (window_of_lane == lax.broadcasted_iota(I32, (N_EXPERTS, n_window_rows), 0))
        target = jnp.dot(rows_in_window.astype(BF16), spread.astype(BF16),
                         preferred_element_type=F32)
        lane_row = (lax.broadcasted_iota(I32, (tm, n_window_rows), 1) % COMBINE_WINDOW).astype(F32)
        sel = jnp.where(target == lane_row, 1.0, 0.0)
        ffn = None
        for k0 in range(0, n_window_rows, kblock):
            part = apply_selection(sel[:, k0:k0 + kblock], k0, kblock)
            ffn = part if ffn is None else ffn + part
        finish(ffn)

    @pl.when(wide_ref[i] != 0)
    def _():
        pos = jnp.int32(0)
        begins = []
        for e in range(N_EXPERTS):
            begins.append(pos)
            pos = pos + nch_ref[i, e] * COMBINE_CHUNK
        begins.append(pos)

        def drain(c, carry):
            chunk_copy(0, jnp.int32(0), cur, jnp.int32(0)).wait()
            return carry
        lax.fori_loop(0, pos // COMBINE_CHUNK, drain, 0)

        expert_lane = lax.broadcasted_iota(I32, (1, N_EXPERTS), 1)
        expert_row = lax.broadcasted_iota(I32, (N_EXPERTS, 1), 0)
        shift = jnp.zeros((1, N_EXPERTS), I32)
        range_lo = jnp.zeros((N_EXPERTS, 1), I32)
        range_hi = jnp.zeros((N_EXPERTS, 1), I32)
        for e in range(N_EXPERTS):
            shift = jnp.where(expert_lane == e, begins[e] - start_ref[i, e], shift)
            range_lo = jnp.where(expert_row == e, begins[e], range_lo)
            range_hi = jnp.where(expert_row == e, begins[e + 1], range_hi)
        slots = slot_ref[...]
        packed_row = slots + shift
        row_hi = jnp.where(slots >= 0, packed_row // ROW_SPLIT, -1).astype(F32).astype(BF16)
        row_lo = jnp.where(slots >= 0, packed_row % ROW_SPLIT, 0).astype(F32).astype(BF16)

        def selection(k0, width):
            col = k0 + lax.broadcasted_iota(I32, (N_EXPERTS, width), 1)
            spread = ((col >= range_lo) & (col < range_hi)).astype(BF16)
            want_hi = jnp.dot(row_hi, spread, preferred_element_type=F32)
            want_lo = jnp.dot(row_lo, spread, preferred_element_type=F32)
            here = k0 + lax.broadcasted_iota(I32, (1, width), 1)
            hit = ((want_hi == (here // ROW_SPLIT).astype(F32))
                   & (want_lo == (here % ROW_SPLIT).astype(F32)) & (here < pos))
            return jnp.where(hit, 1.0, 0.0)

        ffn_ref[...] = apply_selection(selection(0, first_k), 0, first_k)

        def kstep(kb, carry):
            k0 = pl.multiple_of(kb * kblock, kblock)
            ffn_ref[...] += apply_selection(selection(k0, kblock), k0, kblock)
            return carry
        lax.fori_loop(first_k // kblock, (pos + kblock - 1) // kblock, kstep, 0)
        finish(ffn_ref[...])


def _combine(x1, mod3, mod_row, g_final, slot_t, tile_start, tile_nch, tile_wide, y):
    t = x1.shape[0]
    tm = TOKEN_TILE
    first_k = 2 * tm + N_EXPERTS * COMBINE_CHUNK
    kblock = 256
    max_rows = N_EXPERTS * (tm + 2 * COMBINE_CHUNK)
    max_rows = -(-max_rows // kblock) * kblock
    row = lambda i, *_: (i, 0)
    grid_spec = pltpu.PrefetchScalarGridSpec(
        num_scalar_prefetch=3,
        grid=(t // tm,),
        in_specs=[pl.BlockSpec((tm, D_MODEL), row),
                  pl.BlockSpec((1, N_MOD, D_MODEL), lambda i, *_: (mod_row(i), 0, 0)),
                  pl.BlockSpec((1, D_MODEL), lambda i, *_: (0, 0)),
                  pl.BlockSpec((tm, N_EXPERTS), row),
                  pl.BlockSpec(memory_space=pl.ANY)],
        out_specs=pl.BlockSpec((tm, D_MODEL), row),
        scratch_shapes=[pltpu.VMEM((2, max_rows, D_MODEL // 2), I32),
                        pltpu.VMEM((tm, D_MODEL), F32),
                        pltpu.SemaphoreType.DMA((2,))],
    )
    return pl.pallas_call(
        functools.partial(_combine_kernel, first_k=first_k, kblock=kblock),
        grid_spec=grid_spec,
        out_shape=jax.ShapeDtypeStruct((t, D_MODEL), F32),
        compiler_params=pltpu.CompilerParams(
            dimension_semantics=("arbitrary",),
            vmem_limit_bytes=_vmem_limit(2 * max_rows * D_MODEL * 2 + 16 * tm * D_MODEL * 4)),
        name="combine",
    )(tile_start, tile_nch, tile_wide, x1, mod3, g_final, slot_t, y)


def _rope_tables(n):
    f32 = np.float32
    rows = n // GRID_W
    row = np.repeat(np.arange(rows, dtype=f32), GRID_W)
    col = np.tile(np.arange(GRID_W, dtype=f32), rows)
    inv = (f32(ROPE_THETA) ** (-np.arange(ROPE_FREQS, dtype=f32) / f32(ROPE_FREQS))).astype(f32)
    ang_r = row[:, None] * inv
    ang_c = col[:, None] * inv
    zero = np.zeros_like(ang_r)
    cos = np.concatenate([np.cos(ang_r)] * 2 + [np.cos(ang_c)] * 2, axis=1)
    sin_a = np.concatenate([-np.sin(ang_r), zero, -np.sin(ang_c), zero], axis=1)
    sin_b = np.concatenate([zero, np.sin(ang_r), zero, np.sin(ang_c)], axis=1)
    return tuple(jnp.asarray(t, F32) for t in (cos, sin_a, sin_b))


def _token_group(x, mod3, mod_row, seq, weights, rope_tabs, ctx_kv):
    (norm_mix, w_in_bf, sink, pool_w, pool_scale, w_out, norm_ffn, w_router,
     w_gate, w_up, w_down, norm_final) = weights
    b = x.shape[0]
    t = b * seq
    x2 = x.reshape(t, D_MODEL)
    q, k, v, p, *state = _in_projection(x2, mod3, mod_row(PROJ_TILE), norm_mix, w_in_bf,
                                        rope_tabs, w_out)
    w_out_bf = w_out if rope_tabs is not None else state.pop()
    if ctx_kv is None:
        attn = _context_attention(q, k, v, sink, seq)
    else:
        attn = _latent_attention(q, k, v, ctx_kv[0], ctx_kv[1], sink, seq)
    x1, h_packed, aff = _out_projection(attn, p, x2, mod3, mod_row(PROJ_TILE), norm_ffn,
                                        w_out_bf, w_router, pool_w, pool_scale, seq)

    cap = EC_FACTOR * t // N_EXPERTS
    idx, slot_t, off3 = _routing(aff, cap)
    table = h_packed.reshape(-1, LANES)
    per_range = N_EXPERTS // FFN_RANGES
    xs = []
    for r in range(FFN_RANGES):
        ids = _packed_row_ids(idx[r * per_range:(r + 1) * per_range], cap)
        xs.append(_gather_rows(table, ids).reshape(per_range, HQ_TILES, cap, LANES))

    chunks_per_tile = TOKEN_TILE // LANES
    tile_off = off3[:, ::chunks_per_tile, 0]
    tile_end = jnp.concatenate([tile_off[:, 1:], jnp.full((N_EXPERTS, 1), cap, I32)], axis=1)
    tile_start = (tile_off // SUBLANES) * SUBLANES
    tile_nch = jnp.where(tile_end > tile_off,
                         (tile_end - tile_start + COMBINE_CHUNK - 1) // COMBINE_CHUNK, 0)
    tile_wide = jnp.any(tile_end - tile_start > COMBINE_WINDOW, axis=0).astype(I32)

    def finish(y):
        out = _combine(x1, mod3, mod_row(TOKEN_TILE), norm_final, slot_t, tile_start.T,
                       tile_nch.T, tile_wide, y)
        return out.reshape(b, seq, D_MODEL)
    return xs, finish, state, w_out_bf


def kernel(x_prompt, x_sample, c, cache_k, cache_v, c_ctx, w_ada, b_ada, norm_mix, w_in,
           sink_logits, pool_w, pool_scale, w_out, norm_ffn, w_router, w_gate, w_up, w_down,
           norm_final):
    n_b, seq, _ = x_prompt.shape
    n_db, n_lat, _ = x_sample.shape
    assert 1 + n_db <= MOD_ROWS and seq == TOKEN_TILE and n_lat % PROJ_TILE == 0
    assert w_in.shape[0] == 1 and cache_k.shape[1] == 1
    assert (n_b * seq) % PROJ_TILE == 0

    cond = jnp.concatenate(
        [c_ctx[None, :], c, jnp.zeros((MOD_ROWS - 1 - n_db, D_MODEL), F32)], axis=0)
    mod3 = _modulation(cond, w_ada[0], b_ada[0]).reshape(MOD_ROWS, N_MOD, D_MODEL)

    w_router_bf = jnp.pad(w_router[0], ((0, 0), (0, LANES - N_EXPERTS))).astype(BF16)
    weights = (norm_mix[0][None, :], w_in[0].astype(BF16), sink_logits[0], pool_w[0].astype(BF16),
               pool_scale[0][None, :], w_out[0], norm_ffn[0][None, :], w_router_bf,
               w_gate.reshape(w_gate.shape[1:]), w_up.reshape(w_up.shape[1:]),
               w_down.reshape(w_down.shape[1:]), norm_final[None, :])

    xs_p, finish_p, (k_p, v_p), w_out_bf = _token_group(
        x_prompt, mod3, lambda tile: (lambda i: 0), seq, weights, None, None)
    weights = weights[:5] + (w_out_bf,) + weights[6:]

    ck = cache_k.reshape(-1, HEAD_DIM)
    cv = cache_v.reshape(-1, HEAD_DIM)
    xs_l, finish_l, _, _ = _token_group(
        x_sample, mod3, lambda tile: (lambda i: 1 + i // (n_lat // tile)), n_lat, weights,
        _rope_tables(n_lat), (ck, cv))

    ys = None
    for r in range(FFN_RANGES):
        ys = _expert_ffn([xs_p[r], xs_l[r]], weights[8], weights[9], weights[10],
                         r * (N_EXPERTS // FFN_RANGES), ys)
    y_prompt = finish_p(ys[0])
    y_sample = finish_l(ys[1])

    state_k = k_p.reshape(n_b, 1, seq, N_KV_HEADS, HEAD_DIM)
    state_v = v_p.reshape(n_b, 1, seq, N_KV_HEADS, HEAD_DIM)
    return (y_prompt, y_sample, state_k, state_v)
```

```python
import functools

import jax
import jax.numpy as jnp
import numpy as np
from jax import lax
from jax.experimental import pallas as pl
from jax.experimental.pallas import tpu as pltpu
from jax.experimental.pallas import tpu_sc as plsc

F32 = jnp.float32
BF16 = jnp.bfloat16
I32 = jnp.int32

D_MODEL = 2048
N_HEADS = 8
N_KV_HEADS = 2
HEAD_DIM = 128
Q_PER_KV = N_HEADS // N_KV_HEADS
ATTN_WIDTH = N_HEADS * HEAD_DIM
KV_WIDTH = N_KV_HEADS * HEAD_DIM
POOL_WIDTH = D_MODEL - ATTN_WIDTH
POOL_SIZES = (2, 4, 8, 16)
POOL_GROUP = POOL_WIDTH // len(POOL_SIZES)
IN_WIDTH = ATTN_WIDTH + 2 * KV_WIDTH + POOL_WIDTH
WINDOW = 128
BLOCK = 128
GRID_W = 64
ROPE_THETA = 10000.0
ROPE_FREQS = HEAD_DIM // 4
N_EXPERTS = 16
EC_FACTOR = 2
D_EXPERT = 1024
N_MOD = 6
EPS = 1e-6
NEG = -1e30
LOG2_E = 1.4426950408889634
ATTN_SCALE = HEAD_DIM ** -0.5 * LOG2_E

LANES = 128
SUBLANES = 8
BF16_ROWS = 16
VMEM_CAP = 64 * 1024 * 1024
SC_CORES = 2
SC_SUBCORES = 16

MOD_ROWS = 8
TOKEN_TILE = 256
PROJ_TILE = 512
SUB_TILE = 256
HQ_TILES = D_MODEL // 2 // LANES + 1
GATHER_CHUNK = 128
FFN_RANGES = 2
POOL_HALO = 8
COMBINE_CHUNK = BF16_ROWS
COMBINE_WINDOW = 64
ROW_SPLIT = 32


def _vmem_limit(nbytes):
    return int(min(VMEM_CAP - (4 << 20), max(nbytes, 16 << 20)))


def _mod_kernel(c_ref, w_ref, b_ref, o_ref):
    c = c_ref[...]
    s = c * jax.nn.sigmoid(c)
    o_ref[...] = jnp.dot(s.astype(BF16), w_ref[...].astype(BF16),
                         preferred_element_type=F32) + b_ref[...]


def _modulation(cond, w_ada, b_ada):
    n = w_ada.shape[1]
    tn = 1024
    return pl.pallas_call(
        _mod_kernel,
        grid=(n // tn,),
        in_specs=[pl.BlockSpec((MOD_ROWS, D_MODEL), lambda j: (0, 0)),
                  pl.BlockSpec((D_MODEL, tn), lambda j: (0, j)),
                  pl.BlockSpec((1, tn), lambda j: (0, j))],
        out_specs=pl.BlockSpec((MOD_ROWS, tn), lambda j: (0, j)),
        out_shape=jax.ShapeDtypeStruct((MOD_ROWS, n), F32),
        compiler_params=pltpu.CompilerParams(
            vmem_limit_bytes=_vmem_limit(3 * D_MODEL * tn * 4)),
        name="modulation",
    )(cond, w_ada, b_ada.reshape(1, n))


def _norm_mod(x, g, shift, scale):
    ms = jnp.mean(x * x, axis=-1, keepdims=True)
    y = x * lax.rsqrt(ms + EPS)
    return (y * g) * (1.0 + scale) + shift


def _inproj_kernel(*refs, rope):
    if rope:
        x_ref, mod_ref, g_ref, w_ref, cos_ref, sa_ref, sb_ref, q_ref, k_ref, v_ref, p_ref = refs
    else:
        (x_ref, mod_ref, g_ref, w_ref, wo_slab_ref, wi_slab_ref,
         q_ref, k_ref, v_ref, p_ref, ks_ref, vs_ref, wo_bf_ref, wi_bf_ref) = refs
        wo_bf_ref[...] = wo_slab_ref[...].astype(BF16)
        wi_bf_ref[...] = wi_slab_ref[...].astype(BF16)
    mod = mod_ref[0]
    for s in range(x_ref.shape[0] // SUB_TILE):
        rows = slice(s * SUB_TILE, (s + 1) * SUB_TILE)
        h = _norm_mod(x_ref[rows, :], g_ref[...], mod[0:1], mod[1:2])
        u = jnp.dot(h.astype(w_ref.dtype), w_ref[...], preferred_element_type=F32)

        def rot(xh, rows=rows):
            return (xh * cos_ref[rows, :] + pltpu.roll(xh, LANES - ROPE_FREQS, 1) * sa_ref[rows, :]
                    + pltpu.roll(xh, ROPE_FREQS, 1) * sb_ref[rows, :])

        for hd in range(N_HEADS):
            xh = u[:, hd * HEAD_DIM:(hd + 1) * HEAD_DIM] * ATTN_SCALE
            q_ref[rows, hd * HEAD_DIM:(hd + 1) * HEAD_DIM] = (rot(xh) if rope else xh).astype(BF16)
        for hd in range(N_KV_HEADS):
            lo = ATTN_WIDTH + hd * HEAD_DIM
            xh = u[:, lo:lo + HEAD_DIM]
            k_ref[rows, hd * HEAD_DIM:(hd + 1) * HEAD_DIM] = rot(xh) if rope else xh
        v_ref[rows, :] = u[:, ATTN_WIDTH + KV_WIDTH:ATTN_WIDTH + 2 * KV_WIDTH]
        p_ref[rows, :] = u[:, ATTN_WIDTH + 2 * KV_WIDTH:]
        if not rope:
            for hd in range(N_KV_HEADS):
                state_rows = pl.ds(s * SUB_TILE * N_KV_HEADS + hd, SUB_TILE, stride=N_KV_HEADS)
                lo = ATTN_WIDTH + hd * HEAD_DIM
                ks_ref[state_rows, :] = u[:, lo:lo + HEAD_DIM]
                vs_ref[state_rows, :] = u[:, lo + KV_WIDTH:lo + KV_WIDTH + HEAD_DIM]


def _in_projection(x2, mod3, mod_row, g, w_in, rope_tabs, w_out):
    t = x2.shape[0]
    tm = PROJ_TILE
    rope = rope_tabs is not None
    row = lambda i: (i, 0)
    w_mode = None if rope else pl.Buffered(1)
    in_specs = [pl.BlockSpec((tm, D_MODEL), row),
                pl.BlockSpec((1, N_MOD, D_MODEL), lambda i: (mod_row(i), 0, 0)),
                pl.BlockSpec((1, D_MODEL), lambda i: (0, 0)),
                pl.BlockSpec((D_MODEL, IN_WIDTH), lambda i: (0, 0), pipeline_mode=w_mode)]
    args = [x2, mod3, g, w_in]
    if rope:
        n_seq = rope_tabs[0].shape[0]
        seq_blocks = n_seq // tm
        for tab in rope_tabs:
            in_specs.append(pl.BlockSpec((tm, HEAD_DIM), lambda i: (i % seq_blocks, 0)))
            args.append(tab)
    else:
        slab = D_MODEL // (t // tm)
        assert slab * (t // tm) == D_MODEL and slab % BF16_ROWS == 0
        for w in (w_out, w_in):
            in_specs.append(pl.BlockSpec((slab, w.shape[1]), row))
            args.append(w)
    out_specs = [pl.BlockSpec((tm, ATTN_WIDTH), row),
                 pl.BlockSpec((tm, KV_WIDTH), row),
                 pl.BlockSpec((tm, KV_WIDTH), row),
                 pl.BlockSpec((tm, POOL_WIDTH), row)]
    out_shape = [jax.ShapeDtypeStruct((t, ATTN_WIDTH), BF16),
                 jax.ShapeDtypeStruct((t, KV_WIDTH), F32),
                 jax.ShapeDtypeStruct((t, KV_WIDTH), F32),
                 jax.ShapeDtypeStruct((t, POOL_WIDTH), F32)]
    if not rope:
        for _ in range(2):
            out_specs.append(pl.BlockSpec((tm * N_KV_HEADS, HEAD_DIM), row))
            out_shape.append(jax.ShapeDtypeStruct((t * N_KV_HEADS, HEAD_DIM), F32))
        for w in (w_out, w_in):
            out_specs.append(pl.BlockSpec((slab, w.shape[1]), row))
            out_shape.append(jax.ShapeDtypeStruct(w.shape, BF16))
    return pl.pallas_call(
        functools.partial(_inproj_kernel, rope=rope),
        grid=(t // tm,),
        in_specs=in_specs,
        out_specs=out_specs,
        out_shape=out_shape,
        compiler_params=pltpu.CompilerParams(
            vmem_limit_bytes=_vmem_limit(2 * D_MODEL * IN_WIDTH * 2 + 24 * tm * D_MODEL * 4)),
        name="in_projection",
    )(*args)


def _softmax_pv(s_list, v_list, sink_col):
    m = sink_col
    for s in s_list:
        m = jnp.maximum(m, jnp.max(s, axis=-1, keepdims=True))
    denom = jnp.exp2(sink_col - m)
    out = None
    for s, v in zip(s_list, v_list):
        e = jnp.exp2(s - m)
        if v.shape[1] == HEAD_DIM:
            denom = denom + jnp.sum(e, axis=-1, keepdims=True)
        o = jnp.dot(e.astype(BF16), v, preferred_element_type=F32)
        out = o if out is None else out + o
    if out.shape[1] > HEAD_DIM:
        denom = denom + out[:, HEAD_DIM:HEAD_DIM + 1]
    return out[:, 0:HEAD_DIM] * (1.0 / denom)


def _with_ones(v):
    return jnp.concatenate([v.astype(BF16), jnp.ones(v.shape, BF16)], axis=1)


def _stack_heads(q, kv):
    return jnp.concatenate(
        [q[:, (kv * Q_PER_KV + g) * HEAD_DIM:(kv * Q_PER_KV + g + 1) * HEAD_DIM]
         for g in range(Q_PER_KV)], axis=0)


def _sink_column(sink_ref, kv, rows):
    r = lax.broadcasted_iota(I32, (Q_PER_KV * rows, 1), 0)
    col = jnp.zeros((Q_PER_KV * rows, 1), F32)
    for g in range(Q_PER_KV):
        col = jnp.where((r >= g * rows) & (r < (g + 1) * rows), sink_ref[kv * Q_PER_KV + g], col)
    return col * LOG2_E


def _qk(q, k):
    return lax.dot_general(q, k, (((1,), (1,)), ((), ())), preferred_element_type=F32)


def _ctx_attn_kernel(sink_ref, q_ref, k_ref, v_ref, o_ref):
    rows = q_ref.shape[0]
    q = q_ref[...]
    for kv in range(N_KV_HEADS):
        kh = k_ref[:, kv * HEAD_DIM:(kv + 1) * HEAD_DIM].astype(BF16)
        vh = v_ref[:, kv * HEAD_DIM:(kv + 1) * HEAD_DIM].astype(BF16)
        qs = _stack_heads(q, kv)
        s = _qk(qs, kh)
        o = _softmax_pv([s], [vh], _sink_column(sink_ref, kv, rows))
        for g in range(Q_PER_KV):
            hd = kv * Q_PER_KV + g
            o_ref[:, hd * HEAD_DIM:(hd + 1) * HEAD_DIM] = o[g * rows:(g + 1) * rows].astype(BF16)


def _context_attention(q, k, v, sink, seq):
    t = q.shape[0]
    row = lambda b: (b, 0)
    return pl.pallas_call(
        _ctx_attn_kernel,
        grid=(t // seq,),
        in_specs=[pl.BlockSpec(memory_space=pltpu.SMEM),
                  pl.BlockSpec((seq, ATTN_WIDTH), row),
                  pl.BlockSpec((seq, KV_WIDTH), row),
                  pl.BlockSpec((seq, KV_WIDTH), row)],
        out_specs=pl.BlockSpec((seq, ATTN_WIDTH), row),
        out_shape=jax.ShapeDtypeStruct((t, ATTN_WIDTH), BF16),
        name="context_attention",
    )(sink, q, k, v)


def _lat_attn_kernel(sink_ref, q_ref, k_ref, v_ref, ck_ref, cv_ref, o_ref, *, n_seq):
    i = pl.program_id(1)
    band = 3 * BLOCK
    start = pl.multiple_of(jnp.clip((i - 1) * BLOCK, 0, n_seq - band), BLOCK)
    rows = Q_PER_KV * BLOCK
    qpos = i * BLOCK + lax.broadcasted_iota(I32, (rows, band), 0) % BLOCK
    kpos = start + lax.broadcasted_iota(I32, (rows, band), 1)
    mask = jnp.abs(kpos - qpos) <= WINDOW
    q = q_ref[...]
    for kv in range(N_KV_HEADS):
        cols = slice(kv * HEAD_DIM, (kv + 1) * HEAD_DIM)
        kb = k_ref[pl.ds(start, band), cols].astype(BF16)
        vb = _with_ones(v_ref[pl.ds(start, band), cols])
        cached = pl.ds(kv, ck_ref.shape[0] // N_KV_HEADS, stride=N_KV_HEADS)
        ck = ck_ref[cached, :].astype(BF16)
        cv = _with_ones(cv_ref[cached, :])
        qs = _stack_heads(q, kv)
        s_loc = jnp.where(mask, _qk(qs, kb), NEG)
        s_ctx = _qk(qs, ck)
        o = _softmax_pv([s_loc, s_ctx], [vb, cv], _sink_column(sink_ref, kv, BLOCK))
        for g in range(Q_PER_KV):
            hd = kv * Q_PER_KV + g
            o_ref[:, hd * HEAD_DIM:(hd + 1) * HEAD_DIM] = o[g * BLOCK:(g + 1) * BLOCK].astype(BF16)


def _latent_attention(q, k, v, ck, cv, sink, n_seq):
    t = q.shape[0]
    nb = n_seq // BLOCK
    cache_rows = ck.shape[0] // (t // n_seq)
    return pl.pallas_call(
        functools.partial(_lat_attn_kernel, n_seq=n_seq),
        grid=(t // n_seq, nb),
        in_specs=[pl.BlockSpec(memory_space=pltpu.SMEM),
                  pl.BlockSpec((BLOCK, ATTN_WIDTH), lambda b, i: (b * nb + i, 0)),
                  pl.BlockSpec((n_seq, KV_WIDTH), lambda b, i: (b, 0)),
                  pl.BlockSpec((n_seq, KV_WIDTH), lambda b, i: (b, 0)),
                  pl.BlockSpec((cache_rows, HEAD_DIM), lambda b, i: (b, 0)),
                  pl.BlockSpec((cache_rows, HEAD_DIM), lambda b, i: (b, 0))],
        out_specs=pl.BlockSpec((BLOCK, ATTN_WIDTH), lambda b, i: (b * nb + i, 0)),
        out_shape=jax.ShapeDtypeStruct((t, ATTN_WIDTH), BF16),
        name="latent_attention",
    )(sink, q, k, v, ck, cv)


def _pool_group(p_ref, r0, seq, w_ref, s_ref, g):
    n = SUB_TILE
    rows = n + 2 * POOL_HALO
    static = isinstance(r0, int)
    t0 = r0 % seq
    t = t0 + lax.broadcasted_iota(I32, (n, 1), 0)
    has_top = t0 > 0
    has_bottom = t0 + n < seq
    zeros = jnp.zeros((POOL_HALO, POOL_GROUP), F32)
    w = POOL_SIZES[g]
    cols = slice(g * POOL_GROUP, (g + 1) * POOL_GROUP)
    pg = p_ref[pl.ds(r0, n), cols]
    if static:
        top = p_ref[r0 - POOL_HALO:r0, cols] if has_top else zeros
        bottom = p_ref[r0 + n:r0 + n + POOL_HALO, cols] if has_bottom else zeros
    else:
        top_row = pl.multiple_of(jnp.maximum(r0 - POOL_HALO, 0), POOL_HALO)
        bottom_row = pl.multiple_of(jnp.minimum(r0 + n, p_ref.shape[0] - POOL_HALO), POOL_HALO)
        top = jnp.where(has_top, p_ref[pl.ds(top_row, POOL_HALO), cols], 0.0)
        bottom = jnp.where(has_bottom, p_ref[pl.ds(bottom_row, POOL_HALO), cols], 0.0)
    x = jnp.concatenate([top, pg, bottom], axis=0)
    fwd = x
    span = 1
    while span < w // 2:
        fwd = fwd + pltpu.roll(fwd, rows - span, 0)
        span *= 2
    if (w // 2) % SUBLANES == 0:
        wsum = fwd[POOL_HALO - w // 2:POOL_HALO - w // 2 + n] + fwd[POOL_HALO:POOL_HALO + n]
    else:
        wsum = (fwd + pltpu.roll(fwd, w // 2, 0))[POOL_HALO:POOL_HALO + n]
    lo = jnp.maximum(t - w // 2, 0)
    hi = jnp.minimum(t + w - w // 2, seq)
    inv_cnt = 1.0 / (hi - lo).astype(F32)
    mixed = wsum * inv_cnt - pg
    y = jnp.dot(mixed.astype(BF16), w_ref[g], preferred_element_type=F32)
    return (y * s_ref[:, cols]).astype(BF16)


def _pack_pair(lo, hi):
    return lax.bitcast_convert_type(pltpu.pack_elementwise([lo, hi], packed_dtype=BF16), I32)


def _unpack_pair(words):
    lo = pltpu.unpack_elementwise(words, index=0, packed_dtype=BF16, unpacked_dtype=F32)
    hi = pltpu.unpack_elementwise(words, index=1, packed_dtype=BF16, unpacked_dtype=F32)
    return lo, hi


def _outproj_kernel(a_ref, p_ref, x_ref, mod_ref, g_ref, wo_ref, wr_ref, pw_ref, ps_ref,
                    x1_ref, h_ref, aff_ref, *, seq):
    mod = mod_ref[0]
    groups = SUB_TILE // SUBLANES
    half = D_MODEL // 2
    tm = x_ref.shape[0]
    steps_per_p_block = p_ref.shape[0] // tm
    for s in range(tm // SUB_TILE):
        rows = slice(s * SUB_TILE, (s + 1) * SUB_TILE)
        grp = slice(s * groups, (s + 1) * groups)
        r0 = s * SUB_TILE
        if steps_per_p_block > 1:
            r0 = pl.multiple_of((pl.program_id(0) % steps_per_p_block) * tm + r0, SUB_TILE)
        pooled = jnp.concatenate([_pool_group(p_ref, r0, seq, pw_ref, ps_ref, g)
                                  for g in range(len(POOL_SIZES))], axis=1)
        mix = (jnp.dot(a_ref[rows, :], wo_ref[0:ATTN_WIDTH, :], preferred_element_type=F32)
               + jnp.dot(pooled, wo_ref[ATTN_WIDTH:D_MODEL, :], preferred_element_type=F32))
        x1 = x_ref[rows, :] + mod[2:3] * mix
        x1_ref[rows, :] = x1
        h = _norm_mod(x1, g_ref[...], mod[3:4], mod[4:5])
        logits = jnp.dot(h.astype(BF16), wr_ref[...], preferred_element_type=F32)
        lane = lax.broadcasted_iota(I32, logits.shape, 1)
        logits = jnp.where(lane < N_EXPERTS, logits, -jnp.inf)
        m = jnp.max(logits, axis=-1, keepdims=True)
        e = jnp.exp(logits - m)
        aff = e / jnp.sum(e, axis=-1, keepdims=True)
        aff_ref[rows, :] = aff[:, 0:N_EXPERTS]
        for c in range(HQ_TILES - 1):
            words = _pack_pair(h[:, c * LANES:(c + 1) * LANES],
                               h[:, half + c * LANES:half + (c + 1) * LANES])
            h_ref[grp, c * SUBLANES:(c + 1) * SUBLANES, :] = (
                words.reshape(groups, SUBLANES, LANES))
        h_ref[grp, (HQ_TILES - 1) * SUBLANES:, :] = (
            lax.bitcast_convert_type(aff, I32).reshape(groups, SUBLANES, LANES))


def _out_projection(attn, p, x2, mod3, mod_row, g, w_out_bf, w_router, pool_w, pool_scale, seq):
    t = x2.shape[0]
    tm = PROJ_TILE
    row = lambda i: (i, 0)
    p_rows = max(tm, seq)
    steps_per_p_block = p_rows // tm
    return pl.pallas_call(
        functools.partial(_outproj_kernel, seq=seq),
        grid=(t // tm,),
        in_specs=[pl.BlockSpec((tm, ATTN_WIDTH), row),
                  pl.BlockSpec((p_rows, POOL_WIDTH), lambda i: (i // steps_per_p_block, 0)),
                  pl.BlockSpec((tm, D_MODEL), row),
                  pl.BlockSpec((1, N_MOD, D_MODEL), lambda i: (mod_row(i), 0, 0)),
                  pl.BlockSpec((1, D_MODEL), lambda i: (0, 0)),
                  pl.BlockSpec((D_MODEL, D_MODEL), lambda i: (0, 0)),
                  pl.BlockSpec((D_MODEL, LANES), lambda i: (0, 0)),
                  pl.BlockSpec((len(POOL_SIZES), POOL_GROUP, POOL_GROUP), lambda i: (0, 0, 0)),
                  pl.BlockSpec((1, POOL_WIDTH), lambda i: (0, 0))],
        out_specs=[pl.BlockSpec((tm, D_MODEL), row),
                   pl.BlockSpec((tm // SUBLANES, HQ_TILES * SUBLANES, LANES), lambda i: (i, 0, 0)),
                   pl.BlockSpec((tm, N_EXPERTS), row)],
        out_shape=[jax.ShapeDtypeStruct((t, D_MODEL), F32),
                   jax.ShapeDtypeStruct((t // SUBLANES, HQ_TILES * SUBLANES, LANES), I32),
                   jax.ShapeDtypeStruct((t, N_EXPERTS), F32)],
        compiler_params=pltpu.CompilerParams(
            vmem_limit_bytes=_vmem_limit(2 * D_MODEL * D_MODEL * 2 + 24 * tm * D_MODEL * 4)),
        name="out_projection",
    )(attn, p, x2, mod3, g, w_out_bf, w_router, pool_w, pool_scale)


def _route_kernel(a_ref, idx_ref, slot_ref, off_ref, slot_scr, *, cap):
    a = a_ref[...]
    n_e, n_c, _ = a.shape
    rows = n_e * n_c

    def enough(cand):
        cand_f = lax.bitcast_convert_type(cand, F32)
        cnt = jnp.sum(jnp.sum((a >= cand_f).astype(F32), axis=1, keepdims=True),
                      axis=2, keepdims=True)
        return cnt >= cap

    def two_bits(it, thr):
        low = 28 - 2 * it
        for setting in (1, 2, 3):
            cand = thr | jnp.left_shift(jnp.int32(setting), low)
            best = jnp.where(enough(cand), cand, thr if setting == 1 else best)
        return best

    top = jnp.full((n_e, 1, 1), 1 << 30, I32)
    thr = jnp.where(enough(top), top, 0)
    thr = lax.fori_loop(0, 15, two_bits, thr)
    thr_f = lax.bitcast_convert_type(thr, F32)
    gt = (a > thr_f).astype(F32).reshape(rows, LANES)
    eq = (a == thr_f).astype(F32).reshape(rows, LANES)

    li = lax.broadcasted_iota(I32, (LANES, LANES), 0)
    lj = lax.broadcasted_iota(I32, (LANES, LANES), 1)
    upper_incl = (li <= lj).astype(BF16)
    ci = lax.broadcasted_iota(I32, (n_c, n_c), 0)
    cj = lax.broadcasted_iota(I32, (n_c, n_c), 1)
    before = (cj < ci).astype(BF16)
    whole = jnp.ones((n_c, n_c), BF16)

    def over_chunks(mat, col):
        wide = jnp.broadcast_to(col, (rows, LANES)).astype(BF16)
        side_by_side = jnp.concatenate([wide[e * n_c:(e + 1) * n_c] for e in range(n_e)], axis=1)
        res = jnp.dot(mat, side_by_side, preferred_element_type=F32)
        return jnp.concatenate([res[:, e * LANES:(e + 1) * LANES] for e in range(n_e)], axis=0)

    def prefix(x):
        incl = jnp.dot(x.astype(BF16), upper_incl, preferred_element_type=F32)
        tot = incl[:, LANES - 1:LANES]
        return incl, tot, over_chunks(before, tot)

    n_gt = over_chunks(whole, jnp.sum(gt, axis=1, keepdims=True))
    need = cap - n_gt
    incl_eq, _, off_eq = prefix(eq)
    rank_eq = off_eq + incl_eq - eq
    sel = jnp.where((eq > 0) & (rank_eq < need), 1.0, gt)
    incl, tot, off = prefix(sel)
    slot = off + incl - sel
    slot_scr[...] = jnp.where(sel > 0, slot, -1.0)
    for c in range(n_c):
        per_expert = slot_scr[pl.ds(c, n_e, stride=n_c), :]
        slot_ref[c * LANES:(c + 1) * LANES, :] = per_expert.T.astype(I32)
    off_ref[...] = off.astype(I32).reshape(n_e, n_c, LANES)

    s_lane = lax.broadcasted_iota(I32, (1, cap), 1).astype(F32)
    c_col = lax.broadcasted_iota(I32, (n_c, 1), 0).astype(F32)
    for e in range(n_e):
        r0 = e * n_c
        incl_e = incl[r0:r0 + n_c]
        off_e = off[r0:r0 + n_c, 0:1]
        tot_e = tot[r0:r0 + n_c]
        onehot = ((off_e <= s_lane) & (s_lane < off_e + tot_e)).astype(F32)
        counts = lax.dot_general(incl_e.astype(BF16), onehot.astype(BF16),
                                 (((0,), (0,)), ((), ())), preferred_element_type=F32)
        local = s_lane - jnp.sum(onehot * off_e, axis=0, keepdims=True)
        lane = jnp.sum((counts <= local).astype(F32), axis=0, keepdims=True)
        chunk = jnp.sum(onehot * c_col, axis=0, keepdims=True)
        idx_ref[e] = (chunk * LANES + lane).astype(I32)


def _routing(aff, cap):
    t = aff.shape[0]
    n_c = t // LANES
    a3 = aff.T.reshape(N_EXPERTS, n_c, LANES)
    return pl.pallas_call(
        functools.partial(_route_kernel, cap=cap),
        out_shape=[jax.ShapeDtypeStruct((N_EXPERTS, 1, cap), I32),
                   jax.ShapeDtypeStruct((t, N_EXPERTS), I32),
                   jax.ShapeDtypeStruct((N_EXPERTS, n_c, LANES), I32)],
        scratch_shapes=[pltpu.VMEM((N_EXPERTS * n_c, LANES), F32)],
        compiler_params=pltpu.CompilerParams(vmem_limit_bytes=_vmem_limit(48 << 20)),
        name="routing",
    )(a3)


def _gather_rows(table, row_ids):
    n_chunks = row_ids.shape[0]
    n_workers = SC_CORES * SC_SUBCORES
    per_worker = n_chunks // n_workers
    assert row_ids.shape[1] == GATHER_CHUNK and n_chunks % n_workers == 0
    mesh = plsc.VectorSubcoreMesh(core_axis_name="core", subcore_axis_name="subcore")

    @functools.partial(
        pl.kernel, mesh=mesh,
        out_type=jax.ShapeDtypeStruct((n_chunks * GATHER_CHUNK, LANES), I32),
        scratch_types=[pltpu.VMEM((per_worker, GATHER_CHUNK), I32),
                       pltpu.VMEM((2, GATHER_CHUNK, LANES), I32),
                       pltpu.SemaphoreType.DMA((2,)),
                       pltpu.SemaphoreType.DMA((2,))],
        name="gather_rows",
    )
    def gather(table_hbm, ids_hbm, out_hbm, ids_v, rows_v, gather_sem, store_sem):
        worker = lax.axis_index("subcore") * SC_CORES + lax.axis_index("core")
        first = worker * per_worker
        pltpu.sync_copy(ids_hbm.at[worker], ids_v)

        def fetch(j):
            return pltpu.make_async_copy(table_hbm.at[ids_v.at[j]], rows_v.at[j % 2],
                                         gather_sem.at[j % 2])

        def store(j):
            rows = pl.ds(pl.multiple_of((first + j) * GATHER_CHUNK, GATHER_CHUNK), GATHER_CHUNK)
            return pltpu.make_async_copy(rows_v.at[j % 2], out_hbm.at[rows], store_sem.at[j % 2])

        for j in range(per_worker):
            if j >= 2:
                store(j - 2).wait()
            fetch(j).start()
            if j >= 1:
                fetch(j - 1).wait()
                store(j - 1).start()
        fetch(per_worker - 1).wait()
        store(per_worker - 1).start()
        for j in range(max(per_worker - 2, 0), per_worker):
            store(j).wait()

    return gather(table, row_ids.reshape(n_workers, per_worker, GATHER_CHUNK))


def _packed_row_ids(idx, cap):
    tok = idx.reshape(-1, 1, cap // GATHER_CHUNK, GATHER_CHUNK)
    tile = jnp.arange(HQ_TILES, dtype=I32).reshape(1, HQ_TILES, 1, 1)
    ids = ((tok // SUBLANES) * HQ_TILES + tile) * SUBLANES + tok % SUBLANES
    return ids.reshape(-1, GATHER_CHUNK)


def _ffn_kernel(*refs, caps, n_f, row_chunk, first_expert, n_prior):
    n_g = len(caps)
    x_refs = refs[:n_g]
    wg_ref, wu_ref, wd_ref = refs[n_g:n_g + 3]
    y_refs = refs[n_g + 3 + n_prior:2 * n_g + 3 + n_prior]
    acc_ref = refs[2 * n_g + 3 + n_prior]
    e = pl.program_id(0)
    f = pl.program_id(1)

    @pl.when((e == 0) & (f == 0))
    def _():
        acc_ref[...] = jnp.zeros_like(acc_ref)

    def step(last):
        first = f == 0
        base = 0
        for x_ref, y_ref, cap in zip(x_refs, y_refs, caps):
            if last:
                aff = lax.bitcast_convert_type(x_ref[0, HQ_TILES - 1], F32)
                lane = lax.broadcasted_iota(I32, aff.shape, 1)
                gates = jnp.sum(jnp.where(lane == e + first_expert, aff, 0.0), axis=1,
                                keepdims=True)
                y_ref[0, cap:, :] = jnp.zeros((COMBINE_WINDOW, D_MODEL // 2), I32)
            for r in range(cap // row_chunk):
                rs = slice(r * row_chunk, (r + 1) * row_chunk)
                acc_rows = slice(base + r * row_chunk, base + (r + 1) * row_chunk)
                pairs = [_unpack_pair(x_ref[0, c, rs, :]) for c in range(HQ_TILES - 1)]
                x = jnp.concatenate([p[0] for p in pairs] + [p[1] for p in pairs], axis=1)
                gate_act = jnp.dot(x, wg_ref[0], preferred_element_type=F32)
                up = jnp.dot(x, wu_ref[0], preferred_element_type=F32)
                hid = (gate_act * jax.nn.sigmoid(gate_act)) * up
                part = jnp.dot(hid, wd_ref[0], preferred_element_type=F32)
                if last:
                    y = (part + acc_ref[acc_rows, :]) * gates[rs]
                    y_ref[0, rs, :] = _pack_pair(y[:, 0:D_MODEL // 2], y[:, D_MODEL // 2:])
                else:
                    acc_ref[acc_rows, :] = part + jnp.where(first, 0.0, acc_ref[acc_rows, :])
            base += cap

    @pl.when(f < n_f - 1)
    def _():
        step(False)

    @pl.when(f == n_f - 1)
    def _():
        step(True)


def _expert_ffn(xs_groups, w_gate, w_up, w_down, first_expert, prior_outputs):
    n_f = 4
    tf = D_EXPERT // n_f
    row_chunk = 512
    n_e = xs_groups[0].shape[0]
    e0 = first_expert
    caps = tuple(xs.shape[2] for xs in xs_groups)
    rows = sum(caps)
    est = (2 * HQ_TILES * rows * LANES * 4 + rows * D_MODEL * (4 + 2 * 2) + 2 * 3 * D_MODEL * tf * 4
           + row_chunk * (3 * tf + 2 * D_MODEL) * 4)
    x_specs = [pl.BlockSpec((1, HQ_TILES, cap, LANES), lambda e, f: (e, 0, 0, 0)) for cap in caps]
    y_shapes = [(N_EXPERTS, cap + COMBINE_WINDOW, D_MODEL // 2) for cap in caps]
    y_specs = [pl.BlockSpec((1,) + s[1:], lambda e, f: (e + e0, 0, 0)) for s in y_shapes]
    prior = list(prior_outputs or [])
    n_in = len(caps) + 3
    return pl.pallas_call(
        functools.partial(_ffn_kernel, caps=caps, n_f=n_f, row_chunk=row_chunk,
                          first_expert=e0, n_prior=len(prior)),
        grid=(n_e, n_f),
        in_specs=(x_specs + [pl.BlockSpec((1, D_MODEL, tf), lambda e, f: (e + e0, 0, f)),
                             pl.BlockSpec((1, D_MODEL, tf), lambda e, f: (e + e0, 0, f)),
                             pl.BlockSpec((1, tf, D_MODEL), lambda e, f: (e + e0, f, 0))]
                  + [pl.BlockSpec(memory_space=pl.ANY) for _ in prior]),
        out_specs=y_specs,
        out_shape=[jax.ShapeDtypeStruct(s, I32) for s in y_shapes],
        input_output_aliases={n_in + j: j for j in range(len(prior))},
        scratch_shapes=[pltpu.VMEM((rows, D_MODEL), F32)],
        compiler_params=pltpu.CompilerParams(
            dimension_semantics=("arbitrary", "arbitrary"),
            vmem_limit_bytes=_vmem_limit(est + (4 << 20))),
        name="expert_ffn",
    )(*xs_groups, w_gate, w_up, w_down, *prior)


def _combine_kernel(start_ref, nch_ref, wide_ref, x1_ref, mod_ref, g_ref, slot_ref, y_hbm, o_ref,
                    ybuf_ref, ffn_ref, sem, *, first_k, kblock):
    i = pl.program_id(0)
    tm = x1_ref.shape[0]
    cur = i % 2
    n_window_rows = N_EXPERTS * COMBINE_WINDOW

    window_head = COMBINE_WINDOW - COMBINE_CHUNK

    def window_copy(e, src_row, buf):
        return pltpu.make_async_copy(
            y_hbm.at[e, pl.ds(pl.multiple_of(src_row, SUBLANES), window_head)],
            ybuf_ref.at[buf, e * COMBINE_WINDOW:e * COMBINE_WINDOW + window_head], sem.at[buf])

    def window_tail_copy(e, src_row, buf):
        return chunk_copy(e, src_row + window_head, buf,
                          jnp.int32(e * COMBINE_WINDOW + window_head))

    def needs_tail(tile, e):
        return nch_ref[tile, e] * COMBINE_CHUNK > window_head

    def chunk_copy(e, src_row, buf, dst_row):
        return pltpu.make_async_copy(
            y_hbm.at[e, pl.ds(pl.multiple_of(src_row, SUBLANES), COMBINE_CHUNK)],
            ybuf_ref.at[buf, pl.ds(pl.multiple_of(dst_row, COMBINE_CHUNK), COMBINE_CHUNK)],
            sem.at[buf])

    def fetch(tile, buf):
        @pl.when(wide_ref[tile] == 0)
        def _():
            for e in range(N_EXPERTS):
                window_copy(e, start_ref[tile, e], buf).start()

                @pl.when(needs_tail(tile, e))
                def _(e=e):
                    window_tail_copy(e, start_ref[tile, e], buf).start()

        @pl.when(wide_ref[tile] != 0)
        def _():
            pos = jnp.int32(0)
            for e in range(N_EXPERTS):
                start = start_ref[tile, e]
                nch = nch_ref[tile, e]

                def issue(c, carry, e=e, start=start, pos=pos):
                    chunk_copy(e, start + c * COMBINE_CHUNK, buf, pos + c * COMBINE_CHUNK).start()
                    return carry
                lax.fori_loop(0, nch, issue, 0)
                pos = pos + nch * COMBINE_CHUNK

    @pl.when(i == 0)
    def _():
        ybuf_ref[...] = jnp.zeros_like(ybuf_ref)
        fetch(0, 0)

    @pl.when(i + 1 < pl.num_programs(0))
    def _():
        fetch(i + 1, 1 - cur)

    def finish(ffn):
        out = x1_ref[...] + mod_ref[0][5:6] * ffn
        ms = jnp.mean(out * out, axis=-1, keepdims=True)
        o_ref[...] = (out * lax.rsqrt(ms + EPS)) * g_ref[...]

    def apply_selection(sel, k0, width):
        lo, hi = _unpack_pair(ybuf_ref[cur, pl.ds(k0, width), :])
        return jnp.concatenate([jnp.dot(sel, lo, preferred_element_type=F32),
                                jnp.dot(sel, hi, preferred_element_type=F32)], axis=1)

    @pl.when(wide_ref[i] == 0)
    def _():
        for e in range(N_EXPERTS):
            window_copy(e, jnp.int32(0), cur).wait()

            @pl.when(needs_tail(i, e))
            def _(e=e):
                window_tail_copy(e, jnp.int32(0), cur).wait()
        expert_lane = lax.broadcasted_iota(I32, (1, N_EXPERTS), 1)
        starts = jnp.zeros((1, N_EXPERTS), I32)
        for e in range(N_EXPERTS):
            starts = jnp.where(expert_lane == e, start_ref[i, e], starts)
        slots = slot_ref[...]
        rows_in_window = jnp.where(slots >= 0, (slots - starts).astype(F32), -1.0)
        window_of_lane = lax.broadcasted_iota(I32, (N_EXPERTS, n_window_rows), 1) // COMBINE_WINDOW
        spread = (window_of_lane == lax.broadcasted_iota(I32, (N_EXPERTS, n_window_rows), 0))
        target = jnp.dot(rows_in_window.astype(BF16), spread.astype(BF16),
                         preferred_element_type=F32)
        lane_row = (lax.broadcasted_iota(I32, (tm, n_window_rows), 1) % COMBINE_WINDOW).astype(F32)
        sel = jnp.where(target == lane_row, 1.0, 0.0)
        ffn = None
        for k0 in range(0, n_window_rows, kblock):
            part = apply_selection(sel[:, k0:k0 + kblock], k0, kblock)
            ffn = part if ffn is None else ffn + part
        finish(ffn)

    @pl.when(wide_ref[i] != 0)
    def _():
        pos = jnp.int32(0)
        begins = []
        for e in range(N_EXPERTS):
            begins.append(pos)
            pos = pos + nch_ref[i, e] * COMBINE_CHUNK
        begins.append(pos)

        def drain(c, carry):
            chunk_copy(0, jnp.int32(0), cur, jnp.int32(0)).wait()
            return carry
        lax.fori_loop(0, pos // COMBINE_CHUNK, drain, 0)

        expert_lane = lax.broadcasted_iota(I32, (1, N_EXPERTS), 1)
        expert_row = lax.broadcasted_iota(I32, (N_EXPERTS, 1), 0)
        shift = jnp.zeros((1, N_EXPERTS), I32)
        range_lo = jnp.zeros((N_EXPERTS, 1), I32)
        range_hi = jnp.zeros((N_EXPERTS, 1), I32)
        for e in range(N_EXPERTS):
            shift = jnp.where(expert_lane == e, begins[e] - start_ref[i, e], shift)
            range_lo = jnp.where(expert_row == e, begins[e], range_lo)
            range_hi = jnp.where(expert_row == e, begins[e + 1], range_hi)
        slots = slot_ref[...]
        packed_row = slots + shift
        row_hi = jnp.where(slots >= 0, packed_row // ROW_SPLIT, -1).astype(F32).astype(BF16)
        row_lo = jnp.where(slots >= 0, packed_row % ROW_SPLIT, 0).astype(F32).astype(BF16)

        def selection(k0, width):
            col = k0 + lax.broadcasted_iota(I32, (N_EXPERTS, width), 1)
            spread = ((col >= range_lo) & (col < range_hi)).astype(BF16)
            want_hi = jnp.dot(row_hi, spread, preferred_element_type=F32)
            want_lo = jnp.dot(row_lo, spread, preferred_element_type=F32)
            here = k0 + lax.broadcasted_iota(I32, (1, width), 1)
            hit = ((want_hi == (here // ROW_SPLIT).astype(F32))
                   & (want_lo == (here % ROW_SPLIT).astype(F32)) & (here < pos))
            return jnp.where(hit, 1.0, 0.0)

        ffn_ref[...] = apply_selection(selection(0, first_k), 0, first_k)

        def kstep(kb, carry):
            k0 = pl.multiple_of(kb * kblock, kblock)
            ffn_ref[...] += apply_selection(selection(k0, kblock), k0, kblock)
            return carry
        lax.fori_loop(first_k // kblock, (pos + kblock - 1) // kblock, kstep, 0)
        finish(ffn_ref[...])


def _combine(x1, mod3, mod_row, g_final, slot_t, tile_start, tile_nch, tile_wide, y):
    t = x1.shape[0]
    tm = TOKEN_TILE
    first_k = 2 * tm + N_EXPERTS * COMBINE_CHUNK
    kblock = 256
    max_rows = N_EXPERTS * (tm + 2 * COMBINE_CHUNK)
    max_rows = -(-max_rows // kblock) * kblock
    row = lambda i, *_: (i, 0)
    grid_spec = pltpu.PrefetchScalarGridSpec(
        num_scalar_prefetch=3,
        grid=(t // tm,),
        in_specs=[pl.BlockSpec((tm, D_MODEL), row),
                  pl.BlockSpec((1, N_MOD, D_MODEL), lambda i, *_: (mod_row(i), 0, 0)),
                  pl.BlockSpec((1, D_MODEL), lambda i, *_: (0, 0)),
                  pl.BlockSpec((tm, N_EXPERTS), row),
                  pl.BlockSpec(memory_space=pl.ANY)],
        out_specs=pl.BlockSpec((tm, D_MODEL), row),
        scratch_shapes=[pltpu.VMEM((2, max_rows, D_MODEL // 2), I32),
                        pltpu.VMEM((tm, D_MODEL), F32),
                        pltpu.SemaphoreType.DMA((2,))],
    )
    return pl.pallas_call(
        functools.partial(_combine_kernel, first_k=first_k, kblock=kblock),
        grid_spec=grid_spec,
        out_shape=jax.ShapeDtypeStruct((t, D_MODEL), F32),
        compiler_params=pltpu.CompilerParams(
            dimension_semantics=("arbitrary",),
            vmem_limit_bytes=_vmem_limit(2 * max_rows * D_MODEL * 2 + 16 * tm * D_MODEL * 4)),
        name="combine",
    )(tile_start, tile_nch, tile_wide, x1, mod3, g_final, slot_t, y)


def _rope_tables(n):
    f32 = np.float32
    rows = n // GRID_W
    row = np.repeat(np.arange(rows, dtype=f32), GRID_W)
    col = np.tile(np.arange(GRID_W, dtype=f32), rows)
    inv = (f32(ROPE_THETA) ** (-np.arange(ROPE_FREQS, dtype=f32) / f32(ROPE_FREQS))).astype(f32)
    ang_r = row[:, None] * inv
    ang_c = col[:, None] * inv
    zero = np.zeros_like(ang_r)
    cos = np.concatenate([np.cos(ang_r)] * 2 + [np.cos(ang_c)] * 2, axis=1)
    sin_a = np.concatenate([-np.sin(ang_r), zero, -np.sin(ang_c), zero], axis=1)
    sin_b = np.concatenate([zero, np.sin(ang_r), zero, np.sin(ang_c)], axis=1)
    return tuple(jnp.asarray(t, F32) for t in (cos, sin_a, sin_b))


def _token_group(x, mod3, mod_row, seq, weights, rope_tabs, ctx_kv):
    (norm_mix, w_in, sink, pool_w, pool_scale, w_out, norm_ffn, w_router,
     w_gate, w_up, w_down, norm_final) = weights
    b = x.shape[0]
    t = b * seq
    x2 = x.reshape(t, D_MODEL)
    q, k, v, p, *state = _in_projection(x2, mod3, mod_row(PROJ_TILE), norm_mix, w_in,
                                        rope_tabs, w_out)
    w_in_bf = w_in if rope_tabs is not None else state.pop()
    w_out_bf = w_out if rope_tabs is not None else state.pop()
    if ctx_kv is None:
        attn = _context_attention(q, k, v, sink, seq)
    else:
        attn = _latent_attention(q, k, v, ctx_kv[0], ctx_kv[1], sink, seq)
    x1, h_packed, aff = _out_projection(attn, p, x2, mod3, mod_row(PROJ_TILE), norm_ffn,
                                        w_out_bf, w_router, pool_w, pool_scale, seq)

    cap = EC_FACTOR * t // N_EXPERTS
    idx, slot_t, off3 = _routing(aff, cap)
    table = h_packed.reshape(-1, LANES)
    per_range = N_EXPERTS // FFN_RANGES
    xs = []
    for r in range(FFN_RANGES):
        ids = _packed_row_ids(idx[r * per_range:(r + 1) * per_range], cap)
        xs.append(_gather_rows(table, ids).reshape(per_range, HQ_TILES, cap, LANES))

    chunks_per_tile = TOKEN_TILE // LANES
    tile_off = off3[:, ::chunks_per_tile, 0]
    tile_end = jnp.concatenate([tile_off[:, 1:], jnp.full((N_EXPERTS, 1), cap, I32)], axis=1)
    tile_start = (tile_off // SUBLANES) * SUBLANES
    tile_nch = jnp.where(tile_end > tile_off,
                         (tile_end - tile_start + COMBINE_CHUNK - 1) // COMBINE_CHUNK, 0)
    tile_wide = jnp.any(tile_end - tile_start > COMBINE_WINDOW, axis=0).astype(I32)

    def finish(y):
        out = _combine(x1, mod3, mod_row(TOKEN_TILE), norm_final, slot_t, tile_start.T,
                       tile_nch.T, tile_wide, y)
        return out.reshape(b, seq, D_MODEL)
    return xs, finish, state, (w_in_bf, w_out_bf)


def kernel(x_prompt, x_sample, c, cache_k, cache_v, c_ctx, w_ada, b_ada, norm_mix, w_in,
           sink_logits, pool_w, pool_scale, w_out, norm_ffn, w_router, w_gate, w_up, w_down,
           norm_final):
    n_b, seq, _ = x_prompt.shape
    n_db, n_lat, _ = x_sample.shape
    assert 1 + n_db <= MOD_ROWS and seq == TOKEN_TILE and n_lat % PROJ_TILE == 0
    assert w_in.shape[0] == 1 and cache_k.shape[1] == 1
    assert (n_b * seq) % PROJ_TILE == 0

    cond = jnp.concatenate(
        [c_ctx[None, :], c, jnp.zeros((MOD_ROWS - 1 - n_db, D_MODEL), F32)], axis=0)
    mod3 = _modulation(cond, w_ada[0], b_ada[0]).reshape(MOD_ROWS, N_MOD, D_MODEL)

    w_router_bf = jnp.pad(w_router[0], ((0, 0), (0, LANES - N_EXPERTS))).astype(BF16)
    weights = (norm_mix[0][None, :], w_in[0], sink_logits[0], pool_w[0].astype(BF16),
               pool_scale[0][None, :], w_out[0], norm_ffn[0][None, :], w_router_bf,
               w_gate.reshape(w_gate.shape[1:]), w_up.reshape(w_up.shape[1:]),
               w_down.reshape(w_down.shape[1:]), norm_final[None, :])

    xs_p, finish_p, (k_p, v_p), (w_in_bf, w_out_bf) = _token_group(
        x_prompt, mod3, lambda tile: (lambda i: 0), seq, weights, None, None)
    weights = weights[:1] + (w_in_bf,) + weights[2:5] + (w_out_bf,) + weights[6:]

    ck = cache_k.reshape(-1, HEAD_DIM)
    cv = cache_v.reshape(-1, HEAD_DIM)
    xs_l, finish_l, _, _ = _token_group(
        x_sample, mod3, lambda tile: (lambda i: 1 + i // (n_lat // tile)), n_lat, weights,
        _rope_tables(n_lat), (ck, cv))

    ys = None
    for r in range(FFN_RANGES):
        ys = _expert_ffn([xs_p[r], xs_l[r]], weights[8], weights[9], weights[10],
                         r * (N_EXPERTS // FFN_RANGES), ys)
    y_prompt = finish_p(ys[0])
    y_sample = finish_l(ys[1])

    state_k = k_p.reshape(n_b, 1, seq, N_KV_HEADS, HEAD_DIM)
    state_v = v_p.reshape(n_b, 1, seq, N_KV_HEADS, HEAD_DIM)
    return (y_prompt, y_sample, state_k, state_v)
```

```python
import functools

import jax
import jax.numpy as jnp
import numpy as np
from jax import lax
from jax.experimental import pallas as pl
from jax.experimental.pallas import tpu as pltpu
from jax.experimental.pallas import tpu_sc as plsc

F32 = jnp.float32
BF16 = jnp.bfloat16
I32 = jnp.int32

D_MODEL = 2048
N_HEADS = 8
N_KV_HEADS = 2
HEAD_DIM = 128
Q_PER_KV = N_HEADS // N_KV_HEADS
ATTN_WIDTH = N_HEADS * HEAD_DIM
KV_WIDTH = N_KV_HEADS * HEAD_DIM
POOL_WIDTH = D_MODEL - ATTN_WIDTH
POOL_SIZES = (2, 4, 8, 16)
POOL_GROUP = POOL_WIDTH // len(POOL_SIZES)
IN_WIDTH = ATTN_WIDTH + 2 * KV_WIDTH + POOL_WIDTH
WINDOW = 128
BLOCK = 128
GRID_W = 64
ROPE_THETA = 10000.0
ROPE_FREQS = HEAD_DIM // 4
N_EXPERTS = 16
EC_FACTOR = 2
D_EXPERT = 1024
N_MOD = 6
EPS = 1e-6
NEG = -1e30
LOG2_E = 1.4426950408889634
ATTN_SCALE = HEAD_DIM ** -0.5 * LOG2_E

LANES = 128
SUBLANES = 8
BF16_ROWS = 16
VMEM_CAP = 64 * 1024 * 1024
SC_CORES = 2
SC_SUBCORES = 16

MOD_ROWS = 8
TOKEN_TILE = 256
PROJ_TILE = 512
SUB_TILE = 256
HQ_TILES = D_MODEL // 2 // LANES + 1
GATHER_CHUNK = 128
FFN_RANGES = 2
POOL_HALO = 8
COMBINE_CHUNK = BF16_ROWS
COMBINE_WINDOW = 64
ROW_SPLIT = 32


def _vmem_limit(nbytes):
    return int(min(VMEM_CAP - (4 << 20), max(nbytes, 16 << 20)))


def _mod_block(c_ref, w_ref, b_ref):
    c = c_ref[...]
    s = c * jax.nn.sigmoid(c)
    return jnp.dot(s.astype(BF16), w_ref[...].astype(BF16),
                   preferred_element_type=F32) + b_ref[...]


def _mod_kernel(c_ref, w_ref, b_ref, o_ref):
    o_ref[...] = _mod_block(c_ref, w_ref, b_ref)


def _modulation(cond, w_ada, b_ada, n):
    tn = 1024
    return pl.pallas_call(
        _mod_kernel,
        grid=(n // tn,),
        in_specs=[pl.BlockSpec((MOD_ROWS, D_MODEL), lambda j: (0, 0)),
                  pl.BlockSpec((D_MODEL, tn), lambda j: (0, j)),
                  pl.BlockSpec((1, tn), lambda j: (0, j))],
        out_specs=pl.BlockSpec((MOD_ROWS, tn), lambda j: (0, j)),
        out_shape=jax.ShapeDtypeStruct((MOD_ROWS, n), F32),
        compiler_params=pltpu.CompilerParams(
            vmem_limit_bytes=_vmem_limit(3 * D_MODEL * tn * 4)),
        name="modulation",
    )(cond, w_ada, b_ada.reshape(1, -1))


def _norm_mod(x, g, shift, scale):
    ms = jnp.mean(x * x, axis=-1, keepdims=True)
    y = x * lax.rsqrt(ms + EPS)
    return (y * g) * (1.0 + scale) + shift


def _inproj_kernel(*refs, rope):
    if rope:
        x_ref, mod_ref, g_ref, w_ref, cos_ref, sa_ref, sb_ref, q_ref, k_ref, v_ref, p_ref = refs
    else:
        (x_ref, mod_ref, g_ref, w_ref, wo_slab_ref, wi_slab_ref, cond_ref, wada_ref, bada_ref,
         q_ref, k_ref, v_ref, p_ref, ks_ref, vs_ref, wo_bf_ref, wi_bf_ref, late_mod_ref) = refs
        wo_bf_ref[...] = wo_slab_ref[...].astype(BF16)
        wi_bf_ref[...] = wi_slab_ref[...].astype(BF16)
        late_mod_ref[...] = _mod_block(cond_ref, wada_ref, bada_ref)
    mod = mod_ref[0]
    for s in range(x_ref.shape[0] // SUB_TILE):
        rows = slice(s * SUB_TILE, (s + 1) * SUB_TILE)
        h = _norm_mod(x_ref[rows, :], g_ref[...], mod[0:1], mod[1:2])
        u = jnp.dot(h.astype(w_ref.dtype), w_ref[...], preferred_element_type=F32)

        def rot(xh, rows=rows):
            return (xh * cos_ref[rows, :] + pltpu.roll(xh, LANES - ROPE_FREQS, 1) * sa_ref[rows, :]
                    + pltpu.roll(xh, ROPE_FREQS, 1) * sb_ref[rows, :])

        for hd in range(N_HEADS):
            xh = u[:, hd * HEAD_DIM:(hd + 1) * HEAD_DIM] * ATTN_SCALE
            q_ref[rows, hd * HEAD_DIM:(hd + 1) * HEAD_DIM] = (rot(xh) if rope else xh).astype(BF16)
        for hd in range(N_KV_HEADS):
            lo = ATTN_WIDTH + hd * HEAD_DIM
            xh = u[:, lo:lo + HEAD_DIM]
            k_ref[rows, hd * HEAD_DIM:(hd + 1) * HEAD_DIM] = rot(xh) if rope else xh
        v_ref[rows, :] = u[:, ATTN_WIDTH + KV_WIDTH:ATTN_WIDTH + 2 * KV_WIDTH]
        p_ref[rows, :] = u[:, ATTN_WIDTH + 2 * KV_WIDTH:]
        if not rope:
            for hd in range(N_KV_HEADS):
                state_rows = pl.ds(s * SUB_TILE * N_KV_HEADS + hd, SUB_TILE, stride=N_KV_HEADS)
                lo = ATTN_WIDTH + hd * HEAD_DIM
                ks_ref[state_rows, :] = u[:, lo:lo + HEAD_DIM]
                vs_ref[state_rows, :] = u[:, lo + KV_WIDTH:lo + KV_WIDTH + HEAD_DIM]


def _in_projection(x2, mod3, mod_row, g, w_in, rope_tabs, w_out, late_mod):
    t = x2.shape[0]
    tm = PROJ_TILE
    rope = rope_tabs is not None
    row = lambda i: (i, 0)
    w_mode = None if rope else pl.Buffered(1)
    in_specs = [pl.BlockSpec((tm, D_MODEL), row),
                pl.BlockSpec((1,) + mod3.shape[1:], lambda i: (mod_row(i), 0, 0)),
                pl.BlockSpec((1, D_MODEL), lambda i: (0, 0)),
                pl.BlockSpec((D_MODEL, IN_WIDTH), lambda i: (0, 0), pipeline_mode=w_mode)]
    args = [x2, mod3, g, w_in]
    if rope:
        n_seq = rope_tabs[0].shape[0]
        seq_blocks = n_seq // tm
        for tab in rope_tabs:
            in_specs.append(pl.BlockSpec((tm, HEAD_DIM), lambda i: (i % seq_blocks, 0)))
            args.append(tab)
    else:
        slab = D_MODEL // (t // tm)
        assert slab * (t // tm) == D_MODEL and slab % BF16_ROWS == 0
        for w in (w_out, w_in):
            in_specs.append(pl.BlockSpec((slab, w.shape[1]), row))
            args.append(w)
        cond, w_ada, b_ada, first_col = late_mod
        late_cols = (w_ada.shape[1] - first_col) // (t // tm)
        first_block = first_col // late_cols
        assert first_block * late_cols == first_col and late_cols % LANES == 0
        in_specs += [pl.BlockSpec((MOD_ROWS, D_MODEL), lambda i: (0, 0)),
                     pl.BlockSpec((D_MODEL, late_cols), lambda i: (0, first_block + i)),
                     pl.BlockSpec((1, late_cols), lambda i: (0, first_block + i))]
        args += [cond, w_ada, b_ada.reshape(1, -1)]
    out_specs = [pl.BlockSpec((tm, ATTN_WIDTH), row),
                 pl.BlockSpec((tm, KV_WIDTH), row),
                 pl.BlockSpec((tm, KV_WIDTH), row),
                 pl.BlockSpec((tm, POOL_WIDTH), row)]
    out_shape = [jax.ShapeDtypeStruct((t, ATTN_WIDTH), BF16),
                 jax.ShapeDtypeStruct((t, KV_WIDTH), F32),
                 jax.ShapeDtypeStruct((t, KV_WIDTH), F32),
                 jax.ShapeDtypeStruct((t, POOL_WIDTH), F32)]
    if not rope:
        for _ in range(2):
            out_specs.append(pl.BlockSpec((tm * N_KV_HEADS, HEAD_DIM), row))
            out_shape.append(jax.ShapeDtypeStruct((t * N_KV_HEADS, HEAD_DIM), F32))
        for w in (w_out, w_in):
            out_specs.append(pl.BlockSpec((slab, w.shape[1]), row))
            out_shape.append(jax.ShapeDtypeStruct(w.shape, BF16))
        out_specs.append(pl.BlockSpec((MOD_ROWS, late_cols), lambda i: (0, i)))
        out_shape.append(jax.ShapeDtypeStruct((MOD_ROWS, w_ada.shape[1] - first_col), F32))
    return pl.pallas_call(
        functools.partial(_inproj_kernel, rope=rope),
        grid=(t // tm,),
        in_specs=in_specs,
        out_specs=out_specs,
        out_shape=out_shape,
        compiler_params=pltpu.CompilerParams(
            vmem_limit_bytes=_vmem_limit(2 * D_MODEL * IN_WIDTH * 2 + 24 * tm * D_MODEL * 4)),
        name="in_projection",
    )(*args)


def _softmax_pv(s_list, v_list, sink_col):
    m = sink_col
    for s in s_list:
        m = jnp.maximum(m, jnp.max(s, axis=-1, keepdims=True))
    denom = jnp.exp2(sink_col - m)
    out = None
    for s, v in zip(s_list, v_list):
        e = jnp.exp2(s - m)
        if v.shape[1] == HEAD_DIM:
            denom = denom + jnp.sum(e, axis=-1, keepdims=True)
        o = jnp.dot(e.astype(BF16), v, preferred_element_type=F32)
        out = o if out is None else out + o
    if out.shape[1] > HEAD_DIM:
        denom = denom + out[:, HEAD_DIM:HEAD_DIM + 1]
    return out[:, 0:HEAD_DIM] * (1.0 / denom)


def _with_ones(v):
    return jnp.concatenate([v.astype(BF16), jnp.ones(v.shape, BF16)], axis=1)


def _stack_heads(q, kv):
    return jnp.concatenate(
        [q[:, (kv * Q_PER_KV + g) * HEAD_DIM:(kv * Q_PER_KV + g + 1) * HEAD_DIM]
         for g in range(Q_PER_KV)], axis=0)


def _sink_column(sink_ref, kv, rows):
    r = lax.broadcasted_iota(I32, (Q_PER_KV * rows, 1), 0)
    col = jnp.zeros((Q_PER_KV * rows, 1), F32)
    for g in range(Q_PER_KV):
        col = jnp.where((r >= g * rows) & (r < (g + 1) * rows), sink_ref[kv * Q_PER_KV + g], col)
    return col * LOG2_E


def _qk(q, k):
    return lax.dot_general(q, k, (((1,), (1,)), ((), ())), preferred_element_type=F32)


def _ctx_attn_kernel(sink_ref, q_ref, k_ref, v_ref, o_ref):
    rows = q_ref.shape[0]
    q = q_ref[...]
    for kv in range(N_KV_HEADS):
        kh = k_ref[:, kv * HEAD_DIM:(kv + 1) * HEAD_DIM].astype(BF16)
        vh = v_ref[:, kv * HEAD_DIM:(kv + 1) * HEAD_DIM].astype(BF16)
        qs = _stack_heads(q, kv)
        s = _qk(qs, kh)
        o = _softmax_pv([s], [vh], _sink_column(sink_ref, kv, rows))
        for g in range(Q_PER_KV):
            hd = kv * Q_PER_KV + g
            o_ref[:, hd * HEAD_DIM:(hd + 1) * HEAD_DIM] = o[g * rows:(g + 1) * rows].astype(BF16)


def _context_attention(q, k, v, sink, seq):
    t = q.shape[0]
    row = lambda b: (b, 0)
    return pl.pallas_call(
        _ctx_attn_kernel,
        grid=(t // seq,),
        in_specs=[pl.BlockSpec(memory_space=pltpu.SMEM),
                  pl.BlockSpec((seq, ATTN_WIDTH), row),
                  pl.BlockSpec((seq, KV_WIDTH), row),
                  pl.BlockSpec((seq, KV_WIDTH), row)],
        out_specs=pl.BlockSpec((seq, ATTN_WIDTH), row),
        out_shape=jax.ShapeDtypeStruct((t, ATTN_WIDTH), BF16),
        name="context_attention",
    )(sink, q, k, v)


def _lat_attn_kernel(sink_ref, q_ref, k_ref, v_ref, ck_ref, cv_ref, o_ref, *, n_seq):
    i = pl.program_id(1)
    band = 3 * BLOCK
    start = pl.multiple_of(jnp.clip((i - 1) * BLOCK, 0, n_seq - band), BLOCK)
    rows = Q_PER_KV * BLOCK
    qpos = i * BLOCK + lax.broadcasted_iota(I32, (rows, band), 0) % BLOCK
    kpos = start + lax.broadcasted_iota(I32, (rows, band), 1)
    mask = jnp.abs(kpos - qpos) <= WINDOW
    q = q_ref[...]
    for kv in range(N_KV_HEADS):
        cols = slice(kv * HEAD_DIM, (kv + 1) * HEAD_DIM)
        kb = k_ref[pl.ds(start, band), cols].astype(BF16)
        vb = _with_ones(v_ref[pl.ds(start, band), cols])
        cached = pl.ds(kv, ck_ref.shape[0] // N_KV_HEADS, stride=N_KV_HEADS)
        ck = ck_ref[cached, :].astype(BF16)
        cv = _with_ones(cv_ref[cached, :])
        qs = _stack_heads(q, kv)
        s_loc = jnp.where(mask, _qk(qs, kb), NEG)
        s_ctx = _qk(qs, ck)
        o = _softmax_pv([s_loc, s_ctx], [vb, cv], _sink_column(sink_ref, kv, BLOCK))
        for g in range(Q_PER_KV):
            hd = kv * Q_PER_KV + g
            o_ref[:, hd * HEAD_DIM:(hd + 1) * HEAD_DIM] = o[g * BLOCK:(g + 1) * BLOCK].astype(BF16)


def _latent_attention(q, k, v, ck, cv, sink, n_seq):
    t = q.shape[0]
    nb = n_seq // BLOCK
    cache_rows = ck.shape[0] // (t // n_seq)
    return pl.pallas_call(
        functools.partial(_lat_attn_kernel, n_seq=n_seq),
        grid=(t // n_seq, nb),
        in_specs=[pl.BlockSpec(memory_space=pltpu.SMEM),
                  pl.BlockSpec((BLOCK, ATTN_WIDTH), lambda b, i: (b * nb + i, 0)),
                  pl.BlockSpec((n_seq, KV_WIDTH), lambda b, i: (b, 0)),
                  pl.BlockSpec((n_seq, KV_WIDTH), lambda b, i: (b, 0)),
                  pl.BlockSpec((cache_rows, HEAD_DIM), lambda b, i: (b, 0)),
                  pl.BlockSpec((cache_rows, HEAD_DIM), lambda b, i: (b, 0))],
        out_specs=pl.BlockSpec((BLOCK, ATTN_WIDTH), lambda b, i: (b * nb + i, 0)),
        out_shape=jax.ShapeDtypeStruct((t, ATTN_WIDTH), BF16),
        name="latent_attention",
    )(sink, q, k, v, ck, cv)


def _pool_group(p_ref, r0, seq, w_ref, s_ref, g):
    n = SUB_TILE
    rows = n + 2 * POOL_HALO
    static = isinstance(r0, int)
    t0 = r0 % seq
    t = t0 + lax.broadcasted_iota(I32, (n, 1), 0)
    has_top = t0 > 0
    has_bottom = t0 + n < seq
    zeros = jnp.zeros((POOL_HALO, POOL_GROUP), F32)
    w = POOL_SIZES[g]
    cols = slice(g * POOL_GROUP, (g + 1) * POOL_GROUP)
    pg = p_ref[pl.ds(r0, n), cols]
    if static:
        top = p_ref[r0 - POOL_HALO:r0, cols] if has_top else zeros
        bottom = p_ref[r0 + n:r0 + n + POOL_HALO, cols] if has_bottom else zeros
    else:
        top_row = pl.multiple_of(jnp.maximum(r0 - POOL_HALO, 0), POOL_HALO)
        bottom_row = pl.multiple_of(jnp.minimum(r0 + n, p_ref.shape[0] - POOL_HALO), POOL_HALO)
        top = jnp.where(has_top, p_ref[pl.ds(top_row, POOL_HALO), cols], 0.0)
        bottom = jnp.where(has_bottom, p_ref[pl.ds(bottom_row, POOL_HALO), cols], 0.0)
    x = jnp.concatenate([top, pg, bottom], axis=0)
    fwd = x
    span = 1
    while span < w // 2:
        fwd = fwd + pltpu.roll(fwd, rows - span, 0)
        span *= 2
    if (w // 2) % SUBLANES == 0:
        wsum = fwd[POOL_HALO - w // 2:POOL_HALO - w // 2 + n] + fwd[POOL_HALO:POOL_HALO + n]
    else:
        wsum = (fwd + pltpu.roll(fwd, w // 2, 0))[POOL_HALO:POOL_HALO + n]
    lo = jnp.maximum(t - w // 2, 0)
    hi = jnp.minimum(t + w - w // 2, seq)
    inv_cnt = 1.0 / (hi - lo).astype(F32)
    mixed = wsum * inv_cnt - pg
    y = jnp.dot(mixed.astype(BF16), w_ref[g], preferred_element_type=F32)
    return (y * s_ref[:, cols]).astype(BF16)


def _pack_pair(lo, hi):
    return lax.bitcast_convert_type(pltpu.pack_elementwise([lo, hi], packed_dtype=BF16), I32)


def _unpack_pair(words):
    lo = pltpu.unpack_elementwise(words, index=0, packed_dtype=BF16, unpacked_dtype=F32)
    hi = pltpu.unpack_elementwise(words, index=1, packed_dtype=BF16, unpacked_dtype=F32)
    return lo, hi


def _outproj_kernel(a_ref, p_ref, x_ref, mod_ref, g_ref, wo_ref, wr_ref, pw_ref, ps_ref,
                    x1_ref, h_ref, aff_ref, *, seq):
    mod = mod_ref[0]
    groups = SUB_TILE // SUBLANES
    half = D_MODEL // 2
    tm = x_ref.shape[0]
    steps_per_p_block = p_ref.shape[0] // tm
    for s in range(tm // SUB_TILE):
        rows = slice(s * SUB_TILE, (s + 1) * SUB_TILE)
        grp = slice(s * groups, (s + 1) * groups)
        r0 = s * SUB_TILE
        if steps_per_p_block > 1:
            r0 = pl.multiple_of((pl.program_id(0) % steps_per_p_block) * tm + r0, SUB_TILE)
        pooled = jnp.concatenate([_pool_group(p_ref, r0, seq, pw_ref, ps_ref, g)
                                  for g in range(len(POOL_SIZES))], axis=1)
        mix = (jnp.dot(a_ref[rows, :], wo_ref[0:ATTN_WIDTH, :], preferred_element_type=F32)
               + jnp.dot(pooled, wo_ref[ATTN_WIDTH:D_MODEL, :], preferred_element_type=F32))
        x1 = x_ref[rows, :] + mod[2:3] * mix
        x1_ref[rows, :] = x1
        h = _norm_mod(x1, g_ref[...], mod[3:4], mod[4:5])
        logits = jnp.dot(h.astype(BF16), wr_ref[...], preferred_element_type=F32)
        lane = lax.broadcasted_iota(I32, logits.shape, 1)
        logits = jnp.where(lane < N_EXPERTS, logits, -jnp.inf)
        m = jnp.max(logits, axis=-1, keepdims=True)
        e = jnp.exp(logits - m)
        aff = e / jnp.sum(e, axis=-1, keepdims=True)
        aff_ref[rows, :] = aff[:, 0:N_EXPERTS]
        for c in range(HQ_TILES - 1):
            words = _pack_pair(h[:, c * LANES:(c + 1) * LANES],
                               h[:, half + c * LANES:half + (c + 1) * LANES])
            h_ref[grp, c * SUBLANES:(c + 1) * SUBLANES, :] = (
                words.reshape(groups, SUBLANES, LANES))
        h_ref[grp, (HQ_TILES - 1) * SUBLANES:, :] = (
            lax.bitcast_convert_type(aff, I32).reshape(groups, SUBLANES, LANES))


def _out_projection(attn, p, x2, mod3, mod_row, g, w_out_bf, w_router, pool_w, pool_scale, seq):
    t = x2.shape[0]
    tm = PROJ_TILE
    row = lambda i: (i, 0)
    p_rows = max(tm, seq)
    steps_per_p_block = p_rows // tm
    return pl.pallas_call(
        functools.partial(_outproj_kernel, seq=seq),
        grid=(t // tm,),
        in_specs=[pl.BlockSpec((tm, ATTN_WIDTH), row),
                  pl.BlockSpec((p_rows, POOL_WIDTH), lambda i: (i // steps_per_p_block, 0)),
                  pl.BlockSpec((tm, D_MODEL), row),
                  pl.BlockSpec((1, N_MOD, D_MODEL), lambda i: (mod_row(i), 0, 0)),
                  pl.BlockSpec((1, D_MODEL), lambda i: (0, 0)),
                  pl.BlockSpec((D_MODEL, D_MODEL), lambda i: (0, 0)),
                  pl.BlockSpec((D_MODEL, LANES), lambda i: (0, 0)),
                  pl.BlockSpec((len(POOL_SIZES), POOL_GROUP, POOL_GROUP), lambda i: (0, 0, 0)),
                  pl.BlockSpec((1, POOL_WIDTH), lambda i: (0, 0))],
        out_specs=[pl.BlockSpec((tm, D_MODEL), row),
                   pl.BlockSpec((tm // SUBLANES, HQ_TILES * SUBLANES, LANES), lambda i: (i, 0, 0)),
                   pl.BlockSpec((tm, N_EXPERTS), row)],
        out_shape=[jax.ShapeDtypeStruct((t, D_MODEL), F32),
                   jax.ShapeDtypeStruct((t // SUBLANES, HQ_TILES * SUBLANES, LANES), I32),
                   jax.ShapeDtypeStruct((t, N_EXPERTS), F32)],
        compiler_params=pltpu.CompilerParams(
            vmem_limit_bytes=_vmem_limit(2 * D_MODEL * D_MODEL * 2 + 24 * tm * D_MODEL * 4)),
        name="out_projection",
    )(attn, p, x2, mod3, g, w_out_bf, w_router, pool_w, pool_scale)


def _route_kernel(a_ref, idx_ref, slot_ref, off_ref, slot_scr, *, cap):
    a = a_ref[...]
    n_e, n_c, _ = a.shape
    rows = n_e * n_c

    def enough(cand):
        cand_f = lax.bitcast_convert_type(cand, F32)
        cnt = jnp.sum(jnp.sum((a >= cand_f).astype(F32), axis=1, keepdims=True),
                      axis=2, keepdims=True)
        return cnt >= cap

    def two_bits(it, thr):
        low = 28 - 2 * it
        for setting in (1, 2, 3):
            cand = thr | jnp.left_shift(jnp.int32(setting), low)
            best = jnp.where(enough(cand), cand, thr if setting == 1 else best)
        return best

    top = jnp.full((n_e, 1, 1), 1 << 30, I32)
    thr = jnp.where(enough(top), top, 0)
    thr = lax.fori_loop(0, 15, two_bits, thr)
    thr_f = lax.bitcast_convert_type(thr, F32)
    gt = (a > thr_f).astype(F32).reshape(rows, LANES)
    eq = (a == thr_f).astype(F32).reshape(rows, LANES)

    li = lax.broadcasted_iota(I32, (LANES, LANES), 0)
    lj = lax.broadcasted_iota(I32, (LANES, LANES), 1)
    upper_incl = (li <= lj).astype(BF16)
    ci = lax.broadcasted_iota(I32, (n_c, n_c), 0)
    cj = lax.broadcasted_iota(I32, (n_c, n_c), 1)
    before = (cj < ci).astype(BF16)
    whole = jnp.ones((n_c, n_c), BF16)

    def over_chunks(mat, col):
        wide = jnp.broadcast_to(col, (rows, LANES)).astype(BF16)
        side_by_side = jnp.concatenate([wide[e * n_c:(e + 1) * n_c] for e in range(n_e)], axis=1)
        res = jnp.dot(mat, side_by_side, preferred_element_type=F32)
        return jnp.concatenate([res[:, e * LANES:(e + 1) * LANES] for e in range(n_e)], axis=0)

    def prefix(x):
        incl = jnp.dot(x.astype(BF16), upper_incl, preferred_element_type=F32)
        tot = incl[:, LANES - 1:LANES]
        return incl, tot, over_chunks(before, tot)

    n_gt = over_chunks(whole, jnp.sum(gt, axis=1, keepdims=True))
    need = cap - n_gt
    incl_eq, _, off_eq = prefix(eq)
    rank_eq = off_eq + incl_eq - eq
    sel = jnp.where((eq > 0) & (rank_eq < need), 1.0, gt)
    incl, tot, off = prefix(sel)
    slot = off + incl - sel
    slot_scr[...] = jnp.where(sel > 0, slot, -1.0)
    for c in range(n_c):
        per_expert = slot_scr[pl.ds(c, n_e, stride=n_c), :]
        slot_ref[c * LANES:(c + 1) * LANES, :] = per_expert.T.astype(I32)
    off_ref[...] = off.astype(I32).reshape(n_e, n_c, LANES)

    s_lane = lax.broadcasted_iota(I32, (1, cap), 1).astype(F32)
    c_col = lax.broadcasted_iota(I32, (n_c, 1), 0).astype(F32)
    for e in range(n_e):
        r0 = e * n_c
        incl_e = incl[r0:r0 + n_c]
        off_e = off[r0:r0 + n_c, 0:1]
        tot_e = tot[r0:r0 + n_c]
        onehot = ((off_e <= s_lane) & (s_lane < off_e + tot_e)).astype(F32)
        counts = lax.dot_general(incl_e.astype(BF16), onehot.astype(BF16),
                                 (((0,), (0,)), ((), ())), preferred_element_type=F32)
        local = s_lane - jnp.sum(onehot * off_e, axis=0, keepdims=True)
        lane = jnp.sum((counts <= local).astype(F32), axis=0, keepdims=True)
        chunk = jnp.sum(onehot * c_col, axis=0, keepdims=True)
        idx_ref[e] = (chunk * LANES + lane).astype(I32)


def _routing(aff, cap):
    t = aff.shape[0]
    n_c = t // LANES
    a3 = aff.T.reshape(N_EXPERTS, n_c, LANES)
    return pl.pallas_call(
        functools.partial(_route_kernel, cap=cap),
        out_shape=[jax.ShapeDtypeStruct((N_EXPERTS, 1, cap), I32),
                   jax.ShapeDtypeStruct((t, N_EXPERTS), I32),
                   jax.ShapeDtypeStruct((N_EXPERTS, n_c, LANES), I32)],
        scratch_shapes=[pltpu.VMEM((N_EXPERTS * n_c, LANES), F32)],
        compiler_params=pltpu.CompilerParams(vmem_limit_bytes=_vmem_limit(48 << 20)),
        name="routing",
    )(a3)


def _gather_rows(table, row_ids):
    n_chunks = row_ids.shape[0]
    n_workers = SC_CORES * SC_SUBCORES
    per_worker = n_chunks // n_workers
    assert row_ids.shape[1] == GATHER_CHUNK and n_chunks % n_workers == 0
    mesh = plsc.VectorSubcoreMesh(core_axis_name="core", subcore_axis_name="subcore")

    @functools.partial(
        pl.kernel, mesh=mesh,
        out_type=jax.ShapeDtypeStruct((n_chunks * GATHER_CHUNK, LANES), I32),
        scratch_types=[pltpu.VMEM((per_worker, GATHER_CHUNK), I32),
                       pltpu.VMEM((2, GATHER_CHUNK, LANES), I32),
                       pltpu.SemaphoreType.DMA((2,)),
                       pltpu.SemaphoreType.DMA((2,))],
        name="gather_rows",
    )
    def gather(table_hbm, ids_hbm, out_hbm, ids_v, rows_v, gather_sem, store_sem):
        worker = lax.axis_index("subcore") * SC_CORES + lax.axis_index("core")
        first = worker * per_worker
        pltpu.sync_copy(ids_hbm.at[worker], ids_v)

        def fetch(j):
            return pltpu.make_async_copy(table_hbm.at[ids_v.at[j]], rows_v.at[j % 2],
                                         gather_sem.at[j % 2])

        def store(j):
            rows = pl.ds(pl.multiple_of((first + j) * GATHER_CHUNK, GATHER_CHUNK), GATHER_CHUNK)
            return pltpu.make_async_copy(rows_v.at[j % 2], out_hbm.at[rows], store_sem.at[j % 2])

        for j in range(per_worker):
            if j >= 2:
                store(j - 2).wait()
            fetch(j).start()
            if j >= 1:
                fetch(j - 1).wait()
                store(j - 1).start()
        fetch(per_worker - 1).wait()
        store(per_worker - 1).start()
        for j in range(max(per_worker - 2, 0), per_worker):
            store(j).wait()

    return gather(table, row_ids.reshape(n_workers, per_worker, GATHER_CHUNK))


def _packed_row_ids(idx, cap):
    tok = idx.reshape(-1, 1, cap // GATHER_CHUNK, GATHER_CHUNK)
    tile = jnp.arange(HQ_TILES, dtype=I32).reshape(1, HQ_TILES, 1, 1)
    ids = ((tok // SUBLANES) * HQ_TILES + tile) * SUBLANES + tok % SUBLANES
    return ids.reshape(-1, GATHER_CHUNK)


def _ffn_kernel(*refs, caps, n_f, row_chunk, first_expert, n_prior):
    n_g = len(caps)
    x_refs = refs[:n_g]
    wg_ref, wu_ref, wd_ref = refs[n_g:n_g + 3]
    y_refs = refs[n_g + 3 + n_prior:2 * n_g + 3 + n_prior]
    acc_ref = refs[2 * n_g + 3 + n_prior]
    e = pl.program_id(0)
    f = pl.program_id(1)

    @pl.when((e == 0) & (f == 0))
    def _():
        acc_ref[...] = jnp.zeros_like(acc_ref)

    def step(last):
        first = f == 0
        base = 0
        for x_ref, y_ref, cap in zip(x_refs, y_refs, caps):
            if last:
                aff = lax.bitcast_convert_type(x_ref[0, HQ_TILES - 1], F32)
                lane = lax.broadcasted_iota(I32, aff.shape, 1)
                gates = jnp.sum(jnp.where(lane == e + first_expert, aff, 0.0), axis=1,
                                keepdims=True)
                y_ref[0, cap:, :] = jnp.zeros((COMBINE_WINDOW, D_MODEL // 2), I32)
            for r in range(cap // row_chunk):
                rs = slice(r * row_chunk, (r + 1) * row_chunk)
                acc_rows = slice(base + r * row_chunk, base + (r + 1) * row_chunk)
                pairs = [_unpack_pair(x_ref[0, c, rs, :]) for c in range(HQ_TILES - 1)]
                x = jnp.concatenate([p[0] for p in pairs] + [p[1] for p in pairs], axis=1)
                gate_act = jnp.dot(x, wg_ref[0], preferred_element_type=F32)
                up = jnp.dot(x, wu_ref[0], preferred_element_type=F32)
                hid = (gate_act * jax.nn.sigmoid(gate_act)) * up
                part = jnp.dot(hid, wd_ref[0], preferred_element_type=F32)
                if last:
                    y = (part + acc_ref[acc_rows, :]) * gates[rs]
                    y_ref[0, rs, :] = _pack_pair(y[:, 0:D_MODEL // 2], y[:, D_MODEL // 2:])
                else:
                    acc_ref[acc_rows, :] = part + jnp.where(first, 0.0, acc_ref[acc_rows, :])
            base += cap

    @pl.when(f < n_f - 1)
    def _():
        step(False)

    @pl.when(f == n_f - 1)
    def _():
        step(True)


def _expert_ffn(xs_groups, w_gate, w_up, w_down, first_expert, prior_outputs):
    n_f = 4
    tf = D_EXPERT // n_f
    row_chunk = 512
    n_e = xs_groups[0].shape[0]
    e0 = first_expert
    caps = tuple(xs.shape[2] for xs in xs_groups)
    rows = sum(caps)
    est = (2 * HQ_TILES * rows * LANES * 4 + rows * D_MODEL * (4 + 2 * 2) + 2 * 3 * D_MODEL * tf * 4
           + row_chunk * (3 * tf + 2 * D_MODEL) * 4)
    x_specs = [pl.BlockSpec((1, HQ_TILES, cap, LANES), lambda e, f: (e, 0, 0, 0)) for cap in caps]
    y_shapes = [(N_EXPERTS, cap + COMBINE_WINDOW, D_MODEL // 2) for cap in caps]
    y_specs = [pl.BlockSpec((1,) + s[1:], lambda e, f: (e + e0, 0, 0)) for s in y_shapes]
    prior = list(prior_outputs or [])
    n_in = len(caps) + 3
    return pl.pallas_call(
        functools.partial(_ffn_kernel, caps=caps, n_f=n_f, row_chunk=row_chunk,
                          first_expert=e0, n_prior=len(prior)),
        grid=(n_e, n_f),
        in_specs=(x_specs + [pl.BlockSpec((1, D_MODEL, tf), lambda e, f: (e + e0, 0, f)),
                             pl.BlockSpec((1, D_MODEL, tf), lambda e, f: (e + e0, 0, f)),
                             pl.BlockSpec((1, tf, D_MODEL), lambda e, f: (e + e0, f, 0))]
                  + [pl.BlockSpec(memory_space=pl.ANY) for _ in prior]),
        out_specs=y_specs,
        out_shape=[jax.ShapeDtypeStruct(s, I32) for s in y_shapes],
        input_output_aliases={n_in + j: j for j in range(len(prior))},
        scratch_shapes=[pltpu.VMEM((rows, D_MODEL), F32)],
        compiler_params=pltpu.CompilerParams(
            dimension_semantics=("arbitrary", "arbitrary"),
            vmem_limit_bytes=_vmem_limit(est + (4 << 20))),
        name="expert_ffn",
    )(*xs_groups, w_gate, w_up, w_down, *prior)


def _combine_kernel(start_ref, nch_ref, wide_ref, x1_ref, mod_ref, g_ref, slot_ref, y_hbm, o_ref,
                    ybuf_ref, ffn_ref, sem, *, first_k, kblock):
    i = pl.program_id(0)
    tm = x1_ref.shape[0]
    cur = i % 2
    n_window_rows = N_EXPERTS * COMBINE_WINDOW

    window_head = COMBINE_WINDOW - COMBINE_CHUNK

    def window_copy(e, src_row, buf):
        return pltpu.make_async_copy(
            y_hbm.at[e, pl.ds(pl.multiple_of(src_row, SUBLANES), window_head)],
            ybuf_ref.at[buf, e * COMBINE_WINDOW:e * COMBINE_WINDOW + window_head], sem.at[buf])

    def window_tail_copy(e, src_row, buf):
        return chunk_copy(e, src_row + window_head, buf,
                          jnp.int32(e * COMBINE_WINDOW + window_head))

    def needs_tail(tile, e):
        return nch_ref[tile, e] * COMBINE_CHUNK > window_head

    def chunk_copy(e, src_row, buf, dst_row):
        return pltpu.make_async_copy(
            y_hbm.at[e, pl.ds(pl.multiple_of(src_row, SUBLANES), COMBINE_CHUNK)],
            ybuf_ref.at[buf, pl.ds(pl.multiple_of(dst_row, COMBINE_CHUNK), COMBINE_CHUNK)],
            sem.at[buf])

    def fetch(tile, buf):
        @pl.when(wide_ref[tile] == 0)
        def _():
            for e in range(N_EXPERTS):
                window_copy(e, start_ref[tile, e], buf).start()

                @pl.when(needs_tail(tile, e))
                def _(e=e):
                    window_tail_copy(e, start_ref[tile, e], buf).start()

        @pl.when(wide_ref[tile] != 0)
        def _():
            pos = jnp.int32(0)
            for e in range(N_EXPERTS):
                start = start_ref[tile, e]
                nch = nch_ref[tile, e]

                def issue(c, carry, e=e, start=start, pos=pos):
                    chunk_copy(e, start + c * COMBINE_CHUNK, buf, pos + c * COMBINE_CHUNK).start()
                    return carry
                lax.fori_loop(0, nch, issue, 0)
                pos = pos + nch * COMBINE_CHUNK

    @pl.when(i == 0)
    def _():
        ybuf_ref[...] = jnp.zeros_like(ybuf_ref)
        fetch(0, 0)

    @pl.when(i + 1 < pl.num_programs(0))
    def _():
        fetch(i + 1, 1 - cur)

    def finish(ffn):
        out = x1_ref[...] + mod_ref[0][5:6] * ffn
        ms = jnp.mean(out * out, axis=-1, keepdims=True)
        o_ref[...] = (out * lax.rsqrt(ms + EPS)) * g_ref[...]

    def apply_selection(sel, k0, width):
        lo, hi = _unpack_pair(ybuf_ref[cur, pl.ds(k0, width), :])
        return jnp.concatenate([jnp.dot(sel, lo, preferred_element_type=F32),
                                jnp.dot(sel, hi, preferred_element_type=F32)], axis=1)

    @pl.when(wide_ref[i] == 0)
    def _():
        for e in range(N_EXPERTS):
            window_copy(e, jnp.int32(0), cur).wait()

            @pl.when(needs_tail(i, e))
            def _(e=e):
                window_tail_copy(e, jnp.int32(0), cur).wait()
        expert_lane = lax.broadcasted_iota(I32, (1, N_EXPERTS), 1)
        starts = jnp.zeros((1, N_EXPERTS), I32)
        for e in range(N_EXPERTS):
            starts = jnp.where(expert_lane == e, start_ref[i, e], starts)
        slots = slot_ref[...]
        rows_in_window = jnp.where(slots >= 0, (slots - starts).astype(F32), -1.0)
        window_of_lane = lax.broadcasted_iota(I32, (N_EXPERTS, n_window_rows), 1) // COMBINE_WINDOW
        spread = (window_of_lane == lax.broadcasted_iota(I32, (N_EXPERTS, n_window_rows), 0))
        target = jnp.dot(rows_in_window.astype(BF16), spread.astype(BF16),
                         preferred_element_type=F32)
        lane_row = (lax.broadcasted_iota(I32, (tm, n_window_rows), 1) % COMBINE_WINDOW).astype(F32)
        sel = jnp.where(target == lane_row, 1.0, 0.0)
        ffn = None
        for k0 in range(0, n_window_rows, kblock):
            part = apply_selection(sel[:, k0:k0 + kblock], k0, kblock)
            ffn = part if ffn is None else ffn + part
        finish(ffn)

    @pl.when(wide_ref[i] != 0)
    def _():
        pos = jnp.int32(0)
        begins = []
        for e in range(N_EXPERTS):
            begins.append(pos)
            pos = pos + nch_ref[i, e] * COMBINE_CHUNK
        begins.append(pos)

        def drain(c, carry):
            chunk_copy(0, jnp.int32(0), cur, jnp.int32(0)).wait()
            return carry
        lax.fori_loop(0, pos // COMBINE_CHUNK, drain, 0)

        expert_lane = lax.broadcasted_iota(I32, (1, N_EXPERTS), 1)
        expert_row = lax.broadcasted_iota(I32, (N_EXPERTS, 1), 0)
        shift = jnp.zeros((1, N_EXPERTS), I32)
        range_lo = jnp.zeros((N_EXPERTS, 1), I32)
        range_hi = jnp.zeros((N_EXPERTS, 1), I32)
        for e in range(N_EXPERTS):
            shift = jnp.where(expert_lane == e, begins[e] - start_ref[i, e], shift)
            range_lo = jnp.where(expert_row == e, begins[e], range_lo)
            range_hi = jnp.where(expert_row == e, begins[e + 1], range_hi)
        slots = slot_ref[...]
        packed_row = slots + shift
        row_hi = jnp.where(slots >= 0, packed_row // ROW_SPLIT, -1).astype(F32).astype(BF16)
        row_lo = jnp.where(slots >= 0, packed_row % ROW_SPLIT, 0).astype(F32).astype(BF16)

        def selection(k0, width):
            col = k0 + lax.broadcasted_iota(I32, (N_EXPERTS, width), 1)
            spread = ((col >= range_lo) & (col < range_hi)).astype(BF16)
            want_hi = jnp.dot(row_hi, spread, preferred_element_type=F32)
            want_lo = jnp.dot(row_lo, spread, preferred_element_type=F32)
            here = k0 + lax.broadcasted_iota(I32, (1, width), 1)
            hit = ((want_hi == (here // ROW_SPLIT).astype(F32))
                   & (want_lo == (here % ROW_SPLIT).astype(F32)) & (here < pos))
            return jnp.where(hit, 1.0, 0.0)

        ffn_ref[...] = apply_selection(selection(0, first_k), 0, first_k)

        def kstep(kb, carry):
            k0 = pl.multiple_of(kb * kblock, kblock)
            ffn_ref[...] += apply_selection(selection(k0, kblock), k0, kblock)
            return carry
        lax.fori_loop(first_k // kblock, (pos + kblock - 1) // kblock, kstep, 0)
        finish(ffn_ref[...])


def _combine(x1, mod3, mod_row, g_final, slot_t, tile_start, tile_nch, tile_wide, y):
    t = x1.shape[0]
    tm = TOKEN_TILE
    first_k = 2 * tm + N_EXPERTS * COMBINE_CHUNK
    kblock = 256
    max_rows = N_EXPERTS * (tm + 2 * COMBINE_CHUNK)
    max_rows = -(-max_rows // kblock) * kblock
    row = lambda i, *_: (i, 0)
    grid_spec = pltpu.PrefetchScalarGridSpec(
        num_scalar_prefetch=3,
        grid=(t // tm,),
        in_specs=[pl.BlockSpec((tm, D_MODEL), row),
                  pl.BlockSpec((1, N_MOD, D_MODEL), lambda i, *_: (mod_row(i), 0, 0)),
                  pl.BlockSpec((1, D_MODEL), lambda i, *_: (0, 0)),
                  pl.BlockSpec((tm, N_EXPERTS), row),
                  pl.BlockSpec(memory_space=pl.ANY)],
        out_specs=pl.BlockSpec((tm, D_MODEL), row),
        scratch_shapes=[pltpu.VMEM((2, max_rows, D_MODEL // 2), I32),
                        pltpu.VMEM((tm, D_MODEL), F32),
                        pltpu.SemaphoreType.DMA((2,))],
    )
    return pl.pallas_call(
        functools.partial(_combine_kernel, first_k=first_k, kblock=kblock),
        grid_spec=grid_spec,
        out_shape=jax.ShapeDtypeStruct((t, D_MODEL), F32),
        compiler_params=pltpu.CompilerParams(
            dimension_semantics=("arbitrary",),
            vmem_limit_bytes=_vmem_limit(2 * max_rows * D_MODEL * 2 + 16 * tm * D_MODEL * 4)),
        name="combine",
    )(tile_start, tile_nch, tile_wide, x1, mod3, g_final, slot_t, y)


def _rope_tables(n):
    f32 = np.float32
    rows = n // GRID_W
    row = np.repeat(np.arange(rows, dtype=f32), GRID_W)
    col = np.tile(np.arange(GRID_W, dtype=f32), rows)
    inv = (f32(ROPE_THETA) ** (-np.arange(ROPE_FREQS, dtype=f32) / f32(ROPE_FREQS))).astype(f32)
    ang_r = row[:, None] * inv
    ang_c = col[:, None] * inv
    zero = np.zeros_like(ang_r)
    cos = np.concatenate([np.cos(ang_r)] * 2 + [np.cos(ang_c)] * 2, axis=1)
    sin_a = np.concatenate([-np.sin(ang_r), zero, -np.sin(ang_c), zero], axis=1)
    sin_b = np.concatenate([zero, np.sin(ang_r), zero, np.sin(ang_c)], axis=1)
    return tuple(jnp.asarray(t, F32) for t in (cos, sin_a, sin_b))


def _token_group(x, mod_mix, mod3, mod_row, seq, weights, rope_tabs, ctx_kv, late_mod=None):
    (norm_mix, w_in, sink, pool_w, pool_scale, w_out, norm_ffn, w_router,
     w_gate, w_up, w_down, norm_final) = weights
    b = x.shape[0]
    t = b * seq
    x2 = x.reshape(t, D_MODEL)
    q, k, v, p, *state = _in_projection(x2, mod_mix, mod_row(PROJ_TILE), norm_mix, w_in,
                                        rope_tabs, w_out, late_mod)
    if rope_tabs is None:
        late = state.pop()
        mod3 = jnp.concatenate([mod_mix.reshape(MOD_ROWS, -1), late], axis=1).reshape(
            MOD_ROWS, N_MOD, D_MODEL)
    w_in_bf = w_in if rope_tabs is not None else state.pop()
    w_out_bf = w_out if rope_tabs is not None else state.pop()
    if ctx_kv is None:
        attn = _context_attention(q, k, v, sink, seq)
    else:
        attn = _latent_attention(q, k, v, ctx_kv[0], ctx_kv[1], sink, seq)
    x1, h_packed, aff = _out_projection(attn, p, x2, mod3, mod_row(PROJ_TILE), norm_ffn,
                                        w_out_bf, w_router, pool_w, pool_scale, seq)

    cap = EC_FACTOR * t // N_EXPERTS
    idx, slot_t, off3 = _routing(aff, cap)
    table = h_packed.reshape(-1, LANES)
    per_range = N_EXPERTS // FFN_RANGES
    xs = []
    for r in range(FFN_RANGES):
        ids = _packed_row_ids(idx[r * per_range:(r + 1) * per_range], cap)
        xs.append(_gather_rows(table, ids).reshape(per_range, HQ_TILES, cap, LANES))

    chunks_per_tile = TOKEN_TILE // LANES
    tile_off = off3[:, ::chunks_per_tile, 0]
    tile_end = jnp.concatenate([tile_off[:, 1:], jnp.full((N_EXPERTS, 1), cap, I32)], axis=1)
    tile_start = (tile_off // SUBLANES) * SUBLANES
    tile_nch = jnp.where(tile_end > tile_off,
                         (tile_end - tile_start + COMBINE_CHUNK - 1) // COMBINE_CHUNK, 0)
    tile_wide = jnp.any(tile_end - tile_start > COMBINE_WINDOW, axis=0).astype(I32)

    def finish(y):
        out = _combine(x1, mod3, mod_row(TOKEN_TILE), norm_final, slot_t, tile_start.T,
                       tile_nch.T, tile_wide, y)
        return out.reshape(b, seq, D_MODEL)
    return xs, finish, state, (w_in_bf, w_out_bf, mod3)


def kernel(x_prompt, x_sample, c, cache_k, cache_v, c_ctx, w_ada, b_ada, norm_mix, w_in,
           sink_logits, pool_w, pool_scale, w_out, norm_ffn, w_router, w_gate, w_up, w_down,
           norm_final):
    n_b, seq, _ = x_prompt.shape
    n_db, n_lat, _ = x_sample.shape
    assert 1 + n_db <= MOD_ROWS and seq == TOKEN_TILE and n_lat % PROJ_TILE == 0
    assert w_in.shape[0] == 1 and cache_k.shape[1] == 1
    assert (n_b * seq) % PROJ_TILE == 0

    cond = jnp.concatenate(
        [c_ctx[None, :], c, jnp.zeros((MOD_ROWS - 1 - n_db, D_MODEL), F32)], axis=0)
    mixer_cols = 2 * D_MODEL
    mod_mix = _modulation(cond, w_ada[0], b_ada[0], mixer_cols).reshape(MOD_ROWS, 2, D_MODEL)

    w_router_bf = jnp.pad(w_router[0], ((0, 0), (0, LANES - N_EXPERTS))).astype(BF16)
    weights = (norm_mix[0][None, :], w_in[0], sink_logits[0], pool_w[0].astype(BF16),
               pool_scale[0][None, :], w_out[0], norm_ffn[0][None, :], w_router_bf,
               w_gate.reshape(w_gate.shape[1:]), w_up.reshape(w_up.shape[1:]),
               w_down.reshape(w_down.shape[1:]), norm_final[None, :])

    xs_p, finish_p, (k_p, v_p), (w_in_bf, w_out_bf, mod3) = _token_group(
        x_prompt, mod_mix, None, lambda tile: (lambda i: 0), seq, weights, None, None,
        late_mod=(cond, w_ada[0], b_ada[0], mixer_cols))
    weights = weights[:1] + (w_in_bf,) + weights[2:5] + (w_out_bf,) + weights[6:]

    ck = cache_k.reshape(-1, HEAD_DIM)
    cv = cache_v.reshape(-1, HEAD_DIM)
    xs_l, finish_l, _, _ = _token_group(
        x_sample, mod_mix, mod3, lambda tile: (lambda i: 1 + i // (n_lat // tile)), n_lat, weights,
        _rope_tables(n_lat), (ck, cv))

    ys = None
    for r in range(FFN_RANGES):
        ys = _expert_ffn([xs_p[r], xs_l[r]], weights[8], weights[9], weights[10],
                         r * (N_EXPERTS // FFN_RANGES), ys)
    y_prompt = finish_p(ys[0])
    y_sample = finish_l(ys[1])

    state_k = k_p.reshape(n_b, 1, seq, N_KV_HEADS, HEAD_DIM)
    state_v = v_p.reshape(n_b, 1, seq, N_KV_HEADS, HEAD_DIM)
    return (y_prompt, y_sample, state_k, state_v)
```

```python
import functools

import jax
import jax.numpy as jnp
import numpy as np
from jax import lax
from jax.experimental import pallas as pl
from jax.experimental.pallas import tpu as pltpu
from jax.experimental.pallas import tpu_sc as plsc

F32 = jnp.float32
BF16 = jnp.bfloat16
I32 = jnp.int32

D_MODEL = 2048
N_HEADS = 8
N_KV_HEADS = 2
HEAD_DIM = 128
Q_PER_KV = N_HEADS // N_KV_HEADS
ATTN_WIDTH = N_HEADS * HEAD_DIM
KV_WIDTH = N_KV_HEADS * HEAD_DIM
POOL_WIDTH = D_MODEL - ATTN_WIDTH
POOL_SIZES = (2, 4, 8, 16)
POOL_GROUP = POOL_WIDTH // len(POOL_SIZES)
IN_WIDTH = ATTN_WIDTH + 2 * KV_WIDTH + POOL_WIDTH
WINDOW = 128
BLOCK = 128
GRID_W = 64
ROPE_THETA = 10000.0
ROPE_FREQS = HEAD_DIM // 4
N_EXPERTS = 16
EC_FACTOR = 2
D_EXPERT = 1024
N_MOD = 6
EPS = 1e-6
NEG = -1e30
LOG2_E = 1.4426950408889634
ATTN_SCALE = HEAD_DIM ** -0.5 * LOG2_E

LANES = 128
SUBLANES = 8
BF16_ROWS = 16
VMEM_CAP = 64 * 1024 * 1024
SC_CORES = 2
SC_SUBCORES = 16

MOD_ROWS = 8
TOKEN_TILE = 256
PROJ_TILE = 512
SUB_TILE = 256
HQ_TILES = D_MODEL // 2 // LANES + 1
GATHER_CHUNK = 128
FFN_RANGES = 2
POOL_HALO = 8
COMBINE_CHUNK = BF16_ROWS
COMBINE_WINDOW = 64
ROW_SPLIT = 32


def _vmem_limit(nbytes):
    return int(min(VMEM_CAP - (4 << 20), max(nbytes, 16 << 20)))


def _mod_block(c_ref, w_ref, b_ref):
    c = c_ref[...]
    s = c * jax.nn.sigmoid(c)
    return jnp.dot(s.astype(BF16), w_ref[...].astype(BF16),
                   preferred_element_type=F32) + b_ref[...]


def _mod_kernel(c_ref, w_ref, b_ref, o_ref):
    o_ref[...] = _mod_block(c_ref, w_ref, b_ref)


def _modulation(cond, w_ada, b_ada, n):
    tn = 1024
    return pl.pallas_call(
        _mod_kernel,
        grid=(n // tn,),
        in_specs=[pl.BlockSpec((MOD_ROWS, D_MODEL), lambda j: (0, 0)),
                  pl.BlockSpec((D_MODEL, tn), lambda j: (0, j)),
                  pl.BlockSpec((1, tn), lambda j: (0, j))],
        out_specs=pl.BlockSpec((MOD_ROWS, tn), lambda j: (0, j)),
        out_shape=jax.ShapeDtypeStruct((MOD_ROWS, n), F32),
        compiler_params=pltpu.CompilerParams(
            vmem_limit_bytes=_vmem_limit(3 * D_MODEL * tn * 4)),
        name="modulation",
    )(cond, w_ada, b_ada.reshape(1, -1))


def _norm_mod(x, g, shift, scale):
    ms = jnp.mean(x * x, axis=-1, keepdims=True)
    y = x * lax.rsqrt(ms + EPS)
    return (y * g) * (1.0 + scale) + shift


def _inproj_kernel(*refs, rope):
    if rope:
        x_ref, mod_ref, g_ref, w_ref, cos_ref, sa_ref, sb_ref, q_ref, k_ref, v_ref, p_ref = refs
    else:
        (x_ref, mod_ref, g_ref, w_ref, wo_slab_ref, wi_slab_ref, cond_ref, wada_ref, bada_ref,
         q_ref, k_ref, v_ref, p_ref, ks_ref, vs_ref, wo_bf_ref, wi_bf_ref, late_mod_ref) = refs
        wo_bf_ref[...] = wo_slab_ref[...].astype(BF16)
        wi_bf_ref[...] = wi_slab_ref[...].astype(BF16)
        late_mod_ref[...] = _mod_block(cond_ref, wada_ref, bada_ref)
    mod = mod_ref[0]
    for s in range(x_ref.shape[0] // SUB_TILE):
        rows = slice(s * SUB_TILE, (s + 1) * SUB_TILE)
        h = _norm_mod(x_ref[rows, :], g_ref[...], mod[0:1], mod[1:2])
        u = jnp.dot(h.astype(w_ref.dtype), w_ref[...], preferred_element_type=F32)

        def rot(xh, rows=rows):
            return (xh * cos_ref[rows, :] + pltpu.roll(xh, LANES - ROPE_FREQS, 1) * sa_ref[rows, :]
                    + pltpu.roll(xh, ROPE_FREQS, 1) * sb_ref[rows, :])

        for hd in range(N_HEADS):
            xh = u[:, hd * HEAD_DIM:(hd + 1) * HEAD_DIM] * ATTN_SCALE
            q_ref[rows, hd * HEAD_DIM:(hd + 1) * HEAD_DIM] = (rot(xh) if rope else xh).astype(BF16)
        for hd in range(N_KV_HEADS):
            lo = ATTN_WIDTH + hd * HEAD_DIM
            xh = u[:, lo:lo + HEAD_DIM]
            k_ref[rows, hd * HEAD_DIM:(hd + 1) * HEAD_DIM] = rot(xh) if rope else xh
        v_ref[rows, :] = u[:, ATTN_WIDTH + KV_WIDTH:ATTN_WIDTH + 2 * KV_WIDTH]
        p_ref[rows, :] = u[:, ATTN_WIDTH + 2 * KV_WIDTH:]
        if not rope:
            for hd in range(N_KV_HEADS):
                state_rows = pl.ds(s * SUB_TILE * N_KV_HEADS + hd, SUB_TILE, stride=N_KV_HEADS)
                lo = ATTN_WIDTH + hd * HEAD_DIM
                ks_ref[state_rows, :] = u[:, lo:lo + HEAD_DIM]
                vs_ref[state_rows, :] = u[:, lo + KV_WIDTH:lo + KV_WIDTH + HEAD_DIM]


def _in_projection(x2, mod3, mod_row, g, w_in, rope_tabs, w_out, late_mod):
    t = x2.shape[0]
    tm = PROJ_TILE
    rope = rope_tabs is not None
    row = lambda i: (i, 0)
    w_mode = None if rope else pl.Buffered(1)
    in_specs = [pl.BlockSpec((tm, D_MODEL), row),
                pl.BlockSpec((1,) + mod3.shape[1:], lambda i: (mod_row(i), 0, 0)),
                pl.BlockSpec((1, D_MODEL), lambda i: (0, 0)),
                pl.BlockSpec((D_MODEL, IN_WIDTH), lambda i: (0, 0), pipeline_mode=w_mode)]
    args = [x2, mod3, g, w_in]
    if rope:
        n_seq = rope_tabs[0].shape[0]
        seq_blocks = n_seq // tm
        for tab in rope_tabs:
            in_specs.append(pl.BlockSpec((tm, HEAD_DIM), lambda i: (i % seq_blocks, 0)))
            args.append(tab)
    else:
        slab = D_MODEL // (t // tm)
        assert slab * (t // tm) == D_MODEL and slab % BF16_ROWS == 0
        for w in (w_out, w_in):
            in_specs.append(pl.BlockSpec((slab, w.shape[1]), row))
            args.append(w)
        cond, w_ada, b_ada, first_col = late_mod
        late_cols = (w_ada.shape[1] - first_col) // (t // tm)
        first_block = first_col // late_cols
        assert first_block * late_cols == first_col and late_cols % LANES == 0
        in_specs += [pl.BlockSpec((MOD_ROWS, D_MODEL), lambda i: (0, 0)),
                     pl.BlockSpec((D_MODEL, late_cols), lambda i: (0, first_block + i)),
                     pl.BlockSpec((1, late_cols), lambda i: (0, first_block + i))]
        args += [cond, w_ada, b_ada.reshape(1, -1)]
    out_specs = [pl.BlockSpec((tm, ATTN_WIDTH), row),
                 pl.BlockSpec((tm, KV_WIDTH), row),
                 pl.BlockSpec((tm, KV_WIDTH), row),
                 pl.BlockSpec((tm, POOL_WIDTH), row)]
    out_shape = [jax.ShapeDtypeStruct((t, ATTN_WIDTH), BF16),
                 jax.ShapeDtypeStruct((t, KV_WIDTH), F32),
                 jax.ShapeDtypeStruct((t, KV_WIDTH), F32),
                 jax.ShapeDtypeStruct((t, POOL_WIDTH), F32)]
    if not rope:
        for _ in range(2):
            out_specs.append(pl.BlockSpec((tm * N_KV_HEADS, HEAD_DIM), row))
            out_shape.append(jax.ShapeDtypeStruct((t * N_KV_HEADS, HEAD_DIM), F32))
        for w in (w_out, w_in):
            out_specs.append(pl.BlockSpec((slab, w.shape[1]), row))
            out_shape.append(jax.ShapeDtypeStruct(w.shape, BF16))
        out_specs.append(pl.BlockSpec((MOD_ROWS, late_cols), lambda i: (0, i)))
        out_shape.append(jax.ShapeDtypeStruct((MOD_ROWS, w_ada.shape[1] - first_col), F32))
    return pl.pallas_call(
        functools.partial(_inproj_kernel, rope=rope),
        grid=(t // tm,),
        in_specs=in_specs,
        out_specs=out_specs,
        out_shape=out_shape,
        compiler_params=pltpu.CompilerParams(
            vmem_limit_bytes=_vmem_limit(2 * D_MODEL * IN_WIDTH * 2 + 24 * tm * D_MODEL * 4)),
        name="in_projection",
    )(*args)


def _softmax_pv(s_list, v_list, sink_col):
    m = sink_col
    for s in s_list:
        m = jnp.maximum(m, jnp.max(s, axis=-1, keepdims=True))
    denom = jnp.exp2(sink_col - m)
    out = None
    for s, v in zip(s_list, v_list):
        e = jnp.exp2(s - m)
        if v.shape[1] == HEAD_DIM:
            denom = denom + jnp.sum(e, axis=-1, keepdims=True)
        o = jnp.dot(e.astype(BF16), v, preferred_element_type=F32)
        out = o if out is None else out + o
    if out.shape[1] > HEAD_DIM:
        denom = denom + out[:, HEAD_DIM:HEAD_DIM + 1]
    return out[:, 0:HEAD_DIM] * (1.0 / denom)


def _with_ones(v):
    return jnp.concatenate([v.astype(BF16), jnp.ones(v.shape, BF16)], axis=1)


def _stack_heads(q, kv):
    return jnp.concatenate(
        [q[:, (kv * Q_PER_KV + g) * HEAD_DIM:(kv * Q_PER_KV + g + 1) * HEAD_DIM]
         for g in range(Q_PER_KV)], axis=0)


def _sink_column(sink_ref, kv, rows):
    r = lax.broadcasted_iota(I32, (Q_PER_KV * rows, 1), 0)
    col = jnp.zeros((Q_PER_KV * rows, 1), F32)
    for g in range(Q_PER_KV):
        col = jnp.where((r >= g * rows) & (r < (g + 1) * rows), sink_ref[kv * Q_PER_KV + g], col)
    return col * LOG2_E


def _qk(q, k):
    return lax.dot_general(q, k, (((1,), (1,)), ((), ())), preferred_element_type=F32)


def _ctx_attn_kernel(sink_ref, q_ref, k_ref, v_ref, o_ref):
    rows = q_ref.shape[0]
    q = q_ref[...]
    for kv in range(N_KV_HEADS):
        kh = k_ref[:, kv * HEAD_DIM:(kv + 1) * HEAD_DIM].astype(BF16)
        vh = v_ref[:, kv * HEAD_DIM:(kv + 1) * HEAD_DIM].astype(BF16)
        qs = _stack_heads(q, kv)
        s = _qk(qs, kh)
        o = _softmax_pv([s], [vh], _sink_column(sink_ref, kv, rows))
        for g in range(Q_PER_KV):
            hd = kv * Q_PER_KV + g
            o_ref[:, hd * HEAD_DIM:(hd + 1) * HEAD_DIM] = o[g * rows:(g + 1) * rows].astype(BF16)


def _context_attention(q, k, v, sink, seq):
    t = q.shape[0]
    row = lambda b: (b, 0)
    return pl.pallas_call(
        _ctx_attn_kernel,
        grid=(t // seq,),
        in_specs=[pl.BlockSpec(memory_space=pltpu.SMEM),
                  pl.BlockSpec((seq, ATTN_WIDTH), row),
                  pl.BlockSpec((seq, KV_WIDTH), row),
                  pl.BlockSpec((seq, KV_WIDTH), row)],
        out_specs=pl.BlockSpec((seq, ATTN_WIDTH), row),
        out_shape=jax.ShapeDtypeStruct((t, ATTN_WIDTH), BF16),
        name="context_attention",
    )(sink, q, k, v)


def _lat_attn_kernel(sink_ref, q_ref, k_ref, v_ref, ck_ref, cv_ref, o_ref, *, n_seq):
    i = pl.program_id(1)
    band = 3 * BLOCK
    start = pl.multiple_of(jnp.clip((i - 1) * BLOCK, 0, n_seq - band), BLOCK)
    rows = Q_PER_KV * BLOCK
    qpos = i * BLOCK + lax.broadcasted_iota(I32, (rows, band), 0) % BLOCK
    kpos = start + lax.broadcasted_iota(I32, (rows, band), 1)
    mask = jnp.abs(kpos - qpos) <= WINDOW
    q = q_ref[...]
    for kv in range(N_KV_HEADS):
        cols = slice(kv * HEAD_DIM, (kv + 1) * HEAD_DIM)
        kb = k_ref[pl.ds(start, band), cols].astype(BF16)
        vb = _with_ones(v_ref[pl.ds(start, band), cols])
        cached = pl.ds(kv, ck_ref.shape[0] // N_KV_HEADS, stride=N_KV_HEADS)
        ck = ck_ref[cached, :].astype(BF16)
        cv = _with_ones(cv_ref[cached, :])
        qs = _stack_heads(q, kv)
        s_loc = jnp.where(mask, _qk(qs, kb), NEG)
        s_ctx = _qk(qs, ck)
        o = _softmax_pv([s_loc, s_ctx], [vb, cv], _sink_column(sink_ref, kv, BLOCK))
        for g in range(Q_PER_KV):
            hd = kv * Q_PER_KV + g
            o_ref[:, hd * HEAD_DIM:(hd + 1) * HEAD_DIM] = o[g * BLOCK:(g + 1) * BLOCK].astype(BF16)


def _latent_attention(q, k, v, ck, cv, sink, n_seq):
    t = q.shape[0]
    nb = n_seq // BLOCK
    cache_rows = ck.shape[0] // (t // n_seq)
    return pl.pallas_call(
        functools.partial(_lat_attn_kernel, n_seq=n_seq),
        grid=(t // n_seq, nb),
        in_specs=[pl.BlockSpec(memory_space=pltpu.SMEM),
                  pl.BlockSpec((BLOCK, ATTN_WIDTH), lambda b, i: (b * nb + i, 0)),
                  pl.BlockSpec((n_seq, KV_WIDTH), lambda b, i: (b, 0)),
                  pl.BlockSpec((n_seq, KV_WIDTH), lambda b, i: (b, 0)),
                  pl.BlockSpec((cache_rows, HEAD_DIM), lambda b, i: (b, 0)),
                  pl.BlockSpec((cache_rows, HEAD_DIM), lambda b, i: (b, 0))],
        out_specs=pl.BlockSpec((BLOCK, ATTN_WIDTH), lambda b, i: (b * nb + i, 0)),
        out_shape=jax.ShapeDtypeStruct((t, ATTN_WIDTH), BF16),
        name="latent_attention",
    )(sink, q, k, v, ck, cv)


def _pool_group(p_ref, r0, seq, w_ref, s_ref, g):
    n = SUB_TILE
    rows = n + 2 * POOL_HALO
    static = isinstance(r0, int)
    t0 = r0 % seq
    t = t0 + lax.broadcasted_iota(I32, (n, 1), 0)
    has_top = t0 > 0
    has_bottom = t0 + n < seq
    zeros = jnp.zeros((POOL_HALO, POOL_GROUP), F32)
    w = POOL_SIZES[g]
    cols = slice(g * POOL_GROUP, (g + 1) * POOL_GROUP)
    pg = p_ref[pl.ds(r0, n), cols]
    if static:
        top = p_ref[r0 - POOL_HALO:r0, cols] if has_top else zeros
        bottom = p_ref[r0 + n:r0 + n + POOL_HALO, cols] if has_bottom else zeros
    else:
        top_row = pl.multiple_of(jnp.maximum(r0 - POOL_HALO, 0), POOL_HALO)
        bottom_row = pl.multiple_of(jnp.minimum(r0 + n, p_ref.shape[0] - POOL_HALO), POOL_HALO)
        top = jnp.where(has_top, p_ref[pl.ds(top_row, POOL_HALO), cols], 0.0)
        bottom = jnp.where(has_bottom, p_ref[pl.ds(bottom_row, POOL_HALO), cols], 0.0)
    x = jnp.concatenate([top, pg, bottom], axis=0)
    fwd = x
    span = 1
    while span < w // 2:
        fwd = fwd + pltpu.roll(fwd, rows - span, 0)
        span *= 2
    if (w // 2) % SUBLANES == 0:
        wsum = fwd[POOL_HALO - w // 2:POOL_HALO - w // 2 + n] + fwd[POOL_HALO:POOL_HALO + n]
    else:
        wsum = (fwd + pltpu.roll(fwd, w // 2, 0))[POOL_HALO:POOL_HALO + n]
    lo = jnp.maximum(t - w // 2, 0)
    hi = jnp.minimum(t + w - w // 2, seq)
    inv_cnt = 1.0 / (hi - lo).astype(F32)
    mixed = wsum * inv_cnt - pg
    y = jnp.dot(mixed.astype(BF16), w_ref[g], preferred_element_type=F32)
    return (y * s_ref[:, cols]).astype(BF16)


def _pack_pair(lo, hi):
    return lax.bitcast_convert_type(pltpu.pack_elementwise([lo, hi], packed_dtype=BF16), I32)


def _unpack_pair(words):
    lo = pltpu.unpack_elementwise(words, index=0, packed_dtype=BF16, unpacked_dtype=F32)
    hi = pltpu.unpack_elementwise(words, index=1, packed_dtype=BF16, unpacked_dtype=F32)
    return lo, hi


def _fold_router_kernel(mod_ref, g_ref, wrt_ref, w_ref, b_ref):
    mod = mod_ref[0]
    w_ref[0] = (wrt_ref[...] * (g_ref[...] * (1.0 + mod[4:5]))).astype(BF16)
    b_ref[0] = lax.dot_general(mod[3:4], wrt_ref[...], (((1,), (1,)), ((), ())),
                               preferred_element_type=F32)


def _fold_router(mod3, g, w_router_t):
    n = w_router_t.shape[0]
    return pl.pallas_call(
        _fold_router_kernel,
        grid=(MOD_ROWS,),
        in_specs=[pl.BlockSpec((1, N_MOD, D_MODEL), lambda r: (r, 0, 0)),
                  pl.BlockSpec((1, D_MODEL), lambda r: (0, 0)),
                  pl.BlockSpec((n, D_MODEL), lambda r: (0, 0))],
        out_specs=[pl.BlockSpec((1, n, D_MODEL), lambda r: (r, 0, 0)),
                   pl.BlockSpec((1, 1, n), lambda r: (r, 0, 0))],
        out_shape=[jax.ShapeDtypeStruct((MOD_ROWS, n, D_MODEL), BF16),
                   jax.ShapeDtypeStruct((MOD_ROWS, 1, n), F32)],
        name="fold_router",
    )(mod3, g, w_router_t)


def _outproj_kernel(a_ref, p_ref, x_ref, mod_ref, g_ref, wo_ref, wr_ref, rb_ref, pw_ref, ps_ref,
                    x1_ref, h_ref, aff_ref, *, seq):
    mod = mod_ref[0]
    groups = SUB_TILE // SUBLANES
    half = D_MODEL // 2
    tm = x_ref.shape[0]
    steps_per_p_block = p_ref.shape[0] // tm
    for s in range(tm // SUB_TILE):
        rows = slice(s * SUB_TILE, (s + 1) * SUB_TILE)
        grp = slice(s * groups, (s + 1) * groups)
        r0 = s * SUB_TILE
        if steps_per_p_block > 1:
            r0 = pl.multiple_of((pl.program_id(0) % steps_per_p_block) * tm + r0, SUB_TILE)
        pooled = jnp.concatenate([_pool_group(p_ref, r0, seq, pw_ref, ps_ref, g)
                                  for g in range(len(POOL_SIZES))], axis=1)
        mix = (jnp.dot(a_ref[rows, :], wo_ref[0:ATTN_WIDTH, :], preferred_element_type=F32)
               + jnp.dot(pooled, wo_ref[ATTN_WIDTH:D_MODEL, :], preferred_element_type=F32))
        x1 = x_ref[rows, :] + mod[2:3] * mix
        x1_ref[rows, :] = x1
        inv_rms = lax.rsqrt(jnp.mean(x1 * x1, axis=-1, keepdims=True) + EPS)
        h = ((x1 * inv_rms) * g_ref[...]) * (1.0 + mod[4:5]) + mod[3:4]
        raw = lax.dot_general(x1.astype(BF16), wr_ref[0], (((1,), (1,)), ((), ())),
                              preferred_element_type=F32)
        logits = raw * inv_rms + rb_ref[0]
        lane = lax.broadcasted_iota(I32, logits.shape, 1)
        logits = jnp.where(lane < N_EXPERTS, logits, -jnp.inf)
        m = jnp.max(logits, axis=-1, keepdims=True)
        e = jnp.exp(logits - m)
        aff = e / jnp.sum(e, axis=-1, keepdims=True)
        aff_ref[rows, :] = aff[:, 0:N_EXPERTS]
        for c in range(HQ_TILES - 1):
            words = _pack_pair(h[:, c * LANES:(c + 1) * LANES],
                               h[:, half + c * LANES:half + (c + 1) * LANES])
            h_ref[grp, c * SUBLANES:(c + 1) * SUBLANES, :] = (
                words.reshape(groups, SUBLANES, LANES))
        h_ref[grp, (HQ_TILES - 1) * SUBLANES:, :] = (
            lax.bitcast_convert_type(aff, I32).reshape(groups, SUBLANES, LANES))


def _out_projection(attn, p, x2, mod3, mod_row, g, w_out_bf, w_router_t, pool_w, pool_scale, seq):
    t = x2.shape[0]
    tm = PROJ_TILE
    row = lambda i: (i, 0)
    p_rows = max(tm, seq)
    steps_per_p_block = p_rows // tm
    router_w, router_b = _fold_router(mod3, g, w_router_t)
    return pl.pallas_call(
        functools.partial(_outproj_kernel, seq=seq),
        grid=(t // tm,),
        in_specs=[pl.BlockSpec((tm, ATTN_WIDTH), row),
                  pl.BlockSpec((p_rows, POOL_WIDTH), lambda i: (i // steps_per_p_block, 0)),
                  pl.BlockSpec((tm, D_MODEL), row),
                  pl.BlockSpec((1, N_MOD, D_MODEL), lambda i: (mod_row(i), 0, 0)),
                  pl.BlockSpec((1, D_MODEL), lambda i: (0, 0)),
                  pl.BlockSpec((D_MODEL, D_MODEL), lambda i: (0, 0)),
                  pl.BlockSpec((1, LANES, D_MODEL), lambda i: (mod_row(i), 0, 0)),
                  pl.BlockSpec((1, 1, LANES), lambda i: (mod_row(i), 0, 0)),
                  pl.BlockSpec((len(POOL_SIZES), POOL_GROUP, POOL_GROUP), lambda i: (0, 0, 0)),
                  pl.BlockSpec((1, POOL_WIDTH), lambda i: (0, 0))],
        out_specs=[pl.BlockSpec((tm, D_MODEL), row),
                   pl.BlockSpec((tm // SUBLANES, HQ_TILES * SUBLANES, LANES), lambda i: (i, 0, 0)),
                   pl.BlockSpec((tm, N_EXPERTS), row)],
        out_shape=[jax.ShapeDtypeStruct((t, D_MODEL), F32),
                   jax.ShapeDtypeStruct((t // SUBLANES, HQ_TILES * SUBLANES, LANES), I32),
                   jax.ShapeDtypeStruct((t, N_EXPERTS), F32)],
        compiler_params=pltpu.CompilerParams(
            vmem_limit_bytes=_vmem_limit(2 * D_MODEL * D_MODEL * 2 + 24 * tm * D_MODEL * 4)),
        name="out_projection",
    )(attn, p, x2, mod3, g, w_out_bf, router_w, router_b, pool_w, pool_scale)


def _route_kernel(a_ref, idx_ref, slot_ref, off_ref, slot_scr, *, cap):
    a = a_ref[...]
    n_e, n_c, _ = a.shape
    rows = n_e * n_c

    def enough(cand):
        cand_f = lax.bitcast_convert_type(cand, F32)
        cnt = jnp.sum(jnp.sum((a >= cand_f).astype(F32), axis=1, keepdims=True),
                      axis=2, keepdims=True)
        return cnt >= cap

    def two_bits(it, thr):
        low = 28 - 2 * it
        for setting in (1, 2, 3):
            cand = thr | jnp.left_shift(jnp.int32(setting), low)
            best = jnp.where(enough(cand), cand, thr if setting == 1 else best)
        return best

    top = jnp.full((n_e, 1, 1), 1 << 30, I32)
    thr = jnp.where(enough(top), top, 0)
    thr = lax.fori_loop(0, 15, two_bits, thr)
    thr_f = lax.bitcast_convert_type(thr, F32)
    gt = (a > thr_f).astype(F32).reshape(rows, LANES)
    eq = (a == thr_f).astype(F32).reshape(rows, LANES)

    li = lax.broadcasted_iota(I32, (LANES, LANES), 0)
    lj = lax.broadcasted_iota(I32, (LANES, LANES), 1)
    upper_incl = (li <= lj).astype(BF16)
    ci = lax.broadcasted_iota(I32, (n_c, n_c), 0)
    cj = lax.broadcasted_iota(I32, (n_c, n_c), 1)
    before = (cj < ci).astype(BF16)
    whole = jnp.ones((n_c, n_c), BF16)

    def over_chunks(mat, col):
        wide = jnp.broadcast_to(col, (rows, LANES)).astype(BF16)
        side_by_side = jnp.concatenate([wide[e * n_c:(e + 1) * n_c] for e in range(n_e)], axis=1)
        res = jnp.dot(mat, side_by_side, preferred_element_type=F32)
        return jnp.concatenate([res[:, e * LANES:(e + 1) * LANES] for e in range(n_e)], axis=0)

    def prefix(x):
        incl = jnp.dot(x.astype(BF16), upper_incl, preferred_element_type=F32)
        tot = incl[:, LANES - 1:LANES]
        return incl, tot, over_chunks(before, tot)

    n_gt = over_chunks(whole, jnp.sum(gt, axis=1, keepdims=True))
    need = cap - n_gt
    incl_eq, _, off_eq = prefix(eq)
    rank_eq = off_eq + incl_eq - eq
    sel = jnp.where((eq > 0) & (rank_eq < need), 1.0, gt)
    incl, tot, off = prefix(sel)
    slot = off + incl - sel
    slot_scr[...] = jnp.where(sel > 0, slot, -1.0)
    for c in range(n_c):
        per_expert = slot_scr[pl.ds(c, n_e, stride=n_c), :]
        slot_ref[c * LANES:(c + 1) * LANES, :] = per_expert.T.astype(I32)
    off_ref[...] = off.astype(I32).reshape(n_e, n_c, LANES)

    s_lane = lax.broadcasted_iota(I32, (1, cap), 1).astype(F32)
    c_col = lax.broadcasted_iota(I32, (n_c, 1), 0).astype(F32)
    for e in range(n_e):
        r0 = e * n_c
        incl_e = incl[r0:r0 + n_c]
        off_e = off[r0:r0 + n_c, 0:1]
        tot_e = tot[r0:r0 + n_c]
        onehot = ((off_e <= s_lane) & (s_lane < off_e + tot_e)).astype(F32)
        counts = lax.dot_general(incl_e.astype(BF16), onehot.astype(BF16),
                                 (((0,), (0,)), ((), ())), preferred_element_type=F32)
        local = s_lane - jnp.sum(onehot * off_e, axis=0, keepdims=True)
        lane = jnp.sum((counts <= local).astype(F32), axis=0, keepdims=True)
        chunk = jnp.sum(onehot * c_col, axis=0, keepdims=True)
        idx_ref[e] = (chunk * LANES + lane).astype(I32)


def _routing(aff, cap):
    t = aff.shape[0]
    n_c = t // LANES
    a3 = aff.T.reshape(N_EXPERTS, n_c, LANES)
    return pl.pallas_call(
        functools.partial(_route_kernel, cap=cap),
        out_shape=[jax.ShapeDtypeStruct((N_EXPERTS, 1, cap), I32),
                   jax.ShapeDtypeStruct((t, N_EXPERTS), I32),
                   jax.ShapeDtypeStruct((N_EXPERTS, n_c, LANES), I32)],
        scratch_shapes=[pltpu.VMEM((N_EXPERTS * n_c, LANES), F32)],
        compiler_params=pltpu.CompilerParams(vmem_limit_bytes=_vmem_limit(48 << 20)),
        name="routing",
    )(a3)


def _gather_rows(table, row_ids):
    n_chunks = row_ids.shape[0]
    n_workers = SC_CORES * SC_SUBCORES
    per_worker = n_chunks // n_workers
    assert row_ids.shape[1] == GATHER_CHUNK and n_chunks % n_workers == 0
    mesh = plsc.VectorSubcoreMesh(core_axis_name="core", subcore_axis_name="subcore")

    @functools.partial(
        pl.kernel, mesh=mesh,
        out_type=jax.ShapeDtypeStruct((n_chunks * GATHER_CHUNK, LANES), I32),
        scratch_types=[pltpu.VMEM((per_worker, GATHER_CHUNK), I32),
                       pltpu.VMEM((2, GATHER_CHUNK, LANES), I32),
                       pltpu.SemaphoreType.DMA((2,)),
                       pltpu.SemaphoreType.DMA((2,))],
        name="gather_rows",
    )
    def gather(table_hbm, ids_hbm, out_hbm, ids_v, rows_v, gather_sem, store_sem):
        worker = lax.axis_index("subcore") * SC_CORES + lax.axis_index("core")
        first = worker * per_worker
        pltpu.sync_copy(ids_hbm.at[worker], ids_v)

        def fetch(j):
            return pltpu.make_async_copy(table_hbm.at[ids_v.at[j]], rows_v.at[j % 2],
                                         gather_sem.at[j % 2])

        def store(j):
            rows = pl.ds(pl.multiple_of((first + j) * GATHER_CHUNK, GATHER_CHUNK), GATHER_CHUNK)
            return pltpu.make_async_copy(rows_v.at[j % 2], out_hbm.at[rows], store_sem.at[j % 2])

        for j in range(per_worker):
            if j >= 2:
                store(j - 2).wait()
            fetch(j).start()
            if j >= 1:
                fetch(j - 1).wait()
                store(j - 1).start()
        fetch(per_worker - 1).wait()
        store(per_worker - 1).start()
        for j in range(max(per_worker - 2, 0), per_worker):
            store(j).wait()

    return gather(table, row_ids.reshape(n_workers, per_worker, GATHER_CHUNK))


def _packed_row_ids(idx, cap):
    tok = idx.reshape(-1, 1, cap // GATHER_CHUNK, GATHER_CHUNK)
    tile = jnp.arange(HQ_TILES, dtype=I32).reshape(1, HQ_TILES, 1, 1)
    ids = ((tok // SUBLANES) * HQ_TILES + tile) * SUBLANES + tok % SUBLANES
    return ids.reshape(-1, GATHER_CHUNK)


def _ffn_kernel(*refs, caps, n_f, row_chunk, first_expert, n_prior):
    n_g = len(caps)
    x_refs = refs[:n_g]
    wg_ref, wu_ref, wd_ref = refs[n_g:n_g + 3]
    y_refs = refs[n_g + 3 + n_prior:2 * n_g + 3 + n_prior]
    acc_ref = refs[2 * n_g + 3 + n_prior]
    e = pl.program_id(0)
    f = pl.program_id(1)

    @pl.when((e == 0) & (f == 0))
    def _():
        acc_ref[...] = jnp.zeros_like(acc_ref)

    def step(last):
        first = f == 0
        base = 0
        for x_ref, y_ref, cap in zip(x_refs, y_refs, caps):
            if last:
                aff = lax.bitcast_convert_type(x_ref[0, HQ_TILES - 1], F32)
                lane = lax.broadcasted_iota(I32, aff.shape, 1)
                gates = jnp.sum(jnp.where(lane == e + first_expert, aff, 0.0), axis=1,
                                keepdims=True)
                y_ref[0, cap:, :] = jnp.zeros((COMBINE_WINDOW, D_MODEL // 2), I32)
            for r in range(cap // row_chunk):
                rs = slice(r * row_chunk, (r + 1) * row_chunk)
                acc_rows = slice(base + r * row_chunk, base + (r + 1) * row_chunk)
                pairs = [_unpack_pair(x_ref[0, c, rs, :]) for c in range(HQ_TILES - 1)]
                x = jnp.concatenate([p[0] for p in pairs] + [p[1] for p in pairs], axis=1)
                gate_act = jnp.dot(x, wg_ref[0], preferred_element_type=F32)
                up = jnp.dot(x, wu_ref[0], preferred_element_type=F32)
                hid = (gate_act * jax.nn.sigmoid(gate_act)) * up
                part = jnp.dot(hid, wd_ref[0], preferred_element_type=F32)
                if last:
                    y = (part + acc_ref[acc_rows, :]) * gates[rs]
                    y_ref[0, rs, :] = _pack_pair(y[:, 0:D_MODEL // 2], y[:, D_MODEL // 2:])
                else:
                    acc_ref[acc_rows, :] = part + jnp.where(first, 0.0, acc_ref[acc_rows, :])
            base += cap

    @pl.when(f < n_f - 1)
    def _():
        step(False)

    @pl.when(f == n_f - 1)
    def _():
        step(True)


def _expert_ffn(xs_groups, w_gate, w_up, w_down, first_expert, prior_outputs):
    n_f = 4
    tf = D_EXPERT // n_f
    row_chunk = 512
    n_e = xs_groups[0].shape[0]
    e0 = first_expert
    caps = tuple(xs.shape[2] for xs in xs_groups)
    rows = sum(caps)
    est = (2 * HQ_TILES * rows * LANES * 4 + rows * D_MODEL * (4 + 2 * 2) + 2 * 3 * D_MODEL * tf * 4
           + row_chunk * (3 * tf + 2 * D_MODEL) * 4)
    x_specs = [pl.BlockSpec((1, HQ_TILES, cap, LANES), lambda e, f: (e, 0, 0, 0)) for cap in caps]
    y_shapes = [(N_EXPERTS, cap + COMBINE_WINDOW, D_MODEL // 2) for cap in caps]
    y_specs = [pl.BlockSpec((1,) + s[1:], lambda e, f: (e + e0, 0, 0)) for s in y_shapes]
    prior = list(prior_outputs or [])
    n_in = len(caps) + 3
    return pl.pallas_call(
        functools.partial(_ffn_kernel, caps=caps, n_f=n_f, row_chunk=row_chunk,
                          first_expert=e0, n_prior=len(prior)),
        grid=(n_e, n_f),
        in_specs=(x_specs + [pl.BlockSpec((1, D_MODEL, tf), lambda e, f: (e + e0, 0, f)),
                             pl.BlockSpec((1, D_MODEL, tf), lambda e, f: (e + e0, 0, f)),
                             pl.BlockSpec((1, tf, D_MODEL), lambda e, f: (e + e0, f, 0))]
                  + [pl.BlockSpec(memory_space=pl.ANY) for _ in prior]),
        out_specs=y_specs,
        out_shape=[jax.ShapeDtypeStruct(s, I32) for s in y_shapes],
        input_output_aliases={n_in + j: j for j in range(len(prior))},
        scratch_shapes=[pltpu.VMEM((rows, D_MODEL), F32)],
        compiler_params=pltpu.CompilerParams(
            dimension_semantics=("arbitrary", "arbitrary"),
            vmem_limit_bytes=_vmem_limit(est + (4 << 20))),
        name="expert_ffn",
    )(*xs_groups, w_gate, w_up, w_down, *prior)


def _combine_kernel(start_ref, nch_ref, wide_ref, x1_ref, mod_ref, g_ref, slot_ref, y_hbm, o_ref,
                    ybuf_ref, ffn_ref, sem, *, first_k, kblock):
    i = pl.program_id(0)
    tm = x1_ref.shape[0]
    cur = i % 2
    n_window_rows = N_EXPERTS * COMBINE_WINDOW

    window_head = COMBINE_WINDOW - COMBINE_CHUNK

    def window_copy(e, src_row, buf):
        return pltpu.make_async_copy(
            y_hbm.at[e, pl.ds(pl.multiple_of(src_row, SUBLANES), window_head)],
            ybuf_ref.at[buf, e * COMBINE_WINDOW:e * COMBINE_WINDOW + window_head], sem.at[buf])

    def window_tail_copy(e, src_row, buf):
        return chunk_copy(e, src_row + window_head, buf,
                          jnp.int32(e * COMBINE_WINDOW + window_head))

    def needs_tail(tile, e):
        return nch_ref[tile, e] * COMBINE_CHUNK > window_head

    def chunk_copy(e, src_row, buf, dst_row):
        return pltpu.make_async_copy(
            y_hbm.at[e, pl.ds(pl.multiple_of(src_row, SUBLANES), COMBINE_CHUNK)],
            ybuf_ref.at[buf, pl.ds(pl.multiple_of(dst_row, COMBINE_CHUNK), COMBINE_CHUNK)],
            sem.at[buf])

    def fetch(tile, buf):
        @pl.when(wide_ref[tile] == 0)
        def _():
            for e in range(N_EXPERTS):
                window_copy(e, start_ref[tile, e], buf).start()

                @pl.when(needs_tail(tile, e))
                def _(e=e):
                    window_tail_copy(e, start_ref[tile, e], buf).start()

        @pl.when(wide_ref[tile] != 0)
        def _():
            pos = jnp.int32(0)
            for e in range(N_EXPERTS):
                start = start_ref[tile, e]
                nch = nch_ref[tile, e]

                def issue(c, carry, e=e, start=start, pos=pos):
                    chunk_copy(e, start + c * COMBINE_CHUNK, buf, pos + c * COMBINE_CHUNK).start()
                    return carry
                lax.fori_loop(0, nch, issue, 0)
                pos = pos + nch * COMBINE_CHUNK

    @pl.when(i == 0)
    def _():
        ybuf_ref[...] = jnp.zeros_like(ybuf_ref)
        fetch(0, 0)

    @pl.when(i + 1 < pl.num_programs(0))
    def _():
        fetch(i + 1, 1 - cur)

    def finish(ffn):
        out = x1_ref[...] + mod_ref[0][5:6] * ffn
        ms = jnp.mean(out * out, axis=-1, keepdims=True)
        o_ref[...] = (out * lax.rsqrt(ms + EPS)) * g_ref[...]

    def apply_selection(sel, k0, width):
        lo, hi = _unpack_pair(ybuf_ref[cur, pl.ds(k0, width), :])
        return jnp.concatenate([jnp.dot(sel, lo, preferred_element_type=F32),
                                jnp.dot(sel, hi, preferred_element_type=F32)], axis=1)

    @pl.when(wide_ref[i] == 0)
    def _():
        for e in range(N_EXPERTS):
            window_copy(e, jnp.int32(0), cur).wait()

            @pl.when(needs_tail(i, e))
            def _(e=e):
                window_tail_copy(e, jnp.int32(0), cur).wait()
        expert_lane = lax.broadcasted_iota(I32, (1, N_EXPERTS), 1)
        starts = jnp.zeros((1, N_EXPERTS), I32)
        for e in range(N_EXPERTS):
            starts = jnp.where(expert_lane == e, start_ref[i, e], starts)
        slots = slot_ref[...]
        rows_in_window = jnp.where(slots >= 0, (slots - starts).astype(F32), -1.0)
        window_of_lane = lax.broadcasted_iota(I32, (N_EXPERTS, n_window_rows), 1) // COMBINE_WINDOW
        spread = (window_of_lane == lax.broadcasted_iota(I32, (N_EXPERTS, n_window_rows), 0))
        target = jnp.dot(rows_in_window.astype(BF16), spread.astype(BF16),
                         preferred_element_type=F32)
        lane_row = (lax.broadcasted_iota(I32, (tm, n_window_rows), 1) % COMBINE_WINDOW).astype(F32)
        sel = jnp.where(target == lane_row, 1.0, 0.0)
        ffn = None
        for k0 in range(0, n_window_rows, kblock):
            part = apply_selection(sel[:, k0:k0 + kblock], k0, kblock)
            ffn = part if ffn is None else ffn + part
        finish(ffn)

    @pl.when(wide_ref[i] != 0)
    def _():
        pos = jnp.int32(0)
        begins = []
        for e in range(N_EXPERTS):
            begins.append(pos)
            pos = pos + nch_ref[i, e] * COMBINE_CHUNK
        begins.append(pos)

        def drain(c, carry):
            chunk_copy(0, jnp.int32(0), cur, jnp.int32(0)).wait()
            return carry
        lax.fori_loop(0, pos // COMBINE_CHUNK, drain, 0)

        expert_lane = lax.broadcasted_iota(I32, (1, N_EXPERTS), 1)
        expert_row = lax.broadcasted_iota(I32, (N_EXPERTS, 1), 0)
        shift = jnp.zeros((1, N_EXPERTS), I32)
        range_lo = jnp.zeros((N_EXPERTS, 1), I32)
        range_hi = jnp.zeros((N_EXPERTS, 1), I32)
        for e in range(N_EXPERTS):
            shift = jnp.where(expert_lane == e, begins[e] - start_ref[i, e], shift)
            range_lo = jnp.where(expert_row == e, begins[e], range_lo)
            range_hi = jnp.where(expert_row == e, begins[e + 1], range_hi)
        slots = slot_ref[...]
        packed_row = slots + shift
        row_hi = jnp.where(slots >= 0, packed_row // ROW_SPLIT, -1).astype(F32).astype(BF16)
        row_lo = jnp.where(slots >= 0, packed_row % ROW_SPLIT, 0).astype(F32).astype(BF16)

        def selection(k0, width):
            col = k0 + lax.broadcasted_iota(I32, (N_EXPERTS, width), 1)
            spread = ((col >= range_lo) & (col < range_hi)).astype(BF16)
            want_hi = jnp.dot(row_hi, spread, preferred_element_type=F32)
            want_lo = jnp.dot(row_lo, spread, preferred_element_type=F32)
            here = k0 + lax.broadcasted_iota(I32, (1, width), 1)
            hit = ((want_hi == (here // ROW_SPLIT).astype(F32))
                   & (want_lo == (here % ROW_SPLIT).astype(F32)) & (here < pos))
            return jnp.where(hit, 1.0, 0.0)

        ffn_ref[...] = apply_selection(selection(0, first_k), 0, first_k)

        def kstep(kb, carry):
            k0 = pl.multiple_of(kb * kblock, kblock)
            ffn_ref[...] += apply_selection(selection(k0, kblock), k0, kblock)
            return carry
        lax.fori_loop(first_k // kblock, (pos + kblock - 1) // kblock, kstep, 0)
        finish(ffn_ref[...])


def _combine(x1, mod3, mod_row, g_final, slot_t, tile_start, tile_nch, tile_wide, y):
    t = x1.shape[0]
    tm = TOKEN_TILE
    first_k = 2 * tm + N_EXPERTS * COMBINE_CHUNK
    kblock = 256
    max_rows = N_EXPERTS * (tm + 2 * COMBINE_CHUNK)
    max_rows = -(-max_rows // kblock) * kblock
    row = lambda i, *_: (i, 0)
    grid_spec = pltpu.PrefetchScalarGridSpec(
        num_scalar_prefetch=3,
        grid=(t // tm,),
        in_specs=[pl.BlockSpec((tm, D_MODEL), row),
                  pl.BlockSpec((1, N_MOD, D_MODEL), lambda i, *_: (mod_row(i), 0, 0)),
                  pl.BlockSpec((1, D_MODEL), lambda i, *_: (0, 0)),
                  pl.BlockSpec((tm, N_EXPERTS), row),
                  pl.BlockSpec(memory_space=pl.ANY)],
        out_specs=pl.BlockSpec((tm, D_MODEL), row),
        scratch_shapes=[pltpu.VMEM((2, max_rows, D_MODEL // 2), I32),
                        pltpu.VMEM((tm, D_MODEL), F32),
                        pltpu.SemaphoreType.DMA((2,))],
    )
    return pl.pallas_call(
        functools.partial(_combine_kernel, first_k=first_k, kblock=kblock),
        grid_spec=grid_spec,
        out_shape=jax.ShapeDtypeStruct((t, D_MODEL), F32),
        compiler_params=pltpu.CompilerParams(
            dimension_semantics=("arbitrary",),
            vmem_limit_bytes=_vmem_limit(2 * max_rows * D_MODEL * 2 + 16 * tm * D_MODEL * 4)),
        name="combine",
    )(tile_start, tile_nch, tile_wide, x1, mod3, g_final, slot_t, y)


def _rope_tables(n):
    f32 = np.float32
    rows = n // GRID_W
    row = np.repeat(np.arange(rows, dtype=f32), GRID_W)
    col = np.tile(np.arange(GRID_W, dtype=f32), rows)
    inv = (f32(ROPE_THETA) ** (-np.arange(ROPE_FREQS, dtype=f32) / f32(ROPE_FREQS))).astype(f32)
    ang_r = row[:, None] * inv
    ang_c = col[:, None] * inv
    zero = np.zeros_like(ang_r)
    cos = np.concatenate([np.cos(ang_r)] * 2 + [np.cos(ang_c)] * 2, axis=1)
    sin_a = np.concatenate([-np.sin(ang_r), zero, -np.sin(ang_c), zero], axis=1)
    sin_b = np.concatenate([zero, np.sin(ang_r), zero, np.sin(ang_c)], axis=1)
    return tuple(jnp.asarray(t, F32) for t in (cos, sin_a, sin_b))


def _token_group(x, mod_mix, mod3, mod_row, seq, weights, rope_tabs, ctx_kv, late_mod=None):
    (norm_mix, w_in, sink, pool_w, pool_scale, w_out, norm_ffn, w_router,
     w_gate, w_up, w_down, norm_final) = weights
    b = x.shape[0]
    t = b * seq
    x2 = x.reshape(t, D_MODEL)
    q, k, v, p, *state = _in_projection(x2, mod_mix, mod_row(PROJ_TILE), norm_mix, w_in,
                                        rope_tabs, w_out, late_mod)
    if rope_tabs is None:
        late = state.pop()
        mod3 = jnp.concatenate([mod_mix.reshape(MOD_ROWS, -1), late], axis=1).reshape(
            MOD_ROWS, N_MOD, D_MODEL)
    w_in_bf = w_in if rope_tabs is not None else state.pop()
    w_out_bf = w_out if rope_tabs is not None else state.pop()
    if ctx_kv is None:
        attn = _context_attention(q, k, v, sink, seq)
    else:
        attn = _latent_attention(q, k, v, ctx_kv[0], ctx_kv[1], sink, seq)
    x1, h_packed, aff = _out_projection(attn, p, x2, mod3, mod_row(PROJ_TILE), norm_ffn,
                                        w_out_bf, w_router, pool_w, pool_scale, seq)

    cap = EC_FACTOR * t // N_EXPERTS
    idx, slot_t, off3 = _routing(aff, cap)
    table = h_packed.reshape(-1, LANES)
    per_range = N_EXPERTS // FFN_RANGES
    xs = []
    for r in range(FFN_RANGES):
        ids = _packed_row_ids(idx[r * per_range:(r + 1) * per_range], cap)
        xs.append(_gather_rows(table, ids).reshape(per_range, HQ_TILES, cap, LANES))

    chunks_per_tile = TOKEN_TILE // LANES
    tile_off = off3[:, ::chunks_per_tile, 0]
    tile_end = jnp.concatenate([tile_off[:, 1:], jnp.full((N_EXPERTS, 1), cap, I32)], axis=1)
    tile_start = (tile_off // SUBLANES) * SUBLANES
    tile_nch = jnp.where(tile_end > tile_off,
                         (tile_end - tile_start + COMBINE_CHUNK - 1) // COMBINE_CHUNK, 0)
    tile_wide = jnp.any(tile_end - tile_start > COMBINE_WINDOW, axis=0).astype(I32)

    def finish(y):
        out = _combine(x1, mod3, mod_row(TOKEN_TILE), norm_final, slot_t, tile_start.T,
                       tile_nch.T, tile_wide, y)
        return out.reshape(b, seq, D_MODEL)
    return xs, finish, state, (w_in_bf, w_out_bf, mod3)


def kernel(x_prompt, x_sample, c, cache_k, cache_v, c_ctx, w_ada, b_ada, norm_mix, w_in,
           sink_logits, pool_w, pool_scale, w_out, norm_ffn, w_router, w_gate, w_up, w_down,
           norm_final):
    n_b, seq, _ = x_prompt.shape
    n_db, n_lat, _ = x_sample.shape
    assert 1 + n_db <= MOD_ROWS and seq == TOKEN_TILE and n_lat % PROJ_TILE == 0
    assert w_in.shape[0] == 1 and cache_k.shape[1] == 1
    assert (n_b * seq) % PROJ_TILE == 0

    cond = jnp.concatenate(
        [c_ctx[None, :], c, jnp.zeros((MOD_ROWS - 1 - n_db, D_MODEL), F32)], axis=0)
    mixer_cols = 2 * D_MODEL
    mod_mix = _modulation(cond, w_ada[0], b_ada[0], mixer_cols).reshape(MOD_ROWS, 2, D_MODEL)

    w_router_t = jnp.pad(w_router[0], ((0, 0), (0, LANES - N_EXPERTS))).T
    weights = (norm_mix[0][None, :], w_in[0], sink_logits[0], pool_w[0].astype(BF16),
               pool_scale[0][None, :], w_out[0], norm_ffn[0][None, :], w_router_t,
               w_gate.reshape(w_gate.shape[1:]), w_up.reshape(w_up.shape[1:]),
               w_down.reshape(w_down.shape[1:]), norm_final[None, :])

    xs_p, finish_p, (k_p, v_p), (w_in_bf, w_out_bf, mod3) = _token_group(
        x_prompt, mod_mix, None, lambda tile: (lambda i: 0), seq, weights, None, None,
        late_mod=(cond, w_ada[0], b_ada[0], mixer_cols))
    weights = weights[:1] + (w_in_bf,) + weights[2:5] + (w_out_bf,) + weights[6:]

    ck = cache_k.reshape(-1, HEAD_DIM)
    cv = cache_v.reshape(-1, HEAD_DIM)
    xs_l, finish_l, _, _ = _token_group(
        x_sample, mod_mix, mod3, lambda tile: (lambda i: 1 + i // (n_lat // tile)), n_lat, weights,
        _rope_tables(n_lat), (ck, cv))

    ys = None
    for r in range(FFN_RANGES):
        ys = _expert_ffn([xs_p[r], xs_l[r]], weights[8], weights[9], weights[10],
                         r * (N_EXPERTS // FFN_RANGES), ys)
    y_prompt = finish_p(ys[0])
    y_sample = finish_l(ys[1])

    state_k = k_p.reshape(n_b, 1, seq, N_KV_HEADS, HEAD_DIM)
    state_v = v_p.reshape(n_b, 1, seq, N_KV_HEADS, HEAD_DIM)
    return (y_prompt, y_sample, state_k, state_v)
```

```python
import functools

import jax
import jax.numpy as jnp
import numpy as np
from jax import lax
from jax.experimental import pallas as pl
from jax.experimental.pallas import tpu as pltpu
from jax.experimental.pallas import tpu_sc as plsc

F32 = jnp.float32
BF16 = jnp.bfloat16
I32 = jnp.int32

D_MODEL = 2048
N_HEADS = 8
N_KV_HEADS = 2
HEAD_DIM = 128
Q_PER_KV = N_HEADS // N_KV_HEADS
ATTN_WIDTH = N_HEADS * HEAD_DIM
KV_WIDTH = N_KV_HEADS * HEAD_DIM
POOL_WIDTH = D_MODEL - ATTN_WIDTH
POOL_SIZES = (2, 4, 8, 16)
POOL_GROUP = POOL_WIDTH // len(POOL_SIZES)
IN_WIDTH = ATTN_WIDTH + 2 * KV_WIDTH + POOL_WIDTH
WINDOW = 128
BLOCK = 128
GRID_W = 64
ROPE_THETA = 10000.0
ROPE_FREQS = HEAD_DIM // 4
N_EXPERTS = 16
EC_FACTOR = 2
D_EXPERT = 1024
N_MOD = 6
EPS = 1e-6
NEG = -1e30
LOG2_E = 1.4426950408889634
ATTN_SCALE = HEAD_DIM ** -0.5 * LOG2_E

LANES = 128
SUBLANES = 8
BF16_ROWS = 16
VMEM_CAP = 64 * 1024 * 1024
SC_CORES = 2
SC_SUBCORES = 16

MOD_ROWS = 8
TOKEN_TILE = 256
PROJ_TILE = 512
SUB_TILE = 256
HQ_TILES = D_MODEL // 2 // LANES + 1
GATHER_CHUNK = 128
FFN_RANGES = 2
POOL_HALO = 8
COMBINE_CHUNK = BF16_ROWS
COMBINE_WINDOW = 64
ROW_SPLIT = 32


def _vmem_limit(nbytes):
    return int(min(VMEM_CAP - (4 << 20), max(nbytes, 16 << 20)))


def _mod_block(c_ref, w_ref, b_ref):
    c = c_ref[...]
    s = c * jax.nn.sigmoid(c)
    return jnp.dot(s.astype(BF16), w_ref[...].astype(BF16),
                   preferred_element_type=F32) + b_ref[...]


def _mod_kernel(c_ref, w_ref, b_ref, o_ref):
    o_ref[...] = _mod_block(c_ref, w_ref, b_ref)


def _modulation(cond, w_ada, b_ada, n):
    tn = 1024
    return pl.pallas_call(
        _mod_kernel,
        grid=(n // tn,),
        in_specs=[pl.BlockSpec((MOD_ROWS, D_MODEL), lambda j: (0, 0)),
                  pl.BlockSpec((D_MODEL, tn), lambda j: (0, j)),
                  pl.BlockSpec((1, tn), lambda j: (0, j))],
        out_specs=pl.BlockSpec((MOD_ROWS, tn), lambda j: (0, j)),
        out_shape=jax.ShapeDtypeStruct((MOD_ROWS, n), F32),
        compiler_params=pltpu.CompilerParams(
            vmem_limit_bytes=_vmem_limit(3 * D_MODEL * tn * 4)),
        name="modulation",
    )(cond, w_ada, b_ada.reshape(1, -1))


def _norm_mod(x, g, shift, scale):
    ms = jnp.mean(x * x, axis=-1, keepdims=True)
    y = x * lax.rsqrt(ms + EPS)
    return (y * g) * (1.0 + scale) + shift


def _inproj_kernel(*refs, rope):
    if rope:
        x_ref, mod_ref, g_ref, w_ref, cos_ref, sa_ref, sb_ref, q_ref, k_ref, v_ref, p_ref = refs
    else:
        (x_ref, mod_ref, g_ref, w_ref, wo_slab_ref, wi_slab_ref, cond_ref, wada_ref, bada_ref,
         q_ref, k_ref, v_ref, p_ref, ks_ref, vs_ref, wo_bf_ref, wi_bf_ref, late_mod_ref) = refs
        wo_bf_ref[...] = wo_slab_ref[...].astype(BF16)
        wi_bf_ref[...] = wi_slab_ref[...].astype(BF16)
        late_mod_ref[...] = _mod_block(cond_ref, wada_ref, bada_ref)
    mod = mod_ref[0]
    for s in range(x_ref.shape[0] // SUB_TILE):
        rows = slice(s * SUB_TILE, (s + 1) * SUB_TILE)
        h = _norm_mod(x_ref[rows, :], g_ref[...], mod[0:1], mod[1:2])
        u = jnp.dot(h.astype(w_ref.dtype), w_ref[...], preferred_element_type=F32)

        def rot(xh, rows=rows):
            return (xh * cos_ref[rows, :] + pltpu.roll(xh, LANES - ROPE_FREQS, 1) * sa_ref[rows, :]
                    + pltpu.roll(xh, ROPE_FREQS, 1) * sb_ref[rows, :])

        for hd in range(N_HEADS):
            xh = u[:, hd * HEAD_DIM:(hd + 1) * HEAD_DIM] * ATTN_SCALE
            q_ref[rows, hd * HEAD_DIM:(hd + 1) * HEAD_DIM] = (rot(xh) if rope else xh).astype(BF16)
        for hd in range(N_KV_HEADS):
            lo = ATTN_WIDTH + hd * HEAD_DIM
            xh = u[:, lo:lo + HEAD_DIM]
            k_ref[rows, hd * HEAD_DIM:(hd + 1) * HEAD_DIM] = rot(xh) if rope else xh
        v_ref[rows, :] = u[:, ATTN_WIDTH + KV_WIDTH:ATTN_WIDTH + 2 * KV_WIDTH]
        p_ref[rows, :] = u[:, ATTN_WIDTH + 2 * KV_WIDTH:]
        if not rope:
            for hd in range(N_KV_HEADS):
                state_rows = pl.ds(s * SUB_TILE * N_KV_HEADS + hd, SUB_TILE, stride=N_KV_HEADS)
                lo = ATTN_WIDTH + hd * HEAD_DIM
                ks_ref[state_rows, :] = u[:, lo:lo + HEAD_DIM]
                vs_ref[state_rows, :] = u[:, lo + KV_WIDTH:lo + KV_WIDTH + HEAD_DIM]


def _in_projection(x2, mod3, mod_row, g, w_in, rope_tabs, w_out, late_mod):
    t = x2.shape[0]
    tm = PROJ_TILE
    rope = rope_tabs is not None
    row = lambda i: (i, 0)
    w_mode = None if rope else pl.Buffered(1)
    in_specs = [pl.BlockSpec((tm, D_MODEL), row),
                pl.BlockSpec((1,) + mod3.shape[1:], lambda i: (mod_row(i), 0, 0)),
                pl.BlockSpec((1, D_MODEL), lambda i: (0, 0)),
                pl.BlockSpec((D_MODEL, IN_WIDTH), lambda i: (0, 0), pipeline_mode=w_mode)]
    args = [x2, mod3, g, w_in]
    if rope:
        n_seq = rope_tabs[0].shape[0]
        seq_blocks = n_seq // tm
        for tab in rope_tabs:
            in_specs.append(pl.BlockSpec((tm, HEAD_DIM), lambda i: (i % seq_blocks, 0)))
            args.append(tab)
    else:
        slab = D_MODEL // (t // tm)
        assert slab * (t // tm) == D_MODEL and slab % BF16_ROWS == 0
        for w in (w_out, w_in):
            in_specs.append(pl.BlockSpec((slab, w.shape[1]), row))
            args.append(w)
        cond, w_ada, b_ada, first_col = late_mod
        late_cols = (w_ada.shape[1] - first_col) // (t // tm)
        first_block = first_col // late_cols
        assert first_block * late_cols == first_col and late_cols % LANES == 0
        in_specs += [pl.BlockSpec((MOD_ROWS, D_MODEL), lambda i: (0, 0)),
                     pl.BlockSpec((D_MODEL, late_cols), lambda i: (0, first_block + i)),
                     pl.BlockSpec((1, late_cols), lambda i: (0, first_block + i))]
        args += [cond, w_ada, b_ada.reshape(1, -1)]
    out_specs = [pl.BlockSpec((tm, ATTN_WIDTH), row),
                 pl.BlockSpec((tm, KV_WIDTH), row),
                 pl.BlockSpec((tm, KV_WIDTH), row),
                 pl.BlockSpec((tm, POOL_WIDTH), row)]
    out_shape = [jax.ShapeDtypeStruct((t, ATTN_WIDTH), BF16),
                 jax.ShapeDtypeStruct((t, KV_WIDTH), F32),
                 jax.ShapeDtypeStruct((t, KV_WIDTH), F32),
                 jax.ShapeDtypeStruct((t, POOL_WIDTH), F32)]
    if not rope:
        for _ in range(2):
            out_specs.append(pl.BlockSpec((tm * N_KV_HEADS, HEAD_DIM), row))
            out_shape.append(jax.ShapeDtypeStruct((t * N_KV_HEADS, HEAD_DIM), F32))
        for w in (w_out, w_in):
            out_specs.append(pl.BlockSpec((slab, w.shape[1]), row))
            out_shape.append(jax.ShapeDtypeStruct(w.shape, BF16))
        out_specs.append(pl.BlockSpec((MOD_ROWS, late_cols), lambda i: (0, i)))
        out_shape.append(jax.ShapeDtypeStruct((MOD_ROWS, w_ada.shape[1] - first_col), F32))
    return pl.pallas_call(
        functools.partial(_inproj_kernel, rope=rope),
        grid=(t // tm,),
        in_specs=in_specs,
        out_specs=out_specs,
        out_shape=out_shape,
        compiler_params=pltpu.CompilerParams(
            vmem_limit_bytes=_vmem_limit(2 * D_MODEL * IN_WIDTH * 2 + 24 * tm * D_MODEL * 4)),
        name="in_projection",
    )(*args)


def _softmax_pv(s_list, v_list, sink_col):
    m = sink_col
    for s in s_list:
        m = jnp.maximum(m, jnp.max(s, axis=-1, keepdims=True))
    denom = jnp.exp2(sink_col - m)
    out = None
    for s, v in zip(s_list, v_list):
        e = jnp.exp2(s - m)
        if v.shape[1] == HEAD_DIM:
            denom = denom + jnp.sum(e, axis=-1, keepdims=True)
        o = jnp.dot(e.astype(BF16), v, preferred_element_type=F32)
        out = o if out is None else out + o
    if out.shape[1] > HEAD_DIM:
        denom = denom + out[:, HEAD_DIM:HEAD_DIM + 1]
    return out[:, 0:HEAD_DIM] * (1.0 / denom)


def _with_ones(v):
    return jnp.concatenate([v.astype(BF16), jnp.ones(v.shape, BF16)], axis=1)


def _stack_heads(q, kv):
    return jnp.concatenate(
        [q[:, (kv * Q_PER_KV + g) * HEAD_DIM:(kv * Q_PER_KV + g + 1) * HEAD_DIM]
         for g in range(Q_PER_KV)], axis=0)


def _sink_column(sink_ref, kv, rows):
    r = lax.broadcasted_iota(I32, (Q_PER_KV * rows, 1), 0)
    col = jnp.zeros((Q_PER_KV * rows, 1), F32)
    for g in range(Q_PER_KV):
        col = jnp.where((r >= g * rows) & (r < (g + 1) * rows), sink_ref[kv * Q_PER_KV + g], col)
    return col * LOG2_E


def _qk(q, k):
    return lax.dot_general(q, k, (((1,), (1,)), ((), ())), preferred_element_type=F32)


def _ctx_attn_kernel(sink_ref, q_ref, k_ref, v_ref, o_ref):
    rows = q_ref.shape[0]
    q = q_ref[...]
    for kv in range(N_KV_HEADS):
        kh = k_ref[:, kv * HEAD_DIM:(kv + 1) * HEAD_DIM].astype(BF16)
        vh = v_ref[:, kv * HEAD_DIM:(kv + 1) * HEAD_DIM].astype(BF16)
        qs = _stack_heads(q, kv)
        s = _qk(qs, kh)
        o = _softmax_pv([s], [vh], _sink_column(sink_ref, kv, rows))
        for g in range(Q_PER_KV):
            hd = kv * Q_PER_KV + g
            o_ref[:, hd * HEAD_DIM:(hd + 1) * HEAD_DIM] = o[g * rows:(g + 1) * rows].astype(BF16)


def _context_attention(q, k, v, sink, seq):
    t = q.shape[0]
    row = lambda b: (b, 0)
    return pl.pallas_call(
        _ctx_attn_kernel,
        grid=(t // seq,),
        in_specs=[pl.BlockSpec(memory_space=pltpu.SMEM),
                  pl.BlockSpec((seq, ATTN_WIDTH), row),
                  pl.BlockSpec((seq, KV_WIDTH), row),
                  pl.BlockSpec((seq, KV_WIDTH), row)],
        out_specs=pl.BlockSpec((seq, ATTN_WIDTH), row),
        out_shape=jax.ShapeDtypeStruct((t, ATTN_WIDTH), BF16),
        name="context_attention",
    )(sink, q, k, v)


def _lat_attn_kernel(sink_ref, q_ref, k_ref, v_ref, ck_ref, cv_ref, o_ref, *, n_seq):
    i = pl.program_id(1)
    band = 3 * BLOCK
    start = pl.multiple_of(jnp.clip((i - 1) * BLOCK, 0, n_seq - band), BLOCK)
    rows = Q_PER_KV * BLOCK
    qpos = i * BLOCK + lax.broadcasted_iota(I32, (rows, band), 0) % BLOCK
    kpos = start + lax.broadcasted_iota(I32, (rows, band), 1)
    mask = jnp.abs(kpos - qpos) <= WINDOW
    q = q_ref[...]
    for kv in range(N_KV_HEADS):
        cols = slice(kv * HEAD_DIM, (kv + 1) * HEAD_DIM)
        kb = k_ref[pl.ds(start, band), cols].astype(BF16)
        vb = _with_ones(v_ref[pl.ds(start, band), cols])
        cached = pl.ds(kv, ck_ref.shape[0] // N_KV_HEADS, stride=N_KV_HEADS)
        ck = ck_ref[cached, :].astype(BF16)
        cv = _with_ones(cv_ref[cached, :])
        qs = _stack_heads(q, kv)
        s_loc = jnp.where(mask, _qk(qs, kb), NEG)
        s_ctx = _qk(qs, ck)
        o = _softmax_pv([s_loc, s_ctx], [vb, cv], _sink_column(sink_ref, kv, BLOCK))
        for g in range(Q_PER_KV):
            hd = kv * Q_PER_KV + g
            o_ref[:, hd * HEAD_DIM:(hd + 1) * HEAD_DIM] = o[g * BLOCK:(g + 1) * BLOCK].astype(BF16)


def _latent_attention(q, k, v, ck, cv, sink, n_seq):
    t = q.shape[0]
    nb = n_seq // BLOCK
    cache_rows = ck.shape[0] // (t // n_seq)
    return pl.pallas_call(
        functools.partial(_lat_attn_kernel, n_seq=n_seq),
        grid=(t // n_seq, nb),
        in_specs=[pl.BlockSpec(memory_space=pltpu.SMEM),
                  pl.BlockSpec((BLOCK, ATTN_WIDTH), lambda b, i: (b * nb + i, 0)),
                  pl.BlockSpec((n_seq, KV_WIDTH), lambda b, i: (b, 0)),
                  pl.BlockSpec((n_seq, KV_WIDTH), lambda b, i: (b, 0)),
                  pl.BlockSpec((cache_rows, HEAD_DIM), lambda b, i: (b, 0)),
                  pl.BlockSpec((cache_rows, HEAD_DIM), lambda b, i: (b, 0))],
        out_specs=pl.BlockSpec((BLOCK, ATTN_WIDTH), lambda b, i: (b * nb + i, 0)),
        out_shape=jax.ShapeDtypeStruct((t, ATTN_WIDTH), BF16),
        name="latent_attention",
    )(sink, q, k, v, ck, cv)


def _pool_group(p_ref, r0, seq, w_ref, s_ref, g):
    n = SUB_TILE
    rows = n + 2 * POOL_HALO
    static = isinstance(r0, int)
    t0 = r0 % seq
    t = t0 + lax.broadcasted_iota(I32, (n, 1), 0)
    has_top = t0 > 0
    has_bottom = t0 + n < seq
    zeros = jnp.zeros((POOL_HALO, POOL_GROUP), F32)
    w = POOL_SIZES[g]
    cols = slice(g * POOL_GROUP, (g + 1) * POOL_GROUP)
    pg = p_ref[pl.ds(r0, n), cols]
    if static:
        top = p_ref[r0 - POOL_HALO:r0, cols] if has_top else zeros
        bottom = p_ref[r0 + n:r0 + n + POOL_HALO, cols] if has_bottom else zeros
    else:
        top_row = pl.multiple_of(jnp.maximum(r0 - POOL_HALO, 0), POOL_HALO)
        bottom_row = pl.multiple_of(jnp.minimum(r0 + n, p_ref.shape[0] - POOL_HALO), POOL_HALO)
        top = jnp.where(has_top, p_ref[pl.ds(top_row, POOL_HALO), cols], 0.0)
        bottom = jnp.where(has_bottom, p_ref[pl.ds(bottom_row, POOL_HALO), cols], 0.0)
    x = jnp.concatenate([top, pg, bottom], axis=0)
    fwd = x
    span = 1
    while span < w // 2:
        fwd = fwd + pltpu.roll(fwd, rows - span, 0)
        span *= 2
    if (w // 2) % SUBLANES == 0:
        wsum = fwd[POOL_HALO - w // 2:POOL_HALO - w // 2 + n] + fwd[POOL_HALO:POOL_HALO + n]
    else:
        wsum = (fwd + pltpu.roll(fwd, w // 2, 0))[POOL_HALO:POOL_HALO + n]
    lo = jnp.maximum(t - w // 2, 0)
    hi = jnp.minimum(t + w - w // 2, seq)
    inv_cnt = 1.0 / (hi - lo).astype(F32)
    mixed = wsum * inv_cnt - pg
    y = jnp.dot(mixed.astype(BF16), w_ref[g], preferred_element_type=F32)
    return (y * s_ref[:, cols]).astype(BF16)


def _pack_pair(lo, hi):
    return lax.bitcast_convert_type(pltpu.pack_elementwise([lo, hi], packed_dtype=BF16), I32)


def _unpack_pair(words):
    lo = pltpu.unpack_elementwise(words, index=0, packed_dtype=BF16, unpacked_dtype=F32)
    hi = pltpu.unpack_elementwise(words, index=1, packed_dtype=BF16, unpacked_dtype=F32)
    return lo, hi


def _outproj_kernel(a_ref, p_ref, x_ref, mod_ref, g_ref, wo_ref, wr_ref, pw_ref, ps_ref,
                    x1_ref, h_ref, aff_ref, *, seq):
    mod = mod_ref[0]
    groups = SUB_TILE // SUBLANES
    half = D_MODEL // 2
    tm = x_ref.shape[0]
    steps_per_p_block = p_ref.shape[0] // tm
    for s in range(tm // SUB_TILE):
        rows = slice(s * SUB_TILE, (s + 1) * SUB_TILE)
        grp = slice(s * groups, (s + 1) * groups)
        r0 = s * SUB_TILE
        if steps_per_p_block > 1:
            r0 = pl.multiple_of((pl.program_id(0) % steps_per_p_block) * tm + r0, SUB_TILE)
        pooled = jnp.concatenate([_pool_group(p_ref, r0, seq, pw_ref, ps_ref, g)
                                  for g in range(len(POOL_SIZES))], axis=1)
        mix = (jnp.dot(a_ref[rows, :], wo_ref[0:ATTN_WIDTH, :], preferred_element_type=F32)
               + jnp.dot(pooled, wo_ref[ATTN_WIDTH:D_MODEL, :], preferred_element_type=F32))
        x1 = x_ref[rows, :] + mod[2:3] * mix
        x1_ref[rows, :] = x1
        h = _norm_mod(x1, g_ref[...], mod[3:4], mod[4:5])
        logits = jnp.dot(h.astype(BF16), wr_ref[...], preferred_element_type=F32)
        lane = lax.broadcasted_iota(I32, logits.shape, 1)
        logits = jnp.where(lane < N_EXPERTS, logits, -jnp.inf)
        m = jnp.max(logits, axis=-1, keepdims=True)
        e = jnp.exp(logits - m)
        aff = e / jnp.sum(e, axis=-1, keepdims=True)
        aff_ref[rows, :] = aff[:, 0:N_EXPERTS]
        for c in range(HQ_TILES - 1):
            words = _pack_pair(h[:, c * LANES:(c + 1) * LANES],
                               h[:, half + c * LANES:half + (c + 1) * LANES])
            h_ref[grp, c * SUBLANES:(c + 1) * SUBLANES, :] = (
                words.reshape(groups, SUBLANES, LANES))
        h_ref[grp, (HQ_TILES - 1) * SUBLANES:, :] = (
            lax.bitcast_convert_type(aff, I32).reshape(groups, SUBLANES, LANES))


def _out_projection(attn, p, x2, mod3, mod_row, g, w_out_bf, w_router, pool_w, pool_scale, seq):
    t = x2.shape[0]
    tm = PROJ_TILE
    row = lambda i: (i, 0)
    p_rows = max(tm, seq)
    steps_per_p_block = p_rows // tm
    return pl.pallas_call(
        functools.partial(_outproj_kernel, seq=seq),
        grid=(t // tm,),
        in_specs=[pl.BlockSpec((tm, ATTN_WIDTH), row),
                  pl.BlockSpec((p_rows, POOL_WIDTH), lambda i: (i // steps_per_p_block, 0)),
                  pl.BlockSpec((tm, D_MODEL), row),
                  pl.BlockSpec((1, N_MOD, D_MODEL), lambda i: (mod_row(i), 0, 0)),
                  pl.BlockSpec((1, D_MODEL), lambda i: (0, 0)),
                  pl.BlockSpec((D_MODEL, D_MODEL), lambda i: (0, 0)),
                  pl.BlockSpec((D_MODEL, LANES), lambda i: (0, 0)),
                  pl.BlockSpec((len(POOL_SIZES), POOL_GROUP, POOL_GROUP), lambda i: (0, 0, 0)),
                  pl.BlockSpec((1, POOL_WIDTH), lambda i: (0, 0))],
        out_specs=[pl.BlockSpec((tm, D_MODEL), row),
                   pl.BlockSpec((tm // SUBLANES, HQ_TILES * SUBLANES, LANES), lambda i: (i, 0, 0)),
                   pl.BlockSpec((tm, N_EXPERTS), row)],
        out_shape=[jax.ShapeDtypeStruct((t, D_MODEL), F32),
                   jax.ShapeDtypeStruct((t // SUBLANES, HQ_TILES * SUBLANES, LANES), I32),
                   jax.ShapeDtypeStruct((t, N_EXPERTS), F32)],
        compiler_params=pltpu.CompilerParams(
            vmem_limit_bytes=_vmem_limit(2 * D_MODEL * D_MODEL * 2 + 24 * tm * D_MODEL * 4)),
        name="out_projection",
    )(attn, p, x2, mod3, g, w_out_bf, w_router, pool_w, pool_scale)


def _route_kernel(a_ref, idx_ref, slot_ref, off_ref, slot_scr, *, cap):
    a = a_ref[...]
    n_e, n_c, _ = a.shape
    rows = n_e * n_c

    def enough(cand):
        cand_f = lax.bitcast_convert_type(cand, F32)
        cnt = jnp.sum(jnp.sum((a >= cand_f).astype(F32), axis=1, keepdims=True),
                      axis=2, keepdims=True)
        return cnt >= cap

    def two_bits(it, thr):
        low = 28 - 2 * it
        for setting in (1, 2, 3):
            cand = thr | jnp.left_shift(jnp.int32(setting), low)
            best = jnp.where(enough(cand), cand, thr if setting == 1 else best)
        return best

    top = jnp.full((n_e, 1, 1), 1 << 30, I32)
    thr = jnp.where(enough(top), top, 0)
    thr = lax.fori_loop(0, 15, two_bits, thr)
    thr_f = lax.bitcast_convert_type(thr, F32)
    gt = (a > thr_f).astype(F32).reshape(rows, LANES)
    eq = (a == thr_f).astype(F32).reshape(rows, LANES)

    li = lax.broadcasted_iota(I32, (LANES, LANES), 0)
    lj = lax.broadcasted_iota(I32, (LANES, LANES), 1)
    upper_incl = (li <= lj).astype(BF16)
    ci = lax.broadcasted_iota(I32, (n_c, n_c), 0)
    cj = lax.broadcasted_iota(I32, (n_c, n_c), 1)
    before = (cj < ci).astype(BF16)
    whole = jnp.ones((n_c, n_c), BF16)

    def over_chunks(mat, col):
        wide = jnp.broadcast_to(col, (rows, LANES)).astype(BF16)
        side_by_side = jnp.concatenate([wide[e * n_c:(e + 1) * n_c] for e in range(n_e)], axis=1)
        res = jnp.dot(mat, side_by_side, preferred_element_type=F32)
        return jnp.concatenate([res[:, e * LANES:(e + 1) * LANES] for e in range(n_e)], axis=0)

    def prefix(x):
        incl = jnp.dot(x.astype(BF16), upper_incl, preferred_element_type=F32)
        tot = incl[:, LANES - 1:LANES]
        return incl, tot, over_chunks(before, tot)

    n_gt = over_chunks(whole, jnp.sum(gt, axis=1, keepdims=True))
    need = cap - n_gt
    incl_eq, _, off_eq = prefix(eq)
    rank_eq = off_eq + incl_eq - eq
    sel = jnp.where((eq > 0) & (rank_eq < need), 1.0, gt)
    incl, tot, off = prefix(sel)
    slot = off + incl - sel
    slot_scr[...] = jnp.where(sel > 0, slot, -1.0)
    for c in range(n_c):
        per_expert = slot_scr[pl.ds(c, n_e, stride=n_c), :]
        slot_ref[c * LANES:(c + 1) * LANES, :] = per_expert.T.astype(I32)
    off_ref[...] = off.astype(I32).reshape(n_e, n_c, LANES)

    s_lane = lax.broadcasted_iota(I32, (1, cap), 1).astype(F32)
    c_col = lax.broadcasted_iota(I32, (n_c, 1), 0).astype(F32)
    for e in range(n_e):
        r0 = e * n_c
        incl_e = incl[r0:r0 + n_c]
        off_e = off[r0:r0 + n_c, 0:1]
        tot_e = tot[r0:r0 + n_c]
        onehot = ((off_e <= s_lane) & (s_lane < off_e + tot_e)).astype(F32)
        counts = lax.dot_general(incl_e.astype(BF16), onehot.astype(BF16),
                                 (((0,), (0,)), ((), ())), preferred_element_type=F32)
        local = s_lane - jnp.sum(onehot * off_e, axis=0, keepdims=True)
        lane = jnp.sum((counts <= local).astype(F32), axis=0, keepdims=True)
        chunk = jnp.sum(onehot * c_col, axis=0, keepdims=True)
        idx_ref[e] = (chunk * LANES + lane).astype(I32)


def _routing(aff, cap):
    t = aff.shape[0]
    n_c = t // LANES
    a3 = aff.T.reshape(N_EXPERTS, n_c, LANES)
    return pl.pallas_call(
        functools.partial(_route_kernel, cap=cap),
        out_shape=[jax.ShapeDtypeStruct((N_EXPERTS, 1, cap), I32),
                   jax.ShapeDtypeStruct((t, N_EXPERTS), I32),
                   jax.ShapeDtypeStruct((N_EXPERTS, n_c, LANES), I32)],
        scratch_shapes=[pltpu.VMEM((N_EXPERTS * n_c, LANES), F32)],
        compiler_params=pltpu.CompilerParams(vmem_limit_bytes=_vmem_limit(48 << 20)),
        name="routing",
    )(a3)


def _gather_rows(table, row_ids):
    n_chunks = row_ids.shape[0]
    n_workers = SC_CORES * SC_SUBCORES
    per_worker = n_chunks // n_workers
    assert row_ids.shape[1] == GATHER_CHUNK and n_chunks % n_workers == 0
    mesh = plsc.VectorSubcoreMesh(core_axis_name="core", subcore_axis_name="subcore")

    @functools.partial(
        pl.kernel, mesh=mesh,
        out_type=jax.ShapeDtypeStruct((n_chunks * GATHER_CHUNK, LANES), I32),
        scratch_types=[pltpu.VMEM((per_worker, GATHER_CHUNK), I32),
                       pltpu.VMEM((2, GATHER_CHUNK, LANES), I32),
                       pltpu.SemaphoreType.DMA((2,)),
                       pltpu.SemaphoreType.DMA((2,))],
        name="gather_rows",
    )
    def gather(table_hbm, ids_hbm, out_hbm, ids_v, rows_v, gather_sem, store_sem):
        worker = lax.axis_index("subcore") * SC_CORES + lax.axis_index("core")
        first = worker * per_worker
        pltpu.sync_copy(ids_hbm.at[worker], ids_v)

        def fetch(j):
            return pltpu.make_async_copy(table_hbm.at[ids_v.at[j]], rows_v.at[j % 2],
                                         gather_sem.at[j % 2])

        def store(j):
            rows = pl.ds(pl.multiple_of((first + j) * GATHER_CHUNK, GATHER_CHUNK), GATHER_CHUNK)
            return pltpu.make_async_copy(rows_v.at[j % 2], out_hbm.at[rows], store_sem.at[j % 2])

        for j in range(per_worker):
            if j >= 2:
                store(j - 2).wait()
            fetch(j).start()
            if j >= 1:
                fetch(j - 1).wait()
                store(j - 1).start()
        fetch(per_worker - 1).wait()
        store(per_worker - 1).start()
        for j in range(max(per_worker - 2, 0), per_worker):
            store(j).wait()

    return gather(table, row_ids.reshape(n_workers, per_worker, GATHER_CHUNK))


def _packed_row_ids(idx, cap):
    tok = idx.reshape(-1, 1, cap // GATHER_CHUNK, GATHER_CHUNK)
    tile = jnp.arange(HQ_TILES, dtype=I32).reshape(1, HQ_TILES, 1, 1)
    ids = ((tok // SUBLANES) * HQ_TILES + tile) * SUBLANES + tok % SUBLANES
    return ids.reshape(-1, GATHER_CHUNK)


def _ffn_kernel(*refs, caps, n_f, row_chunk, first_expert, n_prior):
    n_g = len(caps)
    x_refs = refs[:n_g]
    wg_ref, wu_ref, wd_ref = refs[n_g:n_g + 3]
    y_refs = refs[n_g + 3 + n_prior:2 * n_g + 3 + n_prior]
    acc_ref = refs[2 * n_g + 3 + n_prior]
    e = pl.program_id(0)
    f = pl.program_id(1)

    @pl.when((e == 0) & (f == 0))
    def _():
        acc_ref[...] = jnp.zeros_like(acc_ref)

    def step(last):
        first = f == 0
        base = 0
        for x_ref, y_ref, cap in zip(x_refs, y_refs, caps):
            if last:
                aff = lax.bitcast_convert_type(x_ref[0, HQ_TILES - 1], F32)
                lane = lax.broadcasted_iota(I32, aff.shape, 1)
                gates = jnp.sum(jnp.where(lane == e + first_expert, aff, 0.0), axis=1,
                                keepdims=True)
                y_ref[0, cap:, :] = jnp.zeros((COMBINE_WINDOW, D_MODEL // 2), I32)
            for r in range(cap // row_chunk):
                rs = slice(r * row_chunk, (r + 1) * row_chunk)
                acc_rows = slice(base + r * row_chunk, base + (r + 1) * row_chunk)
                pairs = [_unpack_pair(x_ref[0, c, rs, :]) for c in range(HQ_TILES - 1)]
                x = jnp.concatenate([p[0] for p in pairs] + [p[1] for p in pairs],
                                    axis=1).astype(BF16)
                gate_act = lax.dot_general(x, wg_ref[0], (((1,), (0,)), ((), ())),
                                           preferred_element_type=F32)
                up = lax.dot_general(x, wu_ref[0], (((1,), (0,)), ((), ())),
                                     preferred_element_type=F32)
                hid = (gate_act * jax.nn.sigmoid(gate_act)) * up
                part = jnp.dot(hid, wd_ref[0], preferred_element_type=F32)
                if last:
                    y = (part + acc_ref[acc_rows, :]) * gates[rs]
                    y_ref[0, rs, :] = _pack_pair(y[:, 0:D_MODEL // 2], y[:, D_MODEL // 2:])
                else:
                    acc_ref[acc_rows, :] = part + jnp.where(first, 0.0, acc_ref[acc_rows, :])
            base += cap

    @pl.when(f < n_f - 1)
    def _():
        step(False)

    @pl.when(f == n_f - 1)
    def _():
        step(True)


def _expert_ffn(xs_groups, w_gate, w_up, w_down, first_expert, prior_outputs):
    n_f = 4
    tf = D_EXPERT // n_f
    row_chunk = 512
    n_e = xs_groups[0].shape[0]
    e0 = first_expert
    caps = tuple(xs.shape[2] for xs in xs_groups)
    rows = sum(caps)
    est = (2 * HQ_TILES * rows * LANES * 4 + rows * D_MODEL * (4 + 2 * 2) + 2 * 3 * D_MODEL * tf * 4
           + row_chunk * (3 * tf + 2 * D_MODEL) * 4)
    x_specs = [pl.BlockSpec((1, HQ_TILES, cap, LANES), lambda e, f: (e, 0, 0, 0)) for cap in caps]
    y_shapes = [(N_EXPERTS, cap + COMBINE_WINDOW, D_MODEL // 2) for cap in caps]
    y_specs = [pl.BlockSpec((1,) + s[1:], lambda e, f: (e + e0, 0, 0)) for s in y_shapes]
    prior = list(prior_outputs or [])
    n_in = len(caps) + 3
    return pl.pallas_call(
        functools.partial(_ffn_kernel, caps=caps, n_f=n_f, row_chunk=row_chunk,
                          first_expert=e0, n_prior=len(prior)),
        grid=(n_e, n_f),
        in_specs=(x_specs + [pl.BlockSpec((1, D_MODEL, tf), lambda e, f: (e + e0, 0, f)),
                             pl.BlockSpec((1, D_MODEL, tf), lambda e, f: (e + e0, 0, f)),
                             pl.BlockSpec((1, tf, D_MODEL), lambda e, f: (e + e0, f, 0))]
                  + [pl.BlockSpec(memory_space=pl.ANY) for _ in prior]),
        out_specs=y_specs,
        out_shape=[jax.ShapeDtypeStruct(s, I32) for s in y_shapes],
        input_output_aliases={n_in + j: j for j in range(len(prior))},
        scratch_shapes=[pltpu.VMEM((rows, D_MODEL), F32)],
        compiler_params=pltpu.CompilerParams(
            dimension_semantics=("arbitrary", "arbitrary"),
            vmem_limit_bytes=_vmem_limit(est + (4 << 20))),
        name="expert_ffn",
    )(*xs_groups, w_gate, w_up, w_down, *prior)


def _combine_kernel(start_ref, nch_ref, wide_ref, x1_ref, mod_ref, g_ref, slot_ref, y_hbm, o_ref,
                    ybuf_ref, ffn_ref, sem, *, first_k, kblock):
    i = pl.program_id(0)
    tm = x1_ref.shape[0]
    cur = i % 2
    n_window_rows = N_EXPERTS * COMBINE_WINDOW

    window_head = COMBINE_WINDOW - COMBINE_CHUNK

    def window_copy(e, src_row, buf):
        return pltpu.make_async_copy(
            y_hbm.at[e, pl.ds(pl.multiple_of(src_row, SUBLANES), window_head)],
            ybuf_ref.at[buf, e * COMBINE_WINDOW:e * COMBINE_WINDOW + window_head], sem.at[buf])

    def window_tail_copy(e, src_row, buf):
        return chunk_copy(e, src_row + window_head, buf,
                          jnp.int32(e * COMBINE_WINDOW + window_head))

    def needs_tail(tile, e):
        return nch_ref[tile, e] * COMBINE_CHUNK > window_head

    def chunk_copy(e, src_row, buf, dst_row):
        return pltpu.make_async_copy(
            y_hbm.at[e, pl.ds(pl.multiple_of(src_row, SUBLANES), COMBINE_CHUNK)],
            ybuf_ref.at[buf, pl.ds(pl.multiple_of(dst_row, COMBINE_CHUNK), COMBINE_CHUNK)],
            sem.at[buf])

    def fetch(tile, buf):
        @pl.when(wide_ref[tile] == 0)
        def _():
            for e in range(N_EXPERTS):
                window_copy(e, start_ref[tile, e], buf).start()

                @pl.when(needs_tail(tile, e))
                def _(e=e):
                    window_tail_copy(e, start_ref[tile, e], buf).start()

        @pl.when(wide_ref[tile] != 0)
        def _():
            pos = jnp.int32(0)
            for e in range(N_EXPERTS):
                start = start_ref[tile, e]
                nch = nch_ref[tile, e]

                def issue(c, carry, e=e, start=start, pos=pos):
                    chunk_copy(e, start + c * COMBINE_CHUNK, buf, pos + c * COMBINE_CHUNK).start()
                    return carry
                lax.fori_loop(0, nch, issue, 0)
                pos = pos + nch * COMBINE_CHUNK

    @pl.when(i == 0)
    def _():
        ybuf_ref[...] = jnp.zeros_like(ybuf_ref)
        fetch(0, 0)

    @pl.when(i + 1 < pl.num_programs(0))
    def _():
        fetch(i + 1, 1 - cur)

    def finish(ffn):
        out = x1_ref[...] + mod_ref[0][5:6] * ffn
        ms = jnp.mean(out * out, axis=-1, keepdims=True)
        o_ref[...] = (out * lax.rsqrt(ms + EPS)) * g_ref[...]

    def apply_selection(sel, k0, width):
        lo, hi = _unpack_pair(ybuf_ref[cur, pl.ds(k0, width), :])
        return jnp.concatenate([jnp.dot(sel, lo, preferred_element_type=F32),
                                jnp.dot(sel, hi, preferred_element_type=F32)], axis=1)

    @pl.when(wide_ref[i] == 0)
    def _():
        for e in range(N_EXPERTS):
            window_copy(e, jnp.int32(0), cur).wait()

            @pl.when(needs_tail(i, e))
            def _(e=e):
                window_tail_copy(e, jnp.int32(0), cur).wait()
        expert_lane = lax.broadcasted_iota(I32, (1, N_EXPERTS), 1)
        starts = jnp.zeros((1, N_EXPERTS), I32)
        for e in range(N_EXPERTS):
            starts = jnp.where(expert_lane == e, start_ref[i, e], starts)
        slots = slot_ref[...]
        rows_in_window = jnp.where(slots >= 0, (slots - starts).astype(F32), -1.0)
        window_of_lane = lax.broadcasted_iota(I32, (N_EXPERTS, n_window_rows), 1) // COMBINE_WINDOW
        spread = (window_of_lane == lax.broadcasted_iota(I32, (N_EXPERTS, n_window_rows), 0))
        target = jnp.dot(rows_in_window.astype(BF16), spread.astype(BF16),
                         preferred_element_type=F32)
        lane_row = (lax.broadcasted_iota(I32, (tm, n_window_rows), 1) % COMBINE_WINDOW).astype(F32)
        sel = jnp.where(target == lane_row, 1.0, 0.0)
        ffn = None
        for k0 in range(0, n_window_rows, kblock):
            part = apply_selection(sel[:, k0:k0 + kblock], k0, kblock)
            ffn = part if ffn is None else ffn + part
        finish(ffn)

    @pl.when(wide_ref[i] != 0)
    def _():
        pos = jnp.int32(0)
        begins = []
        for e in range(N_EXPERTS):
            begins.append(pos)
            pos = pos + nch_ref[i, e] * COMBINE_CHUNK
        begins.append(pos)

        def drain(c, carry):
            chunk_copy(0, jnp.int32(0), cur, jnp.int32(0)).wait()
            return carry
        lax.fori_loop(0, pos // COMBINE_CHUNK, drain, 0)

        expert_lane = lax.broadcasted_iota(I32, (1, N_EXPERTS), 1)
        expert_row = lax.broadcasted_iota(I32, (N_EXPERTS, 1), 0)
        shift = jnp.zeros((1, N_EXPERTS), I32)
        range_lo = jnp.zeros((N_EXPERTS, 1), I32)
        range_hi = jnp.zeros((N_EXPERTS, 1), I32)
        for e in range(N_EXPERTS):
            shift = jnp.where(expert_lane == e, begins[e] - start_ref[i, e], shift)
            range_lo = jnp.where(expert_row == e, begins[e], range_lo)
            range_hi = jnp.where(expert_row == e, begins[e + 1], range_hi)
        slots = slot_ref[...]
        packed_row = slots + shift
        row_hi = jnp.where(slots >= 0, packed_row // ROW_SPLIT, -1).astype(F32).astype(BF16)
        row_lo = jnp.where(slots >= 0, packed_row % ROW_SPLIT, 0).astype(F32).astype(BF16)

        def selection(k0, width):
            col = k0 + lax.broadcasted_iota(I32, (N_EXPERTS, width), 1)
            spread = ((col >= range_lo) & (col < range_hi)).astype(BF16)
            want_hi = jnp.dot(row_hi, spread, preferred_element_type=F32)
            want_lo = jnp.dot(row_lo, spread, preferred_element_type=F32)
            here = k0 + lax.broadcasted_iota(I32, (1, width), 1)
            hit = ((want_hi == (here // ROW_SPLIT).astype(F32))
                   & (want_lo == (here % ROW_SPLIT).astype(F32)) & (here < pos))
            return jnp.where(hit, 1.0, 0.0)

        ffn_ref[...] = apply_selection(selection(0, first_k), 0, first_k)

        def kstep(kb, carry):
            k0 = pl.multiple_of(kb * kblock, kblock)
            ffn_ref[...] += apply_selection(selection(k0, kblock), k0, kblock)
            return carry
        lax.fori_loop(first_k // kblock, (pos + kblock - 1) // kblock, kstep, 0)
        finish(ffn_ref[...])


def _combine(x1, mod3, mod_row, g_final, slot_t, tile_start, tile_nch, tile_wide, y):
    t = x1.shape[0]
    tm = TOKEN_TILE
    first_k = 2 * tm + N_EXPERTS * COMBINE_CHUNK
    kblock = 256
    max_rows = N_EXPERTS * (tm + 2 * COMBINE_CHUNK)
    max_rows = -(-max_rows // kblock) * kblock
    row = lambda i, *_: (i, 0)
    grid_spec = pltpu.PrefetchScalarGridSpec(
        num_scalar_prefetch=3,
        grid=(t // tm,),
        in_specs=[pl.BlockSpec((tm, D_MODEL), row),
                  pl.BlockSpec((1, N_MOD, D_MODEL), lambda i, *_: (mod_row(i), 0, 0)),
                  pl.BlockSpec((1, D_MODEL), lambda i, *_: (0, 0)),
                  pl.BlockSpec((tm, N_EXPERTS), row),
                  pl.BlockSpec(memory_space=pl.ANY)],
        out_specs=pl.BlockSpec((tm, D_MODEL), row),
        scratch_shapes=[pltpu.VMEM((2, max_rows, D_MODEL // 2), I32),
                        pltpu.VMEM((tm, D_MODEL), F32),
                        pltpu.SemaphoreType.DMA((2,))],
    )
    return pl.pallas_call(
        functools.partial(_combine_kernel, first_k=first_k, kblock=kblock),
        grid_spec=grid_spec,
        out_shape=jax.ShapeDtypeStruct((t, D_MODEL), F32),
        compiler_params=pltpu.CompilerParams(
            dimension_semantics=("arbitrary",),
            vmem_limit_bytes=_vmem_limit(2 * max_rows * D_MODEL * 2 + 16 * tm * D_MODEL * 4)),
        name="combine",
    )(tile_start, tile_nch, tile_wide, x1, mod3, g_final, slot_t, y)


def _rope_tables(n):
    f32 = np.float32
    rows = n // GRID_W
    row = np.repeat(np.arange(rows, dtype=f32), GRID_W)
    col = np.tile(np.arange(GRID_W, dtype=f32), rows)
    inv = (f32(ROPE_THETA) ** (-np.arange(ROPE_FREQS, dtype=f32) / f32(ROPE_FREQS))).astype(f32)
    ang_r = row[:, None] * inv
    ang_c = col[:, None] * inv
    zero = np.zeros_like(ang_r)
    cos = np.concatenate([np.cos(ang_r)] * 2 + [np.cos(ang_c)] * 2, axis=1)
    sin_a = np.concatenate([-np.sin(ang_r), zero, -np.sin(ang_c), zero], axis=1)
    sin_b = np.concatenate([zero, np.sin(ang_r), zero, np.sin(ang_c)], axis=1)
    return tuple(jnp.asarray(t, F32) for t in (cos, sin_a, sin_b))


def _token_group(x, mod_mix, mod3, mod_row, seq, weights, rope_tabs, ctx_kv, late_mod=None):
    (norm_mix, w_in, sink, pool_w, pool_scale, w_out, norm_ffn, w_router,
     w_gate, w_up, w_down, norm_final) = weights
    b = x.shape[0]
    t = b * seq
    x2 = x.reshape(t, D_MODEL)
    q, k, v, p, *state = _in_projection(x2, mod_mix, mod_row(PROJ_TILE), norm_mix, w_in,
                                        rope_tabs, w_out, late_mod)
    if rope_tabs is None:
        late = state.pop()
        mod3 = jnp.concatenate([mod_mix.reshape(MOD_ROWS, -1), late], axis=1).reshape(
            MOD_ROWS, N_MOD, D_MODEL)
    w_in_bf = w_in if rope_tabs is not None else state.pop()
    w_out_bf = w_out if rope_tabs is not None else state.pop()
    if ctx_kv is None:
        attn = _context_attention(q, k, v, sink, seq)
    else:
        attn = _latent_attention(q, k, v, ctx_kv[0], ctx_kv[1], sink, seq)
    x1, h_packed, aff = _out_projection(attn, p, x2, mod3, mod_row(PROJ_TILE), norm_ffn,
                                        w_out_bf, w_router, pool_w, pool_scale, seq)

    cap = EC_FACTOR * t // N_EXPERTS
    idx, slot_t, off3 = _routing(aff, cap)
    table = h_packed.reshape(-1, LANES)
    per_range = N_EXPERTS // FFN_RANGES
    xs = []
    for r in range(FFN_RANGES):
        ids = _packed_row_ids(idx[r * per_range:(r + 1) * per_range], cap)
        xs.append(_gather_rows(table, ids).reshape(per_range, HQ_TILES, cap, LANES))

    chunks_per_tile = TOKEN_TILE // LANES
    tile_off = off3[:, ::chunks_per_tile, 0]
    tile_end = jnp.concatenate([tile_off[:, 1:], jnp.full((N_EXPERTS, 1), cap, I32)], axis=1)
    tile_start = (tile_off // SUBLANES) * SUBLANES
    tile_nch = jnp.where(tile_end > tile_off,
                         (tile_end - tile_start + COMBINE_CHUNK - 1) // COMBINE_CHUNK, 0)
    tile_wide = jnp.any(tile_end - tile_start > COMBINE_WINDOW, axis=0).astype(I32)

    def finish(y):
        out = _combine(x1, mod3, mod_row(TOKEN_TILE), norm_final, slot_t, tile_start.T,
                       tile_nch.T, tile_wide, y)
        return out.reshape(b, seq, D_MODEL)
    return xs, finish, state, (w_in_bf, w_out_bf, mod3)


def kernel(x_prompt, x_sample, c, cache_k, cache_v, c_ctx, w_ada, b_ada, norm_mix, w_in,
           sink_logits, pool_w, pool_scale, w_out, norm_ffn, w_router, w_gate, w_up, w_down,
           norm_final):
    n_b, seq, _ = x_prompt.shape
    n_db, n_lat, _ = x_sample.shape
    assert 1 + n_db <= MOD_ROWS and seq == TOKEN_TILE and n_lat % PROJ_TILE == 0
    assert w_in.shape[0] == 1 and cache_k.shape[1] == 1
    assert (n_b * seq) % PROJ_TILE == 0

    cond = jnp.concatenate(
        [c_ctx[None, :], c, jnp.zeros((MOD_ROWS - 1 - n_db, D_MODEL), F32)], axis=0)
    mixer_cols = 2 * D_MODEL
    mod_mix = _modulation(cond, w_ada[0], b_ada[0], mixer_cols).reshape(MOD_ROWS, 2, D_MODEL)

    w_router_bf = jnp.pad(w_router[0], ((0, 0), (0, LANES - N_EXPERTS))).astype(BF16)
    weights = (norm_mix[0][None, :], w_in[0], sink_logits[0], pool_w[0].astype(BF16),
               pool_scale[0][None, :], w_out[0], norm_ffn[0][None, :], w_router_bf,
               w_gate.reshape(w_gate.shape[1:]), w_up.reshape(w_up.shape[1:]),
               w_down.reshape(w_down.shape[1:]), norm_final[None, :])

    xs_p, finish_p, (k_p, v_p), (w_in_bf, w_out_bf, mod3) = _token_group(
        x_prompt, mod_mix, None, lambda tile: (lambda i: 0), seq, weights, None, None,
        late_mod=(cond, w_ada[0], b_ada[0], mixer_cols))
    weights = weights[:1] + (w_in_bf,) + weights[2:5] + (w_out_bf,) + weights[6:]

    ck = cache_k.reshape(-1, HEAD_DIM)
    cv = cache_v.reshape(-1, HEAD_DIM)
    xs_l, finish_l, _, _ = _token_group(
        x_sample, mod_mix, mod3, lambda tile: (lambda i: 1 + i // (n_lat // tile)), n_lat, weights,
        _rope_tables(n_lat), (ck, cv))

    ys = None
    for r in range(FFN_RANGES):
        ys = _expert_ffn([xs_p[r], xs_l[r]], weights[8], weights[9], weights[10],
                         r * (N_EXPERTS // FFN_RANGES), ys)
    y_prompt = finish_p(ys[0])
    y_sample = finish_l(ys[1])

    state_k = k_p.reshape(n_b, 1, seq, N_KV_HEADS, HEAD_DIM)
    state_v = v_p.reshape(n_b, 1, seq, N_KV_HEADS, HEAD_DIM)
    return (y_prompt, y_sample, state_k, state_v)
```

```python
import functools

import jax
import jax.numpy as jnp
import numpy as np
from jax import lax
from jax.experimental import pallas as pl
from jax.experimental.pallas import tpu as pltpu
from jax.experimental.pallas import tpu_sc as plsc

F32 = jnp.float32
BF16 = jnp.bfloat16
I32 = jnp.int32

D_MODEL = 2048
N_HEADS = 8
N_KV_HEADS = 2
HEAD_DIM = 128
Q_PER_KV = N_HEADS // N_KV_HEADS
ATTN_WIDTH = N_HEADS * HEAD_DIM
KV_WIDTH = N_KV_HEADS * HEAD_DIM
POOL_WIDTH = D_MODEL - ATTN_WIDTH
POOL_SIZES = (2, 4, 8, 16)
POOL_GROUP = POOL_WIDTH // len(POOL_SIZES)
IN_WIDTH = ATTN_WIDTH + 2 * KV_WIDTH + POOL_WIDTH
WINDOW = 128
BLOCK = 128
GRID_W = 64
ROPE_THETA = 10000.0
ROPE_FREQS = HEAD_DIM // 4
N_EXPERTS = 16
EC_FACTOR = 2
D_EXPERT = 1024
N_MOD = 6
EPS = 1e-6
NEG = -1e30
LOG2_E = 1.4426950408889634
ATTN_SCALE = HEAD_DIM ** -0.5 * LOG2_E

LANES = 128
SUBLANES = 8
BF16_ROWS = 16
VMEM_CAP = 64 * 1024 * 1024
SC_CORES = 2
SC_SUBCORES = 16

MOD_ROWS = 8
TOKEN_TILE = 256
PROJ_TILE = 512
SUB_TILE = 256
HQ_TILES = D_MODEL // 2 // LANES + 1
GATHER_CHUNK = 128
FFN_RANGES = 2
POOL_HALO = 8
COMBINE_CHUNK = BF16_ROWS
COMBINE_WINDOW = 64
ROW_SPLIT = 32


def _vmem_limit(nbytes):
    return int(min(VMEM_CAP - (4 << 20), max(nbytes, 16 << 20)))


def _mod_block(c_ref, w_ref, b_ref):
    c = c_ref[...]
    s = c * jax.nn.sigmoid(c)
    return jnp.dot(s.astype(BF16), w_ref[...].astype(BF16),
                   preferred_element_type=F32) + b_ref[...]


def _mod_kernel(c_ref, w_ref, b_ref, o_ref):
    o_ref[...] = _mod_block(c_ref, w_ref, b_ref)


def _modulation(cond, w_ada, b_ada, n):
    tn = 1024
    return pl.pallas_call(
        _mod_kernel,
        grid=(n // tn,),
        in_specs=[pl.BlockSpec((MOD_ROWS, D_MODEL), lambda j: (0, 0)),
                  pl.BlockSpec((D_MODEL, tn), lambda j: (0, j)),
                  pl.BlockSpec((1, tn), lambda j: (0, j))],
        out_specs=pl.BlockSpec((MOD_ROWS, tn), lambda j: (0, j)),
        out_shape=jax.ShapeDtypeStruct((MOD_ROWS, n), F32),
        compiler_params=pltpu.CompilerParams(
            vmem_limit_bytes=_vmem_limit(3 * D_MODEL * tn * 4)),
        name="modulation",
    )(cond, w_ada, b_ada.reshape(1, -1))


def _norm_mod(x, g, shift, scale):
    ms = jnp.mean(x * x, axis=-1, keepdims=True)
    y = x * lax.rsqrt(ms + EPS)
    return (y * g) * (1.0 + scale) + shift


def _inproj_kernel(*refs, rope):
    if rope:
        x_ref, mod_ref, g_ref, w_ref, cos_ref, sa_ref, sb_ref, q_ref, k_ref, v_ref, p_ref = refs
    else:
        (x_ref, mod_ref, g_ref, w_ref, wo_slab_ref, wi_slab_ref, cond_ref, wada_ref, bada_ref,
         q_ref, k_ref, v_ref, p_ref, ks_ref, vs_ref, wo_bf_ref, wi_bf_ref, late_mod_ref) = refs
        wo_bf_ref[...] = wo_slab_ref[...].astype(BF16)
        wi_bf_ref[...] = wi_slab_ref[...].astype(BF16)
        late_mod_ref[...] = _mod_block(cond_ref, wada_ref, bada_ref)
    mod = mod_ref[0]
    for s in range(x_ref.shape[0] // SUB_TILE):
        rows = slice(s * SUB_TILE, (s + 1) * SUB_TILE)
        h = _norm_mod(x_ref[rows, :], g_ref[...], mod[0:1], mod[1:2])
        u = jnp.dot(h.astype(w_ref.dtype), w_ref[...], preferred_element_type=F32)

        def rot(xh, rows=rows):
            return (xh * cos_ref[rows, :] + pltpu.roll(xh, LANES - ROPE_FREQS, 1) * sa_ref[rows, :]
                    + pltpu.roll(xh, ROPE_FREQS, 1) * sb_ref[rows, :])

        for hd in range(N_HEADS):
            xh = u[:, hd * HEAD_DIM:(hd + 1) * HEAD_DIM] * ATTN_SCALE
            q_ref[rows, hd * HEAD_DIM:(hd + 1) * HEAD_DIM] = (rot(xh) if rope else xh).astype(BF16)
        for hd in range(N_KV_HEADS):
            lo = ATTN_WIDTH + hd * HEAD_DIM
            xh = u[:, lo:lo + HEAD_DIM]
            k_ref[rows, hd * HEAD_DIM:(hd + 1) * HEAD_DIM] = rot(xh) if rope else xh
        v_ref[rows, :] = u[:, ATTN_WIDTH + KV_WIDTH:ATTN_WIDTH + 2 * KV_WIDTH]
        p_ref[rows, :] = u[:, ATTN_WIDTH + 2 * KV_WIDTH:]
        if not rope:
            for hd in range(N_KV_HEADS):
                state_rows = pl.ds(s * SUB_TILE * N_KV_HEADS + hd, SUB_TILE, stride=N_KV_HEADS)
                lo = ATTN_WIDTH + hd * HEAD_DIM
                ks_ref[state_rows, :] = u[:, lo:lo + HEAD_DIM]
                vs_ref[state_rows, :] = u[:, lo + KV_WIDTH:lo + KV_WIDTH + HEAD_DIM]


def _in_projection(x2, mod3, mod_row, g, w_in, rope_tabs, w_out, late_mod):
    t = x2.shape[0]
    tm = PROJ_TILE
    rope = rope_tabs is not None
    row = lambda i: (i, 0)
    w_mode = None if rope else pl.Buffered(1)
    in_specs = [pl.BlockSpec((tm, D_MODEL), row),
                pl.BlockSpec((1,) + mod3.shape[1:], lambda i: (mod_row(i), 0, 0)),
                pl.BlockSpec((1, D_MODEL), lambda i: (0, 0)),
                pl.BlockSpec((D_MODEL, IN_WIDTH), lambda i: (0, 0), pipeline_mode=w_mode)]
    args = [x2, mod3, g, w_in]
    if rope:
        n_seq = rope_tabs[0].shape[0]
        seq_blocks = n_seq // tm
        for tab in rope_tabs:
            in_specs.append(pl.BlockSpec((tm, HEAD_DIM), lambda i: (i % seq_blocks, 0)))
            args.append(tab)
    else:
        slab = D_MODEL // (t // tm)
        assert slab * (t // tm) == D_MODEL and slab % BF16_ROWS == 0
        for w in (w_out, w_in):
            in_specs.append(pl.BlockSpec((slab, w.shape[1]), row))
            args.append(w)
        cond, w_ada, b_ada, first_col = late_mod
        late_cols = (w_ada.shape[1] - first_col) // (t // tm)
        first_block = first_col // late_cols
        assert first_block * late_cols == first_col and late_cols % LANES == 0
        in_specs += [pl.BlockSpec((MOD_ROWS, D_MODEL), lambda i: (0, 0)),
                     pl.BlockSpec((D_MODEL, late_cols), lambda i: (0, first_block + i)),
                     pl.BlockSpec((1, late_cols), lambda i: (0, first_block + i))]
        args += [cond, w_ada, b_ada.reshape(1, -1)]
    out_specs = [pl.BlockSpec((tm, ATTN_WIDTH), row),
                 pl.BlockSpec((tm, KV_WIDTH), row),
                 pl.BlockSpec((tm, KV_WIDTH), row),
                 pl.BlockSpec((tm, POOL_WIDTH), row)]
    out_shape = [jax.ShapeDtypeStruct((t, ATTN_WIDTH), BF16),
                 jax.ShapeDtypeStruct((t, KV_WIDTH), F32),
                 jax.ShapeDtypeStruct((t, KV_WIDTH), F32),
                 jax.ShapeDtypeStruct((t, POOL_WIDTH), F32)]
    if not rope:
        for _ in range(2):
            out_specs.append(pl.BlockSpec((tm * N_KV_HEADS, HEAD_DIM), row))
            out_shape.append(jax.ShapeDtypeStruct((t * N_KV_HEADS, HEAD_DIM), F32))
        for w in (w_out, w_in):
            out_specs.append(pl.BlockSpec((slab, w.shape[1]), row))
            out_shape.append(jax.ShapeDtypeStruct(w.shape, BF16))
        out_specs.append(pl.BlockSpec((MOD_ROWS, late_cols), lambda i: (0, i)))
        out_shape.append(jax.ShapeDtypeStruct((MOD_ROWS, w_ada.shape[1] - first_col), F32))
    return pl.pallas_call(
        functools.partial(_inproj_kernel, rope=rope),
        grid=(t // tm,),
        in_specs=in_specs,
        out_specs=out_specs,
        out_shape=out_shape,
        compiler_params=pltpu.CompilerParams(
            vmem_limit_bytes=_vmem_limit(2 * D_MODEL * IN_WIDTH * 2 + 24 * tm * D_MODEL * 4)),
        name="in_projection",
    )(*args)


def _softmax_pv(s_list, v_list, sink_col):
    m = sink_col
    for s in s_list:
        m = jnp.maximum(m, jnp.max(s, axis=-1, keepdims=True))
    denom = jnp.exp2(sink_col - m)
    out = None
    for s, v in zip(s_list, v_list):
        e = jnp.exp2(s - m)
        if v.shape[1] == HEAD_DIM:
            denom = denom + jnp.sum(e, axis=-1, keepdims=True)
        o = jnp.dot(e.astype(BF16), v, preferred_element_type=F32)
        out = o if out is None else out + o
    if out.shape[1] > HEAD_DIM:
        denom = denom + out[:, HEAD_DIM:HEAD_DIM + 1]
    return out[:, 0:HEAD_DIM] * (1.0 / denom)


def _with_ones(v):
    return jnp.concatenate([v.astype(BF16), jnp.ones(v.shape, BF16)], axis=1)


def _stack_heads(q, kv):
    return jnp.concatenate(
        [q[:, (kv * Q_PER_KV + g) * HEAD_DIM:(kv * Q_PER_KV + g + 1) * HEAD_DIM]
         for g in range(Q_PER_KV)], axis=0)


def _sink_column(sink_ref, kv, rows):
    r = lax.broadcasted_iota(I32, (Q_PER_KV * rows, 1), 0)
    col = jnp.zeros((Q_PER_KV * rows, 1), F32)
    for g in range(Q_PER_KV):
        col = jnp.where((r >= g * rows) & (r < (g + 1) * rows), sink_ref[kv * Q_PER_KV + g], col)
    return col * LOG2_E


def _qk(q, k):
    return lax.dot_general(q, k, (((1,), (1,)), ((), ())), preferred_element_type=F32)


def _ctx_attn_kernel(sink_ref, q_ref, k_ref, v_ref, o_ref, *, seq):
    rows = seq
    for sq in range(q_ref.shape[0] // seq):
        rs = slice(sq * seq, (sq + 1) * seq)
        q = q_ref[rs, :]
        for kv in range(N_KV_HEADS):
            kh = k_ref[rs, kv * HEAD_DIM:(kv + 1) * HEAD_DIM].astype(BF16)
            vh = v_ref[rs, kv * HEAD_DIM:(kv + 1) * HEAD_DIM].astype(BF16)
            qs = _stack_heads(q, kv)
            s = _qk(qs, kh)
            o = _softmax_pv([s], [vh], _sink_column(sink_ref, kv, rows))
            for g in range(Q_PER_KV):
                hd = kv * Q_PER_KV + g
                o_ref[rs, hd * HEAD_DIM:(hd + 1) * HEAD_DIM] = (
                    o[g * rows:(g + 1) * rows].astype(BF16))


def _context_attention(q, k, v, sink, seq):
    t = q.shape[0]
    row = lambda b: (b, 0)
    per_step = 2 if (t // seq) % 2 == 0 else 1
    tm = per_step * seq
    return pl.pallas_call(
        functools.partial(_ctx_attn_kernel, seq=seq),
        grid=(t // tm,),
        in_specs=[pl.BlockSpec(memory_space=pltpu.SMEM),
                  pl.BlockSpec((tm, ATTN_WIDTH), row),
                  pl.BlockSpec((tm, KV_WIDTH), row),
                  pl.BlockSpec((tm, KV_WIDTH), row)],
        out_specs=pl.BlockSpec((tm, ATTN_WIDTH), row),
        out_shape=jax.ShapeDtypeStruct((t, ATTN_WIDTH), BF16),
        name="context_attention",
    )(sink, q, k, v)


def _lat_attn_kernel(sink_ref, q_ref, k_ref, v_ref, ck_ref, cv_ref, o_ref, *, n_seq):
    i = pl.program_id(1)
    band = 3 * BLOCK
    start = pl.multiple_of(jnp.clip((i - 1) * BLOCK, 0, n_seq - band), BLOCK)
    rows = Q_PER_KV * BLOCK
    qpos = i * BLOCK + lax.broadcasted_iota(I32, (rows, band), 0) % BLOCK
    kpos = start + lax.broadcasted_iota(I32, (rows, band), 1)
    mask = jnp.abs(kpos - qpos) <= WINDOW
    q = q_ref[...]
    for kv in range(N_KV_HEADS):
        cols = slice(kv * HEAD_DIM, (kv + 1) * HEAD_DIM)
        kb = k_ref[pl.ds(start, band), cols].astype(BF16)
        vb = _with_ones(v_ref[pl.ds(start, band), cols])
        cached = pl.ds(kv, ck_ref.shape[0] // N_KV_HEADS, stride=N_KV_HEADS)
        ck = ck_ref[cached, :].astype(BF16)
        cv = _with_ones(cv_ref[cached, :])
        qs = _stack_heads(q, kv)
        s_loc = jnp.where(mask, _qk(qs, kb), NEG)
        s_ctx = _qk(qs, ck)
        o = _softmax_pv([s_loc, s_ctx], [vb, cv], _sink_column(sink_ref, kv, BLOCK))
        for g in range(Q_PER_KV):
            hd = kv * Q_PER_KV + g
            o_ref[:, hd * HEAD_DIM:(hd + 1) * HEAD_DIM] = o[g * BLOCK:(g + 1) * BLOCK].astype(BF16)


def _latent_attention(q, k, v, ck, cv, sink, n_seq):
    t = q.shape[0]
    nb = n_seq // BLOCK
    cache_rows = ck.shape[0] // (t // n_seq)
    return pl.pallas_call(
        functools.partial(_lat_attn_kernel, n_seq=n_seq),
        grid=(t // n_seq, nb),
        in_specs=[pl.BlockSpec(memory_space=pltpu.SMEM),
                  pl.BlockSpec((BLOCK, ATTN_WIDTH), lambda b, i: (b * nb + i, 0)),
                  pl.BlockSpec((n_seq, KV_WIDTH), lambda b, i: (b, 0)),
                  pl.BlockSpec((n_seq, KV_WIDTH), lambda b, i: (b, 0)),
                  pl.BlockSpec((cache_rows, HEAD_DIM), lambda b, i: (b, 0)),
                  pl.BlockSpec((cache_rows, HEAD_DIM), lambda b, i: (b, 0))],
        out_specs=pl.BlockSpec((BLOCK, ATTN_WIDTH), lambda b, i: (b * nb + i, 0)),
        out_shape=jax.ShapeDtypeStruct((t, ATTN_WIDTH), BF16),
        name="latent_attention",
    )(sink, q, k, v, ck, cv)


def _pool_group(p_ref, r0, seq, w_ref, s_ref, g):
    n = SUB_TILE
    rows = n + 2 * POOL_HALO
    static = isinstance(r0, int)
    t0 = r0 % seq
    t = t0 + lax.broadcasted_iota(I32, (n, 1), 0)
    has_top = t0 > 0
    has_bottom = t0 + n < seq
    zeros = jnp.zeros((POOL_HALO, POOL_GROUP), F32)
    w = POOL_SIZES[g]
    cols = slice(g * POOL_GROUP, (g + 1) * POOL_GROUP)
    pg = p_ref[pl.ds(r0, n), cols]
    if static:
        top = p_ref[r0 - POOL_HALO:r0, cols] if has_top else zeros
        bottom = p_ref[r0 + n:r0 + n + POOL_HALO, cols] if has_bottom else zeros
    else:
        top_row = pl.multiple_of(jnp.maximum(r0 - POOL_HALO, 0), POOL_HALO)
        bottom_row = pl.multiple_of(jnp.minimum(r0 + n, p_ref.shape[0] - POOL_HALO), POOL_HALO)
        top = jnp.where(has_top, p_ref[pl.ds(top_row, POOL_HALO), cols], 0.0)
        bottom = jnp.where(has_bottom, p_ref[pl.ds(bottom_row, POOL_HALO), cols], 0.0)
    x = jnp.concatenate([top, pg, bottom], axis=0)
    fwd = x
    span = 1
    while span < w // 2:
        fwd = fwd + pltpu.roll(fwd, rows - span, 0)
        span *= 2
    if (w // 2) % SUBLANES == 0:
        wsum = fwd[POOL_HALO - w // 2:POOL_HALO - w // 2 + n] + fwd[POOL_HALO:POOL_HALO + n]
    else:
        wsum = (fwd + pltpu.roll(fwd, w // 2, 0))[POOL_HALO:POOL_HALO + n]
    lo = jnp.maximum(t - w // 2, 0)
    hi = jnp.minimum(t + w - w // 2, seq)
    inv_cnt = 1.0 / (hi - lo).astype(F32)
    mixed = wsum * inv_cnt - pg
    y = jnp.dot(mixed.astype(BF16), w_ref[g], preferred_element_type=F32)
    return (y * s_ref[:, cols]).astype(BF16)


def _pack_pair(lo, hi):
    return lax.bitcast_convert_type(pltpu.pack_elementwise([lo, hi], packed_dtype=BF16), I32)


def _unpack_pair(words):
    lo = pltpu.unpack_elementwise(words, index=0, packed_dtype=BF16, unpacked_dtype=F32)
    hi = pltpu.unpack_elementwise(words, index=1, packed_dtype=BF16, unpacked_dtype=F32)
    return lo, hi


def _outproj_kernel(a_ref, p_ref, x_ref, mod_ref, g_ref, wo_ref, wr_ref, pw_ref, ps_ref,
                    x1_ref, h_ref, aff_ref, *, seq):
    mod = mod_ref[0]
    groups = SUB_TILE // SUBLANES
    half = D_MODEL // 2
    tm = x_ref.shape[0]
    steps_per_p_block = p_ref.shape[0] // tm
    for s in range(tm // SUB_TILE):
        rows = slice(s * SUB_TILE, (s + 1) * SUB_TILE)
        grp = slice(s * groups, (s + 1) * groups)
        r0 = s * SUB_TILE
        if steps_per_p_block > 1:
            r0 = pl.multiple_of((pl.program_id(0) % steps_per_p_block) * tm + r0, SUB_TILE)
        pooled = jnp.concatenate([_pool_group(p_ref, r0, seq, pw_ref, ps_ref, g)
                                  for g in range(len(POOL_SIZES))], axis=1)
        mix = (jnp.dot(a_ref[rows, :], wo_ref[0:ATTN_WIDTH, :], preferred_element_type=F32)
               + jnp.dot(pooled, wo_ref[ATTN_WIDTH:D_MODEL, :], preferred_element_type=F32))
        x1 = x_ref[rows, :] + mod[2:3] * mix
        x1_ref[rows, :] = x1
        h = _norm_mod(x1, g_ref[...], mod[3:4], mod[4:5])
        logits = jnp.dot(h.astype(BF16), wr_ref[...], preferred_element_type=F32)
        lane = lax.broadcasted_iota(I32, logits.shape, 1)
        logits = jnp.where(lane < N_EXPERTS, logits, -jnp.inf)
        m = jnp.max(logits, axis=-1, keepdims=True)
        e = jnp.exp(logits - m)
        aff = e / jnp.sum(e, axis=-1, keepdims=True)
        aff_ref[rows, :] = aff[:, 0:N_EXPERTS]
        for c in range(HQ_TILES - 1):
            words = _pack_pair(h[:, c * LANES:(c + 1) * LANES],
                               h[:, half + c * LANES:half + (c + 1) * LANES])
            h_ref[grp, c * SUBLANES:(c + 1) * SUBLANES, :] = (
                words.reshape(groups, SUBLANES, LANES))
        h_ref[grp, (HQ_TILES - 1) * SUBLANES:, :] = (
            lax.bitcast_convert_type(aff, I32).reshape(groups, SUBLANES, LANES))


def _out_projection(attn, p, x2, mod3, mod_row, g, w_out_bf, w_router, pool_w, pool_scale, seq):
    t = x2.shape[0]
    tm = PROJ_TILE
    row = lambda i: (i, 0)
    p_rows = max(tm, seq)
    steps_per_p_block = p_rows // tm
    return pl.pallas_call(
        functools.partial(_outproj_kernel, seq=seq),
        grid=(t // tm,),
        in_specs=[pl.BlockSpec((tm, ATTN_WIDTH), row),
                  pl.BlockSpec((p_rows, POOL_WIDTH), lambda i: (i // steps_per_p_block, 0)),
                  pl.BlockSpec((tm, D_MODEL), row),
                  pl.BlockSpec((1, N_MOD, D_MODEL), lambda i: (mod_row(i), 0, 0)),
                  pl.BlockSpec((1, D_MODEL), lambda i: (0, 0)),
                  pl.BlockSpec((D_MODEL, D_MODEL), lambda i: (0, 0)),
                  pl.BlockSpec((D_MODEL, LANES), lambda i: (0, 0)),
                  pl.BlockSpec((len(POOL_SIZES), POOL_GROUP, POOL_GROUP), lambda i: (0, 0, 0)),
                  pl.BlockSpec((1, POOL_WIDTH), lambda i: (0, 0))],
        out_specs=[pl.BlockSpec((tm, D_MODEL), row),
                   pl.BlockSpec((tm // SUBLANES, HQ_TILES * SUBLANES, LANES), lambda i: (i, 0, 0)),
                   pl.BlockSpec((tm, N_EXPERTS), row)],
        out_shape=[jax.ShapeDtypeStruct((t, D_MODEL), F32),
                   jax.ShapeDtypeStruct((t // SUBLANES, HQ_TILES * SUBLANES, LANES), I32),
                   jax.ShapeDtypeStruct((t, N_EXPERTS), F32)],
        compiler_params=pltpu.CompilerParams(
            vmem_limit_bytes=_vmem_limit(2 * D_MODEL * D_MODEL * 2 + 24 * tm * D_MODEL * 4)),
        name="out_projection",
    )(attn, p, x2, mod3, g, w_out_bf, w_router, pool_w, pool_scale)


def _route_kernel(a_ref, idx_ref, slot_ref, off_ref, slot_scr, *, cap):
    a = a_ref[...]
    n_e, n_c, _ = a.shape
    rows = n_e * n_c

    def enough(cand):
        cand_f = lax.bitcast_convert_type(cand, F32)
        cnt = jnp.sum(jnp.sum((a >= cand_f).astype(F32), axis=1, keepdims=True),
                      axis=2, keepdims=True)
        return cnt >= cap

    def two_bits(it, thr):
        low = 28 - 2 * it
        for setting in (1, 2, 3):
            cand = thr | jnp.left_shift(jnp.int32(setting), low)
            best = jnp.where(enough(cand), cand, thr if setting == 1 else best)
        return best

    top = jnp.full((n_e, 1, 1), 1 << 30, I32)
    thr = jnp.where(enough(top), top, 0)
    thr = lax.fori_loop(0, 15, two_bits, thr)
    thr_f = lax.bitcast_convert_type(thr, F32)
    gt = (a > thr_f).astype(F32).reshape(rows, LANES)
    eq = (a == thr_f).astype(F32).reshape(rows, LANES)

    li = lax.broadcasted_iota(I32, (LANES, LANES), 0)
    lj = lax.broadcasted_iota(I32, (LANES, LANES), 1)
    upper_incl = (li <= lj).astype(BF16)
    ci = lax.broadcasted_iota(I32, (n_c, n_c), 0)
    cj = lax.broadcasted_iota(I32, (n_c, n_c), 1)
    before = (cj < ci).astype(BF16)
    whole = jnp.ones((n_c, n_c), BF16)

    def over_chunks(mat, col):
        wide = jnp.broadcast_to(col, (rows, LANES)).astype(BF16)
        side_by_side = jnp.concatenate([wide[e * n_c:(e + 1) * n_c] for e in range(n_e)], axis=1)
        res = jnp.dot(mat, side_by_side, preferred_element_type=F32)
        return jnp.concatenate([res[:, e * LANES:(e + 1) * LANES] for e in range(n_e)], axis=0)

    def prefix(x):
        incl = jnp.dot(x.astype(BF16), upper_incl, preferred_element_type=F32)
        tot = incl[:, LANES - 1:LANES]
        return incl, tot, over_chunks(before, tot)

    n_gt = over_chunks(whole, jnp.sum(gt, axis=1, keepdims=True))
    need = cap - n_gt
    incl_eq, _, off_eq = prefix(eq)
    rank_eq = off_eq + incl_eq - eq
    sel = jnp.where((eq > 0) & (rank_eq < need), 1.0, gt)
    incl, tot, off = prefix(sel)
    slot = off + incl - sel
    slot_scr[...] = jnp.where(sel > 0, slot, -1.0)
    for c in range(n_c):
        per_expert = slot_scr[pl.ds(c, n_e, stride=n_c), :]
        slot_ref[c * LANES:(c + 1) * LANES, :] = per_expert.T.astype(I32)
    off_ref[...] = off.astype(I32).reshape(n_e, n_c, LANES)

    s_lane = lax.broadcasted_iota(I32, (1, cap), 1).astype(F32)
    c_col = lax.broadcasted_iota(I32, (n_c, 1), 0).astype(F32)
    for e in range(n_e):
        r0 = e * n_c
        incl_e = incl[r0:r0 + n_c]
        off_e = off[r0:r0 + n_c, 0:1]
        tot_e = tot[r0:r0 + n_c]
        onehot = ((off_e <= s_lane) & (s_lane < off_e + tot_e)).astype(F32)
        counts = lax.dot_general(incl_e.astype(BF16), onehot.astype(BF16),
                                 (((0,), (0,)), ((), ())), preferred_element_type=F32)
        local = s_lane - jnp.sum(onehot * off_e, axis=0, keepdims=True)
        lane = jnp.sum((counts <= local).astype(F32), axis=0, keepdims=True)
        chunk = jnp.sum(onehot * c_col, axis=0, keepdims=True)
        idx_ref[e] = (chunk * LANES + lane).astype(I32)


def _routing(aff, cap):
    t = aff.shape[0]
    n_c = t // LANES
    a3 = aff.T.reshape(N_EXPERTS, n_c, LANES)
    return pl.pallas_call(
        functools.partial(_route_kernel, cap=cap),
        out_shape=[jax.ShapeDtypeStruct((N_EXPERTS, 1, cap), I32),
                   jax.ShapeDtypeStruct((t, N_EXPERTS), I32),
                   jax.ShapeDtypeStruct((N_EXPERTS, n_c, LANES), I32)],
        scratch_shapes=[pltpu.VMEM((N_EXPERTS * n_c, LANES), F32)],
        compiler_params=pltpu.CompilerParams(vmem_limit_bytes=_vmem_limit(48 << 20)),
        name="routing",
    )(a3)


def _gather_rows(table, row_ids):
    n_chunks = row_ids.shape[0]
    n_workers = SC_CORES * SC_SUBCORES
    per_worker = n_chunks // n_workers
    assert row_ids.shape[1] == GATHER_CHUNK and n_chunks % n_workers == 0
    mesh = plsc.VectorSubcoreMesh(core_axis_name="core", subcore_axis_name="subcore")

    @functools.partial(
        pl.kernel, mesh=mesh,
        out_type=jax.ShapeDtypeStruct((n_chunks * GATHER_CHUNK, LANES), I32),
        scratch_types=[pltpu.VMEM((per_worker, GATHER_CHUNK), I32),
                       pltpu.VMEM((2, GATHER_CHUNK, LANES), I32),
                       pltpu.SemaphoreType.DMA((2,)),
                       pltpu.SemaphoreType.DMA((2,))],
        name="gather_rows",
    )
    def gather(table_hbm, ids_hbm, out_hbm, ids_v, rows_v, gather_sem, store_sem):
        worker = lax.axis_index("subcore") * SC_CORES + lax.axis_index("core")
        first = worker * per_worker
        pltpu.sync_copy(ids_hbm.at[worker], ids_v)

        def fetch(j):
            return pltpu.make_async_copy(table_hbm.at[ids_v.at[j]], rows_v.at[j % 2],
                                         gather_sem.at[j % 2])

        def store(j):
            rows = pl.ds(pl.multiple_of((first + j) * GATHER_CHUNK, GATHER_CHUNK), GATHER_CHUNK)
            return pltpu.make_async_copy(rows_v.at[j % 2], out_hbm.at[rows], store_sem.at[j % 2])

        for j in range(per_worker):
            if j >= 2:
                store(j - 2).wait()
            fetch(j).start()
            if j >= 1:
                fetch(j - 1).wait()
                store(j - 1).start()
        fetch(per_worker - 1).wait()
        store(per_worker - 1).start()
        for j in range(max(per_worker - 2, 0), per_worker):
            store(j).wait()

    return gather(table, row_ids.reshape(n_workers, per_worker, GATHER_CHUNK))


def _packed_row_ids(idx, cap):
    tok = idx.reshape(-1, 1, cap // GATHER_CHUNK, GATHER_CHUNK)
    tile = jnp.arange(HQ_TILES, dtype=I32).reshape(1, HQ_TILES, 1, 1)
    ids = ((tok // SUBLANES) * HQ_TILES + tile) * SUBLANES + tok % SUBLANES
    return ids.reshape(-1, GATHER_CHUNK)


def _ffn_kernel(*refs, caps, n_f, row_chunk, first_expert, n_prior):
    n_g = len(caps)
    x_refs = refs[:n_g]
    wg_ref, wu_ref, wd_ref = refs[n_g:n_g + 3]
    y_refs = refs[n_g + 3 + n_prior:2 * n_g + 3 + n_prior]
    acc_ref = refs[2 * n_g + 3 + n_prior]
    e = pl.program_id(0)
    f = pl.program_id(1)

    @pl.when((e == 0) & (f == 0))
    def _():
        acc_ref[...] = jnp.zeros_like(acc_ref)

    def step(last):
        first = f == 0
        base = 0
        for x_ref, y_ref, cap in zip(x_refs, y_refs, caps):
            if last:
                aff = lax.bitcast_convert_type(x_ref[0, HQ_TILES - 1], F32)
                lane = lax.broadcasted_iota(I32, aff.shape, 1)
                gates = jnp.sum(jnp.where(lane == e + first_expert, aff, 0.0), axis=1,
                                keepdims=True)
                y_ref[0, cap:, :] = jnp.zeros((COMBINE_WINDOW, D_MODEL // 2), I32)
            for r in range(cap // row_chunk):
                rs = slice(r * row_chunk, (r + 1) * row_chunk)
                acc_rows = slice(base + r * row_chunk, base + (r + 1) * row_chunk)
                pairs = [_unpack_pair(x_ref[0, c, rs, :]) for c in range(HQ_TILES - 1)]
                x = jnp.concatenate([p[0] for p in pairs] + [p[1] for p in pairs], axis=1)
                gate_act = jnp.dot(x, wg_ref[0], preferred_element_type=F32)
                up = jnp.dot(x, wu_ref[0], preferred_element_type=F32)
                hid = (gate_act * jax.nn.sigmoid(gate_act)) * up
                part = jnp.dot(hid, wd_ref[0], preferred_element_type=F32)
                if last:
                    y = (part + acc_ref[acc_rows, :]) * gates[rs]
                    y_ref[0, rs, :] = _pack_pair(y[:, 0:D_MODEL // 2], y[:, D_MODEL // 2:])
                else:
                    acc_ref[acc_rows, :] = part + jnp.where(first, 0.0, acc_ref[acc_rows, :])
            base += cap

    @pl.when(f < n_f - 1)
    def _():
        step(False)

    @pl.when(f == n_f - 1)
    def _():
        step(True)


def _expert_ffn(xs_groups, w_gate, w_up, w_down, first_expert, prior_outputs):
    n_f = 4
    tf = D_EXPERT // n_f
    row_chunk = 512
    n_e = xs_groups[0].shape[0]
    e0 = first_expert
    caps = tuple(xs.shape[2] for xs in xs_groups)
    rows = sum(caps)
    est = (2 * HQ_TILES * rows * LANES * 4 + rows * D_MODEL * (4 + 2 * 2) + 2 * 3 * D_MODEL * tf * 4
           + row_chunk * (3 * tf + 2 * D_MODEL) * 4)
    x_specs = [pl.BlockSpec((1, HQ_TILES, cap, LANES), lambda e, f: (e, 0, 0, 0)) for cap in caps]
    y_shapes = [(N_EXPERTS, cap + COMBINE_WINDOW, D_MODEL // 2) for cap in caps]
    y_specs = [pl.BlockSpec((1,) + s[1:], lambda e, f: (e + e0, 0, 0)) for s in y_shapes]
    prior = list(prior_outputs or [])
    n_in = len(caps) + 3
    return pl.pallas_call(
        functools.partial(_ffn_kernel, caps=caps, n_f=n_f, row_chunk=row_chunk,
                          first_expert=e0, n_prior=len(prior)),
        grid=(n_e, n_f),
        in_specs=(x_specs + [pl.BlockSpec((1, D_MODEL, tf), lambda e, f: (e + e0, 0, f)),
                             pl.BlockSpec((1, D_MODEL, tf), lambda e, f: (e + e0, 0, f)),
                             pl.BlockSpec((1, tf, D_MODEL), lambda e, f: (e + e0, f, 0))]
                  + [pl.BlockSpec(memory_space=pl.ANY) for _ in prior]),
        out_specs=y_specs,
        out_shape=[jax.ShapeDtypeStruct(s, I32) for s in y_shapes],
        input_output_aliases={n_in + j: j for j in range(len(prior))},
        scratch_shapes=[pltpu.VMEM((rows, D_MODEL), F32)],
        compiler_params=pltpu.CompilerParams(
            dimension_semantics=("arbitrary", "arbitrary"),
            vmem_limit_bytes=_vmem_limit(est + (4 << 20))),
        name="expert_ffn",
    )(*xs_groups, w_gate, w_up, w_down, *prior)


def _combine_kernel(start_ref, nch_ref, wide_ref, x1_ref, mod_ref, g_ref, slot_ref, y_hbm, o_ref,
                    ybuf_ref, ffn_ref, sem, *, first_k, kblock):
    i = pl.program_id(0)
    tm = x1_ref.shape[0]
    cur = i % 2
    n_window_rows = N_EXPERTS * COMBINE_WINDOW

    window_head = COMBINE_WINDOW - COMBINE_CHUNK

    def window_copy(e, src_row, buf):
        return pltpu.make_async_copy(
            y_hbm.at[e, pl.ds(pl.multiple_of(src_row, SUBLANES), window_head)],
            ybuf_ref.at[buf, e * COMBINE_WINDOW:e * COMBINE_WINDOW + window_head], sem.at[buf])

    def window_tail_copy(e, src_row, buf):
        return chunk_copy(e, src_row + window_head, buf,
                          jnp.int32(e * COMBINE_WINDOW + window_head))

    def needs_tail(tile, e):
        return nch_ref[tile, e] * COMBINE_CHUNK > window_head

    def chunk_copy(e, src_row, buf, dst_row):
        return pltpu.make_async_copy(
            y_hbm.at[e, pl.ds(pl.multiple_of(src_row, SUBLANES), COMBINE_CHUNK)],
            ybuf_ref.at[buf, pl.ds(pl.multiple_of(dst_row, COMBINE_CHUNK), COMBINE_CHUNK)],
            sem.at[buf])

    def fetch(tile, buf):
        @pl.when(wide_ref[tile] == 0)
        def _():
            for e in range(N_EXPERTS):
                window_copy(e, start_ref[tile, e], buf).start()

                @pl.when(needs_tail(tile, e))
                def _(e=e):
                    window_tail_copy(e, start_ref[tile, e], buf).start()

        @pl.when(wide_ref[tile] != 0)
        def _():
            pos = jnp.int32(0)
            for e in range(N_EXPERTS):
                start = start_ref[tile, e]
                nch = nch_ref[tile, e]

                def issue(c, carry, e=e, start=start, pos=pos):
                    chunk_copy(e, start + c * COMBINE_CHUNK, buf, pos + c * COMBINE_CHUNK).start()
                    return carry
                lax.fori_loop(0, nch, issue, 0)
                pos = pos + nch * COMBINE_CHUNK

    @pl.when(i == 0)
    def _():
        ybuf_ref[...] = jnp.zeros_like(ybuf_ref)
        fetch(0, 0)

    @pl.when(i + 1 < pl.num_programs(0))
    def _():
        fetch(i + 1, 1 - cur)

    def finish(ffn):
        out = x1_ref[...] + mod_ref[0][5:6] * ffn
        ms = jnp.mean(out * out, axis=-1, keepdims=True)
        o_ref[...] = (out * lax.rsqrt(ms + EPS)) * g_ref[...]

    def apply_selection(sel, k0, width):
        lo, hi = _unpack_pair(ybuf_ref[cur, pl.ds(k0, width), :])
        return jnp.concatenate([jnp.dot(sel, lo, preferred_element_type=F32),
                                jnp.dot(sel, hi, preferred_element_type=F32)], axis=1)

    @pl.when(wide_ref[i] == 0)
    def _():
        for e in range(N_EXPERTS):
            window_copy(e, jnp.int32(0), cur).wait()

            @pl.when(needs_tail(i, e))
            def _(e=e):
                window_tail_copy(e, jnp.int32(0), cur).wait()
        expert_lane = lax.broadcasted_iota(I32, (1, N_EXPERTS), 1)
        starts = jnp.zeros((1, N_EXPERTS), I32)
        for e in range(N_EXPERTS):
            starts = jnp.where(expert_lane == e, start_ref[i, e], starts)
        slots = slot_ref[...]
        rows_in_window = jnp.where(slots >= 0, (slots - starts).astype(F32), -1.0)
        window_of_lane = lax.broadcasted_iota(I32, (N_EXPERTS, n_window_rows), 1) // COMBINE_WINDOW
        spread = (window_of_lane == lax.broadcasted_iota(I32, (N_EXPERTS, n_window_rows), 0))
        target = jnp.dot(rows_in_window.astype(BF16), spread.astype(BF16),
                         preferred_element_type=F32)
        lane_row = (lax.broadcasted_iota(I32, (tm, n_window_rows), 1) % COMBINE_WINDOW).astype(F32)
        sel = jnp.where(target == lane_row, 1.0, 0.0)
        ffn = None
        for k0 in range(0, n_window_rows, kblock):
            part = apply_selection(sel[:, k0:k0 + kblock], k0, kblock)
            ffn = part if ffn is None else ffn + part
        finish(ffn)

    @pl.when(wide_ref[i] != 0)
    def _():
        pos = jnp.int32(0)
        begins = []
        for e in range(N_EXPERTS):
            begins.append(pos)
            pos = pos + nch_ref[i, e] * COMBINE_CHUNK
        begins.append(pos)

        def drain(c, carry):
            chunk_copy(0, jnp.int32(0), cur, jnp.int32(0)).wait()
            return carry
        lax.fori_loop(0, pos // COMBINE_CHUNK, drain, 0)

        expert_lane = lax.broadcasted_iota(I32, (1, N_EXPERTS), 1)
        expert_row = lax.broadcasted_iota(I32, (N_EXPERTS, 1), 0)
        shift = jnp.zeros((1, N_EXPERTS), I32)
        range_lo = jnp.zeros((N_EXPERTS, 1), I32)
        range_hi = jnp.zeros((N_EXPERTS, 1), I32)
        for e in range(N_EXPERTS):
            shift = jnp.where(expert_lane == e, begins[e] - start_ref[i, e], shift)
            range_lo = jnp.where(expert_row == e, begins[e], range_lo)
            range_hi = jnp.where(expert_row == e, begins[e + 1], range_hi)
        slots = slot_ref[...]
        packed_row = slots + shift
        row_hi = jnp.where(slots >= 0, packed_row // ROW_SPLIT, -1).astype(F32).astype(BF16)
        row_lo = jnp.where(slots >= 0, packed_row % ROW_SPLIT, 0).astype(F32).astype(BF16)

        def selection(k0, width):
            col = k0 + lax.broadcasted_iota(I32, (N_EXPERTS, width), 1)
            spread = ((col >= range_lo) & (col < range_hi)).astype(BF16)
            want_hi = jnp.dot(row_hi, spread, preferred_element_type=F32)
            want_lo = jnp.dot(row_lo, spread, preferred_element_type=F32)
            here = k0 + lax.broadcasted_iota(I32, (1, width), 1)
            hit = ((want_hi == (here // ROW_SPLIT).astype(F32))
                   & (want_lo == (here % ROW_SPLIT).astype(F32)) & (here < pos))
            return jnp.where(hit, 1.0, 0.0)

        ffn_ref[...] = apply_selection(selection(0, first_k), 0, first_k)

        def kstep(kb, carry):
            k0 = pl.multiple_of(kb * kblock, kblock)
            ffn_ref[...] += apply_selection(selection(k0, kblock), k0, kblock)
            return carry
        lax.fori_loop(first_k // kblock, (pos + kblock - 1) // kblock, kstep, 0)
        finish(ffn_ref[...])


def _combine(x1, mod3, mod_row, g_final, slot_t, tile_start, tile_nch, tile_wide, y):
    t = x1.shape[0]
    tm = TOKEN_TILE
    first_k = 2 * tm + N_EXPERTS * COMBINE_CHUNK
    kblock = 256
    max_rows = N_EXPERTS * (tm + 2 * COMBINE_CHUNK)
    max_rows = -(-max_rows // kblock) * kblock
    row = lambda i, *_: (i, 0)
    grid_spec = pltpu.PrefetchScalarGridSpec(
        num_scalar_prefetch=3,
        grid=(t // tm,),
        in_specs=[pl.BlockSpec((tm, D_MODEL), row),
                  pl.BlockSpec((1, N_MOD, D_MODEL), lambda i, *_: (mod_row(i), 0, 0)),
                  pl.BlockSpec((1, D_MODEL), lambda i, *_: (0, 0)),
                  pl.BlockSpec((tm, N_EXPERTS), row),
                  pl.BlockSpec(memory_space=pl.ANY)],
        out_specs=pl.BlockSpec((tm, D_MODEL), row),
        scratch_shapes=[pltpu.VMEM((2, max_rows, D_MODEL // 2), I32),
                        pltpu.VMEM((tm, D_MODEL), F32),
                        pltpu.SemaphoreType.DMA((2,))],
    )
    return pl.pallas_call(
        functools.partial(_combine_kernel, first_k=first_k, kblock=kblock),
        grid_spec=grid_spec,
        out_shape=jax.ShapeDtypeStruct((t, D_MODEL), F32),
        compiler_params=pltpu.CompilerParams(
            dimension_semantics=("arbitrary",),
            vmem_limit_bytes=_vmem_limit(2 * max_rows * D_MODEL * 2 + 16 * tm * D_MODEL * 4)),
        name="combine",
    )(tile_start, tile_nch, tile_wide, x1, mod3, g_final, slot_t, y)


def _rope_tables(n):
    f32 = np.float32
    rows = n // GRID_W
    row = np.repeat(np.arange(rows, dtype=f32), GRID_W)
    col = np.tile(np.arange(GRID_W, dtype=f32), rows)
    inv = (f32(ROPE_THETA) ** (-np.arange(ROPE_FREQS, dtype=f32) / f32(ROPE_FREQS))).astype(f32)
    ang_r = row[:, None] * inv
    ang_c = col[:, None] * inv
    zero = np.zeros_like(ang_r)
    cos = np.concatenate([np.cos(ang_r)] * 2 + [np.cos(ang_c)] * 2, axis=1)
    sin_a = np.concatenate([-np.sin(ang_r), zero, -np.sin(ang_c), zero], axis=1)
    sin_b = np.concatenate([zero, np.sin(ang_r), zero, np.sin(ang_c)], axis=1)
    return tuple(jnp.asarray(t, F32) for t in (cos, sin_a, sin_b))


def _token_group(x, mod_mix, mod3, mod_row, seq, weights, rope_tabs, ctx_kv, late_mod=None):
    (norm_mix, w_in, sink, pool_w, pool_scale, w_out, norm_ffn, w_router,
     w_gate, w_up, w_down, norm_final) = weights
    b = x.shape[0]
    t = b * seq
    x2 = x.reshape(t, D_MODEL)
    q, k, v, p, *state = _in_projection(x2, mod_mix, mod_row(PROJ_TILE), norm_mix, w_in,
                                        rope_tabs, w_out, late_mod)
    if rope_tabs is None:
        late = state.pop()
        mod3 = jnp.concatenate([mod_mix.reshape(MOD_ROWS, -1), late], axis=1).reshape(
            MOD_ROWS, N_MOD, D_MODEL)
    w_in_bf = w_in if rope_tabs is not None else state.pop()
    w_out_bf = w_out if rope_tabs is not None else state.pop()
    if ctx_kv is None:
        attn = _context_attention(q, k, v, sink, seq)
    else:
        attn = _latent_attention(q, k, v, ctx_kv[0], ctx_kv[1], sink, seq)
    x1, h_packed, aff = _out_projection(attn, p, x2, mod3, mod_row(PROJ_TILE), norm_ffn,
                                        w_out_bf, w_router, pool_w, pool_scale, seq)

    cap = EC_FACTOR * t // N_EXPERTS
    idx, slot_t, off3 = _routing(aff, cap)
    table = h_packed.reshape(-1, LANES)
    per_range = N_EXPERTS // FFN_RANGES
    xs = []
    for r in range(FFN_RANGES):
        ids = _packed_row_ids(idx[r * per_range:(r + 1) * per_range], cap)
        xs.append(_gather_rows(table, ids).reshape(per_range, HQ_TILES, cap, LANES))

    chunks_per_tile = TOKEN_TILE // LANES
    tile_off = off3[:, ::chunks_per_tile, 0]
    tile_end = jnp.concatenate([tile_off[:, 1:], jnp.full((N_EXPERTS, 1), cap, I32)], axis=1)
    tile_start = (tile_off // SUBLANES) * SUBLANES
    tile_nch = jnp.where(tile_end > tile_off,
                         (tile_end - tile_start + COMBINE_CHUNK - 1) // COMBINE_CHUNK, 0)
    tile_wide = jnp.any(tile_end - tile_start > COMBINE_WINDOW, axis=0).astype(I32)

    def finish(y):
        out = _combine(x1, mod3, mod_row(TOKEN_TILE), norm_final, slot_t, tile_start.T,
                       tile_nch.T, tile_wide, y)
        return out.reshape(b, seq, D_MODEL)
    return xs, finish, state, (w_in_bf, w_out_bf, mod3)


def kernel(x_prompt, x_sample, c, cache_k, cache_v, c_ctx, w_ada, b_ada, norm_mix, w_in,
           sink_logits, pool_w, pool_scale, w_out, norm_ffn, w_router, w_gate, w_up, w_down,
           norm_final):
    n_b, seq, _ = x_prompt.shape
    n_db, n_lat, _ = x_sample.shape
    assert 1 + n_db <= MOD_ROWS and seq == TOKEN_TILE and n_lat % PROJ_TILE == 0
    assert w_in.shape[0] == 1 and cache_k.shape[1] == 1
    assert (n_b * seq) % PROJ_TILE == 0

    cond = jnp.concatenate(
        [c_ctx[None, :], c, jnp.zeros((MOD_ROWS - 1 - n_db, D_MODEL), F32)], axis=0)
    mixer_cols = 2 * D_MODEL
    mod_mix = _modulation(cond, w_ada[0], b_ada[0], mixer_cols).reshape(MOD_ROWS, 2, D_MODEL)

    w_router_bf = jnp.pad(w_router[0], ((0, 0), (0, LANES - N_EXPERTS))).astype(BF16)
    weights = (norm_mix[0][None, :], w_in[0], sink_logits[0], pool_w[0].astype(BF16),
               pool_scale[0][None, :], w_out[0], norm_ffn[0][None, :], w_router_bf,
               w_gate.reshape(w_gate.shape[1:]), w_up.reshape(w_up.shape[1:]),
               w_down.reshape(w_down.shape[1:]), norm_final[None, :])

    xs_p, finish_p, (k_p, v_p), (w_in_bf, w_out_bf, mod3) = _token_group(
        x_prompt, mod_mix, None, lambda tile: (lambda i: 0), seq, weights, None, None,
        late_mod=(cond, w_ada[0], b_ada[0], mixer_cols))
    weights = weights[:1] + (w_in_bf,) + weights[2:5] + (w_out_bf,) + weights[6:]

    ck = cache_k.reshape(-1, HEAD_DIM)
    cv = cache_v.reshape(-1, HEAD_DIM)
    xs_l, finish_l, _, _ = _token_group(
        x_sample, mod_mix, mod3, lambda tile: (lambda i: 1 + i // (n_lat // tile)), n_lat, weights,
        _rope_tables(n_lat), (ck, cv))

    ys = None
    for r in range(FFN_RANGES):
        ys = _expert_ffn([xs_p[r], xs_l[r]], weights[8], weights[9], weights[10],
                         r * (N_EXPERTS // FFN_RANGES), ys)
    y_prompt = finish_p(ys[0])
    y_sample = finish_l(ys[1])

    state_k = k_p.reshape(n_b, 1, seq, N_KV_HEADS, HEAD_DIM)
    state_v = v_p.reshape(n_b, 1, seq, N_KV_HEADS, HEAD_DIM)
    return (y_prompt, y_sample, state_k, state_v)
```

```python
import functools

import jax
import jax.numpy as jnp
import numpy as np
from jax import lax
from jax.experimental import pallas as pl
from jax.experimental.pallas import tpu as pltpu
from jax.experimental.pallas import tpu_sc as plsc

F32 = jnp.float32
BF16 = jnp.bfloat16
I32 = jnp.int32

D_MODEL = 2048
N_HEADS = 8
N_KV_HEADS = 2
HEAD_DIM = 128
Q_PER_KV = N_HEADS // N_KV_HEADS
ATTN_WIDTH = N_HEADS * HEAD_DIM
KV_WIDTH = N_KV_HEADS * HEAD_DIM
POOL_WIDTH = D_MODEL - ATTN_WIDTH
POOL_SIZES = (2, 4, 8, 16)
POOL_GROUP = POOL_WIDTH // len(POOL_SIZES)
IN_WIDTH = ATTN_WIDTH + 2 * KV_WIDTH + POOL_WIDTH
WINDOW = 128
BLOCK = 128
GRID_W = 64
ROPE_THETA = 10000.0
ROPE_FREQS = HEAD_DIM // 4
N_EXPERTS = 16
EC_FACTOR = 2
D_EXPERT = 1024
N_MOD = 6
EPS = 1e-6
NEG = -1e30
LOG2_E = 1.4426950408889634
ATTN_SCALE = HEAD_DIM ** -0.5 * LOG2_E

LANES = 128
SUBLANES = 8
BF16_ROWS = 16
VMEM_CAP = 64 * 1024 * 1024
SC_CORES = 2
SC_SUBCORES = 16

MOD_ROWS = 8
TOKEN_TILE = 256
PROJ_TILE = 512
SUB_TILE = 256
HQ_TILES = D_MODEL // 2 // LANES + 1
GATHER_CHUNK = 128
FFN_RANGES = 2
POOL_HALO = 8
COMBINE_CHUNK = BF16_ROWS
COMBINE_WINDOW = 64
ROW_SPLIT = 32


def _vmem_limit(nbytes):
    return int(min(VMEM_CAP - (4 << 20), max(nbytes, 16 << 20)))


def _mod_block(c_ref, w_ref, b_ref):
    c = c_ref[...]
    s = c * jax.nn.sigmoid(c)
    return jnp.dot(s.astype(BF16), w_ref[...].astype(BF16),
                   preferred_element_type=F32) + b_ref[...]


def _mod_kernel(c_ref, w_ref, b_ref, o_ref):
    o_ref[...] = _mod_block(c_ref, w_ref, b_ref)


def _modulation(cond, w_ada, b_ada, n):
    tn = 1024
    return pl.pallas_call(
        _mod_kernel,
        grid=(n // tn,),
        in_specs=[pl.BlockSpec((MOD_ROWS, D_MODEL), lambda j: (0, 0)),
                  pl.BlockSpec((D_MODEL, tn), lambda j: (0, j)),
                  pl.BlockSpec((1, tn), lambda j: (0, j))],
        out_specs=pl.BlockSpec((MOD_ROWS, tn), lambda j: (0, j)),
        out_shape=jax.ShapeDtypeStruct((MOD_ROWS, n), F32),
        compiler_params=pltpu.CompilerParams(
            vmem_limit_bytes=_vmem_limit(3 * D_MODEL * tn * 4)),
        name="modulation",
    )(cond, w_ada, b_ada.reshape(1, -1))


def _norm_mod(x, g, shift, scale):
    ms = jnp.mean(x * x, axis=-1, keepdims=True)
    y = x * lax.rsqrt(ms + EPS)
    return (y * g) * (1.0 + scale) + shift


def _inproj_kernel(*refs, rope):
    if rope:
        x_ref, mod_ref, g_ref, w_ref, cos_ref, sa_ref, sb_ref, q_ref, k_ref, v_ref, p_ref = refs
    else:
        (x_ref, mod_ref, g_ref, w_ref, wo_slab_ref, wi_slab_ref, cond_ref, wada_ref, bada_ref,
         q_ref, k_ref, v_ref, p_ref, ks_ref, vs_ref, wo_bf_ref, wi_bf_ref, late_mod_ref) = refs
        wo_bf_ref[...] = wo_slab_ref[...].astype(BF16)
        wi_bf_ref[...] = wi_slab_ref[...].astype(BF16)
        late_mod_ref[...] = _mod_block(cond_ref, wada_ref, bada_ref)
    mod = mod_ref[0]
    for s in range(x_ref.shape[0] // SUB_TILE):
        rows = slice(s * SUB_TILE, (s + 1) * SUB_TILE)
        h = _norm_mod(x_ref[rows, :], g_ref[...], mod[0:1], mod[1:2])
        u = jnp.dot(h.astype(w_ref.dtype), w_ref[...], preferred_element_type=F32)

        def rot(xh, rows=rows):
            return (xh * cos_ref[rows, :] + pltpu.roll(xh, LANES - ROPE_FREQS, 1) * sa_ref[rows, :]
                    + pltpu.roll(xh, ROPE_FREQS, 1) * sb_ref[rows, :])

        for hd in range(N_HEADS):
            xh = u[:, hd * HEAD_DIM:(hd + 1) * HEAD_DIM] * ATTN_SCALE
            q_ref[rows, hd * HEAD_DIM:(hd + 1) * HEAD_DIM] = (rot(xh) if rope else xh).astype(BF16)
        for hd in range(N_KV_HEADS):
            lo = ATTN_WIDTH + hd * HEAD_DIM
            xh = u[:, lo:lo + HEAD_DIM]
            k_ref[rows, hd * HEAD_DIM:(hd + 1) * HEAD_DIM] = rot(xh) if rope else xh
        v_ref[rows, :] = u[:, ATTN_WIDTH + KV_WIDTH:ATTN_WIDTH + 2 * KV_WIDTH]
        p_ref[rows, :] = u[:, ATTN_WIDTH + 2 * KV_WIDTH:]
        if not rope:
            for hd in range(N_KV_HEADS):
                state_rows = pl.ds(s * SUB_TILE * N_KV_HEADS + hd, SUB_TILE, stride=N_KV_HEADS)
                lo = ATTN_WIDTH + hd * HEAD_DIM
                ks_ref[state_rows, :] = u[:, lo:lo + HEAD_DIM]
                vs_ref[state_rows, :] = u[:, lo + KV_WIDTH:lo + KV_WIDTH + HEAD_DIM]


def _in_projection(x2, mod3, mod_row, g, w_in, rope_tabs, w_out, late_mod):
    t = x2.shape[0]
    tm = PROJ_TILE
    rope = rope_tabs is not None
    row = lambda i: (i, 0)
    w_mode = None if rope else pl.Buffered(1)
    in_specs = [pl.BlockSpec((tm, D_MODEL), row),
                pl.BlockSpec((1,) + mod3.shape[1:], lambda i: (mod_row(i), 0, 0)),
                pl.BlockSpec((1, D_MODEL), lambda i: (0, 0)),
                pl.BlockSpec((D_MODEL, IN_WIDTH), lambda i: (0, 0), pipeline_mode=w_mode)]
    args = [x2, mod3, g, w_in]
    if rope:
        n_seq = rope_tabs[0].shape[0]
        seq_blocks = n_seq // tm
        for tab in rope_tabs:
            in_specs.append(pl.BlockSpec((tm, HEAD_DIM), lambda i: (i % seq_blocks, 0)))
            args.append(tab)
    else:
        slab = D_MODEL // (t // tm)
        assert slab * (t // tm) == D_MODEL and slab % BF16_ROWS == 0
        for w in (w_out, w_in):
            in_specs.append(pl.BlockSpec((slab, w.shape[1]), row))
            args.append(w)
        cond, w_ada, b_ada, first_col = late_mod
        late_cols = (w_ada.shape[1] - first_col) // (t // tm)
        first_block = first_col // late_cols
        assert first_block * late_cols == first_col and late_cols % LANES == 0
        in_specs += [pl.BlockSpec((MOD_ROWS, D_MODEL), lambda i: (0, 0)),
                     pl.BlockSpec((D_MODEL, late_cols), lambda i: (0, first_block + i)),
                     pl.BlockSpec((1, late_cols), lambda i: (0, first_block + i))]
        args += [cond, w_ada, b_ada.reshape(1, -1)]
    out_specs = [pl.BlockSpec((tm, ATTN_WIDTH), row),
                 pl.BlockSpec((tm, KV_WIDTH), row),
                 pl.BlockSpec((tm, KV_WIDTH), row),
                 pl.BlockSpec((tm, POOL_WIDTH), row)]
    out_shape = [jax.ShapeDtypeStruct((t, ATTN_WIDTH), BF16),
                 jax.ShapeDtypeStruct((t, KV_WIDTH), F32),
                 jax.ShapeDtypeStruct((t, KV_WIDTH), F32),
                 jax.ShapeDtypeStruct((t, POOL_WIDTH), F32)]
    if not rope:
        for _ in range(2):
            out_specs.append(pl.BlockSpec((tm * N_KV_HEADS, HEAD_DIM), row))
            out_shape.append(jax.ShapeDtypeStruct((t * N_KV_HEADS, HEAD_DIM), F32))
        for w in (w_out, w_in):
            out_specs.append(pl.BlockSpec((slab, w.shape[1]), row))
            out_shape.append(jax.ShapeDtypeStruct(w.shape, BF16))
        out_specs.append(pl.BlockSpec((MOD_ROWS, late_cols), lambda i: (0, i)))
        out_shape.append(jax.ShapeDtypeStruct((MOD_ROWS, w_ada.shape[1] - first_col), F32))
    return pl.pallas_call(
        functools.partial(_inproj_kernel, rope=rope),
        grid=(t // tm,),
        in_specs=in_specs,
        out_specs=out_specs,
        out_shape=out_shape,
        compiler_params=pltpu.CompilerParams(
            vmem_limit_bytes=_vmem_limit(2 * D_MODEL * IN_WIDTH * 2 + 24 * tm * D_MODEL * 4)),
        name="in_projection",
    )(*args)


def _softmax_pv(s_list, v_list, sink_col):
    m = sink_col
    for s in s_list:
        m = jnp.maximum(m, jnp.max(s, axis=-1, keepdims=True))
    denom = jnp.exp2(sink_col - m)
    out = None
    for s, v in zip(s_list, v_list):
        e = jnp.exp2(s - m)
        if v.shape[1] == HEAD_DIM:
            denom = denom + jnp.sum(e, axis=-1, keepdims=True)
        o = jnp.dot(e.astype(BF16), v, preferred_element_type=F32)
        out = o if out is None else out + o
    if out.shape[1] > HEAD_DIM:
        denom = denom + out[:, HEAD_DIM:HEAD_DIM + 1]
    return out[:, 0:HEAD_DIM] * (1.0 / denom)


def _with_ones(v):
    return jnp.concatenate([v.astype(BF16), jnp.ones(v.shape, BF16)], axis=1)


def _stack_heads(q, kv):
    return jnp.concatenate(
        [q[:, (kv * Q_PER_KV + g) * HEAD_DIM:(kv * Q_PER_KV + g + 1) * HEAD_DIM]
         for g in range(Q_PER_KV)], axis=0)


def _sink_column(sink_ref, kv, rows):
    r = lax.broadcasted_iota(I32, (Q_PER_KV * rows, 1), 0)
    col = jnp.zeros((Q_PER_KV * rows, 1), F32)
    for g in range(Q_PER_KV):
        col = jnp.where((r >= g * rows) & (r < (g + 1) * rows), sink_ref[kv * Q_PER_KV + g], col)
    return col * LOG2_E


def _qk(q, k):
    return lax.dot_general(q, k, (((1,), (1,)), ((), ())), preferred_element_type=F32)


def _ctx_attn_kernel(sink_ref, q_ref, k_ref, v_ref, o_ref, *, seq):
    rows = seq
    for sq in range(q_ref.shape[0] // seq):
        rs = slice(sq * seq, (sq + 1) * seq)
        q = q_ref[rs, :]
        for kv in range(N_KV_HEADS):
            kh = k_ref[rs, kv * HEAD_DIM:(kv + 1) * HEAD_DIM].astype(BF16)
            vh = v_ref[rs, kv * HEAD_DIM:(kv + 1) * HEAD_DIM].astype(BF16)
            qs = _stack_heads(q, kv)
            s = _qk(qs, kh)
            o = _softmax_pv([s], [vh], _sink_column(sink_ref, kv, rows))
            for g in range(Q_PER_KV):
                hd = kv * Q_PER_KV + g
                o_ref[rs, hd * HEAD_DIM:(hd + 1) * HEAD_DIM] = (
                    o[g * rows:(g + 1) * rows].astype(BF16))


def _context_attention(q, k, v, sink, seq):
    t = q.shape[0]
    row = lambda b: (b, 0)
    per_step = next(n for n in (4, 2, 1) if (t // seq) % n == 0)
    tm = per_step * seq
    return pl.pallas_call(
        functools.partial(_ctx_attn_kernel, seq=seq),
        grid=(t // tm,),
        in_specs=[pl.BlockSpec(memory_space=pltpu.SMEM),
                  pl.BlockSpec((tm, ATTN_WIDTH), row),
                  pl.BlockSpec((tm, KV_WIDTH), row),
                  pl.BlockSpec((tm, KV_WIDTH), row)],
        out_specs=pl.BlockSpec((tm, ATTN_WIDTH), row),
        out_shape=jax.ShapeDtypeStruct((t, ATTN_WIDTH), BF16),
        name="context_attention",
    )(sink, q, k, v)


def _lat_attn_kernel(sink_ref, q_ref, k_ref, v_ref, ck_ref, cv_ref, o_ref, *, n_seq):
    band = 3 * BLOCK
    rows = Q_PER_KV * BLOCK
    per_step = q_ref.shape[0] // BLOCK
    cached = [pl.ds(kv, ck_ref.shape[0] // N_KV_HEADS, stride=N_KV_HEADS)
              for kv in range(N_KV_HEADS)]
    ck = [ck_ref[c, :].astype(BF16) for c in cached]
    cv = [_with_ones(cv_ref[c, :]) for c in cached]
    for j in range(per_step):
        i = pl.program_id(1) * per_step + j
        start = pl.multiple_of(jnp.clip((i - 1) * BLOCK, 0, n_seq - band), BLOCK)
        qpos = i * BLOCK + lax.broadcasted_iota(I32, (rows, band), 0) % BLOCK
        kpos = start + lax.broadcasted_iota(I32, (rows, band), 1)
        mask = jnp.abs(kpos - qpos) <= WINDOW
        rs = slice(j * BLOCK, (j + 1) * BLOCK)
        q = q_ref[rs, :]
        for kv in range(N_KV_HEADS):
            cols = slice(kv * HEAD_DIM, (kv + 1) * HEAD_DIM)
            kb = k_ref[pl.ds(start, band), cols].astype(BF16)
            vb = _with_ones(v_ref[pl.ds(start, band), cols])
            qs = _stack_heads(q, kv)
            s_loc = jnp.where(mask, _qk(qs, kb), NEG)
            s_ctx = _qk(qs, ck[kv])
            o = _softmax_pv([s_loc, s_ctx], [vb, cv[kv]], _sink_column(sink_ref, kv, BLOCK))
            for g in range(Q_PER_KV):
                hd = kv * Q_PER_KV + g
                o_ref[rs, hd * HEAD_DIM:(hd + 1) * HEAD_DIM] = (
                    o[g * BLOCK:(g + 1) * BLOCK].astype(BF16))


def _latent_attention(q, k, v, ck, cv, sink, n_seq):
    t = q.shape[0]
    per_step = 2 if (n_seq // BLOCK) % 2 == 0 else 1
    tq = per_step * BLOCK
    nb = n_seq // tq
    cache_rows = ck.shape[0] // (t // n_seq)
    return pl.pallas_call(
        functools.partial(_lat_attn_kernel, n_seq=n_seq),
        grid=(t // n_seq, nb),
        in_specs=[pl.BlockSpec(memory_space=pltpu.SMEM),
                  pl.BlockSpec((tq, ATTN_WIDTH), lambda b, i: (b * nb + i, 0)),
                  pl.BlockSpec((n_seq, KV_WIDTH), lambda b, i: (b, 0)),
                  pl.BlockSpec((n_seq, KV_WIDTH), lambda b, i: (b, 0)),
                  pl.BlockSpec((cache_rows, HEAD_DIM), lambda b, i: (b, 0)),
                  pl.BlockSpec((cache_rows, HEAD_DIM), lambda b, i: (b, 0))],
        out_specs=pl.BlockSpec((tq, ATTN_WIDTH), lambda b, i: (b * nb + i, 0)),
        out_shape=jax.ShapeDtypeStruct((t, ATTN_WIDTH), BF16),
        name="latent_attention",
    )(sink, q, k, v, ck, cv)


def _pool_group(p_ref, r0, seq, w_ref, s_ref, g):
    n = SUB_TILE
    rows = n + 2 * POOL_HALO
    static = isinstance(r0, int)
    t0 = r0 % seq
    t = t0 + lax.broadcasted_iota(I32, (n, 1), 0)
    has_top = t0 > 0
    has_bottom = t0 + n < seq
    zeros = jnp.zeros((POOL_HALO, POOL_GROUP), F32)
    w = POOL_SIZES[g]
    cols = slice(g * POOL_GROUP, (g + 1) * POOL_GROUP)
    pg = p_ref[pl.ds(r0, n), cols]
    if static:
        top = p_ref[r0 - POOL_HALO:r0, cols] if has_top else zeros
        bottom = p_ref[r0 + n:r0 + n + POOL_HALO, cols] if has_bottom else zeros
    else:
        top_row = pl.multiple_of(jnp.maximum(r0 - POOL_HALO, 0), POOL_HALO)
        bottom_row = pl.multiple_of(jnp.minimum(r0 + n, p_ref.shape[0] - POOL_HALO), POOL_HALO)
        top = jnp.where(has_top, p_ref[pl.ds(top_row, POOL_HALO), cols], 0.0)
        bottom = jnp.where(has_bottom, p_ref[pl.ds(bottom_row, POOL_HALO), cols], 0.0)
    x = jnp.concatenate([top, pg, bottom], axis=0)
    fwd = x
    span = 1
    while span < w // 2:
        fwd = fwd + pltpu.roll(fwd, rows - span, 0)
        span *= 2
    if (w // 2) % SUBLANES == 0:
        wsum = fwd[POOL_HALO - w // 2:POOL_HALO - w // 2 + n] + fwd[POOL_HALO:POOL_HALO + n]
    else:
        wsum = (fwd + pltpu.roll(fwd, w // 2, 0))[POOL_HALO:POOL_HALO + n]
    lo = jnp.maximum(t - w // 2, 0)
    hi = jnp.minimum(t + w - w // 2, seq)
    inv_cnt = 1.0 / (hi - lo).astype(F32)
    mixed = wsum * inv_cnt - pg
    y = jnp.dot(mixed.astype(BF16), w_ref[g], preferred_element_type=F32)
    return (y * s_ref[:, cols]).astype(BF16)


def _pack_pair(lo, hi):
    return lax.bitcast_convert_type(pltpu.pack_elementwise([lo, hi], packed_dtype=BF16), I32)


def _unpack_pair(words):
    lo = pltpu.unpack_elementwise(words, index=0, packed_dtype=BF16, unpacked_dtype=F32)
    hi = pltpu.unpack_elementwise(words, index=1, packed_dtype=BF16, unpacked_dtype=F32)
    return lo, hi


def _outproj_kernel(a_ref, p_ref, x_ref, mod_ref, g_ref, wo_ref, wr_ref, pw_ref, ps_ref,
                    x1_ref, h_ref, aff_ref, *, seq):
    mod = mod_ref[0]
    groups = SUB_TILE // SUBLANES
    half = D_MODEL // 2
    tm = x_ref.shape[0]
    steps_per_p_block = p_ref.shape[0] // tm
    for s in range(tm // SUB_TILE):
        rows = slice(s * SUB_TILE, (s + 1) * SUB_TILE)
        grp = slice(s * groups, (s + 1) * groups)
        r0 = s * SUB_TILE
        if steps_per_p_block > 1:
            r0 = pl.multiple_of((pl.program_id(0) % steps_per_p_block) * tm + r0, SUB_TILE)
        pooled = jnp.concatenate([_pool_group(p_ref, r0, seq, pw_ref, ps_ref, g)
                                  for g in range(len(POOL_SIZES))], axis=1)
        mix = (jnp.dot(a_ref[rows, :], wo_ref[0:ATTN_WIDTH, :], preferred_element_type=F32)
               + jnp.dot(pooled, wo_ref[ATTN_WIDTH:D_MODEL, :], preferred_element_type=F32))
        x1 = x_ref[rows, :] + mod[2:3] * mix
        x1_ref[rows, :] = x1
        h = _norm_mod(x1, g_ref[...], mod[3:4], mod[4:5])
        logits = jnp.dot(h.astype(BF16), wr_ref[...], preferred_element_type=F32)
        lane = lax.broadcasted_iota(I32, logits.shape, 1)
        logits = jnp.where(lane < N_EXPERTS, logits, -jnp.inf)
        m = jnp.max(logits, axis=-1, keepdims=True)
        e = jnp.exp(logits - m)
        aff = e / jnp.sum(e, axis=-1, keepdims=True)
        aff_ref[rows, :] = aff[:, 0:N_EXPERTS]
        for c in range(HQ_TILES - 1):
            words = _pack_pair(h[:, c * LANES:(c + 1) * LANES],
                               h[:, half + c * LANES:half + (c + 1) * LANES])
            h_ref[grp, c * SUBLANES:(c + 1) * SUBLANES, :] = (
                words.reshape(groups, SUBLANES, LANES))
        h_ref[grp, (HQ_TILES - 1) * SUBLANES:, :] = (
            lax.bitcast_convert_type(aff, I32).reshape(groups, SUBLANES, LANES))


def _out_projection(attn, p, x2, mod3, mod_row, g, w_out_bf, w_router, pool_w, pool_scale, seq):
    t = x2.shape[0]
    tm = PROJ_TILE
    row = lambda i: (i, 0)
    p_rows = max(tm, seq)
    steps_per_p_block = p_rows // tm
    return pl.pallas_call(
        functools.partial(_outproj_kernel, seq=seq),
        grid=(t // tm,),
        in_specs=[pl.BlockSpec((tm, ATTN_WIDTH), row),
                  pl.BlockSpec((p_rows, POOL_WIDTH), lambda i: (i // steps_per_p_block, 0)),
                  pl.BlockSpec((tm, D_MODEL), row),
                  pl.BlockSpec((1, N_MOD, D_MODEL), lambda i: (mod_row(i), 0, 0)),
                  pl.BlockSpec((1, D_MODEL), lambda i: (0, 0)),
                  pl.BlockSpec((D_MODEL, D_MODEL), lambda i: (0, 0)),
                  pl.BlockSpec((D_MODEL, LANES), lambda i: (0, 0)),
                  pl.BlockSpec((len(POOL_SIZES), POOL_GROUP, POOL_GROUP), lambda i: (0, 0, 0)),
                  pl.BlockSpec((1, POOL_WIDTH), lambda i: (0, 0))],
        out_specs=[pl.BlockSpec((tm, D_MODEL), row),
                   pl.BlockSpec((tm // SUBLANES, HQ_TILES * SUBLANES, LANES), lambda i: (i, 0, 0)),
                   pl.BlockSpec((tm, N_EXPERTS), row)],
        out_shape=[jax.ShapeDtypeStruct((t, D_MODEL), F32),
                   jax.ShapeDtypeStruct((t // SUBLANES, HQ_TILES * SUBLANES, LANES), I32),
                   jax.ShapeDtypeStruct((t, N_EXPERTS), F32)],
        compiler_params=pltpu.CompilerParams(
            vmem_limit_bytes=_vmem_limit(2 * D_MODEL * D_MODEL * 2 + 24 * tm * D_MODEL * 4)),
        name="out_projection",
    )(attn, p, x2, mod3, g, w_out_bf, w_router, pool_w, pool_scale)


def _route_kernel(a_ref, idx_ref, slot_ref, off_ref, slot_scr, *, cap):
    a = a_ref[...]
    n_e, n_c, _ = a.shape
    rows = n_e * n_c

    def enough(cand):
        cand_f = lax.bitcast_convert_type(cand, F32)
        cnt = jnp.sum(jnp.sum((a >= cand_f).astype(F32), axis=1, keepdims=True),
                      axis=2, keepdims=True)
        return cnt >= cap

    def two_bits(it, thr):
        low = 28 - 2 * it
        for setting in (1, 2, 3):
            cand = thr | jnp.left_shift(jnp.int32(setting), low)
            best = jnp.where(enough(cand), cand, thr if setting == 1 else best)
        return best

    top = jnp.full((n_e, 1, 1), 1 << 30, I32)
    thr = jnp.where(enough(top), top, 0)
    thr = lax.fori_loop(0, 15, two_bits, thr)
    thr_f = lax.bitcast_convert_type(thr, F32)
    gt = (a > thr_f).astype(F32).reshape(rows, LANES)
    eq = (a == thr_f).astype(F32).reshape(rows, LANES)

    li = lax.broadcasted_iota(I32, (LANES, LANES), 0)
    lj = lax.broadcasted_iota(I32, (LANES, LANES), 1)
    upper_incl = (li <= lj).astype(BF16)
    ci = lax.broadcasted_iota(I32, (n_c, n_c), 0)
    cj = lax.broadcasted_iota(I32, (n_c, n_c), 1)
    before = (cj < ci).astype(BF16)
    whole = jnp.ones((n_c, n_c), BF16)

    def over_chunks(mat, col):
        wide = jnp.broadcast_to(col, (rows, LANES)).astype(BF16)
        side_by_side = jnp.concatenate([wide[e * n_c:(e + 1) * n_c] for e in range(n_e)], axis=1)
        res = jnp.dot(mat, side_by_side, preferred_element_type=F32)
        return jnp.concatenate([res[:, e * LANES:(e + 1) * LANES] for e in range(n_e)], axis=0)

    def prefix(x):
        incl = jnp.dot(x.astype(BF16), upper_incl, preferred_element_type=F32)
        tot = incl[:, LANES - 1:LANES]
        return incl, tot, over_chunks(before, tot)

    n_gt = over_chunks(whole, jnp.sum(gt, axis=1, keepdims=True))
    need = cap - n_gt
    incl_eq, _, off_eq = prefix(eq)
    rank_eq = off_eq + incl_eq - eq
    sel = jnp.where((eq > 0) & (rank_eq < need), 1.0, gt)
    incl, tot, off = prefix(sel)
    slot = off + incl - sel
    slot_scr[...] = jnp.where(sel > 0, slot, -1.0)
    for c in range(n_c):
        per_expert = slot_scr[pl.ds(c, n_e, stride=n_c), :]
        slot_ref[c * LANES:(c + 1) * LANES, :] = per_expert.T.astype(I32)
    off_ref[...] = off.astype(I32).reshape(n_e, n_c, LANES)

    s_lane = lax.broadcasted_iota(I32, (1, cap), 1).astype(F32)
    c_col = lax.broadcasted_iota(I32, (n_c, 1), 0).astype(F32)
    for e in range(n_e):
        r0 = e * n_c
        incl_e = incl[r0:r0 + n_c]
        off_e = off[r0:r0 + n_c, 0:1]
        tot_e = tot[r0:r0 + n_c]
        onehot = ((off_e <= s_lane) & (s_lane < off_e + tot_e)).astype(F32)
        counts = lax.dot_general(incl_e.astype(BF16), onehot.astype(BF16),
                                 (((0,), (0,)), ((), ())), preferred_element_type=F32)
        local = s_lane - jnp.sum(onehot * off_e, axis=0, keepdims=True)
        lane = jnp.sum((counts <= local).astype(F32), axis=0, keepdims=True)
        chunk = jnp.sum(onehot * c_col, axis=0, keepdims=True)
        idx_ref[e] = (chunk * LANES + lane).astype(I32)


def _routing(aff, cap):
    t = aff.shape[0]
    n_c = t // LANES
    a3 = aff.T.reshape(N_EXPERTS, n_c, LANES)
    return pl.pallas_call(
        functools.partial(_route_kernel, cap=cap),
        out_shape=[jax.ShapeDtypeStruct((N_EXPERTS, 1, cap), I32),
                   jax.ShapeDtypeStruct((t, N_EXPERTS), I32),
                   jax.ShapeDtypeStruct((N_EXPERTS, n_c, LANES), I32)],
        scratch_shapes=[pltpu.VMEM((N_EXPERTS * n_c, LANES), F32)],
        compiler_params=pltpu.CompilerParams(vmem_limit_bytes=_vmem_limit(48 << 20)),
        name="routing",
    )(a3)


def _gather_rows(table, row_ids):
    n_chunks = row_ids.shape[0]
    n_workers = SC_CORES * SC_SUBCORES
    per_worker = n_chunks // n_workers
    assert row_ids.shape[1] == GATHER_CHUNK and n_chunks % n_workers == 0
    mesh = plsc.VectorSubcoreMesh(core_axis_name="core", subcore_axis_name="subcore")

    @functools.partial(
        pl.kernel, mesh=mesh,
        out_type=jax.ShapeDtypeStruct((n_chunks * GATHER_CHUNK, LANES), I32),
        scratch_types=[pltpu.VMEM((per_worker, GATHER_CHUNK), I32),
                       pltpu.VMEM((2, GATHER_CHUNK, LANES), I32),
                       pltpu.SemaphoreType.DMA((2,)),
                       pltpu.SemaphoreType.DMA((2,))],
        name="gather_rows",
    )
    def gather(table_hbm, ids_hbm, out_hbm, ids_v, rows_v, gather_sem, store_sem):
        worker = lax.axis_index("subcore") * SC_CORES + lax.axis_index("core")
        first = worker * per_worker
        pltpu.sync_copy(ids_hbm.at[worker], ids_v)

        def fetch(j):
            return pltpu.make_async_copy(table_hbm.at[ids_v.at[j]], rows_v.at[j % 2],
                                         gather_sem.at[j % 2])

        def store(j):
            rows = pl.ds(pl.multiple_of((first + j) * GATHER_CHUNK, GATHER_CHUNK), GATHER_CHUNK)
            return pltpu.make_async_copy(rows_v.at[j % 2], out_hbm.at[rows], store_sem.at[j % 2])

        for j in range(per_worker):
            if j >= 2:
                store(j - 2).wait()
            fetch(j).start()
            if j >= 1:
                fetch(j - 1).wait()
                store(j - 1).start()
        fetch(per_worker - 1).wait()
        store(per_worker - 1).start()
        for j in range(max(per_worker - 2, 0), per_worker):
            store(j).wait()

    return gather(table, row_ids.reshape(n_workers, per_worker, GATHER_CHUNK))


def _packed_row_ids(idx, cap):
    tok = idx.reshape(-1, 1, cap // GATHER_CHUNK, GATHER_CHUNK)
    tile = jnp.arange(HQ_TILES, dtype=I32).reshape(1, HQ_TILES, 1, 1)
    ids = ((tok // SUBLANES) * HQ_TILES + tile) * SUBLANES + tok % SUBLANES
    return ids.reshape(-1, GATHER_CHUNK)


def _ffn_kernel(*refs, caps, n_f, row_chunk, first_expert, n_prior):
    n_g = len(caps)
    x_refs = refs[:n_g]
    wg_ref, wu_ref, wd_ref = refs[n_g:n_g + 3]
    y_refs = refs[n_g + 3 + n_prior:2 * n_g + 3 + n_prior]
    acc_ref = refs[2 * n_g + 3 + n_prior]
    e = pl.program_id(0)
    f = pl.program_id(1)

    @pl.when((e == 0) & (f == 0))
    def _():
        acc_ref[...] = jnp.zeros_like(acc_ref)

    def step(last):
        first = f == 0
        base = 0
        for x_ref, y_ref, cap in zip(x_refs, y_refs, caps):
            if last:
                aff = lax.bitcast_convert_type(x_ref[0, HQ_TILES - 1], F32)
                lane = lax.broadcasted_iota(I32, aff.shape, 1)
                gates = jnp.sum(jnp.where(lane == e + first_expert, aff, 0.0), axis=1,
                                keepdims=True)
                y_ref[0, cap:, :] = jnp.zeros((COMBINE_WINDOW, D_MODEL // 2), I32)
            for r in range(cap // row_chunk):
                rs = slice(r * row_chunk, (r + 1) * row_chunk)
                acc_rows = slice(base + r * row_chunk, base + (r + 1) * row_chunk)
                pairs = [_unpack_pair(x_ref[0, c, rs, :]) for c in range(HQ_TILES - 1)]
                x = jnp.concatenate([p[0] for p in pairs] + [p[1] for p in pairs], axis=1)
                gate_act = jnp.dot(x, wg_ref[0], preferred_element_type=F32)
                up = jnp.dot(x, wu_ref[0], preferred_element_type=F32)
                hid = (gate_act * jax.nn.sigmoid(gate_act)) * up
                part = jnp.dot(hid, wd_ref[0], preferred_element_type=F32)
                if last:
                    y = (part + acc_ref[acc_rows, :]) * gates[rs]
                    y_ref[0, rs, :] = _pack_pair(y[:, 0:D_MODEL // 2], y[:, D_MODEL // 2:])
                else:
                    acc_ref[acc_rows, :] = part + jnp.where(first, 0.0, acc_ref[acc_rows, :])
            base += cap

    @pl.when(f < n_f - 1)
    def _():
        step(False)

    @pl.when(f == n_f - 1)
    def _():
        step(True)


def _expert_ffn(xs_groups, w_gate, w_up, w_down, first_expert, prior_outputs):
    n_f = 4
    tf = D_EXPERT // n_f
    row_chunk = 512
    n_e = xs_groups[0].shape[0]
    e0 = first_expert
    caps = tuple(xs.shape[2] for xs in xs_groups)
    rows = sum(caps)
    est = (2 * HQ_TILES * rows * LANES * 4 + rows * D_MODEL * (4 + 2 * 2) + 2 * 3 * D_MODEL * tf * 4
           + row_chunk * (3 * tf + 2 * D_MODEL) * 4)
    x_specs = [pl.BlockSpec((1, HQ_TILES, cap, LANES), lambda e, f: (e, 0, 0, 0)) for cap in caps]
    y_shapes = [(N_EXPERTS, cap + COMBINE_WINDOW, D_MODEL // 2) for cap in caps]
    y_specs = [pl.BlockSpec((1,) + s[1:], lambda e, f: (e + e0, 0, 0)) for s in y_shapes]
    prior = list(prior_outputs or [])
    n_in = len(caps) + 3
    return pl.pallas_call(
        functools.partial(_ffn_kernel, caps=caps, n_f=n_f, row_chunk=row_chunk,
                          first_expert=e0, n_prior=len(prior)),
        grid=(n_e, n_f),
        in_specs=(x_specs + [pl.BlockSpec((1, D_MODEL, tf), lambda e, f: (e + e0, 0, f)),
                             pl.BlockSpec((1, D_MODEL, tf), lambda e, f: (e + e0, 0, f)),
                             pl.BlockSpec((1, tf, D_MODEL), lambda e, f: (e + e0, f, 0))]
                  + [pl.BlockSpec(memory_space=pl.ANY) for _ in prior]),
        out_specs=y_specs,
        out_shape=[jax.ShapeDtypeStruct(s, I32) for s in y_shapes],
        input_output_aliases={n_in + j: j for j in range(len(prior))},
        scratch_shapes=[pltpu.VMEM((rows, D_MODEL), F32)],
        compiler_params=pltpu.CompilerParams(
            dimension_semantics=("arbitrary", "arbitrary"),
            vmem_limit_bytes=_vmem_limit(est + (4 << 20))),
        name="expert_ffn",
    )(*xs_groups, w_gate, w_up, w_down, *prior)


def _combine_kernel(start_ref, nch_ref, wide_ref, x1_ref, mod_ref, g_ref, slot_ref, y_hbm, o_ref,
                    ybuf_ref, ffn_ref, sem, *, first_k, kblock):
    i = pl.program_id(0)
    tm = x1_ref.shape[0]
    cur = i % 2
    n_window_rows = N_EXPERTS * COMBINE_WINDOW

    window_head = COMBINE_WINDOW - COMBINE_CHUNK

    def window_copy(e, src_row, buf):
        return pltpu.make_async_copy(
            y_hbm.at[e, pl.ds(pl.multiple_of(src_row, SUBLANES), window_head)],
            ybuf_ref.at[buf, e * COMBINE_WINDOW:e * COMBINE_WINDOW + window_head], sem.at[buf])

    def window_tail_copy(e, src_row, buf):
        return chunk_copy(e, src_row + window_head, buf,
                          jnp.int32(e * COMBINE_WINDOW + window_head))

    def needs_tail(tile, e):
        return nch_ref[tile, e] * COMBINE_CHUNK > window_head

    def chunk_copy(e, src_row, buf, dst_row):
        return pltpu.make_async_copy(
            y_hbm.at[e, pl.ds(pl.multiple_of(src_row, SUBLANES), COMBINE_CHUNK)],
            ybuf_ref.at[buf, pl.ds(pl.multiple_of(dst_row, COMBINE_CHUNK), COMBINE_CHUNK)],
            sem.at[buf])

    def fetch(tile, buf):
        @pl.when(wide_ref[tile] == 0)
        def _():
            for e in range(N_EXPERTS):
                window_copy(e, start_ref[tile, e], buf).start()

                @pl.when(needs_tail(tile, e))
                def _(e=e):
                    window_tail_copy(e, start_ref[tile, e], buf).start()

        @pl.when(wide_ref[tile] != 0)
        def _():
            pos = jnp.int32(0)
            for e in range(N_EXPERTS):
                start = start_ref[tile, e]
                nch = nch_ref[tile, e]

                def issue(c, carry, e=e, start=start, pos=pos):
                    chunk_copy(e, start + c * COMBINE_CHUNK, buf, pos + c * COMBINE_CHUNK).start()
                    return carry
                lax.fori_loop(0, nch, issue, 0)
                pos = pos + nch * COMBINE_CHUNK

    @pl.when(i == 0)
    def _():
        ybuf_ref[...] = jnp.zeros_like(ybuf_ref)
        fetch(0, 0)

    @pl.when(i + 1 < pl.num_programs(0))
    def _():
        fetch(i + 1, 1 - cur)

    def finish(ffn):
        out = x1_ref[...] + mod_ref[0][5:6] * ffn
        ms = jnp.mean(out * out, axis=-1, keepdims=True)
        o_ref[...] = (out * lax.rsqrt(ms + EPS)) * g_ref[...]

    def apply_selection(sel, k0, width):
        lo, hi = _unpack_pair(ybuf_ref[cur, pl.ds(k0, width), :])
        return jnp.concatenate([jnp.dot(sel, lo, preferred_element_type=F32),
                                jnp.dot(sel, hi, preferred_element_type=F32)], axis=1)

    @pl.when(wide_ref[i] == 0)
    def _():
        for e in range(N_EXPERTS):
            window_copy(e, jnp.int32(0), cur).wait()

            @pl.when(needs_tail(i, e))
            def _(e=e):
                window_tail_copy(e, jnp.int32(0), cur).wait()
        expert_lane = lax.broadcasted_iota(I32, (1, N_EXPERTS), 1)
        starts = jnp.zeros((1, N_EXPERTS), I32)
        for e in range(N_EXPERTS):
            starts = jnp.where(expert_lane == e, start_ref[i, e], starts)
        slots = slot_ref[...]
        rows_in_window = jnp.where(slots >= 0, (slots - starts).astype(F32), -1.0)
        window_of_lane = lax.broadcasted_iota(I32, (N_EXPERTS, n_window_rows), 1) // COMBINE_WINDOW
        spread = (window_of_lane == lax.broadcasted_iota(I32, (N_EXPERTS, n_window_rows), 0))
        target = jnp.dot(rows_in_window.astype(BF16), spread.astype(BF16),
                         preferred_element_type=F32)
        lane_row = (lax.broadcasted_iota(I32, (tm, n_window_rows), 1) % COMBINE_WINDOW).astype(F32)
        sel = jnp.where(target == lane_row, 1.0, 0.0)
        ffn = None
        for k0 in range(0, n_window_rows, kblock):
            part = apply_selection(sel[:, k0:k0 + kblock], k0, kblock)
            ffn = part if ffn is None else ffn + part
        finish(ffn)

    @pl.when(wide_ref[i] != 0)
    def _():
        pos = jnp.int32(0)
        begins = []
        for e in range(N_EXPERTS):
            begins.append(pos)
            pos = pos + nch_ref[i, e] * COMBINE_CHUNK
        begins.append(pos)

        def drain(c, carry):
            chunk_copy(0, jnp.int32(0), cur, jnp.int32(0)).wait()
            return carry
        lax.fori_loop(0, pos // COMBINE_CHUNK, drain, 0)

        expert_lane = lax.broadcasted_iota(I32, (1, N_EXPERTS), 1)
        expert_row = lax.broadcasted_iota(I32, (N_EXPERTS, 1), 0)
        shift = jnp.zeros((1, N_EXPERTS), I32)
        range_lo = jnp.zeros((N_EXPERTS, 1), I32)
        range_hi = jnp.zeros((N_EXPERTS, 1), I32)
        for e in range(N_EXPERTS):
            shift = jnp.where(expert_lane == e, begins[e] - start_ref[i, e], shift)
            range_lo = jnp.where(expert_row == e, begins[e], range_lo)
            range_hi = jnp.where(expert_row == e, begins[e + 1], range_hi)
        slots = slot_ref[...]
        packed_row = slots + shift
        row_hi = jnp.where(slots >= 0, packed_row // ROW_SPLIT, -1).astype(F32).astype(BF16)
        row_lo = jnp.where(slots >= 0, packed_row % ROW_SPLIT, 0).astype(F32).astype(BF16)

        def selection(k0, width):
            col = k0 + lax.broadcasted_iota(I32, (N_EXPERTS, width), 1)
            spread = ((col >= range_lo) & (col < range_hi)).astype(BF16)
            want_hi = jnp.dot(row_hi, spread, preferred_element_type=F32)
            want_lo = jnp.dot(row_lo, spread, preferred_element_type=F32)
            here = k0 + lax.broadcasted_iota(I32, (1, width), 1)
            hit = ((want_hi == (here // ROW_SPLIT).astype(F32))
                   & (want_lo == (here % ROW_SPLIT).astype(F32)) & (here < pos))
            return jnp.where(hit, 1.0, 0.0)

        ffn_ref[...] = apply_selection(selection(0, first_k), 0, first_k)

        def kstep(kb, carry):
            k0 = pl.multiple_of(kb * kblock, kblock)
            ffn_ref[...] += apply_selection(selection(k0, kblock), k0, kblock)
            return carry
        lax.fori_loop(first_k // kblock, (pos + kblock - 1) // kblock, kstep, 0)
        finish(ffn_ref[...])


def _combine(x1, mod3, mod_row, g_final, slot_t, tile_start, tile_nch, tile_wide, y):
    t = x1.shape[0]
    tm = TOKEN_TILE
    first_k = 2 * tm + N_EXPERTS * COMBINE_CHUNK
    kblock = 256
    max_rows = N_EXPERTS * (tm + 2 * COMBINE_CHUNK)
    max_rows = -(-max_rows // kblock) * kblock
    row = lambda i, *_: (i, 0)
    grid_spec = pltpu.PrefetchScalarGridSpec(
        num_scalar_prefetch=3,
        grid=(t // tm,),
        in_specs=[pl.BlockSpec((tm, D_MODEL), row),
                  pl.BlockSpec((1, N_MOD, D_MODEL), lambda i, *_: (mod_row(i), 0, 0)),
                  pl.BlockSpec((1, D_MODEL), lambda i, *_: (0, 0)),
                  pl.BlockSpec((tm, N_EXPERTS), row),
                  pl.BlockSpec(memory_space=pl.ANY)],
        out_specs=pl.BlockSpec((tm, D_MODEL), row),
        scratch_shapes=[pltpu.VMEM((2, max_rows, D_MODEL // 2), I32),
                        pltpu.VMEM((tm, D_MODEL), F32),
                        pltpu.SemaphoreType.DMA((2,))],
    )
    return pl.pallas_call(
        functools.partial(_combine_kernel, first_k=first_k, kblock=kblock),
        grid_spec=grid_spec,
        out_shape=jax.ShapeDtypeStruct((t, D_MODEL), F32),
        compiler_params=pltpu.CompilerParams(
            dimension_semantics=("arbitrary",),
            vmem_limit_bytes=_vmem_limit(2 * max_rows * D_MODEL * 2 + 16 * tm * D_MODEL * 4)),
        name="combine",
    )(tile_start, tile_nch, tile_wide, x1, mod3, g_final, slot_t, y)


def _rope_tables(n):
    f32 = np.float32
    rows = n // GRID_W
    row = np.repeat(np.arange(rows, dtype=f32), GRID_W)
    col = np.tile(np.arange(GRID_W, dtype=f32), rows)
    inv = (f32(ROPE_THETA) ** (-np.arange(ROPE_FREQS, dtype=f32) / f32(ROPE_FREQS))).astype(f32)
    ang_r = row[:, None] * inv
    ang_c = col[:, None] * inv
    zero = np.zeros_like(ang_r)
    cos = np.concatenate([np.cos(ang_r)] * 2 + [np.cos(ang_c)] * 2, axis=1)
    sin_a = np.concatenate([-np.sin(ang_r), zero, -np.sin(ang_c), zero], axis=1)
    sin_b = np.concatenate([zero, np.sin(ang_r), zero, np.sin(ang_c)], axis=1)
    return tuple(jnp.asarray(t, F32) for t in (cos, sin_a, sin_b))


def _token_group(x, mod_mix, mod3, mod_row, seq, weights, rope_tabs, ctx_kv, late_mod=None):
    (norm_mix, w_in, sink, pool_w, pool_scale, w_out, norm_ffn, w_router,
     w_gate, w_up, w_down, norm_final) = weights
    b = x.shape[0]
    t = b * seq
    x2 = x.reshape(t, D_MODEL)
    q, k, v, p, *state = _in_projection(x2, mod_mix, mod_row(PROJ_TILE), norm_mix, w_in,
                                        rope_tabs, w_out, late_mod)
    if rope_tabs is None:
        late = state.pop()
        mod3 = jnp.concatenate([mod_mix.reshape(MOD_ROWS, -1), late], axis=1).reshape(
            MOD_ROWS, N_MOD, D_MODEL)
    w_in_bf = w_in if rope_tabs is not None else state.pop()
    w_out_bf = w_out if rope_tabs is not None else state.pop()
    if ctx_kv is None:
        attn = _context_attention(q, k, v, sink, seq)
    else:
        attn = _latent_attention(q, k, v, ctx_kv[0], ctx_kv[1], sink, seq)
    x1, h_packed, aff = _out_projection(attn, p, x2, mod3, mod_row(PROJ_TILE), norm_ffn,
                                        w_out_bf, w_router, pool_w, pool_scale, seq)

    cap = EC_FACTOR * t // N_EXPERTS
    idx, slot_t, off3 = _routing(aff, cap)
    table = h_packed.reshape(-1, LANES)
    per_range = N_EXPERTS // FFN_RANGES
    xs = []
    for r in range(FFN_RANGES):
        ids = _packed_row_ids(idx[r * per_range:(r + 1) * per_range], cap)
        xs.append(_gather_rows(table, ids).reshape(per_range, HQ_TILES, cap, LANES))

    chunks_per_tile = TOKEN_TILE // LANES
    tile_off = off3[:, ::chunks_per_tile, 0]
    tile_end = jnp.concatenate([tile_off[:, 1:], jnp.full((N_EXPERTS, 1), cap, I32)], axis=1)
    tile_start = (tile_off // SUBLANES) * SUBLANES
    tile_nch = jnp.where(tile_end > tile_off,
                         (tile_end - tile_start + COMBINE_CHUNK - 1) // COMBINE_CHUNK, 0)
    tile_wide = jnp.any(tile_end - tile_start > COMBINE_WINDOW, axis=0).astype(I32)

    def finish(y):
        out = _combine(x1, mod3, mod_row(TOKEN_TILE), norm_final, slot_t, tile_start.T,
                       tile_nch.T, tile_wide, y)
        return out.reshape(b, seq, D_MODEL)
    return xs, finish, state, (w_in_bf, w_out_bf, mod3)


def kernel(x_prompt, x_sample, c, cache_k, cache_v, c_ctx, w_ada, b_ada, norm_mix, w_in,
           sink_logits, pool_w, pool_scale, w_out, norm_ffn, w_router, w_gate, w_up, w_down,
           norm_final):
    n_b, seq, _ = x_prompt.shape
    n_db, n_lat, _ = x_sample.shape
    assert 1 + n_db <= MOD_ROWS and seq == TOKEN_TILE and n_lat % PROJ_TILE == 0
    assert w_in.shape[0] == 1 and cache_k.shape[1] == 1
    assert (n_b * seq) % PROJ_TILE == 0

    cond = jnp.concatenate(
        [c_ctx[None, :], c, jnp.zeros((MOD_ROWS - 1 - n_db, D_MODEL), F32)], axis=0)
    mixer_cols = 2 * D_MODEL
    mod_mix = _modulation(cond, w_ada[0], b_ada[0], mixer_cols).reshape(MOD_ROWS, 2, D_MODEL)

    w_router_bf = jnp.pad(w_router[0], ((0, 0), (0, LANES - N_EXPERTS))).astype(BF16)
    weights = (norm_mix[0][None, :], w_in[0], sink_logits[0], pool_w[0].astype(BF16),
               pool_scale[0][None, :], w_out[0], norm_ffn[0][None, :], w_router_bf,
               w_gate.reshape(w_gate.shape[1:]), w_up.reshape(w_up.shape[1:]),
               w_down.reshape(w_down.shape[1:]), norm_final[None, :])

    xs_p, finish_p, (k_p, v_p), (w_in_bf, w_out_bf, mod3) = _token_group(
        x_prompt, mod_mix, None, lambda tile: (lambda i: 0), seq, weights, None, None,
        late_mod=(cond, w_ada[0], b_ada[0], mixer_cols))
    weights = weights[:1] + (w_in_bf,) + weights[2:5] + (w_out_bf,) + weights[6:]

    ck = cache_k.reshape(-1, HEAD_DIM)
    cv = cache_v.reshape(-1, HEAD_DIM)
    xs_l, finish_l, _, _ = _token_group(
        x_sample, mod_mix, mod3, lambda tile: (lambda i: 1 + i // (n_lat // tile)), n_lat, weights,
        _rope_tables(n_lat), (ck, cv))

    ys = None
    for r in range(FFN_RANGES):
        ys = _expert_ffn([xs_p[r], xs_l[r]], weights[8], weights[9], weights[10],
                         r * (N_EXPERTS // FFN_RANGES), ys)
    y_prompt = finish_p(ys[0])
    y_sample = finish_l(ys[1])

    state_k = k_p.reshape(n_b, 1, seq, N_KV_HEADS, HEAD_DIM)
    state_v = v_p.reshape(n_b, 1, seq, N_KV_HEADS, HEAD_DIM)
    return (y_prompt, y_sample, state_k, state_v)
```

```python
import functools

import jax
import jax.numpy as jnp
import numpy as np
from jax import lax
from jax.experimental import pallas as pl
from jax.experimental.pallas import tpu as pltpu
from jax.experimental.pallas import tpu_sc as plsc

F32 = jnp.float32
BF16 = jnp.bfloat16
I32 = jnp.int32

D_MODEL = 2048
N_HEADS = 8
N_KV_HEADS = 2
HEAD_DIM = 128
Q_PER_KV = N_HEADS // N_KV_HEADS
ATTN_WIDTH = N_HEADS * HEAD_DIM
KV_WIDTH = N_KV_HEADS * HEAD_DIM
POOL_WIDTH = D_MODEL - ATTN_WIDTH
POOL_SIZES = (2, 4, 8, 16)
POOL_GROUP = POOL_WIDTH // len(POOL_SIZES)
IN_WIDTH = ATTN_WIDTH + 2 * KV_WIDTH + POOL_WIDTH
WINDOW = 128
BLOCK = 128
GRID_W = 64
ROPE_THETA = 10000.0
ROPE_FREQS = HEAD_DIM // 4
N_EXPERTS = 16
EC_FACTOR = 2
D_EXPERT = 1024
N_MOD = 6
EPS = 1e-6
NEG = -1e30
LOG2_E = 1.4426950408889634
ATTN_SCALE = HEAD_DIM ** -0.5 * LOG2_E

LANES = 128
SUBLANES = 8
BF16_ROWS = 16
VMEM_CAP = 64 * 1024 * 1024
SC_CORES = 2
SC_SUBCORES = 16

MOD_ROWS = 8
TOKEN_TILE = 256
PROJ_TILE = 512
SUB_TILE = 256
HQ_TILES = D_MODEL // 2 // LANES + 1
GATHER_CHUNK = 128
FFN_RANGES = 2
POOL_HALO = 8
COMBINE_CHUNK = BF16_ROWS
COMBINE_WINDOW = 64
ROW_SPLIT = 32


def _vmem_limit(nbytes):
    return int(min(VMEM_CAP - (4 << 20), max(nbytes, 16 << 20)))


def _mod_block(c_ref, w_ref, b_ref):
    c = c_ref[...]
    s = c * jax.nn.sigmoid(c)
    return jnp.dot(s.astype(BF16), w_ref[...].astype(BF16),
                   preferred_element_type=F32) + b_ref[...]


def _mod_kernel(c_ref, w_ref, b_ref, o_ref):
    o_ref[...] = _mod_block(c_ref, w_ref, b_ref)


def _modulation(cond, w_ada, b_ada, n):
    tn = 1024
    return pl.pallas_call(
        _mod_kernel,
        grid=(n // tn,),
        in_specs=[pl.BlockSpec((MOD_ROWS, D_MODEL), lambda j: (0, 0)),
                  pl.BlockSpec((D_MODEL, tn), lambda j: (0, j)),
                  pl.BlockSpec((1, tn), lambda j: (0, j))],
        out_specs=pl.BlockSpec((MOD_ROWS, tn), lambda j: (0, j)),
        out_shape=jax.ShapeDtypeStruct((MOD_ROWS, n), F32),
        compiler_params=pltpu.CompilerParams(
            vmem_limit_bytes=_vmem_limit(3 * D_MODEL * tn * 4)),
        name="modulation",
    )(cond, w_ada, b_ada.reshape(1, -1))


def _norm_mod(x, g, shift, scale):
    ms = jnp.mean(x * x, axis=-1, keepdims=True)
    y = x * lax.rsqrt(ms + EPS)
    return (y * g) * (1.0 + scale) + shift


def _inproj_kernel(*refs, rope):
    if rope:
        x_ref, mod_ref, g_ref, w_ref, cos_ref, sa_ref, sb_ref, q_ref, k_ref, v_ref, p_ref = refs
    else:
        (x_ref, mod_ref, g_ref, w_ref, wo_slab_ref, wi_slab_ref, cond_ref, wada_ref, bada_ref,
         q_ref, k_ref, v_ref, p_ref, ks_ref, vs_ref, wo_bf_ref, wi_bf_ref, late_mod_ref) = refs
        wo_bf_ref[...] = wo_slab_ref[...].astype(BF16)
        wi_bf_ref[...] = wi_slab_ref[...].astype(BF16)
        late_mod_ref[...] = _mod_block(cond_ref, wada_ref, bada_ref)
    mod = mod_ref[0]
    for s in range(x_ref.shape[0] // SUB_TILE):
        rows = slice(s * SUB_TILE, (s + 1) * SUB_TILE)
        h = _norm_mod(x_ref[rows, :], g_ref[...], mod[0:1], mod[1:2])
        u = jnp.dot(h.astype(w_ref.dtype), w_ref[...], preferred_element_type=F32)

        def rot(xh, rows=rows):
            return (xh * cos_ref[rows, :] + pltpu.roll(xh, LANES - ROPE_FREQS, 1) * sa_ref[rows, :]
                    + pltpu.roll(xh, ROPE_FREQS, 1) * sb_ref[rows, :])

        for hd in range(N_HEADS):
            xh = u[:, hd * HEAD_DIM:(hd + 1) * HEAD_DIM] * ATTN_SCALE
            q_ref[rows, hd * HEAD_DIM:(hd + 1) * HEAD_DIM] = (rot(xh) if rope else xh).astype(BF16)
        for hd in range(N_KV_HEADS):
            lo = ATTN_WIDTH + hd * HEAD_DIM
            xh = u[:, lo:lo + HEAD_DIM]
            k_ref[rows, hd * HEAD_DIM:(hd + 1) * HEAD_DIM] = rot(xh) if rope else xh
        v_ref[rows, :] = u[:, ATTN_WIDTH + KV_WIDTH:ATTN_WIDTH + 2 * KV_WIDTH]
        p_ref[rows, :] = u[:, ATTN_WIDTH + 2 * KV_WIDTH:]
        if not rope:
            for hd in range(N_KV_HEADS):
                state_rows = pl.ds(s * SUB_TILE * N_KV_HEADS + hd, SUB_TILE, stride=N_KV_HEADS)
                lo = ATTN_WIDTH + hd * HEAD_DIM
                ks_ref[state_rows, :] = u[:, lo:lo + HEAD_DIM]
                vs_ref[state_rows, :] = u[:, lo + KV_WIDTH:lo + KV_WIDTH + HEAD_DIM]


def _in_projection(x2, mod3, mod_row, g, w_in, rope_tabs, w_out, late_mod):
    t = x2.shape[0]
    tm = PROJ_TILE
    rope = rope_tabs is not None
    row = lambda i: (i, 0)
    w_mode = None if rope else pl.Buffered(1)
    in_specs = [pl.BlockSpec((tm, D_MODEL), row),
                pl.BlockSpec((1,) + mod3.shape[1:], lambda i: (mod_row(i), 0, 0)),
                pl.BlockSpec((1, D_MODEL), lambda i: (0, 0)),
                pl.BlockSpec((D_MODEL, IN_WIDTH), lambda i: (0, 0), pipeline_mode=w_mode)]
    args = [x2, mod3, g, w_in]
    if rope:
        n_seq = rope_tabs[0].shape[0]
        seq_blocks = n_seq // tm
        for tab in rope_tabs:
            in_specs.append(pl.BlockSpec((tm, HEAD_DIM), lambda i: (i % seq_blocks, 0)))
            args.append(tab)
    else:
        slab = D_MODEL // (t // tm)
        assert slab * (t // tm) == D_MODEL and slab % BF16_ROWS == 0
        for w in (w_out, w_in):
            in_specs.append(pl.BlockSpec((slab, w.shape[1]), row))
            args.append(w)
        cond, w_ada, b_ada, first_col = late_mod
        late_cols = (w_ada.shape[1] - first_col) // (t // tm)
        first_block = first_col // late_cols
        assert first_block * late_cols == first_col and late_cols % LANES == 0
        in_specs += [pl.BlockSpec((MOD_ROWS, D_MODEL), lambda i: (0, 0)),
                     pl.BlockSpec((D_MODEL, late_cols), lambda i: (0, first_block + i)),
                     pl.BlockSpec((1, late_cols), lambda i: (0, first_block + i))]
        args += [cond, w_ada, b_ada.reshape(1, -1)]
    out_specs = [pl.BlockSpec((tm, ATTN_WIDTH), row),
                 pl.BlockSpec((tm, KV_WIDTH), row),
                 pl.BlockSpec((tm, KV_WIDTH), row),
                 pl.BlockSpec((tm, POOL_WIDTH), row)]
    out_shape = [jax.ShapeDtypeStruct((t, ATTN_WIDTH), BF16),
                 jax.ShapeDtypeStruct((t, KV_WIDTH), F32),
                 jax.ShapeDtypeStruct((t, KV_WIDTH), F32),
                 jax.ShapeDtypeStruct((t, POOL_WIDTH), F32)]
    if not rope:
        for _ in range(2):
            out_specs.append(pl.BlockSpec((tm * N_KV_HEADS, HEAD_DIM), row))
            out_shape.append(jax.ShapeDtypeStruct((t * N_KV_HEADS, HEAD_DIM), F32))
        for w in (w_out, w_in):
            out_specs.append(pl.BlockSpec((slab, w.shape[1]), row))
            out_shape.append(jax.ShapeDtypeStruct(w.shape, BF16))
        out_specs.append(pl.BlockSpec((MOD_ROWS, late_cols), lambda i: (0, i)))
        out_shape.append(jax.ShapeDtypeStruct((MOD_ROWS, w_ada.shape[1] - first_col), F32))
    return pl.pallas_call(
        functools.partial(_inproj_kernel, rope=rope),
        grid=(t // tm,),
        in_specs=in_specs,
        out_specs=out_specs,
        out_shape=out_shape,
        compiler_params=pltpu.CompilerParams(
            vmem_limit_bytes=_vmem_limit(2 * D_MODEL * IN_WIDTH * 2 + 24 * tm * D_MODEL * 4)),
        name="in_projection",
    )(*args)


def _softmax_pv(s_list, v_list, sink_col):
    m = sink_col
    for s in s_list:
        m = jnp.maximum(m, jnp.max(s, axis=-1, keepdims=True))
    denom = jnp.exp2(sink_col - m)
    out = None
    for s, v in zip(s_list, v_list):
        e = jnp.exp2(s - m)
        if v.shape[1] == HEAD_DIM:
            denom = denom + jnp.sum(e, axis=-1, keepdims=True)
        o = jnp.dot(e.astype(BF16), v, preferred_element_type=F32)
        out = o if out is None else out + o
    if out.shape[1] > HEAD_DIM:
        denom = denom + out[:, HEAD_DIM:HEAD_DIM + 1]
    return out[:, 0:HEAD_DIM] * (1.0 / denom)


def _with_ones(v):
    return jnp.concatenate([v.astype(BF16), jnp.ones(v.shape, BF16)], axis=1)


def _stack_heads(q, kv):
    return jnp.concatenate(
        [q[:, (kv * Q_PER_KV + g) * HEAD_DIM:(kv * Q_PER_KV + g + 1) * HEAD_DIM]
         for g in range(Q_PER_KV)], axis=0)


def _sink_column(sink_ref, kv, rows):
    r = lax.broadcasted_iota(I32, (Q_PER_KV * rows, 1), 0)
    col = jnp.zeros((Q_PER_KV * rows, 1), F32)
    for g in range(Q_PER_KV):
        col = jnp.where((r >= g * rows) & (r < (g + 1) * rows), sink_ref[kv * Q_PER_KV + g], col)
    return col * LOG2_E


def _qk(q, k):
    return lax.dot_general(q, k, (((1,), (1,)), ((), ())), preferred_element_type=F32)


def _ctx_attn_kernel(sink_ref, q_ref, k_ref, v_ref, o_ref, *, seq):
    rows = seq
    for sq in range(q_ref.shape[0] // seq):
        rs = slice(sq * seq, (sq + 1) * seq)
        q = q_ref[rs, :]
        for kv in range(N_KV_HEADS):
            kh = k_ref[rs, kv * HEAD_DIM:(kv + 1) * HEAD_DIM].astype(BF16)
            vh = v_ref[rs, kv * HEAD_DIM:(kv + 1) * HEAD_DIM].astype(BF16)
            qs = _stack_heads(q, kv)
            s = _qk(qs, kh)
            o = _softmax_pv([s], [vh], _sink_column(sink_ref, kv, rows))
            for g in range(Q_PER_KV):
                hd = kv * Q_PER_KV + g
                o_ref[rs, hd * HEAD_DIM:(hd + 1) * HEAD_DIM] = (
                    o[g * rows:(g + 1) * rows].astype(BF16))


def _context_attention(q, k, v, sink, seq):
    t = q.shape[0]
    row = lambda b: (b, 0)
    per_step = next(n for n in (4, 2, 1) if (t // seq) % n == 0)
    tm = per_step * seq
    return pl.pallas_call(
        functools.partial(_ctx_attn_kernel, seq=seq),
        grid=(t // tm,),
        in_specs=[pl.BlockSpec(memory_space=pltpu.SMEM),
                  pl.BlockSpec((tm, ATTN_WIDTH), row),
                  pl.BlockSpec((tm, KV_WIDTH), row),
                  pl.BlockSpec((tm, KV_WIDTH), row)],
        out_specs=pl.BlockSpec((tm, ATTN_WIDTH), row),
        out_shape=jax.ShapeDtypeStruct((t, ATTN_WIDTH), BF16),
        name="context_attention",
    )(sink, q, k, v)


def _lat_attn_kernel(sink_ref, q_ref, k_ref, v_ref, ck_ref, cv_ref, o_ref, *, n_seq):
    band = 3 * BLOCK
    rows = Q_PER_KV * BLOCK
    per_step = q_ref.shape[0] // BLOCK
    cached = [pl.ds(kv, ck_ref.shape[0] // N_KV_HEADS, stride=N_KV_HEADS)
              for kv in range(N_KV_HEADS)]
    ck = [ck_ref[c, :].astype(BF16) for c in cached]
    cv = [_with_ones(cv_ref[c, :]) for c in cached]
    for j in range(per_step):
        i = pl.program_id(1) * per_step + j
        start = pl.multiple_of(jnp.clip((i - 1) * BLOCK, 0, n_seq - band), BLOCK)
        qpos = i * BLOCK + lax.broadcasted_iota(I32, (rows, band), 0) % BLOCK
        kpos = start + lax.broadcasted_iota(I32, (rows, band), 1)
        mask = jnp.abs(kpos - qpos) <= WINDOW
        rs = slice(j * BLOCK, (j + 1) * BLOCK)
        q = q_ref[rs, :]
        for kv in range(N_KV_HEADS):
            cols = slice(kv * HEAD_DIM, (kv + 1) * HEAD_DIM)
            kb = k_ref[pl.ds(start, band), cols].astype(BF16)
            vb = _with_ones(v_ref[pl.ds(start, band), cols])
            qs = _stack_heads(q, kv)
            s_loc = jnp.where(mask, _qk(qs, kb), NEG)
            s_ctx = _qk(qs, ck[kv])
            o = _softmax_pv([s_loc, s_ctx], [vb, cv[kv]], _sink_column(sink_ref, kv, BLOCK))
            for g in range(Q_PER_KV):
                hd = kv * Q_PER_KV + g
                o_ref[rs, hd * HEAD_DIM:(hd + 1) * HEAD_DIM] = (
                    o[g * BLOCK:(g + 1) * BLOCK].astype(BF16))


def _latent_attention(q, k, v, ck, cv, sink, n_seq):
    t = q.shape[0]
    per_step = next(n for n in (4, 2, 1) if (n_seq // BLOCK) % n == 0)
    tq = per_step * BLOCK
    nb = n_seq // tq
    cache_rows = ck.shape[0] // (t // n_seq)
    return pl.pallas_call(
        functools.partial(_lat_attn_kernel, n_seq=n_seq),
        grid=(t // n_seq, nb),
        in_specs=[pl.BlockSpec(memory_space=pltpu.SMEM),
                  pl.BlockSpec((tq, ATTN_WIDTH), lambda b, i: (b * nb + i, 0)),
                  pl.BlockSpec((n_seq, KV_WIDTH), lambda b, i: (b, 0)),
                  pl.BlockSpec((n_seq, KV_WIDTH), lambda b, i: (b, 0)),
                  pl.BlockSpec((cache_rows, HEAD_DIM), lambda b, i: (b, 0)),
                  pl.BlockSpec((cache_rows, HEAD_DIM), lambda b, i: (b, 0))],
        out_specs=pl.BlockSpec((tq, ATTN_WIDTH), lambda b, i: (b * nb + i, 0)),
        out_shape=jax.ShapeDtypeStruct((t, ATTN_WIDTH), BF16),
        name="latent_attention",
    )(sink, q, k, v, ck, cv)


def _pool_group(p_ref, r0, seq, w_ref, s_ref, g):
    n = SUB_TILE
    rows = n + 2 * POOL_HALO
    static = isinstance(r0, int)
    t0 = r0 % seq
    t = t0 + lax.broadcasted_iota(I32, (n, 1), 0)
    has_top = t0 > 0
    has_bottom = t0 + n < seq
    zeros = jnp.zeros((POOL_HALO, POOL_GROUP), F32)
    w = POOL_SIZES[g]
    cols = slice(g * POOL_GROUP, (g + 1) * POOL_GROUP)
    pg = p_ref[pl.ds(r0, n), cols]
    if static:
        top = p_ref[r0 - POOL_HALO:r0, cols] if has_top else zeros
        bottom = p_ref[r0 + n:r0 + n + POOL_HALO, cols] if has_bottom else zeros
    else:
        top_row = pl.multiple_of(jnp.maximum(r0 - POOL_HALO, 0), POOL_HALO)
        bottom_row = pl.multiple_of(jnp.minimum(r0 + n, p_ref.shape[0] - POOL_HALO), POOL_HALO)
        top = jnp.where(has_top, p_ref[pl.ds(top_row, POOL_HALO), cols], 0.0)
        bottom = jnp.where(has_bottom, p_ref[pl.ds(bottom_row, POOL_HALO), cols], 0.0)
    x = jnp.concatenate([top, pg, bottom], axis=0)
    fwd = x
    span = 1
    while span < w // 2:
        fwd = fwd + pltpu.roll(fwd, rows - span, 0)
        span *= 2
    if (w // 2) % SUBLANES == 0:
        wsum = fwd[POOL_HALO - w // 2:POOL_HALO - w // 2 + n] + fwd[POOL_HALO:POOL_HALO + n]
    else:
        wsum = (fwd + pltpu.roll(fwd, w // 2, 0))[POOL_HALO:POOL_HALO + n]
    lo = jnp.maximum(t - w // 2, 0)
    hi = jnp.minimum(t + w - w // 2, seq)
    inv_cnt = 1.0 / (hi - lo).astype(F32)
    mixed = wsum * inv_cnt - pg
    y = jnp.dot(mixed.astype(BF16), w_ref[g], preferred_element_type=F32)
    return (y * s_ref[:, cols]).astype(BF16)


def _pack_pair(lo, hi):
    return lax.bitcast_convert_type(pltpu.pack_elementwise([lo, hi], packed_dtype=BF16), I32)


def _unpack_pair(words):
    lo = pltpu.unpack_elementwise(words, index=0, packed_dtype=BF16, unpacked_dtype=F32)
    hi = pltpu.unpack_elementwise(words, index=1, packed_dtype=BF16, unpacked_dtype=F32)
    return lo, hi


def _outproj_kernel(a_ref, p_ref, x_ref, mod_ref, g_ref, wo_ref, wr_ref, pw_ref, ps_ref,
                    x1_ref, h_ref, aff_ref, *, seq):
    mod = mod_ref[0]
    groups = SUB_TILE // SUBLANES
    half = D_MODEL // 2
    tm = x_ref.shape[0]
    steps_per_p_block = p_ref.shape[0] // tm
    for s in range(tm // SUB_TILE):
        rows = slice(s * SUB_TILE, (s + 1) * SUB_TILE)
        grp = slice(s * groups, (s + 1) * groups)
        r0 = s * SUB_TILE
        if steps_per_p_block > 1:
            r0 = pl.multiple_of((pl.program_id(0) % steps_per_p_block) * tm + r0, SUB_TILE)
        pooled = jnp.concatenate([_pool_group(p_ref, r0, seq, pw_ref, ps_ref, g)
                                  for g in range(len(POOL_SIZES))], axis=1)
        mix = (jnp.dot(a_ref[rows, :], wo_ref[0:ATTN_WIDTH, :], preferred_element_type=F32)
               + jnp.dot(pooled, wo_ref[ATTN_WIDTH:D_MODEL, :], preferred_element_type=F32))
        x1 = x_ref[rows, :] + mod[2:3] * mix
        x1_ref[rows, :] = x1
        h = _norm_mod(x1, g_ref[...], mod[3:4], mod[4:5])
        logits = jnp.dot(h.astype(BF16), wr_ref[...], preferred_element_type=F32)
        lane = lax.broadcasted_iota(I32, logits.shape, 1)
        logits = jnp.where(lane < N_EXPERTS, logits, -jnp.inf)
        m = jnp.max(logits, axis=-1, keepdims=True)
        e = jnp.exp(logits - m)
        aff = e / jnp.sum(e, axis=-1, keepdims=True)
        aff_ref[rows, :] = aff[:, 0:N_EXPERTS]
        for c in range(HQ_TILES - 1):
            words = _pack_pair(h[:, c * LANES:(c + 1) * LANES],
                               h[:, half + c * LANES:half + (c + 1) * LANES])
            h_ref[grp, c * SUBLANES:(c + 1) * SUBLANES, :] = (
                words.reshape(groups, SUBLANES, LANES))
        h_ref[grp, (HQ_TILES - 1) * SUBLANES:, :] = (
            lax.bitcast_convert_type(aff, I32).reshape(groups, SUBLANES, LANES))


def _out_projection(attn, p, x2, mod3, mod_row, g, w_out_bf, w_router, pool_w, pool_scale, seq):
    t = x2.shape[0]
    tm = PROJ_TILE
    row = lambda i: (i, 0)
    p_rows = max(tm, seq)
    steps_per_p_block = p_rows // tm
    return pl.pallas_call(
        functools.partial(_outproj_kernel, seq=seq),
        grid=(t // tm,),
        in_specs=[pl.BlockSpec((tm, ATTN_WIDTH), row),
                  pl.BlockSpec((p_rows, POOL_WIDTH), lambda i: (i // steps_per_p_block, 0)),
                  pl.BlockSpec((tm, D_MODEL), row),
                  pl.BlockSpec((1, N_MOD, D_MODEL), lambda i: (mod_row(i), 0, 0)),
                  pl.BlockSpec((1, D_MODEL), lambda i: (0, 0)),
                  pl.BlockSpec((D_MODEL, D_MODEL), lambda i: (0, 0)),
                  pl.BlockSpec((D_MODEL, LANES), lambda i: (0, 0)),
                  pl.BlockSpec((len(POOL_SIZES), POOL_GROUP, POOL_GROUP), lambda i: (0, 0, 0)),
                  pl.BlockSpec((1, POOL_WIDTH), lambda i: (0, 0))],
        out_specs=[pl.BlockSpec((tm, D_MODEL), row),
                   pl.BlockSpec((tm // SUBLANES, HQ_TILES * SUBLANES, LANES), lambda i: (i, 0, 0)),
                   pl.BlockSpec((tm, N_EXPERTS), row)],
        out_shape=[jax.ShapeDtypeStruct((t, D_MODEL), F32),
                   jax.ShapeDtypeStruct((t // SUBLANES, HQ_TILES * SUBLANES, LANES), I32),
                   jax.ShapeDtypeStruct((t, N_EXPERTS), F32)],
        compiler_params=pltpu.CompilerParams(
            vmem_limit_bytes=_vmem_limit(2 * D_MODEL * D_MODEL * 2 + 24 * tm * D_MODEL * 4)),
        name="out_projection",
    )(attn, p, x2, mod3, g, w_out_bf, w_router, pool_w, pool_scale)


def _route_kernel(a_ref, idx_ref, slot_ref, off_ref, slot_scr, *, cap):
    a = a_ref[...]
    n_e, n_c, _ = a.shape
    rows = n_e * n_c

    def enough(cand):
        cand_f = lax.bitcast_convert_type(cand, F32)
        cnt = jnp.sum(jnp.sum((a >= cand_f).astype(F32), axis=1, keepdims=True),
                      axis=2, keepdims=True)
        return cnt >= cap

    def two_bits(it, thr):
        low = 28 - 2 * it
        for setting in (1, 2, 3):
            cand = thr | jnp.left_shift(jnp.int32(setting), low)
            best = jnp.where(enough(cand), cand, thr if setting == 1 else best)
        return best

    top = jnp.full((n_e, 1, 1), 1 << 30, I32)
    thr = jnp.where(enough(top), top, 0)
    thr = lax.fori_loop(0, 15, two_bits, thr)
    thr_f = lax.bitcast_convert_type(thr, F32)
    gt = (a > thr_f).astype(F32).reshape(rows, LANES)
    eq = (a == thr_f).astype(F32).reshape(rows, LANES)

    li = lax.broadcasted_iota(I32, (LANES, LANES), 0)
    lj = lax.broadcasted_iota(I32, (LANES, LANES), 1)
    upper_incl = (li <= lj).astype(BF16)
    ci = lax.broadcasted_iota(I32, (n_c, n_c), 0)
    cj = lax.broadcasted_iota(I32, (n_c, n_c), 1)
    before = (cj < ci).astype(BF16)
    whole = jnp.ones((n_c, n_c), BF16)

    def over_chunks(mat, col):
        wide = jnp.broadcast_to(col, (rows, LANES)).astype(BF16)
        side_by_side = jnp.concatenate([wide[e * n_c:(e + 1) * n_c] for e in range(n_e)], axis=1)
        res = jnp.dot(mat, side_by_side, preferred_element_type=F32)
        return jnp.concatenate([res[:, e * LANES:(e + 1) * LANES] for e in range(n_e)], axis=0)

    def prefix(x):
        incl = jnp.dot(x.astype(BF16), upper_incl, preferred_element_type=F32)
        tot = incl[:, LANES - 1:LANES]
        return incl, tot, over_chunks(before, tot)

    n_gt = over_chunks(whole, jnp.sum(gt, axis=1, keepdims=True))
    need = cap - n_gt
    incl_eq, _, off_eq = prefix(eq)
    rank_eq = off_eq + incl_eq - eq
    sel = jnp.where((eq > 0) & (rank_eq < need), 1.0, gt)
    incl, tot, off = prefix(sel)
    slot = off + incl - sel
    slot_scr[...] = jnp.where(sel > 0, slot, -1.0)
    for c in range(n_c):
        per_expert = slot_scr[pl.ds(c, n_e, stride=n_c), :]
        slot_ref[c * LANES:(c + 1) * LANES, :] = per_expert.T.astype(I32)
    off_ref[...] = off.astype(I32).reshape(n_e, n_c, LANES)

    s_lane = lax.broadcasted_iota(I32, (1, cap), 1).astype(F32)
    c_col = lax.broadcasted_iota(I32, (n_c, 1), 0).astype(F32)
    for e in range(n_e):
        r0 = e * n_c
        incl_e = incl[r0:r0 + n_c]
        off_e = off[r0:r0 + n_c, 0:1]
        tot_e = tot[r0:r0 + n_c]
        onehot = ((off_e <= s_lane) & (s_lane < off_e + tot_e)).astype(F32)
        counts = lax.dot_general(incl_e.astype(BF16), onehot.astype(BF16),
                                 (((0,), (0,)), ((), ())), preferred_element_type=F32)
        local = s_lane - jnp.sum(onehot * off_e, axis=0, keepdims=True)
        lane = jnp.sum((counts <= local).astype(F32), axis=0, keepdims=True)
        chunk = jnp.sum(onehot * c_col, axis=0, keepdims=True)
        idx_ref[e] = (chunk * LANES + lane).astype(I32)


def _routing(aff, cap):
    t = aff.shape[0]
    n_c = t // LANES
    a3 = aff.T.reshape(N_EXPERTS, n_c, LANES)
    return pl.pallas_call(
        functools.partial(_route_kernel, cap=cap),
        out_shape=[jax.ShapeDtypeStruct((N_EXPERTS, 1, cap), I32),
                   jax.ShapeDtypeStruct((t, N_EXPERTS), I32),
                   jax.ShapeDtypeStruct((N_EXPERTS, n_c, LANES), I32)],
        scratch_shapes=[pltpu.VMEM((N_EXPERTS * n_c, LANES), F32)],
        compiler_params=pltpu.CompilerParams(vmem_limit_bytes=_vmem_limit(48 << 20)),
        name="routing",
    )(a3)


def _gather_rows(table, row_ids):
    n_chunks = row_ids.shape[0]
    n_workers = SC_CORES * SC_SUBCORES
    per_worker = n_chunks // n_workers
    assert row_ids.shape[1] == GATHER_CHUNK and n_chunks % n_workers == 0
    mesh = plsc.VectorSubcoreMesh(core_axis_name="core", subcore_axis_name="subcore")

    @functools.partial(
        pl.kernel, mesh=mesh,
        out_type=jax.ShapeDtypeStruct((n_chunks * GATHER_CHUNK, LANES), I32),
        scratch_types=[pltpu.VMEM((per_worker, GATHER_CHUNK), I32),
                       pltpu.VMEM((2, GATHER_CHUNK, LANES), I32),
                       pltpu.SemaphoreType.DMA((2,)),
                       pltpu.SemaphoreType.DMA((2,))],
        name="gather_rows",
    )
    def gather(table_hbm, ids_hbm, out_hbm, ids_v, rows_v, gather_sem, store_sem):
        worker = lax.axis_index("subcore") * SC_CORES + lax.axis_index("core")
        first = worker * per_worker
        pltpu.sync_copy(ids_hbm.at[worker], ids_v)

        def fetch(j):
            return pltpu.make_async_copy(table_hbm.at[ids_v.at[j]], rows_v.at[j % 2],
                                         gather_sem.at[j % 2])

        def store(j):
            rows = pl.ds(pl.multiple_of((first + j) * GATHER_CHUNK, GATHER_CHUNK), GATHER_CHUNK)
            return pltpu.make_async_copy(rows_v.at[j % 2], out_hbm.at[rows], store_sem.at[j % 2])

        for j in range(per_worker):
            if j >= 2:
                store(j - 2).wait()
            fetch(j).start()
            if j >= 1:
                fetch(j - 1).wait()
                store(j - 1).start()
        fetch(per_worker - 1).wait()
        store(per_worker - 1).start()
        for j in range(max(per_worker - 2, 0), per_worker):
            store(j).wait()

    return gather(table, row_ids.reshape(n_workers, per_worker, GATHER_CHUNK))


def _packed_row_ids(idx, cap):
    tok = idx.reshape(-1, 1, cap // GATHER_CHUNK, GATHER_CHUNK)
    tile = jnp.arange(HQ_TILES, dtype=I32).reshape(1, HQ_TILES, 1, 1)
    ids = ((tok // SUBLANES) * HQ_TILES + tile) * SUBLANES + tok % SUBLANES
    return ids.reshape(-1, GATHER_CHUNK)


def _ffn_kernel(*refs, caps, n_f, row_chunk, first_expert, n_prior):
    n_g = len(caps)
    x_refs = refs[:n_g]
    wg_ref, wu_ref, wd_ref = refs[n_g:n_g + 3]
    y_refs = refs[n_g + 3 + n_prior:2 * n_g + 3 + n_prior]
    acc_ref = refs[2 * n_g + 3 + n_prior]
    e = pl.program_id(0)
    f = pl.program_id(1)

    @pl.when((e == 0) & (f == 0))
    def _():
        acc_ref[...] = jnp.zeros_like(acc_ref)

    def step(last):
        first = f == 0
        base = 0
        for x_ref, y_ref, cap in zip(x_refs, y_refs, caps):
            if last:
                aff = lax.bitcast_convert_type(x_ref[0, HQ_TILES - 1], F32)
                lane = lax.broadcasted_iota(I32, aff.shape, 1)
                gates = jnp.sum(jnp.where(lane == e + first_expert, aff, 0.0), axis=1,
                                keepdims=True)
                y_ref[0, cap:, :] = jnp.zeros((COMBINE_WINDOW, D_MODEL // 2), I32)
            for r in range(cap // row_chunk):
                rs = slice(r * row_chunk, (r + 1) * row_chunk)
                acc_rows = slice(base + r * row_chunk, base + (r + 1) * row_chunk)
                pairs = [_unpack_pair(x_ref[0, c, rs, :]) for c in range(HQ_TILES - 1)]
                x = jnp.concatenate([p[0] for p in pairs] + [p[1] for p in pairs], axis=1)
                gate_act = jnp.dot(x, wg_ref[0], preferred_element_type=F32)
                up = jnp.dot(x, wu_ref[0], preferred_element_type=F32)
                hid = (gate_act * jax.nn.sigmoid(gate_act)) * up
                part = jnp.dot(hid, wd_ref[0], preferred_element_type=F32)
                if last:
                    y = (part + acc_ref[acc_rows, :]) * gates[rs]
                    y_ref[0, rs, :] = _pack_pair(y[:, 0:D_MODEL // 2], y[:, D_MODEL // 2:])
                else:
                    acc_ref[acc_rows, :] = part + jnp.where(first, 0.0, acc_ref[acc_rows, :])
            base += cap

    @pl.when(f < n_f - 1)
    def _():
        step(False)

    @pl.when(f == n_f - 1)
    def _():
        step(True)


def _expert_ffn(xs_groups, w_gate, w_up, w_down, first_expert, prior_outputs):
    n_f = 4
    tf = D_EXPERT // n_f
    row_chunk = 512
    n_e = xs_groups[0].shape[0]
    e0 = first_expert
    caps = tuple(xs.shape[2] for xs in xs_groups)
    rows = sum(caps)
    est = (2 * HQ_TILES * rows * LANES * 4 + rows * D_MODEL * (4 + 2 * 2) + 2 * 3 * D_MODEL * tf * 4
           + row_chunk * (3 * tf + 2 * D_MODEL) * 4)
    x_specs = [pl.BlockSpec((1, HQ_TILES, cap, LANES), lambda e, f: (e, 0, 0, 0)) for cap in caps]
    y_shapes = [(N_EXPERTS, cap + COMBINE_WINDOW, D_MODEL // 2) for cap in caps]
    y_specs = [pl.BlockSpec((1,) + s[1:], lambda e, f: (e + e0, 0, 0)) for s in y_shapes]
    prior = list(prior_outputs or [])
    n_in = len(caps) + 3
    return pl.pallas_call(
        functools.partial(_ffn_kernel, caps=caps, n_f=n_f, row_chunk=row_chunk,
                          first_expert=e0, n_prior=len(prior)),
        grid=(n_e, n_f),
        in_specs=(x_specs + [pl.BlockSpec((1, D_MODEL, tf), lambda e, f: (e + e0, 0, f)),
                             pl.BlockSpec((1, D_MODEL, tf), lambda e, f: (e + e0, 0, f)),
                             pl.BlockSpec((1, tf, D_MODEL), lambda e, f: (e + e0, f, 0))]
                  + [pl.BlockSpec(memory_space=pl.ANY) for _ in prior]),
        out_specs=y_specs,
        out_shape=[jax.ShapeDtypeStruct(s, I32) for s in y_shapes],
        input_output_aliases={n_in + j: j for j in range(len(prior))},
        scratch_shapes=[pltpu.VMEM((rows, D_MODEL), F32)],
        compiler_params=pltpu.CompilerParams(
            dimension_semantics=("arbitrary", "arbitrary"),
            vmem_limit_bytes=_vmem_limit(est + (4 << 20))),
        name="expert_ffn",
    )(*xs_groups, w_gate, w_up, w_down, *prior)


def _combine_kernel(start_ref, nch_ref, wide_ref, x1_ref, mod_ref, g_ref, slot_ref, y_hbm, o_ref,
                    ybuf_ref, ffn_ref, sem, *, first_k, kblock):
    i = pl.program_id(0)
    tm = x1_ref.shape[0]
    cur = i % 2
    n_window_rows = N_EXPERTS * COMBINE_WINDOW

    window_head = COMBINE_WINDOW - COMBINE_CHUNK

    def window_copy(e, src_row, buf):
        return pltpu.make_async_copy(
            y_hbm.at[e, pl.ds(pl.multiple_of(src_row, SUBLANES), window_head)],
            ybuf_ref.at[buf, e * COMBINE_WINDOW:e * COMBINE_WINDOW + window_head], sem.at[buf])

    def window_tail_copy(e, src_row, buf):
        return chunk_copy(e, src_row + window_head, buf,
                          jnp.int32(e * COMBINE_WINDOW + window_head))

    def needs_tail(tile, e):
        return nch_ref[tile, e] * COMBINE_CHUNK > window_head

    def chunk_copy(e, src_row, buf, dst_row):
        return pltpu.make_async_copy(
            y_hbm.at[e, pl.ds(pl.multiple_of(src_row, SUBLANES), COMBINE_CHUNK)],
            ybuf_ref.at[buf, pl.ds(pl.multiple_of(dst_row, COMBINE_CHUNK), COMBINE_CHUNK)],
            sem.at[buf])

    def fetch(tile, buf):
        @pl.when(wide_ref[tile] == 0)
        def _():
            for e in range(N_EXPERTS):
                window_copy(e, start_ref[tile, e], buf).start()

                @pl.when(needs_tail(tile, e))
                def _(e=e):
                    window_tail_copy(e, start_ref[tile, e], buf).start()

        @pl.when(wide_ref[tile] != 0)
        def _():
            pos = jnp.int32(0)
            for e in range(N_EXPERTS):
                start = start_ref[tile, e]
                nch = nch_ref[tile, e]

                def issue(c, carry, e=e, start=start, pos=pos):
                    chunk_copy(e, start + c * COMBINE_CHUNK, buf, pos + c * COMBINE_CHUNK).start()
                    return carry
                lax.fori_loop(0, nch, issue, 0)
                pos = pos + nch * COMBINE_CHUNK

    @pl.when(i == 0)
    def _():
        ybuf_ref[...] = jnp.zeros_like(ybuf_ref)
        fetch(0, 0)

    @pl.when(i + 1 < pl.num_programs(0))
    def _():
        fetch(i + 1, 1 - cur)

    def finish(ffn):
        out = x1_ref[...] + mod_ref[0][5:6] * ffn
        ms = jnp.mean(out * out, axis=-1, keepdims=True)
        o_ref[...] = (out * lax.rsqrt(ms + EPS)) * g_ref[...]

    def apply_selection(sel, k0, width):
        lo, hi = _unpack_pair(ybuf_ref[cur, pl.ds(k0, width), :])
        return jnp.concatenate([jnp.dot(sel, lo, preferred_element_type=F32),
                                jnp.dot(sel, hi, preferred_element_type=F32)], axis=1)

    @pl.when(wide_ref[i] == 0)
    def _():
        for e in range(N_EXPERTS):
            window_copy(e, jnp.int32(0), cur).wait()

            @pl.when(needs_tail(i, e))
            def _(e=e):
                window_tail_copy(e, jnp.int32(0), cur).wait()
        expert_lane = lax.broadcasted_iota(I32, (1, N_EXPERTS), 1)
        starts = jnp.zeros((1, N_EXPERTS), I32)
        for e in range(N_EXPERTS):
            starts = jnp.where(expert_lane == e, start_ref[i, e], starts)
        slots = slot_ref[...]
        rows_in_window = jnp.where(slots >= 0, (slots - starts).astype(F32), -1.0)
        window_of_lane = lax.broadcasted_iota(I32, (N_EXPERTS, n_window_rows), 1) // COMBINE_WINDOW
        spread = (window_of_lane == lax.broadcasted_iota(I32, (N_EXPERTS, n_window_rows), 0))
        target = jnp.dot(rows_in_window.astype(BF16), spread.astype(BF16),
                         preferred_element_type=F32)
        lane_row = (lax.broadcasted_iota(I32, (tm, n_window_rows), 1) % COMBINE_WINDOW).astype(F32)
        sel = jnp.where(target == lane_row, 1.0, 0.0)
        ffn = None
        for k0 in range(0, n_window_rows, kblock):
            part = apply_selection(sel[:, k0:k0 + kblock], k0, kblock)
            ffn = part if ffn is None else ffn + part
        finish(ffn)

    @pl.when(wide_ref[i] != 0)
    def _():
        pos = jnp.int32(0)
        begins = []
        for e in range(N_EXPERTS):
            begins.append(pos)
            pos = pos + nch_ref[i, e] * COMBINE_CHUNK
        begins.append(pos)

        def drain(c, carry):
            chunk_copy(0, jnp.int32(0), cur, jnp.int32(0)).wait()
            return carry
        lax.fori_loop(0, pos // COMBINE_CHUNK, drain, 0)

        expert_lane = lax.broadcasted_iota(I32, (1, N_EXPERTS), 1)
        expert_row = lax.broadcasted_iota(I32, (N_EXPERTS, 1), 0)
        shift = jnp.zeros((1, N_EXPERTS), I32)
        range_lo = jnp.zeros((N_EXPERTS, 1), I32)
        range_hi = jnp.zeros((N_EXPERTS, 1), I32)
        for e in range(N_EXPERTS):
            shift = jnp.where(expert_lane == e, begins[e] - start_ref[i, e], shift)
            range_lo = jnp.where(expert_row == e, begins[e], range_lo)
            range_hi = jnp.where(expert_row == e, begins[e + 1], range_hi)
        slots = slot_ref[...]
        packed_row = slots + shift
        row_hi = jnp.where(slots >= 0, packed_row // ROW_SPLIT, -1).astype(F32).astype(BF16)
        row_lo = jnp.where(slots >= 0, packed_row % ROW_SPLIT, 0).astype(F32).astype(BF16)

        def selection(k0, width):
            col = k0 + lax.broadcasted_iota(I32, (N_EXPERTS, width), 1)
            spread = ((col >= range_lo) & (col < range_hi)).astype(BF16)
            want_hi = jnp.dot(row_hi, spread, preferred_element_type=F32)
            want_lo = jnp.dot(row_lo, spread, preferred_element_type=F32)
            here = k0 + lax.broadcasted_iota(I32, (1, width), 1)
            hit = ((want_hi == (here // ROW_SPLIT).astype(F32))
                   & (want_lo == (here % ROW_SPLIT).astype(F32)) & (here < pos))
            return jnp.where(hit, 1.0, 0.0)

        ffn_ref[...] = apply_selection(selection(0, first_k), 0, first_k)

        def kstep(kb, carry):
            k0 = pl.multiple_of(kb * kblock, kblock)
            ffn_ref[...] += apply_selection(selection(k0, kblock), k0, kblock)
            return carry
        lax.fori_loop(first_k // kblock, (pos + kblock - 1) // kblock, kstep, 0)
        finish(ffn_ref[...])


def _combine(x1, mod3, mod_row, g_final, slot_t, tile_start, tile_nch, tile_wide, y):
    t = x1.shape[0]
    tm = TOKEN_TILE
    first_k = 2 * tm + N_EXPERTS * COMBINE_CHUNK
    kblock = 256
    max_rows = N_EXPERTS * (tm + 2 * COMBINE_CHUNK)
    max_rows = -(-max_rows // kblock) * kblock
    row = lambda i, *_: (i, 0)
    grid_spec = pltpu.PrefetchScalarGridSpec(
        num_scalar_prefetch=3,
        grid=(t // tm,),
        in_specs=[pl.BlockSpec((tm, D_MODEL), row),
                  pl.BlockSpec((1, N_MOD, D_MODEL), lambda i, *_: (mod_row(i), 0, 0)),
                  pl.BlockSpec((1, D_MODEL), lambda i, *_: (0, 0)),
                  pl.BlockSpec((tm, N_EXPERTS), row),
                  pl.BlockSpec(memory_space=pl.ANY)],
        out_specs=pl.BlockSpec((tm, D_MODEL), row),
        scratch_shapes=[pltpu.VMEM((2, max_rows, D_MODEL // 2), I32),
                        pltpu.VMEM((tm, D_MODEL), F32),
                        pltpu.SemaphoreType.DMA((2,))],
    )
    return pl.pallas_call(
        functools.partial(_combine_kernel, first_k=first_k, kblock=kblock),
        grid_spec=grid_spec,
        out_shape=jax.ShapeDtypeStruct((t, D_MODEL), F32),
        compiler_params=pltpu.CompilerParams(
            dimension_semantics=("arbitrary",),
            vmem_limit_bytes=_vmem_limit(2 * max_rows * D_MODEL * 2 + 16 * tm * D_MODEL * 4)),
        name="combine",
    )(tile_start, tile_nch, tile_wide, x1, mod3, g_final, slot_t, y)


def _rope_tables(n):
    f32 = np.float32
    rows = n // GRID_W
    row = np.repeat(np.arange(rows, dtype=f32), GRID_W)
    col = np.tile(np.arange(GRID_W, dtype=f32), rows)
    inv = (f32(ROPE_THETA) ** (-np.arange(ROPE_FREQS, dtype=f32) / f32(ROPE_FREQS))).astype(f32)
    ang_r = row[:, None] * inv
    ang_c = col[:, None] * inv
    zero = np.zeros_like(ang_r)
    cos = np.concatenate([np.cos(ang_r)] * 2 + [np.cos(ang_c)] * 2, axis=1)
    sin_a = np.concatenate([-np.sin(ang_r), zero, -np.sin(ang_c), zero], axis=1)
    sin_b = np.concatenate([zero, np.sin(ang_r), zero, np.sin(ang_c)], axis=1)
    return tuple(jnp.asarray(t, F32) for t in (cos, sin_a, sin_b))


def _token_group(x, mod_mix, mod3, mod_row, seq, weights, rope_tabs, ctx_kv, late_mod=None):
    (norm_mix, w_in, sink, pool_w, pool_scale, w_out, norm_ffn, w_router,
     w_gate, w_up, w_down, norm_final) = weights
    b = x.shape[0]
    t = b * seq
    x2 = x.reshape(t, D_MODEL)
    q, k, v, p, *state = _in_projection(x2, mod_mix, mod_row(PROJ_TILE), norm_mix, w_in,
                                        rope_tabs, w_out, late_mod)
    if rope_tabs is None:
        late = state.pop()
        mod3 = jnp.concatenate([mod_mix.reshape(MOD_ROWS, -1), late], axis=1).reshape(
            MOD_ROWS, N_MOD, D_MODEL)
    w_in_bf = w_in if rope_tabs is not None else state.pop()
    w_out_bf = w_out if rope_tabs is not None else state.pop()
    if ctx_kv is None:
        attn = _context_attention(q, k, v, sink, seq)
    else:
        attn = _latent_attention(q, k, v, ctx_kv[0], ctx_kv[1], sink, seq)
    x1, h_packed, aff = _out_projection(attn, p, x2, mod3, mod_row(PROJ_TILE), norm_ffn,
                                        w_out_bf, w_router, pool_w, pool_scale, seq)

    cap = EC_FACTOR * t // N_EXPERTS
    idx, slot_t, off3 = _routing(aff, cap)
    table = h_packed.reshape(-1, LANES)
    per_range = N_EXPERTS // FFN_RANGES
    xs = []
    for r in range(FFN_RANGES):
        ids = _packed_row_ids(idx[r * per_range:(r + 1) * per_range], cap)
        xs.append(_gather_rows(table, ids).reshape(per_range, HQ_TILES, cap, LANES))

    chunks_per_tile = TOKEN_TILE // LANES
    tile_off = off3[:, ::chunks_per_tile, 0]
    tile_end = jnp.concatenate([tile_off[:, 1:], jnp.full((N_EXPERTS, 1), cap, I32)], axis=1)
    tile_start = (tile_off // SUBLANES) * SUBLANES
    tile_nch = jnp.where(tile_end > tile_off,
                         (tile_end - tile_start + COMBINE_CHUNK - 1) // COMBINE_CHUNK, 0)
    tile_wide = jnp.any(tile_end - tile_start > COMBINE_WINDOW, axis=0).astype(I32)

    def finish(y):
        out = _combine(x1, mod3, mod_row(TOKEN_TILE), norm_final, slot_t, tile_start.T,
                       tile_nch.T, tile_wide, y)
        return out.reshape(b, seq, D_MODEL)
    return xs, finish, state, (w_in_bf, w_out_bf, mod3)


def kernel(x_prompt, x_sample, c, cache_k, cache_v, c_ctx, w_ada, b_ada, norm_mix, w_in,
           sink_logits, pool_w, pool_scale, w_out, norm_ffn, w_router, w_gate, w_up, w_down,
           norm_final):
    n_b, seq, _ = x_prompt.shape
    n_db, n_lat, _ = x_sample.shape
    assert 1 + n_db <= MOD_ROWS and seq == TOKEN_TILE and n_lat % PROJ_TILE == 0
    assert w_in.shape[0] == 1 and cache_k.shape[1] == 1
    assert (n_b * seq) % PROJ_TILE == 0

    cond = jnp.concatenate(
        [c_ctx[None, :], c, jnp.zeros((MOD_ROWS - 1 - n_db, D_MODEL), F32)], axis=0)
    mixer_cols = 2 * D_MODEL
    mod_mix = _modulation(cond, w_ada[0], b_ada[0], mixer_cols).reshape(MOD_ROWS, 2, D_MODEL)

    w_router_bf = jnp.pad(w_router[0], ((0, 0), (0, LANES - N_EXPERTS))).astype(BF16)
    weights = (norm_mix[0][None, :], w_in[0], sink_logits[0], pool_w[0].astype(BF16),
               pool_scale[0][None, :], w_out[0], norm_ffn[0][None, :], w_router_bf,
               w_gate.reshape(w_gate.shape[1:]), w_up.reshape(w_up.shape[1:]),
               w_down.reshape(w_down.shape[1:]), norm_final[None, :])

    xs_p, finish_p, (k_p, v_p), (w_in_bf, w_out_bf, mod3) = _token_group(
        x_prompt, mod_mix, None, lambda tile: (lambda i: 0), seq, weights, None, None,
        late_mod=(cond, w_ada[0], b_ada[0], mixer_cols))
    weights = weights[:1] + (w_in_bf,) + weights[2:5] + (w_out_bf,) + weights[6:]

    ck = cache_k.reshape(-1, HEAD_DIM)
    cv = cache_v.reshape(-1, HEAD_DIM)
    xs_l, finish_l, _, _ = _token_group(
        x_sample, mod_mix, mod3, lambda tile: (lambda i: 1 + i // (n_lat // tile)), n_lat, weights,
        _rope_tables(n_lat), (ck, cv))

    ys = None
    for r in range(FFN_RANGES):
        ys = _expert_ffn([xs_p[r], xs_l[r]], weights[8], weights[9], weights[10],
                         r * (N_EXPERTS // FFN_RANGES), ys)
    y_prompt = finish_p(ys[0])
    y_sample = finish_l(ys[1])

    state_k = k_p.reshape(n_b, 1, seq, N_KV_HEADS, HEAD_DIM)
    state_v = v_p.reshape(n_b, 1, seq, N_KV_HEADS, HEAD_DIM)
    return (y_prompt, y_sample, state_k, state_v)
```
